```python
import jax, jax.numpy as jnp
from jax import lax
import numpy as np

D_MODEL = 2048
BATCH = 8
SEQ = 2048
DEPTH = 4

CHUNK = 64
D_MIX = D_MODEL
D_GLA = D_MIX // 2
D_ATT = D_MIX - D_GLA
GLA_HEADS = 4
GLA_DK = D_GLA // 2 // GLA_HEADS
GLA_DV = D_GLA // GLA_HEADS
GLA_KW = GLA_HEADS * GLA_DK
GLA_GATE_RANK = 16
GLA_TAU = 16.0
ATT_HEADS = 8
ATT_HD = D_ATT // ATT_HEADS
LEFT_CHUNKS = 8
BAND = (LEFT_CHUNKS + 1) * CHUNK
REL_CLIP = 128
N_REL = 2 * REL_CLIP + 1
EPS = 1e-6

SPLIT_SIZES = (GLA_KW, GLA_KW, D_GLA, D_GLA, GLA_GATE_RANK, D_ATT, D_ATT, D_ATT, D_ATT)
D_IN = GLA_KW * 2 + D_GLA * 2 + GLA_GATE_RANK + D_ATT * 4

kernel_name = "hymba_gla_chunkattn_sandwich"


def rmsnorm(x, g):
    xf = x.astype(jnp.float32)
    y = xf * lax.rsqrt(jnp.mean(xf * xf, axis=-1, keepdims=True) + EPS) * g.astype(jnp.float32)
    return y.astype(x.dtype)


def split_columns(z):
    points = []
    acc = 0
    for s in SPLIT_SIZES[:-1]:
        acc += s
        points.append(acc)
    return jnp.split(z, points, axis=-1)


def gla_chunk_causal(q, k, v, log_a):
    out_dtype = v.dtype
    B, S, H, DK = q.shape
    DV = v.shape[-1]
    nc = S // CHUNK
    qf = q.astype(jnp.float32).reshape(B, nc, CHUNK, H, DK) * (DK ** -0.5)
    kf = k.astype(jnp.float32).reshape(B, nc, CHUNK, H, DK)
    vf = v.astype(jnp.float32).reshape(B, nc, CHUNK, H, DV)
    L = jnp.cumsum(log_a.astype(jnp.float32).reshape(B, nc, CHUNK, H, DK), axis=2)
    L_end = L[:, :, -1]
    k_dec = kf * jnp.exp(L_end[:, :, None] - L)
    U = jnp.einsum('bnchk,bnchv->bnhkv', k_dec, vf)
    A = jnp.exp(L_end)

    def step(state, inp):
        a, u = inp
        new = a[..., None] * state + u
        return new, new

    init = jnp.zeros((B, H, DK, DV), jnp.float32)
    _, states = lax.scan(step, init, (jnp.swapaxes(A, 0, 1), jnp.swapaxes(U, 0, 1)))
    states = jnp.swapaxes(states, 0, 1)
    o = jnp.einsum('bnchk,bnhkv->bnchv', qf, states)
    return o.reshape(B, S, H, DV).astype(out_dtype)


def chunk_band_attention(q, k, v, rel_bias):
    B, S, H, D = q.shape
    nc = S // CHUNK
    qc = q.reshape(B, nc, CHUNK, H, D)
    pad = ((0, 0), (LEFT_CHUNKS * CHUNK, 0), (0, 0), (0, 0))
    kp = jnp.pad(k, pad).reshape(B, nc + LEFT_CHUNKS, CHUNK, H, D)
    vp = jnp.pad(v, pad).reshape(B, nc + LEFT_CHUNKS, CHUNK, H, D)
    band_idx = jnp.arange(nc)[:, None] + jnp.arange(LEFT_CHUNKS + 1)[None, :]
    kb = kp[:, band_idx].reshape(B, nc, BAND, H, D)
    vb = vp[:, band_idx].reshape(B, nc, BAND, H, D)
    scores = jnp.einsum('bnqhd,bnkhd->bnhqk', qc, kb,
                        preferred_element_type=jnp.float32) * (D ** -0.5)
    qi = jnp.arange(CHUNK)[:, None] + LEFT_CHUNKS * CHUNK
    kj = jnp.arange(BAND)[None, :]
    rel = jnp.clip(qi - kj, -REL_CLIP, REL_CLIP) + REL_CLIP
    bias = rel_bias.astype(jnp.float32)[:, rel]
    key_chunk = band_idx - LEFT_CHUNKS
    valid = jnp.repeat(key_chunk >= 0, CHUNK, axis=1)
    scores = jnp.where(valid[None, :, None, None, :], scores + bias[None, None], -jnp.inf)
    p = jax.nn.softmax(scores, axis=-1).astype(v.dtype)
    o = jnp.einsum('bnhqk,bnkhd->bnqhd', p, vb)
    return o.reshape(B, S, H, D)


def hybrid_layer(x, w_in, w_out, g_pre, g_post, w_alpha, b_alpha, g_gla, g_att, rel_bias):
    B, S, _ = x.shape
    h = rmsnorm(x, g_pre)
    z = h @ w_in
    gq, gk, gv, gg, ga, aq, ak, av, ag = split_columns(z)
    log_a = jax.nn.log_sigmoid((ga @ w_alpha + b_alpha).astype(jnp.float32)) / GLA_TAU
    o_gla = gla_chunk_causal(gq.reshape(B, S, GLA_HEADS, GLA_DK),
                             gk.reshape(B, S, GLA_HEADS, GLA_DK),
                             gv.reshape(B, S, GLA_HEADS, GLA_DV),
                             log_a.reshape(B, S, GLA_HEADS, GLA_DK))
    o_gla = rmsnorm(o_gla, g_gla.reshape(GLA_HEADS, GLA_DV)).reshape(B, S, D_GLA)
    o_gla = o_gla * jax.nn.silu(gg)
    o_att = chunk_band_attention(aq.reshape(B, S, ATT_HEADS, ATT_HD),
                                 ak.reshape(B, S, ATT_HEADS, ATT_HD),
                                 av.reshape(B, S, ATT_HEADS, ATT_HD), rel_bias)
    o_att = rmsnorm(o_att, g_att.reshape(ATT_HEADS, ATT_HD)).reshape(B, S, D_ATT)
    o_att = o_att * jax.nn.silu(ag)
    y = jnp.concatenate([o_gla, o_att], axis=-1) @ w_out
    return x + rmsnorm(y, g_post)


def _fwd_setup_inputs(seed: int = 0) -> dict:
    key = jax.random.key(seed)
    ks = jax.random.split(key, 10)
    x = jax.random.normal(ks[0], (BATCH, SEQ, D_MODEL), jnp.float32)
    w_in = jax.random.normal(ks[1], (DEPTH, D_MODEL, D_IN), jnp.float32) * (D_MODEL ** -0.5)
    w_out = jax.random.normal(ks[2], (DEPTH, D_MIX, D_MODEL), jnp.float32) * (D_MIX ** -0.5)
    g_pre = 1.0 + 0.02 * jax.random.normal(ks[3], (DEPTH, D_MODEL), jnp.float32)
    g_post = 1.0 + 0.02 * jax.random.normal(ks[4], (DEPTH, D_MODEL), jnp.float32)
    w_alpha = jax.random.normal(ks[5], (DEPTH, GLA_GATE_RANK, GLA_KW), jnp.float32) * (GLA_GATE_RANK ** -0.5)
    b_alpha = 0.1 * jax.random.normal(ks[6], (DEPTH, GLA_KW), jnp.float32)
    g_gla = 1.0 + 0.02 * jax.random.normal(ks[7], (DEPTH, D_GLA), jnp.float32)
    g_att = 1.0 + 0.02 * jax.random.normal(ks[8], (DEPTH, D_ATT), jnp.float32)
    rel_bias = 0.1 * jax.random.normal(ks[9], (DEPTH, ATT_HEADS, N_REL), jnp.float32)
    return {"x": x, "w_in": w_in, "w_out": w_out, "g_pre": g_pre, "g_post": g_post,
            "w_alpha": w_alpha, "b_alpha": b_alpha, "g_gla": g_gla, "g_att": g_att,
            "rel_bias": rel_bias}


def _fwd_reference(x, w_in, w_out, g_pre, g_post, w_alpha, b_alpha, g_gla, g_att, rel_bias):
    h = x
    for l in range(DEPTH):
        h = hybrid_layer(h, w_in[l], w_out[l], g_pre[l], g_post[l], w_alpha[l], b_alpha[l],
                         g_gla[l], g_att[l], rel_bias[l])
    return h


import jax as _jax
import jax.numpy as _jnp

TWIN_FORMAT = 'train_step'
FWD_PARAMS = ['x', 'w_in', 'w_out', 'g_pre', 'g_post', 'w_alpha', 'b_alpha', 'g_gla', 'g_att', 'rel_bias']
TWIN_WEIGHTS = ['w_in', 'w_out', 'g_pre', 'g_post', 'w_alpha', 'b_alpha', 'g_gla', 'g_att', 'rel_bias']
TWIN_DIFF_INPUT = 'x'
TWIN_INPUTS = ['x', 'w_in', 'w_out', 'g_pre', 'g_post', 'w_alpha', 'b_alpha', 'g_gla', 'g_att', 'rel_bias', 'loss_target', 'm_w_in', 'm_w_out', 'm_g_pre', 'm_g_post', 'm_w_alpha', 'm_b_alpha', 'm_g_gla', 'm_g_att', 'm_rel_bias', 'v_w_in', 'v_w_out', 'v_g_pre', 'v_g_post', 'v_w_alpha', 'v_b_alpha', 'v_g_gla', 'v_g_att', 'v_rel_bias']
TWIN_OUTPUTS = ['loss', 'grad_x', 'grad_w_in', 'grad_w_out', 'grad_g_pre', 'grad_g_post', 'grad_w_alpha', 'grad_b_alpha', 'grad_g_gla', 'grad_g_att', 'grad_rel_bias', 'delta_w_in', 'delta_w_out', 'delta_g_pre', 'delta_g_post', 'delta_w_alpha', 'delta_b_alpha', 'delta_g_gla', 'delta_g_att', 'delta_rel_bias', 'new_m_w_in', 'new_m_w_out', 'new_m_g_pre', 'new_m_g_post', 'new_m_w_alpha', 'new_m_b_alpha', 'new_m_g_gla', 'new_m_g_att', 'new_m_rel_bias', 'new_v_w_in', 'new_v_w_out', 'new_v_g_pre', 'new_v_g_post', 'new_v_w_alpha', 'new_v_b_alpha', 'new_v_g_gla', 'new_v_g_att', 'new_v_rel_bias']
TWIN_LEAF_KINDS = {'loss': 'loss', 'grad_x': 'grad_x', 'grad_w_in': 'grad_w', 'grad_w_out': 'grad_w', 'grad_g_pre': 'grad_w', 'grad_g_post': 'grad_w', 'grad_w_alpha': 'grad_w', 'grad_b_alpha': 'grad_w', 'grad_g_gla': 'grad_w', 'grad_g_att': 'grad_w', 'grad_rel_bias': 'grad_w', 'delta_w_in': 'delta_w', 'delta_w_out': 'delta_w', 'delta_g_pre': 'delta_w', 'delta_g_post': 'delta_w', 'delta_w_alpha': 'delta_w', 'delta_b_alpha': 'delta_w', 'delta_g_gla': 'delta_w', 'delta_g_att': 'delta_w', 'delta_rel_bias': 'delta_w', 'new_m_w_in': 'new_m', 'new_m_w_out': 'new_m', 'new_m_g_pre': 'new_m', 'new_m_g_post': 'new_m', 'new_m_w_alpha': 'new_m', 'new_m_b_alpha': 'new_m', 'new_m_g_gla': 'new_m', 'new_m_g_att': 'new_m', 'new_m_rel_bias': 'new_m', 'new_v_w_in': 'new_v', 'new_v_w_out': 'new_v', 'new_v_g_pre': 'new_v', 'new_v_g_post': 'new_v', 'new_v_w_alpha': 'new_v', 'new_v_b_alpha': 'new_v', 'new_v_g_gla': 'new_v', 'new_v_g_att': 'new_v', 'new_v_rel_bias': 'new_v'}


def _forward(args):
    return _fwd_reference(*[args[k] for k in FWD_PARAMS])


def _output_shape():
    out = _jax.eval_shape(lambda: _forward(_fwd_setup_inputs(0)))
    return out.shape, out.dtype

N_MICROBATCH = 1
ADAM_LR = 0.001
ADAM_B1 = 0.9
ADAM_B2 = 0.999
ADAM_EPS = 1e-08
ADAM_WD = 0.01
ADAM_STEP = 10
PER_EXAMPLE_BATCH_AXIS = {'x': 0, 'loss_target': 0}
SHARED_INPUTS = []
_WEIGHT_DTYPES = {'w_in': _jnp.float32, 'w_out': _jnp.float32, 'g_pre': _jnp.float32, 'g_post': _jnp.float32, 'w_alpha': _jnp.float32, 'b_alpha': _jnp.float32, 'g_gla': _jnp.float32, 'g_att': _jnp.float32, 'rel_bias': _jnp.float32}
MOMENT_SCALE = {'w_in': 3.214976e-01, 'w_out': 3.927122e-01, 'g_pre': 6.276037e-01, 'g_post': 7.962215e+00, 'w_alpha': 4.835146e-02, 'b_alpha': 1.853000e-01, 'g_gla': 2.906495e-01, 'g_att': 4.857443e-01, 'rel_bias': 1.765019e-01}


def _to_microbatches(a, axis):
    t = _jnp.moveaxis(a, axis, 0)
    t = t.reshape((N_MICROBATCH, t.shape[0] // N_MICROBATCH) + t.shape[1:])
    return _jnp.moveaxis(t, 1, axis + 1)


def setup_inputs(seed: int = 0) -> dict:
    inp = _fwd_setup_inputs(seed)
    key = _jax.random.fold_in(_jax.random.key(seed), 7919)
    shape, _ = _output_shape()
    out = dict(inp)
    out["loss_target"] = _jax.random.normal(_jax.random.fold_in(key, 0), shape, _jnp.float32)
    for i, name in enumerate(TWIN_WEIGHTS):
        w = inp[name].astype(_jnp.float32)
        if MOMENT_SCALE is None:
            s = _jnp.sqrt(_jnp.mean(_jnp.square(w)) + 1e-30)
        else:
            s = MOMENT_SCALE[name]
        km, kv = _jax.random.split(_jax.random.fold_in(key, i + 1))
        out[name] = w
        out["m_" + name] = s * _jax.random.normal(km, w.shape, _jnp.float32)
        out["v_" + name] = (s * s) * _jax.random.uniform(kv, w.shape, _jnp.float32, 0.5, 1.5)
    if N_MICROBATCH > 1:
        for name, axis in PER_EXAMPLE_BATCH_AXIS.items():
            out[name] = _to_microbatches(out[name], axis)
    return {'x': out['x'], 'w_in': out['w_in'], 'w_out': out['w_out'], 'g_pre': out['g_pre'], 'g_post': out['g_post'], 'w_alpha': out['w_alpha'], 'b_alpha': out['b_alpha'], 'g_gla': out['g_gla'], 'g_att': out['g_att'], 'rel_bias': out['rel_bias'], 'loss_target': out['loss_target'], 'm_w_in': out['m_w_in'], 'm_w_out': out['m_w_out'], 'm_g_pre': out['m_g_pre'], 'm_g_post': out['m_g_post'], 'm_w_alpha': out['m_w_alpha'], 'm_b_alpha': out['m_b_alpha'], 'm_g_gla': out['m_g_gla'], 'm_g_att': out['m_g_att'], 'm_rel_bias': out['m_rel_bias'], 'v_w_in': out['v_w_in'], 'v_w_out': out['v_w_out'], 'v_g_pre': out['v_g_pre'], 'v_g_post': out['v_g_post'], 'v_w_alpha': out['v_w_alpha'], 'v_b_alpha': out['v_b_alpha'], 'v_g_gla': out['v_g_gla'], 'v_g_att': out['v_g_att'], 'v_rel_bias': out['v_rel_bias']}


def _loss(weights, diff, rest, loss_target):
    with _jax.named_scope("forward"):
        args = {**rest, TWIN_DIFF_INPUT: diff, **{k: w.astype(_WEIGHT_DTYPES[k]) for k, w in weights.items()}}
        y = _forward(args)
    with _jax.named_scope("loss_head"):
        err = _jnp.square(y.astype(_jnp.float32) - loss_target)
        return 0.5 * _jnp.sum(_jnp.mean(err, axis=-1)) if err.ndim else 0.5 * err


def _adamw(w, g, m, v):
    m = ADAM_B1 * m + (1.0 - ADAM_B1) * g
    v = ADAM_B2 * v + (1.0 - ADAM_B2) * _jnp.square(g)
    m_hat = m / (1.0 - ADAM_B1 ** ADAM_STEP)
    v_hat = v / (1.0 - ADAM_B2 ** ADAM_STEP)
    delta = -ADAM_LR * (m_hat / (_jnp.sqrt(v_hat) + ADAM_EPS) + ADAM_WD * w)
    return delta, m, v


def reference(x, w_in, w_out, g_pre, g_post, w_alpha, b_alpha, g_gla, g_att, rel_bias, loss_target, m_w_in, m_w_out, m_g_pre, m_g_post, m_w_alpha, m_b_alpha, m_g_gla, m_g_att, m_rel_bias, v_w_in, v_w_out, v_g_pre, v_g_post, v_w_alpha, v_b_alpha, v_g_gla, v_g_att, v_rel_bias):
    given = dict(x=x, w_in=w_in, w_out=w_out, g_pre=g_pre, g_post=g_post, w_alpha=w_alpha, b_alpha=b_alpha, g_gla=g_gla, g_att=g_att, rel_bias=rel_bias, loss_target=loss_target, m_w_in=m_w_in, m_w_out=m_w_out, m_g_pre=m_g_pre, m_g_post=m_g_post, m_w_alpha=m_w_alpha, m_b_alpha=m_b_alpha, m_g_gla=m_g_gla, m_g_att=m_g_att, m_rel_bias=m_rel_bias, v_w_in=v_w_in, v_w_out=v_w_out, v_g_pre=v_g_pre, v_g_post=v_g_post, v_w_alpha=v_w_alpha, v_b_alpha=v_b_alpha, v_g_gla=v_g_gla, v_g_att=v_g_att, v_rel_bias=v_rel_bias)
    weights = {n: given[n] for n in TWIN_WEIGHTS}
    shared = {n: given[n] for n in SHARED_INPUTS}
    per_example = {n: given[n] for n in ['x']}
    grad_fn = _jax.value_and_grad(_loss, argnums=(0, 1))

    def one_microbatch(ex, loss_target):
        ex = dict(ex)
        diff = ex.pop(TWIN_DIFF_INPUT)
        return grad_fn(weights, diff, {**shared, **ex}, loss_target)

    if N_MICROBATCH == 1:
        loss, (grad_w, grad_x) = one_microbatch(per_example, given["loss_target"])
    else:
        def body(carry, xs):
            loss_sum, grad_sum = carry
            l_k, (gw_k, gx_k) = one_microbatch(xs[0], xs[1])
            with _jax.named_scope("update"):
                return (loss_sum + l_k, _jax.tree.map(_jnp.add, grad_sum, gw_k)), gx_k

        init = (_jnp.zeros((), _jnp.float32), _jax.tree.map(_jnp.zeros_like, weights))
        (loss, grad_w), grad_x = _jax.lax.scan(body, init, (per_example, given["loss_target"]))
    with _jax.named_scope("update"):
        delta_w, new_m, new_v = {}, {}, {}
        for n in TWIN_WEIGHTS:
            delta_w[n], new_m[n], new_v[n] = _adamw(weights[n], grad_w[n], given["m_" + n], given["v_" + n])
    return (loss, grad_x, *[grad_w[n] for n in TWIN_WEIGHTS], *[delta_w[n] for n in TWIN_WEIGHTS],
            *[new_m[n] for n in TWIN_WEIGHTS], *[new_v[n] for n in TWIN_WEIGHTS])
```

```python
import functools

import jax
import jax.numpy as jnp
from jax import lax
from jax.experimental import pallas as pl
from jax.experimental.pallas import tpu as pltpu

D = 2048
DEPTH = 4
CHUNK = 64
GH, GDK, GDV = 4, 128, 256
DGLA = GH * GDV
RANK = 16
TAU = 16.0
AH, AHD = 8, 128
DATT = AH * AHD
LEFT = 8
NREL = 257
EPS = 1e-6
DIN = 7184
ADAM_LR, ADAM_B1, ADAM_B2, ADAM_EPS, ADAM_WD, ADAM_STEP = 0.001, 0.9, 0.999, 1e-08, 0.01, 10

GW = 2 * GDK + 2 * GDV
AW = 4 * AHD
ZG = GH * GW
ZA = AH * AW
ZM = ZG + ZA
GAP = 128
QB = 2 * CHUNK
BANDW = (LEFT + 2) * CHUNK
PADK = LEFT * CHUNK
NCHIP = 4
SHARD = DIN // NCHIP
NEG = -1e30
F32 = jnp.float32
_CDT = jnp.bfloat16
_XDT = jnp.bfloat16
_VMEM = 56 * 1024 * 1024
MESH = pl.DeviceIdType.MESH
ANY = pl.BlockSpec(memory_space=pl.ANY)


def _dot(a, b):
    return jnp.dot(a, b, preferred_element_type=F32)


def _dot_nt(a, b):
    return lax.dot_general(a, b, (((1,), (1,)), ((), ())), preferred_element_type=F32)


def _dot_tn(a, b):
    return lax.dot_general(a, b, (((0,), (0,)), ((), ())), preferred_element_type=F32)


def _rms_rows(v):
    return lax.rsqrt(jnp.mean(v * v, axis=-1, keepdims=True) + EPS)


def _sigmoid(v):
    return 1.0 / (1.0 + jnp.exp(-v))


def _log_sigmoid(v):
    return jnp.minimum(v, 0.0) - jnp.log(1.0 + jnp.exp(-jnp.abs(v)))


def _exact_dot(tri, v):
    hi = v.astype(_CDT)
    r1 = v - hi.astype(F32)
    mid = r1.astype(_CDT)
    lo = (r1 - mid.astype(F32)).astype(_CDT)
    return _dot(tri, hi) + _dot(tri, mid) + _dot(tri, lo)


def _tri(strict):
    row = lax.broadcasted_iota(jnp.int32, (CHUNK, CHUNK), 0)
    col = lax.broadcasted_iota(jnp.int32, (CHUNK, CHUNK), 1)
    return jnp.where((col < row) if strict else (col <= row), 1.0, 0.0).astype(_CDT)


def _norm_gate_bwd(o, g, gate, dcat):
    r = _rms_rows(o)
    oh = o * r
    sg = _sigmoid(gate)
    dn = dcat * (gate * sg)
    dgate = dcat * (oh * g) * (sg * (1.0 + gate * (1.0 - sg)))
    dg = jnp.sum(dn * oh, axis=0, keepdims=True)
    dnn = dn * g
    do = r * (dnn - oh * jnp.mean(dnn * oh, axis=-1, keepdims=True))
    return do, dgate, dg


def _params(sem=None, vmem=_VMEM):
    return pltpu.CompilerParams(dimension_semantics=sem, vmem_limit_bytes=vmem)


def _inproj(x, g, wm, wga, tm=512, tn=1024):
    s = x.shape[0]

    def body(x_ref, g_ref, wm_ref, wga_ref, z_ref, zga_ref, h_ref, hs):
        @pl.when(pl.program_id(1) == 0)
        def _():
            xv = x_ref[...]
            hv = (xv * _rms_rows(xv) * g_ref[...]).astype(_CDT)
            hs[...] = hv
            h_ref[...] = hv
            zga_ref[...] = _dot(hv, wga_ref[...]).astype(_CDT)

        z_ref[...] = _dot(hs[...], wm_ref[...]).astype(_CDT)

    return pl.pallas_call(
        body, name="inproj", grid=(s // tm, ZM // tn),
        in_specs=[pl.BlockSpec((tm, D), lambda i, j: (i, 0)), pl.BlockSpec((1, D), lambda i, j: (0, 0)),
                  pl.BlockSpec((D, tn), lambda i, j: (0, j)), pl.BlockSpec((D, GAP), lambda i, j: (0, 0))],
        out_specs=[pl.BlockSpec((tm, tn), lambda i, j: (i, j)), pl.BlockSpec((tm, GAP), lambda i, j: (i, 0)),
                   pl.BlockSpec((tm, D), lambda i, j: (i, 0))],
        out_shape=[jax.ShapeDtypeStruct((s, ZM), _CDT), jax.ShapeDtypeStruct((s, GAP), _CDT),
                   jax.ShapeDtypeStruct((s, D), _CDT)],
        scratch_shapes=[pltpu.VMEM((tm, D), _CDT)],
        compiler_params=_params(("parallel", "arbitrary")),
    )(x, g, wm, wga)


def _gla_fwd(z, zga, wa, ba, ggla):
    s = z.shape[0]
    nc = s // CHUNK

    def body(zg_ref, zga_ref, wa_ref, ba_ref, g_ref, cat_ref, la_s, st):
        la_s[...] = _log_sigmoid(_dot(zga_ref[...], wa_ref[...]) + ba_ref[...]) * (1.0 / TAU)
        st[...] = jnp.zeros_like(st)
        tri = _tri(False)

        def step(n, carry):
            rows = pl.ds(pl.multiple_of(n * CHUNK, CHUNK), CHUNK)
            la = la_s[rows, :]
            lc = _exact_dot(tri, la)
            lend = jnp.sum(la, axis=0, keepdims=True)
            kdec = (zg_ref[rows, GDK:2 * GDK].astype(F32) * jnp.exp(lend - lc)).astype(_CDT)
            stn = jnp.exp(lend) * st[...] + _dot_tn(zg_ref[rows, 2 * GDK:2 * GDK + GDV], kdec)
            st[...] = stn
            qs = (zg_ref[rows, 0:GDK].astype(F32) * (GDK ** -0.5)).astype(_CDT)
            o = _dot_nt(qs, stn.astype(_CDT))
            gate = zg_ref[rows, 2 * GDK + GDV:GW].astype(F32)
            cat_ref[rows, :] = (o * _rms_rows(o) * g_ref[...] * (gate * _sigmoid(gate))).astype(_CDT)
            return carry

        lax.fori_loop(0, nc, step, 0)

    return pl.pallas_call(
        body, name="gla_fwd", grid=(GH,),
        in_specs=[pl.BlockSpec((s, GW), lambda h: (0, h)), pl.BlockSpec((s, GAP), lambda h: (0, 0)),
                  pl.BlockSpec((GAP, GDK), lambda h: (0, h)), pl.BlockSpec((1, GDK), lambda h: (0, h)),
                  pl.BlockSpec((1, GDV), lambda h: (0, h))],
        out_specs=pl.BlockSpec((s, GDV), lambda h: (0, h)),
        out_shape=jax.ShapeDtypeStruct((s, DGLA), _CDT),
        scratch_shapes=[pltpu.VMEM((s, GDK), F32), pltpu.VMEM((GDV, GDK), F32)],
        compiler_params=_params(("arbitrary",)),
    )(z, zga, wa, ba, ggla)


def _band_bias(b0):
    row = lax.broadcasted_iota(jnp.int32, (QB, 256), 0)
    col = lax.broadcasted_iota(jnp.int32, (QB, 256), 1)
    lane = lax.broadcasted_iota(jnp.int32, (1, 256), 1)
    c0 = jnp.sum(jnp.where(lane == 0, b0, 0.0), axis=1, keepdims=True)
    xv = jnp.broadcast_to(b0, (QB, 256))
    for bit in range(7):
        xv = jnp.where(((row >> bit) & 1) == 1, pltpu.roll(xv, 1 << bit, 1), xv)
    xv = jnp.where(col < row, c0, xv)
    return jnp.concatenate([jnp.broadcast_to(c0, (QB, BANDW - 256)), xv], axis=1)


def _band_static_mask():
    row = lax.broadcasted_iota(jnp.int32, (QB, BANDW), 0) >> 6
    col = lax.broadcasted_iota(jnp.int32, (QB, BANDW), 1) >> 6
    return (col >= row) & (col <= row + LEFT)


def _fold_bias_grad(t):
    row = lax.broadcasted_iota(jnp.int32, (QB, 256), 0)
    col = lax.broadcasted_iota(jnp.int32, (QB, 256), 1)
    xv = t[:, BANDW - 256:]
    low = col < row
    far = jnp.sum(t[:, 0:BANDW - 256], axis=1, keepdims=True) + jnp.sum(jnp.where(low, xv, 0.0), axis=1, keepdims=True)
    far = jnp.sum(far, axis=0, keepdims=True)
    xv = jnp.where(low, 0.0, xv)
    for bit in range(7):
        xv = jnp.where(((row >> bit) & 1) == 1, pltpu.roll(xv, 256 - (1 << bit), 1), xv)
    dp = jnp.sum(xv, axis=0, keepdims=True)
    lane = lax.broadcasted_iota(jnp.int32, (1, 256), 1)
    return dp + jnp.where(lane == 0, far, 0.0)


def _att_fwd(z, rbx, gatt):
    s = z.shape[0]
    nb = s // QB

    def body(za_ref, rb_ref, g_ref, cat_ref, o_ref, lse_ref, kp, vp, bias_s):
        kp[0:PADK, :] = jnp.zeros((PADK, AHD), _CDT)
        vp[0:PADK, :] = jnp.zeros((PADK, AHD), _CDT)
        kp[PADK:, :] = za_ref[:, AHD:2 * AHD]
        vp[PADK:, :] = za_ref[:, 2 * AHD:3 * AHD]
        bias_s[...] = jnp.where(_band_static_mask(), _band_bias(rb_ref[0]), NEG)

        def step(b, carry):
            r0 = pl.multiple_of(b * QB, QB)
            rows = pl.ds(r0, QB)
            band = pl.ds(r0, BANDW)
            col = lax.broadcasted_iota(jnp.int32, (QB, BANDW), 1)
            sc = _dot_nt(za_ref[rows, 0:AHD], kp[band, :]) * (AHD ** -0.5) + bias_s[...]
            sc = jnp.where(col >= PADK - r0, sc, NEG)
            m = jnp.max(sc, axis=-1, keepdims=True)
            p = jnp.exp(sc - m)
            l = jnp.sum(p, axis=-1, keepdims=True)
            o = _dot((p * (1.0 / l)).astype(_CDT), vp[band, :])
            o_ref[rows, :] = o.astype(_CDT)
            lse_ref[rows, :] = jnp.broadcast_to(m + jnp.log(l), (QB, AHD))
            gate = za_ref[rows, 3 * AHD:AW].astype(F32)
            cat_ref[rows, :] = (o * _rms_rows(o) * g_ref[...] * (gate * _sigmoid(gate))).astype(_CDT)
            return carry

        lax.fori_loop(0, nb, step, 0)

    return pl.pallas_call(
        body, name="att_fwd", grid=(AH,),
        in_specs=[pl.BlockSpec((s, AW), lambda h: (0, ZG // AW + h)), pl.BlockSpec((1, 1, 256), lambda h: (h, 0, 0)),
                  pl.BlockSpec((1, AHD), lambda h: (0, h))],
        out_specs=[pl.BlockSpec((s, AHD), lambda h: (0, h)), pl.BlockSpec((s, AHD), lambda h: (0, h)),
                   pl.BlockSpec((s, AHD), lambda h: (0, h))],
        out_shape=[jax.ShapeDtypeStruct((s, DATT), _CDT), jax.ShapeDtypeStruct((s, DATT), _CDT),
                   jax.ShapeDtypeStruct((s, DATT), F32)],
        scratch_shapes=[pltpu.VMEM((s + PADK, AHD), _CDT), pltpu.VMEM((s + PADK, AHD), _CDT),
                        pltpu.VMEM((QB, BANDW), F32)],
        compiler_params=_params(("arbitrary",)),
    )(z, rbx, gatt)


def _outproj(cg, ca, wout, x, gpost, tm=256):
    s = x.shape[0]

    def body(cg_ref, ca_ref, w_ref, x_ref, g_ref, y_ref, xo_ref):
        y = _dot(cg_ref[...], w_ref[0:DGLA, :]) + _dot(ca_ref[...], w_ref[DGLA:, :])
        y_ref[...] = y
        xo_ref[...] = x_ref[...] + y * _rms_rows(y) * g_ref[...]

    return pl.pallas_call(
        body, name="outproj", grid=(s // tm,),
        in_specs=[pl.BlockSpec((tm, DGLA), lambda i: (i, 0)), pl.BlockSpec((tm, DATT), lambda i: (i, 0)),
                  pl.BlockSpec((D, D), lambda i: (0, 0)), pl.BlockSpec((tm, D), lambda i: (i, 0)),
                  pl.BlockSpec((1, D), lambda i: (0, 0))],
        out_specs=[pl.BlockSpec((tm, D), lambda i: (i, 0)), pl.BlockSpec((tm, D), lambda i: (i, 0))],
        out_shape=[jax.ShapeDtypeStruct((s, D), F32), jax.ShapeDtypeStruct((s, D), F32)],
        compiler_params=_params(("parallel",)),
    )(cg, ca, wout, x, gpost)


def _loss_grad(xo, tgt, tm=256):
    s = xo.shape[0]

    def body(xo_ref, t_ref, d_ref, l_ref):
        @pl.when(pl.program_id(0) == 0)
        def _():
            l_ref[...] = jnp.zeros_like(l_ref)

        e = xo_ref[...] - t_ref[...]
        d_ref[...] = e * (1.0 / D)
        l_ref[...] += jnp.sum(jnp.sum(e * e, axis=1, keepdims=True), axis=0, keepdims=True) * (0.5 / D)

    return pl.pallas_call(
        body, name="loss_grad", grid=(s // tm,),
        in_specs=[pl.BlockSpec((tm, D), lambda i: (i, 0)), pl.BlockSpec((tm, D), lambda i: (i, 0))],
        out_specs=[pl.BlockSpec((tm, D), lambda i: (i, 0)), pl.BlockSpec((1, 1), lambda i: (0, 0))],
        out_shape=[jax.ShapeDtypeStruct((s, D), F32), jax.ShapeDtypeStruct((1, 1), F32)],
        compiler_params=_params(("arbitrary",)),
    )(xo, tgt)


def _post_bwd(dout, y, gpost, wout, tm=256):
    s = y.shape[0]

    def body(d_ref, y_ref, g_ref, w_ref, dy_ref, dcg_ref, dca_ref, dg_ref):
        @pl.when(pl.program_id(0) == 0)
        def _():
            dg_ref[...] = jnp.zeros_like(dg_ref)

        yv = y_ref[...]
        r = _rms_rows(yv)
        yh = yv * r
        dv = d_ref[...]
        dg_ref[...] += jnp.sum(dv * yh, axis=0, keepdims=True)
        dn = dv * g_ref[...]
        dyb = (r * (dn - yh * jnp.mean(dn * yh, axis=-1, keepdims=True))).astype(_CDT)
        dy_ref[...] = dyb
        dcg_ref[...] = _dot_nt(dyb, w_ref[0:DGLA, :]).astype(_CDT)
        dca_ref[...] = _dot_nt(dyb, w_ref[DGLA:, :]).astype(_CDT)

    return pl.pallas_call(
        body, name="post_bwd", grid=(s // tm,),
        in_specs=[pl.BlockSpec((tm, D), lambda i: (i, 0)), pl.BlockSpec((tm, D), lambda i: (i, 0)),
                  pl.BlockSpec((1, D), lambda i: (0, 0)), pl.BlockSpec((D, D), lambda i: (0, 0))],
        out_specs=[pl.BlockSpec((tm, D), lambda i: (i, 0)), pl.BlockSpec((tm, DGLA), lambda i: (i, 0)),
                   pl.BlockSpec((tm, DATT), lambda i: (i, 0)), pl.BlockSpec((1, D), lambda i: (0, 0))],
        out_shape=[jax.ShapeDtypeStruct((s, D), _CDT), jax.ShapeDtypeStruct((s, DGLA), _CDT),
                   jax.ShapeDtypeStruct((s, DATT), _CDT), jax.ShapeDtypeStruct((1, D), F32)],
        compiler_params=_params(("arbitrary",)),
    )(dout, y, gpost, wout)


def _matmul_tn(a, b, out_dtype, tm, tn, name):
    k, m = a.shape
    n = b.shape[1]

    def body(a_ref, b_ref, o_ref):
        o_ref[...] = _dot_tn(a_ref[...], b_ref[...]).astype(out_dtype)

    return pl.pallas_call(
        body, name=name, grid=(m // tm, n // tn),
        in_specs=[pl.BlockSpec((k, tm), lambda i, j: (0, i)), pl.BlockSpec((k, tn), lambda i, j: (0, j))],
        out_specs=pl.BlockSpec((tm, tn), lambda i, j: (i, j)),
        out_shape=jax.ShapeDtypeStruct((m, n), out_dtype),
        compiler_params=_params(("parallel", "parallel")),
    )(a, b)


def _att_bwd(z, oraw, lse, dca, rbx, gatt):
    s = z.shape[0]
    nb = s // QB

    def body(za_ref, o_ref, lse_ref, dc_ref, rb_ref, g_ref, dz_ref, dg_ref, db_ref, kp, vp, dkp, dvp, bias_s, t_s, dg_s):
        kp[0:PADK, :] = jnp.zeros((PADK, AHD), _CDT)
        vp[0:PADK, :] = jnp.zeros((PADK, AHD), _CDT)
        kp[PADK:, :] = za_ref[:, AHD:2 * AHD]
        vp[PADK:, :] = za_ref[:, 2 * AHD:3 * AHD]
        dkp[...] = jnp.zeros_like(dkp)
        dvp[...] = jnp.zeros_like(dvp)
        t_s[...] = jnp.zeros_like(t_s)
        dg_s[...] = jnp.zeros_like(dg_s)
        bias_s[...] = jnp.where(_band_static_mask(), _band_bias(rb_ref[0]), NEG)

        def step(b, carry):
            r0 = pl.multiple_of(b * QB, QB)
            rows = pl.ds(r0, QB)
            band = pl.ds(r0, BANDW)
            col = lax.broadcasted_iota(jnp.int32, (QB, BANDW), 1)
            o = o_ref[rows, :].astype(F32)
            do, dgate, dg = _norm_gate_bwd(o, g_ref[...], za_ref[rows, 3 * AHD:AW].astype(F32),
                                           dc_ref[rows, :].astype(F32))
            dg_s[...] += dg
            q = za_ref[rows, 0:AHD]
            kb = kp[band, :]
            sc = _dot_nt(q, kb) * (AHD ** -0.5) + bias_s[...]
            sc = jnp.where(col >= PADK - r0, sc, NEG)
            p = jnp.exp(sc - jnp.max(lse_ref[rows, :], axis=-1, keepdims=True))
            dob = do.astype(_CDT)
            dp = _dot_nt(dob, vp[band, :])
            ds = p * (dp - jnp.sum(do * o, axis=-1, keepdims=True))
            t_s[...] += ds
            dsb = (ds * (AHD ** -0.5)).astype(_CDT)
            dz_ref[rows, 0:AHD] = _dot(dsb, kb).astype(_CDT)
            dz_ref[rows, 3 * AHD:AW] = dgate.astype(_CDT)
            dkp[band, :] += _dot_tn(dsb, q)
            dvp[band, :] += _dot_tn(p.astype(_CDT), dob)
            return carry

        lax.fori_loop(0, nb, step, 0)
        dz_ref[:, AHD:2 * AHD] = dkp[PADK:, :].astype(_CDT)
        dz_ref[:, 2 * AHD:3 * AHD] = dvp[PADK:, :].astype(_CDT)
        dg_ref[...] = dg_s[...]
        db_ref[0] = _fold_bias_grad(t_s[...])

    return pl.pallas_call(
        body, name="att_bwd", grid=(AH,),
        in_specs=[pl.BlockSpec((s, AW), lambda h: (0, ZG // AW + h)), pl.BlockSpec((s, AHD), lambda h: (0, h)),
                  pl.BlockSpec((s, AHD), lambda h: (0, h)), pl.BlockSpec((s, AHD), lambda h: (0, h)),
                  pl.BlockSpec((1, 1, 256), lambda h: (h, 0, 0)), pl.BlockSpec((1, AHD), lambda h: (0, h))],
        out_specs=[pl.BlockSpec((s, AW), lambda h: (0, h)), pl.BlockSpec((1, AHD), lambda h: (0, h)),
                   pl.BlockSpec((1, 1, 256), lambda h: (h, 0, 0))],
        out_shape=[jax.ShapeDtypeStruct((s, ZA), _CDT), jax.ShapeDtypeStruct((1, DATT), F32),
                   jax.ShapeDtypeStruct((AH, 1, 256), F32)],
        scratch_shapes=[pltpu.VMEM((s + PADK, AHD), _CDT), pltpu.VMEM((s + PADK, AHD), _CDT),
                        pltpu.VMEM((s + PADK, AHD), F32), pltpu.VMEM((s + PADK, AHD), F32),
                        pltpu.VMEM((QB, BANDW), F32), pltpu.VMEM((QB, BANDW), F32), pltpu.VMEM((1, AHD), F32)],
        compiler_params=_params(("arbitrary",)),
    )(z, oraw, lse, dca, rbx, gatt)


def _gla_bwd(z, zga, wa, ba, ggla, dcg):
    s = z.shape[0]
    nc = s // CHUNK

    def body(zg_ref, zga_ref, wa_ref, ba_ref, g_ref, dc_ref, dz_ref, dga_ref, dwa_ref, dba_ref, dg_ref,
             la_s, om_s, sall, dpre_s, c_s, dga_s, dg_s):
        h = pl.program_id(0)
        pre = _dot(zga_ref[...], wa_ref[...]) + ba_ref[...]
        la_s[...] = _log_sigmoid(pre) * (1.0 / TAU)
        om_s[...] = (1.0 - _sigmoid(pre)) * (1.0 / TAU)
        c_s[...] = jnp.zeros_like(c_s)
        dg_s[...] = jnp.zeros_like(dg_s)
        tri = _tri(False)
        tri_strict = _tri(True)

        def decay(rows):
            la = la_s[rows, :]
            lend = jnp.sum(la, axis=0, keepdims=True)
            return jnp.exp(lend - _exact_dot(tri, la)), jnp.exp(lend)

        def fwd(n, st):
            rows = pl.ds(pl.multiple_of(n * CHUNK, CHUNK), CHUNK)
            dec, a = decay(rows)
            kdec = (zg_ref[rows, GDK:2 * GDK].astype(F32) * dec).astype(_CDT)
            stn = a * st + _dot_tn(zg_ref[rows, 2 * GDK:2 * GDK + GDV], kdec)
            sall[n] = stn
            return stn

        lax.fori_loop(0, nc, fwd, jnp.zeros((GDV, GDK), F32))

        def bwd(i, carry):
            n = nc - 1 - i
            rows = pl.ds(pl.multiple_of(n * CHUNK, CHUNK), CHUNK)
            dec, a = decay(rows)
            kf = zg_ref[rows, GDK:2 * GDK].astype(F32)
            kdec = kf * dec
            kdb = kdec.astype(_CDT)
            v = zg_ref[rows, 2 * GDK:2 * GDK + GDV]
            qs = (zg_ref[rows, 0:GDK].astype(F32) * (GDK ** -0.5)).astype(_CDT)
            stn = sall[n]
            stb = stn.astype(_CDT)
            st_prev = sall[jnp.maximum(n - 1, 0)] * jnp.where(n > 0, 1.0, 0.0)
            o = _dot_nt(qs, stb)
            do, dgate, dg = _norm_gate_bwd(o, g_ref[...], zg_ref[rows, 2 * GDK + GDV:GW].astype(F32),
                                           dc_ref[rows, :].astype(F32))
            dg_s[...] += dg
            dob = do.astype(_CDT)
            gt = _dot_tn(dob, qs) + c_s[...]
            gtb = gt.astype(_CDT)
            da = jnp.sum(gt * st_prev, axis=0, keepdims=True)
            dkdec = _dot(v, gtb)
            dla = _exact_dot(tri_strict, dkdec * kdec) + da * a
            dpre_s[rows, :] = dla * om_s[rows, :]
            dz_ref[rows, 0:GDK] = (_dot(dob, stb) * (GDK ** -0.5)).astype(_CDT)
            dz_ref[rows, GDK:2 * GDK] = (dkdec * dec).astype(_CDT)
            dz_ref[rows, 2 * GDK:2 * GDK + GDV] = _dot_nt(kdb, gtb).astype(_CDT)
            dz_ref[rows, 2 * GDK + GDV:GW] = dgate.astype(_CDT)
            c_s[...] = a * gt
            return carry

        lax.fori_loop(0, nc, bwd, 0)
        dpre = dpre_s[...]
        dpb = dpre.astype(_CDT)
        dg_ref[...] = dg_s[...]
        dba_ref[...] = jnp.sum(dpre, axis=0, keepdims=True)
        dwa_ref[...] = _dot_tn(zga_ref[...], dpb)
        part = _dot_nt(dpb, wa_ref[...])

        @pl.when(h == 0)
        def _():
            dga_s[...] = part

        @pl.when(h > 0)
        def _():
            dga_s[...] += part

        @pl.when(h == GH - 1)
        def _():
            dga_ref[...] = dga_s[...].astype(_CDT)

    return pl.pallas_call(
        body, name="gla_bwd", grid=(GH,),
        in_specs=[pl.BlockSpec((s, GW), lambda h: (0, h)), pl.BlockSpec((s, GAP), lambda h: (0, 0)),
                  pl.BlockSpec((GAP, GDK), lambda h: (0, h)), pl.BlockSpec((1, GDK), lambda h: (0, h)),
                  pl.BlockSpec((1, GDV), lambda h: (0, h)), pl.BlockSpec((s, GDV), lambda h: (0, h))],
        out_specs=[pl.BlockSpec((s, GW), lambda h: (0, h)), pl.BlockSpec((s, GAP), lambda h: (0, 0)),
                   pl.BlockSpec((GAP, GDK), lambda h: (0, h)), pl.BlockSpec((1, GDK), lambda h: (0, h)),
                   pl.BlockSpec((1, GDV), lambda h: (0, h))],
        out_shape=[jax.ShapeDtypeStruct((s, ZG), _CDT), jax.ShapeDtypeStruct((s, GAP), _CDT),
                   jax.ShapeDtypeStruct((GAP, GH * GDK), F32), jax.ShapeDtypeStruct((1, GH * GDK), F32),
                   jax.ShapeDtypeStruct((1, DGLA), F32)],
        scratch_shapes=[pltpu.VMEM((s, GDK), F32), pltpu.VMEM((s, GDK), F32), pltpu.VMEM((nc, GDV, GDK), F32),
                        pltpu.VMEM((s, GDK), F32), pltpu.VMEM((GDV, GDK), F32), pltpu.VMEM((s, GAP), F32),
                        pltpu.VMEM((1, GDV), F32)],
        compiler_params=_params(("arbitrary",)),
    )(z, zga, wa, ba, ggla, dcg)


def _dh(dzg, dza, dga, wm, wga, x, dout, gpre, tm=512, tk=1024):
    s = x.shape[0]
    nkg, nk = ZG // tk, ZM // tk

    def body(dzg_ref, dza_ref, dga_ref, wm_ref, wga_ref, x_ref, d_ref, g_ref, dx_ref, dg_ref, acc):
        i, k = pl.program_id(0), pl.program_id(1)

        @pl.when((i == 0) & (k == 0))
        def _():
            dg_ref[...] = jnp.zeros_like(dg_ref)

        @pl.when(k == 0)
        def _():
            acc[...] = _dot_nt(dga_ref[...], wga_ref[...])

        @pl.when(k < nkg)
        def _():
            acc[...] += _dot_nt(dzg_ref[...], wm_ref[...])

        @pl.when(k >= nkg)
        def _():
            acc[...] += _dot_nt(dza_ref[...], wm_ref[...])

        @pl.when(k == nk - 1)
        def _():
            xv = x_ref[...]
            r = _rms_rows(xv)
            xh = xv * r
            dh = acc[...]
            dg_ref[...] += jnp.sum(dh * xh, axis=0, keepdims=True)
            dn = dh * g_ref[...]
            dx_ref[...] = d_ref[...] + r * (dn - xh * jnp.mean(dn * xh, axis=-1, keepdims=True))

    return pl.pallas_call(
        body, name="dh", grid=(s // tm, nk),
        in_specs=[pl.BlockSpec((tm, tk), lambda i, k: (i, jnp.minimum(k, nkg - 1))),
                  pl.BlockSpec((tm, tk), lambda i, k: (i, jnp.maximum(k - nkg, 0))),
                  pl.BlockSpec((tm, GAP), lambda i, k: (i, 0)), pl.BlockSpec((D, tk), lambda i, k: (0, k)),
                  pl.BlockSpec((D, GAP), lambda i, k: (0, 0)), pl.BlockSpec((tm, D), lambda i, k: (i, 0)),
                  pl.BlockSpec((tm, D), lambda i, k: (i, 0)), pl.BlockSpec((1, D), lambda i, k: (0, 0))],
        out_specs=[pl.BlockSpec((tm, D), lambda i, k: (i, 0)), pl.BlockSpec((1, D), lambda i, k: (0, 0))],
        out_shape=[jax.ShapeDtypeStruct((s, D), F32), jax.ShapeDtypeStruct((1, D), F32)],
        scratch_shapes=[pltpu.VMEM((tm, D), F32)],
        compiler_params=_params(("arbitrary", "arbitrary")),
    )(dzg, dza, dga, wm, wga, x, dout, gpre)


def _adam(w, g, m, v, tr, name):
    rws, cols = w.shape

    def body(w_ref, g_ref, m_ref, v_ref, d_ref, mo_ref, vo_ref):
        gv = g_ref[...]
        mn = ADAM_B1 * m_ref[...] + (1.0 - ADAM_B1) * gv
        vn = ADAM_B2 * v_ref[...] + (1.0 - ADAM_B2) * (gv * gv)
        mh = mn / (1.0 - ADAM_B1 ** ADAM_STEP)
        vh = vn / (1.0 - ADAM_B2 ** ADAM_STEP)
        d_ref[...] = -ADAM_LR * (mh / (jnp.sqrt(vh) + ADAM_EPS) + ADAM_WD * w_ref[...])
        mo_ref[...] = mn
        vo_ref[...] = vn

    spec = pl.BlockSpec((tr, cols), lambda i: (i, 0))
    return pl.pallas_call(
        body, name=name, grid=(rws // tr,), in_specs=[spec] * 4, out_specs=[spec] * 3,
        out_shape=[jax.ShapeDtypeStruct((rws, cols), F32)] * 3,
        compiler_params=_params(("parallel",)),
    )(w, g, m, v)


def _place():
    x, y, c = lax.axis_index("x"), lax.axis_index("y"), lax.axis_index("c")
    chips = [(1 - x, y), (x, 1 - y), (1 - x, 1 - y)]
    return x, y, c, chips


def _gather_weights(win, wout, wa):
    nl = win.shape[0]
    hin, hout = D // 2, (D // NCHIP) // 2

    def body(win_ref, wout_ref, wa_ref, win_all, wout_all, wa_all, send_sems, recv_sems, wa_send, wa_recv, loc_sems):
        x, y, c, chips = _place()
        me = 2 * x + y
        sib = (x, y, 1 - c)

        def in_rows(chip):
            return win_all.at[chip, :, pl.ds(c * hin, hin), :]

        def out_rows(chip):
            return wout_all.at[:, pl.ds(chip * (2 * hout) + c * hout, hout), :]

        def cp(src, dst, a, k, to):
            return pltpu.make_async_remote_copy(src_ref=src, dst_ref=dst, send_sem=send_sems.at[a, k],
                                                recv_sem=recv_sems.at[a, k], device_id=to, device_id_type=MESH)

        def wa_cp(k, chip, to):
            return pltpu.make_async_remote_copy(src_ref=wa_ref, dst_ref=wa_all.at[chip], send_sem=wa_send.at[k],
                                                recv_sem=wa_recv.at[k], device_id=to, device_id_type=MESH)

        local = [pltpu.make_async_copy(win_ref, win_all.at[me], loc_sems.at[0]),
                 pltpu.make_async_copy(wout_ref, wout_all.at[:, pl.ds(me * 2 * hout, 2 * hout), :], loc_sems.at[1]),
                 pltpu.make_async_copy(wa_ref, wa_all.at[me], loc_sems.at[2])]
        for cpy in local:
            cpy.start()
        first = []
        for k, (px, py) in enumerate(chips):
            to = (px, py, c)
            first.append(cp(win_ref.at[:, pl.ds(c * hin, hin), :], in_rows(me), 0, k, to))
            first.append(cp(wout_ref.at[:, pl.ds(c * hout, hout), :], out_rows(me), 1, k, to))
            first.append(wa_cp(k, me, to))
        for cpy in first:
            cpy.start()
        passed = []
        for k, (px, py) in enumerate(chips):
            chip = 2 * px + py
            cp(in_rows(chip), in_rows(chip), 0, k, sib).wait_recv()
            fwd = cp(in_rows(chip), in_rows(chip), 0, 3 + k, sib)
            fwd.start()
            passed.append(fwd)
            cp(out_rows(chip), out_rows(chip), 1, k, sib).wait_recv()
            fwd = cp(out_rows(chip), out_rows(chip), 1, 3 + k, sib)
            fwd.start()
            passed.append(fwd)
            wa_cp(k, chip, sib).wait_recv()
        for k in range(3):
            cp(in_rows(me), in_rows(me), 0, 3 + k, sib).wait_recv()
            cp(out_rows(me), out_rows(me), 1, 3 + k, sib).wait_recv()
        for cpy in first + passed:
            cpy.wait_send()
        for cpy in local:
            cpy.wait()

    return pl.pallas_call(
        body, name="gather_weights", in_specs=[ANY, ANY, ANY], out_specs=[ANY, ANY, ANY],
        out_shape=[jax.ShapeDtypeStruct((NCHIP,) + win.shape, win.dtype),
                   jax.ShapeDtypeStruct((nl, D, D), wout.dtype),
                   jax.ShapeDtypeStruct((NCHIP,) + wa.shape, wa.dtype)],
        scratch_shapes=[pltpu.SemaphoreType.DMA((2, 6)), pltpu.SemaphoreType.DMA((2, 6)),
                        pltpu.SemaphoreType.DMA((3,)), pltpu.SemaphoreType.DMA((3,)), pltpu.SemaphoreType.DMA((3,))],
    )(win, wout, wa)


def _swap_halves(gin2, gout2):
    def body(gin_ref, gout_ref, rin, rout, send_sems, recv_sems):
        x, y, c, _ = _place()
        sib = (x, y, 1 - c)
        cps = [pltpu.make_async_remote_copy(src_ref=src.at[1 - c], dst_ref=dst, send_sem=send_sems.at[a],
                                            recv_sem=recv_sems.at[a], device_id=sib, device_id_type=MESH)
               for a, (src, dst) in enumerate([(gin_ref, rin), (gout_ref, rout)])]
        for cpy in cps:
            cpy.start()
        for cpy in cps:
            cpy.wait()

    return pl.pallas_call(
        body, name="swap_halves", in_specs=[ANY, ANY], out_specs=[ANY, ANY],
        out_shape=[jax.ShapeDtypeStruct(gin2.shape[1:], gin2.dtype), jax.ShapeDtypeStruct(gout2.shape[1:], gout2.dtype)],
        scratch_shapes=[pltpu.SemaphoreType.DMA((2,)), pltpu.SemaphoreType.DMA((2,))],
    )(gin2, gout2)


def _add_halves(c_idx, g2, r, tr, name):
    rows, cols = r.shape

    def body(c_ref, g_ref, r_ref, o_ref):
        o_ref[...] = (g_ref[0].astype(F32) + r_ref[...].astype(F32)).astype(_XDT)

    return pl.pallas_call(
        body, name=name,
        grid_spec=pltpu.PrefetchScalarGridSpec(
            num_scalar_prefetch=1, grid=(rows // tr,),
            in_specs=[pl.BlockSpec((1, tr, cols), lambda i, c_ref: (c_ref[0], i, 0)),
                      pl.BlockSpec((tr, cols), lambda i, c_ref: (i, 0))],
            out_specs=pl.BlockSpec((tr, cols), lambda i, c_ref: (i, 0))),
        out_shape=jax.ShapeDtypeStruct((rows, cols), _XDT),
        compiler_params=_params(("parallel",)),
    )(c_idx, g2, r)


def _send_to_owners(pin, pout):
    nl = pin.shape[0]

    def body(pin_ref, pout_ref, rin, rout, send_sems, recv_sems):
        x, y, c, chips = _place()
        cps = []
        for k, (px, py) in enumerate(chips):
            chip = 2 * px + py
            for a, (src, dst) in enumerate([(pin_ref, rin), (pout_ref, rout)]):
                cps.append(pltpu.make_async_remote_copy(src_ref=src.at[:, chip], dst_ref=dst.at[k], send_sem=send_sems.at[a, k],
                                                        recv_sem=recv_sems.at[a, k], device_id=(px, py, c), device_id_type=MESH))
        for cpy in cps:
            cpy.start()
        for cpy in cps:
            cpy.wait()

    return pl.pallas_call(
        body, name="send_to_owners", in_specs=[ANY, ANY], out_specs=[ANY, ANY],
        out_shape=[jax.ShapeDtypeStruct((3, nl) + pin.shape[2:], pin.dtype),
                   jax.ShapeDtypeStruct((3, nl) + pout.shape[2:], pout.dtype)],
        scratch_shapes=[pltpu.SemaphoreType.DMA((2, 3)), pltpu.SemaphoreType.DMA((2, 3))],
    )(pin, pout)


def _add_chips(chip_idx, p, r, tr, name):
    nl, _, rows, cols = p.shape

    def body(c_ref, p_ref, r_ref, o_ref):
        o_ref[0] = ((p_ref[0, 0].astype(F32) + r_ref[0, 0].astype(F32)) + r_ref[1, 0].astype(F32)) + r_ref[2, 0].astype(F32)

    return pl.pallas_call(
        body, name=name,
        grid_spec=pltpu.PrefetchScalarGridSpec(
            num_scalar_prefetch=1, grid=(nl, rows // tr),
            in_specs=[pl.BlockSpec((1, 1, tr, cols), lambda l, i, c_ref: (l, c_ref[0], i, 0)),
                      pl.BlockSpec((3, 1, tr, cols), lambda l, i, c_ref: (0, l, i, 0))],
            out_specs=pl.BlockSpec((1, tr, cols), lambda l, i, c_ref: (l, i, 0))),
        out_shape=jax.ShapeDtypeStruct((nl, rows, cols), F32),
        compiler_params=_params(("parallel", "parallel")),
    )(chip_idx, p, r)


def _join_halves(hin, hout):
    nl, rin_, cin = hin.shape
    _, rout_, cout = hout.shape

    def body(hin_ref, hout_ref, fin, fout, send_sems, recv_sems, loc_sems):
        x, y, c, _ = _place()
        sib = (x, y, 1 - c)
        pairs = [(hin_ref, fin.at[:, pl.ds(c * rin_, rin_), :]), (hout_ref, fout.at[:, pl.ds(c * rout_, rout_), :])]
        local = [pltpu.make_async_copy(src, dst, loc_sems.at[a]) for a, (src, dst) in enumerate(pairs)]
        remote = [pltpu.make_async_remote_copy(src_ref=src, dst_ref=dst, send_sem=send_sems.at[a], recv_sem=recv_sems.at[a],
                                               device_id=sib, device_id_type=MESH) for a, (src, dst) in enumerate(pairs)]
        for cpy in local + remote:
            cpy.start()
        other = [(hin_ref, fin.at[:, pl.ds((1 - c) * rin_, rin_), :]), (hout_ref, fout.at[:, pl.ds((1 - c) * rout_, rout_), :])]
        for a, (src, dst) in enumerate(other):
            pltpu.make_async_remote_copy(src_ref=src, dst_ref=dst, send_sem=send_sems.at[a], recv_sem=recv_sems.at[a],
                                         device_id=sib, device_id_type=MESH).wait_recv()
        for cpy in remote:
            cpy.wait_send()
        for cpy in local:
            cpy.wait()

    return pl.pallas_call(
        body, name="join_halves", in_specs=[ANY, ANY], out_specs=[ANY, ANY],
        out_shape=[jax.ShapeDtypeStruct((nl, 2 * rin_, cin), F32), jax.ShapeDtypeStruct((nl, 2 * rout_, cout), F32)],
        scratch_shapes=[pltpu.SemaphoreType.DMA((2,)), pltpu.SemaphoreType.DMA((2,)), pltpu.SemaphoreType.DMA((2,))],
    )(hin, hout)


def _allreduce_small(sg):
    rows = sg.shape[0]
    vm = pl.BlockSpec(memory_space=pltpu.VMEM)

    def body(sg_ref, tot_ref, all_ref, send_sems, recv_sems):
        x, y, c, _ = _place()
        me = 4 * x + 2 * y + c
        all_ref[me] = sg_ref[...]
        cps = []
        for mask in range(1, 8):
            to = (1 - x if mask & 4 else x, 1 - y if mask & 2 else y, 1 - c if mask & 1 else c)
            cps.append(pltpu.make_async_remote_copy(src_ref=sg_ref, dst_ref=all_ref.at[me], send_sem=send_sems.at[mask - 1],
                                                    recv_sem=recv_sems.at[mask - 1], device_id=to, device_id_type=MESH))
        for cpy in cps:
            cpy.start()
        for cpy in cps:
            cpy.wait()
        acc = all_ref[0]
        for d in range(1, 8):
            acc = acc + all_ref[d]
        tot_ref[...] = acc

    return pl.pallas_call(
        body, name="allreduce_small", in_specs=[vm], out_specs=[vm, vm],
        out_shape=[jax.ShapeDtypeStruct((rows, 128), F32), jax.ShapeDtypeStruct((8, rows, 128), F32)],
        scratch_shapes=[pltpu.SemaphoreType.DMA((7,)), pltpu.SemaphoreType.DMA((7,))],
        compiler_params=_params(),
    )(sg)[0]


def _to_internal(w):
    lead = w.shape[:-1]
    cuts = [0, 512, 1024, 2048, 3072, 3088, 4112, 5136, 6160, 7184]
    gq, gk, gv, gg, ga, aq, ak, av, ag = [w[..., cuts[i]:cuts[i + 1]] for i in range(9)]
    g = jnp.concatenate([gq.reshape(lead + (GH, GDK)), gk.reshape(lead + (GH, GDK)),
                         gv.reshape(lead + (GH, GDV)), gg.reshape(lead + (GH, GDV))], axis=-1).reshape(lead + (ZG,))
    a = jnp.concatenate([t.reshape(lead + (AH, AHD)) for t in (aq, ak, av, ag)], axis=-1).reshape(lead + (ZA,))
    pad = [(0, 0)] * len(lead) + [(0, GAP - RANK)]
    return jnp.concatenate([g, a], axis=-1), jnp.pad(ga, pad)


def _from_internal(g, a, ga):
    lead = g.shape[:-1]
    g = g.reshape(lead + (GH, GW))
    a = a.reshape(lead + (AH, AW))
    parts = [g[..., 0:GDK], g[..., GDK:2 * GDK], g[..., 2 * GDK:2 * GDK + GDV], g[..., 2 * GDK + GDV:GW]]
    parts = [t.reshape(lead + (-1,)) for t in parts] + [ga[..., 0:RANK]]
    parts += [a[..., i * AHD:(i + 1) * AHD].reshape(lead + (-1,)) for i in range(4)]
    return jnp.concatenate(parts, axis=-1)


def _pack_rows(parts):
    rows = []
    for t in parts:
        flat = t.reshape(-1)
        rows.append(jnp.pad(flat, (0, (-flat.shape[0]) % 128)).reshape(-1, 128))
    buf = jnp.concatenate(rows, axis=0)
    return jnp.pad(buf, ((0, (-buf.shape[0]) % 8), (0, 0)))


def _unpack_rows(buf, shapes):
    out, r = [], 0
    for shp in shapes:
        n = 1
        for d in shp:
            n *= d
        nr = -(-n // 128)
        out.append(buf[r:r + nr].reshape(-1)[:n].reshape(shp))
        r += nr
    return out


def _layer_fwd(x, wm, wga, wout, gpre, gpost, wa, ba, ggla, gatt, rbx):
    z, zga, h = _inproj(x, gpre, wm, wga)
    cg = _gla_fwd(z, zga, wa, ba, ggla)
    ca, oraw, lse = _att_fwd(z, rbx, gatt)
    y, xo = _outproj(cg, ca, wout, x, gpost)
    return xo, (x, z, zga, h, cg, ca, oraw, lse, y)


def _layer_bwd(dout, saved, wm, wga, wout, gpre, gpost, wa, ba, ggla, gatt, rbx):
    x, z, zga, h, cg, ca, oraw, lse, y = saved
    dy, dcg, dca, dgpost = _post_bwd(dout, y, gpost, wout)
    dwout = jnp.concatenate([_matmul_tn(cg, dy, _XDT, 512, 1024, "dwout_gla"),
                             _matmul_tn(ca, dy, _XDT, 512, 1024, "dwout_att")], axis=0)
    dza, dgatt, dbx = _att_bwd(z, oraw, lse, dca, rbx, gatt)
    dzg, dga, dwa, dba, dggla = _gla_bwd(z, zga, wa, ba, ggla, dcg)
    dx, dgpre = _dh(dzg, dza, dga, wm, wga, x, dout, gpre)
    dwin = _from_internal(_matmul_tn(h, dzg, _XDT, 512, 1024, "dwin_gla"), _matmul_tn(h, dza, _XDT, 512, 1024, "dwin_att"),
                          _matmul_tn(h, dga, _XDT, 512, GAP, "dwin_gate"))
    drb = jnp.concatenate([jnp.zeros((AH, 1), F32), dbx[:, 0, ::-1]], axis=1)
    return dx, dwin, dwout, (dgpre[0], dgpost[0], dwa[0:RANK], dba[0], dggla[0], dgatt[0], drb)


def _rel_rows(rb):
    return rb[:, :0:-1][:, None, :]


def kernel(x, w_in, w_out, g_pre, g_post, w_alpha, b_alpha, g_gla, g_att, rel_bias, loss_target, m_w_in, m_w_out, m_g_pre, m_g_post, m_w_alpha, m_b_alpha, m_g_gla, m_g_att, m_rel_bias, v_w_in, v_w_out, v_g_pre, v_g_post, v_w_alpha, v_b_alpha, v_g_gla, v_g_att, v_rel_bias):
    nl = w_in.shape[0]
    ax, ay, ac = lax.axis_index("x"), lax.axis_index("y"), lax.axis_index("c")
    chip = 2 * ax + ay
    c_idx = jnp.reshape(ac, (1,)).astype(jnp.int32)
    chip_idx = jnp.reshape(chip, (1,)).astype(jnp.int32)

    win_all, wout_all, wa_all = _gather_weights(w_in.astype(_CDT), w_out.astype(_CDT), w_alpha)
    wfull = jnp.transpose(win_all, (1, 2, 0, 3)).reshape(nl, D, DIN)
    wm, wga = _to_internal(wfull)
    wa_full = jnp.transpose(wa_all, (1, 2, 0, 3)).reshape(nl, RANK, GH * GDK)
    wa_pad = jnp.pad(wa_full, ((0, 0), (0, GAP - RANK), (0, 0))).astype(_CDT)
    rbx = [_rel_rows(rel_bias[l]) for l in range(nl)]

    def weights(l):
        return (wm[l], wga[l], wout_all[l], g_pre[l][None], g_post[l][None], wa_pad[l], b_alpha[l][None],
                g_gla[l][None], g_att[l][None], rbx[l])

    h = x[0]
    saved = []
    for l in range(nl):
        h, sv = _layer_fwd(h, *weights(l))
        saved.append(sv)
    dout, loss_part = _loss_grad(h, loss_target[0])
    loss = lax.psum(loss_part[0, 0], ("x", "y", "c"))

    dwin, dwout, small = [None] * nl, [None] * nl, [None] * nl
    for l in reversed(range(nl)):
        dout, dwin[l], dwout[l], small[l] = _layer_bwd(dout, saved[l], *weights(l))
    grad_x = dout[None]

    gin = jnp.stack(dwin).reshape(nl, 2, D // 2, NCHIP, SHARD)
    gin2 = jnp.transpose(gin, (1, 0, 3, 2, 4)).reshape(2, nl * NCHIP * (D // 2), SHARD)
    gout = jnp.stack(dwout).reshape(nl, NCHIP, 2, D // NCHIP // 2, D)
    gout2 = jnp.transpose(gout, (2, 0, 1, 3, 4)).reshape(2, nl * NCHIP * (D // NCHIP // 2), D)
    rin, rout = _swap_halves(gin2, gout2)
    pin = _add_halves(c_idx, gin2, rin, 512, "add_halves_in").reshape(nl, NCHIP, D // 2, SHARD)
    pout = _add_halves(c_idx, gout2, rout, 256, "add_halves_out").reshape(nl, NCHIP, D // NCHIP // 2, D)
    bin_, bout = _send_to_owners(pin, pout)
    hin = _add_chips(chip_idx, pin, bin_, 256, "add_chips_in")
    hout = _add_chips(chip_idx, pout, bout, 256, "add_chips_out")
    g_w_in, g_w_out = _join_halves(hin, hout)

    def adam_big(w, g, m, v, name):
        shp = w.shape
        flat = lambda t: t.reshape(-1, shp[-1])
        return [t.reshape(shp) for t in _adam(flat(w), flat(g), flat(m), flat(v), 256, name)]

    d_w_in, nm_w_in, nv_w_in = adam_big(w_in, g_w_in, m_w_in, v_w_in, "adam_w_in")
    d_w_out, nm_w_out, nv_w_out = adam_big(w_out, g_w_out, m_w_out, v_w_out, "adam_w_out")

    stacked = [jnp.stack([small[l][i] for l in range(nl)]) for i in range(7)]
    g_small = _unpack_rows(_allreduce_small(_pack_rows(stacked)), [t.shape for t in stacked])
    g_gpre, g_gpost, g_wa_full, g_ba, g_ggla, g_gatt, g_rb = g_small
    g_wa = lax.dynamic_slice_in_dim(g_wa_full, chip * GDK, GDK, axis=2)
    names = [(g_pre, m_g_pre, v_g_pre, g_gpre), (g_post, m_g_post, v_g_post, g_gpost), (w_alpha, m_w_alpha, v_w_alpha, g_wa),
             (b_alpha, m_b_alpha, v_b_alpha, g_ba), (g_gla, m_g_gla, v_g_gla, g_ggla), (g_att, m_g_att, v_g_att, g_gatt),
             (rel_bias, m_rel_bias, v_rel_bias, g_rb)]
    shapes = [t[0].shape for t in names]
    packed = [_pack_rows([t[i] for t in names]) for i in range(4)]
    d_s, nm_s, nv_s = [_unpack_rows(t, shapes) for t in _adam(packed[0], packed[3], packed[1], packed[2], packed[0].shape[0], "adam_small")]

    grads = [g_w_in, g_w_out, g_gpre, g_gpost, g_wa, g_ba, g_ggla, g_gatt, g_rb]
    deltas = [d_w_in, d_w_out] + d_s
    new_m = [nm_w_in, nm_w_out] + nm_s
    new_v = [nv_w_in, nv_w_out] + nv_s
    return (loss, grad_x, *grads, *deltas, *new_m, *new_v)
```

```python
import functools

import jax
import jax.numpy as jnp
from jax import lax
from jax.experimental import pallas as pl
from jax.experimental.pallas import tpu as pltpu

D = 2048
DEPTH = 4
CHUNK = 64
GH, GDK, GDV = 4, 128, 256
DGLA = GH * GDV
RANK = 16
TAU = 16.0
AH, AHD = 8, 128
DATT = AH * AHD
LEFT = 8
NREL = 257
EPS = 1e-6
DIN = 7184
ADAM_LR, ADAM_B1, ADAM_B2, ADAM_EPS, ADAM_WD, ADAM_STEP = 0.001, 0.9, 0.999, 1e-08, 0.01, 10

GW = 2 * GDK + 2 * GDV
AW = 4 * AHD
ZG = GH * GW
ZA = AH * AW
ZM = ZG + ZA
GAP = 128
QB = 2 * CHUNK
BANDW = (LEFT + 2) * CHUNK
PADK = LEFT * CHUNK
NCHIP = 4
SHARD = DIN // NCHIP
NEG = -1e30
F32 = jnp.float32
_CDT = jnp.bfloat16
_XDT = jnp.bfloat16
_VMEM = 56 * 1024 * 1024
MESH = pl.DeviceIdType.MESH
ANY = pl.BlockSpec(memory_space=pl.ANY)


def _dot(a, b):
    return jnp.dot(a, b, preferred_element_type=F32)


def _dot_nt(a, b):
    return lax.dot_general(a, b, (((1,), (1,)), ((), ())), preferred_element_type=F32)


def _dot_tn(a, b):
    return lax.dot_general(a, b, (((0,), (0,)), ((), ())), preferred_element_type=F32)


def _rms_rows(v):
    return lax.rsqrt(jnp.mean(v * v, axis=-1, keepdims=True) + EPS)


def _sigmoid(v):
    return 1.0 / (1.0 + jnp.exp(-v))


def _log_sigmoid(v):
    return jnp.minimum(v, 0.0) - jnp.log(1.0 + jnp.exp(-jnp.abs(v)))


def _exact_dot(tri, v):
    hi = v.astype(_CDT)
    r1 = v - hi.astype(F32)
    mid = r1.astype(_CDT)
    lo = (r1 - mid.astype(F32)).astype(_CDT)
    return _dot(tri, hi) + _dot(tri, mid) + _dot(tri, lo)


def _tri(strict):
    row = lax.broadcasted_iota(jnp.int32, (CHUNK, CHUNK), 0)
    col = lax.broadcasted_iota(jnp.int32, (CHUNK, CHUNK), 1)
    return jnp.where((col < row) if strict else (col <= row), 1.0, 0.0).astype(_CDT)


def _norm_gate_bwd(o, g, gate, dcat):
    r = _rms_rows(o)
    oh = o * r
    sg = _sigmoid(gate)
    dn = dcat * (gate * sg)
    dgate = dcat * (oh * g) * (sg * (1.0 + gate * (1.0 - sg)))
    dg = jnp.sum(dn * oh, axis=0, keepdims=True)
    dnn = dn * g
    do = r * (dnn - oh * jnp.mean(dnn * oh, axis=-1, keepdims=True))
    return do, dgate, dg


def _params(sem=None, vmem=_VMEM):
    return pltpu.CompilerParams(dimension_semantics=sem, vmem_limit_bytes=vmem)


def _inproj(x, g, wm, wga, tm=512, tn=1024):
    s = x.shape[0]

    def body(x_ref, g_ref, wm_ref, wga_ref, z_ref, zga_ref, h_ref, hs):
        @pl.when(pl.program_id(1) == 0)
        def _():
            xv = x_ref[...]
            hv = (xv * _rms_rows(xv) * g_ref[...]).astype(_CDT)
            hs[...] = hv
            h_ref[...] = hv
            zga_ref[...] = _dot(hv, wga_ref[...]).astype(_CDT)

        z_ref[...] = _dot(hs[...], wm_ref[...]).astype(_CDT)

    return pl.pallas_call(
        body, name="inproj", grid=(s // tm, ZM // tn),
        in_specs=[pl.BlockSpec((tm, D), lambda i, j: (i, 0)), pl.BlockSpec((1, D), lambda i, j: (0, 0)),
                  pl.BlockSpec((D, tn), lambda i, j: (0, j)), pl.BlockSpec((D, GAP), lambda i, j: (0, 0))],
        out_specs=[pl.BlockSpec((tm, tn), lambda i, j: (i, j)), pl.BlockSpec((tm, GAP), lambda i, j: (i, 0)),
                   pl.BlockSpec((tm, D), lambda i, j: (i, 0))],
        out_shape=[jax.ShapeDtypeStruct((s, ZM), _CDT), jax.ShapeDtypeStruct((s, GAP), _CDT),
                   jax.ShapeDtypeStruct((s, D), _CDT)],
        scratch_shapes=[pltpu.VMEM((tm, D), _CDT)],
        compiler_params=_params(("parallel", "arbitrary")),
    )(x, g, wm, wga)


def _gla_fwd(z, zga, wa, ba, ggla):
    s = z.shape[0]
    nc = s // CHUNK

    def body(zg_ref, zga_ref, wa_ref, ba_ref, g_ref, cat_ref, la_s, st):
        la_s[...] = _log_sigmoid(_dot(zga_ref[...], wa_ref[...]) + ba_ref[...]) * (1.0 / TAU)
        st[...] = jnp.zeros_like(st)
        tri = _tri(False)

        def step(n, carry):
            rows = pl.ds(pl.multiple_of(n * CHUNK, CHUNK), CHUNK)
            la = la_s[rows, :]
            lc = _exact_dot(tri, la)
            lend = jnp.sum(la, axis=0, keepdims=True)
            kdec = (zg_ref[rows, GDK:2 * GDK].astype(F32) * jnp.exp(lend - lc)).astype(_CDT)
            stn = jnp.exp(lend) * st[...] + _dot_tn(zg_ref[rows, 2 * GDK:2 * GDK + GDV], kdec)
            st[...] = stn
            qs = (zg_ref[rows, 0:GDK].astype(F32) * (GDK ** -0.5)).astype(_CDT)
            o = _dot_nt(qs, stn.astype(_CDT))
            gate = zg_ref[rows, 2 * GDK + GDV:GW].astype(F32)
            cat_ref[rows, :] = (o * _rms_rows(o) * g_ref[...] * (gate * _sigmoid(gate))).astype(_CDT)
            return carry

        lax.fori_loop(0, nc, step, 0)

    return pl.pallas_call(
        body, name="gla_fwd", grid=(GH,),
        in_specs=[pl.BlockSpec((s, GW), lambda h: (0, h)), pl.BlockSpec((s, GAP), lambda h: (0, 0)),
                  pl.BlockSpec((GAP, GDK), lambda h: (0, h)), pl.BlockSpec((1, GDK), lambda h: (0, h)),
                  pl.BlockSpec((1, GDV), lambda h: (0, h))],
        out_specs=pl.BlockSpec((s, GDV), lambda h: (0, h)),
        out_shape=jax.ShapeDtypeStruct((s, DGLA), _CDT),
        scratch_shapes=[pltpu.VMEM((s, GDK), F32), pltpu.VMEM((GDV, GDK), F32)],
        compiler_params=_params(("arbitrary",)),
    )(z, zga, wa, ba, ggla)


def _band_bias(b0):
    row = lax.broadcasted_iota(jnp.int32, (QB, 256), 0)
    col = lax.broadcasted_iota(jnp.int32, (QB, 256), 1)
    lane = lax.broadcasted_iota(jnp.int32, (1, 256), 1)
    c0 = jnp.sum(jnp.where(lane == 0, b0, 0.0), axis=1, keepdims=True)
    xv = jnp.broadcast_to(b0, (QB, 256))
    for bit in range(7):
        xv = jnp.where(((row >> bit) & 1) == 1, pltpu.roll(xv, 1 << bit, 1), xv)
    xv = jnp.where(col < row, c0, xv)
    return jnp.concatenate([jnp.broadcast_to(c0, (QB, BANDW - 256)), xv], axis=1)


def _band_static_mask():
    row = lax.broadcasted_iota(jnp.int32, (QB, BANDW), 0) >> 6
    col = lax.broadcasted_iota(jnp.int32, (QB, BANDW), 1) >> 6
    return (col >= row) & (col <= row + LEFT)


def _fold_bias_grad(t):
    row = lax.broadcasted_iota(jnp.int32, (QB, 256), 0)
    col = lax.broadcasted_iota(jnp.int32, (QB, 256), 1)
    xv = t[:, BANDW - 256:]
    low = col < row
    far = jnp.sum(t[:, 0:BANDW - 256], axis=1, keepdims=True) + jnp.sum(jnp.where(low, xv, 0.0), axis=1, keepdims=True)
    far = jnp.sum(far, axis=0, keepdims=True)
    xv = jnp.where(low, 0.0, xv)
    for bit in range(7):
        xv = jnp.where(((row >> bit) & 1) == 1, pltpu.roll(xv, 256 - (1 << bit), 1), xv)
    dp = jnp.sum(xv, axis=0, keepdims=True)
    lane = lax.broadcasted_iota(jnp.int32, (1, 256), 1)
    return dp + jnp.where(lane == 0, far, 0.0)


def _att_fwd(z, rbx, gatt):
    s = z.shape[0]
    nb = s // QB

    def body(za_ref, rb_ref, g_ref, cat_ref, o_ref, lse_ref, kp, vp, bias_s):
        kp[0:PADK, :] = jnp.zeros((PADK, AHD), _CDT)
        vp[0:PADK, :] = jnp.zeros((PADK, AHD), _CDT)
        kp[PADK:, :] = za_ref[:, AHD:2 * AHD]
        vp[PADK:, :] = za_ref[:, 2 * AHD:3 * AHD]
        bias_s[...] = jnp.where(_band_static_mask(), _band_bias(rb_ref[0]), NEG)

        def step(b, carry):
            r0 = pl.multiple_of(b * QB, QB)
            rows = pl.ds(r0, QB)
            band = pl.ds(r0, BANDW)
            col = lax.broadcasted_iota(jnp.int32, (QB, BANDW), 1)
            sc = _dot_nt(za_ref[rows, 0:AHD], kp[band, :]) * (AHD ** -0.5) + bias_s[...]
            sc = jnp.where(col >= PADK - r0, sc, NEG)
            m = jnp.max(sc, axis=-1, keepdims=True)
            p = jnp.exp(sc - m)
            l = jnp.sum(p, axis=-1, keepdims=True)
            o = _dot((p * (1.0 / l)).astype(_CDT), vp[band, :])
            o_ref[rows, :] = o.astype(_CDT)
            lse_ref[rows, :] = jnp.broadcast_to(m + jnp.log(l), (QB, AHD))
            gate = za_ref[rows, 3 * AHD:AW].astype(F32)
            cat_ref[rows, :] = (o * _rms_rows(o) * g_ref[...] * (gate * _sigmoid(gate))).astype(_CDT)
            return carry

        lax.fori_loop(0, nb, step, 0)

    return pl.pallas_call(
        body, name="att_fwd", grid=(AH,),
        in_specs=[pl.BlockSpec((s, AW), lambda h: (0, ZG // AW + h)), pl.BlockSpec((1, 1, 256), lambda h: (h, 0, 0)),
                  pl.BlockSpec((1, AHD), lambda h: (0, h))],
        out_specs=[pl.BlockSpec((s, AHD), lambda h: (0, h)), pl.BlockSpec((s, AHD), lambda h: (0, h)),
                   pl.BlockSpec((s, AHD), lambda h: (0, h))],
        out_shape=[jax.ShapeDtypeStruct((s, DATT), _CDT), jax.ShapeDtypeStruct((s, DATT), _CDT),
                   jax.ShapeDtypeStruct((s, DATT), F32)],
        scratch_shapes=[pltpu.VMEM((s + PADK, AHD), _CDT), pltpu.VMEM((s + PADK, AHD), _CDT),
                        pltpu.VMEM((QB, BANDW), F32)],
        compiler_params=_params(("arbitrary",)),
    )(z, rbx, gatt)


def _outproj(cg, ca, wout, x, gpost, tm=256):
    s = x.shape[0]

    def body(cg_ref, ca_ref, w_ref, x_ref, g_ref, y_ref, xo_ref):
        y = _dot(cg_ref[...], w_ref[0:DGLA, :]) + _dot(ca_ref[...], w_ref[DGLA:, :])
        y_ref[...] = y
        xo_ref[...] = x_ref[...] + y * _rms_rows(y) * g_ref[...]

    return pl.pallas_call(
        body, name="outproj", grid=(s // tm,),
        in_specs=[pl.BlockSpec((tm, DGLA), lambda i: (i, 0)), pl.BlockSpec((tm, DATT), lambda i: (i, 0)),
                  pl.BlockSpec((D, D), lambda i: (0, 0)), pl.BlockSpec((tm, D), lambda i: (i, 0)),
                  pl.BlockSpec((1, D), lambda i: (0, 0))],
        out_specs=[pl.BlockSpec((tm, D), lambda i: (i, 0)), pl.BlockSpec((tm, D), lambda i: (i, 0))],
        out_shape=[jax.ShapeDtypeStruct((s, D), F32), jax.ShapeDtypeStruct((s, D), F32)],
        compiler_params=_params(("parallel",)),
    )(cg, ca, wout, x, gpost)


def _loss_grad(xo, tgt, tm=256):
    s = xo.shape[0]

    def body(xo_ref, t_ref, d_ref, l_ref):
        @pl.when(pl.program_id(0) == 0)
        def _():
            l_ref[...] = jnp.zeros_like(l_ref)

        e = xo_ref[...] - t_ref[...]
        d_ref[...] = e * (1.0 / D)
        l_ref[...] += jnp.sum(jnp.sum(e * e, axis=1, keepdims=True), axis=0, keepdims=True) * (0.5 / D)

    return pl.pallas_call(
        body, name="loss_grad", grid=(s // tm,),
        in_specs=[pl.BlockSpec((tm, D), lambda i: (i, 0)), pl.BlockSpec((tm, D), lambda i: (i, 0))],
        out_specs=[pl.BlockSpec((tm, D), lambda i: (i, 0)), pl.BlockSpec((1, 1), lambda i: (0, 0))],
        out_shape=[jax.ShapeDtypeStruct((s, D), F32), jax.ShapeDtypeStruct((1, 1), F32)],
        compiler_params=_params(("arbitrary",)),
    )(xo, tgt)


def _post_bwd(dout, y, gpost, wout, tm=256):
    s = y.shape[0]

    def body(d_ref, y_ref, g_ref, w_ref, dy_ref, dcg_ref, dca_ref, dg_ref):
        @pl.when(pl.program_id(0) == 0)
        def _():
            dg_ref[...] = jnp.zeros_like(dg_ref)

        yv = y_ref[...]
        r = _rms_rows(yv)
        yh = yv * r
        dv = d_ref[...]
        dg_ref[...] += jnp.sum(dv * yh, axis=0, keepdims=True)
        dn = dv * g_ref[...]
        dyb = (r * (dn - yh * jnp.mean(dn * yh, axis=-1, keepdims=True))).astype(_CDT)
        dy_ref[...] = dyb
        dcg_ref[...] = _dot_nt(dyb, w_ref[0:DGLA, :]).astype(_CDT)
        dca_ref[...] = _dot_nt(dyb, w_ref[DGLA:, :]).astype(_CDT)

    return pl.pallas_call(
        body, name="post_bwd", grid=(s // tm,),
        in_specs=[pl.BlockSpec((tm, D), lambda i: (i, 0)), pl.BlockSpec((tm, D), lambda i: (i, 0)),
                  pl.BlockSpec((1, D), lambda i: (0, 0)), pl.BlockSpec((D, D), lambda i: (0, 0))],
        out_specs=[pl.BlockSpec((tm, D), lambda i: (i, 0)), pl.BlockSpec((tm, DGLA), lambda i: (i, 0)),
                   pl.BlockSpec((tm, DATT), lambda i: (i, 0)), pl.BlockSpec((1, D), lambda i: (0, 0))],
        out_shape=[jax.ShapeDtypeStruct((s, D), _CDT), jax.ShapeDtypeStruct((s, DGLA), _CDT),
                   jax.ShapeDtypeStruct((s, DATT), _CDT), jax.ShapeDtypeStruct((1, D), F32)],
        compiler_params=_params(("arbitrary",)),
    )(dout, y, gpost, wout)


def _matmul_tn(a, b, out_dtype, tm, tn, name):
    k, m = a.shape
    n = b.shape[1]

    def body(a_ref, b_ref, o_ref):
        o_ref[...] = _dot_tn(a_ref[...], b_ref[...]).astype(out_dtype)

    return pl.pallas_call(
        body, name=name, grid=(m // tm, n // tn),
        in_specs=[pl.BlockSpec((k, tm), lambda i, j: (0, i)), pl.BlockSpec((k, tn), lambda i, j: (0, j))],
        out_specs=pl.BlockSpec((tm, tn), lambda i, j: (i, j)),
        out_shape=jax.ShapeDtypeStruct((m, n), out_dtype),
        compiler_params=_params(("parallel", "parallel")),
    )(a, b)


def _att_bwd(z, oraw, lse, dca, rbx, gatt):
    s = z.shape[0]
    nb = s // QB

    def body(za_ref, o_ref, lse_ref, dc_ref, rb_ref, g_ref, dz_ref, dg_ref, db_ref, kp, vp, dkp, dvp, bias_s, t_s, dg_s):
        kp[0:PADK, :] = jnp.zeros((PADK, AHD), _CDT)
        vp[0:PADK, :] = jnp.zeros((PADK, AHD), _CDT)
        kp[PADK:, :] = za_ref[:, AHD:2 * AHD]
        vp[PADK:, :] = za_ref[:, 2 * AHD:3 * AHD]
        dkp[...] = jnp.zeros_like(dkp)
        dvp[...] = jnp.zeros_like(dvp)
        t_s[...] = jnp.zeros_like(t_s)
        dg_s[...] = jnp.zeros_like(dg_s)
        bias_s[...] = jnp.where(_band_static_mask(), _band_bias(rb_ref[0]), NEG)

        def step(b, carry):
            r0 = pl.multiple_of(b * QB, QB)
            rows = pl.ds(r0, QB)
            band = pl.ds(r0, BANDW)
            col = lax.broadcasted_iota(jnp.int32, (QB, BANDW), 1)
            o = o_ref[rows, :].astype(F32)
            do, dgate, dg = _norm_gate_bwd(o, g_ref[...], za_ref[rows, 3 * AHD:AW].astype(F32),
                                           dc_ref[rows, :].astype(F32))
            dg_s[...] += dg
            q = za_ref[rows, 0:AHD]
            kb = kp[band, :]
            sc = _dot_nt(q, kb) * (AHD ** -0.5) + bias_s[...]
            sc = jnp.where(col >= PADK - r0, sc, NEG)
            p = jnp.exp(sc - jnp.max(lse_ref[rows, :], axis=-1, keepdims=True))
            dob = do.astype(_CDT)
            dp = _dot_nt(dob, vp[band, :])
            ds = p * (dp - jnp.sum(do * o, axis=-1, keepdims=True))
            t_s[...] += ds
            dsb = (ds * (AHD ** -0.5)).astype(_CDT)
            dz_ref[rows, 0:AHD] = _dot(dsb, kb).astype(_CDT)
            dz_ref[rows, 3 * AHD:AW] = dgate.astype(_CDT)
            dkp[band, :] += _dot_tn(dsb, q)
            dvp[band, :] += _dot_tn(p.astype(_CDT), dob)
            return carry

        lax.fori_loop(0, nb, step, 0)
        dz_ref[:, AHD:2 * AHD] = dkp[PADK:, :].astype(_CDT)
        dz_ref[:, 2 * AHD:3 * AHD] = dvp[PADK:, :].astype(_CDT)
        dg_ref[...] = dg_s[...]
        db_ref[0] = _fold_bias_grad(t_s[...])

    return pl.pallas_call(
        body, name="att_bwd", grid=(AH,),
        in_specs=[pl.BlockSpec((s, AW), lambda h: (0, ZG // AW + h)), pl.BlockSpec((s, AHD), lambda h: (0, h)),
                  pl.BlockSpec((s, AHD), lambda h: (0, h)), pl.BlockSpec((s, AHD), lambda h: (0, h)),
                  pl.BlockSpec((1, 1, 256), lambda h: (h, 0, 0)), pl.BlockSpec((1, AHD), lambda h: (0, h))],
        out_specs=[pl.BlockSpec((s, AW), lambda h: (0, h)), pl.BlockSpec((1, AHD), lambda h: (0, h)),
                   pl.BlockSpec((1, 1, 256), lambda h: (h, 0, 0))],
        out_shape=[jax.ShapeDtypeStruct((s, ZA), _CDT), jax.ShapeDtypeStruct((1, DATT), F32),
                   jax.ShapeDtypeStruct((AH, 1, 256), F32)],
        scratch_shapes=[pltpu.VMEM((s + PADK, AHD), _CDT), pltpu.VMEM((s + PADK, AHD), _CDT),
                        pltpu.VMEM((s + PADK, AHD), F32), pltpu.VMEM((s + PADK, AHD), F32),
                        pltpu.VMEM((QB, BANDW), F32), pltpu.VMEM((QB, BANDW), F32), pltpu.VMEM((1, AHD), F32)],
        compiler_params=_params(("arbitrary",)),
    )(z, oraw, lse, dca, rbx, gatt)


def _gla_bwd(z, zga, wa, ba, ggla, dcg):
    s = z.shape[0]
    nc = s // CHUNK

    def body(zg_ref, zga_ref, wa_ref, ba_ref, g_ref, dc_ref, dz_ref, dga_ref, dwa_ref, dba_ref, dg_ref,
             la_s, om_s, sall, dpre_s, c_s, dga_s, dg_s):
        h = pl.program_id(0)
        pre = _dot(zga_ref[...], wa_ref[...]) + ba_ref[...]
        la_s[...] = _log_sigmoid(pre) * (1.0 / TAU)
        om_s[...] = (1.0 - _sigmoid(pre)) * (1.0 / TAU)
        c_s[...] = jnp.zeros_like(c_s)
        dg_s[...] = jnp.zeros_like(dg_s)
        tri = _tri(False)
        tri_strict = _tri(True)

        def decay(rows):
            la = la_s[rows, :]
            lend = jnp.sum(la, axis=0, keepdims=True)
            return jnp.exp(lend - _exact_dot(tri, la)), jnp.exp(lend)

        def fwd(n, st):
            rows = pl.ds(pl.multiple_of(n * CHUNK, CHUNK), CHUNK)
            dec, a = decay(rows)
            kdec = (zg_ref[rows, GDK:2 * GDK].astype(F32) * dec).astype(_CDT)
            stn = a * st + _dot_tn(zg_ref[rows, 2 * GDK:2 * GDK + GDV], kdec)
            sall[n] = stn
            return stn

        lax.fori_loop(0, nc, fwd, jnp.zeros((GDV, GDK), F32))

        def bwd(i, carry):
            n = nc - 1 - i
            rows = pl.ds(pl.multiple_of(n * CHUNK, CHUNK), CHUNK)
            dec, a = decay(rows)
            kf = zg_ref[rows, GDK:2 * GDK].astype(F32)
            kdec = kf * dec
            kdb = kdec.astype(_CDT)
            v = zg_ref[rows, 2 * GDK:2 * GDK + GDV]
            qs = (zg_ref[rows, 0:GDK].astype(F32) * (GDK ** -0.5)).astype(_CDT)
            stn = sall[n]
            stb = stn.astype(_CDT)
            st_prev = sall[jnp.maximum(n - 1, 0)] * jnp.where(n > 0, 1.0, 0.0)
            o = _dot_nt(qs, stb)
            do, dgate, dg = _norm_gate_bwd(o, g_ref[...], zg_ref[rows, 2 * GDK + GDV:GW].astype(F32),
                                           dc_ref[rows, :].astype(F32))
            dg_s[...] += dg
            dob = do.astype(_CDT)
            gt = _dot_tn(dob, qs) + c_s[...]
            gtb = gt.astype(_CDT)
            da = jnp.sum(gt * st_prev, axis=0, keepdims=True)
            dkdec = _dot(v, gtb)
            dla = _exact_dot(tri_strict, dkdec * kdec) + da * a
            dpre_s[rows, :] = dla * om_s[rows, :]
            dz_ref[rows, 0:GDK] = (_dot(dob, stb) * (GDK ** -0.5)).astype(_CDT)
            dz_ref[rows, GDK:2 * GDK] = (dkdec * dec).astype(_CDT)
            dz_ref[rows, 2 * GDK:2 * GDK + GDV] = _dot_nt(kdb, gtb).astype(_CDT)
            dz_ref[rows, 2 * GDK + GDV:GW] = dgate.astype(_CDT)
            c_s[...] = a * gt
            return carry

        lax.fori_loop(0, nc, bwd, 0)
        dpre = dpre_s[...]
        dpb = dpre.astype(_CDT)
        dg_ref[...] = dg_s[...]
        dba_ref[...] = jnp.sum(dpre, axis=0, keepdims=True)
        dwa_ref[...] = _dot_tn(zga_ref[...], dpb)
        part = _dot_nt(dpb, wa_ref[...])

        @pl.when(h == 0)
        def _():
            dga_s[...] = part

        @pl.when(h > 0)
        def _():
            dga_s[...] += part

        @pl.when(h == GH - 1)
        def _():
            dga_ref[...] = dga_s[...].astype(_CDT)

    return pl.pallas_call(
        body, name="gla_bwd", grid=(GH,),
        in_specs=[pl.BlockSpec((s, GW), lambda h: (0, h)), pl.BlockSpec((s, GAP), lambda h: (0, 0)),
                  pl.BlockSpec((GAP, GDK), lambda h: (0, h)), pl.BlockSpec((1, GDK), lambda h: (0, h)),
                  pl.BlockSpec((1, GDV), lambda h: (0, h)), pl.BlockSpec((s, GDV), lambda h: (0, h))],
        out_specs=[pl.BlockSpec((s, GW), lambda h: (0, h)), pl.BlockSpec((s, GAP), lambda h: (0, 0)),
                   pl.BlockSpec((GAP, GDK), lambda h: (0, h)), pl.BlockSpec((1, GDK), lambda h: (0, h)),
                   pl.BlockSpec((1, GDV), lambda h: (0, h))],
        out_shape=[jax.ShapeDtypeStruct((s, ZG), _CDT), jax.ShapeDtypeStruct((s, GAP), _CDT),
                   jax.ShapeDtypeStruct((GAP, GH * GDK), F32), jax.ShapeDtypeStruct((1, GH * GDK), F32),
                   jax.ShapeDtypeStruct((1, DGLA), F32)],
        scratch_shapes=[pltpu.VMEM((s, GDK), F32), pltpu.VMEM((s, GDK), F32), pltpu.VMEM((nc, GDV, GDK), F32),
                        pltpu.VMEM((s, GDK), F32), pltpu.VMEM((GDV, GDK), F32), pltpu.VMEM((s, GAP), F32),
                        pltpu.VMEM((1, GDV), F32)],
        compiler_params=_params(("arbitrary",)),
    )(z, zga, wa, ba, ggla, dcg)


def _dh(dzg, dza, dga, wm, wga, x, dout, gpre, tm=512, tk=1024):
    s = x.shape[0]
    nkg, nk = ZG // tk, ZM // tk

    def body(dzg_ref, dza_ref, dga_ref, wm_ref, wga_ref, x_ref, d_ref, g_ref, dx_ref, dg_ref, acc):
        i, k = pl.program_id(0), pl.program_id(1)

        @pl.when((i == 0) & (k == 0))
        def _():
            dg_ref[...] = jnp.zeros_like(dg_ref)

        @pl.when(k == 0)
        def _():
            acc[...] = _dot_nt(dga_ref[...], wga_ref[...])

        @pl.when(k < nkg)
        def _():
            acc[...] += _dot_nt(dzg_ref[...], wm_ref[...])

        @pl.when(k >= nkg)
        def _():
            acc[...] += _dot_nt(dza_ref[...], wm_ref[...])

        @pl.when(k == nk - 1)
        def _():
            xv = x_ref[...]
            r = _rms_rows(xv)
            xh = xv * r
            dh = acc[...]
            dg_ref[...] += jnp.sum(dh * xh, axis=0, keepdims=True)
            dn = dh * g_ref[...]
            dx_ref[...] = d_ref[...] + r * (dn - xh * jnp.mean(dn * xh, axis=-1, keepdims=True))

    return pl.pallas_call(
        body, name="dh", grid=(s // tm, nk),
        in_specs=[pl.BlockSpec((tm, tk), lambda i, k: (i, jnp.minimum(k, nkg - 1))),
                  pl.BlockSpec((tm, tk), lambda i, k: (i, jnp.maximum(k - nkg, 0))),
                  pl.BlockSpec((tm, GAP), lambda i, k: (i, 0)), pl.BlockSpec((D, tk), lambda i, k: (0, k)),
                  pl.BlockSpec((D, GAP), lambda i, k: (0, 0)), pl.BlockSpec((tm, D), lambda i, k: (i, 0)),
                  pl.BlockSpec((tm, D), lambda i, k: (i, 0)), pl.BlockSpec((1, D), lambda i, k: (0, 0))],
        out_specs=[pl.BlockSpec((tm, D), lambda i, k: (i, 0)), pl.BlockSpec((1, D), lambda i, k: (0, 0))],
        out_shape=[jax.ShapeDtypeStruct((s, D), F32), jax.ShapeDtypeStruct((1, D), F32)],
        scratch_shapes=[pltpu.VMEM((tm, D), F32)],
        compiler_params=_params(("arbitrary", "arbitrary")),
    )(dzg, dza, dga, wm, wga, x, dout, gpre)


def _adam(w, g, m, v, tr, name):
    rws, cols = w.shape

    def body(w_ref, g_ref, m_ref, v_ref, d_ref, mo_ref, vo_ref):
        gv = g_ref[...]
        mn = ADAM_B1 * m_ref[...] + (1.0 - ADAM_B1) * gv
        vn = ADAM_B2 * v_ref[...] + (1.0 - ADAM_B2) * (gv * gv)
        mh = mn / (1.0 - ADAM_B1 ** ADAM_STEP)
        vh = vn / (1.0 - ADAM_B2 ** ADAM_STEP)
        d_ref[...] = -ADAM_LR * (mh / (jnp.sqrt(vh) + ADAM_EPS) + ADAM_WD * w_ref[...])
        mo_ref[...] = mn
        vo_ref[...] = vn

    spec = pl.BlockSpec((tr, cols), lambda i: (i, 0))
    return pl.pallas_call(
        body, name=name, grid=(rws // tr,), in_specs=[spec] * 4, out_specs=[spec] * 3,
        out_shape=[jax.ShapeDtypeStruct((rws, cols), F32)] * 3,
        compiler_params=_params(("parallel",)),
    )(w, g, m, v)


def _place():
    x, y, c = lax.axis_index("x"), lax.axis_index("y"), lax.axis_index("c")
    chips = [(1 - x, y), (x, 1 - y), (1 - x, 1 - y)]
    return x, y, c, chips


def _gather_weights(win, wout, wa):
    nl = win.shape[0]

    def body(win_ref, wout_ref, wa_ref, win_all, wout_all, wa_all, send_sems, recv_sems, wa_send, wa_recv, loc_sems):
        x, y, c, chips = _place()
        me = 2 * x + y
        sib = (x, y, 1 - c)

        def in_rows(chip):
            return win_all.at[chip, :, c]

        def out_rows(chip):
            return wout_all.at[:, chip, c]

        def cp(src, dst, a, k, to):
            return pltpu.make_async_remote_copy(src_ref=src, dst_ref=dst, send_sem=send_sems.at[a, k],
                                                recv_sem=recv_sems.at[a, k], device_id=to, device_id_type=MESH)

        def wa_cp(k, chip, to):
            return pltpu.make_async_remote_copy(src_ref=wa_ref, dst_ref=wa_all.at[chip], send_sem=wa_send.at[k],
                                                recv_sem=wa_recv.at[k], device_id=to, device_id_type=MESH)

        local = [pltpu.make_async_copy(win_ref, win_all.at[me], loc_sems.at[0]),
                 pltpu.make_async_copy(wout_ref, wout_all.at[:, me], loc_sems.at[1]),
                 pltpu.make_async_copy(wa_ref, wa_all.at[me], loc_sems.at[2])]
        for cpy in local:
            cpy.start()
        first = []
        for k, (px, py) in enumerate(chips):
            to = (px, py, c)
            first.append(cp(win_ref.at[:, c], in_rows(me), 0, k, to))
            first.append(cp(wout_ref.at[:, c], out_rows(me), 1, k, to))
            first.append(wa_cp(k, me, to))
        for cpy in first:
            cpy.start()
        passed = []
        for k, (px, py) in enumerate(chips):
            chip = 2 * px + py
            cp(in_rows(chip), in_rows(chip), 0, k, sib).wait_recv()
            fwd = cp(in_rows(chip), in_rows(chip), 0, 3 + k, sib)
            fwd.start()
            passed.append(fwd)
            cp(out_rows(chip), out_rows(chip), 1, k, sib).wait_recv()
            fwd = cp(out_rows(chip), out_rows(chip), 1, 3 + k, sib)
            fwd.start()
            passed.append(fwd)
            wa_cp(k, chip, sib).wait_recv()
        for k in range(3):
            cp(in_rows(me), in_rows(me), 0, 3 + k, sib).wait_recv()
            cp(out_rows(me), out_rows(me), 1, 3 + k, sib).wait_recv()
        for cpy in first + passed:
            cpy.wait_send()
        for cpy in local:
            cpy.wait()

    return pl.pallas_call(
        body, name="gather_weights", in_specs=[ANY, ANY, ANY], out_specs=[ANY, ANY, ANY],
        out_shape=[jax.ShapeDtypeStruct((NCHIP,) + win.shape, win.dtype),
                   jax.ShapeDtypeStruct((nl, NCHIP) + wout.shape[1:], wout.dtype),
                   jax.ShapeDtypeStruct((NCHIP,) + wa.shape, wa.dtype)],
        scratch_shapes=[pltpu.SemaphoreType.DMA((2, 6)), pltpu.SemaphoreType.DMA((2, 6)),
                        pltpu.SemaphoreType.DMA((3,)), pltpu.SemaphoreType.DMA((3,)), pltpu.SemaphoreType.DMA((3,))],
    )(win, wout, wa)


def _swap_halves(gin2, gout2):
    def body(gin_ref, gout_ref, rin, rout, send_sems, recv_sems):
        x, y, c, _ = _place()
        sib = (x, y, 1 - c)
        cps = [pltpu.make_async_remote_copy(src_ref=src.at[1 - c], dst_ref=dst, send_sem=send_sems.at[a],
                                            recv_sem=recv_sems.at[a], device_id=sib, device_id_type=MESH)
               for a, (src, dst) in enumerate([(gin_ref, rin), (gout_ref, rout)])]
        for cpy in cps:
            cpy.start()
        for cpy in cps:
            cpy.wait()

    return pl.pallas_call(
        body, name="swap_halves", in_specs=[ANY, ANY], out_specs=[ANY, ANY],
        out_shape=[jax.ShapeDtypeStruct(gin2.shape[1:], gin2.dtype), jax.ShapeDtypeStruct(gout2.shape[1:], gout2.dtype)],
        scratch_shapes=[pltpu.SemaphoreType.DMA((2,)), pltpu.SemaphoreType.DMA((2,))],
    )(gin2, gout2)


def _add_halves(c_idx, g2, r, tr, name):
    rows, cols = r.shape

    def body(c_ref, g_ref, r_ref, o_ref):
        o_ref[...] = (g_ref[0].astype(F32) + r_ref[...].astype(F32)).astype(_XDT)

    return pl.pallas_call(
        body, name=name,
        grid_spec=pltpu.PrefetchScalarGridSpec(
            num_scalar_prefetch=1, grid=(rows // tr,),
            in_specs=[pl.BlockSpec((1, tr, cols), lambda i, c_ref: (c_ref[0], i, 0)),
                      pl.BlockSpec((tr, cols), lambda i, c_ref: (i, 0))],
            out_specs=pl.BlockSpec((tr, cols), lambda i, c_ref: (i, 0))),
        out_shape=jax.ShapeDtypeStruct((rows, cols), _XDT),
        compiler_params=_params(("parallel",)),
    )(c_idx, g2, r)


def _send_to_owners(pin, pout):
    nl = pin.shape[0]

    def body(pin_ref, pout_ref, rin, rout, send_sems, recv_sems):
        x, y, c, chips = _place()
        cps = []
        for k, (px, py) in enumerate(chips):
            chip = 2 * px + py
            for a, (src, dst) in enumerate([(pin_ref, rin), (pout_ref, rout)]):
                cps.append(pltpu.make_async_remote_copy(src_ref=src.at[:, chip], dst_ref=dst.at[k], send_sem=send_sems.at[a, k],
                                                        recv_sem=recv_sems.at[a, k], device_id=(px, py, c), device_id_type=MESH))
        for cpy in cps:
            cpy.start()
        for cpy in cps:
            cpy.wait()

    return pl.pallas_call(
        body, name="send_to_owners", in_specs=[ANY, ANY], out_specs=[ANY, ANY],
        out_shape=[jax.ShapeDtypeStruct((3, nl) + pin.shape[2:], pin.dtype),
                   jax.ShapeDtypeStruct((3, nl) + pout.shape[2:], pout.dtype)],
        scratch_shapes=[pltpu.SemaphoreType.DMA((2, 3)), pltpu.SemaphoreType.DMA((2, 3))],
    )(pin, pout)


def _add_chips(chip_idx, p, r, tr, name):
    nl, _, rows, cols = p.shape

    def body(c_ref, p_ref, r_ref, o_ref):
        o_ref[0] = ((p_ref[0, 0].astype(F32) + r_ref[0, 0].astype(F32)) + r_ref[1, 0].astype(F32)) + r_ref[2, 0].astype(F32)

    return pl.pallas_call(
        body, name=name,
        grid_spec=pltpu.PrefetchScalarGridSpec(
            num_scalar_prefetch=1, grid=(nl, rows // tr),
            in_specs=[pl.BlockSpec((1, 1, tr, cols), lambda l, i, c_ref: (l, c_ref[0], i, 0)),
                      pl.BlockSpec((3, 1, tr, cols), lambda l, i, c_ref: (0, l, i, 0))],
            out_specs=pl.BlockSpec((1, tr, cols), lambda l, i, c_ref: (l, i, 0))),
        out_shape=jax.ShapeDtypeStruct((nl, rows, cols), F32),
        compiler_params=_params(("parallel", "parallel")),
    )(chip_idx, p, r)


def _join_halves(hin, hout):
    def body(hin_ref, hout_ref, fin, fout, send_sems, recv_sems, loc_sems):
        x, y, c, _ = _place()
        sib = (x, y, 1 - c)
        pairs = [(hin_ref, fin.at[:, c]), (hout_ref, fout.at[:, c])]
        local = [pltpu.make_async_copy(src, dst, loc_sems.at[a]) for a, (src, dst) in enumerate(pairs)]
        remote = [pltpu.make_async_remote_copy(src_ref=src, dst_ref=dst, send_sem=send_sems.at[a], recv_sem=recv_sems.at[a],
                                               device_id=sib, device_id_type=MESH) for a, (src, dst) in enumerate(pairs)]
        for cpy in local + remote:
            cpy.start()
        other = [(hin_ref, fin.at[:, 1 - c]), (hout_ref, fout.at[:, 1 - c])]
        for a, (src, dst) in enumerate(other):
            pltpu.make_async_remote_copy(src_ref=src, dst_ref=dst, send_sem=send_sems.at[a], recv_sem=recv_sems.at[a],
                                         device_id=sib, device_id_type=MESH).wait_recv()
        for cpy in remote:
            cpy.wait_send()
        for cpy in local:
            cpy.wait()

    return pl.pallas_call(
        body, name="join_halves", in_specs=[ANY, ANY], out_specs=[ANY, ANY],
        out_shape=[jax.ShapeDtypeStruct((t.shape[0], 2) + t.shape[1:], F32) for t in (hin, hout)],
        scratch_shapes=[pltpu.SemaphoreType.DMA((2,)), pltpu.SemaphoreType.DMA((2,)), pltpu.SemaphoreType.DMA((2,))],
    )(hin, hout)


def _allreduce_small(sg):
    rows = sg.shape[0]
    vm = pl.BlockSpec(memory_space=pltpu.VMEM)

    def body(sg_ref, tot_ref, all_ref, send_sems, recv_sems):
        x, y, c, _ = _place()
        me = 4 * x + 2 * y + c
        all_ref[me] = sg_ref[...]
        cps = []
        for mask in range(1, 8):
            to = (1 - x if mask & 4 else x, 1 - y if mask & 2 else y, 1 - c if mask & 1 else c)
            cps.append(pltpu.make_async_remote_copy(src_ref=sg_ref, dst_ref=all_ref.at[me], send_sem=send_sems.at[mask - 1],
                                                    recv_sem=recv_sems.at[mask - 1], device_id=to, device_id_type=MESH))
        for cpy in cps:
            cpy.start()
        for cpy in cps:
            cpy.wait()
        acc = all_ref[0]
        for d in range(1, 8):
            acc = acc + all_ref[d]
        tot_ref[...] = acc

    return pl.pallas_call(
        body, name="allreduce_small", in_specs=[vm], out_specs=[vm, vm],
        out_shape=[jax.ShapeDtypeStruct((rows, 128), F32), jax.ShapeDtypeStruct((8, rows, 128), F32)],
        scratch_shapes=[pltpu.SemaphoreType.DMA((7,)), pltpu.SemaphoreType.DMA((7,))],
        compiler_params=_params(),
    )(sg)[0]


def _to_internal(w):
    lead = w.shape[:-1]
    cuts = [0, 512, 1024, 2048, 3072, 3088, 4112, 5136, 6160, 7184]
    gq, gk, gv, gg, ga, aq, ak, av, ag = [w[..., cuts[i]:cuts[i + 1]] for i in range(9)]
    g = jnp.concatenate([gq.reshape(lead + (GH, GDK)), gk.reshape(lead + (GH, GDK)),
                         gv.reshape(lead + (GH, GDV)), gg.reshape(lead + (GH, GDV))], axis=-1).reshape(lead + (ZG,))
    a = jnp.concatenate([t.reshape(lead + (AH, AHD)) for t in (aq, ak, av, ag)], axis=-1).reshape(lead + (ZA,))
    pad = [(0, 0)] * len(lead) + [(0, GAP - RANK)]
    return jnp.concatenate([g, a], axis=-1), jnp.pad(ga, pad)


def _from_internal(g, a, ga):
    lead = g.shape[:-1]
    g = g.reshape(lead + (GH, GW))
    a = a.reshape(lead + (AH, AW))
    parts = [g[..., 0:GDK], g[..., GDK:2 * GDK], g[..., 2 * GDK:2 * GDK + GDV], g[..., 2 * GDK + GDV:GW]]
    parts = [t.reshape(lead + (-1,)) for t in parts] + [ga[..., 0:RANK]]
    parts += [a[..., i * AHD:(i + 1) * AHD].reshape(lead + (-1,)) for i in range(4)]
    return jnp.concatenate(parts, axis=-1)


def _pack_rows(parts):
    rows = []
    for t in parts:
        flat = t.reshape(-1)
        rows.append(jnp.pad(flat, (0, (-flat.shape[0]) % 128)).reshape(-1, 128))
    buf = jnp.concatenate(rows, axis=0)
    return jnp.pad(buf, ((0, (-buf.shape[0]) % 8), (0, 0)))


def _unpack_rows(buf, shapes):
    out, r = [], 0
    for shp in shapes:
        n = 1
        for d in shp:
            n *= d
        nr = -(-n // 128)
        out.append(buf[r:r + nr].reshape(-1)[:n].reshape(shp))
        r += nr
    return out


def _layer_fwd(x, wm, wga, wout, gpre, gpost, wa, ba, ggla, gatt, rbx):
    z, zga, h = _inproj(x, gpre, wm, wga)
    cg = _gla_fwd(z, zga, wa, ba, ggla)
    ca, oraw, lse = _att_fwd(z, rbx, gatt)
    y, xo = _outproj(cg, ca, wout, x, gpost)
    return xo, (x, z, zga, h, cg, ca, oraw, lse, y)


def _layer_bwd(dout, saved, wm, wga, wout, gpre, gpost, wa, ba, ggla, gatt, rbx):
    x, z, zga, h, cg, ca, oraw, lse, y = saved
    dy, dcg, dca, dgpost = _post_bwd(dout, y, gpost, wout)
    dwout = jnp.concatenate([_matmul_tn(cg, dy, _XDT, 512, 1024, "dwout_gla"),
                             _matmul_tn(ca, dy, _XDT, 512, 1024, "dwout_att")], axis=0)
    dza, dgatt, dbx = _att_bwd(z, oraw, lse, dca, rbx, gatt)
    dzg, dga, dwa, dba, dggla = _gla_bwd(z, zga, wa, ba, ggla, dcg)
    dx, dgpre = _dh(dzg, dza, dga, wm, wga, x, dout, gpre)
    dwin = _from_internal(_matmul_tn(h, dzg, _XDT, 512, 1024, "dwin_gla"), _matmul_tn(h, dza, _XDT, 512, 1024, "dwin_att"),
                          _matmul_tn(h, dga, _XDT, 512, GAP, "dwin_gate"))
    drb = jnp.concatenate([jnp.zeros((AH, 1), F32), dbx[:, 0, ::-1]], axis=1)
    return dx, dwin, dwout, (dgpre[0], dgpost[0], dwa[0:RANK], dba[0], dggla[0], dgatt[0], drb)


def _rel_rows(rb):
    return rb[:, :0:-1][:, None, :]


def kernel(x, w_in, w_out, g_pre, g_post, w_alpha, b_alpha, g_gla, g_att, rel_bias, loss_target, m_w_in, m_w_out, m_g_pre, m_g_post, m_w_alpha, m_b_alpha, m_g_gla, m_g_att, m_rel_bias, v_w_in, v_w_out, v_g_pre, v_g_post, v_w_alpha, v_b_alpha, v_g_gla, v_g_att, v_rel_bias):
    nl = w_in.shape[0]
    ax, ay, ac = lax.axis_index("x"), lax.axis_index("y"), lax.axis_index("c")
    chip = 2 * ax + ay
    c_idx = jnp.reshape(ac, (1,)).astype(jnp.int32)
    chip_idx = jnp.reshape(chip, (1,)).astype(jnp.int32)

    win_all, wout_all, wa_all = _gather_weights(w_in.astype(_CDT).reshape(nl, 2, D // 2, SHARD),
                                                w_out.astype(_CDT).reshape(nl, 2, D // NCHIP // 2, D), w_alpha)
    wout_all = wout_all.reshape(nl, D, D)
    wfull = jnp.transpose(win_all.reshape(NCHIP, nl, D, SHARD), (1, 2, 0, 3)).reshape(nl, D, DIN)
    wm, wga = _to_internal(wfull)
    wa_full = jnp.transpose(wa_all, (1, 2, 0, 3)).reshape(nl, RANK, GH * GDK)
    wa_pad = jnp.pad(wa_full, ((0, 0), (0, GAP - RANK), (0, 0))).astype(_CDT)
    rbx = [_rel_rows(rel_bias[l]) for l in range(nl)]

    def weights(l):
        return (wm[l], wga[l], wout_all[l], g_pre[l][None], g_post[l][None], wa_pad[l], b_alpha[l][None],
                g_gla[l][None], g_att[l][None], rbx[l])

    h = x[0]
    saved = []
    for l in range(nl):
        h, sv = _layer_fwd(h, *weights(l))
        saved.append(sv)
    dout, loss_part = _loss_grad(h, loss_target[0])
    loss = lax.psum(loss_part[0, 0], ("x", "y", "c"))

    dwin, dwout, small = [None] * nl, [None] * nl, [None] * nl
    for l in reversed(range(nl)):
        dout, dwin[l], dwout[l], small[l] = _layer_bwd(dout, saved[l], *weights(l))
    grad_x = dout[None]

    gin = jnp.stack(dwin).reshape(nl, 2, D // 2, NCHIP, SHARD)
    gin2 = jnp.transpose(gin, (1, 0, 3, 2, 4)).reshape(2, nl * NCHIP * (D // 2), SHARD)
    gout = jnp.stack(dwout).reshape(nl, NCHIP, 2, D // NCHIP // 2, D)
    gout2 = jnp.transpose(gout, (2, 0, 1, 3, 4)).reshape(2, nl * NCHIP * (D // NCHIP // 2), D)
    rin, rout = _swap_halves(gin2, gout2)
    pin = _add_halves(c_idx, gin2, rin, 512, "add_halves_in").reshape(nl, NCHIP, D // 2, SHARD)
    pout = _add_halves(c_idx, gout2, rout, 256, "add_halves_out").reshape(nl, NCHIP, D // NCHIP // 2, D)
    bin_, bout = _send_to_owners(pin, pout)
    hin = _add_chips(chip_idx, pin, bin_, 256, "add_chips_in")
    hout = _add_chips(chip_idx, pout, bout, 256, "add_chips_out")
    g_w_in, g_w_out = [t.reshape(w.shape) for t, w in zip(_join_halves(hin, hout), (w_in, w_out))]

    def adam_big(w, g, m, v, name):
        shp = w.shape
        flat = lambda t: t.reshape(-1, shp[-1])
        return [t.reshape(shp) for t in _adam(flat(w), flat(g), flat(m), flat(v), 256, name)]

    d_w_in, nm_w_in, nv_w_in = adam_big(w_in, g_w_in, m_w_in, v_w_in, "adam_w_in")
    d_w_out, nm_w_out, nv_w_out = adam_big(w_out, g_w_out, m_w_out, v_w_out, "adam_w_out")

    stacked = [jnp.stack([small[l][i] for l in range(nl)]) for i in range(7)]
    g_small = _unpack_rows(_allreduce_small(_pack_rows(stacked)), [t.shape for t in stacked])
    g_gpre, g_gpost, g_wa_full, g_ba, g_ggla, g_gatt, g_rb = g_small
    g_wa = lax.dynamic_slice_in_dim(g_wa_full, chip * GDK, GDK, axis=2)
    names = [(g_pre, m_g_pre, v_g_pre, g_gpre), (g_post, m_g_post, v_g_post, g_gpost), (w_alpha, m_w_alpha, v_w_alpha, g_wa),
             (b_alpha, m_b_alpha, v_b_alpha, g_ba), (g_gla, m_g_gla, v_g_gla, g_ggla), (g_att, m_g_att, v_g_att, g_gatt),
             (rel_bias, m_rel_bias, v_rel_bias, g_rb)]
    shapes = [t[0].shape for t in names]
    packed = [_pack_rows([t[i] for t in names]) for i in range(4)]
    d_s, nm_s, nv_s = [_unpack_rows(t, shapes) for t in _adam(packed[0], packed[3], packed[1], packed[2], packed[0].shape[0], "adam_small")]

    grads = [g_w_in, g_w_out, g_gpre, g_gpost, g_wa, g_ba, g_ggla, g_gatt, g_rb]
    deltas = [d_w_in, d_w_out] + d_s
    new_m = [nm_w_in, nm_w_out] + nm_s
    new_v = [nv_w_in, nv_w_out] + nv_s
    return (loss, grad_x, *grads, *deltas, *new_m, *new_v)
```

```python
import functools

import jax
import jax.numpy as jnp
from jax import lax
from jax.experimental import pallas as pl
from jax.experimental.pallas import tpu as pltpu

D = 2048
DEPTH = 4
CHUNK = 64
GH, GDK, GDV = 4, 128, 256
DGLA = GH * GDV
RANK = 16
TAU = 16.0
AH, AHD = 8, 128
DATT = AH * AHD
LEFT = 8
NREL = 257
EPS = 1e-6
DIN = 7184
ADAM_LR, ADAM_B1, ADAM_B2, ADAM_EPS, ADAM_WD, ADAM_STEP = 0.001, 0.9, 0.999, 1e-08, 0.01, 10

GW = 2 * GDK + 2 * GDV
AW = 4 * AHD
ZG = GH * GW
ZA = AH * AW
ZM = ZG + ZA
GAP = 128
QB = 2 * CHUNK
BANDW = (LEFT + 2) * CHUNK
PADK = LEFT * CHUNK
NCHIP = 4
SHARD = DIN // NCHIP
NEG = -1e30
F32 = jnp.float32
_CDT = jnp.bfloat16
_XDT = jnp.bfloat16
_VMEM = 56 * 1024 * 1024
MESH = pl.DeviceIdType.MESH
ANY = pl.BlockSpec(memory_space=pl.ANY)


def _dot(a, b):
    return jnp.dot(a, b, preferred_element_type=F32)


def _dot_nt(a, b):
    return lax.dot_general(a, b, (((1,), (1,)), ((), ())), preferred_element_type=F32)


def _dot_tn(a, b):
    return lax.dot_general(a, b, (((0,), (0,)), ((), ())), preferred_element_type=F32)


def _rms_rows(v):
    return lax.rsqrt(jnp.mean(v * v, axis=-1, keepdims=True) + EPS)


def _sigmoid(v):
    return 1.0 / (1.0 + jnp.exp(-v))


def _log_sigmoid(v):
    return jnp.minimum(v, 0.0) - jnp.log(1.0 + jnp.exp(-jnp.abs(v)))


def _exact_dot(tri, v):
    hi = v.astype(_CDT)
    r1 = v - hi.astype(F32)
    mid = r1.astype(_CDT)
    lo = (r1 - mid.astype(F32)).astype(_CDT)
    return _dot(tri, hi) + _dot(tri, mid) + _dot(tri, lo)


def _tri(strict):
    row = lax.broadcasted_iota(jnp.int32, (CHUNK, CHUNK), 0)
    col = lax.broadcasted_iota(jnp.int32, (CHUNK, CHUNK), 1)
    return jnp.where((col < row) if strict else (col <= row), 1.0, 0.0).astype(_CDT)


def _norm_gate_bwd(o, g, gate, dcat):
    r = _rms_rows(o)
    oh = o * r
    sg = _sigmoid(gate)
    dn = dcat * (gate * sg)
    dgate = dcat * (oh * g) * (sg * (1.0 + gate * (1.0 - sg)))
    dg = jnp.sum(dn * oh, axis=0, keepdims=True)
    dnn = dn * g
    do = r * (dnn - oh * jnp.mean(dnn * oh, axis=-1, keepdims=True))
    return do, dgate, dg


def _params(sem=None, vmem=_VMEM):
    return pltpu.CompilerParams(dimension_semantics=sem, vmem_limit_bytes=vmem)


def _inproj(x, g, wm, wga, tm=512, tn=1024):
    s = x.shape[0]

    def body(x_ref, g_ref, wm_ref, wga_ref, z_ref, zga_ref, h_ref, hs):
        @pl.when(pl.program_id(1) == 0)
        def _():
            xv = x_ref[...]
            hv = (xv * _rms_rows(xv) * g_ref[...]).astype(_CDT)
            hs[...] = hv
            h_ref[...] = hv
            zga_ref[...] = _dot(hv, wga_ref[...]).astype(_CDT)

        z_ref[...] = _dot(hs[...], wm_ref[...]).astype(_CDT)

    return pl.pallas_call(
        body, name="inproj", grid=(s // tm, ZM // tn),
        in_specs=[pl.BlockSpec((tm, D), lambda i, j: (i, 0)), pl.BlockSpec((1, D), lambda i, j: (0, 0)),
                  pl.BlockSpec((D, tn), lambda i, j: (0, j)), pl.BlockSpec((D, GAP), lambda i, j: (0, 0))],
        out_specs=[pl.BlockSpec((tm, tn), lambda i, j: (i, j)), pl.BlockSpec((tm, GAP), lambda i, j: (i, 0)),
                   pl.BlockSpec((tm, D), lambda i, j: (i, 0))],
        out_shape=[jax.ShapeDtypeStruct((s, ZM), _CDT), jax.ShapeDtypeStruct((s, GAP), _CDT),
                   jax.ShapeDtypeStruct((s, D), _CDT)],
        scratch_shapes=[pltpu.VMEM((tm, D), _CDT)],
        compiler_params=_params(("parallel", "arbitrary")),
    )(x, g, wm, wga)


def _gla_fwd(z, zga, wa, ba, ggla):
    s = z.shape[0]
    nc = s // CHUNK

    def body(zg_ref, zga_ref, wa_ref, ba_ref, g_ref, cat_ref, la_s, st):
        la_s[...] = _log_sigmoid(_dot(zga_ref[...], wa_ref[...]) + ba_ref[...]) * (1.0 / TAU)
        st[...] = jnp.zeros_like(st)
        tri = _tri(False)

        def step(n, carry):
            rows = pl.ds(pl.multiple_of(n * CHUNK, CHUNK), CHUNK)
            la = la_s[rows, :]
            lc = _exact_dot(tri, la)
            lend = jnp.sum(la, axis=0, keepdims=True)
            kdec = (zg_ref[rows, GDK:2 * GDK].astype(F32) * jnp.exp(lend - lc)).astype(_CDT)
            stn = jnp.exp(lend) * st[...] + _dot_tn(zg_ref[rows, 2 * GDK:2 * GDK + GDV], kdec)
            st[...] = stn
            qs = (zg_ref[rows, 0:GDK].astype(F32) * (GDK ** -0.5)).astype(_CDT)
            o = _dot_nt(qs, stn.astype(_CDT))
            gate = zg_ref[rows, 2 * GDK + GDV:GW].astype(F32)
            cat_ref[rows, :] = (o * _rms_rows(o) * g_ref[...] * (gate * _sigmoid(gate))).astype(_CDT)
            return carry

        lax.fori_loop(0, nc, step, 0)

    return pl.pallas_call(
        body, name="gla_fwd", grid=(GH,),
        in_specs=[pl.BlockSpec((s, GW), lambda h: (0, h)), pl.BlockSpec((s, GAP), lambda h: (0, 0)),
                  pl.BlockSpec((GAP, GDK), lambda h: (0, h)), pl.BlockSpec((1, GDK), lambda h: (0, h)),
                  pl.BlockSpec((1, GDV), lambda h: (0, h))],
        out_specs=pl.BlockSpec((s, GDV), lambda h: (0, h)),
        out_shape=jax.ShapeDtypeStruct((s, DGLA), _CDT),
        scratch_shapes=[pltpu.VMEM((s, GDK), F32), pltpu.VMEM((GDV, GDK), F32)],
        compiler_params=_params(("arbitrary",)),
    )(z, zga, wa, ba, ggla)


def _band_bias(b0):
    row = lax.broadcasted_iota(jnp.int32, (QB, 256), 0)
    col = lax.broadcasted_iota(jnp.int32, (QB, 256), 1)
    lane = lax.broadcasted_iota(jnp.int32, (1, 256), 1)
    c0 = jnp.sum(jnp.where(lane == 0, b0, 0.0), axis=1, keepdims=True)
    xv = jnp.broadcast_to(b0, (QB, 256))
    for bit in range(7):
        xv = jnp.where(((row >> bit) & 1) == 1, pltpu.roll(xv, 1 << bit, 1), xv)
    xv = jnp.where(col < row, c0, xv)
    return jnp.concatenate([jnp.broadcast_to(c0, (QB, BANDW - 256)), xv], axis=1)


def _band_static_mask():
    row = lax.broadcasted_iota(jnp.int32, (QB, BANDW), 0) >> 6
    col = lax.broadcasted_iota(jnp.int32, (QB, BANDW), 1) >> 6
    return (col >= row) & (col <= row + LEFT)


def _fold_bias_grad(t):
    row = lax.broadcasted_iota(jnp.int32, (QB, 256), 0)
    col = lax.broadcasted_iota(jnp.int32, (QB, 256), 1)
    xv = t[:, BANDW - 256:]
    low = col < row
    far = jnp.sum(t[:, 0:BANDW - 256], axis=1, keepdims=True) + jnp.sum(jnp.where(low, xv, 0.0), axis=1, keepdims=True)
    far = jnp.sum(far, axis=0, keepdims=True)
    xv = jnp.where(low, 0.0, xv)
    for bit in range(7):
        xv = jnp.where(((row >> bit) & 1) == 1, pltpu.roll(xv, 256 - (1 << bit), 1), xv)
    dp = jnp.sum(xv, axis=0, keepdims=True)
    lane = lax.broadcasted_iota(jnp.int32, (1, 256), 1)
    return dp + jnp.where(lane == 0, far, 0.0)


def _att_fwd(z, rbx, gatt):
    s = z.shape[0]
    nb = s // QB

    def body(za_ref, rb_ref, g_ref, cat_ref, o_ref, lse_ref, kp, vp, bias_s):
        kp[0:PADK, :] = jnp.zeros((PADK, AHD), _CDT)
        vp[0:PADK, :] = jnp.zeros((PADK, AHD), _CDT)
        kp[PADK:, :] = za_ref[:, AHD:2 * AHD]
        vp[PADK:, :] = za_ref[:, 2 * AHD:3 * AHD]
        bias_s[...] = jnp.where(_band_static_mask(), _band_bias(rb_ref[0]), NEG)

        def step(b, carry):
            r0 = pl.multiple_of(b * QB, QB)
            rows = pl.ds(r0, QB)
            band = pl.ds(r0, BANDW)
            col = lax.broadcasted_iota(jnp.int32, (QB, BANDW), 1)
            sc = _dot_nt(za_ref[rows, 0:AHD], kp[band, :]) * (AHD ** -0.5) + bias_s[...]
            sc = jnp.where(col >= PADK - r0, sc, NEG)
            m = jnp.max(sc, axis=-1, keepdims=True)
            p = jnp.exp(sc - m)
            l = jnp.sum(p, axis=-1, keepdims=True)
            o = _dot((p * (1.0 / l)).astype(_CDT), vp[band, :])
            o_ref[rows, :] = o.astype(_CDT)
            lse_ref[rows, :] = jnp.broadcast_to(m + jnp.log(l), (QB, AHD))
            gate = za_ref[rows, 3 * AHD:AW].astype(F32)
            cat_ref[rows, :] = (o * _rms_rows(o) * g_ref[...] * (gate * _sigmoid(gate))).astype(_CDT)
            return carry

        lax.fori_loop(0, nb, step, 0)

    return pl.pallas_call(
        body, name="att_fwd", grid=(AH,),
        in_specs=[pl.BlockSpec((s, AW), lambda h: (0, ZG // AW + h)), pl.BlockSpec((1, 1, 256), lambda h: (h, 0, 0)),
                  pl.BlockSpec((1, AHD), lambda h: (0, h))],
        out_specs=[pl.BlockSpec((s, AHD), lambda h: (0, h)), pl.BlockSpec((s, AHD), lambda h: (0, h)),
                   pl.BlockSpec((s, AHD), lambda h: (0, h))],
        out_shape=[jax.ShapeDtypeStruct((s, DATT), _CDT), jax.ShapeDtypeStruct((s, DATT), _CDT),
                   jax.ShapeDtypeStruct((s, DATT), F32)],
        scratch_shapes=[pltpu.VMEM((s + PADK, AHD), _CDT), pltpu.VMEM((s + PADK, AHD), _CDT),
                        pltpu.VMEM((QB, BANDW), F32)],
        compiler_params=_params(("arbitrary",)),
    )(z, rbx, gatt)


def _outproj(cg, ca, wout, x, gpost, tm=256):
    s = x.shape[0]

    def body(cg_ref, ca_ref, w_ref, x_ref, g_ref, y_ref, xo_ref):
        y = _dot(cg_ref[...], w_ref[0:DGLA, :]) + _dot(ca_ref[...], w_ref[DGLA:, :])
        y_ref[...] = y
        xo_ref[...] = x_ref[...] + y * _rms_rows(y) * g_ref[...]

    return pl.pallas_call(
        body, name="outproj", grid=(s // tm,),
        in_specs=[pl.BlockSpec((tm, DGLA), lambda i: (i, 0)), pl.BlockSpec((tm, DATT), lambda i: (i, 0)),
                  pl.BlockSpec((D, D), lambda i: (0, 0)), pl.BlockSpec((tm, D), lambda i: (i, 0)),
                  pl.BlockSpec((1, D), lambda i: (0, 0))],
        out_specs=[pl.BlockSpec((tm, D), lambda i: (i, 0)), pl.BlockSpec((tm, D), lambda i: (i, 0))],
        out_shape=[jax.ShapeDtypeStruct((s, D), F32), jax.ShapeDtypeStruct((s, D), F32)],
        compiler_params=_params(("parallel",)),
    )(cg, ca, wout, x, gpost)


def _loss_grad(xo, tgt, tm=256):
    s = xo.shape[0]

    def body(xo_ref, t_ref, d_ref, l_ref):
        @pl.when(pl.program_id(0) == 0)
        def _():
            l_ref[...] = jnp.zeros_like(l_ref)

        e = xo_ref[...] - t_ref[...]
        d_ref[...] = e * (1.0 / D)
        l_ref[...] += jnp.sum(jnp.sum(e * e, axis=1, keepdims=True), axis=0, keepdims=True) * (0.5 / D)

    return pl.pallas_call(
        body, name="loss_grad", grid=(s // tm,),
        in_specs=[pl.BlockSpec((tm, D), lambda i: (i, 0)), pl.BlockSpec((tm, D), lambda i: (i, 0))],
        out_specs=[pl.BlockSpec((tm, D), lambda i: (i, 0)), pl.BlockSpec((1, 1), lambda i: (0, 0))],
        out_shape=[jax.ShapeDtypeStruct((s, D), F32), jax.ShapeDtypeStruct((1, 1), F32)],
        compiler_params=_params(("arbitrary",)),
    )(xo, tgt)


def _post_bwd(dout, y, gpost, wout, tm=256):
    s = y.shape[0]

    def body(d_ref, y_ref, g_ref, w_ref, dy_ref, dcg_ref, dca_ref, dg_ref):
        @pl.when(pl.program_id(0) == 0)
        def _():
            dg_ref[...] = jnp.zeros_like(dg_ref)

        yv = y_ref[...]
        r = _rms_rows(yv)
        yh = yv * r
        dv = d_ref[...]
        dg_ref[...] += jnp.sum(dv * yh, axis=0, keepdims=True)
        dn = dv * g_ref[...]
        dyb = (r * (dn - yh * jnp.mean(dn * yh, axis=-1, keepdims=True))).astype(_CDT)
        dy_ref[...] = dyb
        dcg_ref[...] = _dot_nt(dyb, w_ref[0:DGLA, :]).astype(_CDT)
        dca_ref[...] = _dot_nt(dyb, w_ref[DGLA:, :]).astype(_CDT)

    return pl.pallas_call(
        body, name="post_bwd", grid=(s // tm,),
        in_specs=[pl.BlockSpec((tm, D), lambda i: (i, 0)), pl.BlockSpec((tm, D), lambda i: (i, 0)),
                  pl.BlockSpec((1, D), lambda i: (0, 0)), pl.BlockSpec((D, D), lambda i: (0, 0))],
        out_specs=[pl.BlockSpec((tm, D), lambda i: (i, 0)), pl.BlockSpec((tm, DGLA), lambda i: (i, 0)),
                   pl.BlockSpec((tm, DATT), lambda i: (i, 0)), pl.BlockSpec((1, D), lambda i: (0, 0))],
        out_shape=[jax.ShapeDtypeStruct((s, D), _CDT), jax.ShapeDtypeStruct((s, DGLA), _CDT),
                   jax.ShapeDtypeStruct((s, DATT), _CDT), jax.ShapeDtypeStruct((1, D), F32)],
        compiler_params=_params(("arbitrary",)),
    )(dout, y, gpost, wout)


def _matmul_tn(a, b, out_dtype, tm, tn, name):
    k, m = a.shape
    n = b.shape[1]

    def body(a_ref, b_ref, o_ref):
        o_ref[...] = _dot_tn(a_ref[...], b_ref[...]).astype(out_dtype)

    return pl.pallas_call(
        body, name=name, grid=(m // tm, n // tn),
        in_specs=[pl.BlockSpec((k, tm), lambda i, j: (0, i)), pl.BlockSpec((k, tn), lambda i, j: (0, j))],
        out_specs=pl.BlockSpec((tm, tn), lambda i, j: (i, j)),
        out_shape=jax.ShapeDtypeStruct((m, n), out_dtype),
        compiler_params=_params(("parallel", "parallel")),
    )(a, b)


def _att_bwd(z, oraw, lse, dca, rbx, gatt):
    s = z.shape[0]
    nb = s // QB

    def body(za_ref, o_ref, lse_ref, dc_ref, rb_ref, g_ref, dz_ref, dg_ref, db_ref, kp, vp, dkp, dvp, bias_s, t_s, dg_s):
        kp[0:PADK, :] = jnp.zeros((PADK, AHD), _CDT)
        vp[0:PADK, :] = jnp.zeros((PADK, AHD), _CDT)
        kp[PADK:, :] = za_ref[:, AHD:2 * AHD]
        vp[PADK:, :] = za_ref[:, 2 * AHD:3 * AHD]
        dkp[...] = jnp.zeros_like(dkp)
        dvp[...] = jnp.zeros_like(dvp)
        t_s[...] = jnp.zeros_like(t_s)
        dg_s[...] = jnp.zeros_like(dg_s)
        bias_s[...] = jnp.where(_band_static_mask(), _band_bias(rb_ref[0]), NEG)

        def step(b, carry):
            r0 = pl.multiple_of(b * QB, QB)
            rows = pl.ds(r0, QB)
            band = pl.ds(r0, BANDW)
            col = lax.broadcasted_iota(jnp.int32, (QB, BANDW), 1)
            o = o_ref[rows, :].astype(F32)
            do, dgate, dg = _norm_gate_bwd(o, g_ref[...], za_ref[rows, 3 * AHD:AW].astype(F32),
                                           dc_ref[rows, :].astype(F32))
            dg_s[...] += dg
            q = za_ref[rows, 0:AHD]
            kb = kp[band, :]
            sc = _dot_nt(q, kb) * (AHD ** -0.5) + bias_s[...]
            sc = jnp.where(col >= PADK - r0, sc, NEG)
            p = jnp.exp(sc - jnp.max(lse_ref[rows, :], axis=-1, keepdims=True))
            dob = do.astype(_CDT)
            dp = _dot_nt(dob, vp[band, :])
            ds = p * (dp - jnp.sum(do * o, axis=-1, keepdims=True))
            t_s[...] += ds
            dsb = (ds * (AHD ** -0.5)).astype(_CDT)
            dz_ref[rows, 0:AHD] = _dot(dsb, kb).astype(_CDT)
            dz_ref[rows, 3 * AHD:AW] = dgate.astype(_CDT)
            dkp[band, :] += _dot_tn(dsb, q)
            dvp[band, :] += _dot_tn(p.astype(_CDT), dob)
            return carry

        lax.fori_loop(0, nb, step, 0)
        dz_ref[:, AHD:2 * AHD] = dkp[PADK:, :].astype(_CDT)
        dz_ref[:, 2 * AHD:3 * AHD] = dvp[PADK:, :].astype(_CDT)
        dg_ref[...] = dg_s[...]
        db_ref[0] = _fold_bias_grad(t_s[...])

    return pl.pallas_call(
        body, name="att_bwd", grid=(AH,),
        in_specs=[pl.BlockSpec((s, AW), lambda h: (0, ZG // AW + h)), pl.BlockSpec((s, AHD), lambda h: (0, h)),
                  pl.BlockSpec((s, AHD), lambda h: (0, h)), pl.BlockSpec((s, AHD), lambda h: (0, h)),
                  pl.BlockSpec((1, 1, 256), lambda h: (h, 0, 0)), pl.BlockSpec((1, AHD), lambda h: (0, h))],
        out_specs=[pl.BlockSpec((s, AW), lambda h: (0, h)), pl.BlockSpec((1, AHD), lambda h: (0, h)),
                   pl.BlockSpec((1, 1, 256), lambda h: (h, 0, 0))],
        out_shape=[jax.ShapeDtypeStruct((s, ZA), _CDT), jax.ShapeDtypeStruct((1, DATT), F32),
                   jax.ShapeDtypeStruct((AH, 1, 256), F32)],
        scratch_shapes=[pltpu.VMEM((s + PADK, AHD), _CDT), pltpu.VMEM((s + PADK, AHD), _CDT),
                        pltpu.VMEM((s + PADK, AHD), F32), pltpu.VMEM((s + PADK, AHD), F32),
                        pltpu.VMEM((QB, BANDW), F32), pltpu.VMEM((QB, BANDW), F32), pltpu.VMEM((1, AHD), F32)],
        compiler_params=_params(("arbitrary",)),
    )(z, oraw, lse, dca, rbx, gatt)


def _gla_bwd(z, zga, wa, ba, ggla, dcg):
    s = z.shape[0]
    nc = s // CHUNK

    def body(zg_ref, zga_ref, wa_ref, ba_ref, g_ref, dc_ref, dz_ref, dga_ref, dwa_ref, dba_ref, dg_ref,
             la_s, om_s, sall, dpre_s, c_s, dga_s, dg_s):
        h = pl.program_id(0)
        pre = _dot(zga_ref[...], wa_ref[...]) + ba_ref[...]
        la_s[...] = _log_sigmoid(pre) * (1.0 / TAU)
        om_s[...] = (1.0 - _sigmoid(pre)) * (1.0 / TAU)
        c_s[...] = jnp.zeros_like(c_s)
        dg_s[...] = jnp.zeros_like(dg_s)
        tri = _tri(False)
        tri_strict = _tri(True)

        def decay(rows):
            la = la_s[rows, :]
            lend = jnp.sum(la, axis=0, keepdims=True)
            return jnp.exp(lend - _exact_dot(tri, la)), jnp.exp(lend)

        def fwd(n, st):
            rows = pl.ds(pl.multiple_of(n * CHUNK, CHUNK), CHUNK)
            dec, a = decay(rows)
            kdec = (zg_ref[rows, GDK:2 * GDK].astype(F32) * dec).astype(_CDT)
            stn = a * st + _dot_tn(zg_ref[rows, 2 * GDK:2 * GDK + GDV], kdec)
            sall[n] = stn
            return stn

        lax.fori_loop(0, nc, fwd, jnp.zeros((GDV, GDK), F32))

        def bwd(i, carry):
            n = nc - 1 - i
            rows = pl.ds(pl.multiple_of(n * CHUNK, CHUNK), CHUNK)
            dec, a = decay(rows)
            kf = zg_ref[rows, GDK:2 * GDK].astype(F32)
            kdec = kf * dec
            kdb = kdec.astype(_CDT)
            v = zg_ref[rows, 2 * GDK:2 * GDK + GDV]
            qs = (zg_ref[rows, 0:GDK].astype(F32) * (GDK ** -0.5)).astype(_CDT)
            stn = sall[n]
            stb = stn.astype(_CDT)
            st_prev = sall[jnp.maximum(n - 1, 0)] * jnp.where(n > 0, 1.0, 0.0)
            o = _dot_nt(qs, stb)
            do, dgate, dg = _norm_gate_bwd(o, g_ref[...], zg_ref[rows, 2 * GDK + GDV:GW].astype(F32),
                                           dc_ref[rows, :].astype(F32))
            dg_s[...] += dg
            dob = do.astype(_CDT)
            gt = _dot_tn(dob, qs) + c_s[...]
            gtb = gt.astype(_CDT)
            da = jnp.sum(gt * st_prev, axis=0, keepdims=True)
            dkdec = _dot(v, gtb)
            dla = _exact_dot(tri_strict, dkdec * kdec) + da * a
            dpre_s[rows, :] = dla * om_s[rows, :]
            dz_ref[rows, 0:GDK] = (_dot(dob, stb) * (GDK ** -0.5)).astype(_CDT)
            dz_ref[rows, GDK:2 * GDK] = (dkdec * dec).astype(_CDT)
            dz_ref[rows, 2 * GDK:2 * GDK + GDV] = _dot_nt(kdb, gtb).astype(_CDT)
            dz_ref[rows, 2 * GDK + GDV:GW] = dgate.astype(_CDT)
            c_s[...] = a * gt
            return carry

        lax.fori_loop(0, nc, bwd, 0)
        dpre = dpre_s[...]
        dpb = dpre.astype(_CDT)
        dg_ref[...] = dg_s[...]
        dba_ref[...] = jnp.sum(dpre, axis=0, keepdims=True)
        dwa_ref[...] = _dot_tn(zga_ref[...], dpb)
        part = _dot_nt(dpb, wa_ref[...])

        @pl.when(h == 0)
        def _():
            dga_s[...] = part

        @pl.when(h > 0)
        def _():
            dga_s[...] += part

        @pl.when(h == GH - 1)
        def _():
            dga_ref[...] = dga_s[...].astype(_CDT)

    return pl.pallas_call(
        body, name="gla_bwd", grid=(GH,),
        in_specs=[pl.BlockSpec((s, GW), lambda h: (0, h)), pl.BlockSpec((s, GAP), lambda h: (0, 0)),
                  pl.BlockSpec((GAP, GDK), lambda h: (0, h)), pl.BlockSpec((1, GDK), lambda h: (0, h)),
                  pl.BlockSpec((1, GDV), lambda h: (0, h)), pl.BlockSpec((s, GDV), lambda h: (0, h))],
        out_specs=[pl.BlockSpec((s, GW), lambda h: (0, h)), pl.BlockSpec((s, GAP), lambda h: (0, 0)),
                   pl.BlockSpec((GAP, GDK), lambda h: (0, h)), pl.BlockSpec((1, GDK), lambda h: (0, h)),
                   pl.BlockSpec((1, GDV), lambda h: (0, h))],
        out_shape=[jax.ShapeDtypeStruct((s, ZG), _CDT), jax.ShapeDtypeStruct((s, GAP), _CDT),
                   jax.ShapeDtypeStruct((GAP, GH * GDK), F32), jax.ShapeDtypeStruct((1, GH * GDK), F32),
                   jax.ShapeDtypeStruct((1, DGLA), F32)],
        scratch_shapes=[pltpu.VMEM((s, GDK), F32), pltpu.VMEM((s, GDK), F32), pltpu.VMEM((nc, GDV, GDK), F32),
                        pltpu.VMEM((s, GDK), F32), pltpu.VMEM((GDV, GDK), F32), pltpu.VMEM((s, GAP), F32),
                        pltpu.VMEM((1, GDV), F32)],
        compiler_params=_params(("arbitrary",)),
    )(z, zga, wa, ba, ggla, dcg)


def _dh(dzg, dza, dga, wm, wga, x, dout, gpre, tm=512, tk=1024):
    s = x.shape[0]
    nkg, nk = ZG // tk, ZM // tk

    def body(dzg_ref, dza_ref, dga_ref, wm_ref, wga_ref, x_ref, d_ref, g_ref, dx_ref, dg_ref, acc):
        i, k = pl.program_id(0), pl.program_id(1)

        @pl.when((i == 0) & (k == 0))
        def _():
            dg_ref[...] = jnp.zeros_like(dg_ref)

        @pl.when(k == 0)
        def _():
            acc[...] = _dot_nt(dga_ref[...], wga_ref[...])

        @pl.when(k < nkg)
        def _():
            acc[...] += _dot_nt(dzg_ref[...], wm_ref[...])

        @pl.when(k >= nkg)
        def _():
            acc[...] += _dot_nt(dza_ref[...], wm_ref[...])

        @pl.when(k == nk - 1)
        def _():
            xv = x_ref[...]
            r = _rms_rows(xv)
            xh = xv * r
            dh = acc[...]
            dg_ref[...] += jnp.sum(dh * xh, axis=0, keepdims=True)
            dn = dh * g_ref[...]
            dx_ref[...] = d_ref[...] + r * (dn - xh * jnp.mean(dn * xh, axis=-1, keepdims=True))

    return pl.pallas_call(
        body, name="dh", grid=(s // tm, nk),
        in_specs=[pl.BlockSpec((tm, tk), lambda i, k: (i, jnp.minimum(k, nkg - 1))),
                  pl.BlockSpec((tm, tk), lambda i, k: (i, jnp.maximum(k - nkg, 0))),
                  pl.BlockSpec((tm, GAP), lambda i, k: (i, 0)), pl.BlockSpec((D, tk), lambda i, k: (0, k)),
                  pl.BlockSpec((D, GAP), lambda i, k: (0, 0)), pl.BlockSpec((tm, D), lambda i, k: (i, 0)),
                  pl.BlockSpec((tm, D), lambda i, k: (i, 0)), pl.BlockSpec((1, D), lambda i, k: (0, 0))],
        out_specs=[pl.BlockSpec((tm, D), lambda i, k: (i, 0)), pl.BlockSpec((1, D), lambda i, k: (0, 0))],
        out_shape=[jax.ShapeDtypeStruct((s, D), F32), jax.ShapeDtypeStruct((1, D), F32)],
        scratch_shapes=[pltpu.VMEM((tm, D), F32)],
        compiler_params=_params(("arbitrary", "arbitrary")),
    )(dzg, dza, dga, wm, wga, x, dout, gpre)


def _adam(w, g, m, v, tr, name):
    rws, cols = w.shape

    def body(w_ref, g_ref, m_ref, v_ref, d_ref, mo_ref, vo_ref):
        gv = g_ref[...]
        mn = ADAM_B1 * m_ref[...] + (1.0 - ADAM_B1) * gv
        vn = ADAM_B2 * v_ref[...] + (1.0 - ADAM_B2) * (gv * gv)
        mh = mn / (1.0 - ADAM_B1 ** ADAM_STEP)
        vh = vn / (1.0 - ADAM_B2 ** ADAM_STEP)
        d_ref[...] = -ADAM_LR * (mh / (jnp.sqrt(vh) + ADAM_EPS) + ADAM_WD * w_ref[...])
        mo_ref[...] = mn
        vo_ref[...] = vn

    spec = pl.BlockSpec((tr, cols), lambda i: (i, 0))
    return pl.pallas_call(
        body, name=name, grid=(rws // tr,), in_specs=[spec] * 4, out_specs=[spec] * 3,
        out_shape=[jax.ShapeDtypeStruct((rws, cols), F32)] * 3,
        compiler_params=_params(("parallel",)),
    )(w, g, m, v)


def _place():
    x, y, c = lax.axis_index("x"), lax.axis_index("y"), lax.axis_index("c")
    chips = [(1 - x, y), (x, 1 - y), (1 - x, 1 - y)]
    return x, y, c, chips


def _gather_weights(win, wout, wa):
    nl = win.shape[0]

    def body(win_ref, wout_ref, wa_ref, win_all, wout_all, wa_all, send_sems, recv_sems, wa_send, wa_recv, loc_sems):
        x, y, c, chips = _place()
        me = 2 * x + y
        sib = (x, y, 1 - c)
        xn, yn, dg = chips
        arrays = [(win_ref, lambda chip, l: win_all.at[chip, l, c]), (wout_ref, lambda chip, l: wout_all.at[l, chip, c])]

        def cid(p):
            return 2 * p[0] + p[1]

        def cp(src, dst, a, l, k, to):
            return pltpu.make_async_remote_copy(src_ref=src, dst_ref=dst, send_sem=send_sems.at[a, l, k],
                                                recv_sem=recv_sems.at[a, l, k], device_id=to, device_id_type=MESH)

        def wa_cp(k, chip, to):
            return pltpu.make_async_remote_copy(src_ref=wa_ref, dst_ref=wa_all.at[chip], send_sem=wa_send.at[k],
                                                recv_sem=wa_recv.at[k], device_id=to, device_id_type=MESH)

        local = [pltpu.make_async_copy(win_ref, win_all.at[me], loc_sems.at[0]),
                 pltpu.make_async_copy(wout_ref, wout_all.at[:, me], loc_sems.at[1]),
                 pltpu.make_async_copy(wa_ref, wa_all.at[me], loc_sems.at[2])]
        for cpy in local:
            cpy.start()
        sent = []

        def go(cpy):
            cpy.start()
            sent.append(cpy)

        for l in range(nl):
            for a, (src, place) in enumerate(arrays):
                go(cp(src.at[l, c], place(me, l), a, l, 0, (xn[0], xn[1], c)))
                go(cp(src.at[l, c], place(me, l), a, l, 1, (yn[0], yn[1], c)))
        for k, p in enumerate(chips):
            go(wa_cp(k, me, (p[0], p[1], c)))
        for l in range(nl):
            for a, (src, place) in enumerate(arrays):
                got = place(cid(xn), l)
                cp(got, got, a, l, 0, sib).wait_recv()
                go(cp(got.at[0], got.at[0], a, l, 2, (yn[0], yn[1], c)))
                go(cp(got, got, a, l, 4, sib))
                got = place(cid(yn), l)
                cp(got, got, a, l, 1, sib).wait_recv()
                go(cp(got.at[1], got.at[1], a, l, 3, (xn[0], xn[1], c)))
                go(cp(got, got, a, l, 5, sib))
        for l in range(nl):
            for a, (src, place) in enumerate(arrays):
                got = place(cid(dg), l)
                cp(got.at[0], got.at[0], a, l, 2, sib).wait_recv()
                cp(got.at[1], got.at[1], a, l, 3, sib).wait_recv()
                go(cp(got, got, a, l, 6, sib))
        for k, p in enumerate(chips):
            wa_cp(k, cid(p), sib).wait_recv()
        for l in range(nl):
            for a, (src, place) in enumerate(arrays):
                for k in (4, 5, 6):
                    cp(place(me, l), place(me, l), a, l, k, sib).wait_recv()
        for cpy in sent:
            cpy.wait_send()
        for cpy in local:
            cpy.wait()

    return pl.pallas_call(
        body, name="gather_weights", in_specs=[ANY, ANY, ANY], out_specs=[ANY, ANY, ANY],
        out_shape=[jax.ShapeDtypeStruct((NCHIP,) + win.shape, win.dtype),
                   jax.ShapeDtypeStruct((nl, NCHIP) + wout.shape[1:], wout.dtype),
                   jax.ShapeDtypeStruct((NCHIP,) + wa.shape, wa.dtype)],
        scratch_shapes=[pltpu.SemaphoreType.DMA((2, nl, 7)), pltpu.SemaphoreType.DMA((2, nl, 7)),
                        pltpu.SemaphoreType.DMA((3,)), pltpu.SemaphoreType.DMA((3,)), pltpu.SemaphoreType.DMA((3,))],
    )(win, wout, wa)


def _swap_halves(gin2, gout2):
    def body(gin_ref, gout_ref, rin, rout, send_sems, recv_sems):
        x, y, c, _ = _place()
        sib = (x, y, 1 - c)
        cps = [pltpu.make_async_remote_copy(src_ref=src.at[1 - c], dst_ref=dst, send_sem=send_sems.at[a],
                                            recv_sem=recv_sems.at[a], device_id=sib, device_id_type=MESH)
               for a, (src, dst) in enumerate([(gin_ref, rin), (gout_ref, rout)])]
        for cpy in cps:
            cpy.start()
        for cpy in cps:
            cpy.wait()

    return pl.pallas_call(
        body, name="swap_halves", in_specs=[ANY, ANY], out_specs=[ANY, ANY],
        out_shape=[jax.ShapeDtypeStruct(gin2.shape[1:], gin2.dtype), jax.ShapeDtypeStruct(gout2.shape[1:], gout2.dtype)],
        scratch_shapes=[pltpu.SemaphoreType.DMA((2,)), pltpu.SemaphoreType.DMA((2,))],
    )(gin2, gout2)


def _add_halves(c_idx, g2, r, tr, name):
    rows, cols = r.shape

    def body(c_ref, g_ref, r_ref, o_ref):
        o_ref[...] = (g_ref[0].astype(F32) + r_ref[...].astype(F32)).astype(_XDT)

    return pl.pallas_call(
        body, name=name,
        grid_spec=pltpu.PrefetchScalarGridSpec(
            num_scalar_prefetch=1, grid=(rows // tr,),
            in_specs=[pl.BlockSpec((1, tr, cols), lambda i, c_ref: (c_ref[0], i, 0)),
                      pl.BlockSpec((tr, cols), lambda i, c_ref: (i, 0))],
            out_specs=pl.BlockSpec((tr, cols), lambda i, c_ref: (i, 0))),
        out_shape=jax.ShapeDtypeStruct((rows, cols), _XDT),
        compiler_params=_params(("parallel",)),
    )(c_idx, g2, r)


def _send_to_owners(pin, pout):
    nl = pin.shape[0]

    def body(pin_ref, pout_ref, rin, rout, send_sems, recv_sems):
        x, y, c, chips = _place()
        cps = []
        for k, (px, py) in enumerate(chips):
            chip = 2 * px + py
            for a, (src, dst) in enumerate([(pin_ref, rin), (pout_ref, rout)]):
                cps.append(pltpu.make_async_remote_copy(src_ref=src.at[:, chip], dst_ref=dst.at[k], send_sem=send_sems.at[a, k],
                                                        recv_sem=recv_sems.at[a, k], device_id=(px, py, c), device_id_type=MESH))
        for cpy in cps:
            cpy.start()
        for cpy in cps:
            cpy.wait()

    return pl.pallas_call(
        body, name="send_to_owners", in_specs=[ANY, ANY], out_specs=[ANY, ANY],
        out_shape=[jax.ShapeDtypeStruct((3, nl) + pin.shape[2:], pin.dtype),
                   jax.ShapeDtypeStruct((3, nl) + pout.shape[2:], pout.dtype)],
        scratch_shapes=[pltpu.SemaphoreType.DMA((2, 3)), pltpu.SemaphoreType.DMA((2, 3))],
    )(pin, pout)


def _add_chips(chip_idx, p, r, tr, name):
    nl, _, rows, cols = p.shape

    def body(c_ref, p_ref, r_ref, o_ref):
        o_ref[0] = ((p_ref[0, 0].astype(F32) + r_ref[0, 0].astype(F32)) + r_ref[1, 0].astype(F32)) + r_ref[2, 0].astype(F32)

    return pl.pallas_call(
        body, name=name,
        grid_spec=pltpu.PrefetchScalarGridSpec(
            num_scalar_prefetch=1, grid=(nl, rows // tr),
            in_specs=[pl.BlockSpec((1, 1, tr, cols), lambda l, i, c_ref: (l, c_ref[0], i, 0)),
                      pl.BlockSpec((3, 1, tr, cols), lambda l, i, c_ref: (0, l, i, 0))],
            out_specs=pl.BlockSpec((1, tr, cols), lambda l, i, c_ref: (l, i, 0))),
        out_shape=jax.ShapeDtypeStruct((nl, rows, cols), F32),
        compiler_params=_params(("parallel", "parallel")),
    )(chip_idx, p, r)


def _exchange_halves(hin, hout):
    def body(hin_ref, hout_ref, oin, oout, send_sems, recv_sems):
        x, y, c, _ = _place()
        cps = [pltpu.make_async_remote_copy(src_ref=src, dst_ref=dst, send_sem=send_sems.at[a], recv_sem=recv_sems.at[a],
                                            device_id=(x, y, 1 - c), device_id_type=MESH)
               for a, (src, dst) in enumerate([(hin_ref, oin), (hout_ref, oout)])]
        for cpy in cps:
            cpy.start()
        for cpy in cps:
            cpy.wait()

    return pl.pallas_call(
        body, name="exchange_halves", in_specs=[ANY, ANY], out_specs=[ANY, ANY],
        out_shape=[jax.ShapeDtypeStruct(hin.shape, hin.dtype), jax.ShapeDtypeStruct(hout.shape, hout.dtype)],
        scratch_shapes=[pltpu.SemaphoreType.DMA((2,)), pltpu.SemaphoreType.DMA((2,))],
    )(hin, hout)


def _adam_halves(c_idx, w, g_own, g_other, m, v, tr, name):
    nl, _, rows, cols = w.shape

    def body(c_ref, w_ref, go_ref, gx_ref, m_ref, v_ref, g_ref, d_ref, mo_ref, vo_ref):
        gv = jnp.where(pl.program_id(1) == c_ref[0], go_ref[0], gx_ref[0])
        mn = ADAM_B1 * m_ref[0, 0] + (1.0 - ADAM_B1) * gv
        vn = ADAM_B2 * v_ref[0, 0] + (1.0 - ADAM_B2) * (gv * gv)
        mh = mn / (1.0 - ADAM_B1 ** ADAM_STEP)
        vh = vn / (1.0 - ADAM_B2 ** ADAM_STEP)
        g_ref[0, 0] = gv
        d_ref[0, 0] = -ADAM_LR * (mh / (jnp.sqrt(vh) + ADAM_EPS) + ADAM_WD * w_ref[0, 0])
        mo_ref[0, 0] = mn
        vo_ref[0, 0] = vn

    full = pl.BlockSpec((1, 1, tr, cols), lambda l, hh, i, c_ref: (l, hh, i, 0))
    own = pl.BlockSpec((1, tr, cols), lambda l, hh, i, c_ref: (l, jnp.where(hh == c_ref[0], i, 0), 0))
    other = pl.BlockSpec((1, tr, cols), lambda l, hh, i, c_ref: (l, jnp.where(hh == c_ref[0], 0, i), 0))
    return pl.pallas_call(
        body, name=name,
        grid_spec=pltpu.PrefetchScalarGridSpec(
            num_scalar_prefetch=1, grid=(nl, 2, rows // tr),
            in_specs=[full, own, other, full, full], out_specs=[full] * 4),
        out_shape=[jax.ShapeDtypeStruct(w.shape, F32)] * 4,
        compiler_params=_params(("parallel", "parallel", "parallel")),
    )(c_idx, w, g_own, g_other, m, v)


def _allreduce_small(sg):
    rows = sg.shape[0]
    vm = pl.BlockSpec(memory_space=pltpu.VMEM)

    def body(sg_ref, tot_ref, all_ref, send_sems, recv_sems):
        x, y, c, _ = _place()
        me = 4 * x + 2 * y + c
        all_ref[me] = sg_ref[...]
        cps = []
        for mask in range(1, 8):
            to = (1 - x if mask & 4 else x, 1 - y if mask & 2 else y, 1 - c if mask & 1 else c)
            cps.append(pltpu.make_async_remote_copy(src_ref=sg_ref, dst_ref=all_ref.at[me], send_sem=send_sems.at[mask - 1],
                                                    recv_sem=recv_sems.at[mask - 1], device_id=to, device_id_type=MESH))
        for cpy in cps:
            cpy.start()
        for cpy in cps:
            cpy.wait()
        acc = all_ref[0]
        for d in range(1, 8):
            acc = acc + all_ref[d]
        tot_ref[...] = acc

    return pl.pallas_call(
        body, name="allreduce_small", in_specs=[vm], out_specs=[vm, vm],
        out_shape=[jax.ShapeDtypeStruct((rows, 128), F32), jax.ShapeDtypeStruct((8, rows, 128), F32)],
        scratch_shapes=[pltpu.SemaphoreType.DMA((7,)), pltpu.SemaphoreType.DMA((7,))],
        compiler_params=_params(),
    )(sg)[0]


def _to_internal(w):
    lead = w.shape[:-1]
    cuts = [0, 512, 1024, 2048, 3072, 3088, 4112, 5136, 6160, 7184]
    gq, gk, gv, gg, ga, aq, ak, av, ag = [w[..., cuts[i]:cuts[i + 1]] for i in range(9)]
    g = jnp.concatenate([gq.reshape(lead + (GH, GDK)), gk.reshape(lead + (GH, GDK)),
                         gv.reshape(lead + (GH, GDV)), gg.reshape(lead + (GH, GDV))], axis=-1).reshape(lead + (ZG,))
    a = jnp.concatenate([t.reshape(lead + (AH, AHD)) for t in (aq, ak, av, ag)], axis=-1).reshape(lead + (ZA,))
    pad = [(0, 0)] * len(lead) + [(0, GAP - RANK)]
    return jnp.concatenate([g, a], axis=-1), jnp.pad(ga, pad)


def _from_internal(g, a, ga):
    lead = g.shape[:-1]
    g = g.reshape(lead + (GH, GW))
    a = a.reshape(lead + (AH, AW))
    parts = [g[..., 0:GDK], g[..., GDK:2 * GDK], g[..., 2 * GDK:2 * GDK + GDV], g[..., 2 * GDK + GDV:GW]]
    parts = [t.reshape(lead + (-1,)) for t in parts] + [ga[..., 0:RANK]]
    parts += [a[..., i * AHD:(i + 1) * AHD].reshape(lead + (-1,)) for i in range(4)]
    return jnp.concatenate(parts, axis=-1)


def _pack_rows(parts):
    rows = []
    for t in parts:
        flat = t.reshape(-1)
        rows.append(jnp.pad(flat, (0, (-flat.shape[0]) % 128)).reshape(-1, 128))
    buf = jnp.concatenate(rows, axis=0)
    return jnp.pad(buf, ((0, (-buf.shape[0]) % 8), (0, 0)))


def _unpack_rows(buf, shapes):
    out, r = [], 0
    for shp in shapes:
        n = 1
        for d in shp:
            n *= d
        nr = -(-n // 128)
        out.append(buf[r:r + nr].reshape(-1)[:n].reshape(shp))
        r += nr
    return out


def _layer_fwd(x, wm, wga, wout, gpre, gpost, wa, ba, ggla, gatt, rbx):
    z, zga, h = _inproj(x, gpre, wm, wga)
    cg = _gla_fwd(z, zga, wa, ba, ggla)
    ca, oraw, lse = _att_fwd(z, rbx, gatt)
    y, xo = _outproj(cg, ca, wout, x, gpost)
    return xo, (x, z, zga, h, cg, ca, oraw, lse, y)


def _layer_bwd(dout, saved, wm, wga, wout, gpre, gpost, wa, ba, ggla, gatt, rbx):
    x, z, zga, h, cg, ca, oraw, lse, y = saved
    dy, dcg, dca, dgpost = _post_bwd(dout, y, gpost, wout)
    dwout = jnp.concatenate([_matmul_tn(cg, dy, _XDT, 512, 1024, "dwout_gla"),
                             _matmul_tn(ca, dy, _XDT, 512, 1024, "dwout_att")], axis=0)
    dza, dgatt, dbx = _att_bwd(z, oraw, lse, dca, rbx, gatt)
    dzg, dga, dwa, dba, dggla = _gla_bwd(z, zga, wa, ba, ggla, dcg)
    dx, dgpre = _dh(dzg, dza, dga, wm, wga, x, dout, gpre)
    dwin = _from_internal(_matmul_tn(h, dzg, _XDT, 512, 1024, "dwin_gla"), _matmul_tn(h, dza, _XDT, 512, 1024, "dwin_att"),
                          _matmul_tn(h, dga, _XDT, 512, GAP, "dwin_gate"))
    drb = jnp.concatenate([jnp.zeros((AH, 1), F32), dbx[:, 0, ::-1]], axis=1)
    return dx, dwin, dwout, (dgpre[0], dgpost[0], dwa[0:RANK], dba[0], dggla[0], dgatt[0], drb)


def _rel_rows(rb):
    return rb[:, :0:-1][:, None, :]


def kernel(x, w_in, w_out, g_pre, g_post, w_alpha, b_alpha, g_gla, g_att, rel_bias, loss_target, m_w_in, m_w_out, m_g_pre, m_g_post, m_w_alpha, m_b_alpha, m_g_gla, m_g_att, m_rel_bias, v_w_in, v_w_out, v_g_pre, v_g_post, v_w_alpha, v_b_alpha, v_g_gla, v_g_att, v_rel_bias):
    nl = w_in.shape[0]
    ax, ay, ac = lax.axis_index("x"), lax.axis_index("y"), lax.axis_index("c")
    chip = 2 * ax + ay
    c_idx = jnp.reshape(ac, (1,)).astype(jnp.int32)
    chip_idx = jnp.reshape(chip, (1,)).astype(jnp.int32)

    win_all, wout_all, wa_all = _gather_weights(w_in.astype(_CDT).reshape(nl, 2, 2, D // 4, SHARD),
                                                w_out.astype(_CDT).reshape(nl, 2, 2, D // NCHIP // 4, D), w_alpha)
    wout_all = wout_all.reshape(nl, D, D)
    wfull = jnp.transpose(win_all.reshape(NCHIP, nl, D, SHARD), (1, 2, 0, 3)).reshape(nl, D, DIN)
    wm, wga = _to_internal(wfull)
    wa_full = jnp.transpose(wa_all, (1, 2, 0, 3)).reshape(nl, RANK, GH * GDK)
    wa_pad = jnp.pad(wa_full, ((0, 0), (0, GAP - RANK), (0, 0))).astype(_CDT)
    rbx = [_rel_rows(rel_bias[l]) for l in range(nl)]

    def weights(l):
        return (wm[l], wga[l], wout_all[l], g_pre[l][None], g_post[l][None], wa_pad[l], b_alpha[l][None],
                g_gla[l][None], g_att[l][None], rbx[l])

    h = x[0]
    saved = []
    for l in range(nl):
        h, sv = _layer_fwd(h, *weights(l))
        saved.append(sv)
    dout, loss_part = _loss_grad(h, loss_target[0])
    loss = lax.psum(loss_part[0, 0], ("x", "y", "c"))

    dwin, dwout, small = [None] * nl, [None] * nl, [None] * nl
    for l in reversed(range(nl)):
        dout, dwin[l], dwout[l], small[l] = _layer_bwd(dout, saved[l], *weights(l))
    grad_x = dout[None]

    gin = jnp.stack(dwin).reshape(nl, 2, D // 2, NCHIP, SHARD)
    gin2 = jnp.transpose(gin, (1, 0, 3, 2, 4)).reshape(2, nl * NCHIP * (D // 2), SHARD)
    gout = jnp.stack(dwout).reshape(nl, NCHIP, 2, D // NCHIP // 2, D)
    gout2 = jnp.transpose(gout, (2, 0, 1, 3, 4)).reshape(2, nl * NCHIP * (D // NCHIP // 2), D)
    rin, rout = _swap_halves(gin2, gout2)
    pin = _add_halves(c_idx, gin2, rin, 512, "add_halves_in").reshape(nl, NCHIP, D // 2, SHARD)
    pout = _add_halves(c_idx, gout2, rout, 256, "add_halves_out").reshape(nl, NCHIP, D // NCHIP // 2, D)
    bin_, bout = _send_to_owners(pin, pout)
    hin = _add_chips(chip_idx, pin, bin_, 256, "add_chips_in")
    hout = _add_chips(chip_idx, pout, bout, 256, "add_chips_out")
    xin, xout = _exchange_halves(hin, hout)

    def adam_big(w, g_own, g_other, m, v, name):
        shp = w.shape
        halves = lambda t: t.reshape(shp[0], 2, shp[1] // 2, shp[2])
        return [t.reshape(shp) for t in _adam_halves(c_idx, halves(w), g_own, g_other, halves(m), halves(v), 256, name)]

    g_w_in, d_w_in, nm_w_in, nv_w_in = adam_big(w_in, hin, xin, m_w_in, v_w_in, "adam_w_in")
    g_w_out, d_w_out, nm_w_out, nv_w_out = adam_big(w_out, hout, xout, m_w_out, v_w_out, "adam_w_out")

    stacked = [jnp.stack([small[l][i] for l in range(nl)]) for i in range(7)]
    g_small = _unpack_rows(_allreduce_small(_pack_rows(stacked)), [t.shape for t in stacked])
    g_gpre, g_gpost, g_wa_full, g_ba, g_ggla, g_gatt, g_rb = g_small
    g_wa = lax.dynamic_slice_in_dim(g_wa_full, chip * GDK, GDK, axis=2)
    names = [(g_pre, m_g_pre, v_g_pre, g_gpre), (g_post, m_g_post, v_g_post, g_gpost), (w_alpha, m_w_alpha, v_w_alpha, g_wa),
             (b_alpha, m_b_alpha, v_b_alpha, g_ba), (g_gla, m_g_gla, v_g_gla, g_ggla), (g_att, m_g_att, v_g_att, g_gatt),
             (rel_bias, m_rel_bias, v_rel_bias, g_rb)]
    shapes = [t[0].shape for t in names]
    packed = [_pack_rows([t[i] for t in names]) for i in range(4)]
    d_s, nm_s, nv_s = [_unpack_rows(t, shapes) for t in _adam(packed[0], packed[3], packed[1], packed[2], packed[0].shape[0], "adam_small")]

    grads = [g_w_in, g_w_out, g_gpre, g_gpost, g_wa, g_ba, g_ggla, g_gatt, g_rb]
    deltas = [d_w_in, d_w_out] + d_s
    new_m = [nm_w_in, nm_w_out] + nm_s
    new_v = [nv_w_in, nv_w_out] + nv_s
    return (loss, grad_x, *grads, *deltas, *new_m, *new_v)
```

```python
import functools

import jax
import jax.numpy as jnp
from jax import lax
from jax.experimental import pallas as pl
from jax.experimental.pallas import tpu as pltpu

D = 2048
DEPTH = 4
CHUNK = 64
GH, GDK, GDV = 4, 128, 256
DGLA = GH * GDV
RANK = 16
TAU = 16.0
AH, AHD = 8, 128
DATT = AH * AHD
LEFT = 8
NREL = 257
EPS = 1e-6
DIN = 7184
ADAM_LR, ADAM_B1, ADAM_B2, ADAM_EPS, ADAM_WD, ADAM_STEP = 0.001, 0.9, 0.999, 1e-08, 0.01, 10

GW = 2 * GDK + 2 * GDV
AW = 4 * AHD
ZG = GH * GW
ZA = AH * AW
ZM = ZG + ZA
GAP = 128
QB = 2 * CHUNK
BANDW = (LEFT + 2) * CHUNK
PADK = LEFT * CHUNK
NCHIP = 4
SHARD = DIN // NCHIP
SLAB = 1824
HSLAB = SLAB // 2
NEG = -1e30
F32 = jnp.float32
_CDT = jnp.bfloat16
_XDT = jnp.bfloat16
_VMEM = 56 * 1024 * 1024
MESH = pl.DeviceIdType.MESH
ANY = pl.BlockSpec(memory_space=pl.ANY)


def _dot(a, b):
    return jnp.dot(a, b, preferred_element_type=F32)


def _dot_nt(a, b):
    return lax.dot_general(a, b, (((1,), (1,)), ((), ())), preferred_element_type=F32)


def _dot_tn(a, b):
    return lax.dot_general(a, b, (((0,), (0,)), ((), ())), preferred_element_type=F32)


def _rms_rows(v):
    return lax.rsqrt(jnp.mean(v * v, axis=-1, keepdims=True) + EPS)


def _sigmoid(v):
    return 1.0 / (1.0 + jnp.exp(-v))


def _log_sigmoid(v):
    return jnp.minimum(v, 0.0) - jnp.log(1.0 + jnp.exp(-jnp.abs(v)))


def _exact_dot(tri, v):
    hi = v.astype(_CDT)
    r1 = v - hi.astype(F32)
    mid = r1.astype(_CDT)
    lo = (r1 - mid.astype(F32)).astype(_CDT)
    return _dot(tri, hi) + _dot(tri, mid) + _dot(tri, lo)


def _tri(strict):
    row = lax.broadcasted_iota(jnp.int32, (CHUNK, CHUNK), 0)
    col = lax.broadcasted_iota(jnp.int32, (CHUNK, CHUNK), 1)
    return jnp.where((col < row) if strict else (col <= row), 1.0, 0.0).astype(_CDT)


def _norm_gate_bwd(o, g, gate, dcat):
    r = _rms_rows(o)
    oh = o * r
    sg = _sigmoid(gate)
    dn = dcat * (gate * sg)
    dgate = dcat * (oh * g) * (sg * (1.0 + gate * (1.0 - sg)))
    dg = jnp.sum(dn * oh, axis=0, keepdims=True)
    dnn = dn * g
    do = r * (dnn - oh * jnp.mean(dnn * oh, axis=-1, keepdims=True))
    return do, dgate, dg


def _params(sem=None, vmem=_VMEM):
    return pltpu.CompilerParams(dimension_semantics=sem, vmem_limit_bytes=vmem)


def _inproj(x, g, wm, wga, tm=512, tn=1024):
    s = x.shape[0]

    def body(x_ref, g_ref, wm_ref, wga_ref, z_ref, zga_ref, h_ref, hs):
        @pl.when(pl.program_id(1) == 0)
        def _():
            xv = x_ref[...]
            hv = (xv * _rms_rows(xv) * g_ref[...]).astype(_CDT)
            hs[...] = hv
            h_ref[...] = hv
            zga_ref[...] = _dot_nt(hv, wga_ref[...]).astype(_CDT)

        z_ref[...] = _dot_nt(hs[...], wm_ref[...]).astype(_CDT)

    return pl.pallas_call(
        body, name="inproj", grid=(s // tm, ZM // tn),
        in_specs=[pl.BlockSpec((tm, D), lambda i, j: (i, 0)), pl.BlockSpec((1, D), lambda i, j: (0, 0)),
                  pl.BlockSpec((tn, D), lambda i, j: (j, 0)), pl.BlockSpec((GAP, D), lambda i, j: (0, 0))],
        out_specs=[pl.BlockSpec((tm, tn), lambda i, j: (i, j)), pl.BlockSpec((tm, GAP), lambda i, j: (i, 0)),
                   pl.BlockSpec((tm, D), lambda i, j: (i, 0))],
        out_shape=[jax.ShapeDtypeStruct((s, ZM), _CDT), jax.ShapeDtypeStruct((s, GAP), _CDT),
                   jax.ShapeDtypeStruct((s, D), _CDT)],
        scratch_shapes=[pltpu.VMEM((tm, D), _CDT)],
        compiler_params=_params(("parallel", "arbitrary")),
    )(x, g, wm, wga)


def _gla_fwd(z, zga, wa, ba, ggla):
    s = z.shape[0]
    nc = s // CHUNK

    def body(zg_ref, zga_ref, wa_ref, ba_ref, g_ref, cat_ref, la_s, st):
        la_s[...] = _log_sigmoid(_dot(zga_ref[...], wa_ref[...]) + ba_ref[...]) * (1.0 / TAU)
        st[...] = jnp.zeros_like(st)
        tri = _tri(False)

        def step(n, carry):
            rows = pl.ds(pl.multiple_of(n * CHUNK, CHUNK), CHUNK)
            la = la_s[rows, :]
            lc = _exact_dot(tri, la)
            lend = jnp.sum(la, axis=0, keepdims=True)
            kdec = (zg_ref[rows, GDK:2 * GDK].astype(F32) * jnp.exp(lend - lc)).astype(_CDT)
            stn = jnp.exp(lend) * st[...] + _dot_tn(zg_ref[rows, 2 * GDK:2 * GDK + GDV], kdec)
            st[...] = stn
            qs = (zg_ref[rows, 0:GDK].astype(F32) * (GDK ** -0.5)).astype(_CDT)
            o = _dot_nt(qs, stn.astype(_CDT))
            gate = zg_ref[rows, 2 * GDK + GDV:GW].astype(F32)
            cat_ref[rows, :] = (o * _rms_rows(o) * g_ref[...] * (gate * _sigmoid(gate))).astype(_CDT)
            return carry

        lax.fori_loop(0, nc, step, 0)

    return pl.pallas_call(
        body, name="gla_fwd", grid=(GH,),
        in_specs=[pl.BlockSpec((s, GW), lambda h: (0, h)), pl.BlockSpec((s, GAP), lambda h: (0, 0)),
                  pl.BlockSpec((GAP, GDK), lambda h: (0, h)), pl.BlockSpec((1, GDK), lambda h: (0, h)),
                  pl.BlockSpec((1, GDV), lambda h: (0, h))],
        out_specs=pl.BlockSpec((s, GDV), lambda h: (0, h)),
        out_shape=jax.ShapeDtypeStruct((s, DGLA), _CDT),
        scratch_shapes=[pltpu.VMEM((s, GDK), F32), pltpu.VMEM((GDV, GDK), F32)],
        compiler_params=_params(("arbitrary",)),
    )(z, zga, wa, ba, ggla)


def _band_bias(b0):
    row = lax.broadcasted_iota(jnp.int32, (QB, 256), 0)
    col = lax.broadcasted_iota(jnp.int32, (QB, 256), 1)
    lane = lax.broadcasted_iota(jnp.int32, (1, 256), 1)
    c0 = jnp.sum(jnp.where(lane == 0, b0, 0.0), axis=1, keepdims=True)
    xv = jnp.broadcast_to(b0, (QB, 256))
    for bit in range(7):
        xv = jnp.where(((row >> bit) & 1) == 1, pltpu.roll(xv, 1 << bit, 1), xv)
    xv = jnp.where(col < row, c0, xv)
    return jnp.concatenate([jnp.broadcast_to(c0, (QB, BANDW - 256)), xv], axis=1)


def _band_static_mask():
    row = lax.broadcasted_iota(jnp.int32, (QB, BANDW), 0) >> 6
    col = lax.broadcasted_iota(jnp.int32, (QB, BANDW), 1) >> 6
    return (col >= row) & (col <= row + LEFT)


def _fold_bias_grad(t):
    row = lax.broadcasted_iota(jnp.int32, (QB, 256), 0)
    col = lax.broadcasted_iota(jnp.int32, (QB, 256), 1)
    xv = t[:, BANDW - 256:]
    low = col < row
    far = jnp.sum(t[:, 0:BANDW - 256], axis=1, keepdims=True) + jnp.sum(jnp.where(low, xv, 0.0), axis=1, keepdims=True)
    far = jnp.sum(far, axis=0, keepdims=True)
    xv = jnp.where(low, 0.0, xv)
    for bit in range(7):
        xv = jnp.where(((row >> bit) & 1) == 1, pltpu.roll(xv, 256 - (1 << bit), 1), xv)
    dp = jnp.sum(xv, axis=0, keepdims=True)
    lane = lax.broadcasted_iota(jnp.int32, (1, 256), 1)
    return dp + jnp.where(lane == 0, far, 0.0)


def _att_fwd(z, rbx, gatt):
    s = z.shape[0]
    nb = s // QB

    def body(za_ref, rb_ref, g_ref, cat_ref, o_ref, lse_ref, kp, vp, bias_s):
        kp[0:PADK, :] = jnp.zeros((PADK, AHD), _CDT)
        vp[0:PADK, :] = jnp.zeros((PADK, AHD), _CDT)
        kp[PADK:, :] = za_ref[:, AHD:2 * AHD]
        vp[PADK:, :] = za_ref[:, 2 * AHD:3 * AHD]
        bias_s[...] = jnp.where(_band_static_mask(), _band_bias(rb_ref[0]), NEG)

        def step(b, carry):
            r0 = pl.multiple_of(b * QB, QB)
            rows = pl.ds(r0, QB)
            band = pl.ds(r0, BANDW)
            col = lax.broadcasted_iota(jnp.int32, (QB, BANDW), 1)
            sc = _dot_nt(za_ref[rows, 0:AHD], kp[band, :]) * (AHD ** -0.5) + bias_s[...]
            sc = jnp.where(col >= PADK - r0, sc, NEG)
            m = jnp.max(sc, axis=-1, keepdims=True)
            p = jnp.exp(sc - m)
            l = jnp.sum(p, axis=-1, keepdims=True)
            o = _dot((p * (1.0 / l)).astype(_CDT), vp[band, :])
            o_ref[rows, :] = o.astype(_CDT)
            lse_ref[rows, :] = jnp.broadcast_to(m + jnp.log(l), (QB, AHD))
            gate = za_ref[rows, 3 * AHD:AW].astype(F32)
            cat_ref[rows, :] = (o * _rms_rows(o) * g_ref[...] * (gate * _sigmoid(gate))).astype(_CDT)
            return carry

        lax.fori_loop(0, nb, step, 0)

    return pl.pallas_call(
        body, name="att_fwd", grid=(AH,),
        in_specs=[pl.BlockSpec((s, AW), lambda h: (0, ZG // AW + h)), pl.BlockSpec((1, 1, 256), lambda h: (h, 0, 0)),
                  pl.BlockSpec((1, AHD), lambda h: (0, h))],
        out_specs=[pl.BlockSpec((s, AHD), lambda h: (0, h)), pl.BlockSpec((s, AHD), lambda h: (0, h)),
                   pl.BlockSpec((s, AHD), lambda h: (0, h))],
        out_shape=[jax.ShapeDtypeStruct((s, DATT), _CDT), jax.ShapeDtypeStruct((s, DATT), _CDT),
                   jax.ShapeDtypeStruct((s, DATT), F32)],
        scratch_shapes=[pltpu.VMEM((s + PADK, AHD), _CDT), pltpu.VMEM((s + PADK, AHD), _CDT),
                        pltpu.VMEM((QB, BANDW), F32)],
        compiler_params=_params(("arbitrary",)),
    )(z, rbx, gatt)


def _outproj(cg, ca, wout, x, gpost, tm=256):
    s = x.shape[0]

    def body(cg_ref, ca_ref, w_ref, x_ref, g_ref, y_ref, xo_ref):
        y = _dot(cg_ref[...], w_ref[0:DGLA, :]) + _dot(ca_ref[...], w_ref[DGLA:, :])
        y_ref[...] = y
        xo_ref[...] = x_ref[...] + y * _rms_rows(y) * g_ref[...]

    return pl.pallas_call(
        body, name="outproj", grid=(s // tm,),
        in_specs=[pl.BlockSpec((tm, DGLA), lambda i: (i, 0)), pl.BlockSpec((tm, DATT), lambda i: (i, 0)),
                  pl.BlockSpec((D, D), lambda i: (0, 0)), pl.BlockSpec((tm, D), lambda i: (i, 0)),
                  pl.BlockSpec((1, D), lambda i: (0, 0))],
        out_specs=[pl.BlockSpec((tm, D), lambda i: (i, 0)), pl.BlockSpec((tm, D), lambda i: (i, 0))],
        out_shape=[jax.ShapeDtypeStruct((s, D), F32), jax.ShapeDtypeStruct((s, D), F32)],
        compiler_params=_params(("parallel",)),
    )(cg, ca, wout, x, gpost)


def _loss_grad(xo, tgt, tm=256):
    s = xo.shape[0]

    def body(xo_ref, t_ref, d_ref, l_ref):
        @pl.when(pl.program_id(0) == 0)
        def _():
            l_ref[...] = jnp.zeros_like(l_ref)

        e = xo_ref[...] - t_ref[...]
        d_ref[...] = e * (1.0 / D)
        l_ref[...] += jnp.sum(jnp.sum(e * e, axis=1, keepdims=True), axis=0, keepdims=True) * (0.5 / D)

    return pl.pallas_call(
        body, name="loss_grad", grid=(s // tm,),
        in_specs=[pl.BlockSpec((tm, D), lambda i: (i, 0)), pl.BlockSpec((tm, D), lambda i: (i, 0))],
        out_specs=[pl.BlockSpec((tm, D), lambda i: (i, 0)), pl.BlockSpec((1, 1), lambda i: (0, 0))],
        out_shape=[jax.ShapeDtypeStruct((s, D), F32), jax.ShapeDtypeStruct((1, 1), F32)],
        compiler_params=_params(("arbitrary",)),
    )(xo, tgt)


def _post_bwd(dout, y, gpost, wout, tm=256):
    s = y.shape[0]

    def body(d_ref, y_ref, g_ref, w_ref, dy_ref, dcg_ref, dca_ref, dg_ref):
        @pl.when(pl.program_id(0) == 0)
        def _():
            dg_ref[...] = jnp.zeros_like(dg_ref)

        yv = y_ref[...]
        r = _rms_rows(yv)
        yh = yv * r
        dv = d_ref[...]
        dg_ref[...] += jnp.sum(dv * yh, axis=0, keepdims=True)
        dn = dv * g_ref[...]
        dyb = (r * (dn - yh * jnp.mean(dn * yh, axis=-1, keepdims=True))).astype(_CDT)
        dy_ref[...] = dyb
        dcg_ref[...] = _dot_nt(dyb, w_ref[0:DGLA, :]).astype(_CDT)
        dca_ref[...] = _dot_nt(dyb, w_ref[DGLA:, :]).astype(_CDT)

    return pl.pallas_call(
        body, name="post_bwd", grid=(s // tm,),
        in_specs=[pl.BlockSpec((tm, D), lambda i: (i, 0)), pl.BlockSpec((tm, D), lambda i: (i, 0)),
                  pl.BlockSpec((1, D), lambda i: (0, 0)), pl.BlockSpec((D, D), lambda i: (0, 0))],
        out_specs=[pl.BlockSpec((tm, D), lambda i: (i, 0)), pl.BlockSpec((tm, DGLA), lambda i: (i, 0)),
                   pl.BlockSpec((tm, DATT), lambda i: (i, 0)), pl.BlockSpec((1, D), lambda i: (0, 0))],
        out_shape=[jax.ShapeDtypeStruct((s, D), _CDT), jax.ShapeDtypeStruct((s, DGLA), _CDT),
                   jax.ShapeDtypeStruct((s, DATT), _CDT), jax.ShapeDtypeStruct((1, D), F32)],
        compiler_params=_params(("arbitrary",)),
    )(dout, y, gpost, wout)


def _matmul_tn(a, b, out_dtype, tm, tn, name):
    k, m = a.shape
    n = b.shape[1]

    def body(a_ref, b_ref, o_ref):
        o_ref[...] = _dot_tn(a_ref[...], b_ref[...]).astype(out_dtype)

    return pl.pallas_call(
        body, name=name, grid=(m // tm, n // tn),
        in_specs=[pl.BlockSpec((k, tm), lambda i, j: (0, i)), pl.BlockSpec((k, tn), lambda i, j: (0, j))],
        out_specs=pl.BlockSpec((tm, tn), lambda i, j: (i, j)),
        out_shape=jax.ShapeDtypeStruct((m, n), out_dtype),
        compiler_params=_params(("parallel", "parallel")),
    )(a, b)


def _att_bwd(z, oraw, lse, dca, rbx, gatt):
    s = z.shape[0]
    nb = s // QB

    def body(za_ref, o_ref, lse_ref, dc_ref, rb_ref, g_ref, dz_ref, dg_ref, db_ref, kp, vp, dkp, dvp, bias_s, t_s, dg_s):
        kp[0:PADK, :] = jnp.zeros((PADK, AHD), _CDT)
        vp[0:PADK, :] = jnp.zeros((PADK, AHD), _CDT)
        kp[PADK:, :] = za_ref[:, AHD:2 * AHD]
        vp[PADK:, :] = za_ref[:, 2 * AHD:3 * AHD]
        dkp[...] = jnp.zeros_like(dkp)
        dvp[...] = jnp.zeros_like(dvp)
        t_s[...] = jnp.zeros_like(t_s)
        dg_s[...] = jnp.zeros_like(dg_s)
        bias_s[...] = jnp.where(_band_static_mask(), _band_bias(rb_ref[0]), NEG)

        def step(b, carry):
            r0 = pl.multiple_of(b * QB, QB)
            rows = pl.ds(r0, QB)
            band = pl.ds(r0, BANDW)
            col = lax.broadcasted_iota(jnp.int32, (QB, BANDW), 1)
            o = o_ref[rows, :].astype(F32)
            do, dgate, dg = _norm_gate_bwd(o, g_ref[...], za_ref[rows, 3 * AHD:AW].astype(F32),
                                           dc_ref[rows, :].astype(F32))
            dg_s[...] += dg
            q = za_ref[rows, 0:AHD]
            kb = kp[band, :]
            sc = _dot_nt(q, kb) * (AHD ** -0.5) + bias_s[...]
            sc = jnp.where(col >= PADK - r0, sc, NEG)
            p = jnp.exp(sc - jnp.max(lse_ref[rows, :], axis=-1, keepdims=True))
            dob = do.astype(_CDT)
            dp = _dot_nt(dob, vp[band, :])
            ds = p * (dp - jnp.sum(do * o, axis=-1, keepdims=True))
            t_s[...] += ds
            dsb = (ds * (AHD ** -0.5)).astype(_CDT)
            dz_ref[rows, 0:AHD] = _dot(dsb, kb).astype(_CDT)
            dz_ref[rows, 3 * AHD:AW] = dgate.astype(_CDT)
            dkp[band, :] += _dot_tn(dsb, q)
            dvp[band, :] += _dot_tn(p.astype(_CDT), dob)
            return carry

        lax.fori_loop(0, nb, step, 0)
        dz_ref[:, AHD:2 * AHD] = dkp[PADK:, :].astype(_CDT)
        dz_ref[:, 2 * AHD:3 * AHD] = dvp[PADK:, :].astype(_CDT)
        dg_ref[...] = dg_s[...]
        db_ref[0] = _fold_bias_grad(t_s[...])

    return pl.pallas_call(
        body, name="att_bwd", grid=(AH,),
        in_specs=[pl.BlockSpec((s, AW), lambda h: (0, ZG // AW + h)), pl.BlockSpec((s, AHD), lambda h: (0, h)),
                  pl.BlockSpec((s, AHD), lambda h: (0, h)), pl.BlockSpec((s, AHD), lambda h: (0, h)),
                  pl.BlockSpec((1, 1, 256), lambda h: (h, 0, 0)), pl.BlockSpec((1, AHD), lambda h: (0, h))],
        out_specs=[pl.BlockSpec((s, AW), lambda h: (0, h)), pl.BlockSpec((1, AHD), lambda h: (0, h)),
                   pl.BlockSpec((1, 1, 256), lambda h: (h, 0, 0))],
        out_shape=[jax.ShapeDtypeStruct((s, ZA), _CDT), jax.ShapeDtypeStruct((1, DATT), F32),
                   jax.ShapeDtypeStruct((AH, 1, 256), F32)],
        scratch_shapes=[pltpu.VMEM((s + PADK, AHD), _CDT), pltpu.VMEM((s + PADK, AHD), _CDT),
                        pltpu.VMEM((s + PADK, AHD), F32), pltpu.VMEM((s + PADK, AHD), F32),
                        pltpu.VMEM((QB, BANDW), F32), pltpu.VMEM((QB, BANDW), F32), pltpu.VMEM((1, AHD), F32)],
        compiler_params=_params(("arbitrary",)),
    )(z, oraw, lse, dca, rbx, gatt)


def _gla_bwd(z, zga, wa, ba, ggla, dcg):
    s = z.shape[0]
    nc = s // CHUNK

    def body(zg_ref, zga_ref, wa_ref, ba_ref, g_ref, dc_ref, dz_ref, dga_ref, dwa_ref, dba_ref, dg_ref,
             la_s, om_s, sall, dpre_s, c_s, dga_s, dg_s):
        h = pl.program_id(0)
        pre = _dot(zga_ref[...], wa_ref[...]) + ba_ref[...]
        la_s[...] = _log_sigmoid(pre) * (1.0 / TAU)
        om_s[...] = (1.0 - _sigmoid(pre)) * (1.0 / TAU)
        c_s[...] = jnp.zeros_like(c_s)
        dg_s[...] = jnp.zeros_like(dg_s)
        tri = _tri(False)
        tri_strict = _tri(True)

        def decay(rows):
            la = la_s[rows, :]
            lend = jnp.sum(la, axis=0, keepdims=True)
            return jnp.exp(lend - _exact_dot(tri, la)), jnp.exp(lend)

        def fwd(n, st):
            rows = pl.ds(pl.multiple_of(n * CHUNK, CHUNK), CHUNK)
            dec, a = decay(rows)
            kdec = (zg_ref[rows, GDK:2 * GDK].astype(F32) * dec).astype(_CDT)
            stn = a * st + _dot_tn(zg_ref[rows, 2 * GDK:2 * GDK + GDV], kdec)
            sall[n] = stn
            return stn

        lax.fori_loop(0, nc, fwd, jnp.zeros((GDV, GDK), F32))

        def bwd(i, carry):
            n = nc - 1 - i
            rows = pl.ds(pl.multiple_of(n * CHUNK, CHUNK), CHUNK)
            dec, a = decay(rows)
            kf = zg_ref[rows, GDK:2 * GDK].astype(F32)
            kdec = kf * dec
            kdb = kdec.astype(_CDT)
            v = zg_ref[rows, 2 * GDK:2 * GDK + GDV]
            qs = (zg_ref[rows, 0:GDK].astype(F32) * (GDK ** -0.5)).astype(_CDT)
            stn = sall[n]
            stb = stn.astype(_CDT)
            st_prev = sall[jnp.maximum(n - 1, 0)] * jnp.where(n > 0, 1.0, 0.0)
            o = _dot_nt(qs, stb)
            do, dgate, dg = _norm_gate_bwd(o, g_ref[...], zg_ref[rows, 2 * GDK + GDV:GW].astype(F32),
                                           dc_ref[rows, :].astype(F32))
            dg_s[...] += dg
            dob = do.astype(_CDT)
            gt = _dot_tn(dob, qs) + c_s[...]
            gtb = gt.astype(_CDT)
            da = jnp.sum(gt * st_prev, axis=0, keepdims=True)
            dkdec = _dot(v, gtb)
            dla = _exact_dot(tri_strict, dkdec * kdec) + da * a
            dpre_s[rows, :] = dla * om_s[rows, :]
            dz_ref[rows, 0:GDK] = (_dot(dob, stb) * (GDK ** -0.5)).astype(_CDT)
            dz_ref[rows, GDK:2 * GDK] = (dkdec * dec).astype(_CDT)
            dz_ref[rows, 2 * GDK:2 * GDK + GDV] = _dot_nt(kdb, gtb).astype(_CDT)
            dz_ref[rows, 2 * GDK + GDV:GW] = dgate.astype(_CDT)
            c_s[...] = a * gt
            return carry

        lax.fori_loop(0, nc, bwd, 0)
        dpre = dpre_s[...]
        dpb = dpre.astype(_CDT)
        dg_ref[...] = dg_s[...]
        dba_ref[...] = jnp.sum(dpre, axis=0, keepdims=True)
        dwa_ref[...] = _dot_tn(zga_ref[...], dpb)
        part = _dot_nt(dpb, wa_ref[...])

        @pl.when(h == 0)
        def _():
            dga_s[...] = part

        @pl.when(h > 0)
        def _():
            dga_s[...] += part

        @pl.when(h == GH - 1)
        def _():
            dga_ref[...] = dga_s[...].astype(_CDT)

    return pl.pallas_call(
        body, name="gla_bwd", grid=(GH,),
        in_specs=[pl.BlockSpec((s, GW), lambda h: (0, h)), pl.BlockSpec((s, GAP), lambda h: (0, 0)),
                  pl.BlockSpec((GAP, GDK), lambda h: (0, h)), pl.BlockSpec((1, GDK), lambda h: (0, h)),
                  pl.BlockSpec((1, GDV), lambda h: (0, h)), pl.BlockSpec((s, GDV), lambda h: (0, h))],
        out_specs=[pl.BlockSpec((s, GW), lambda h: (0, h)), pl.BlockSpec((s, GAP), lambda h: (0, 0)),
                   pl.BlockSpec((GAP, GDK), lambda h: (0, h)), pl.BlockSpec((1, GDK), lambda h: (0, h)),
                   pl.BlockSpec((1, GDV), lambda h: (0, h))],
        out_shape=[jax.ShapeDtypeStruct((s, ZG), _CDT), jax.ShapeDtypeStruct((s, GAP), _CDT),
                   jax.ShapeDtypeStruct((GAP, GH * GDK), F32), jax.ShapeDtypeStruct((1, GH * GDK), F32),
                   jax.ShapeDtypeStruct((1, DGLA), F32)],
        scratch_shapes=[pltpu.VMEM((s, GDK), F32), pltpu.VMEM((s, GDK), F32), pltpu.VMEM((nc, GDV, GDK), F32),
                        pltpu.VMEM((s, GDK), F32), pltpu.VMEM((GDV, GDK), F32), pltpu.VMEM((s, GAP), F32),
                        pltpu.VMEM((1, GDV), F32)],
        compiler_params=_params(("arbitrary",)),
    )(z, zga, wa, ba, ggla, dcg)


def _dh(dzg, dza, dga, wm, wga, x, dout, gpre, tm=512, tk=1024):
    s = x.shape[0]
    nkg, nk = ZG // tk, ZM // tk

    def body(dzg_ref, dza_ref, dga_ref, wm_ref, wga_ref, x_ref, d_ref, g_ref, dx_ref, dg_ref, acc):
        i, k = pl.program_id(0), pl.program_id(1)

        @pl.when((i == 0) & (k == 0))
        def _():
            dg_ref[...] = jnp.zeros_like(dg_ref)

        @pl.when(k == 0)
        def _():
            acc[...] = _dot(dga_ref[...], wga_ref[...])

        @pl.when(k < nkg)
        def _():
            acc[...] += _dot(dzg_ref[...], wm_ref[...])

        @pl.when(k >= nkg)
        def _():
            acc[...] += _dot(dza_ref[...], wm_ref[...])

        @pl.when(k == nk - 1)
        def _():
            xv = x_ref[...]
            r = _rms_rows(xv)
            xh = xv * r
            dh = acc[...]
            dg_ref[...] += jnp.sum(dh * xh, axis=0, keepdims=True)
            dn = dh * g_ref[...]
            dx_ref[...] = d_ref[...] + r * (dn - xh * jnp.mean(dn * xh, axis=-1, keepdims=True))

    return pl.pallas_call(
        body, name="dh", grid=(s // tm, nk),
        in_specs=[pl.BlockSpec((tm, tk), lambda i, k: (i, jnp.minimum(k, nkg - 1))),
                  pl.BlockSpec((tm, tk), lambda i, k: (i, jnp.maximum(k - nkg, 0))),
                  pl.BlockSpec((tm, GAP), lambda i, k: (i, 0)), pl.BlockSpec((tk, D), lambda i, k: (k, 0)),
                  pl.BlockSpec((GAP, D), lambda i, k: (0, 0)), pl.BlockSpec((tm, D), lambda i, k: (i, 0)),
                  pl.BlockSpec((tm, D), lambda i, k: (i, 0)), pl.BlockSpec((1, D), lambda i, k: (0, 0))],
        out_specs=[pl.BlockSpec((tm, D), lambda i, k: (i, 0)), pl.BlockSpec((1, D), lambda i, k: (0, 0))],
        out_shape=[jax.ShapeDtypeStruct((s, D), F32), jax.ShapeDtypeStruct((1, D), F32)],
        scratch_shapes=[pltpu.VMEM((tm, D), F32)],
        compiler_params=_params(("arbitrary", "arbitrary")),
    )(dzg, dza, dga, wm, wga, x, dout, gpre)


def _adam(w, g, m, v, tr, name):
    rws, cols = w.shape

    def body(w_ref, g_ref, m_ref, v_ref, d_ref, mo_ref, vo_ref):
        gv = g_ref[...]
        mn = ADAM_B1 * m_ref[...] + (1.0 - ADAM_B1) * gv
        vn = ADAM_B2 * v_ref[...] + (1.0 - ADAM_B2) * (gv * gv)
        mh = mn / (1.0 - ADAM_B1 ** ADAM_STEP)
        vh = vn / (1.0 - ADAM_B2 ** ADAM_STEP)
        d_ref[...] = -ADAM_LR * (mh / (jnp.sqrt(vh) + ADAM_EPS) + ADAM_WD * w_ref[...])
        mo_ref[...] = mn
        vo_ref[...] = vn

    spec = pl.BlockSpec((tr, cols), lambda i: (i, 0))
    return pl.pallas_call(
        body, name=name, grid=(rws // tr,), in_specs=[spec] * 4, out_specs=[spec] * 3,
        out_shape=[jax.ShapeDtypeStruct((rws, cols), F32)] * 3,
        compiler_params=_params(("parallel",)),
    )(w, g, m, v)


def _adam_rows(w, g, m, v, tj, name):
    rows = w.shape[0]

    def body(w_ref, g_ref, m_ref, v_ref, d_ref, mo_ref, vo_ref):
        gv = g_ref[...]
        mn = ADAM_B1 * m_ref[...] + (1.0 - ADAM_B1) * gv
        vn = ADAM_B2 * v_ref[...] + (1.0 - ADAM_B2) * (gv * gv)
        mh = mn / (1.0 - ADAM_B1 ** ADAM_STEP)
        vh = vn / (1.0 - ADAM_B2 ** ADAM_STEP)
        d_ref[...] = -ADAM_LR * (mh / (jnp.sqrt(vh) + ADAM_EPS) + ADAM_WD * w_ref[...])
        mo_ref[...] = mn
        vo_ref[...] = vn

    spec = pl.BlockSpec((tj,) + w.shape[1:], lambda i: (i, 0, 0))
    return pl.pallas_call(
        body, name=name, grid=(pl.cdiv(rows, tj),), in_specs=[spec] * 4, out_specs=[spec] * 3,
        out_shape=[jax.ShapeDtypeStruct(w.shape, F32)] * 3,
        compiler_params=_params(("parallel",)),
    )(w, g, m, v)


def _place():
    x, y, c = lax.axis_index("x"), lax.axis_index("y"), lax.axis_index("c")
    chips = [(1 - x, y), (x, 1 - y), (1 - x, 1 - y)]
    return x, y, c, chips


def _row_chunks(n, parts):
    base, extra = divmod(n, parts)
    out, r = [], 0
    for i in range(parts):
        size = base + (1 if i < extra else 0)
        out.append((r, size))
        r += size
    return out


def _gather_weights(wt, wout, wa):
    nl = wout.shape[0]
    chunks = _row_chunks(wt.shape[2], nl)

    def variant(x, y, c, wt_ref, wout_ref, wa_ref, wt_all, wout_all, wa_all, send_sems, recv_sems, wa_send, wa_recv, loc_sems):
        me = 2 * x + y
        sib = (x, y, 1 - c)
        xn, yn, dg = (1 - x, y), (x, 1 - y), (1 - x, 1 - y)

        def wt_rows(ref, k):
            return ref.at[:, pl.ds(chunks[k][0], chunks[k][1])]

        arrays = [(lambda k: wt_rows(wt_ref.at[c], k), lambda chip, k: wt_rows(wt_all.at[chip, c], k)),
                  (lambda k: wout_ref.at[k, c], lambda chip, k: wout_all.at[k, chip, c])]

        def cid(p):
            return 2 * p[0] + p[1]

        def cp(src, dst, a, l, k, to):
            return pltpu.make_async_remote_copy(src_ref=src, dst_ref=dst, send_sem=send_sems.at[a, l, k],
                                                recv_sem=recv_sems.at[a, l, k], device_id=to, device_id_type=MESH)

        def wa_cp(k, chip, to):
            return pltpu.make_async_remote_copy(src_ref=wa_ref, dst_ref=wa_all.at[chip], send_sem=wa_send.at[k],
                                                recv_sem=wa_recv.at[k], device_id=to, device_id_type=MESH)

        local = [pltpu.make_async_copy(wt_ref, wt_all.at[me], loc_sems.at[0]),
                 pltpu.make_async_copy(wout_ref, wout_all.at[:, me], loc_sems.at[1]),
                 pltpu.make_async_copy(wa_ref, wa_all.at[me], loc_sems.at[2])]
        for cpy in local:
            cpy.start()
        sent = []

        def go(cpy):
            cpy.start()
            sent.append(cpy)

        for l in range(nl):
            for a, (src, place) in enumerate(arrays):
                go(cp(src(l), place(me, l), a, l, 0, (xn[0], xn[1], c)))
                go(cp(src(l), place(me, l), a, l, 1, (yn[0], yn[1], c)))
        for k, p in enumerate((xn, yn, dg)):
            go(wa_cp(k, me, (p[0], p[1], c)))
        for l in range(nl):
            for a, (src, place) in enumerate(arrays):
                got = place(cid(xn), l)
                cp(got, got, a, l, 0, sib).wait_recv()
                go(cp(got.at[0], got.at[0], a, l, 2, (yn[0], yn[1], c)))
                go(cp(got, got, a, l, 4, sib))
                got = place(cid(yn), l)
                cp(got, got, a, l, 1, sib).wait_recv()
                go(cp(got.at[1], got.at[1], a, l, 3, (xn[0], xn[1], c)))
                go(cp(got, got, a, l, 5, sib))
        for l in range(nl):
            for a, (src, place) in enumerate(arrays):
                got = place(cid(dg), l)
                cp(got.at[0], got.at[0], a, l, 2, sib).wait_recv()
                cp(got.at[1], got.at[1], a, l, 3, sib).wait_recv()
                go(cp(got, got, a, l, 6, sib))
        for k, p in enumerate((xn, yn, dg)):
            wa_cp(k, cid(p), sib).wait_recv()
        for l in range(nl):
            for a, (src, place) in enumerate(arrays):
                for k in (4, 5, 6):
                    cp(place(me, l), place(me, l), a, l, k, sib).wait_recv()
        for cpy in sent:
            cpy.wait_send()
        for cpy in local:
            cpy.wait()

    def body(*refs):
        x, y, c, _ = _place()
        for jx in range(2):
            for jy in range(2):
                for jc in range(2):
                    pl.when((x == jx) & (y == jy) & (c == jc))(functools.partial(variant, jx, jy, jc, *refs))

    return pl.pallas_call(
        body, name="gather_weights", in_specs=[ANY, ANY, ANY], out_specs=[ANY, ANY, ANY],
        out_shape=[jax.ShapeDtypeStruct((NCHIP,) + wt.shape, wt.dtype),
                   jax.ShapeDtypeStruct((nl, NCHIP) + wout.shape[1:], wout.dtype),
                   jax.ShapeDtypeStruct((NCHIP,) + wa.shape, wa.dtype)],
        scratch_shapes=[pltpu.SemaphoreType.DMA((2, nl, 7)), pltpu.SemaphoreType.DMA((2, nl, 7)),
                        pltpu.SemaphoreType.DMA((3,)), pltpu.SemaphoreType.DMA((3,)), pltpu.SemaphoreType.DMA((3,))],
    )(wt, wout, wa)


def _swap_halves(gin2, gout2):
    def body(gin_ref, gout_ref, rin, rout, send_sems, recv_sems):
        x, y, c, _ = _place()
        sib = (x, y, 1 - c)
        cps = [pltpu.make_async_remote_copy(src_ref=src.at[1 - c], dst_ref=dst, send_sem=send_sems.at[a],
                                            recv_sem=recv_sems.at[a], device_id=sib, device_id_type=MESH)
               for a, (src, dst) in enumerate([(gin_ref, rin), (gout_ref, rout)])]
        for cpy in cps:
            cpy.start()
        for cpy in cps:
            cpy.wait()

    return pl.pallas_call(
        body, name="swap_halves", in_specs=[ANY, ANY], out_specs=[ANY, ANY],
        out_shape=[jax.ShapeDtypeStruct(gin2.shape[1:], gin2.dtype), jax.ShapeDtypeStruct(gout2.shape[1:], gout2.dtype)],
        scratch_shapes=[pltpu.SemaphoreType.DMA((2,)), pltpu.SemaphoreType.DMA((2,))],
    )(gin2, gout2)


def _add_halves(c_idx, g2, r, tr, name):
    rows, cols = r.shape

    def body(c_ref, g_ref, r_ref, o_ref):
        o_ref[...] = (g_ref[0].astype(F32) + r_ref[...].astype(F32)).astype(_XDT)

    return pl.pallas_call(
        body, name=name,
        grid_spec=pltpu.PrefetchScalarGridSpec(
            num_scalar_prefetch=1, grid=(rows // tr,),
            in_specs=[pl.BlockSpec((1, tr, cols), lambda i, c_ref: (c_ref[0], i, 0)),
                      pl.BlockSpec((tr, cols), lambda i, c_ref: (i, 0))],
            out_specs=pl.BlockSpec((tr, cols), lambda i, c_ref: (i, 0))),
        out_shape=jax.ShapeDtypeStruct((rows, cols), _XDT),
        compiler_params=_params(("parallel",)),
    )(c_idx, g2, r)


def _send_to_owners(pin, pout):
    nl = pin.shape[0]

    def body(pin_ref, pout_ref, rin, rout, send_sems, recv_sems):
        x, y, c, chips = _place()
        cps = []
        for k, (px, py) in enumerate(chips):
            chip = 2 * px + py
            for a, (src, dst) in enumerate([(pin_ref, rin), (pout_ref, rout)]):
                cps.append(pltpu.make_async_remote_copy(src_ref=src.at[:, chip], dst_ref=dst.at[k], send_sem=send_sems.at[a, k],
                                                        recv_sem=recv_sems.at[a, k], device_id=(px, py, c), device_id_type=MESH))
        for cpy in cps:
            cpy.start()
        for cpy in cps:
            cpy.wait()

    return pl.pallas_call(
        body, name="send_to_owners", in_specs=[ANY, ANY], out_specs=[ANY, ANY],
        out_shape=[jax.ShapeDtypeStruct((3, nl) + pin.shape[2:], pin.dtype),
                   jax.ShapeDtypeStruct((3, nl) + pout.shape[2:], pout.dtype)],
        scratch_shapes=[pltpu.SemaphoreType.DMA((2, 3)), pltpu.SemaphoreType.DMA((2, 3))],
    )(pin, pout)


def _add_chips(chip_idx, p, r, tr, name):
    nl, _, rows, cols = p.shape

    def body(c_ref, p_ref, r_ref, o_ref):
        o_ref[0] = ((p_ref[0, 0].astype(F32) + r_ref[0, 0].astype(F32)) + r_ref[1, 0].astype(F32)) + r_ref[2, 0].astype(F32)

    return pl.pallas_call(
        body, name=name,
        grid_spec=pltpu.PrefetchScalarGridSpec(
            num_scalar_prefetch=1, grid=(nl, rows // tr),
            in_specs=[pl.BlockSpec((1, 1, tr, cols), lambda l, i, c_ref: (l, c_ref[0], i, 0)),
                      pl.BlockSpec((3, 1, tr, cols), lambda l, i, c_ref: (0, l, i, 0))],
            out_specs=pl.BlockSpec((1, tr, cols), lambda l, i, c_ref: (l, i, 0))),
        out_shape=jax.ShapeDtypeStruct((nl, rows, cols), F32),
        compiler_params=_params(("parallel", "parallel")),
    )(chip_idx, p, r)


def _exchange_halves(hin, hout):
    def body(hin_ref, hout_ref, oin, oout, send_sems, recv_sems):
        x, y, c, _ = _place()
        cps = [pltpu.make_async_remote_copy(src_ref=src, dst_ref=dst, send_sem=send_sems.at[a], recv_sem=recv_sems.at[a],
                                            device_id=(x, y, 1 - c), device_id_type=MESH)
               for a, (src, dst) in enumerate([(hin_ref, oin), (hout_ref, oout)])]
        for cpy in cps:
            cpy.start()
        for cpy in cps:
            cpy.wait()

    return pl.pallas_call(
        body, name="exchange_halves", in_specs=[ANY, ANY], out_specs=[ANY, ANY],
        out_shape=[jax.ShapeDtypeStruct(hin.shape, hin.dtype), jax.ShapeDtypeStruct(hout.shape, hout.dtype)],
        scratch_shapes=[pltpu.SemaphoreType.DMA((2,)), pltpu.SemaphoreType.DMA((2,))],
    )(hin, hout)


def _adam_halves(c_idx, w, g_own, g_other, m, v, tr, name):
    nl, _, rows, cols = w.shape

    def body(c_ref, w_ref, go_ref, gx_ref, m_ref, v_ref, g_ref, d_ref, mo_ref, vo_ref):
        gv = jnp.where(pl.program_id(1) == c_ref[0], go_ref[0], gx_ref[0])
        mn = ADAM_B1 * m_ref[0, 0] + (1.0 - ADAM_B1) * gv
        vn = ADAM_B2 * v_ref[0, 0] + (1.0 - ADAM_B2) * (gv * gv)
        mh = mn / (1.0 - ADAM_B1 ** ADAM_STEP)
        vh = vn / (1.0 - ADAM_B2 ** ADAM_STEP)
        g_ref[0, 0] = gv
        d_ref[0, 0] = -ADAM_LR * (mh / (jnp.sqrt(vh) + ADAM_EPS) + ADAM_WD * w_ref[0, 0])
        mo_ref[0, 0] = mn
        vo_ref[0, 0] = vn

    full = pl.BlockSpec((1, 1, tr, cols), lambda l, hh, i, c_ref: (l, hh, i, 0))
    own = pl.BlockSpec((1, tr, cols), lambda l, hh, i, c_ref: (l, jnp.where(hh == c_ref[0], i, 0), 0))
    other = pl.BlockSpec((1, tr, cols), lambda l, hh, i, c_ref: (l, jnp.where(hh == c_ref[0], 0, i), 0))
    return pl.pallas_call(
        body, name=name,
        grid_spec=pltpu.PrefetchScalarGridSpec(
            num_scalar_prefetch=1, grid=(nl, 2, rows // tr),
            in_specs=[full, own, other, full, full], out_specs=[full] * 4),
        out_shape=[jax.ShapeDtypeStruct(w.shape, F32)] * 4,
        compiler_params=_params(("parallel", "parallel", "parallel")),
    )(c_idx, w, g_own, g_other, m, v)


def _allreduce_small(sg):
    rows = sg.shape[0]
    vm = pl.BlockSpec(memory_space=pltpu.VMEM)

    def body(sg_ref, tot_ref, all_ref, send_sems, recv_sems):
        x, y, c, _ = _place()
        me = 4 * x + 2 * y + c
        all_ref[me] = sg_ref[...]
        cps = []
        for mask in range(1, 8):
            to = (1 - x if mask & 4 else x, 1 - y if mask & 2 else y, 1 - c if mask & 1 else c)
            cps.append(pltpu.make_async_remote_copy(src_ref=sg_ref, dst_ref=all_ref.at[me], send_sem=send_sems.at[mask - 1],
                                                    recv_sem=recv_sems.at[mask - 1], device_id=to, device_id_type=MESH))
        for cpy in cps:
            cpy.start()
        for cpy in cps:
            cpy.wait()
        acc = all_ref[0]
        for d in range(1, 8):
            acc = acc + all_ref[d]
        tot_ref[...] = acc

    return pl.pallas_call(
        body, name="allreduce_small", in_specs=[vm], out_specs=[vm, vm],
        out_shape=[jax.ShapeDtypeStruct((rows, 128), F32), jax.ShapeDtypeStruct((8, rows, 128), F32)],
        scratch_shapes=[pltpu.SemaphoreType.DMA((7,)), pltpu.SemaphoreType.DMA((7,))],
        compiler_params=_params(),
    )(sg)[0]


_CUTS = [0, 512, 1024, 2048, 3072, 3088, 4112, 5136, 6160, 7184]


def _rows_to_internal(w):
    tail = w.shape[1:]
    gq, gk, gv, gg, ga, aq, ak, av, ag = [w[_CUTS[i]:_CUTS[i + 1]] for i in range(9)]
    g = jnp.concatenate([gq.reshape((GH, GDK) + tail), gk.reshape((GH, GDK) + tail),
                         gv.reshape((GH, GDV) + tail), gg.reshape((GH, GDV) + tail)], axis=1).reshape((ZG,) + tail)
    a = jnp.concatenate([t.reshape((AH, AHD) + tail) for t in (aq, ak, av, ag)], axis=1).reshape((ZA,) + tail)
    pad = [(0, GAP - RANK)] + [(0, 0)] * len(tail)
    return jnp.concatenate([g, a], axis=0), jnp.pad(ga, pad)


def _rows_from_internal(g, a, ga):
    tail = g.shape[1:]
    g = g.reshape((GH, GW) + tail)
    a = a.reshape((AH, AW) + tail)
    parts = [g[:, 0:GDK], g[:, GDK:2 * GDK], g[:, 2 * GDK:2 * GDK + GDV], g[:, 2 * GDK + GDV:GW]]
    parts = [t.reshape((-1,) + tail) for t in parts] + [ga[0:RANK]]
    parts += [a[:, i * AHD:(i + 1) * AHD].reshape((-1,) + tail) for i in range(4)]
    return jnp.concatenate(parts, axis=0)


def _slab_lo(chip):
    return min(SHARD * chip // 16 * 16, DIN - SLAB)


def _pack_rows(parts):
    rows = []
    for t in parts:
        flat = t.reshape(-1)
        rows.append(jnp.pad(flat, (0, (-flat.shape[0]) % 128)).reshape(-1, 128))
    buf = jnp.concatenate(rows, axis=0)
    return jnp.pad(buf, ((0, (-buf.shape[0]) % 8), (0, 0)))


def _unpack_rows(buf, shapes):
    out, r = [], 0
    for shp in shapes:
        n = 1
        for d in shp:
            n *= d
        nr = -(-n // 128)
        out.append(buf[r:r + nr].reshape(-1)[:n].reshape(shp))
        r += nr
    return out


def _layer_fwd(x, wm, wga, wout, gpre, gpost, wa, ba, ggla, gatt, rbx):
    z, zga, h = _inproj(x, gpre, wm, wga)
    cg = _gla_fwd(z, zga, wa, ba, ggla)
    ca, oraw, lse = _att_fwd(z, rbx, gatt)
    y, xo = _outproj(cg, ca, wout, x, gpost)
    return xo, (x, z, zga, h, cg, ca, oraw, lse, y)


def _layer_bwd(dout, saved, wm, wga, wout, gpre, gpost, wa, ba, ggla, gatt, rbx):
    x, z, zga, h, cg, ca, oraw, lse, y = saved
    dy, dcg, dca, dgpost = _post_bwd(dout, y, gpost, wout)
    dwout = jnp.concatenate([_matmul_tn(cg, dy, _XDT, 512, 1024, "dwout_gla"),
                             _matmul_tn(ca, dy, _XDT, 512, 1024, "dwout_att")], axis=0)
    dza, dgatt, dbx = _att_bwd(z, oraw, lse, dca, rbx, gatt)
    dzg, dga, dwa, dba, dggla = _gla_bwd(z, zga, wa, ba, ggla, dcg)
    dx, dgpre = _dh(dzg, dza, dga, wm, wga, x, dout, gpre)
    dwin = _rows_from_internal(_matmul_tn(dzg, h, _XDT, 512, 1024, "dwin_gla"), _matmul_tn(dza, h, _XDT, 512, 1024, "dwin_att"),
                               _matmul_tn(dga, h, _XDT, GAP, 1024, "dwin_gate"))
    drb = jnp.concatenate([jnp.zeros((AH, 1), F32), dbx[:, 0, ::-1]], axis=1)
    return dx, dwin, dwout, (dgpre[0], dgpost[0], dwa[0:RANK], dba[0], dggla[0], dgatt[0], drb)


def _rel_rows(rb):
    return rb[:, :0:-1][:, None, :]


def kernel(x, w_in, w_out, g_pre, g_post, w_alpha, b_alpha, g_gla, g_att, rel_bias, loss_target, m_w_in, m_w_out, m_g_pre, m_g_post, m_w_alpha, m_b_alpha, m_g_gla, m_g_att, m_rel_bias, v_w_in, v_w_out, v_g_pre, v_g_post, v_w_alpha, v_b_alpha, v_g_gla, v_g_att, v_rel_bias):
    nl = w_in.shape[0]
    ax, ay, ac = lax.axis_index("x"), lax.axis_index("y"), lax.axis_index("c")
    chip = 2 * ax + ay
    c_idx = jnp.reshape(ac, (1,)).astype(jnp.int32)
    chip_idx = jnp.reshape(chip, (1,)).astype(jnp.int32)

    wt_own = jnp.transpose(w_in, (2, 0, 1)).astype(_CDT).reshape(2, 2, SHARD // 4, nl, D)
    wt_all, wout_all, wa_all = _gather_weights(wt_own, w_out.astype(_CDT).reshape(nl, 2, 2, D // NCHIP // 4, D), w_alpha)
    wout_all = wout_all.reshape(nl, D, D)
    wm, wga = [jnp.transpose(t, (1, 0, 2)) for t in _rows_to_internal(wt_all.reshape(DIN, nl, D))]
    wa_full = jnp.transpose(wa_all, (1, 2, 0, 3)).reshape(nl, RANK, GH * GDK)
    wa_pad = jnp.pad(wa_full, ((0, 0), (0, GAP - RANK), (0, 0))).astype(_CDT)
    rbx = [_rel_rows(rel_bias[l]) for l in range(nl)]

    def weights(l):
        return (wm[l], wga[l], wout_all[l], g_pre[l][None], g_post[l][None], wa_pad[l], b_alpha[l][None],
                g_gla[l][None], g_att[l][None], rbx[l])

    h = x[0]
    saved = []
    for l in range(nl):
        h, sv = _layer_fwd(h, *weights(l))
        saved.append(sv)
    dout, loss_part = _loss_grad(h, loss_target[0])
    loss = lax.psum(loss_part[0, 0], ("x", "y", "c"))

    dwin, dwout, small = [None] * nl, [None] * nl, [None] * nl
    for l in reversed(range(nl)):
        dout, dwin[l], dwout[l], small[l] = _layer_bwd(dout, saved[l], *weights(l))
    grad_x = dout[None]

    gt = jnp.stack(dwin)
    slabs = jnp.stack([gt[:, _slab_lo(i):_slab_lo(i) + SLAB] for i in range(NCHIP)], axis=1)
    gin2 = jnp.transpose(slabs.reshape(nl, NCHIP, 2, HSLAB, D), (2, 0, 1, 3, 4)).reshape(2, nl * NCHIP * HSLAB, D)
    gout = jnp.stack(dwout).reshape(nl, NCHIP, 2, D // NCHIP // 2, D)
    gout2 = jnp.transpose(gout, (2, 0, 1, 3, 4)).reshape(2, nl * NCHIP * (D // NCHIP // 2), D)
    rin, rout = _swap_halves(gin2, gout2)
    pin = _add_halves(c_idx, gin2, rin, 256, "add_halves_in").reshape(nl, NCHIP, HSLAB, D)
    pout = _add_halves(c_idx, gout2, rout, 256, "add_halves_out").reshape(nl, NCHIP, D // NCHIP // 2, D)
    bin_, bout = _send_to_owners(pin, pout)
    hin = _add_chips(chip_idx, pin, bin_, 304, "add_chips_in")
    hout = _add_chips(chip_idx, pout, bout, 256, "add_chips_out")
    xin, xout = _exchange_halves(hin, hout)

    slab = jnp.concatenate([jnp.where(ac == 0, hin, xin), jnp.where(ac == 0, xin, hin)], axis=1)
    off = sum(jnp.where(chip == i, SHARD * i - _slab_lo(i), 0) for i in range(NCHIP))
    g_rows = jnp.transpose(lax.dynamic_slice_in_dim(slab, off, SHARD, axis=1), (1, 0, 2))
    rows_first = lambda t: jnp.transpose(t, (2, 0, 1))
    d_rows, nm_rows, nv_rows = _adam_rows(rows_first(w_in), g_rows, rows_first(m_w_in), rows_first(v_w_in), 32, "adam_w_in")
    g_w_in, d_w_in, nm_w_in, nv_w_in = [jnp.transpose(t, (1, 2, 0)) for t in (g_rows, d_rows, nm_rows, nv_rows)]

    def adam_big(w, g_own, g_other, m, v, name):
        shp = w.shape
        halves = lambda t: t.reshape(shp[0], 2, shp[1] // 2, shp[2])
        return [t.reshape(shp) for t in _adam_halves(c_idx, halves(w), g_own, g_other, halves(m), halves(v), 256, name)]

    g_w_out, d_w_out, nm_w_out, nv_w_out = adam_big(w_out, hout, xout, m_w_out, v_w_out, "adam_w_out")

    stacked = [jnp.stack([small[l][i] for l in range(nl)]) for i in range(7)]
    g_small = _unpack_rows(_allreduce_small(_pack_rows(stacked)), [t.shape for t in stacked])
    g_gpre, g_gpost, g_wa_full, g_ba, g_ggla, g_gatt, g_rb = g_small
    g_wa = lax.dynamic_slice_in_dim(g_wa_full, chip * GDK, GDK, axis=2)
    names = [(g_pre, m_g_pre, v_g_pre, g_gpre), (g_post, m_g_post, v_g_post, g_gpost), (w_alpha, m_w_alpha, v_w_alpha, g_wa),
             (b_alpha, m_b_alpha, v_b_alpha, g_ba), (g_gla, m_g_gla, v_g_gla, g_ggla), (g_att, m_g_att, v_g_att, g_gatt),
             (rel_bias, m_rel_bias, v_rel_bias, g_rb)]
    shapes = [t[0].shape for t in names]
    packed = [_pack_rows([t[i] for t in names]) for i in range(4)]
    d_s, nm_s, nv_s = [_unpack_rows(t, shapes) for t in _adam(packed[0], packed[3], packed[1], packed[2], packed[0].shape[0], "adam_small")]

    grads = [g_w_in, g_w_out, g_gpre, g_gpost, g_wa, g_ba, g_ggla, g_gatt, g_rb]
    deltas = [d_w_in, d_w_out] + d_s
    new_m = [nm_w_in, nm_w_out] + nm_s
    new_v = [nv_w_in, nv_w_out] + nv_s
    return (loss, grad_x, *grads, *deltas, *new_m, *new_v)
```

```python
import functools

import jax
import jax.numpy as jnp
import numpy as np
from jax import lax
from jax.experimental import pallas as pl
from jax.experimental.pallas import tpu as pltpu

D = 2048
DEPTH = 4
CHUNK = 64
GH, GDK, GDV = 4, 128, 256
DGLA = GH * GDV
RANK = 16
TAU = 16.0
AH, AHD = 8, 128
DATT = AH * AHD
LEFT = 8
NREL = 257
EPS = 1e-6
DIN = 7184
ADAM_LR, ADAM_B1, ADAM_B2, ADAM_EPS, ADAM_WD, ADAM_STEP = 0.001, 0.9, 0.999, 1e-08, 0.01, 10

GW = 2 * GDK + 2 * GDV
AW = 4 * AHD
ZG = GH * GW
ZA = AH * AW
ZM = ZG + ZA
GAP = 128
QB = 2 * CHUNK
BANDW = (LEFT + 2) * CHUNK
PADK = LEFT * CHUNK
NCHIP = 4
SHARD = DIN // NCHIP
SLAB = 1824
HSLAB = SLAB // 2
NEG = -1e30
F32 = jnp.float32
_CDT = jnp.bfloat16
_XDT = jnp.bfloat16
_VMEM = 56 * 1024 * 1024
MESH = pl.DeviceIdType.MESH
ANY = pl.BlockSpec(memory_space=pl.ANY)


def _dot(a, b):
    return jnp.dot(a, b, preferred_element_type=F32)


def _dot_nt(a, b):
    return lax.dot_general(a, b, (((1,), (1,)), ((), ())), preferred_element_type=F32)


def _dot_tn(a, b):
    return lax.dot_general(a, b, (((0,), (0,)), ((), ())), preferred_element_type=F32)


def _rms_rows(v):
    return lax.rsqrt(jnp.mean(v * v, axis=-1, keepdims=True) + EPS)


def _sigmoid(v):
    return 1.0 / (1.0 + jnp.exp(-v))


def _log_sigmoid(v):
    return jnp.minimum(v, 0.0) - jnp.log(1.0 + jnp.exp(-jnp.abs(v)))


def _exact_dot(tri, v):
    hi = v.astype(_CDT)
    r1 = v - hi.astype(F32)
    mid = r1.astype(_CDT)
    lo = (r1 - mid.astype(F32)).astype(_CDT)
    return _dot(tri, hi) + _dot(tri, mid) + _dot(tri, lo)


def _tri(strict):
    row = lax.broadcasted_iota(jnp.int32, (CHUNK, CHUNK), 0)
    col = lax.broadcasted_iota(jnp.int32, (CHUNK, CHUNK), 1)
    return jnp.where((col < row) if strict else (col <= row), 1.0, 0.0).astype(_CDT)


def _norm_gate_bwd(o, g, gate, dcat):
    r = _rms_rows(o)
    oh = o * r
    sg = _sigmoid(gate)
    dn = dcat * (gate * sg)
    dgate = dcat * (oh * g) * (sg * (1.0 + gate * (1.0 - sg)))
    dg = jnp.sum(dn * oh, axis=0, keepdims=True)
    dnn = dn * g
    do = r * (dnn - oh * jnp.mean(dnn * oh, axis=-1, keepdims=True))
    return do, dgate, dg


def _params(sem=None, vmem=_VMEM):
    return pltpu.CompilerParams(dimension_semantics=sem, vmem_limit_bytes=vmem)


def _inproj(x, g, wm, wga, tm=512, tn=1024):
    s = x.shape[0]

    def body(x_ref, g_ref, wm_ref, wga_ref, z_ref, zga_ref, h_ref, hs):
        @pl.when(pl.program_id(1) == 0)
        def _():
            xv = x_ref[...]
            hv = (xv * _rms_rows(xv) * g_ref[...]).astype(_CDT)
            hs[...] = hv
            h_ref[...] = hv
            zga_ref[...] = _dot_nt(hv, wga_ref[...]).astype(_CDT)

        z_ref[...] = _dot_nt(hs[...], wm_ref[...]).astype(_CDT)

    return pl.pallas_call(
        body, name="inproj", grid=(s // tm, ZM // tn),
        in_specs=[pl.BlockSpec((tm, D), lambda i, j: (i, 0)), pl.BlockSpec((1, D), lambda i, j: (0, 0)),
                  pl.BlockSpec((tn, D), lambda i, j: (j, 0)), pl.BlockSpec((GAP, D), lambda i, j: (0, 0))],
        out_specs=[pl.BlockSpec((tm, tn), lambda i, j: (i, j)), pl.BlockSpec((tm, GAP), lambda i, j: (i, 0)),
                   pl.BlockSpec((tm, D), lambda i, j: (i, 0))],
        out_shape=[jax.ShapeDtypeStruct((s, ZM), _CDT), jax.ShapeDtypeStruct((s, GAP), _CDT),
                   jax.ShapeDtypeStruct((s, D), _CDT)],
        scratch_shapes=[pltpu.VMEM((tm, D), _CDT)],
        compiler_params=_params(("parallel", "arbitrary")),
    )(x, g, wm, wga)


def _gla_fwd(z, zga, wa, ba, ggla):
    s = z.shape[0]
    nc = s // CHUNK

    def body(zg_ref, zga_ref, wa_ref, ba_ref, g_ref, cat_ref, la_s, st):
        la_s[...] = _log_sigmoid(_dot(zga_ref[...], wa_ref[...]) + ba_ref[...]) * (1.0 / TAU)
        st[...] = jnp.zeros_like(st)
        tri = _tri(False)

        def step(n, carry):
            rows = pl.ds(pl.multiple_of(n * CHUNK, CHUNK), CHUNK)
            la = la_s[rows, :]
            lc = _exact_dot(tri, la)
            lend = jnp.sum(la, axis=0, keepdims=True)
            kdec = (zg_ref[rows, GDK:2 * GDK].astype(F32) * jnp.exp(lend - lc)).astype(_CDT)
            stn = jnp.exp(lend) * st[...] + _dot_tn(zg_ref[rows, 2 * GDK:2 * GDK + GDV], kdec)
            st[...] = stn
            qs = (zg_ref[rows, 0:GDK].astype(F32) * (GDK ** -0.5)).astype(_CDT)
            o = _dot_nt(qs, stn.astype(_CDT))
            gate = zg_ref[rows, 2 * GDK + GDV:GW].astype(F32)
            cat_ref[rows, :] = (o * _rms_rows(o) * g_ref[...] * (gate * _sigmoid(gate))).astype(_CDT)
            return carry

        lax.fori_loop(0, nc, step, 0)

    return pl.pallas_call(
        body, name="gla_fwd", grid=(GH,),
        in_specs=[pl.BlockSpec((s, GW), lambda h: (0, h)), pl.BlockSpec((s, GAP), lambda h: (0, 0)),
                  pl.BlockSpec((GAP, GDK), lambda h: (0, h)), pl.BlockSpec((1, GDK), lambda h: (0, h)),
                  pl.BlockSpec((1, GDV), lambda h: (0, h))],
        out_specs=pl.BlockSpec((s, GDV), lambda h: (0, h)),
        out_shape=jax.ShapeDtypeStruct((s, DGLA), _CDT),
        scratch_shapes=[pltpu.VMEM((s, GDK), F32), pltpu.VMEM((GDV, GDK), F32)],
        compiler_params=_params(("arbitrary",)),
    )(z, zga, wa, ba, ggla)


def _band_bias(b0):
    row = lax.broadcasted_iota(jnp.int32, (QB, 256), 0)
    col = lax.broadcasted_iota(jnp.int32, (QB, 256), 1)
    lane = lax.broadcasted_iota(jnp.int32, (1, 256), 1)
    c0 = jnp.sum(jnp.where(lane == 0, b0, 0.0), axis=1, keepdims=True)
    xv = jnp.broadcast_to(b0, (QB, 256))
    for bit in range(7):
        xv = jnp.where(((row >> bit) & 1) == 1, pltpu.roll(xv, 1 << bit, 1), xv)
    xv = jnp.where(col < row, c0, xv)
    return jnp.concatenate([jnp.broadcast_to(c0, (QB, BANDW - 256)), xv], axis=1)


def _band_static_mask():
    row = lax.broadcasted_iota(jnp.int32, (QB, BANDW), 0) >> 6
    col = lax.broadcasted_iota(jnp.int32, (QB, BANDW), 1) >> 6
    return (col >= row) & (col <= row + LEFT)


def _fold_bias_grad(t):
    row = lax.broadcasted_iota(jnp.int32, (QB, 256), 0)
    col = lax.broadcasted_iota(jnp.int32, (QB, 256), 1)
    xv = t[:, BANDW - 256:]
    low = col < row
    far = jnp.sum(t[:, 0:BANDW - 256], axis=1, keepdims=True) + jnp.sum(jnp.where(low, xv, 0.0), axis=1, keepdims=True)
    far = jnp.sum(far, axis=0, keepdims=True)
    xv = jnp.where(low, 0.0, xv)
    for bit in range(7):
        xv = jnp.where(((row >> bit) & 1) == 1, pltpu.roll(xv, 256 - (1 << bit), 1), xv)
    dp = jnp.sum(xv, axis=0, keepdims=True)
    lane = lax.broadcasted_iota(jnp.int32, (1, 256), 1)
    return dp + jnp.where(lane == 0, far, 0.0)


def _att_fwd(z, rbx, gatt):
    s = z.shape[0]
    nb = s // QB

    def body(za_ref, rb_ref, g_ref, cat_ref, o_ref, lse_ref, kp, vp, bias_s):
        kp[0:PADK, :] = jnp.zeros((PADK, AHD), _CDT)
        vp[0:PADK, :] = jnp.zeros((PADK, AHD), _CDT)
        kp[PADK:, :] = za_ref[:, AHD:2 * AHD]
        vp[PADK:, :] = za_ref[:, 2 * AHD:3 * AHD]
        bias_s[...] = jnp.where(_band_static_mask(), _band_bias(rb_ref[0]), NEG)

        def step(b, carry):
            r0 = pl.multiple_of(b * QB, QB)
            rows = pl.ds(r0, QB)
            band = pl.ds(r0, BANDW)
            col = lax.broadcasted_iota(jnp.int32, (QB, BANDW), 1)
            sc = _dot_nt(za_ref[rows, 0:AHD], kp[band, :]) * (AHD ** -0.5) + bias_s[...]
            sc = jnp.where(col >= PADK - r0, sc, NEG)
            m = jnp.max(sc, axis=-1, keepdims=True)
            p = jnp.exp(sc - m)
            l = jnp.sum(p, axis=-1, keepdims=True)
            o = _dot((p * (1.0 / l)).astype(_CDT), vp[band, :])
            o_ref[rows, :] = o.astype(_CDT)
            lse_ref[rows, :] = jnp.broadcast_to(m + jnp.log(l), (QB, AHD))
            gate = za_ref[rows, 3 * AHD:AW].astype(F32)
            cat_ref[rows, :] = (o * _rms_rows(o) * g_ref[...] * (gate * _sigmoid(gate))).astype(_CDT)
            return carry

        lax.fori_loop(0, nb, step, 0)

    return pl.pallas_call(
        body, name="att_fwd", grid=(AH,),
        in_specs=[pl.BlockSpec((s, AW), lambda h: (0, ZG // AW + h)), pl.BlockSpec((1, 1, 256), lambda h: (h, 0, 0)),
                  pl.BlockSpec((1, AHD), lambda h: (0, h))],
        out_specs=[pl.BlockSpec((s, AHD), lambda h: (0, h)), pl.BlockSpec((s, AHD), lambda h: (0, h)),
                   pl.BlockSpec((s, AHD), lambda h: (0, h))],
        out_shape=[jax.ShapeDtypeStruct((s, DATT), _CDT), jax.ShapeDtypeStruct((s, DATT), _CDT),
                   jax.ShapeDtypeStruct((s, DATT), F32)],
        scratch_shapes=[pltpu.VMEM((s + PADK, AHD), _CDT), pltpu.VMEM((s + PADK, AHD), _CDT),
                        pltpu.VMEM((QB, BANDW), F32)],
        compiler_params=_params(("arbitrary",)),
    )(z, rbx, gatt)


def _outproj(cg, ca, wout, x, gpost, tm=256):
    s = x.shape[0]

    def body(cg_ref, ca_ref, w_ref, x_ref, g_ref, y_ref, xo_ref):
        y = _dot(cg_ref[...], w_ref[0:DGLA, :]) + _dot(ca_ref[...], w_ref[DGLA:, :])
        y_ref[...] = y
        xo_ref[...] = x_ref[...] + y * _rms_rows(y) * g_ref[...]

    return pl.pallas_call(
        body, name="outproj", grid=(s // tm,),
        in_specs=[pl.BlockSpec((tm, DGLA), lambda i: (i, 0)), pl.BlockSpec((tm, DATT), lambda i: (i, 0)),
                  pl.BlockSpec((D, D), lambda i: (0, 0)), pl.BlockSpec((tm, D), lambda i: (i, 0)),
                  pl.BlockSpec((1, D), lambda i: (0, 0))],
        out_specs=[pl.BlockSpec((tm, D), lambda i: (i, 0)), pl.BlockSpec((tm, D), lambda i: (i, 0))],
        out_shape=[jax.ShapeDtypeStruct((s, D), F32), jax.ShapeDtypeStruct((s, D), F32)],
        compiler_params=_params(("parallel",)),
    )(cg, ca, wout, x, gpost)


def _loss_grad(xo, tgt, tm=256):
    s = xo.shape[0]

    def body(xo_ref, t_ref, d_ref, l_ref):
        @pl.when(pl.program_id(0) == 0)
        def _():
            l_ref[...] = jnp.zeros_like(l_ref)

        e = xo_ref[...] - t_ref[...]
        d_ref[...] = e * (1.0 / D)
        l_ref[...] += jnp.sum(jnp.sum(e * e, axis=1, keepdims=True), axis=0, keepdims=True) * (0.5 / D)

    return pl.pallas_call(
        body, name="loss_grad", grid=(s // tm,),
        in_specs=[pl.BlockSpec((tm, D), lambda i: (i, 0)), pl.BlockSpec((tm, D), lambda i: (i, 0))],
        out_specs=[pl.BlockSpec((tm, D), lambda i: (i, 0)), pl.BlockSpec((1, 1), lambda i: (0, 0))],
        out_shape=[jax.ShapeDtypeStruct((s, D), F32), jax.ShapeDtypeStruct((1, 1), F32)],
        compiler_params=_params(("arbitrary",)),
    )(xo, tgt)


def _post_bwd(dout, y, gpost, wout, tm=256):
    s = y.shape[0]

    def body(d_ref, y_ref, g_ref, w_ref, dy_ref, dcg_ref, dca_ref, dg_ref):
        @pl.when(pl.program_id(0) == 0)
        def _():
            dg_ref[...] = jnp.zeros_like(dg_ref)

        yv = y_ref[...]
        r = _rms_rows(yv)
        yh = yv * r
        dv = d_ref[...]
        dg_ref[...] += jnp.sum(dv * yh, axis=0, keepdims=True)
        dn = dv * g_ref[...]
        dyb = (r * (dn - yh * jnp.mean(dn * yh, axis=-1, keepdims=True))).astype(_CDT)
        dy_ref[...] = dyb
        dcg_ref[...] = _dot_nt(dyb, w_ref[0:DGLA, :]).astype(_CDT)
        dca_ref[...] = _dot_nt(dyb, w_ref[DGLA:, :]).astype(_CDT)

    return pl.pallas_call(
        body, name="post_bwd", grid=(s // tm,),
        in_specs=[pl.BlockSpec((tm, D), lambda i: (i, 0)), pl.BlockSpec((tm, D), lambda i: (i, 0)),
                  pl.BlockSpec((1, D), lambda i: (0, 0)), pl.BlockSpec((D, D), lambda i: (0, 0))],
        out_specs=[pl.BlockSpec((tm, D), lambda i: (i, 0)), pl.BlockSpec((tm, DGLA), lambda i: (i, 0)),
                   pl.BlockSpec((tm, DATT), lambda i: (i, 0)), pl.BlockSpec((1, D), lambda i: (0, 0))],
        out_shape=[jax.ShapeDtypeStruct((s, D), _CDT), jax.ShapeDtypeStruct((s, DGLA), _CDT),
                   jax.ShapeDtypeStruct((s, DATT), _CDT), jax.ShapeDtypeStruct((1, D), F32)],
        compiler_params=_params(("arbitrary",)),
    )(dout, y, gpost, wout)


def _matmul_tn(a, b, out_dtype, tm, tn, name):
    k, m = a.shape
    n = b.shape[1]

    def body(a_ref, b_ref, o_ref):
        o_ref[...] = _dot_tn(a_ref[...], b_ref[...]).astype(out_dtype)

    return pl.pallas_call(
        body, name=name, grid=(m // tm, n // tn),
        in_specs=[pl.BlockSpec((k, tm), lambda i, j: (0, i)), pl.BlockSpec((k, tn), lambda i, j: (0, j))],
        out_specs=pl.BlockSpec((tm, tn), lambda i, j: (i, j)),
        out_shape=jax.ShapeDtypeStruct((m, n), out_dtype),
        compiler_params=_params(("parallel", "parallel")),
    )(a, b)


def _att_bwd(z, oraw, lse, dca, rbx, gatt):
    s = z.shape[0]
    nb = s // QB

    def body(za_ref, o_ref, lse_ref, dc_ref, rb_ref, g_ref, dz_ref, dg_ref, db_ref, kp, vp, dkp, dvp, bias_s, t_s, dg_s):
        kp[0:PADK, :] = jnp.zeros((PADK, AHD), _CDT)
        vp[0:PADK, :] = jnp.zeros((PADK, AHD), _CDT)
        kp[PADK:, :] = za_ref[:, AHD:2 * AHD]
        vp[PADK:, :] = za_ref[:, 2 * AHD:3 * AHD]
        dkp[...] = jnp.zeros_like(dkp)
        dvp[...] = jnp.zeros_like(dvp)
        t_s[...] = jnp.zeros_like(t_s)
        dg_s[...] = jnp.zeros_like(dg_s)
        bias_s[...] = jnp.where(_band_static_mask(), _band_bias(rb_ref[0]), NEG)

        def step(b, carry):
            r0 = pl.multiple_of(b * QB, QB)
            rows = pl.ds(r0, QB)
            band = pl.ds(r0, BANDW)
            col = lax.broadcasted_iota(jnp.int32, (QB, BANDW), 1)
            o = o_ref[rows, :].astype(F32)
            do, dgate, dg = _norm_gate_bwd(o, g_ref[...], za_ref[rows, 3 * AHD:AW].astype(F32),
                                           dc_ref[rows, :].astype(F32))
            dg_s[...] += dg
            q = za_ref[rows, 0:AHD]
            kb = kp[band, :]
            sc = _dot_nt(q, kb) * (AHD ** -0.5) + bias_s[...]
            sc = jnp.where(col >= PADK - r0, sc, NEG)
            p = jnp.exp(sc - jnp.max(lse_ref[rows, :], axis=-1, keepdims=True))
            dob = do.astype(_CDT)
            dp = _dot_nt(dob, vp[band, :])
            ds = p * (dp - jnp.sum(do * o, axis=-1, keepdims=True))
            t_s[...] += ds
            dsb = (ds * (AHD ** -0.5)).astype(_CDT)
            dz_ref[rows, 0:AHD] = _dot(dsb, kb).astype(_CDT)
            dz_ref[rows, 3 * AHD:AW] = dgate.astype(_CDT)
            dkp[band, :] += _dot_tn(dsb, q)
            dvp[band, :] += _dot_tn(p.astype(_CDT), dob)
            return carry

        lax.fori_loop(0, nb, step, 0)
        dz_ref[:, AHD:2 * AHD] = dkp[PADK:, :].astype(_CDT)
        dz_ref[:, 2 * AHD:3 * AHD] = dvp[PADK:, :].astype(_CDT)
        dg_ref[...] = dg_s[...]
        db_ref[0] = _fold_bias_grad(t_s[...])

    return pl.pallas_call(
        body, name="att_bwd", grid=(AH,),
        in_specs=[pl.BlockSpec((s, AW), lambda h: (0, ZG // AW + h)), pl.BlockSpec((s, AHD), lambda h: (0, h)),
                  pl.BlockSpec((s, AHD), lambda h: (0, h)), pl.BlockSpec((s, AHD), lambda h: (0, h)),
                  pl.BlockSpec((1, 1, 256), lambda h: (h, 0, 0)), pl.BlockSpec((1, AHD), lambda h: (0, h))],
        out_specs=[pl.BlockSpec((s, AW), lambda h: (0, h)), pl.BlockSpec((1, AHD), lambda h: (0, h)),
                   pl.BlockSpec((1, 1, 256), lambda h: (h, 0, 0))],
        out_shape=[jax.ShapeDtypeStruct((s, ZA), _CDT), jax.ShapeDtypeStruct((1, DATT), F32),
                   jax.ShapeDtypeStruct((AH, 1, 256), F32)],
        scratch_shapes=[pltpu.VMEM((s + PADK, AHD), _CDT), pltpu.VMEM((s + PADK, AHD), _CDT),
                        pltpu.VMEM((s + PADK, AHD), F32), pltpu.VMEM((s + PADK, AHD), F32),
                        pltpu.VMEM((QB, BANDW), F32), pltpu.VMEM((QB, BANDW), F32), pltpu.VMEM((1, AHD), F32)],
        compiler_params=_params(("arbitrary",)),
    )(z, oraw, lse, dca, rbx, gatt)


def _gla_bwd(z, zga, wa, ba, ggla, dcg):
    s = z.shape[0]
    nc = s // CHUNK

    def body(zg_ref, zga_ref, wa_ref, ba_ref, g_ref, dc_ref, dz_ref, dga_ref, dwa_ref, dba_ref, dg_ref,
             la_s, om_s, sall, dpre_s, c_s, dga_s, dg_s):
        h = pl.program_id(0)
        pre = _dot(zga_ref[...], wa_ref[...]) + ba_ref[...]
        la_s[...] = _log_sigmoid(pre) * (1.0 / TAU)
        om_s[...] = (1.0 - _sigmoid(pre)) * (1.0 / TAU)
        c_s[...] = jnp.zeros_like(c_s)
        dg_s[...] = jnp.zeros_like(dg_s)
        tri = _tri(False)
        tri_strict = _tri(True)

        def decay(rows):
            la = la_s[rows, :]
            lend = jnp.sum(la, axis=0, keepdims=True)
            return jnp.exp(lend - _exact_dot(tri, la)), jnp.exp(lend)

        def fwd(n, st):
            rows = pl.ds(pl.multiple_of(n * CHUNK, CHUNK), CHUNK)
            dec, a = decay(rows)
            kdec = (zg_ref[rows, GDK:2 * GDK].astype(F32) * dec).astype(_CDT)
            stn = a * st + _dot_tn(zg_ref[rows, 2 * GDK:2 * GDK + GDV], kdec)
            sall[n] = stn
            return stn

        lax.fori_loop(0, nc, fwd, jnp.zeros((GDV, GDK), F32))

        def bwd(i, carry):
            n = nc - 1 - i
            rows = pl.ds(pl.multiple_of(n * CHUNK, CHUNK), CHUNK)
            dec, a = decay(rows)
            kf = zg_ref[rows, GDK:2 * GDK].astype(F32)
            kdec = kf * dec
            kdb = kdec.astype(_CDT)
            v = zg_ref[rows, 2 * GDK:2 * GDK + GDV]
            qs = (zg_ref[rows, 0:GDK].astype(F32) * (GDK ** -0.5)).astype(_CDT)
            stn = sall[n]
            stb = stn.astype(_CDT)
            st_prev = sall[jnp.maximum(n - 1, 0)] * jnp.where(n > 0, 1.0, 0.0)
            o = _dot_nt(qs, stb)
            do, dgate, dg = _norm_gate_bwd(o, g_ref[...], zg_ref[rows, 2 * GDK + GDV:GW].astype(F32),
                                           dc_ref[rows, :].astype(F32))
            dg_s[...] += dg
            dob = do.astype(_CDT)
            gt = _dot_tn(dob, qs) + c_s[...]
            gtb = gt.astype(_CDT)
            da = jnp.sum(gt * st_prev, axis=0, keepdims=True)
            dkdec = _dot(v, gtb)
            dla = _exact_dot(tri_strict, dkdec * kdec) + da * a
            dpre_s[rows, :] = dla * om_s[rows, :]
            dz_ref[rows, 0:GDK] = (_dot(dob, stb) * (GDK ** -0.5)).astype(_CDT)
            dz_ref[rows, GDK:2 * GDK] = (dkdec * dec).astype(_CDT)
            dz_ref[rows, 2 * GDK:2 * GDK + GDV] = _dot_nt(kdb, gtb).astype(_CDT)
            dz_ref[rows, 2 * GDK + GDV:GW] = dgate.astype(_CDT)
            c_s[...] = a * gt
            return carry

        lax.fori_loop(0, nc, bwd, 0)
        dpre = dpre_s[...]
        dpb = dpre.astype(_CDT)
        dg_ref[...] = dg_s[...]
        dba_ref[...] = jnp.sum(dpre, axis=0, keepdims=True)
        dwa_ref[...] = _dot_tn(zga_ref[...], dpb)
        part = _dot_nt(dpb, wa_ref[...])

        @pl.when(h == 0)
        def _():
            dga_s[...] = part

        @pl.when(h > 0)
        def _():
            dga_s[...] += part

        @pl.when(h == GH - 1)
        def _():
            dga_ref[...] = dga_s[...].astype(_CDT)

    return pl.pallas_call(
        body, name="gla_bwd", grid=(GH,),
        in_specs=[pl.BlockSpec((s, GW), lambda h: (0, h)), pl.BlockSpec((s, GAP), lambda h: (0, 0)),
                  pl.BlockSpec((GAP, GDK), lambda h: (0, h)), pl.BlockSpec((1, GDK), lambda h: (0, h)),
                  pl.BlockSpec((1, GDV), lambda h: (0, h)), pl.BlockSpec((s, GDV), lambda h: (0, h))],
        out_specs=[pl.BlockSpec((s, GW), lambda h: (0, h)), pl.BlockSpec((s, GAP), lambda h: (0, 0)),
                   pl.BlockSpec((GAP, GDK), lambda h: (0, h)), pl.BlockSpec((1, GDK), lambda h: (0, h)),
                   pl.BlockSpec((1, GDV), lambda h: (0, h))],
        out_shape=[jax.ShapeDtypeStruct((s, ZG), _CDT), jax.ShapeDtypeStruct((s, GAP), _CDT),
                   jax.ShapeDtypeStruct((GAP, GH * GDK), F32), jax.ShapeDtypeStruct((1, GH * GDK), F32),
                   jax.ShapeDtypeStruct((1, DGLA), F32)],
        scratch_shapes=[pltpu.VMEM((s, GDK), F32), pltpu.VMEM((s, GDK), F32), pltpu.VMEM((nc, GDV, GDK), F32),
                        pltpu.VMEM((s, GDK), F32), pltpu.VMEM((GDV, GDK), F32), pltpu.VMEM((s, GAP), F32),
                        pltpu.VMEM((1, GDV), F32)],
        compiler_params=_params(("arbitrary",)),
    )(z, zga, wa, ba, ggla, dcg)


def _dh(dzg, dza, dga, wm, wga, x, dout, gpre, tm=512, tk=1024):
    s = x.shape[0]
    nkg, nk = ZG // tk, ZM // tk

    def body(dzg_ref, dza_ref, dga_ref, wm_ref, wga_ref, x_ref, d_ref, g_ref, dx_ref, dg_ref, acc):
        i, k = pl.program_id(0), pl.program_id(1)

        @pl.when((i == 0) & (k == 0))
        def _():
            dg_ref[...] = jnp.zeros_like(dg_ref)

        @pl.when(k == 0)
        def _():
            acc[...] = _dot(dga_ref[...], wga_ref[...])

        @pl.when(k < nkg)
        def _():
            acc[...] += _dot(dzg_ref[...], wm_ref[...])

        @pl.when(k >= nkg)
        def _():
            acc[...] += _dot(dza_ref[...], wm_ref[...])

        @pl.when(k == nk - 1)
        def _():
            xv = x_ref[...]
            r = _rms_rows(xv)
            xh = xv * r
            dh = acc[...]
            dg_ref[...] += jnp.sum(dh * xh, axis=0, keepdims=True)
            dn = dh * g_ref[...]
            dx_ref[...] = d_ref[...] + r * (dn - xh * jnp.mean(dn * xh, axis=-1, keepdims=True))

    return pl.pallas_call(
        body, name="dh", grid=(s // tm, nk),
        in_specs=[pl.BlockSpec((tm, tk), lambda i, k: (i, jnp.minimum(k, nkg - 1))),
                  pl.BlockSpec((tm, tk), lambda i, k: (i, jnp.maximum(k - nkg, 0))),
                  pl.BlockSpec((tm, GAP), lambda i, k: (i, 0)), pl.BlockSpec((tk, D), lambda i, k: (k, 0)),
                  pl.BlockSpec((GAP, D), lambda i, k: (0, 0)), pl.BlockSpec((tm, D), lambda i, k: (i, 0)),
                  pl.BlockSpec((tm, D), lambda i, k: (i, 0)), pl.BlockSpec((1, D), lambda i, k: (0, 0))],
        out_specs=[pl.BlockSpec((tm, D), lambda i, k: (i, 0)), pl.BlockSpec((1, D), lambda i, k: (0, 0))],
        out_shape=[jax.ShapeDtypeStruct((s, D), F32), jax.ShapeDtypeStruct((1, D), F32)],
        scratch_shapes=[pltpu.VMEM((tm, D), F32)],
        compiler_params=_params(("arbitrary", "arbitrary")),
    )(dzg, dza, dga, wm, wga, x, dout, gpre)


def _adam(w, g, m, v, tr, name):
    rws, cols = w.shape

    def body(w_ref, g_ref, m_ref, v_ref, d_ref, mo_ref, vo_ref):
        gv = g_ref[...]
        mn = ADAM_B1 * m_ref[...] + (1.0 - ADAM_B1) * gv
        vn = ADAM_B2 * v_ref[...] + (1.0 - ADAM_B2) * (gv * gv)
        mh = mn / (1.0 - ADAM_B1 ** ADAM_STEP)
        vh = vn / (1.0 - ADAM_B2 ** ADAM_STEP)
        d_ref[...] = -ADAM_LR * (mh / (jnp.sqrt(vh) + ADAM_EPS) + ADAM_WD * w_ref[...])
        mo_ref[...] = mn
        vo_ref[...] = vn

    spec = pl.BlockSpec((tr, cols), lambda i: (i, 0))
    return pl.pallas_call(
        body, name=name, grid=(rws // tr,), in_specs=[spec] * 4, out_specs=[spec] * 3,
        out_shape=[jax.ShapeDtypeStruct((rws, cols), F32)] * 3,
        compiler_params=_params(("parallel",)),
    )(w, g, m, v)


def _adam_rows(w, g, m, v, tj, name):
    rows = w.shape[0]

    def body(w_ref, g_ref, m_ref, v_ref, d_ref, mo_ref, vo_ref):
        gv = g_ref[...]
        mn = ADAM_B1 * m_ref[...] + (1.0 - ADAM_B1) * gv
        vn = ADAM_B2 * v_ref[...] + (1.0 - ADAM_B2) * (gv * gv)
        mh = mn / (1.0 - ADAM_B1 ** ADAM_STEP)
        vh = vn / (1.0 - ADAM_B2 ** ADAM_STEP)
        d_ref[...] = -ADAM_LR * (mh / (jnp.sqrt(vh) + ADAM_EPS) + ADAM_WD * w_ref[...])
        mo_ref[...] = mn
        vo_ref[...] = vn

    spec = pl.BlockSpec((tj,) + w.shape[1:], lambda i: (i, 0, 0))
    return pl.pallas_call(
        body, name=name, grid=(pl.cdiv(rows, tj),), in_specs=[spec] * 4, out_specs=[spec] * 3,
        out_shape=[jax.ShapeDtypeStruct(w.shape, F32)] * 3,
        compiler_params=_params(("parallel",)),
    )(w, g, m, v)


def _place():
    x, y, c = lax.axis_index("x"), lax.axis_index("y"), lax.axis_index("c")
    chips = [(1 - x, y), (x, 1 - y), (1 - x, 1 - y)]
    return x, y, c, chips


def _row_chunks(n, parts):
    base, extra = divmod(n, parts)
    out, r = [], 0
    for i in range(parts):
        size = base + (1 if i < extra else 0)
        out.append((r, size))
        r += size
    return out


def _gather_weights(wt, wout, wa):
    nl = wout.shape[0]
    chunks = _row_chunks(wt.shape[2], nl)

    def variant(x, y, c, wt_ref, wout_ref, wa_ref, wt_all, wout_all, wa_all, send_sems, recv_sems, wa_send, wa_recv):
        me = 2 * x + y
        sib = (x, y, 1 - c)
        xn, yn, dg = (1 - x, y), (x, 1 - y), (1 - x, 1 - y)

        def wt_rows(ref, k):
            return ref.at[:, pl.ds(chunks[k][0], chunks[k][1])]

        arrays = [(lambda k: wt_rows(wt_ref.at[c], k), lambda chip, k: wt_rows(wt_all.at[chip, c], k)),
                  (lambda k: wout_ref.at[k, c], lambda chip, k: wout_all.at[k, chip, c])]

        def cid(p):
            return 2 * p[0] + p[1]

        def cp(src, dst, a, l, k, to):
            return pltpu.make_async_remote_copy(src_ref=src, dst_ref=dst, send_sem=send_sems.at[a, l, k],
                                                recv_sem=recv_sems.at[a, l, k], device_id=to, device_id_type=MESH)

        def wa_cp(k, chip, to):
            return pltpu.make_async_remote_copy(src_ref=wa_ref, dst_ref=wa_all.at[chip], send_sem=wa_send.at[k],
                                                recv_sem=wa_recv.at[k], device_id=to, device_id_type=MESH)

        sent = []

        def go(cpy):
            cpy.start()
            sent.append(cpy)

        for l in range(nl):
            for a, (src, place) in enumerate(arrays):
                go(cp(src(l), place(me, l), a, l, 0, (xn[0], xn[1], c)))
                go(cp(src(l), place(me, l), a, l, 1, (yn[0], yn[1], c)))
        for k, p in enumerate((xn, yn, dg)):
            go(wa_cp(k, me, (p[0], p[1], c)))
        for l in range(nl):
            for a, (src, place) in enumerate(arrays):
                got = place(cid(xn), l)
                cp(got, got, a, l, 0, sib).wait_recv()
                go(cp(got.at[0], got.at[0], a, l, 2, (yn[0], yn[1], c)))
                go(cp(got, got, a, l, 4, sib))
                got = place(cid(yn), l)
                cp(got, got, a, l, 1, sib).wait_recv()
                go(cp(got.at[1], got.at[1], a, l, 3, (xn[0], xn[1], c)))
                go(cp(got, got, a, l, 5, sib))
        for l in range(nl):
            for a, (src, place) in enumerate(arrays):
                got = place(cid(dg), l)
                cp(got.at[0], got.at[0], a, l, 2, sib).wait_recv()
                cp(got.at[1], got.at[1], a, l, 3, sib).wait_recv()
                go(cp(got, got, a, l, 6, sib))
        for k, p in enumerate((xn, yn, dg)):
            wa_cp(k, cid(p), sib).wait_recv()
        for l in range(nl):
            for a, (src, place) in enumerate(arrays):
                for k in (4, 5, 6):
                    cp(place(me, l), place(me, l), a, l, k, sib).wait_recv()
        for cpy in sent:
            cpy.wait_send()

    def body(*refs):
        x, y, c, _ = _place()
        for jx in range(2):
            for jy in range(2):
                for jc in range(2):
                    pl.when((x == jx) & (y == jy) & (c == jc))(functools.partial(variant, jx, jy, jc, *refs))

    return pl.pallas_call(
        body, name="gather_weights", in_specs=[ANY, ANY, ANY], out_specs=[ANY, ANY, ANY],
        out_shape=[jax.ShapeDtypeStruct((NCHIP,) + wt.shape, wt.dtype),
                   jax.ShapeDtypeStruct((nl, NCHIP) + wout.shape[1:], wout.dtype),
                   jax.ShapeDtypeStruct((NCHIP,) + wa.shape, wa.dtype)],
        scratch_shapes=[pltpu.SemaphoreType.DMA((2, nl, 7)), pltpu.SemaphoreType.DMA((2, nl, 7)),
                        pltpu.SemaphoreType.DMA((3,)), pltpu.SemaphoreType.DMA((3,))],
    )(wt, wout, wa)


def _swap_halves(gin2, gout2):
    def body(gin_ref, gout_ref, rin, rout, send_sems, recv_sems):
        x, y, c, _ = _place()
        sib = (x, y, 1 - c)
        cps = [pltpu.make_async_remote_copy(src_ref=src.at[1 - c], dst_ref=dst, send_sem=send_sems.at[a],
                                            recv_sem=recv_sems.at[a], device_id=sib, device_id_type=MESH)
               for a, (src, dst) in enumerate([(gin_ref, rin), (gout_ref, rout)])]
        for cpy in cps:
            cpy.start()
        for cpy in cps:
            cpy.wait()

    return pl.pallas_call(
        body, name="swap_halves", in_specs=[ANY, ANY], out_specs=[ANY, ANY],
        out_shape=[jax.ShapeDtypeStruct(gin2.shape[1:], gin2.dtype), jax.ShapeDtypeStruct(gout2.shape[1:], gout2.dtype)],
        scratch_shapes=[pltpu.SemaphoreType.DMA((2,)), pltpu.SemaphoreType.DMA((2,))],
    )(gin2, gout2)


def _add_halves(c_idx, g2, r, tr, name):
    rows, cols = r.shape

    def body(c_ref, g_ref, r_ref, o_ref):
        o_ref[...] = (g_ref[0].astype(F32) + r_ref[...].astype(F32)).astype(_XDT)

    return pl.pallas_call(
        body, name=name,
        grid_spec=pltpu.PrefetchScalarGridSpec(
            num_scalar_prefetch=1, grid=(rows // tr,),
            in_specs=[pl.BlockSpec((1, tr, cols), lambda i, c_ref: (c_ref[0], i, 0)),
                      pl.BlockSpec((tr, cols), lambda i, c_ref: (i, 0))],
            out_specs=pl.BlockSpec((tr, cols), lambda i, c_ref: (i, 0))),
        out_shape=jax.ShapeDtypeStruct((rows, cols), _XDT),
        compiler_params=_params(("parallel",)),
    )(c_idx, g2, r)


def _send_to_owners(pin, pout):
    nl = pin.shape[0]

    def body(pin_ref, pout_ref, rin, rout, send_sems, recv_sems):
        x, y, c, chips = _place()
        cps = []
        for k, (px, py) in enumerate(chips):
            chip = 2 * px + py
            for a, (src, dst) in enumerate([(pin_ref, rin), (pout_ref, rout)]):
                cps.append(pltpu.make_async_remote_copy(src_ref=src.at[:, chip], dst_ref=dst.at[k], send_sem=send_sems.at[a, k],
                                                        recv_sem=recv_sems.at[a, k], device_id=(px, py, c), device_id_type=MESH))
        for cpy in cps:
            cpy.start()
        for cpy in cps:
            cpy.wait()

    return pl.pallas_call(
        body, name="send_to_owners", in_specs=[ANY, ANY], out_specs=[ANY, ANY],
        out_shape=[jax.ShapeDtypeStruct((3, nl) + pin.shape[2:], pin.dtype),
                   jax.ShapeDtypeStruct((3, nl) + pout.shape[2:], pout.dtype)],
        scratch_shapes=[pltpu.SemaphoreType.DMA((2, 3)), pltpu.SemaphoreType.DMA((2, 3))],
    )(pin, pout)


def _add_chips(chip_idx, p, r, tr, name):
    nl, _, rows, cols = p.shape

    def body(c_ref, p_ref, r_ref, o_ref):
        o_ref[0] = ((p_ref[0, 0].astype(F32) + r_ref[0, 0].astype(F32)) + r_ref[1, 0].astype(F32)) + r_ref[2, 0].astype(F32)

    return pl.pallas_call(
        body, name=name,
        grid_spec=pltpu.PrefetchScalarGridSpec(
            num_scalar_prefetch=1, grid=(nl, rows // tr),
            in_specs=[pl.BlockSpec((1, 1, tr, cols), lambda l, i, c_ref: (l, c_ref[0], i, 0)),
                      pl.BlockSpec((3, 1, tr, cols), lambda l, i, c_ref: (0, l, i, 0))],
            out_specs=pl.BlockSpec((1, tr, cols), lambda l, i, c_ref: (l, i, 0))),
        out_shape=jax.ShapeDtypeStruct((nl, rows, cols), F32),
        compiler_params=_params(("parallel", "parallel")),
    )(chip_idx, p, r)


def _exchange_halves(hin, hout):
    def body(hin_ref, hout_ref, oin, oout, send_sems, recv_sems):
        x, y, c, _ = _place()
        cps = [pltpu.make_async_remote_copy(src_ref=src, dst_ref=dst, send_sem=send_sems.at[a], recv_sem=recv_sems.at[a],
                                            device_id=(x, y, 1 - c), device_id_type=MESH)
               for a, (src, dst) in enumerate([(hin_ref, oin), (hout_ref, oout)])]
        for cpy in cps:
            cpy.start()
        for cpy in cps:
            cpy.wait()

    return pl.pallas_call(
        body, name="exchange_halves", in_specs=[ANY, ANY], out_specs=[ANY, ANY],
        out_shape=[jax.ShapeDtypeStruct(hin.shape, hin.dtype), jax.ShapeDtypeStruct(hout.shape, hout.dtype)],
        scratch_shapes=[pltpu.SemaphoreType.DMA((2,)), pltpu.SemaphoreType.DMA((2,))],
    )(hin, hout)


def _adam_halves(c_idx, w, g_own, g_other, m, v, tr, name):
    nl, _, rows, cols = w.shape

    def body(c_ref, w_ref, go_ref, gx_ref, m_ref, v_ref, g_ref, d_ref, mo_ref, vo_ref):
        gv = jnp.where(pl.program_id(1) == c_ref[0], go_ref[0], gx_ref[0])
        mn = ADAM_B1 * m_ref[0, 0] + (1.0 - ADAM_B1) * gv
        vn = ADAM_B2 * v_ref[0, 0] + (1.0 - ADAM_B2) * (gv * gv)
        mh = mn / (1.0 - ADAM_B1 ** ADAM_STEP)
        vh = vn / (1.0 - ADAM_B2 ** ADAM_STEP)
        g_ref[0, 0] = gv
        d_ref[0, 0] = -ADAM_LR * (mh / (jnp.sqrt(vh) + ADAM_EPS) + ADAM_WD * w_ref[0, 0])
        mo_ref[0, 0] = mn
        vo_ref[0, 0] = vn

    full = pl.BlockSpec((1, 1, tr, cols), lambda l, hh, i, c_ref: (l, hh, i, 0))
    own = pl.BlockSpec((1, tr, cols), lambda l, hh, i, c_ref: (l, jnp.where(hh == c_ref[0], i, 0), 0))
    other = pl.BlockSpec((1, tr, cols), lambda l, hh, i, c_ref: (l, jnp.where(hh == c_ref[0], 0, i), 0))
    return pl.pallas_call(
        body, name=name,
        grid_spec=pltpu.PrefetchScalarGridSpec(
            num_scalar_prefetch=1, grid=(nl, 2, rows // tr),
            in_specs=[full, own, other, full, full], out_specs=[full] * 4),
        out_shape=[jax.ShapeDtypeStruct(w.shape, F32)] * 4,
        compiler_params=_params(("parallel", "parallel", "parallel")),
    )(c_idx, w, g_own, g_other, m, v)


def _allreduce_small(sg):
    rows = sg.shape[0]
    vm = pl.BlockSpec(memory_space=pltpu.VMEM)

    def body(sg_ref, tot_ref, all_ref, send_sems, recv_sems):
        x, y, c, _ = _place()
        me = 4 * x + 2 * y + c
        all_ref[me] = sg_ref[...]
        cps = []
        for mask in range(1, 8):
            to = (1 - x if mask & 4 else x, 1 - y if mask & 2 else y, 1 - c if mask & 1 else c)
            cps.append(pltpu.make_async_remote_copy(src_ref=sg_ref, dst_ref=all_ref.at[me], send_sem=send_sems.at[mask - 1],
                                                    recv_sem=recv_sems.at[mask - 1], device_id=to, device_id_type=MESH))
        for cpy in cps:
            cpy.start()
        for cpy in cps:
            cpy.wait()
        acc = all_ref[0]
        for d in range(1, 8):
            acc = acc + all_ref[d]
        tot_ref[...] = acc

    return pl.pallas_call(
        body, name="allreduce_small", in_specs=[vm], out_specs=[vm, vm],
        out_shape=[jax.ShapeDtypeStruct((rows, 128), F32), jax.ShapeDtypeStruct((8, rows, 128), F32)],
        scratch_shapes=[pltpu.SemaphoreType.DMA((7,)), pltpu.SemaphoreType.DMA((7,))],
        compiler_params=_params(),
    )(sg)[0]


_CUTS = [0, 512, 1024, 2048, 3072, 3088, 4112, 5136, 6160, 7184]


def _rows_to_internal(w):
    tail = w.shape[1:]
    gq, gk, gv, gg, ga, aq, ak, av, ag = [w[_CUTS[i]:_CUTS[i + 1]] for i in range(9)]
    g = jnp.concatenate([gq.reshape((GH, GDK) + tail), gk.reshape((GH, GDK) + tail),
                         gv.reshape((GH, GDV) + tail), gg.reshape((GH, GDV) + tail)], axis=1).reshape((ZG,) + tail)
    a = jnp.concatenate([t.reshape((AH, AHD) + tail) for t in (aq, ak, av, ag)], axis=1).reshape((ZA,) + tail)
    pad = [(0, GAP - RANK)] + [(0, 0)] * len(tail)
    return jnp.concatenate([g, a], axis=0), jnp.pad(ga, pad)


def _rows_from_internal(g, a, ga):
    tail = g.shape[1:]
    g = g.reshape((GH, GW) + tail)
    a = a.reshape((AH, AW) + tail)
    parts = [g[:, 0:GDK], g[:, GDK:2 * GDK], g[:, 2 * GDK:2 * GDK + GDV], g[:, 2 * GDK + GDV:GW]]
    parts = [t.reshape((-1,) + tail) for t in parts] + [ga[0:RANK]]
    parts += [a[:, i * AHD:(i + 1) * AHD].reshape((-1,) + tail) for i in range(4)]
    return jnp.concatenate(parts, axis=0)


def _ref_row_code():
    g = np.arange(ZG).reshape(GH, GW)
    a = ZG + np.arange(ZA).reshape(AH, AW)
    parts = [g[:, 0:GDK], g[:, GDK:2 * GDK], g[:, 2 * GDK:2 * GDK + GDV], g[:, 2 * GDK + GDV:GW]]
    parts = [t.reshape(-1) for t in parts] + [ZM + np.arange(RANK)]
    parts += [a[:, i * AHD:(i + 1) * AHD].reshape(-1) for i in range(4)]
    return np.concatenate(parts)


def _ref_rows(pieces, lo, n):
    code = _ref_row_code()[lo:lo + n]
    bounds = [0, ZG, ZM, ZM + GAP]
    out, start = [], 0
    for i in range(1, n + 1):
        if i == n or code[i] != code[i - 1] + 1 or any(code[i] == b for b in bounds):
            p = max(k for k in range(3) if code[start] >= bounds[k])
            out.append(pieces[p][code[start] - bounds[p]:code[i - 1] + 1 - bounds[p]])
            start = i
    return jnp.concatenate(out, axis=0)


def _slab_lo(chip):
    return min(SHARD * chip // 16 * 16, DIN - SLAB)


def _pack_rows(parts):
    rows = []
    for t in parts:
        flat = t.reshape(-1)
        rows.append(jnp.pad(flat, (0, (-flat.shape[0]) % 128)).reshape(-1, 128))
    buf = jnp.concatenate(rows, axis=0)
    return jnp.pad(buf, ((0, (-buf.shape[0]) % 8), (0, 0)))


def _unpack_rows(buf, shapes):
    out, r = [], 0
    for shp in shapes:
        n = 1
        for d in shp:
            n *= d
        nr = -(-n // 128)
        out.append(buf[r:r + nr].reshape(-1)[:n].reshape(shp))
        r += nr
    return out


def _layer_fwd(x, wm, wga, wout, gpre, gpost, wa, ba, ggla, gatt, rbx):
    z, zga, h = _inproj(x, gpre, wm, wga)
    cg = _gla_fwd(z, zga, wa, ba, ggla)
    ca, oraw, lse = _att_fwd(z, rbx, gatt)
    y, xo = _outproj(cg, ca, wout, x, gpost)
    return xo, (x, z, zga, h, cg, ca, oraw, lse, y)


def _layer_bwd(dout, saved, wm, wga, wout, gpre, gpost, wa, ba, ggla, gatt, rbx):
    x, z, zga, h, cg, ca, oraw, lse, y = saved
    dy, dcg, dca, dgpost = _post_bwd(dout, y, gpost, wout)
    dwout = jnp.concatenate([_matmul_tn(cg, dy, _XDT, 512, 1024, "dwout_gla"),
                             _matmul_tn(ca, dy, _XDT, 512, 1024, "dwout_att")], axis=0)
    dza, dgatt, dbx = _att_bwd(z, oraw, lse, dca, rbx, gatt)
    dzg, dga, dwa, dba, dggla = _gla_bwd(z, zga, wa, ba, ggla, dcg)
    dx, dgpre = _dh(dzg, dza, dga, wm, wga, x, dout, gpre)
    dwin = (_matmul_tn(dzg, h, _XDT, 512, 1024, "dwin_gla"), _matmul_tn(dza, h, _XDT, 512, 1024, "dwin_att"),
            _matmul_tn(dga, h, _XDT, GAP, 1024, "dwin_gate"))
    drb = jnp.concatenate([jnp.zeros((AH, 1), F32), dbx[:, 0, ::-1]], axis=1)
    return dx, dwin, dwout, (dgpre[0], dgpost[0], dwa[0:RANK], dba[0], dggla[0], dgatt[0], drb)


def _rel_rows(rb):
    return rb[:, :0:-1][:, None, :]


def kernel(x, w_in, w_out, g_pre, g_post, w_alpha, b_alpha, g_gla, g_att, rel_bias, loss_target, m_w_in, m_w_out, m_g_pre, m_g_post, m_w_alpha, m_b_alpha, m_g_gla, m_g_att, m_rel_bias, v_w_in, v_w_out, v_g_pre, v_g_post, v_w_alpha, v_b_alpha, v_g_gla, v_g_att, v_rel_bias):
    nl = w_in.shape[0]
    ax, ay, ac = lax.axis_index("x"), lax.axis_index("y"), lax.axis_index("c")
    chip = 2 * ax + ay
    c_idx = jnp.reshape(ac, (1,)).astype(jnp.int32)
    chip_idx = jnp.reshape(chip, (1,)).astype(jnp.int32)

    wt_own = jnp.transpose(w_in, (2, 0, 1)).astype(_CDT).reshape(2, 2, SHARD // 4, nl, D)
    wout_own = w_out.astype(_CDT).reshape(nl, 2, 2, D // NCHIP // 4, D)
    wt_all, wout_all, wa_all = _gather_weights(wt_own, wout_own, w_alpha)
    mine = lambda i, own, got: jnp.where(chip == i, own, got)
    wt_all = jnp.stack([mine(i, wt_own, wt_all[i]) for i in range(NCHIP)]).reshape(DIN, nl, D)
    wout_all = jnp.stack([mine(i, wout_own, wout_all[:, i]) for i in range(NCHIP)], axis=1).reshape(nl, D, D)
    wa_all = jnp.stack([mine(i, w_alpha, wa_all[i]) for i in range(NCHIP)])
    wm, wga = zip(*[_rows_to_internal(wt_all[:, l]) for l in range(nl)])
    wa_full = jnp.transpose(wa_all, (1, 2, 0, 3)).reshape(nl, RANK, GH * GDK)
    wa_pad = jnp.pad(wa_full, ((0, 0), (0, GAP - RANK), (0, 0))).astype(_CDT)
    rbx = [_rel_rows(rel_bias[l]) for l in range(nl)]

    def weights(l):
        return (wm[l], wga[l], wout_all[l], g_pre[l][None], g_post[l][None], wa_pad[l], b_alpha[l][None],
                g_gla[l][None], g_att[l][None], rbx[l])

    h = x[0]
    saved = []
    for l in range(nl):
        h, sv = _layer_fwd(h, *weights(l))
        saved.append(sv)
    dout, loss_part = _loss_grad(h, loss_target[0])

    dwin, dwout, small = [None] * nl, [None] * nl, [None] * nl
    for l in reversed(range(nl)):
        dout, dwin[l], dwout[l], small[l] = _layer_bwd(dout, saved[l], *weights(l))
    grad_x = dout[None]

    gin2 = jnp.concatenate([_ref_rows(dwin[l], _slab_lo(i) + hh * HSLAB, HSLAB)
                            for hh in range(2) for l in range(nl) for i in range(NCHIP)]).reshape(2, nl * NCHIP * HSLAB, D)
    gout = jnp.stack(dwout).reshape(nl, NCHIP, 2, D // NCHIP // 2, D)
    gout2 = jnp.transpose(gout, (2, 0, 1, 3, 4)).reshape(2, nl * NCHIP * (D // NCHIP // 2), D)
    rin, rout = _swap_halves(gin2, gout2)
    pin = _add_halves(c_idx, gin2, rin, 256, "add_halves_in").reshape(nl, NCHIP, HSLAB, D)
    pout = _add_halves(c_idx, gout2, rout, 256, "add_halves_out").reshape(nl, NCHIP, D // NCHIP // 2, D)
    bin_, bout = _send_to_owners(pin, pout)
    hin = _add_chips(chip_idx, pin, bin_, 304, "add_chips_in")
    hout = _add_chips(chip_idx, pout, bout, 256, "add_chips_out")
    xin, xout = _exchange_halves(hin, hout)

    slab = jnp.concatenate([jnp.where(ac == 0, hin, xin), jnp.where(ac == 0, xin, hin)], axis=1)
    off = sum(jnp.where(chip == i, SHARD * i - _slab_lo(i), 0) for i in range(NCHIP))
    g_rows = jnp.transpose(lax.dynamic_slice_in_dim(slab, off, SHARD, axis=1), (1, 0, 2))
    rows_first = lambda t: jnp.transpose(t, (2, 0, 1))
    d_rows, nm_rows, nv_rows = _adam_rows(rows_first(w_in), g_rows, rows_first(m_w_in), rows_first(v_w_in), 32, "adam_w_in")
    g_w_in, d_w_in, nm_w_in, nv_w_in = [jnp.transpose(t, (1, 2, 0)) for t in (g_rows, d_rows, nm_rows, nv_rows)]

    def adam_big(w, g_own, g_other, m, v, name):
        shp = w.shape
        halves = lambda t: t.reshape(shp[0], 2, shp[1] // 2, shp[2])
        return [t.reshape(shp) for t in _adam_halves(c_idx, halves(w), g_own, g_other, halves(m), halves(v), 256, name)]

    g_w_out, d_w_out, nm_w_out, nv_w_out = adam_big(w_out, hout, xout, m_w_out, v_w_out, "adam_w_out")

    stacked = [jnp.stack([small[l][i] for l in range(nl)]) for i in range(7)] + [loss_part]
    g_small = _unpack_rows(_allreduce_small(_pack_rows(stacked)), [t.shape for t in stacked])
    g_gpre, g_gpost, g_wa_full, g_ba, g_ggla, g_gatt, g_rb, loss_sum = g_small
    loss = loss_sum[0, 0]
    g_wa = lax.dynamic_slice_in_dim(g_wa_full, chip * GDK, GDK, axis=2)
    names = [(g_pre, m_g_pre, v_g_pre, g_gpre), (g_post, m_g_post, v_g_post, g_gpost), (w_alpha, m_w_alpha, v_w_alpha, g_wa),
             (b_alpha, m_b_alpha, v_b_alpha, g_ba), (g_gla, m_g_gla, v_g_gla, g_ggla), (g_att, m_g_att, v_g_att, g_gatt),
             (rel_bias, m_rel_bias, v_rel_bias, g_rb)]
    shapes = [t[0].shape for t in names]
    packed = [_pack_rows([t[i] for t in names]) for i in range(4)]
    d_s, nm_s, nv_s = [_unpack_rows(t, shapes) for t in _adam(packed[0], packed[3], packed[1], packed[2], packed[0].shape[0], "adam_small")]

    grads = [g_w_in, g_w_out, g_gpre, g_gpost, g_wa, g_ba, g_ggla, g_gatt, g_rb]
    deltas = [d_w_in, d_w_out] + d_s
    new_m = [nm_w_in, nm_w_out] + nm_s
    new_v = [nv_w_in, nv_w_out] + nv_s
    return (loss, grad_x, *grads, *deltas, *new_m, *new_v)
```

```python
import functools

import jax
import jax.numpy as jnp
from jax import lax
from jax.experimental import pallas as pl
from jax.experimental.pallas import tpu as pltpu

D = 2048
DEPTH = 4
CHUNK = 64
GH, GDK, GDV = 4, 128, 256
DGLA = GH * GDV
RANK = 16
TAU = 16.0
AH, AHD = 8, 128
DATT = AH * AHD
LEFT = 8
NREL = 257
EPS = 1e-6
DIN = 7184
ADAM_LR, ADAM_B1, ADAM_B2, ADAM_EPS, ADAM_WD, ADAM_STEP = 0.001, 0.9, 0.999, 1e-08, 0.01, 10

GW = 2 * GDK + 2 * GDV
AW = 4 * AHD
ZG = GH * GW
ZA = AH * AW
ZM = ZG + ZA
GAP = 128
QB = 2 * CHUNK
HP = 2
BANDW = (LEFT + 2) * CHUNK
PADK = LEFT * CHUNK
NCHIP = 4
SHARD = DIN // NCHIP
SLAB = 1824
HSLAB = SLAB // 2
NEG = -1e30
F32 = jnp.float32
_CDT = jnp.bfloat16
_XDT = jnp.bfloat16
_VMEM = 56 * 1024 * 1024
MESH = pl.DeviceIdType.MESH
ANY = pl.BlockSpec(memory_space=pl.ANY)


def _dot(a, b):
    return jnp.dot(a, b, preferred_element_type=F32)


def _dot_nt(a, b):
    return lax.dot_general(a, b, (((1,), (1,)), ((), ())), preferred_element_type=F32)


def _dot_tn(a, b):
    return lax.dot_general(a, b, (((0,), (0,)), ((), ())), preferred_element_type=F32)


def _rms_rows(v):
    return lax.rsqrt(jnp.mean(v * v, axis=-1, keepdims=True) + EPS)


def _sigmoid(v):
    return 1.0 / (1.0 + jnp.exp(-v))


def _log_sigmoid(v):
    return jnp.minimum(v, 0.0) - jnp.log(1.0 + jnp.exp(-jnp.abs(v)))


def _exact_dot(tri, v):
    hi = v.astype(_CDT)
    r1 = v - hi.astype(F32)
    mid = r1.astype(_CDT)
    lo = (r1 - mid.astype(F32)).astype(_CDT)
    return _dot(tri, hi) + _dot(tri, mid) + _dot(tri, lo)


def _tri(strict):
    row = lax.broadcasted_iota(jnp.int32, (CHUNK, CHUNK), 0)
    col = lax.broadcasted_iota(jnp.int32, (CHUNK, CHUNK), 1)
    return jnp.where((col < row) if strict else (col <= row), 1.0, 0.0).astype(_CDT)


def _norm_gate_bwd(o, g, gate, dcat):
    r = _rms_rows(o)
    oh = o * r
    sg = _sigmoid(gate)
    dn = dcat * (gate * sg)
    dgate = dcat * (oh * g) * (sg * (1.0 + gate * (1.0 - sg)))
    dg = jnp.sum(dn * oh, axis=0, keepdims=True)
    dnn = dn * g
    do = r * (dnn - oh * jnp.mean(dnn * oh, axis=-1, keepdims=True))
    return do, dgate, dg


def _params(sem=None, vmem=_VMEM):
    return pltpu.CompilerParams(dimension_semantics=sem, vmem_limit_bytes=vmem)


def _inproj(x, g, wm, wga, tm=512, tn=1024):
    s = x.shape[0]

    def body(x_ref, g_ref, wm_ref, wga_ref, z_ref, zga_ref, h_ref, hs):
        @pl.when(pl.program_id(1) == 0)
        def _():
            xv = x_ref[...]
            hv = (xv * _rms_rows(xv) * g_ref[...]).astype(_CDT)
            hs[...] = hv
            h_ref[...] = hv
            zga_ref[...] = _dot_nt(hv, wga_ref[...]).astype(_CDT)

        z_ref[...] = _dot_nt(hs[...], wm_ref[...]).astype(_CDT)

    return pl.pallas_call(
        body, name="inproj", grid=(s // tm, ZM // tn),
        in_specs=[pl.BlockSpec((tm, D), lambda i, j: (i, 0)), pl.BlockSpec((1, D), lambda i, j: (0, 0)),
                  pl.BlockSpec((tn, D), lambda i, j: (j, 0)), pl.BlockSpec((GAP, D), lambda i, j: (0, 0))],
        out_specs=[pl.BlockSpec((tm, tn), lambda i, j: (i, j)), pl.BlockSpec((tm, GAP), lambda i, j: (i, 0)),
                   pl.BlockSpec((tm, D), lambda i, j: (i, 0))],
        out_shape=[jax.ShapeDtypeStruct((s, ZM), _CDT), jax.ShapeDtypeStruct((s, GAP), _CDT),
                   jax.ShapeDtypeStruct((s, D), _CDT)],
        scratch_shapes=[pltpu.VMEM((tm, D), _CDT)],
        compiler_params=_params(("parallel", "arbitrary")),
    )(x, g, wm, wga)


def _gla_fwd(z, zga, wa, ba, ggla):
    s = z.shape[0]
    nc = s // CHUNK

    def body(zg_ref, zga_ref, wa_ref, ba_ref, g_ref, cat_ref, la_s, st):
        la_s[...] = _log_sigmoid(_dot(zga_ref[...], wa_ref[...]) + ba_ref[...]) * (1.0 / TAU)
        st[...] = jnp.zeros_like(st)
        tri = _tri(False)

        def step(n, carry):
            rows = pl.ds(pl.multiple_of(n * CHUNK, CHUNK), CHUNK)
            for p in range(HP):
                z0 = p * GW
                la = la_s[rows, p * GDK:(p + 1) * GDK]
                lc = _exact_dot(tri, la)
                lend = jnp.sum(la, axis=0, keepdims=True)
                kdec = (zg_ref[rows, z0 + GDK:z0 + 2 * GDK].astype(F32) * jnp.exp(lend - lc)).astype(_CDT)
                stn = jnp.exp(lend) * st[p] + _dot_tn(zg_ref[rows, z0 + 2 * GDK:z0 + 2 * GDK + GDV], kdec)
                st[p] = stn
                qs = (zg_ref[rows, z0:z0 + GDK].astype(F32) * (GDK ** -0.5)).astype(_CDT)
                o = _dot_nt(qs, stn.astype(_CDT))
                gate = zg_ref[rows, z0 + 2 * GDK + GDV:z0 + GW].astype(F32)
                gain = g_ref[:, p * GDV:(p + 1) * GDV]
                cat_ref[rows, p * GDV:(p + 1) * GDV] = (o * _rms_rows(o) * gain * (gate * _sigmoid(gate))).astype(_CDT)
            return carry

        lax.fori_loop(0, nc, step, 0)

    return pl.pallas_call(
        body, name="gla_fwd", grid=(GH // HP,),
        in_specs=[pl.BlockSpec((s, HP * GW), lambda h: (0, h)), pl.BlockSpec((s, GAP), lambda h: (0, 0)),
                  pl.BlockSpec((GAP, HP * GDK), lambda h: (0, h)), pl.BlockSpec((1, HP * GDK), lambda h: (0, h)),
                  pl.BlockSpec((1, HP * GDV), lambda h: (0, h))],
        out_specs=pl.BlockSpec((s, HP * GDV), lambda h: (0, h)),
        out_shape=jax.ShapeDtypeStruct((s, DGLA), _CDT),
        scratch_shapes=[pltpu.VMEM((s, HP * GDK), F32), pltpu.VMEM((HP, GDV, GDK), F32)],
        compiler_params=_params(("arbitrary",)),
    )(z, zga, wa, ba, ggla)


def _band_bias(b0):
    row = lax.broadcasted_iota(jnp.int32, (QB, 256), 0)
    col = lax.broadcasted_iota(jnp.int32, (QB, 256), 1)
    lane = lax.broadcasted_iota(jnp.int32, (1, 256), 1)
    c0 = jnp.sum(jnp.where(lane == 0, b0, 0.0), axis=1, keepdims=True)
    xv = jnp.broadcast_to(b0, (QB, 256))
    for bit in range(7):
        xv = jnp.where(((row >> bit) & 1) == 1, pltpu.roll(xv, 1 << bit, 1), xv)
    xv = jnp.where(col < row, c0, xv)
    return jnp.concatenate([jnp.broadcast_to(c0, (QB, BANDW - 256)), xv], axis=1)


def _band_static_mask():
    row = lax.broadcasted_iota(jnp.int32, (QB, BANDW), 0) >> 6
    col = lax.broadcasted_iota(jnp.int32, (QB, BANDW), 1) >> 6
    return (col >= row) & (col <= row + LEFT)


def _fold_bias_grad(t):
    row = lax.broadcasted_iota(jnp.int32, (QB, 256), 0)
    col = lax.broadcasted_iota(jnp.int32, (QB, 256), 1)
    xv = t[:, BANDW - 256:]
    low = col < row
    far = jnp.sum(t[:, 0:BANDW - 256], axis=1, keepdims=True) + jnp.sum(jnp.where(low, xv, 0.0), axis=1, keepdims=True)
    far = jnp.sum(far, axis=0, keepdims=True)
    xv = jnp.where(low, 0.0, xv)
    for bit in range(7):
        xv = jnp.where(((row >> bit) & 1) == 1, pltpu.roll(xv, 256 - (1 << bit), 1), xv)
    dp = jnp.sum(xv, axis=0, keepdims=True)
    lane = lax.broadcasted_iota(jnp.int32, (1, 256), 1)
    return dp + jnp.where(lane == 0, far, 0.0)


def _att_fwd(z, rbx, gatt):
    s = z.shape[0]
    nb = s // QB

    def body(za_ref, rb_ref, g_ref, cat_ref, o_ref, lse_ref, kp, vp, bias_s):
        for p in range(HP):
            z0 = p * AW
            kp[p, 0:PADK, :] = jnp.zeros((PADK, AHD), _CDT)
            vp[p, 0:PADK, :] = jnp.zeros((PADK, AHD), _CDT)
            kp[p, PADK:, :] = za_ref[:, z0 + AHD:z0 + 2 * AHD]
            vp[p, PADK:, :] = za_ref[:, z0 + 2 * AHD:z0 + 3 * AHD]
            bias_s[p] = jnp.where(_band_static_mask(), _band_bias(rb_ref[p]), NEG)

        def step(b, carry):
            r0 = pl.multiple_of(b * QB, QB)
            rows = pl.ds(r0, QB)
            band = pl.ds(r0, BANDW)
            live = lax.broadcasted_iota(jnp.int32, (QB, BANDW), 1) >= PADK - r0
            for p in range(HP):
                z0 = p * AW
                cols = slice(p * AHD, (p + 1) * AHD)
                sc = _dot_nt(za_ref[rows, z0:z0 + AHD], kp[p, band, :]) * (AHD ** -0.5) + bias_s[p]
                sc = jnp.where(live, sc, NEG)
                m = jnp.max(sc, axis=-1, keepdims=True)
                pr = jnp.exp(sc - m)
                l = jnp.sum(pr, axis=-1, keepdims=True)
                o = _dot((pr * (1.0 / l)).astype(_CDT), vp[p, band, :])
                o_ref[rows, cols] = o.astype(_CDT)
                lse_ref[rows, cols] = jnp.broadcast_to(m + jnp.log(l), (QB, AHD))
                gate = za_ref[rows, z0 + 3 * AHD:z0 + AW].astype(F32)
                cat_ref[rows, cols] = (o * _rms_rows(o) * g_ref[:, cols] * (gate * _sigmoid(gate))).astype(_CDT)
            return carry

        lax.fori_loop(0, nb, step, 0)

    return pl.pallas_call(
        body, name="att_fwd", grid=(AH // HP,),
        in_specs=[pl.BlockSpec((s, HP * AW), lambda h: (0, ZG // (HP * AW) + h)), pl.BlockSpec((HP, 1, 256), lambda h: (h, 0, 0)),
                  pl.BlockSpec((1, HP * AHD), lambda h: (0, h))],
        out_specs=[pl.BlockSpec((s, HP * AHD), lambda h: (0, h)), pl.BlockSpec((s, HP * AHD), lambda h: (0, h)),
                   pl.BlockSpec((s, HP * AHD), lambda h: (0, h))],
        out_shape=[jax.ShapeDtypeStruct((s, DATT), _CDT), jax.ShapeDtypeStruct((s, DATT), _CDT),
                   jax.ShapeDtypeStruct((s, DATT), F32)],
        scratch_shapes=[pltpu.VMEM((HP, s + PADK, AHD), _CDT), pltpu.VMEM((HP, s + PADK, AHD), _CDT),
                        pltpu.VMEM((HP, QB, BANDW), F32)],
        compiler_params=_params(("arbitrary",)),
    )(z, rbx, gatt)


def _outproj(cg, ca, wout, x, gpost, tm=256):
    s = x.shape[0]

    def body(cg_ref, ca_ref, w_ref, x_ref, g_ref, y_ref, xo_ref):
        y = _dot(cg_ref[...], w_ref[0:DGLA, :]) + _dot(ca_ref[...], w_ref[DGLA:, :])
        y_ref[...] = y
        xo_ref[...] = x_ref[...] + y * _rms_rows(y) * g_ref[...]

    return pl.pallas_call(
        body, name="outproj", grid=(s // tm,),
        in_specs=[pl.BlockSpec((tm, DGLA), lambda i: (i, 0)), pl.BlockSpec((tm, DATT), lambda i: (i, 0)),
                  pl.BlockSpec((D, D), lambda i: (0, 0)), pl.BlockSpec((tm, D), lambda i: (i, 0)),
                  pl.BlockSpec((1, D), lambda i: (0, 0))],
        out_specs=[pl.BlockSpec((tm, D), lambda i: (i, 0)), pl.BlockSpec((tm, D), lambda i: (i, 0))],
        out_shape=[jax.ShapeDtypeStruct((s, D), F32), jax.ShapeDtypeStruct((s, D), F32)],
        compiler_params=_params(("parallel",)),
    )(cg, ca, wout, x, gpost)


def _loss_grad(xo, tgt, tm=256):
    s = xo.shape[0]

    def body(xo_ref, t_ref, d_ref, l_ref):
        @pl.when(pl.program_id(0) == 0)
        def _():
            l_ref[...] = jnp.zeros_like(l_ref)

        e = xo_ref[...] - t_ref[...]
        d_ref[...] = e * (1.0 / D)
        l_ref[...] += jnp.sum(jnp.sum(e * e, axis=1, keepdims=True), axis=0, keepdims=True) * (0.5 / D)

    return pl.pallas_call(
        body, name="loss_grad", grid=(s // tm,),
        in_specs=[pl.BlockSpec((tm, D), lambda i: (i, 0)), pl.BlockSpec((tm, D), lambda i: (i, 0))],
        out_specs=[pl.BlockSpec((tm, D), lambda i: (i, 0)), pl.BlockSpec((1, 1), lambda i: (0, 0))],
        out_shape=[jax.ShapeDtypeStruct((s, D), F32), jax.ShapeDtypeStruct((1, 1), F32)],
        compiler_params=_params(("arbitrary",)),
    )(xo, tgt)


def _post_bwd(dout, y, gpost, wout, tm=256):
    s = y.shape[0]

    def body(d_ref, y_ref, g_ref, w_ref, dy_ref, dcg_ref, dca_ref, dg_ref):
        @pl.when(pl.program_id(0) == 0)
        def _():
            dg_ref[...] = jnp.zeros_like(dg_ref)

        yv = y_ref[...]
        r = _rms_rows(yv)
        yh = yv * r
        dv = d_ref[...]
        dg_ref[...] += jnp.sum(dv * yh, axis=0, keepdims=True)
        dn = dv * g_ref[...]
        dyb = (r * (dn - yh * jnp.mean(dn * yh, axis=-1, keepdims=True))).astype(_CDT)
        dy_ref[...] = dyb
        dcg_ref[...] = _dot_nt(dyb, w_ref[0:DGLA, :]).astype(_CDT)
        dca_ref[...] = _dot_nt(dyb, w_ref[DGLA:, :]).astype(_CDT)

    return pl.pallas_call(
        body, name="post_bwd", grid=(s // tm,),
        in_specs=[pl.BlockSpec((tm, D), lambda i: (i, 0)), pl.BlockSpec((tm, D), lambda i: (i, 0)),
                  pl.BlockSpec((1, D), lambda i: (0, 0)), pl.BlockSpec((D, D), lambda i: (0, 0))],
        out_specs=[pl.BlockSpec((tm, D), lambda i: (i, 0)), pl.BlockSpec((tm, DGLA), lambda i: (i, 0)),
                   pl.BlockSpec((tm, DATT), lambda i: (i, 0)), pl.BlockSpec((1, D), lambda i: (0, 0))],
        out_shape=[jax.ShapeDtypeStruct((s, D), _CDT), jax.ShapeDtypeStruct((s, DGLA), _CDT),
                   jax.ShapeDtypeStruct((s, DATT), _CDT), jax.ShapeDtypeStruct((1, D), F32)],
        compiler_params=_params(("arbitrary",)),
    )(dout, y, gpost, wout)


def _matmul_tn(a, b, out_dtype, tm, tn, name):
    k, m = a.shape
    n = b.shape[1]

    def body(a_ref, b_ref, o_ref):
        o_ref[...] = _dot_tn(a_ref[...], b_ref[...]).astype(out_dtype)

    return pl.pallas_call(
        body, name=name, grid=(m // tm, n // tn),
        in_specs=[pl.BlockSpec((k, tm), lambda i, j: (0, i)), pl.BlockSpec((k, tn), lambda i, j: (0, j))],
        out_specs=pl.BlockSpec((tm, tn), lambda i, j: (i, j)),
        out_shape=jax.ShapeDtypeStruct((m, n), out_dtype),
        compiler_params=_params(("parallel", "parallel")),
    )(a, b)


def _att_bwd(z, oraw, lse, dca, rbx, gatt):
    s = z.shape[0]
    nb = s // QB

    def body(za_ref, o_ref, lse_ref, dc_ref, rb_ref, g_ref, dz_ref, dg_ref, db_ref, kp, vp, dkp, dvp, bias_s, t_s, dg_s):
        for p in range(HP):
            z0 = p * AW
            kp[p, 0:PADK, :] = jnp.zeros((PADK, AHD), _CDT)
            vp[p, 0:PADK, :] = jnp.zeros((PADK, AHD), _CDT)
            kp[p, PADK:, :] = za_ref[:, z0 + AHD:z0 + 2 * AHD]
            vp[p, PADK:, :] = za_ref[:, z0 + 2 * AHD:z0 + 3 * AHD]
            bias_s[p] = jnp.where(_band_static_mask(), _band_bias(rb_ref[p]), NEG)
        dkp[...] = jnp.zeros_like(dkp)
        dvp[...] = jnp.zeros_like(dvp)
        t_s[...] = jnp.zeros_like(t_s)
        dg_s[...] = jnp.zeros_like(dg_s)

        def step(b, carry):
            r0 = pl.multiple_of(b * QB, QB)
            rows = pl.ds(r0, QB)
            band = pl.ds(r0, BANDW)
            live = lax.broadcasted_iota(jnp.int32, (QB, BANDW), 1) >= PADK - r0
            for p in range(HP):
                z0 = p * AW
                cols = slice(p * AHD, (p + 1) * AHD)
                o = o_ref[rows, cols].astype(F32)
                do, dgate, dg = _norm_gate_bwd(o, g_ref[:, cols], za_ref[rows, z0 + 3 * AHD:z0 + AW].astype(F32),
                                               dc_ref[rows, cols].astype(F32))
                dg_s[:, cols] += dg
                q = za_ref[rows, z0:z0 + AHD]
                kb = kp[p, band, :]
                sc = _dot_nt(q, kb) * (AHD ** -0.5) + bias_s[p]
                sc = jnp.where(live, sc, NEG)
                pr = jnp.exp(sc - jnp.max(lse_ref[rows, cols], axis=-1, keepdims=True))
                dob = do.astype(_CDT)
                dp = _dot_nt(dob, vp[p, band, :])
                ds = pr * (dp - jnp.sum(do * o, axis=-1, keepdims=True))
                t_s[p] += ds
                dsb = (ds * (AHD ** -0.5)).astype(_CDT)
                dz_ref[rows, z0:z0 + AHD] = _dot(dsb, kb).astype(_CDT)
                dz_ref[rows, z0 + 3 * AHD:z0 + AW] = dgate.astype(_CDT)
                dkp[p, band, :] += _dot_tn(dsb, q)
                dvp[p, band, :] += _dot_tn(pr.astype(_CDT), dob)
            return carry

        lax.fori_loop(0, nb, step, 0)
        for p in range(HP):
            z0 = p * AW
            dz_ref[:, z0 + AHD:z0 + 2 * AHD] = dkp[p, PADK:, :].astype(_CDT)
            dz_ref[:, z0 + 2 * AHD:z0 + 3 * AHD] = dvp[p, PADK:, :].astype(_CDT)
            db_ref[p] = _fold_bias_grad(t_s[p])
        dg_ref[...] = dg_s[...]

    return pl.pallas_call(
        body, name="att_bwd", grid=(AH // HP,),
        in_specs=[pl.BlockSpec((s, HP * AW), lambda h: (0, ZG // (HP * AW) + h)), pl.BlockSpec((s, HP * AHD), lambda h: (0, h)),
                  pl.BlockSpec((s, HP * AHD), lambda h: (0, h)), pl.BlockSpec((s, HP * AHD), lambda h: (0, h)),
                  pl.BlockSpec((HP, 1, 256), lambda h: (h, 0, 0)), pl.BlockSpec((1, HP * AHD), lambda h: (0, h))],
        out_specs=[pl.BlockSpec((s, HP * AW), lambda h: (0, h)), pl.BlockSpec((1, HP * AHD), lambda h: (0, h)),
                   pl.BlockSpec((HP, 1, 256), lambda h: (h, 0, 0))],
        out_shape=[jax.ShapeDtypeStruct((s, ZA), _CDT), jax.ShapeDtypeStruct((1, DATT), F32),
                   jax.ShapeDtypeStruct((AH, 1, 256), F32)],
        scratch_shapes=[pltpu.VMEM((HP, s + PADK, AHD), _CDT), pltpu.VMEM((HP, s + PADK, AHD), _CDT),
                        pltpu.VMEM((HP, s + PADK, AHD), F32), pltpu.VMEM((HP, s + PADK, AHD), F32),
                        pltpu.VMEM((HP, QB, BANDW), F32), pltpu.VMEM((HP, QB, BANDW), F32), pltpu.VMEM((1, HP * AHD), F32)],
        compiler_params=_params(("arbitrary",)),
    )(z, oraw, lse, dca, rbx, gatt)


def _gla_bwd(z, zga, wa, ba, ggla, dcg):
    s = z.shape[0]
    nc = s // CHUNK

    def body(zg_ref, zga_ref, wa_ref, ba_ref, g_ref, dc_ref, dz_ref, dga_ref, dwa_ref, dba_ref, dg_ref,
             la_s, om_s, sall, dpre_s, c_s, dga_s, dg_s):
        h = pl.program_id(0)
        pre = _dot(zga_ref[...], wa_ref[...]) + ba_ref[...]
        la_s[...] = _log_sigmoid(pre) * (1.0 / TAU)
        om_s[...] = (1.0 - _sigmoid(pre)) * (1.0 / TAU)
        c_s[...] = jnp.zeros_like(c_s)
        dg_s[...] = jnp.zeros_like(dg_s)
        tri = _tri(False)
        tri_strict = _tri(True)

        def decay(rows, p):
            la = la_s[rows, p * GDK:(p + 1) * GDK]
            lend = jnp.sum(la, axis=0, keepdims=True)
            return jnp.exp(lend - _exact_dot(tri, la)), jnp.exp(lend)

        def fwd(n, sts):
            rows = pl.ds(pl.multiple_of(n * CHUNK, CHUNK), CHUNK)
            out = []
            for p in range(HP):
                z0 = p * GW
                dec, a = decay(rows, p)
                kdec = (zg_ref[rows, z0 + GDK:z0 + 2 * GDK].astype(F32) * dec).astype(_CDT)
                stn = a * sts[p] + _dot_tn(zg_ref[rows, z0 + 2 * GDK:z0 + 2 * GDK + GDV], kdec)
                sall[p, n] = stn
                out.append(stn)
            return tuple(out)

        lax.fori_loop(0, nc, fwd, tuple(jnp.zeros((GDV, GDK), F32) for _ in range(HP)))

        def bwd(i, carry):
            n = nc - 1 - i
            rows = pl.ds(pl.multiple_of(n * CHUNK, CHUNK), CHUNK)
            for p in range(HP):
                z0 = p * GW
                kc = slice(p * GDK, (p + 1) * GDK)
                vc = slice(p * GDV, (p + 1) * GDV)
                dec, a = decay(rows, p)
                kdec = zg_ref[rows, z0 + GDK:z0 + 2 * GDK].astype(F32) * dec
                kdb = kdec.astype(_CDT)
                v = zg_ref[rows, z0 + 2 * GDK:z0 + 2 * GDK + GDV]
                qs = (zg_ref[rows, z0:z0 + GDK].astype(F32) * (GDK ** -0.5)).astype(_CDT)
                stb = sall[p, n].astype(_CDT)
                st_prev = sall[p, jnp.maximum(n - 1, 0)] * jnp.where(n > 0, 1.0, 0.0)
                o = _dot_nt(qs, stb)
                do, dgate, dg = _norm_gate_bwd(o, g_ref[:, vc], zg_ref[rows, z0 + 2 * GDK + GDV:z0 + GW].astype(F32),
                                               dc_ref[rows, vc].astype(F32))
                dg_s[:, vc] += dg
                dob = do.astype(_CDT)
                gt = _dot_tn(dob, qs) + c_s[p]
                gtb = gt.astype(_CDT)
                da = jnp.sum(gt * st_prev, axis=0, keepdims=True)
                dkdec = _dot(v, gtb)
                dla = _exact_dot(tri_strict, dkdec * kdec) + da * a
                dpre_s[rows, kc] = dla * om_s[rows, kc]
                dz_ref[rows, z0:z0 + GDK] = (_dot(dob, stb) * (GDK ** -0.5)).astype(_CDT)
                dz_ref[rows, z0 + GDK:z0 + 2 * GDK] = (dkdec * dec).astype(_CDT)
                dz_ref[rows, z0 + 2 * GDK:z0 + 2 * GDK + GDV] = _dot_nt(kdb, gtb).astype(_CDT)
                dz_ref[rows, z0 + 2 * GDK + GDV:z0 + GW] = dgate.astype(_CDT)
                c_s[p] = a * gt
            return carry

        lax.fori_loop(0, nc, bwd, 0)
        dpre = dpre_s[...]
        dpb = dpre.astype(_CDT)
        dg_ref[...] = dg_s[...]
        dba_ref[...] = jnp.sum(dpre, axis=0, keepdims=True)
        dwa_ref[...] = _dot_tn(zga_ref[...], dpb)
        part = _dot_nt(dpb, wa_ref[...])

        @pl.when(h == 0)
        def _():
            dga_s[...] = part

        @pl.when(h > 0)
        def _():
            dga_s[...] += part

        @pl.when(h == GH // HP - 1)
        def _():
            dga_ref[...] = dga_s[...].astype(_CDT)

    return pl.pallas_call(
        body, name="gla_bwd", grid=(GH // HP,),
        in_specs=[pl.BlockSpec((s, HP * GW), lambda h: (0, h)), pl.BlockSpec((s, GAP), lambda h: (0, 0)),
                  pl.BlockSpec((GAP, HP * GDK), lambda h: (0, h)), pl.BlockSpec((1, HP * GDK), lambda h: (0, h)),
                  pl.BlockSpec((1, HP * GDV), lambda h: (0, h)), pl.BlockSpec((s, HP * GDV), lambda h: (0, h))],
        out_specs=[pl.BlockSpec((s, HP * GW), lambda h: (0, h)), pl.BlockSpec((s, GAP), lambda h: (0, 0)),
                   pl.BlockSpec((GAP, HP * GDK), lambda h: (0, h)), pl.BlockSpec((1, HP * GDK), lambda h: (0, h)),
                   pl.BlockSpec((1, HP * GDV), lambda h: (0, h))],
        out_shape=[jax.ShapeDtypeStruct((s, ZG), _CDT), jax.ShapeDtypeStruct((s, GAP), _CDT),
                   jax.ShapeDtypeStruct((GAP, GH * GDK), F32), jax.ShapeDtypeStruct((1, GH * GDK), F32),
                   jax.ShapeDtypeStruct((1, DGLA), F32)],
        scratch_shapes=[pltpu.VMEM((s, HP * GDK), F32), pltpu.VMEM((s, HP * GDK), F32), pltpu.VMEM((HP, nc, GDV, GDK), F32),
                        pltpu.VMEM((s, HP * GDK), F32), pltpu.VMEM((HP, GDV, GDK), F32), pltpu.VMEM((s, GAP), F32),
                        pltpu.VMEM((1, HP * GDV), F32)],
        compiler_params=_params(("arbitrary",)),
    )(z, zga, wa, ba, ggla, dcg)


def _dh(dzg, dza, dga, wm, wga, x, dout, gpre, tm=512, tk=1024):
    s = x.shape[0]
    nkg, nk = ZG // tk, ZM // tk

    def body(dzg_ref, dza_ref, dga_ref, wm_ref, wga_ref, x_ref, d_ref, g_ref, dx_ref, dg_ref, acc):
        i, k = pl.program_id(0), pl.program_id(1)

        @pl.when((i == 0) & (k == 0))
        def _():
            dg_ref[...] = jnp.zeros_like(dg_ref)

        @pl.when(k == 0)
        def _():
            acc[...] = _dot(dga_ref[...], wga_ref[...])

        @pl.when(k < nkg)
        def _():
            acc[...] += _dot(dzg_ref[...], wm_ref[...])

        @pl.when(k >= nkg)
        def _():
            acc[...] += _dot(dza_ref[...], wm_ref[...])

        @pl.when(k == nk - 1)
        def _():
            xv = x_ref[...]
            r = _rms_rows(xv)
            xh = xv * r
            dh = acc[...]
            dg_ref[...] += jnp.sum(dh * xh, axis=0, keepdims=True)
            dn = dh * g_ref[...]
            dx_ref[...] = d_ref[...] + r * (dn - xh * jnp.mean(dn * xh, axis=-1, keepdims=True))

    return pl.pallas_call(
        body, name="dh", grid=(s // tm, nk),
        in_specs=[pl.BlockSpec((tm, tk), lambda i, k: (i, jnp.minimum(k, nkg - 1))),
                  pl.BlockSpec((tm, tk), lambda i, k: (i, jnp.maximum(k - nkg, 0))),
                  pl.BlockSpec((tm, GAP), lambda i, k: (i, 0)), pl.BlockSpec((tk, D), lambda i, k: (k, 0)),
                  pl.BlockSpec((GAP, D), lambda i, k: (0, 0)), pl.BlockSpec((tm, D), lambda i, k: (i, 0)),
                  pl.BlockSpec((tm, D), lambda i, k: (i, 0)), pl.BlockSpec((1, D), lambda i, k: (0, 0))],
        out_specs=[pl.BlockSpec((tm, D), lambda i, k: (i, 0)), pl.BlockSpec((1, D), lambda i, k: (0, 0))],
        out_shape=[jax.ShapeDtypeStruct((s, D), F32), jax.ShapeDtypeStruct((1, D), F32)],
        scratch_shapes=[pltpu.VMEM((tm, D), F32)],
        compiler_params=_params(("arbitrary", "arbitrary")),
    )(dzg, dza, dga, wm, wga, x, dout, gpre)


def _adam(w, g, m, v, tr, name):
    rws, cols = w.shape

    def body(w_ref, g_ref, m_ref, v_ref, d_ref, mo_ref, vo_ref):
        gv = g_ref[...]
        mn = ADAM_B1 * m_ref[...] + (1.0 - ADAM_B1) * gv
        vn = ADAM_B2 * v_ref[...] + (1.0 - ADAM_B2) * (gv * gv)
        mh = mn / (1.0 - ADAM_B1 ** ADAM_STEP)
        vh = vn / (1.0 - ADAM_B2 ** ADAM_STEP)
        d_ref[...] = -ADAM_LR * (mh / (jnp.sqrt(vh) + ADAM_EPS) + ADAM_WD * w_ref[...])
        mo_ref[...] = mn
        vo_ref[...] = vn

    spec = pl.BlockSpec((tr, cols), lambda i: (i, 0))
    return pl.pallas_call(
        body, name=name, grid=(rws // tr,), in_specs=[spec] * 4, out_specs=[spec] * 3,
        out_shape=[jax.ShapeDtypeStruct((rws, cols), F32)] * 3,
        compiler_params=_params(("parallel",)),
    )(w, g, m, v)


def _adam_rows(w, g, m, v, tj, name):
    rows = w.shape[0]

    def body(w_ref, g_ref, m_ref, v_ref, d_ref, mo_ref, vo_ref):
        gv = g_ref[...]
        mn = ADAM_B1 * m_ref[...] + (1.0 - ADAM_B1) * gv
        vn = ADAM_B2 * v_ref[...] + (1.0 - ADAM_B2) * (gv * gv)
        mh = mn / (1.0 - ADAM_B1 ** ADAM_STEP)
        vh = vn / (1.0 - ADAM_B2 ** ADAM_STEP)
        d_ref[...] = -ADAM_LR * (mh / (jnp.sqrt(vh) + ADAM_EPS) + ADAM_WD * w_ref[...])
        mo_ref[...] = mn
        vo_ref[...] = vn

    spec = pl.BlockSpec((tj,) + w.shape[1:], lambda i: (i, 0, 0))
    return pl.pallas_call(
        body, name=name, grid=(pl.cdiv(rows, tj),), in_specs=[spec] * 4, out_specs=[spec] * 3,
        out_shape=[jax.ShapeDtypeStruct(w.shape, F32)] * 3,
        compiler_params=_params(("parallel",)),
    )(w, g, m, v)


def _place():
    x, y, c = lax.axis_index("x"), lax.axis_index("y"), lax.axis_index("c")
    chips = [(1 - x, y), (x, 1 - y), (1 - x, 1 - y)]
    return x, y, c, chips


def _row_chunks(n, parts):
    base, extra = divmod(n, parts)
    out, r = [], 0
    for i in range(parts):
        size = base + (1 if i < extra else 0)
        out.append((r, size))
        r += size
    return out


def _gather_weights(wt, wout, wa, wt_all, wout_all, wa_all):
    nl = wout.shape[0]
    chunks = _row_chunks(wt.shape[2], nl)

    def variant(x, y, c, wt_ref, wout_ref, wa_ref, _wt, _wout, _wa, wt_all, wout_all, wa_all, send_sems, recv_sems, wa_send, wa_recv):
        me = 2 * x + y
        sib = (x, y, 1 - c)
        xn, yn, dg = (1 - x, y), (x, 1 - y), (1 - x, 1 - y)

        def wt_rows(ref, k):
            return ref.at[:, pl.ds(chunks[k][0], chunks[k][1])]

        arrays = [(lambda k: wt_rows(wt_ref.at[c], k), lambda chip, k: wt_rows(wt_all.at[chip, c], k)),
                  (lambda k: wout_ref.at[k, c], lambda chip, k: wout_all.at[k, chip, c])]

        def cid(p):
            return 2 * p[0] + p[1]

        def cp(src, dst, a, l, k, to):
            return pltpu.make_async_remote_copy(src_ref=src, dst_ref=dst, send_sem=send_sems.at[a, l, k],
                                                recv_sem=recv_sems.at[a, l, k], device_id=to, device_id_type=MESH)

        def wa_cp(k, chip, to):
            return pltpu.make_async_remote_copy(src_ref=wa_ref, dst_ref=wa_all.at[chip], send_sem=wa_send.at[k],
                                                recv_sem=wa_recv.at[k], device_id=to, device_id_type=MESH)

        sent = []

        def go(cpy):
            cpy.start()
            sent.append(cpy)

        for l in range(nl):
            for a, (src, place) in enumerate(arrays):
                go(cp(src(l), place(me, l), a, l, 0, (xn[0], xn[1], c)))
                go(cp(src(l), place(me, l), a, l, 1, (yn[0], yn[1], c)))
        for k, p in enumerate((xn, yn, dg)):
            go(wa_cp(k, me, (p[0], p[1], c)))
        for l in range(nl):
            for a, (src, place) in enumerate(arrays):
                got = place(cid(xn), l)
                cp(got, got, a, l, 0, sib).wait_recv()
                go(cp(got.at[0], got.at[0], a, l, 2, (yn[0], yn[1], c)))
                go(cp(got, got, a, l, 4, sib))
                got = place(cid(yn), l)
                cp(got, got, a, l, 1, sib).wait_recv()
                go(cp(got.at[1], got.at[1], a, l, 3, (xn[0], xn[1], c)))
                go(cp(got, got, a, l, 5, sib))
        for l in range(nl):
            for a, (src, place) in enumerate(arrays):
                got = place(cid(dg), l)
                cp(got.at[0], got.at[0], a, l, 2, sib).wait_recv()
                cp(got.at[1], got.at[1], a, l, 3, sib).wait_recv()
                go(cp(got, got, a, l, 6, sib))
        for k, p in enumerate((xn, yn, dg)):
            wa_cp(k, cid(p), sib).wait_recv()
        for l in range(nl):
            for a, (src, place) in enumerate(arrays):
                for k in (4, 5, 6):
                    cp(place(me, l), place(me, l), a, l, k, sib).wait_recv()
        for cpy in sent:
            cpy.wait_send()

    def body(*refs):
        x, y, c, _ = _place()
        for jx in range(2):
            for jy in range(2):
                for jc in range(2):
                    pl.when((x == jx) & (y == jy) & (c == jc))(functools.partial(variant, jx, jy, jc, *refs))

    return pl.pallas_call(
        body, name="gather_weights", in_specs=[ANY] * 6, out_specs=[ANY] * 3, input_output_aliases={3: 0, 4: 1, 5: 2},
        out_shape=[jax.ShapeDtypeStruct(t.shape, t.dtype) for t in (wt_all, wout_all, wa_all)],
        scratch_shapes=[pltpu.SemaphoreType.DMA((2, nl, 7)), pltpu.SemaphoreType.DMA((2, nl, 7)),
                        pltpu.SemaphoreType.DMA((3,)), pltpu.SemaphoreType.DMA((3,))],
    )(wt, wout, wa, wt_all, wout_all, wa_all)


def _swap_halves(gin2, gout2):
    def body(gin_ref, gout_ref, rin, rout, send_sems, recv_sems):
        x, y, c, _ = _place()
        sib = (x, y, 1 - c)
        cps = [pltpu.make_async_remote_copy(src_ref=src.at[1 - c], dst_ref=dst, send_sem=send_sems.at[a],
                                            recv_sem=recv_sems.at[a], device_id=sib, device_id_type=MESH)
               for a, (src, dst) in enumerate([(gin_ref, rin), (gout_ref, rout)])]
        for cpy in cps:
            cpy.start()
        for cpy in cps:
            cpy.wait()

    return pl.pallas_call(
        body, name="swap_halves", in_specs=[ANY, ANY], out_specs=[ANY, ANY],
        out_shape=[jax.ShapeDtypeStruct(gin2.shape[1:], gin2.dtype), jax.ShapeDtypeStruct(gout2.shape[1:], gout2.dtype)],
        scratch_shapes=[pltpu.SemaphoreType.DMA((2,)), pltpu.SemaphoreType.DMA((2,))],
    )(gin2, gout2)


def _add_halves(c_idx, g2, r, tr, name):
    rows, cols = r.shape

    def body(c_ref, g_ref, r_ref, o_ref):
        o_ref[...] = (g_ref[0].astype(F32) + r_ref[...].astype(F32)).astype(_XDT)

    return pl.pallas_call(
        body, name=name,
        grid_spec=pltpu.PrefetchScalarGridSpec(
            num_scalar_prefetch=1, grid=(rows // tr,),
            in_specs=[pl.BlockSpec((1, tr, cols), lambda i, c_ref: (c_ref[0], i, 0)),
                      pl.BlockSpec((tr, cols), lambda i, c_ref: (i, 0))],
            out_specs=pl.BlockSpec((tr, cols), lambda i, c_ref: (i, 0))),
        out_shape=jax.ShapeDtypeStruct((rows, cols), _XDT),
        compiler_params=_params(("parallel",)),
    )(c_idx, g2, r)


def _send_to_owners(pin, pout):
    nl = pin.shape[0]

    def body(pin_ref, pout_ref, rin, rout, send_sems, recv_sems):
        x, y, c, chips = _place()
        cps = []
        for k, (px, py) in enumerate(chips):
            chip = 2 * px + py
            for a, (src, dst) in enumerate([(pin_ref, rin), (pout_ref, rout)]):
                cps.append(pltpu.make_async_remote_copy(src_ref=src.at[:, chip], dst_ref=dst.at[k], send_sem=send_sems.at[a, k],
                                                        recv_sem=recv_sems.at[a, k], device_id=(px, py, c), device_id_type=MESH))
        for cpy in cps:
            cpy.start()
        for cpy in cps:
            cpy.wait()

    return pl.pallas_call(
        body, name="send_to_owners", in_specs=[ANY, ANY], out_specs=[ANY, ANY],
        out_shape=[jax.ShapeDtypeStruct((3, nl) + pin.shape[2:], pin.dtype),
                   jax.ShapeDtypeStruct((3, nl) + pout.shape[2:], pout.dtype)],
        scratch_shapes=[pltpu.SemaphoreType.DMA((2, 3)), pltpu.SemaphoreType.DMA((2, 3))],
    )(pin, pout)


def _add_chips(chip_idx, p, r, tr, name):
    nl, _, rows, cols = p.shape

    def body(c_ref, p_ref, r_ref, o_ref):
        o_ref[0] = ((p_ref[0, 0].astype(F32) + r_ref[0, 0].astype(F32)) + r_ref[1, 0].astype(F32)) + r_ref[2, 0].astype(F32)

    return pl.pallas_call(
        body, name=name,
        grid_spec=pltpu.PrefetchScalarGridSpec(
            num_scalar_prefetch=1, grid=(nl, rows // tr),
            in_specs=[pl.BlockSpec((1, 1, tr, cols), lambda l, i, c_ref: (l, c_ref[0], i, 0)),
                      pl.BlockSpec((3, 1, tr, cols), lambda l, i, c_ref: (0, l, i, 0))],
            out_specs=pl.BlockSpec((1, tr, cols), lambda l, i, c_ref: (l, i, 0))),
        out_shape=jax.ShapeDtypeStruct((nl, rows, cols), F32),
        compiler_params=_params(("parallel", "parallel")),
    )(chip_idx, p, r)


def _exchange_halves(hin, hout):
    def body(hin_ref, hout_ref, oin, oout, send_sems, recv_sems):
        x, y, c, _ = _place()
        cps = [pltpu.make_async_remote_copy(src_ref=src, dst_ref=dst, send_sem=send_sems.at[a], recv_sem=recv_sems.at[a],
                                            device_id=(x, y, 1 - c), device_id_type=MESH)
               for a, (src, dst) in enumerate([(hin_ref, oin), (hout_ref, oout)])]
        for cpy in cps:
            cpy.start()
        for cpy in cps:
            cpy.wait()

    return pl.pallas_call(
        body, name="exchange_halves", in_specs=[ANY, ANY], out_specs=[ANY, ANY],
        out_shape=[jax.ShapeDtypeStruct(hin.shape, hin.dtype), jax.ShapeDtypeStruct(hout.shape, hout.dtype)],
        scratch_shapes=[pltpu.SemaphoreType.DMA((2,)), pltpu.SemaphoreType.DMA((2,))],
    )(hin, hout)


def _adam_halves(c_idx, w, g_own, g_other, m, v, tr, name):
    nl, _, rows, cols = w.shape

    def body(c_ref, w_ref, go_ref, gx_ref, m_ref, v_ref, g_ref, d_ref, mo_ref, vo_ref):
        gv = jnp.where(pl.program_id(1) == c_ref[0], go_ref[0], gx_ref[0])
        mn = ADAM_B1 * m_ref[0, 0] + (1.0 - ADAM_B1) * gv
        vn = ADAM_B2 * v_ref[0, 0] + (1.0 - ADAM_B2) * (gv * gv)
        mh = mn / (1.0 - ADAM_B1 ** ADAM_STEP)
        vh = vn / (1.0 - ADAM_B2 ** ADAM_STEP)
        g_ref[0, 0] = gv
        d_ref[0, 0] = -ADAM_LR * (mh / (jnp.sqrt(vh) + ADAM_EPS) + ADAM_WD * w_ref[0, 0])
        mo_ref[0, 0] = mn
        vo_ref[0, 0] = vn

    full = pl.BlockSpec((1, 1, tr, cols), lambda l, hh, i, c_ref: (l, hh, i, 0))
    own = pl.BlockSpec((1, tr, cols), lambda l, hh, i, c_ref: (l, jnp.where(hh == c_ref[0], i, 0), 0))
    other = pl.BlockSpec((1, tr, cols), lambda l, hh, i, c_ref: (l, jnp.where(hh == c_ref[0], 0, i), 0))
    return pl.pallas_call(
        body, name=name,
        grid_spec=pltpu.PrefetchScalarGridSpec(
            num_scalar_prefetch=1, grid=(nl, 2, rows // tr),
            in_specs=[full, own, other, full, full], out_specs=[full] * 4),
        out_shape=[jax.ShapeDtypeStruct(w.shape, F32)] * 4,
        compiler_params=_params(("parallel", "parallel", "parallel")),
    )(c_idx, w, g_own, g_other, m, v)


def _allreduce_small(sg):
    rows = sg.shape[0]
    vm = pl.BlockSpec(memory_space=pltpu.VMEM)

    def body(sg_ref, tot_ref, all_ref, send_sems, recv_sems):
        x, y, c, _ = _place()
        me = 4 * x + 2 * y + c
        all_ref[me] = sg_ref[...]
        cps = []
        for mask in range(1, 8):
            to = (1 - x if mask & 4 else x, 1 - y if mask & 2 else y, 1 - c if mask & 1 else c)
            cps.append(pltpu.make_async_remote_copy(src_ref=sg_ref, dst_ref=all_ref.at[me], send_sem=send_sems.at[mask - 1],
                                                    recv_sem=recv_sems.at[mask - 1], device_id=to, device_id_type=MESH))
        for cpy in cps:
            cpy.start()
        for cpy in cps:
            cpy.wait()
        acc = all_ref[0]
        for d in range(1, 8):
            acc = acc + all_ref[d]
        tot_ref[...] = acc

    return pl.pallas_call(
        body, name="allreduce_small", in_specs=[vm], out_specs=[vm, vm],
        out_shape=[jax.ShapeDtypeStruct((rows, 128), F32), jax.ShapeDtypeStruct((8, rows, 128), F32)],
        scratch_shapes=[pltpu.SemaphoreType.DMA((7,)), pltpu.SemaphoreType.DMA((7,))],
        compiler_params=_params(),
    )(sg)[0]


_CUTS = [0, 512, 1024, 2048, 3072, 3088, 4112, 5136, 6160, 7184]


def _rows_to_internal(w):
    tail = w.shape[1:]
    gq, gk, gv, gg, ga, aq, ak, av, ag = [w[_CUTS[i]:_CUTS[i + 1]] for i in range(9)]
    g = jnp.concatenate([gq.reshape((GH, GDK) + tail), gk.reshape((GH, GDK) + tail),
                         gv.reshape((GH, GDV) + tail), gg.reshape((GH, GDV) + tail)], axis=1).reshape((ZG,) + tail)
    a = jnp.concatenate([t.reshape((AH, AHD) + tail) for t in (aq, ak, av, ag)], axis=1).reshape((ZA,) + tail)
    pad = [(0, GAP - RANK)] + [(0, 0)] * len(tail)
    return jnp.concatenate([g, a], axis=0), jnp.pad(ga, pad)


def _rows_from_internal(g, a, ga):
    tail = g.shape[1:]
    g = g.reshape((GH, GW) + tail)
    a = a.reshape((AH, AW) + tail)
    parts = [g[:, 0:GDK], g[:, GDK:2 * GDK], g[:, 2 * GDK:2 * GDK + GDV], g[:, 2 * GDK + GDV:GW]]
    parts = [t.reshape((-1,) + tail) for t in parts] + [ga[0:RANK]]
    parts += [a[:, i * AHD:(i + 1) * AHD].reshape((-1,) + tail) for i in range(4)]
    return jnp.concatenate(parts, axis=0)


def _slab_lo(chip):
    return min(SHARD * chip // 16 * 16, DIN - SLAB)


def _pack_rows(parts):
    rows = []
    for t in parts:
        flat = t.reshape(-1)
        rows.append(jnp.pad(flat, (0, (-flat.shape[0]) % 128)).reshape(-1, 128))
    buf = jnp.concatenate(rows, axis=0)
    return jnp.pad(buf, ((0, (-buf.shape[0]) % 8), (0, 0)))


def _unpack_rows(buf, shapes):
    out, r = [], 0
    for shp in shapes:
        n = 1
        for d in shp:
            n *= d
        nr = -(-n // 128)
        out.append(buf[r:r + nr].reshape(-1)[:n].reshape(shp))
        r += nr
    return out


def _layer_fwd(x, wm, wga, wout, gpre, gpost, wa, ba, ggla, gatt, rbx):
    z, zga, h = _inproj(x, gpre, wm, wga)
    cg = _gla_fwd(z, zga, wa, ba, ggla)
    ca, oraw, lse = _att_fwd(z, rbx, gatt)
    y, xo = _outproj(cg, ca, wout, x, gpost)
    return xo, (x, z, zga, h, cg, ca, oraw, lse, y)


def _layer_bwd(dout, saved, wm, wga, wout, gpre, gpost, wa, ba, ggla, gatt, rbx):
    x, z, zga, h, cg, ca, oraw, lse, y = saved
    dy, dcg, dca, dgpost = _post_bwd(dout, y, gpost, wout)
    dwout = jnp.concatenate([_matmul_tn(cg, dy, _XDT, 512, 1024, "dwout_gla"),
                             _matmul_tn(ca, dy, _XDT, 512, 1024, "dwout_att")], axis=0)
    dza, dgatt, dbx = _att_bwd(z, oraw, lse, dca, rbx, gatt)
    dzg, dga, dwa, dba, dggla = _gla_bwd(z, zga, wa, ba, ggla, dcg)
    dx, dgpre = _dh(dzg, dza, dga, wm, wga, x, dout, gpre)
    dwin = (_matmul_tn(dzg, h, _XDT, 512, 1024, "dwin_gla"), _matmul_tn(dza, h, _XDT, 512, 1024, "dwin_att"),
            _matmul_tn(dga, h, _XDT, GAP, 1024, "dwin_gate"))
    drb = jnp.concatenate([jnp.zeros((AH, 1), F32), dbx[:, 0, ::-1]], axis=1)
    return dx, dwin, dwout, (dgpre[0], dgpost[0], dwa[0:RANK], dba[0], dggla[0], dgatt[0], drb)


def _rel_rows(rb):
    return rb[:, :0:-1][:, None, :]


def kernel(x, w_in, w_out, g_pre, g_post, w_alpha, b_alpha, g_gla, g_att, rel_bias, loss_target, m_w_in, m_w_out, m_g_pre, m_g_post, m_w_alpha, m_b_alpha, m_g_gla, m_g_att, m_rel_bias, v_w_in, v_w_out, v_g_pre, v_g_post, v_w_alpha, v_b_alpha, v_g_gla, v_g_att, v_rel_bias):
    nl = w_in.shape[0]
    ax, ay, ac = lax.axis_index("x"), lax.axis_index("y"), lax.axis_index("c")
    chip = 2 * ax + ay
    c_idx = jnp.reshape(ac, (1,)).astype(jnp.int32)
    chip_idx = jnp.reshape(chip, (1,)).astype(jnp.int32)

    wt_own = jnp.transpose(w_in, (2, 0, 1)).astype(_CDT).reshape(2, 2, SHARD // 4, nl, D)
    wout_own = w_out.astype(_CDT).reshape(nl, 2, 2, D // NCHIP // 4, D)

    def with_own(own, axis):
        shape = own.shape[:axis] + (NCHIP,) + own.shape[axis:]
        start = [0] * axis + [chip] + [0] * (own.ndim - axis)
        return lax.dynamic_update_slice(lax.empty(shape, own.dtype), jnp.expand_dims(own, axis), start)

    wt_all, wout_all, wa_all = _gather_weights(wt_own, wout_own, w_alpha, with_own(wt_own, 0), with_own(wout_own, 1),
                                               with_own(w_alpha, 0))
    wt_all = wt_all.reshape(DIN, nl, D)
    wout_all = wout_all.reshape(nl, D, D)
    wm, wga = zip(*[_rows_to_internal(wt_all[:, l]) for l in range(nl)])
    wa_full = jnp.transpose(wa_all, (1, 2, 0, 3)).reshape(nl, RANK, GH * GDK)
    wa_pad = jnp.pad(wa_full, ((0, 0), (0, GAP - RANK), (0, 0))).astype(_CDT)
    rbx = [_rel_rows(rel_bias[l]) for l in range(nl)]

    def weights(l):
        return (wm[l], wga[l], wout_all[l], g_pre[l][None], g_post[l][None], wa_pad[l], b_alpha[l][None],
                g_gla[l][None], g_att[l][None], rbx[l])

    h = x[0]
    saved = []
    for l in range(nl):
        h, sv = _layer_fwd(h, *weights(l))
        saved.append(sv)
    dout, loss_part = _loss_grad(h, loss_target[0])

    dwin, dwout, small = [None] * nl, [None] * nl, [None] * nl
    for l in reversed(range(nl)):
        dout, dwin[l], dwout[l], small[l] = _layer_bwd(dout, saved[l], *weights(l))
    grad_x = dout[None]

    gt = jnp.stack([_rows_from_internal(*dwin[l]) for l in range(nl)])
    slabs = jnp.stack([gt[:, _slab_lo(i):_slab_lo(i) + SLAB] for i in range(NCHIP)], axis=1)
    gin2 = jnp.transpose(slabs.reshape(nl, NCHIP, 2, HSLAB, D), (2, 0, 1, 3, 4)).reshape(2, nl * NCHIP * HSLAB, D)
    gout = jnp.stack(dwout).reshape(nl, NCHIP, 2, D // NCHIP // 2, D)
    gout2 = jnp.transpose(gout, (2, 0, 1, 3, 4)).reshape(2, nl * NCHIP * (D // NCHIP // 2), D)
    rin, rout = _swap_halves(gin2, gout2)
    pin = _add_halves(c_idx, gin2, rin, 256, "add_halves_in").reshape(nl, NCHIP, HSLAB, D)
    pout = _add_halves(c_idx, gout2, rout, 256, "add_halves_out").reshape(nl, NCHIP, D // NCHIP // 2, D)
    bin_, bout = _send_to_owners(pin, pout)
    hin = _add_chips(chip_idx, pin, bin_, 304, "add_chips_in")
    hout = _add_chips(chip_idx, pout, bout, 256, "add_chips_out")
    xin, xout = _exchange_halves(hin, hout)

    slab = jnp.concatenate([jnp.where(ac == 0, hin, xin), jnp.where(ac == 0, xin, hin)], axis=1)
    off = sum(jnp.where(chip == i, SHARD * i - _slab_lo(i), 0) for i in range(NCHIP))
    g_rows = jnp.transpose(lax.dynamic_slice_in_dim(slab, off, SHARD, axis=1), (1, 0, 2))
    rows_first = lambda t: jnp.transpose(t, (2, 0, 1))
    d_rows, nm_rows, nv_rows = _adam_rows(rows_first(w_in), g_rows, rows_first(m_w_in), rows_first(v_w_in), 32, "adam_w_in")
    g_w_in, d_w_in, nm_w_in, nv_w_in = [jnp.transpose(t, (1, 2, 0)) for t in (g_rows, d_rows, nm_rows, nv_rows)]

    def adam_big(w, g_own, g_other, m, v, name):
        shp = w.shape
        halves = lambda t: t.reshape(shp[0], 2, shp[1] // 2, shp[2])
        return [t.reshape(shp) for t in _adam_halves(c_idx, halves(w), g_own, g_other, halves(m), halves(v), 256, name)]

    g_w_out, d_w_out, nm_w_out, nv_w_out = adam_big(w_out, hout, xout, m_w_out, v_w_out, "adam_w_out")

    stacked = [jnp.stack([small[l][i] for l in range(nl)]) for i in range(7)] + [loss_part]
    g_small = _unpack_rows(_allreduce_small(_pack_rows(stacked)), [t.shape for t in stacked])
    g_gpre, g_gpost, g_wa_full, g_ba, g_ggla, g_gatt, g_rb, loss_sum = g_small
    loss = loss_sum[0, 0]
    g_wa = lax.dynamic_slice_in_dim(g_wa_full, chip * GDK, GDK, axis=2)
    names = [(g_pre, m_g_pre, v_g_pre, g_gpre), (g_post, m_g_post, v_g_post, g_gpost), (w_alpha, m_w_alpha, v_w_alpha, g_wa),
             (b_alpha, m_b_alpha, v_b_alpha, g_ba), (g_gla, m_g_gla, v_g_gla, g_ggla), (g_att, m_g_att, v_g_att, g_gatt),
             (rel_bias, m_rel_bias, v_rel_bias, g_rb)]
    shapes = [t[0].shape for t in names]
    packed = [_pack_rows([t[i] for t in names]) for i in range(4)]
    d_s, nm_s, nv_s = [_unpack_rows(t, shapes) for t in _adam(packed[0], packed[3], packed[1], packed[2], packed[0].shape[0], "adam_small")]

    grads = [g_w_in, g_w_out, g_gpre, g_gpost, g_wa, g_ba, g_ggla, g_gatt, g_rb]
    deltas = [d_w_in, d_w_out] + d_s
    new_m = [nm_w_in, nm_w_out] + nm_s
    new_v = [nv_w_in, nv_w_out] + nv_s
    return (loss, grad_x, *grads, *deltas, *new_m, *new_v)
```

```python
import functools

import jax
import jax.numpy as jnp
from jax import lax
from jax.experimental import pallas as pl
from jax.experimental.pallas import tpu as pltpu

D = 2048
DEPTH = 4
CHUNK = 64
GH, GDK, GDV = 4, 128, 256
DGLA = GH * GDV
RANK = 16
TAU = 16.0
AH, AHD = 8, 128
DATT = AH * AHD
LEFT = 8
NREL = 257
EPS = 1e-6
DIN = 7184
ADAM_LR, ADAM_B1, ADAM_B2, ADAM_EPS, ADAM_WD, ADAM_STEP = 0.001, 0.9, 0.999, 1e-08, 0.01, 10

GW = 2 * GDK + 2 * GDV
AW = 4 * AHD
ZG = GH * GW
ZA = AH * AW
ZM = ZG + ZA
GAP = 128
QB = 2 * CHUNK
HP = 2
BANDW = (LEFT + 2) * CHUNK
PADK = LEFT * CHUNK
NCHIP = 4
SHARD = DIN // NCHIP
SLAB = 1824
HSLAB = SLAB // 2
SPLIT_IN, SPLIT_OUT = 432, 128
NEG = -1e30
F32 = jnp.float32
_CDT = jnp.bfloat16
_XDT = jnp.bfloat16
_VMEM = 56 * 1024 * 1024
MESH = pl.DeviceIdType.MESH
ANY = pl.BlockSpec(memory_space=pl.ANY)


def _dot(a, b):
    return jnp.dot(a, b, preferred_element_type=F32)


def _dot_nt(a, b):
    return lax.dot_general(a, b, (((1,), (1,)), ((), ())), preferred_element_type=F32)


def _dot_tn(a, b):
    return lax.dot_general(a, b, (((0,), (0,)), ((), ())), preferred_element_type=F32)


def _rms_rows(v):
    return lax.rsqrt(jnp.mean(v * v, axis=-1, keepdims=True) + EPS)


def _sigmoid(v):
    return 1.0 / (1.0 + jnp.exp(-v))


def _log_sigmoid(v):
    return jnp.minimum(v, 0.0) - jnp.log(1.0 + jnp.exp(-jnp.abs(v)))


def _exact_dot(tri, v):
    hi = v.astype(_CDT)
    r1 = v - hi.astype(F32)
    mid = r1.astype(_CDT)
    lo = (r1 - mid.astype(F32)).astype(_CDT)
    return _dot(tri, hi) + _dot(tri, mid) + _dot(tri, lo)


def _tri(strict):
    row = lax.broadcasted_iota(jnp.int32, (CHUNK, CHUNK), 0)
    col = lax.broadcasted_iota(jnp.int32, (CHUNK, CHUNK), 1)
    return jnp.where((col < row) if strict else (col <= row), 1.0, 0.0).astype(_CDT)


def _norm_gate_bwd(o, g, gate, dcat):
    r = _rms_rows(o)
    oh = o * r
    sg = _sigmoid(gate)
    dn = dcat * (gate * sg)
    dgate = dcat * (oh * g) * (sg * (1.0 + gate * (1.0 - sg)))
    dg = jnp.sum(dn * oh, axis=0, keepdims=True)
    dnn = dn * g
    do = r * (dnn - oh * jnp.mean(dnn * oh, axis=-1, keepdims=True))
    return do, dgate, dg


def _params(sem=None, vmem=_VMEM):
    return pltpu.CompilerParams(dimension_semantics=sem, vmem_limit_bytes=vmem)


def _inproj(x, g, wm, wga, tm=512, tn=1024):
    s = x.shape[0]

    def body(x_ref, g_ref, wm_ref, wga_ref, z_ref, zga_ref, h_ref, hs):
        @pl.when(pl.program_id(1) == 0)
        def _():
            xv = x_ref[...]
            hv = (xv * _rms_rows(xv) * g_ref[...]).astype(_CDT)
            hs[...] = hv
            h_ref[...] = hv
            zga_ref[...] = _dot_nt(hv, wga_ref[...]).astype(_CDT)

        z_ref[...] = _dot_nt(hs[...], wm_ref[...]).astype(_CDT)

    return pl.pallas_call(
        body, name="inproj", grid=(s // tm, ZM // tn),
        in_specs=[pl.BlockSpec((tm, D), lambda i, j: (i, 0)), pl.BlockSpec((1, D), lambda i, j: (0, 0)),
                  pl.BlockSpec((tn, D), lambda i, j: (j, 0)), pl.BlockSpec((GAP, D), lambda i, j: (0, 0))],
        out_specs=[pl.BlockSpec((tm, tn), lambda i, j: (i, j)), pl.BlockSpec((tm, GAP), lambda i, j: (i, 0)),
                   pl.BlockSpec((tm, D), lambda i, j: (i, 0))],
        out_shape=[jax.ShapeDtypeStruct((s, ZM), _CDT), jax.ShapeDtypeStruct((s, GAP), _CDT),
                   jax.ShapeDtypeStruct((s, D), _CDT)],
        scratch_shapes=[pltpu.VMEM((tm, D), _CDT)],
        compiler_params=_params(("parallel", "arbitrary")),
    )(x, g, wm, wga)


def _gla_fwd(z, zga, wa, ba, ggla):
    s = z.shape[0]
    nc = s // CHUNK

    def body(zg_ref, zga_ref, wa_ref, ba_ref, g_ref, cat_ref, la_s, st):
        la_s[...] = _log_sigmoid(_dot(zga_ref[...], wa_ref[...]) + ba_ref[...]) * (1.0 / TAU)
        st[...] = jnp.zeros_like(st)
        tri = _tri(False)

        def step(n, carry):
            rows = pl.ds(pl.multiple_of(n * CHUNK, CHUNK), CHUNK)
            for p in range(HP):
                z0 = p * GW
                la = la_s[rows, p * GDK:(p + 1) * GDK]
                lc = _exact_dot(tri, la)
                lend = jnp.sum(la, axis=0, keepdims=True)
                kdec = (zg_ref[rows, z0 + GDK:z0 + 2 * GDK].astype(F32) * jnp.exp(lend - lc)).astype(_CDT)
                stn = jnp.exp(lend) * st[p] + _dot_tn(zg_ref[rows, z0 + 2 * GDK:z0 + 2 * GDK + GDV], kdec)
                st[p] = stn
                qs = (zg_ref[rows, z0:z0 + GDK].astype(F32) * (GDK ** -0.5)).astype(_CDT)
                o = _dot_nt(qs, stn.astype(_CDT))
                gate = zg_ref[rows, z0 + 2 * GDK + GDV:z0 + GW].astype(F32)
                gain = g_ref[:, p * GDV:(p + 1) * GDV]
                cat_ref[rows, p * GDV:(p + 1) * GDV] = (o * _rms_rows(o) * gain * (gate * _sigmoid(gate))).astype(_CDT)
            return carry

        lax.fori_loop(0, nc, step, 0)

    return pl.pallas_call(
        body, name="gla_fwd", grid=(GH // HP,),
        in_specs=[pl.BlockSpec((s, HP * GW), lambda h: (0, h)), pl.BlockSpec((s, GAP), lambda h: (0, 0)),
                  pl.BlockSpec((GAP, HP * GDK), lambda h: (0, h)), pl.BlockSpec((1, HP * GDK), lambda h: (0, h)),
                  pl.BlockSpec((1, HP * GDV), lambda h: (0, h))],
        out_specs=pl.BlockSpec((s, HP * GDV), lambda h: (0, h)),
        out_shape=jax.ShapeDtypeStruct((s, DGLA), _CDT),
        scratch_shapes=[pltpu.VMEM((s, HP * GDK), F32), pltpu.VMEM((HP, GDV, GDK), F32)],
        compiler_params=_params(("arbitrary",)),
    )(z, zga, wa, ba, ggla)


def _band_bias(b0):
    row = lax.broadcasted_iota(jnp.int32, (QB, 256), 0)
    col = lax.broadcasted_iota(jnp.int32, (QB, 256), 1)
    lane = lax.broadcasted_iota(jnp.int32, (1, 256), 1)
    c0 = jnp.sum(jnp.where(lane == 0, b0, 0.0), axis=1, keepdims=True)
    xv = jnp.broadcast_to(b0, (QB, 256))
    for bit in range(7):
        xv = jnp.where(((row >> bit) & 1) == 1, pltpu.roll(xv, 1 << bit, 1), xv)
    xv = jnp.where(col < row, c0, xv)
    return jnp.concatenate([jnp.broadcast_to(c0, (QB, BANDW - 256)), xv], axis=1)


def _band_static_mask():
    row = lax.broadcasted_iota(jnp.int32, (QB, BANDW), 0) >> 6
    col = lax.broadcasted_iota(jnp.int32, (QB, BANDW), 1) >> 6
    return (col >= row) & (col <= row + LEFT)


def _fold_bias_grad(t):
    row = lax.broadcasted_iota(jnp.int32, (QB, 256), 0)
    col = lax.broadcasted_iota(jnp.int32, (QB, 256), 1)
    xv = t[:, BANDW - 256:]
    low = col < row
    far = jnp.sum(t[:, 0:BANDW - 256], axis=1, keepdims=True) + jnp.sum(jnp.where(low, xv, 0.0), axis=1, keepdims=True)
    far = jnp.sum(far, axis=0, keepdims=True)
    xv = jnp.where(low, 0.0, xv)
    for bit in range(7):
        xv = jnp.where(((row >> bit) & 1) == 1, pltpu.roll(xv, 256 - (1 << bit), 1), xv)
    dp = jnp.sum(xv, axis=0, keepdims=True)
    lane = lax.broadcasted_iota(jnp.int32, (1, 256), 1)
    return dp + jnp.where(lane == 0, far, 0.0)


def _att_fwd(z, rbx, gatt):
    s = z.shape[0]
    nb = s // QB

    def body(za_ref, rb_ref, g_ref, cat_ref, o_ref, lse_ref, kp, vp, bias_s):
        for p in range(HP):
            z0 = p * AW
            kp[p, 0:PADK, :] = jnp.zeros((PADK, AHD), _CDT)
            vp[p, 0:PADK, :] = jnp.zeros((PADK, AHD), _CDT)
            kp[p, PADK:, :] = za_ref[:, z0 + AHD:z0 + 2 * AHD]
            vp[p, PADK:, :] = za_ref[:, z0 + 2 * AHD:z0 + 3 * AHD]
            bias_s[p] = jnp.where(_band_static_mask(), _band_bias(rb_ref[p]), NEG)

        def step(b, carry):
            r0 = pl.multiple_of(b * QB, QB)
            rows = pl.ds(r0, QB)
            band = pl.ds(r0, BANDW)
            live = lax.broadcasted_iota(jnp.int32, (QB, BANDW), 1) >= PADK - r0
            for p in range(HP):
                z0 = p * AW
                cols = slice(p * AHD, (p + 1) * AHD)
                sc = _dot_nt(za_ref[rows, z0:z0 + AHD], kp[p, band, :]) * (AHD ** -0.5) + bias_s[p]
                sc = jnp.where(live, sc, NEG)
                m = jnp.max(sc, axis=-1, keepdims=True)
                pr = jnp.exp(sc - m)
                l = jnp.sum(pr, axis=-1, keepdims=True)
                o = _dot((pr * (1.0 / l)).astype(_CDT), vp[p, band, :])
                o_ref[rows, cols] = o.astype(_CDT)
                lse_ref[rows, cols] = jnp.broadcast_to(m + jnp.log(l), (QB, AHD))
                gate = za_ref[rows, z0 + 3 * AHD:z0 + AW].astype(F32)
                cat_ref[rows, cols] = (o * _rms_rows(o) * g_ref[:, cols] * (gate * _sigmoid(gate))).astype(_CDT)
            return carry

        lax.fori_loop(0, nb, step, 0)

    return pl.pallas_call(
        body, name="att_fwd", grid=(AH // HP,),
        in_specs=[pl.BlockSpec((s, HP * AW), lambda h: (0, ZG // (HP * AW) + h)), pl.BlockSpec((HP, 1, 256), lambda h: (h, 0, 0)),
                  pl.BlockSpec((1, HP * AHD), lambda h: (0, h))],
        out_specs=[pl.BlockSpec((s, HP * AHD), lambda h: (0, h)), pl.BlockSpec((s, HP * AHD), lambda h: (0, h)),
                   pl.BlockSpec((s, HP * AHD), lambda h: (0, h))],
        out_shape=[jax.ShapeDtypeStruct((s, DATT), _CDT), jax.ShapeDtypeStruct((s, DATT), _CDT),
                   jax.ShapeDtypeStruct((s, DATT), F32)],
        scratch_shapes=[pltpu.VMEM((HP, s + PADK, AHD), _CDT), pltpu.VMEM((HP, s + PADK, AHD), _CDT),
                        pltpu.VMEM((HP, QB, BANDW), F32)],
        compiler_params=_params(("arbitrary",)),
    )(z, rbx, gatt)


def _outproj(cg, ca, wout, x, gpost, tm=256):
    s = x.shape[0]

    def body(cg_ref, ca_ref, w_ref, x_ref, g_ref, y_ref, xo_ref):
        y = _dot(cg_ref[...], w_ref[0:DGLA, :]) + _dot(ca_ref[...], w_ref[DGLA:, :])
        y_ref[...] = y
        xo_ref[...] = x_ref[...] + y * _rms_rows(y) * g_ref[...]

    return pl.pallas_call(
        body, name="outproj", grid=(s // tm,),
        in_specs=[pl.BlockSpec((tm, DGLA), lambda i: (i, 0)), pl.BlockSpec((tm, DATT), lambda i: (i, 0)),
                  pl.BlockSpec((D, D), lambda i: (0, 0)), pl.BlockSpec((tm, D), lambda i: (i, 0)),
                  pl.BlockSpec((1, D), lambda i: (0, 0))],
        out_specs=[pl.BlockSpec((tm, D), lambda i: (i, 0)), pl.BlockSpec((tm, D), lambda i: (i, 0))],
        out_shape=[jax.ShapeDtypeStruct((s, D), F32), jax.ShapeDtypeStruct((s, D), F32)],
        compiler_params=_params(("parallel",)),
    )(cg, ca, wout, x, gpost)


def _loss_grad(xo, tgt, tm=256):
    s = xo.shape[0]

    def body(xo_ref, t_ref, d_ref, l_ref):
        @pl.when(pl.program_id(0) == 0)
        def _():
            l_ref[...] = jnp.zeros_like(l_ref)

        e = xo_ref[...] - t_ref[...]
        d_ref[...] = e * (1.0 / D)
        l_ref[...] += jnp.sum(jnp.sum(e * e, axis=1, keepdims=True), axis=0, keepdims=True) * (0.5 / D)

    return pl.pallas_call(
        body, name="loss_grad", grid=(s // tm,),
        in_specs=[pl.BlockSpec((tm, D), lambda i: (i, 0)), pl.BlockSpec((tm, D), lambda i: (i, 0))],
        out_specs=[pl.BlockSpec((tm, D), lambda i: (i, 0)), pl.BlockSpec((1, 1), lambda i: (0, 0))],
        out_shape=[jax.ShapeDtypeStruct((s, D), F32), jax.ShapeDtypeStruct((1, 1), F32)],
        compiler_params=_params(("arbitrary",)),
    )(xo, tgt)


def _place():
    x, y, c = lax.axis_index("x"), lax.axis_index("y"), lax.axis_index("c")
    chips = [(1 - x, y), (x, 1 - y), (1 - x, 1 - y)]
    return x, y, c, chips


def _part_rows(total, split, part):
    return (0, split) if part == 0 else (split, total - split)


def _owner_copies(part, pin_ref, pout_ref, rin, rout, send_sems, recv_sems):
    x, y, c, chips = _place()
    cps = []
    for k, (px, py) in enumerate(chips):
        for a, (src, dst, split) in enumerate([(pin_ref, rin, SPLIT_IN), (pout_ref, rout, SPLIT_OUT)]):
            r0, n = _part_rows(src.shape[2], split, part)
            cps.append(pltpu.make_async_remote_copy(src_ref=src.at[0, 2 * px + py, pl.ds(r0, n)], dst_ref=dst.at[k],
                                                    send_sem=send_sems.at[a, k], recv_sem=recv_sems.at[a, k],
                                                    device_id=(px, py, c), device_id_type=MESH))
    return cps


class _ride_specs:
    def __init__(self, send, part):
        self.in_specs, self.out_specs, self.out_shape, self.scratch, self.operands = [], [], [], [], []
        if send is not None:
            self.in_specs, self.out_specs, self.operands = [ANY, ANY], [ANY, ANY], list(send)
            self.out_shape = [jax.ShapeDtypeStruct((3, _part_rows(t.shape[2], split, part)[1], D), t.dtype)
                              for t, split in zip(send, (SPLIT_IN, SPLIT_OUT))]
            self.scratch = [pltpu.SemaphoreType.DMA((2, 3)), pltpu.SemaphoreType.DMA((2, 3))]


def _ride_refs(refs, send, n_out):
    if send is None:
        return None, refs
    pin_ref, pout_ref = refs[:2]
    own_out = refs[2:2 + n_out]
    rin, rout = refs[2 + n_out:4 + n_out]
    return (pin_ref, pout_ref, rin, rout, refs[-2], refs[-1]), tuple(own_out) + tuple(refs[4 + n_out:-2])


def _ride_start(refs, send, n_out, part):
    ride = _ride_refs(refs, send, n_out)[0]
    if ride is None:
        return []
    cps = _owner_copies(part, *ride)

    @pl.when(pl.program_id(0) == 0)
    def _():
        for cpy in cps:
            cpy.start()

    return cps


def _ride_wait(cps, steps):
    if cps:
        @pl.when(pl.program_id(0) == steps - 1)
        def _():
            for cpy in cps:
                cpy.wait()


def _post_bwd(dout, y, gpost, wout, tm=256):
    s = y.shape[0]

    def body(d_ref, y_ref, g_ref, w_ref, dy_ref, dcg_ref, dca_ref, dg_ref):
        @pl.when(pl.program_id(0) == 0)
        def _():
            dg_ref[...] = jnp.zeros_like(dg_ref)

        yv = y_ref[...]
        r = _rms_rows(yv)
        yh = yv * r
        dv = d_ref[...]
        dg_ref[...] += jnp.sum(dv * yh, axis=0, keepdims=True)
        dn = dv * g_ref[...]
        dyb = (r * (dn - yh * jnp.mean(dn * yh, axis=-1, keepdims=True))).astype(_CDT)
        dy_ref[...] = dyb
        dcg_ref[...] = _dot_nt(dyb, w_ref[0:DGLA, :]).astype(_CDT)
        dca_ref[...] = _dot_nt(dyb, w_ref[DGLA:, :]).astype(_CDT)

    return pl.pallas_call(
        body, name="post_bwd", grid=(s // tm,),
        in_specs=[pl.BlockSpec((tm, D), lambda i: (i, 0)), pl.BlockSpec((tm, D), lambda i: (i, 0)),
                  pl.BlockSpec((1, D), lambda i: (0, 0)), pl.BlockSpec((D, D), lambda i: (0, 0))],
        out_specs=[pl.BlockSpec((tm, D), lambda i: (i, 0)), pl.BlockSpec((tm, DGLA), lambda i: (i, 0)),
                   pl.BlockSpec((tm, DATT), lambda i: (i, 0)), pl.BlockSpec((1, D), lambda i: (0, 0))],
        out_shape=[jax.ShapeDtypeStruct((s, D), _CDT), jax.ShapeDtypeStruct((s, DGLA), _CDT),
                   jax.ShapeDtypeStruct((s, DATT), _CDT), jax.ShapeDtypeStruct((1, D), F32)],
        compiler_params=_params(("arbitrary",)),
    )(dout, y, gpost, wout)


def _matmul_tn(a, b, out_dtype, tm, tn, name):
    k, m = a.shape
    n = b.shape[1]

    def body(a_ref, b_ref, o_ref):
        o_ref[...] = _dot_tn(a_ref[...], b_ref[...]).astype(out_dtype)

    return pl.pallas_call(
        body, name=name, grid=(m // tm, n // tn),
        in_specs=[pl.BlockSpec((k, tm), lambda i, j: (0, i)), pl.BlockSpec((k, tn), lambda i, j: (0, j))],
        out_specs=pl.BlockSpec((tm, tn), lambda i, j: (i, j)),
        out_shape=jax.ShapeDtypeStruct((m, n), out_dtype),
        compiler_params=_params(("parallel", "parallel")),
    )(a, b)


def _att_bwd(z, oraw, lse, dca, rbx, gatt, send=None):
    s = z.shape[0]
    nb = s // QB

    def body(*refs):
        za_ref, o_ref, lse_ref, dc_ref, rb_ref, g_ref = refs[:6]
        dz_ref, dg_ref, db_ref, kp, vp, dkp, dvp, bias_s, t_s, dg_s = _ride_refs(refs[6:], send, 3)[1]
        cps = _ride_start(refs[6:], send, 3, 0)
        for p in range(HP):
            z0 = p * AW
            kp[p, 0:PADK, :] = jnp.zeros((PADK, AHD), _CDT)
            vp[p, 0:PADK, :] = jnp.zeros((PADK, AHD), _CDT)
            kp[p, PADK:, :] = za_ref[:, z0 + AHD:z0 + 2 * AHD]
            vp[p, PADK:, :] = za_ref[:, z0 + 2 * AHD:z0 + 3 * AHD]
            bias_s[p] = jnp.where(_band_static_mask(), _band_bias(rb_ref[p]), NEG)
        dkp[...] = jnp.zeros_like(dkp)
        dvp[...] = jnp.zeros_like(dvp)
        t_s[...] = jnp.zeros_like(t_s)
        dg_s[...] = jnp.zeros_like(dg_s)

        def step(b, carry):
            r0 = pl.multiple_of(b * QB, QB)
            rows = pl.ds(r0, QB)
            band = pl.ds(r0, BANDW)
            live = lax.broadcasted_iota(jnp.int32, (QB, BANDW), 1) >= PADK - r0
            for p in range(HP):
                z0 = p * AW
                cols = slice(p * AHD, (p + 1) * AHD)
                o = o_ref[rows, cols].astype(F32)
                do, dgate, dg = _norm_gate_bwd(o, g_ref[:, cols], za_ref[rows, z0 + 3 * AHD:z0 + AW].astype(F32),
                                               dc_ref[rows, cols].astype(F32))
                dg_s[:, cols] += dg
                q = za_ref[rows, z0:z0 + AHD]
                kb = kp[p, band, :]
                sc = _dot_nt(q, kb) * (AHD ** -0.5) + bias_s[p]
                sc = jnp.where(live, sc, NEG)
                pr = jnp.exp(sc - jnp.max(lse_ref[rows, cols], axis=-1, keepdims=True))
                dob = do.astype(_CDT)
                dp = _dot_nt(dob, vp[p, band, :])
                ds = pr * (dp - jnp.sum(do * o, axis=-1, keepdims=True))
                t_s[p] += ds
                dsb = (ds * (AHD ** -0.5)).astype(_CDT)
                dz_ref[rows, z0:z0 + AHD] = _dot(dsb, kb).astype(_CDT)
                dz_ref[rows, z0 + 3 * AHD:z0 + AW] = dgate.astype(_CDT)
                dkp[p, band, :] += _dot_tn(dsb, q)
                dvp[p, band, :] += _dot_tn(pr.astype(_CDT), dob)
            return carry

        lax.fori_loop(0, nb, step, 0)
        for p in range(HP):
            z0 = p * AW
            dz_ref[:, z0 + AHD:z0 + 2 * AHD] = dkp[p, PADK:, :].astype(_CDT)
            dz_ref[:, z0 + 2 * AHD:z0 + 3 * AHD] = dvp[p, PADK:, :].astype(_CDT)
            db_ref[p] = _fold_bias_grad(t_s[p])
        dg_ref[...] = dg_s[...]
        _ride_wait(cps, AH // HP)

    ride = _ride_specs(send, 0)
    return pl.pallas_call(
        body, name="att_bwd", grid=(AH // HP,),
        in_specs=[pl.BlockSpec((s, HP * AW), lambda h: (0, ZG // (HP * AW) + h)), pl.BlockSpec((s, HP * AHD), lambda h: (0, h)),
                  pl.BlockSpec((s, HP * AHD), lambda h: (0, h)), pl.BlockSpec((s, HP * AHD), lambda h: (0, h)),
                  pl.BlockSpec((HP, 1, 256), lambda h: (h, 0, 0)), pl.BlockSpec((1, HP * AHD), lambda h: (0, h))] + ride.in_specs,
        out_specs=[pl.BlockSpec((s, HP * AW), lambda h: (0, h)), pl.BlockSpec((1, HP * AHD), lambda h: (0, h)),
                   pl.BlockSpec((HP, 1, 256), lambda h: (h, 0, 0))] + ride.out_specs,
        out_shape=[jax.ShapeDtypeStruct((s, ZA), _CDT), jax.ShapeDtypeStruct((1, DATT), F32),
                   jax.ShapeDtypeStruct((AH, 1, 256), F32)] + ride.out_shape,
        scratch_shapes=[pltpu.VMEM((HP, s + PADK, AHD), _CDT), pltpu.VMEM((HP, s + PADK, AHD), _CDT),
                        pltpu.VMEM((HP, s + PADK, AHD), F32), pltpu.VMEM((HP, s + PADK, AHD), F32),
                        pltpu.VMEM((HP, QB, BANDW), F32), pltpu.VMEM((HP, QB, BANDW), F32), pltpu.VMEM((1, HP * AHD), F32)] + ride.scratch,
        compiler_params=_params(("arbitrary",)),
    )(z, oraw, lse, dca, rbx, gatt, *ride.operands)


def _gla_bwd(z, zga, wa, ba, ggla, dcg, send=None):
    s = z.shape[0]
    nc = s // CHUNK

    def body(*refs):
        zg_ref, zga_ref, wa_ref, ba_ref, g_ref, dc_ref = refs[:6]
        (dz_ref, dga_ref, dwa_ref, dba_ref, dg_ref,
         la_s, om_s, sall, dpre_s, c_s, dga_s, dg_s) = _ride_refs(refs[6:], send, 5)[1]
        cps = _ride_start(refs[6:], send, 5, 1)
        h = pl.program_id(0)
        pre = _dot(zga_ref[...], wa_ref[...]) + ba_ref[...]
        la_s[...] = _log_sigmoid(pre) * (1.0 / TAU)
        om_s[...] = (1.0 - _sigmoid(pre)) * (1.0 / TAU)
        c_s[...] = jnp.zeros_like(c_s)
        dg_s[...] = jnp.zeros_like(dg_s)
        tri = _tri(False)
        tri_strict = _tri(True)

        def decay(rows, p):
            la = la_s[rows, p * GDK:(p + 1) * GDK]
            lend = jnp.sum(la, axis=0, keepdims=True)
            return jnp.exp(lend - _exact_dot(tri, la)), jnp.exp(lend)

        def fwd(n, sts):
            rows = pl.ds(pl.multiple_of(n * CHUNK, CHUNK), CHUNK)
            out = []
            for p in range(HP):
                z0 = p * GW
                dec, a = decay(rows, p)
                kdec = (zg_ref[rows, z0 + GDK:z0 + 2 * GDK].astype(F32) * dec).astype(_CDT)
                stn = a * sts[p] + _dot_tn(zg_ref[rows, z0 + 2 * GDK:z0 + 2 * GDK + GDV], kdec)
                sall[p, n] = stn
                out.append(stn)
            return tuple(out)

        lax.fori_loop(0, nc, fwd, tuple(jnp.zeros((GDV, GDK), F32) for _ in range(HP)))

        def bwd(i, carry):
            n = nc - 1 - i
            rows = pl.ds(pl.multiple_of(n * CHUNK, CHUNK), CHUNK)
            for p in range(HP):
                z0 = p * GW
                kc = slice(p * GDK, (p + 1) * GDK)
                vc = slice(p * GDV, (p + 1) * GDV)
                dec, a = decay(rows, p)
                kdec = zg_ref[rows, z0 + GDK:z0 + 2 * GDK].astype(F32) * dec
                kdb = kdec.astype(_CDT)
                v = zg_ref[rows, z0 + 2 * GDK:z0 + 2 * GDK + GDV]
                qs = (zg_ref[rows, z0:z0 + GDK].astype(F32) * (GDK ** -0.5)).astype(_CDT)
                stb = sall[p, n].astype(_CDT)
                st_prev = sall[p, jnp.maximum(n - 1, 0)] * jnp.where(n > 0, 1.0, 0.0)
                o = _dot_nt(qs, stb)
                do, dgate, dg = _norm_gate_bwd(o, g_ref[:, vc], zg_ref[rows, z0 + 2 * GDK + GDV:z0 + GW].astype(F32),
                                               dc_ref[rows, vc].astype(F32))
                dg_s[:, vc] += dg
                dob = do.astype(_CDT)
                gt = _dot_tn(dob, qs) + c_s[p]
                gtb = gt.astype(_CDT)
                da = jnp.sum(gt * st_prev, axis=0, keepdims=True)
                dkdec = _dot(v, gtb)
                dla = _exact_dot(tri_strict, dkdec * kdec) + da * a
                dpre_s[rows, kc] = dla * om_s[rows, kc]
                dz_ref[rows, z0:z0 + GDK] = (_dot(dob, stb) * (GDK ** -0.5)).astype(_CDT)
                dz_ref[rows, z0 + GDK:z0 + 2 * GDK] = (dkdec * dec).astype(_CDT)
                dz_ref[rows, z0 + 2 * GDK:z0 + 2 * GDK + GDV] = _dot_nt(kdb, gtb).astype(_CDT)
                dz_ref[rows, z0 + 2 * GDK + GDV:z0 + GW] = dgate.astype(_CDT)
                c_s[p] = a * gt
            return carry

        lax.fori_loop(0, nc, bwd, 0)
        dpre = dpre_s[...]
        dpb = dpre.astype(_CDT)
        dg_ref[...] = dg_s[...]
        dba_ref[...] = jnp.sum(dpre, axis=0, keepdims=True)
        dwa_ref[...] = _dot_tn(zga_ref[...], dpb)
        part = _dot_nt(dpb, wa_ref[...])

        @pl.when(h == 0)
        def _():
            dga_s[...] = part

        @pl.when(h > 0)
        def _():
            dga_s[...] += part

        @pl.when(h == GH // HP - 1)
        def _():
            dga_ref[...] = dga_s[...].astype(_CDT)

        _ride_wait(cps, GH // HP)

    ride = _ride_specs(send, 1)
    return pl.pallas_call(
        body, name="gla_bwd", grid=(GH // HP,),
        in_specs=[pl.BlockSpec((s, HP * GW), lambda h: (0, h)), pl.BlockSpec((s, GAP), lambda h: (0, 0)),
                  pl.BlockSpec((GAP, HP * GDK), lambda h: (0, h)), pl.BlockSpec((1, HP * GDK), lambda h: (0, h)),
                  pl.BlockSpec((1, HP * GDV), lambda h: (0, h)), pl.BlockSpec((s, HP * GDV), lambda h: (0, h))] + ride.in_specs,
        out_specs=[pl.BlockSpec((s, HP * GW), lambda h: (0, h)), pl.BlockSpec((s, GAP), lambda h: (0, 0)),
                   pl.BlockSpec((GAP, HP * GDK), lambda h: (0, h)), pl.BlockSpec((1, HP * GDK), lambda h: (0, h)),
                   pl.BlockSpec((1, HP * GDV), lambda h: (0, h))] + ride.out_specs,
        out_shape=[jax.ShapeDtypeStruct((s, ZG), _CDT), jax.ShapeDtypeStruct((s, GAP), _CDT),
                   jax.ShapeDtypeStruct((GAP, GH * GDK), F32), jax.ShapeDtypeStruct((1, GH * GDK), F32),
                   jax.ShapeDtypeStruct((1, DGLA), F32)] + ride.out_shape,
        scratch_shapes=[pltpu.VMEM((s, HP * GDK), F32), pltpu.VMEM((s, HP * GDK), F32), pltpu.VMEM((HP, nc, GDV, GDK), F32),
                        pltpu.VMEM((s, HP * GDK), F32), pltpu.VMEM((HP, GDV, GDK), F32), pltpu.VMEM((s, GAP), F32),
                        pltpu.VMEM((1, HP * GDV), F32)] + ride.scratch,
        compiler_params=_params(("arbitrary",)),
    )(z, zga, wa, ba, ggla, dcg, *ride.operands)


def _dh(dzg, dza, dga, wm, wga, x, dout, gpre, tm=512, tk=1024):
    s = x.shape[0]
    nkg, nk = ZG // tk, ZM // tk

    def body(dzg_ref, dza_ref, dga_ref, wm_ref, wga_ref, x_ref, d_ref, g_ref, dx_ref, dg_ref, acc):
        i, k = pl.program_id(0), pl.program_id(1)

        @pl.when((i == 0) & (k == 0))
        def _():
            dg_ref[...] = jnp.zeros_like(dg_ref)

        @pl.when(k == 0)
        def _():
            acc[...] = _dot(dga_ref[...], wga_ref[...])

        @pl.when(k < nkg)
        def _():
            acc[...] += _dot(dzg_ref[...], wm_ref[...])

        @pl.when(k >= nkg)
        def _():
            acc[...] += _dot(dza_ref[...], wm_ref[...])

        @pl.when(k == nk - 1)
        def _():
            xv = x_ref[...]
            r = _rms_rows(xv)
            xh = xv * r
            dh = acc[...]
            dg_ref[...] += jnp.sum(dh * xh, axis=0, keepdims=True)
            dn = dh * g_ref[...]
            dx_ref[...] = d_ref[...] + r * (dn - xh * jnp.mean(dn * xh, axis=-1, keepdims=True))

    return pl.pallas_call(
        body, name="dh", grid=(s // tm, nk),
        in_specs=[pl.BlockSpec((tm, tk), lambda i, k: (i, jnp.minimum(k, nkg - 1))),
                  pl.BlockSpec((tm, tk), lambda i, k: (i, jnp.maximum(k - nkg, 0))),
                  pl.BlockSpec((tm, GAP), lambda i, k: (i, 0)), pl.BlockSpec((tk, D), lambda i, k: (k, 0)),
                  pl.BlockSpec((GAP, D), lambda i, k: (0, 0)), pl.BlockSpec((tm, D), lambda i, k: (i, 0)),
                  pl.BlockSpec((tm, D), lambda i, k: (i, 0)), pl.BlockSpec((1, D), lambda i, k: (0, 0))],
        out_specs=[pl.BlockSpec((tm, D), lambda i, k: (i, 0)), pl.BlockSpec((1, D), lambda i, k: (0, 0))],
        out_shape=[jax.ShapeDtypeStruct((s, D), F32), jax.ShapeDtypeStruct((1, D), F32)],
        scratch_shapes=[pltpu.VMEM((tm, D), F32)],
        compiler_params=_params(("arbitrary", "arbitrary")),
    )(dzg, dza, dga, wm, wga, x, dout, gpre)


def _adam(w, g, m, v, tr, name):
    rws, cols = w.shape

    def body(w_ref, g_ref, m_ref, v_ref, d_ref, mo_ref, vo_ref):
        gv = g_ref[...]
        mn = ADAM_B1 * m_ref[...] + (1.0 - ADAM_B1) * gv
        vn = ADAM_B2 * v_ref[...] + (1.0 - ADAM_B2) * (gv * gv)
        mh = mn / (1.0 - ADAM_B1 ** ADAM_STEP)
        vh = vn / (1.0 - ADAM_B2 ** ADAM_STEP)
        d_ref[...] = -ADAM_LR * (mh / (jnp.sqrt(vh) + ADAM_EPS) + ADAM_WD * w_ref[...])
        mo_ref[...] = mn
        vo_ref[...] = vn

    spec = pl.BlockSpec((tr, cols), lambda i: (i, 0))
    return pl.pallas_call(
        body, name=name, grid=(rws // tr,), in_specs=[spec] * 4, out_specs=[spec] * 3,
        out_shape=[jax.ShapeDtypeStruct((rws, cols), F32)] * 3,
        compiler_params=_params(("parallel",)),
    )(w, g, m, v)


def _adam_rows(w, g, m, v, tj, name):
    rows = w.shape[0]

    def body(w_ref, g_ref, m_ref, v_ref, d_ref, mo_ref, vo_ref):
        gv = g_ref[...]
        mn = ADAM_B1 * m_ref[...] + (1.0 - ADAM_B1) * gv
        vn = ADAM_B2 * v_ref[...] + (1.0 - ADAM_B2) * (gv * gv)
        mh = mn / (1.0 - ADAM_B1 ** ADAM_STEP)
        vh = vn / (1.0 - ADAM_B2 ** ADAM_STEP)
        d_ref[...] = -ADAM_LR * (mh / (jnp.sqrt(vh) + ADAM_EPS) + ADAM_WD * w_ref[...])
        mo_ref[...] = mn
        vo_ref[...] = vn

    spec = pl.BlockSpec((tj,) + w.shape[1:], lambda i: (i, 0, 0))
    return pl.pallas_call(
        body, name=name, grid=(pl.cdiv(rows, tj),), in_specs=[spec] * 4, out_specs=[spec] * 3,
        out_shape=[jax.ShapeDtypeStruct(w.shape, F32)] * 3,
        compiler_params=_params(("parallel",)),
    )(w, g, m, v)


def _row_chunks(n, parts):
    base, extra = divmod(n, parts)
    out, r = [], 0
    for i in range(parts):
        size = base + (1 if i < extra else 0)
        out.append((r, size))
        r += size
    return out


def _gather_weights(wt, wout, wa, wt_all, wout_all, wa_all):
    nl = wout.shape[0]
    chunks = _row_chunks(wt.shape[2], nl)

    def variant(x, y, c, wt_ref, wout_ref, wa_ref, _wt, _wout, _wa, wt_all, wout_all, wa_all, send_sems, recv_sems, wa_send, wa_recv):
        me = 2 * x + y
        sib = (x, y, 1 - c)
        xn, yn, dg = (1 - x, y), (x, 1 - y), (1 - x, 1 - y)

        def wt_rows(ref, k):
            return ref.at[:, pl.ds(chunks[k][0], chunks[k][1])]

        arrays = [(lambda k: wt_rows(wt_ref.at[c], k), lambda chip, k: wt_rows(wt_all.at[chip, c], k)),
                  (lambda k: wout_ref.at[k, c], lambda chip, k: wout_all.at[k, chip, c])]

        def cid(p):
            return 2 * p[0] + p[1]

        def cp(src, dst, a, l, k, to):
            return pltpu.make_async_remote_copy(src_ref=src, dst_ref=dst, send_sem=send_sems.at[a, l, k],
                                                recv_sem=recv_sems.at[a, l, k], device_id=to, device_id_type=MESH)

        def wa_cp(k, chip, to):
            return pltpu.make_async_remote_copy(src_ref=wa_ref, dst_ref=wa_all.at[chip], send_sem=wa_send.at[k],
                                                recv_sem=wa_recv.at[k], device_id=to, device_id_type=MESH)

        sent = []

        def go(cpy):
            cpy.start()
            sent.append(cpy)

        for l in range(nl):
            for a, (src, place) in enumerate(arrays):
                go(cp(src(l), place(me, l), a, l, 0, (xn[0], xn[1], c)))
                go(cp(src(l), place(me, l), a, l, 1, (yn[0], yn[1], c)))
        for k, p in enumerate((xn, yn, dg)):
            go(wa_cp(k, me, (p[0], p[1], c)))
        for l in range(nl):
            for a, (src, place) in enumerate(arrays):
                got = place(cid(xn), l)
                cp(got, got, a, l, 0, sib).wait_recv()
                go(cp(got.at[0], got.at[0], a, l, 2, (yn[0], yn[1], c)))
                go(cp(got, got, a, l, 4, sib))
                got = place(cid(yn), l)
                cp(got, got, a, l, 1, sib).wait_recv()
                go(cp(got.at[1], got.at[1], a, l, 3, (xn[0], xn[1], c)))
                go(cp(got, got, a, l, 5, sib))
        for l in range(nl):
            for a, (src, place) in enumerate(arrays):
                got = place(cid(dg), l)
                cp(got.at[0], got.at[0], a, l, 2, sib).wait_recv()
                cp(got.at[1], got.at[1], a, l, 3, sib).wait_recv()
                go(cp(got, got, a, l, 6, sib))
        for k, p in enumerate((xn, yn, dg)):
            wa_cp(k, cid(p), sib).wait_recv()
        for l in range(nl):
            for a, (src, place) in enumerate(arrays):
                for k in (4, 5, 6):
                    cp(place(me, l), place(me, l), a, l, k, sib).wait_recv()
        for cpy in sent:
            cpy.wait_send()

    def body(*refs):
        x, y, c, _ = _place()
        for jx in range(2):
            for jy in range(2):
                for jc in range(2):
                    pl.when((x == jx) & (y == jy) & (c == jc))(functools.partial(variant, jx, jy, jc, *refs))

    return pl.pallas_call(
        body, name="gather_weights", in_specs=[ANY] * 6, out_specs=[ANY] * 3, input_output_aliases={3: 0, 4: 1, 5: 2},
        out_shape=[jax.ShapeDtypeStruct(t.shape, t.dtype) for t in (wt_all, wout_all, wa_all)],
        scratch_shapes=[pltpu.SemaphoreType.DMA((2, nl, 7)), pltpu.SemaphoreType.DMA((2, nl, 7)),
                        pltpu.SemaphoreType.DMA((3,)), pltpu.SemaphoreType.DMA((3,))],
    )(wt, wout, wa, wt_all, wout_all, wa_all)


def _swap_halves(gin2, gout2):
    def body(gin_ref, gout_ref, rin, rout, send_sems, recv_sems):
        x, y, c, _ = _place()
        sib = (x, y, 1 - c)
        cps = [pltpu.make_async_remote_copy(src_ref=src.at[1 - c], dst_ref=dst, send_sem=send_sems.at[a],
                                            recv_sem=recv_sems.at[a], device_id=sib, device_id_type=MESH)
               for a, (src, dst) in enumerate([(gin_ref, rin), (gout_ref, rout)])]
        for cpy in cps:
            cpy.start()
        for cpy in cps:
            cpy.wait()

    return pl.pallas_call(
        body, name="swap_halves", in_specs=[ANY, ANY], out_specs=[ANY, ANY],
        out_shape=[jax.ShapeDtypeStruct(gin2.shape[1:], gin2.dtype), jax.ShapeDtypeStruct(gout2.shape[1:], gout2.dtype)],
        scratch_shapes=[pltpu.SemaphoreType.DMA((2,)), pltpu.SemaphoreType.DMA((2,))],
    )(gin2, gout2)


def _add_halves(c_idx, g2, r, tr, name):
    rows, cols = r.shape

    def body(c_ref, g_ref, r_ref, o_ref):
        o_ref[...] = (g_ref[0].astype(F32) + r_ref[...].astype(F32)).astype(_XDT)

    return pl.pallas_call(
        body, name=name,
        grid_spec=pltpu.PrefetchScalarGridSpec(
            num_scalar_prefetch=1, grid=(rows // tr,),
            in_specs=[pl.BlockSpec((1, tr, cols), lambda i, c_ref: (c_ref[0], i, 0)),
                      pl.BlockSpec((tr, cols), lambda i, c_ref: (i, 0))],
            out_specs=pl.BlockSpec((tr, cols), lambda i, c_ref: (i, 0))),
        out_shape=jax.ShapeDtypeStruct((rows, cols), _XDT),
        compiler_params=_params(("parallel",)),
    )(c_idx, g2, r)


def _send_to_owners(pin, pout):
    def body(pin_ref, pout_ref, rin_a, rout_a, rin_b, rout_b, send_sems, recv_sems):
        cps = (_owner_copies(0, pin_ref, pout_ref, rin_a, rout_a, send_sems.at[0], recv_sems.at[0])
               + _owner_copies(1, pin_ref, pout_ref, rin_b, rout_b, send_sems.at[1], recv_sems.at[1]))
        for cpy in cps:
            cpy.start()
        for cpy in cps:
            cpy.wait()

    shapes = [jax.ShapeDtypeStruct((3, _part_rows(t.shape[2], split, part)[1], D), t.dtype)
              for part in range(2) for t, split in zip((pin, pout), (SPLIT_IN, SPLIT_OUT))]
    return pl.pallas_call(
        body, name="send_to_owners", in_specs=[ANY, ANY], out_specs=[ANY] * 4, out_shape=shapes,
        scratch_shapes=[pltpu.SemaphoreType.DMA((2, 2, 3)), pltpu.SemaphoreType.DMA((2, 2, 3))],
    )(pin, pout)


def _add_chips(chip_idx, p, ra, rb, tr, name):
    rows = p.shape[2]
    na = ra.shape[1] // tr

    def body(c_ref, p_ref, ra_ref, rb_ref, o_ref):
        r = jnp.where(pl.program_id(0) < na, ra_ref[...], rb_ref[...]).astype(F32)
        o_ref[0] = ((p_ref[0, 0].astype(F32) + r[0]) + r[1]) + r[2]

    return pl.pallas_call(
        body, name=name,
        grid_spec=pltpu.PrefetchScalarGridSpec(
            num_scalar_prefetch=1, grid=(rows // tr,),
            in_specs=[pl.BlockSpec((1, 1, tr, D), lambda i, c_ref: (0, c_ref[0], i, 0)),
                      pl.BlockSpec((3, tr, D), lambda i, c_ref: (0, jnp.minimum(i, na - 1), 0)),
                      pl.BlockSpec((3, tr, D), lambda i, c_ref: (0, jnp.maximum(i - na, 0), 0))],
            out_specs=pl.BlockSpec((1, tr, D), lambda i, c_ref: (0, i, 0))),
        out_shape=jax.ShapeDtypeStruct((1, rows, D), F32),
        compiler_params=_params(("parallel",)),
    )(chip_idx, p, ra, rb)


def _exchange_halves(arrs):
    n = len(arrs)

    def body(*refs):
        x, y, c, _ = _place()
        cps = [pltpu.make_async_remote_copy(src_ref=refs[a], dst_ref=refs[n + a], send_sem=refs[2 * n].at[a], recv_sem=refs[2 * n + 1].at[a],
                                            device_id=(x, y, 1 - c), device_id_type=MESH) for a in range(n)]
        for cpy in cps:
            cpy.start()
        for cpy in cps:
            cpy.wait()

    return pl.pallas_call(
        body, name="exchange_halves", in_specs=[ANY] * n, out_specs=[ANY] * n,
        out_shape=[jax.ShapeDtypeStruct(t.shape, t.dtype) for t in arrs],
        scratch_shapes=[pltpu.SemaphoreType.DMA((n,)), pltpu.SemaphoreType.DMA((n,))],
    )(*arrs)


def _adam_halves(c_idx, w, g_own, g_other, m, v, tr, name):
    nl, _, rows, cols = w.shape

    def body(c_ref, w_ref, go_ref, gx_ref, m_ref, v_ref, g_ref, d_ref, mo_ref, vo_ref):
        gv = jnp.where(pl.program_id(1) == c_ref[0], go_ref[0], gx_ref[0])
        mn = ADAM_B1 * m_ref[0, 0] + (1.0 - ADAM_B1) * gv
        vn = ADAM_B2 * v_ref[0, 0] + (1.0 - ADAM_B2) * (gv * gv)
        mh = mn / (1.0 - ADAM_B1 ** ADAM_STEP)
        vh = vn / (1.0 - ADAM_B2 ** ADAM_STEP)
        g_ref[0, 0] = gv
        d_ref[0, 0] = -ADAM_LR * (mh / (jnp.sqrt(vh) + ADAM_EPS) + ADAM_WD * w_ref[0, 0])
        mo_ref[0, 0] = mn
        vo_ref[0, 0] = vn

    full = pl.BlockSpec((1, 1, tr, cols), lambda l, hh, i, c_ref: (l, hh, i, 0))
    own = pl.BlockSpec((1, tr, cols), lambda l, hh, i, c_ref: (l, jnp.where(hh == c_ref[0], i, 0), 0))
    other = pl.BlockSpec((1, tr, cols), lambda l, hh, i, c_ref: (l, jnp.where(hh == c_ref[0], 0, i), 0))
    return pl.pallas_call(
        body, name=name,
        grid_spec=pltpu.PrefetchScalarGridSpec(
            num_scalar_prefetch=1, grid=(nl, 2, rows // tr),
            in_specs=[full, own, other, full, full], out_specs=[full] * 4),
        out_shape=[jax.ShapeDtypeStruct(w.shape, F32)] * 4,
        compiler_params=_params(("parallel", "parallel", "parallel")),
    )(c_idx, w, g_own, g_other, m, v)


def _allreduce_small(sg):
    rows = sg.shape[0]
    vm = pl.BlockSpec(memory_space=pltpu.VMEM)

    def body(sg_ref, tot_ref, all_ref, send_sems, recv_sems):
        x, y, c, _ = _place()
        me = 4 * x + 2 * y + c
        all_ref[me] = sg_ref[...]
        cps = []
        for mask in range(1, 8):
            to = (1 - x if mask & 4 else x, 1 - y if mask & 2 else y, 1 - c if mask & 1 else c)
            cps.append(pltpu.make_async_remote_copy(src_ref=sg_ref, dst_ref=all_ref.at[me], send_sem=send_sems.at[mask - 1],
                                                    recv_sem=recv_sems.at[mask - 1], device_id=to, device_id_type=MESH))
        for cpy in cps:
            cpy.start()
        for cpy in cps:
            cpy.wait()
        acc = all_ref[0]
        for d in range(1, 8):
            acc = acc + all_ref[d]
        tot_ref[...] = acc

    return pl.pallas_call(
        body, name="allreduce_small", in_specs=[vm], out_specs=[vm, vm],
        out_shape=[jax.ShapeDtypeStruct((rows, 128), F32), jax.ShapeDtypeStruct((8, rows, 128), F32)],
        scratch_shapes=[pltpu.SemaphoreType.DMA((7,)), pltpu.SemaphoreType.DMA((7,))],
        compiler_params=_params(),
    )(sg)[0]


_CUTS = [0, 512, 1024, 2048, 3072, 3088, 4112, 5136, 6160, 7184]


def _rows_to_internal(w):
    tail = w.shape[1:]
    gq, gk, gv, gg, ga, aq, ak, av, ag = [w[_CUTS[i]:_CUTS[i + 1]] for i in range(9)]
    g = jnp.concatenate([gq.reshape((GH, GDK) + tail), gk.reshape((GH, GDK) + tail),
                         gv.reshape((GH, GDV) + tail), gg.reshape((GH, GDV) + tail)], axis=1).reshape((ZG,) + tail)
    a = jnp.concatenate([t.reshape((AH, AHD) + tail) for t in (aq, ak, av, ag)], axis=1).reshape((ZA,) + tail)
    pad = [(0, GAP - RANK)] + [(0, 0)] * len(tail)
    return jnp.concatenate([g, a], axis=0), jnp.pad(ga, pad)


def _rows_from_internal(g, a, ga):
    tail = g.shape[1:]
    g = g.reshape((GH, GW) + tail)
    a = a.reshape((AH, AW) + tail)
    parts = [g[:, 0:GDK], g[:, GDK:2 * GDK], g[:, 2 * GDK:2 * GDK + GDV], g[:, 2 * GDK + GDV:GW]]
    parts = [t.reshape((-1,) + tail) for t in parts] + [ga[0:RANK]]
    parts += [a[:, i * AHD:(i + 1) * AHD].reshape((-1,) + tail) for i in range(4)]
    return jnp.concatenate(parts, axis=0)


def _slab_lo(chip):
    return min(SHARD * chip // 16 * 16, DIN - SLAB)


def _pack_rows(parts):
    rows = []
    for t in parts:
        flat = t.reshape(-1)
        rows.append(jnp.pad(flat, (0, (-flat.shape[0]) % 128)).reshape(-1, 128))
    buf = jnp.concatenate(rows, axis=0)
    return jnp.pad(buf, ((0, (-buf.shape[0]) % 8), (0, 0)))


def _unpack_rows(buf, shapes):
    out, r = [], 0
    for shp in shapes:
        n = 1
        for d in shp:
            n *= d
        nr = -(-n // 128)
        out.append(buf[r:r + nr].reshape(-1)[:n].reshape(shp))
        r += nr
    return out


def _layer_fwd(x, wm, wga, wout, gpre, gpost, wa, ba, ggla, gatt, rbx):
    z, zga, h = _inproj(x, gpre, wm, wga)
    cg = _gla_fwd(z, zga, wa, ba, ggla)
    ca, oraw, lse = _att_fwd(z, rbx, gatt)
    y, xo = _outproj(cg, ca, wout, x, gpost)
    return xo, (x, z, zga, h, cg, ca, oraw, lse, y)


def _layer_bwd(dout, saved, wm, wga, wout, gpre, gpost, wa, ba, ggla, gatt, rbx, send=None):
    x, z, zga, h, cg, ca, oraw, lse, y = saved
    dy, dcg, dca, dgpost = _post_bwd(dout, y, gpost, wout)
    dwout = jnp.concatenate([_matmul_tn(cg, dy, _XDT, 512, 1024, "dwout_gla"),
                             _matmul_tn(ca, dy, _XDT, 512, 1024, "dwout_att")], axis=0)
    dza, dgatt, dbx, *got_a = _att_bwd(z, oraw, lse, dca, rbx, gatt, send)
    dzg, dga, dwa, dba, dggla, *got_b = _gla_bwd(z, zga, wa, ba, ggla, dcg, send)
    dx, dgpre = _dh(dzg, dza, dga, wm, wga, x, dout, gpre)
    dwin = (_matmul_tn(dzg, h, _XDT, 512, 1024, "dwin_gla"), _matmul_tn(dza, h, _XDT, 512, 1024, "dwin_att"),
            _matmul_tn(dga, h, _XDT, GAP, 1024, "dwin_gate"))
    drb = jnp.concatenate([jnp.zeros((AH, 1), F32), dbx[:, 0, ::-1]], axis=1)
    return dx, dwin, dwout, (dgpre[0], dgpost[0], dwa[0:RANK], dba[0], dggla[0], dgatt[0], drb), got_a + got_b


def _rel_rows(rb):
    return rb[:, :0:-1][:, None, :]


def kernel(x, w_in, w_out, g_pre, g_post, w_alpha, b_alpha, g_gla, g_att, rel_bias, loss_target, m_w_in, m_w_out, m_g_pre, m_g_post, m_w_alpha, m_b_alpha, m_g_gla, m_g_att, m_rel_bias, v_w_in, v_w_out, v_g_pre, v_g_post, v_w_alpha, v_b_alpha, v_g_gla, v_g_att, v_rel_bias):
    nl = w_in.shape[0]
    ax, ay, ac = lax.axis_index("x"), lax.axis_index("y"), lax.axis_index("c")
    chip = 2 * ax + ay
    c_idx = jnp.reshape(ac, (1,)).astype(jnp.int32)
    chip_idx = jnp.reshape(chip, (1,)).astype(jnp.int32)

    wt_own = jnp.transpose(w_in, (2, 0, 1)).astype(_CDT).reshape(2, 2, SHARD // 4, nl, D)
    wout_own = w_out.astype(_CDT).reshape(nl, 2, 2, D // NCHIP // 4, D)

    def with_own(own, axis):
        shape = own.shape[:axis] + (NCHIP,) + own.shape[axis:]
        start = [0] * axis + [chip] + [0] * (own.ndim - axis)
        return lax.dynamic_update_slice(lax.empty(shape, own.dtype), jnp.expand_dims(own, axis), start)

    wt_all, wout_all, wa_all = _gather_weights(wt_own, wout_own, w_alpha, with_own(wt_own, 0), with_own(wout_own, 1),
                                               with_own(w_alpha, 0))
    wt_all = wt_all.reshape(DIN, nl, D)
    wout_all = wout_all.reshape(nl, D, D)
    wm, wga = zip(*[_rows_to_internal(wt_all[:, l]) for l in range(nl)])
    wa_full = jnp.transpose(wa_all, (1, 2, 0, 3)).reshape(nl, RANK, GH * GDK)
    wa_pad = jnp.pad(wa_full, ((0, 0), (0, GAP - RANK), (0, 0))).astype(_CDT)
    rbx = [_rel_rows(rel_bias[l]) for l in range(nl)]

    def weights(l):
        return (wm[l], wga[l], wout_all[l], g_pre[l][None], g_post[l][None], wa_pad[l], b_alpha[l][None],
                g_gla[l][None], g_att[l][None], rbx[l])

    h = x[0]
    saved = []
    for l in range(nl):
        h, sv = _layer_fwd(h, *weights(l))
        saved.append(sv)
    dout, loss_part = _loss_grad(h, loss_target[0])

    small, hin, hout = [None] * nl, [None] * nl, [None] * nl
    hw = D // NCHIP // 2

    def reduce_owner(sent, got):
        rin_a, rout_a, rin_b, rout_b = got
        return (_add_chips(chip_idx, sent[0], rin_a, rin_b, 48, "add_chips_in"),
                _add_chips(chip_idx, sent[1], rout_a, rout_b, 128, "add_chips_out"))

    sent = None
    for l in reversed(range(nl)):
        dout, dwin, dwout, small[l], got = _layer_bwd(dout, saved[l], *weights(l), send=sent)
        if sent is not None:
            hin[l + 1], hout[l + 1] = reduce_owner(sent, got)
        gt = _rows_from_internal(*dwin)
        slabs = jnp.stack([gt[_slab_lo(i):_slab_lo(i) + SLAB] for i in range(NCHIP)])
        gin2 = jnp.transpose(slabs.reshape(NCHIP, 2, HSLAB, D), (1, 0, 2, 3)).reshape(2, NCHIP * HSLAB, D)
        gout2 = jnp.transpose(dwout.reshape(NCHIP, 2, hw, D), (1, 0, 2, 3)).reshape(2, NCHIP * hw, D)
        rin, rout = _swap_halves(gin2, gout2)
        sent = (_add_halves(c_idx, gin2, rin, 192, "add_halves_in").reshape(1, NCHIP, HSLAB, D),
                _add_halves(c_idx, gout2, rout, 256, "add_halves_out").reshape(1, NCHIP, hw, D))
    grad_x = dout[None]
    hin[0], hout[0] = reduce_owner(sent, _send_to_owners(*sent))
    xchg = _exchange_halves(hin + hout)
    hin, xin = jnp.concatenate(hin), jnp.concatenate(xchg[:nl])
    hout, xout = jnp.concatenate(hout), jnp.concatenate(xchg[nl:])

    slab = jnp.concatenate([jnp.where(ac == 0, hin, xin), jnp.where(ac == 0, xin, hin)], axis=1)
    off = sum(jnp.where(chip == i, SHARD * i - _slab_lo(i), 0) for i in range(NCHIP))
    g_rows = jnp.transpose(lax.dynamic_slice_in_dim(slab, off, SHARD, axis=1), (1, 0, 2))
    rows_first = lambda t: jnp.transpose(t, (2, 0, 1))
    d_rows, nm_rows, nv_rows = _adam_rows(rows_first(w_in), g_rows, rows_first(m_w_in), rows_first(v_w_in), 32, "adam_w_in")
    g_w_in, d_w_in, nm_w_in, nv_w_in = [jnp.transpose(t, (1, 2, 0)) for t in (g_rows, d_rows, nm_rows, nv_rows)]

    def adam_big(w, g_own, g_other, m, v, name):
        shp = w.shape
        halves = lambda t: t.reshape(shp[0], 2, shp[1] // 2, shp[2])
        return [t.reshape(shp) for t in _adam_halves(c_idx, halves(w), g_own, g_other, halves(m), halves(v), 256, name)]

    g_w_out, d_w_out, nm_w_out, nv_w_out = adam_big(w_out, hout, xout, m_w_out, v_w_out, "adam_w_out")

    stacked = [jnp.stack([small[l][i] for l in range(nl)]) for i in range(7)] + [loss_part]
    g_small = _unpack_rows(_allreduce_small(_pack_rows(stacked)), [t.shape for t in stacked])
    g_gpre, g_gpost, g_wa_full, g_ba, g_ggla, g_gatt, g_rb, loss_sum = g_small
    loss = loss_sum[0, 0]
    g_wa = lax.dynamic_slice_in_dim(g_wa_full, chip * GDK, GDK, axis=2)
    names = [(g_pre, m_g_pre, v_g_pre, g_gpre), (g_post, m_g_post, v_g_post, g_gpost), (w_alpha, m_w_alpha, v_w_alpha, g_wa),
             (b_alpha, m_b_alpha, v_b_alpha, g_ba), (g_gla, m_g_gla, v_g_gla, g_ggla), (g_att, m_g_att, v_g_att, g_gatt),
             (rel_bias, m_rel_bias, v_rel_bias, g_rb)]
    shapes = [t[0].shape for t in names]
    packed = [_pack_rows([t[i] for t in names]) for i in range(4)]
    d_s, nm_s, nv_s = [_unpack_rows(t, shapes) for t in _adam(packed[0], packed[3], packed[1], packed[2], packed[0].shape[0], "adam_small")]

    grads = [g_w_in, g_w_out, g_gpre, g_gpost, g_wa, g_ba, g_ggla, g_gatt, g_rb]
    deltas = [d_w_in, d_w_out] + d_s
    new_m = [nm_w_in, nm_w_out] + nm_s
    new_v = [nv_w_in, nv_w_out] + nv_s
    return (loss, grad_x, *grads, *deltas, *new_m, *new_v)
```

```python
import functools

import jax
import jax.numpy as jnp
from jax import lax
from jax.experimental import pallas as pl
from jax.experimental.pallas import tpu as pltpu

D = 2048
DEPTH = 4
CHUNK = 64
GH, GDK, GDV = 4, 128, 256
DGLA = GH * GDV
RANK = 16
TAU = 16.0
AH, AHD = 8, 128
DATT = AH * AHD
LEFT = 8
NREL = 257
EPS = 1e-6
DIN = 7184
ADAM_LR, ADAM_B1, ADAM_B2, ADAM_EPS, ADAM_WD, ADAM_STEP = 0.001, 0.9, 0.999, 1e-08, 0.01, 10

GW = 2 * GDK + 2 * GDV
AW = 4 * AHD
ZG = GH * GW
ZA = AH * AW
ZM = ZG + ZA
GAP = 128
QB = 2 * CHUNK
HP = 2
BANDW = (LEFT + 2) * CHUNK
PADK = LEFT * CHUNK
NCHIP = 4
SHARD = DIN // NCHIP
SLAB = 1824
HSLAB = SLAB // 2
WSLOT = 1824
GPARTS_IN = [(0, 464), (464, 448)]
GPARTS_OUT = [(0, 128), (128, 128)]
SPLIT_IN, SPLIT_OUT = 336, 128
NEG = -1e30
F32 = jnp.float32
_CDT = jnp.bfloat16
_XDT = jnp.bfloat16
_VMEM = 56 * 1024 * 1024
MESH = pl.DeviceIdType.MESH
ANY = pl.BlockSpec(memory_space=pl.ANY)


def _dot(a, b):
    return jnp.dot(a, b, preferred_element_type=F32)


def _dot_nt(a, b):
    return lax.dot_general(a, b, (((1,), (1,)), ((), ())), preferred_element_type=F32)


def _dot_tn(a, b):
    return lax.dot_general(a, b, (((0,), (0,)), ((), ())), preferred_element_type=F32)


def _rms_rows(v):
    return lax.rsqrt(jnp.mean(v * v, axis=-1, keepdims=True) + EPS)


def _sigmoid(v):
    return 1.0 / (1.0 + jnp.exp(-v))


def _log_sigmoid(v):
    return jnp.minimum(v, 0.0) - jnp.log(1.0 + jnp.exp(-jnp.abs(v)))


def _exact_dot(tri, v):
    hi = v.astype(_CDT)
    r1 = v - hi.astype(F32)
    mid = r1.astype(_CDT)
    lo = (r1 - mid.astype(F32)).astype(_CDT)
    return _dot(tri, hi) + _dot(tri, mid) + _dot(tri, lo)


def _tri(strict):
    row = lax.broadcasted_iota(jnp.int32, (CHUNK, CHUNK), 0)
    col = lax.broadcasted_iota(jnp.int32, (CHUNK, CHUNK), 1)
    return jnp.where((col < row) if strict else (col <= row), 1.0, 0.0).astype(_CDT)


def _norm_gate_bwd(o, g, gate, dcat):
    r = _rms_rows(o)
    oh = o * r
    sg = _sigmoid(gate)
    dn = dcat * (gate * sg)
    dgate = dcat * (oh * g) * (sg * (1.0 + gate * (1.0 - sg)))
    dg = jnp.sum(dn * oh, axis=0, keepdims=True)
    dnn = dn * g
    do = r * (dnn - oh * jnp.mean(dnn * oh, axis=-1, keepdims=True))
    return do, dgate, dg


def _params(sem=None, vmem=_VMEM):
    return pltpu.CompilerParams(dimension_semantics=sem, vmem_limit_bytes=vmem)


def _inproj(x, g, wm, wga, ride=None, tm=512, tn=1024):
    s = x.shape[0]
    gr = _gather_ride(ride, 0)

    def body(*refs):
        (x_ref, g_ref, wm_ref, wga_ref, z_ref, zga_ref, h_ref, hs), rr = gr.split(refs, 4, 3)
        i, j = pl.program_id(0), pl.program_id(1)
        gr.start(rr, (i == 0) & (j == 0))

        @pl.when(pl.program_id(1) == 0)
        def _():
            xv = x_ref[...]
            hv = (xv * _rms_rows(xv) * g_ref[...]).astype(_CDT)
            hs[...] = hv
            h_ref[...] = hv
            zga_ref[...] = _dot_nt(hv, wga_ref[...]).astype(_CDT)

        z_ref[...] = _dot_nt(hs[...], wm_ref[...]).astype(_CDT)
        gr.wait(rr, (i == s // tm - 1) & (j == ZM // tn - 1))

    return pl.pallas_call(
        body, name="inproj", grid=(s // tm, ZM // tn),
        in_specs=[pl.BlockSpec((tm, D), lambda i, j: (i, 0)), pl.BlockSpec((1, D), lambda i, j: (0, 0)),
                  pl.BlockSpec((tn, D), lambda i, j: (j, 0)), pl.BlockSpec((GAP, D), lambda i, j: (0, 0))] + gr.in_specs,
        out_specs=[pl.BlockSpec((tm, tn), lambda i, j: (i, j)), pl.BlockSpec((tm, GAP), lambda i, j: (i, 0)),
                   pl.BlockSpec((tm, D), lambda i, j: (i, 0))] + gr.out_specs,
        out_shape=[jax.ShapeDtypeStruct((s, ZM), _CDT), jax.ShapeDtypeStruct((s, GAP), _CDT),
                   jax.ShapeDtypeStruct((s, D), _CDT)] + gr.out_shape,
        scratch_shapes=[pltpu.VMEM((tm, D), _CDT)] + gr.scratch, input_output_aliases=gr.alias(4, 3),
        compiler_params=_params(("arbitrary", "arbitrary")),
    )(x, g, wm, wga, *gr.operands)


def _gla_fwd(z, zga, wa, ba, ggla):
    s = z.shape[0]
    nc = s // CHUNK

    def body(zg_ref, zga_ref, wa_ref, ba_ref, g_ref, cat_ref, la_s, st):
        la_s[...] = _log_sigmoid(_dot(zga_ref[...], wa_ref[...]) + ba_ref[...]) * (1.0 / TAU)
        st[...] = jnp.zeros_like(st)
        tri = _tri(False)

        def step(n, carry):
            rows = pl.ds(pl.multiple_of(n * CHUNK, CHUNK), CHUNK)
            for p in range(HP):
                z0 = p * GW
                la = la_s[rows, p * GDK:(p + 1) * GDK]
                lc = _exact_dot(tri, la)
                lend = jnp.sum(la, axis=0, keepdims=True)
                kdec = (zg_ref[rows, z0 + GDK:z0 + 2 * GDK].astype(F32) * jnp.exp(lend - lc)).astype(_CDT)
                stn = jnp.exp(lend) * st[p] + _dot_tn(zg_ref[rows, z0 + 2 * GDK:z0 + 2 * GDK + GDV], kdec)
                st[p] = stn
                qs = (zg_ref[rows, z0:z0 + GDK].astype(F32) * (GDK ** -0.5)).astype(_CDT)
                o = _dot_nt(qs, stn.astype(_CDT))
                gate = zg_ref[rows, z0 + 2 * GDK + GDV:z0 + GW].astype(F32)
                gain = g_ref[:, p * GDV:(p + 1) * GDV]
                cat_ref[rows, p * GDV:(p + 1) * GDV] = (o * _rms_rows(o) * gain * (gate * _sigmoid(gate))).astype(_CDT)
            return carry

        lax.fori_loop(0, nc, step, 0)

    return pl.pallas_call(
        body, name="gla_fwd", grid=(GH // HP,),
        in_specs=[pl.BlockSpec((s, HP * GW), lambda h: (0, h)), pl.BlockSpec((s, GAP), lambda h: (0, 0)),
                  pl.BlockSpec((GAP, HP * GDK), lambda h: (0, h)), pl.BlockSpec((1, HP * GDK), lambda h: (0, h)),
                  pl.BlockSpec((1, HP * GDV), lambda h: (0, h))],
        out_specs=pl.BlockSpec((s, HP * GDV), lambda h: (0, h)),
        out_shape=jax.ShapeDtypeStruct((s, DGLA), _CDT),
        scratch_shapes=[pltpu.VMEM((s, HP * GDK), F32), pltpu.VMEM((HP, GDV, GDK), F32)],
        compiler_params=_params(("arbitrary",)),
    )(z, zga, wa, ba, ggla)


def _band_bias(b0):
    row = lax.broadcasted_iota(jnp.int32, (QB, 256), 0)
    col = lax.broadcasted_iota(jnp.int32, (QB, 256), 1)
    lane = lax.broadcasted_iota(jnp.int32, (1, 256), 1)
    c0 = jnp.sum(jnp.where(lane == 0, b0, 0.0), axis=1, keepdims=True)
    xv = jnp.broadcast_to(b0, (QB, 256))
    for bit in range(7):
        xv = jnp.where(((row >> bit) & 1) == 1, pltpu.roll(xv, 1 << bit, 1), xv)
    xv = jnp.where(col < row, c0, xv)
    return jnp.concatenate([jnp.broadcast_to(c0, (QB, BANDW - 256)), xv], axis=1)


def _band_static_mask():
    row = lax.broadcasted_iota(jnp.int32, (QB, BANDW), 0) >> 6
    col = lax.broadcasted_iota(jnp.int32, (QB, BANDW), 1) >> 6
    return (col >= row) & (col <= row + LEFT)


def _fold_bias_grad(t):
    row = lax.broadcasted_iota(jnp.int32, (QB, 256), 0)
    col = lax.broadcasted_iota(jnp.int32, (QB, 256), 1)
    xv = t[:, BANDW - 256:]
    low = col < row
    far = jnp.sum(t[:, 0:BANDW - 256], axis=1, keepdims=True) + jnp.sum(jnp.where(low, xv, 0.0), axis=1, keepdims=True)
    far = jnp.sum(far, axis=0, keepdims=True)
    xv = jnp.where(low, 0.0, xv)
    for bit in range(7):
        xv = jnp.where(((row >> bit) & 1) == 1, pltpu.roll(xv, 256 - (1 << bit), 1), xv)
    dp = jnp.sum(xv, axis=0, keepdims=True)
    lane = lax.broadcasted_iota(jnp.int32, (1, 256), 1)
    return dp + jnp.where(lane == 0, far, 0.0)


def _att_fwd(z, rbx, gatt, ride=None):
    s = z.shape[0]
    nb = s // QB
    gr = _gather_ride(ride, 1)

    def body(*refs):
        (za_ref, rb_ref, g_ref, cat_ref, o_ref, lse_ref, kp, vp, bias_s), rr = gr.split(refs, 3, 3)
        gr.start(rr, pl.program_id(0) == 0)
        for p in range(HP):
            z0 = p * AW
            kp[p, 0:PADK, :] = jnp.zeros((PADK, AHD), _CDT)
            vp[p, 0:PADK, :] = jnp.zeros((PADK, AHD), _CDT)
            kp[p, PADK:, :] = za_ref[:, z0 + AHD:z0 + 2 * AHD]
            vp[p, PADK:, :] = za_ref[:, z0 + 2 * AHD:z0 + 3 * AHD]
            bias_s[p] = jnp.where(_band_static_mask(), _band_bias(rb_ref[p]), NEG)

        def step(b, carry):
            r0 = pl.multiple_of(b * QB, QB)
            rows = pl.ds(r0, QB)
            band = pl.ds(r0, BANDW)
            live = lax.broadcasted_iota(jnp.int32, (QB, BANDW), 1) >= PADK - r0
            for p in range(HP):
                z0 = p * AW
                cols = slice(p * AHD, (p + 1) * AHD)
                sc = _dot_nt(za_ref[rows, z0:z0 + AHD], kp[p, band, :]) * (AHD ** -0.5) + bias_s[p]
                sc = jnp.where(live, sc, NEG)
                m = jnp.max(sc, axis=-1, keepdims=True)
                pr = jnp.exp(sc - m)
                l = jnp.sum(pr, axis=-1, keepdims=True)
                o = _dot((pr * (1.0 / l)).astype(_CDT), vp[p, band, :])
                o_ref[rows, cols] = o.astype(_CDT)
                lse_ref[rows, cols] = jnp.broadcast_to(m + jnp.log(l), (QB, AHD))
                gate = za_ref[rows, z0 + 3 * AHD:z0 + AW].astype(F32)
                cat_ref[rows, cols] = (o * _rms_rows(o) * g_ref[:, cols] * (gate * _sigmoid(gate))).astype(_CDT)
            return carry

        lax.fori_loop(0, nb, step, 0)
        gr.wait(rr, pl.program_id(0) == AH // HP - 1)

    return pl.pallas_call(
        body, name="att_fwd", grid=(AH // HP,),
        in_specs=[pl.BlockSpec((s, HP * AW), lambda h: (0, ZG // (HP * AW) + h)), pl.BlockSpec((HP, 1, 256), lambda h: (h, 0, 0)),
                  pl.BlockSpec((1, HP * AHD), lambda h: (0, h))] + gr.in_specs,
        out_specs=[pl.BlockSpec((s, HP * AHD), lambda h: (0, h)), pl.BlockSpec((s, HP * AHD), lambda h: (0, h)),
                   pl.BlockSpec((s, HP * AHD), lambda h: (0, h))] + gr.out_specs,
        out_shape=[jax.ShapeDtypeStruct((s, DATT), _CDT), jax.ShapeDtypeStruct((s, DATT), _CDT),
                   jax.ShapeDtypeStruct((s, DATT), F32)] + gr.out_shape,
        scratch_shapes=[pltpu.VMEM((HP, s + PADK, AHD), _CDT), pltpu.VMEM((HP, s + PADK, AHD), _CDT),
                        pltpu.VMEM((HP, QB, BANDW), F32)] + gr.scratch, input_output_aliases=gr.alias(3, 3),
        compiler_params=_params(("arbitrary",)),
    )(z, rbx, gatt, *gr.operands)


def _outproj(cg, ca, wout, x, gpost, ride=None, tm=256):
    s = x.shape[0]
    gr = _gather_ride(ride, 2)

    def body(*refs):
        (cg_ref, ca_ref, w_ref, x_ref, g_ref, y_ref, xo_ref), rr = gr.split(refs, 5, 2)
        gr.start(rr, pl.program_id(0) == 0)
        y = _dot(cg_ref[...], w_ref[0:DGLA, :]) + _dot(ca_ref[...], w_ref[DGLA:, :])
        y_ref[...] = y
        xo_ref[...] = x_ref[...] + y * _rms_rows(y) * g_ref[...]
        gr.wait(rr, pl.program_id(0) == s // tm - 1)

    return pl.pallas_call(
        body, name="outproj", grid=(s // tm,),
        in_specs=[pl.BlockSpec((tm, DGLA), lambda i: (i, 0)), pl.BlockSpec((tm, DATT), lambda i: (i, 0)),
                  pl.BlockSpec((D, D), lambda i: (0, 0)), pl.BlockSpec((tm, D), lambda i: (i, 0)),
                  pl.BlockSpec((1, D), lambda i: (0, 0))] + gr.in_specs,
        out_specs=[pl.BlockSpec((tm, D), lambda i: (i, 0)), pl.BlockSpec((tm, D), lambda i: (i, 0))] + gr.out_specs,
        out_shape=[jax.ShapeDtypeStruct((s, D), F32), jax.ShapeDtypeStruct((s, D), F32)] + gr.out_shape,
        scratch_shapes=gr.scratch, input_output_aliases=gr.alias(5, 2),
        compiler_params=_params(("arbitrary",)),
    )(cg, ca, wout, x, gpost, *gr.operands)


def _loss_grad(xo, tgt, tm=256):
    s = xo.shape[0]

    def body(xo_ref, t_ref, d_ref, l_ref):
        @pl.when(pl.program_id(0) == 0)
        def _():
            l_ref[...] = jnp.zeros_like(l_ref)

        e = xo_ref[...] - t_ref[...]
        d_ref[...] = e * (1.0 / D)
        l_ref[...] += jnp.sum(jnp.sum(e * e, axis=1, keepdims=True), axis=0, keepdims=True) * (0.5 / D)

    return pl.pallas_call(
        body, name="loss_grad", grid=(s // tm,),
        in_specs=[pl.BlockSpec((tm, D), lambda i: (i, 0)), pl.BlockSpec((tm, D), lambda i: (i, 0))],
        out_specs=[pl.BlockSpec((tm, D), lambda i: (i, 0)), pl.BlockSpec((1, 1), lambda i: (0, 0))],
        out_shape=[jax.ShapeDtypeStruct((s, D), F32), jax.ShapeDtypeStruct((1, 1), F32)],
        compiler_params=_params(("arbitrary",)),
    )(xo, tgt)


def _place():
    x, y, c = lax.axis_index("x"), lax.axis_index("y"), lax.axis_index("c")
    chips = [(1 - x, y), (x, 1 - y), (1 - x, 1 - y)]
    return x, y, c, chips


def _variants(fn):
    x, y, c, _ = _place()
    for jx in range(2):
        for jy in range(2):
            for jc in range(2):
                pl.when((x == jx) & (y == jy) & (c == jc))(functools.partial(fn, jx, jy, jc))


def _gather_copies(part, x, y, c, src_in, src_out, buf_in, buf_out, send_sems, recv_sems):
    me = 2 * x + y
    peers = [(1 - x, y), (x, 1 - y), (1 - x, 1 - y)]
    cps = []
    for k, (px, py) in enumerate(peers):
        for a, (src, buf, parts) in enumerate([(src_in, buf_in, GPARTS_IN), (src_out, buf_out, GPARTS_OUT)]):
            if part < 2:
                rows = pl.ds(*parts[part])
                pair = (src.at[c, rows], buf.at[me, c, rows], (px, py, c))
            else:
                piece = buf.at[2 * px + py, c]
                pair = (piece, piece, (x, y, 1 - c))
            cps.append(pltpu.make_async_remote_copy(src_ref=pair[0], dst_ref=pair[1], send_sem=send_sems.at[a, k],
                                                    recv_sem=recv_sems.at[a, k], device_id=pair[2], device_id_type=MESH))
    return cps


class _gather_ride:
    def __init__(self, ride, part):
        self.part, self.on = part, ride is not None
        self.in_specs, self.out_specs, self.out_shape, self.scratch, self.operands, self.aliases = [], [], [], [], [], {}
        if self.on:
            self.operands = list(ride)
            self.in_specs, self.out_specs = [ANY] * 4, [ANY] * 2
            self.out_shape = [jax.ShapeDtypeStruct(t.shape, t.dtype) for t in ride[2:]]
            self.scratch = [pltpu.SemaphoreType.DMA((2, 3)), pltpu.SemaphoreType.DMA((2, 3))]

    def alias(self, n_in, n_out):
        return {n_in + 2: n_out, n_in + 3: n_out + 1} if self.on else {}

    def split(self, refs, n_in, n_out):
        if not self.on:
            return refs, None
        own = refs[:n_in] + refs[n_in + 4:n_in + 4 + n_out] + refs[n_in + 6 + n_out:-2]
        return own, refs[n_in:n_in + 2] + refs[n_in + 4 + n_out:n_in + 6 + n_out] + refs[-2:]

    def start(self, ride_refs, first):
        if self.on:
            def go(x, y, c):
                for cpy in _gather_copies(self.part, x, y, c, *ride_refs):
                    cpy.start()
            pl.when(first)(lambda: _variants(go))

    def wait(self, ride_refs, last):
        if self.on:
            def done(x, y, c):
                for cpy in _gather_copies(self.part, x, y, c, *ride_refs):
                    cpy.wait()
            pl.when(last)(lambda: _variants(done))


def _part_rows(total, split, part):
    return (0, split) if part == 0 else (split, total - split)


def _owner_copies(part, pin_ref, pout_ref, rin, rout, send_sems, recv_sems):
    x, y, c, chips = _place()
    cps = []
    for k, (px, py) in enumerate(chips):
        for a, (src, dst, split) in enumerate([(pin_ref, rin, SPLIT_IN), (pout_ref, rout, SPLIT_OUT)]):
            r0, n = _part_rows(src.shape[2], split, part)
            cps.append(pltpu.make_async_remote_copy(src_ref=src.at[0, 2 * px + py, pl.ds(r0, n)], dst_ref=dst.at[k],
                                                    send_sem=send_sems.at[a, k], recv_sem=recv_sems.at[a, k],
                                                    device_id=(px, py, c), device_id_type=MESH))
    return cps


class _ride_specs:
    def __init__(self, send, part):
        self.in_specs, self.out_specs, self.out_shape, self.scratch, self.operands = [], [], [], [], []
        if send is not None:
            self.in_specs, self.out_specs, self.operands = [ANY, ANY], [ANY, ANY], list(send)
            self.out_shape = [jax.ShapeDtypeStruct((3, _part_rows(t.shape[2], split, part)[1], D), t.dtype)
                              for t, split in zip(send, (SPLIT_IN, SPLIT_OUT))]
            self.scratch = [pltpu.SemaphoreType.DMA((2, 3)), pltpu.SemaphoreType.DMA((2, 3))]


def _ride_refs(refs, send, n_out):
    if send is None:
        return None, refs
    pin_ref, pout_ref = refs[:2]
    own_out = refs[2:2 + n_out]
    rin, rout = refs[2 + n_out:4 + n_out]
    return (pin_ref, pout_ref, rin, rout, refs[-2], refs[-1]), tuple(own_out) + tuple(refs[4 + n_out:-2])


def _ride_start(refs, send, n_out, part):
    ride = _ride_refs(refs, send, n_out)[0]
    if ride is None:
        return []
    cps = _owner_copies(part, *ride)

    @pl.when(pl.program_id(0) == 0)
    def _():
        for cpy in cps:
            cpy.start()

    return cps


def _ride_wait(cps, steps):
    if cps:
        @pl.when(pl.program_id(0) == steps - 1)
        def _():
            for cpy in cps:
                cpy.wait()


def _post_bwd(dout, y, gpost, wout, tm=256):
    s = y.shape[0]

    def body(d_ref, y_ref, g_ref, w_ref, dy_ref, dcg_ref, dca_ref, dg_ref):
        @pl.when(pl.program_id(0) == 0)
        def _():
            dg_ref[...] = jnp.zeros_like(dg_ref)

        yv = y_ref[...]
        r = _rms_rows(yv)
        yh = yv * r
        dv = d_ref[...]
        dg_ref[...] += jnp.sum(dv * yh, axis=0, keepdims=True)
        dn = dv * g_ref[...]
        dyb = (r * (dn - yh * jnp.mean(dn * yh, axis=-1, keepdims=True))).astype(_CDT)
        dy_ref[...] = dyb
        dcg_ref[...] = _dot_nt(dyb, w_ref[0:DGLA, :]).astype(_CDT)
        dca_ref[...] = _dot_nt(dyb, w_ref[DGLA:, :]).astype(_CDT)

    return pl.pallas_call(
        body, name="post_bwd", grid=(s // tm,),
        in_specs=[pl.BlockSpec((tm, D), lambda i: (i, 0)), pl.BlockSpec((tm, D), lambda i: (i, 0)),
                  pl.BlockSpec((1, D), lambda i: (0, 0)), pl.BlockSpec((D, D), lambda i: (0, 0))],
        out_specs=[pl.BlockSpec((tm, D), lambda i: (i, 0)), pl.BlockSpec((tm, DGLA), lambda i: (i, 0)),
                   pl.BlockSpec((tm, DATT), lambda i: (i, 0)), pl.BlockSpec((1, D), lambda i: (0, 0))],
        out_shape=[jax.ShapeDtypeStruct((s, D), _CDT), jax.ShapeDtypeStruct((s, DGLA), _CDT),
                   jax.ShapeDtypeStruct((s, DATT), _CDT), jax.ShapeDtypeStruct((1, D), F32)],
        compiler_params=_params(("arbitrary",)),
    )(dout, y, gpost, wout)


def _matmul_tn(a, b, out_dtype, tm, tn, name):
    k, m = a.shape
    n = b.shape[1]

    def body(a_ref, b_ref, o_ref):
        o_ref[...] = _dot_tn(a_ref[...], b_ref[...]).astype(out_dtype)

    return pl.pallas_call(
        body, name=name, grid=(m // tm, n // tn),
        in_specs=[pl.BlockSpec((k, tm), lambda i, j: (0, i)), pl.BlockSpec((k, tn), lambda i, j: (0, j))],
        out_specs=pl.BlockSpec((tm, tn), lambda i, j: (i, j)),
        out_shape=jax.ShapeDtypeStruct((m, n), out_dtype),
        compiler_params=_params(("parallel", "parallel")),
    )(a, b)


def _att_bwd(z, oraw, lse, dca, rbx, gatt, send=None):
    s = z.shape[0]
    nb = s // QB

    def body(*refs):
        za_ref, o_ref, lse_ref, dc_ref, rb_ref, g_ref = refs[:6]
        dz_ref, dg_ref, db_ref, kp, vp, dkp, dvp, bias_s, t_s, dg_s = _ride_refs(refs[6:], send, 3)[1]
        cps = _ride_start(refs[6:], send, 3, 0)
        for p in range(HP):
            z0 = p * AW
            kp[p, 0:PADK, :] = jnp.zeros((PADK, AHD), _CDT)
            vp[p, 0:PADK, :] = jnp.zeros((PADK, AHD), _CDT)
            kp[p, PADK:, :] = za_ref[:, z0 + AHD:z0 + 2 * AHD]
            vp[p, PADK:, :] = za_ref[:, z0 + 2 * AHD:z0 + 3 * AHD]
            bias_s[p] = jnp.where(_band_static_mask(), _band_bias(rb_ref[p]), NEG)
        dkp[...] = jnp.zeros_like(dkp)
        dvp[...] = jnp.zeros_like(dvp)
        t_s[...] = jnp.zeros_like(t_s)
        dg_s[...] = jnp.zeros_like(dg_s)

        def step(b, carry):
            r0 = pl.multiple_of(b * QB, QB)
            rows = pl.ds(r0, QB)
            band = pl.ds(r0, BANDW)
            live = lax.broadcasted_iota(jnp.int32, (QB, BANDW), 1) >= PADK - r0
            for p in range(HP):
                z0 = p * AW
                cols = slice(p * AHD, (p + 1) * AHD)
                o = o_ref[rows, cols].astype(F32)
                do, dgate, dg = _norm_gate_bwd(o, g_ref[:, cols], za_ref[rows, z0 + 3 * AHD:z0 + AW].astype(F32),
                                               dc_ref[rows, cols].astype(F32))
                dg_s[:, cols] += dg
                q = za_ref[rows, z0:z0 + AHD]
                kb = kp[p, band, :]
                sc = _dot_nt(q, kb) * (AHD ** -0.5) + bias_s[p]
                sc = jnp.where(live, sc, NEG)
                pr = jnp.exp(sc - jnp.max(lse_ref[rows, cols], axis=-1, keepdims=True))
                dob = do.astype(_CDT)
                dp = _dot_nt(dob, vp[p, band, :])
                ds = pr * (dp - jnp.sum(do * o, axis=-1, keepdims=True))
                t_s[p] += ds
                dsb = (ds * (AHD ** -0.5)).astype(_CDT)
                dz_ref[rows, z0:z0 + AHD] = _dot(dsb, kb).astype(_CDT)
                dz_ref[rows, z0 + 3 * AHD:z0 + AW] = dgate.astype(_CDT)
                dkp[p, band, :] += _dot_tn(dsb, q)
                dvp[p, band, :] += _dot_tn(pr.astype(_CDT), dob)
            return carry

        lax.fori_loop(0, nb, step, 0)
        for p in range(HP):
            z0 = p * AW
            dz_ref[:, z0 + AHD:z0 + 2 * AHD] = dkp[p, PADK:, :].astype(_CDT)
            dz_ref[:, z0 + 2 * AHD:z0 + 3 * AHD] = dvp[p, PADK:, :].astype(_CDT)
            db_ref[p] = _fold_bias_grad(t_s[p])
        dg_ref[...] = dg_s[...]
        _ride_wait(cps, AH // HP)

    ride = _ride_specs(send, 0)
    return pl.pallas_call(
        body, name="att_bwd", grid=(AH // HP,),
        in_specs=[pl.BlockSpec((s, HP * AW), lambda h: (0, ZG // (HP * AW) + h)), pl.BlockSpec((s, HP * AHD), lambda h: (0, h)),
                  pl.BlockSpec((s, HP * AHD), lambda h: (0, h)), pl.BlockSpec((s, HP * AHD), lambda h: (0, h)),
                  pl.BlockSpec((HP, 1, 256), lambda h: (h, 0, 0)), pl.BlockSpec((1, HP * AHD), lambda h: (0, h))] + ride.in_specs,
        out_specs=[pl.BlockSpec((s, HP * AW), lambda h: (0, h)), pl.BlockSpec((1, HP * AHD), lambda h: (0, h)),
                   pl.BlockSpec((HP, 1, 256), lambda h: (h, 0, 0))] + ride.out_specs,
        out_shape=[jax.ShapeDtypeStruct((s, ZA), _CDT), jax.ShapeDtypeStruct((1, DATT), F32),
                   jax.ShapeDtypeStruct((AH, 1, 256), F32)] + ride.out_shape,
        scratch_shapes=[pltpu.VMEM((HP, s + PADK, AHD), _CDT), pltpu.VMEM((HP, s + PADK, AHD), _CDT),
                        pltpu.VMEM((HP, s + PADK, AHD), F32), pltpu.VMEM((HP, s + PADK, AHD), F32),
                        pltpu.VMEM((HP, QB, BANDW), F32), pltpu.VMEM((HP, QB, BANDW), F32), pltpu.VMEM((1, HP * AHD), F32)] + ride.scratch,
        compiler_params=_params(("arbitrary",)),
    )(z, oraw, lse, dca, rbx, gatt, *ride.operands)


def _gla_bwd(z, zga, wa, ba, ggla, dcg, send=None):
    s = z.shape[0]
    nc = s // CHUNK

    def body(*refs):
        zg_ref, zga_ref, wa_ref, ba_ref, g_ref, dc_ref = refs[:6]
        (dz_ref, dga_ref, dwa_ref, dba_ref, dg_ref,
         la_s, om_s, sall, dpre_s, c_s, dga_s, dg_s) = _ride_refs(refs[6:], send, 5)[1]
        cps = _ride_start(refs[6:], send, 5, 1)
        h = pl.program_id(0)
        pre = _dot(zga_ref[...], wa_ref[...]) + ba_ref[...]
        la_s[...] = _log_sigmoid(pre) * (1.0 / TAU)
        om_s[...] = (1.0 - _sigmoid(pre)) * (1.0 / TAU)
        c_s[...] = jnp.zeros_like(c_s)
        dg_s[...] = jnp.zeros_like(dg_s)
        tri = _tri(False)
        tri_strict = _tri(True)

        def decay(rows, p):
            la = la_s[rows, p * GDK:(p + 1) * GDK]
            lend = jnp.sum(la, axis=0, keepdims=True)
            return jnp.exp(lend - _exact_dot(tri, la)), jnp.exp(lend)

        def fwd(n, sts):
            rows = pl.ds(pl.multiple_of(n * CHUNK, CHUNK), CHUNK)
            out = []
            for p in range(HP):
                z0 = p * GW
                dec, a = decay(rows, p)
                kdec = (zg_ref[rows, z0 + GDK:z0 + 2 * GDK].astype(F32) * dec).astype(_CDT)
                stn = a * sts[p] + _dot_tn(zg_ref[rows, z0 + 2 * GDK:z0 + 2 * GDK + GDV], kdec)
                sall[p, n] = stn
                out.append(stn)
            return tuple(out)

        lax.fori_loop(0, nc, fwd, tuple(jnp.zeros((GDV, GDK), F32) for _ in range(HP)))

        def bwd(i, carry):
            n = nc - 1 - i
            rows = pl.ds(pl.multiple_of(n * CHUNK, CHUNK), CHUNK)
            for p in range(HP):
                z0 = p * GW
                kc = slice(p * GDK, (p + 1) * GDK)
                vc = slice(p * GDV, (p + 1) * GDV)
                dec, a = decay(rows, p)
                kdec = zg_ref[rows, z0 + GDK:z0 + 2 * GDK].astype(F32) * dec
                kdb = kdec.astype(_CDT)
                v = zg_ref[rows, z0 + 2 * GDK:z0 + 2 * GDK + GDV]
                qs = (zg_ref[rows, z0:z0 + GDK].astype(F32) * (GDK ** -0.5)).astype(_CDT)
                stb = sall[p, n].astype(_CDT)
                st_prev = sall[p, jnp.maximum(n - 1, 0)] * jnp.where(n > 0, 1.0, 0.0)
                o = _dot_nt(qs, stb)
                do, dgate, dg = _norm_gate_bwd(o, g_ref[:, vc], zg_ref[rows, z0 + 2 * GDK + GDV:z0 + GW].astype(F32),
                                               dc_ref[rows, vc].astype(F32))
                dg_s[:, vc] += dg
                dob = do.astype(_CDT)
                gt = _dot_tn(dob, qs) + c_s[p]
                gtb = gt.astype(_CDT)
                da = jnp.sum(gt * st_prev, axis=0, keepdims=True)
                dkdec = _dot(v, gtb)
                dla = _exact_dot(tri_strict, dkdec * kdec) + da * a
                dpre_s[rows, kc] = dla * om_s[rows, kc]
                dz_ref[rows, z0:z0 + GDK] = (_dot(dob, stb) * (GDK ** -0.5)).astype(_CDT)
                dz_ref[rows, z0 + GDK:z0 + 2 * GDK] = (dkdec * dec).astype(_CDT)
                dz_ref[rows, z0 + 2 * GDK:z0 + 2 * GDK + GDV] = _dot_nt(kdb, gtb).astype(_CDT)
                dz_ref[rows, z0 + 2 * GDK + GDV:z0 + GW] = dgate.astype(_CDT)
                c_s[p] = a * gt
            return carry

        lax.fori_loop(0, nc, bwd, 0)
        dpre = dpre_s[...]
        dpb = dpre.astype(_CDT)
        dg_ref[...] = dg_s[...]
        dba_ref[...] = jnp.sum(dpre, axis=0, keepdims=True)
        dwa_ref[...] = _dot_tn(zga_ref[...], dpb)
        part = _dot_nt(dpb, wa_ref[...])

        @pl.when(h == 0)
        def _():
            dga_s[...] = part

        @pl.when(h > 0)
        def _():
            dga_s[...] += part

        @pl.when(h == GH // HP - 1)
        def _():
            dga_ref[...] = dga_s[...].astype(_CDT)

        _ride_wait(cps, GH // HP)

    ride = _ride_specs(send, 1)
    return pl.pallas_call(
        body, name="gla_bwd", grid=(GH // HP,),
        in_specs=[pl.BlockSpec((s, HP * GW), lambda h: (0, h)), pl.BlockSpec((s, GAP), lambda h: (0, 0)),
                  pl.BlockSpec((GAP, HP * GDK), lambda h: (0, h)), pl.BlockSpec((1, HP * GDK), lambda h: (0, h)),
                  pl.BlockSpec((1, HP * GDV), lambda h: (0, h)), pl.BlockSpec((s, HP * GDV), lambda h: (0, h))] + ride.in_specs,
        out_specs=[pl.BlockSpec((s, HP * GW), lambda h: (0, h)), pl.BlockSpec((s, GAP), lambda h: (0, 0)),
                   pl.BlockSpec((GAP, HP * GDK), lambda h: (0, h)), pl.BlockSpec((1, HP * GDK), lambda h: (0, h)),
                   pl.BlockSpec((1, HP * GDV), lambda h: (0, h))] + ride.out_specs,
        out_shape=[jax.ShapeDtypeStruct((s, ZG), _CDT), jax.ShapeDtypeStruct((s, GAP), _CDT),
                   jax.ShapeDtypeStruct((GAP, GH * GDK), F32), jax.ShapeDtypeStruct((1, GH * GDK), F32),
                   jax.ShapeDtypeStruct((1, DGLA), F32)] + ride.out_shape,
        scratch_shapes=[pltpu.VMEM((s, HP * GDK), F32), pltpu.VMEM((s, HP * GDK), F32), pltpu.VMEM((HP, nc, GDV, GDK), F32),
                        pltpu.VMEM((s, HP * GDK), F32), pltpu.VMEM((HP, GDV, GDK), F32), pltpu.VMEM((s, GAP), F32),
                        pltpu.VMEM((1, HP * GDV), F32)] + ride.scratch,
        compiler_params=_params(("arbitrary",)),
    )(z, zga, wa, ba, ggla, dcg, *ride.operands)


def _dh(dzg, dza, dga, wm, wga, x, dout, gpre, tm=512, tk=1024):
    s = x.shape[0]
    nkg, nk = ZG // tk, ZM // tk

    def body(dzg_ref, dza_ref, dga_ref, wm_ref, wga_ref, x_ref, d_ref, g_ref, dx_ref, dg_ref, acc):
        i, k = pl.program_id(0), pl.program_id(1)

        @pl.when((i == 0) & (k == 0))
        def _():
            dg_ref[...] = jnp.zeros_like(dg_ref)

        @pl.when(k == 0)
        def _():
            acc[...] = _dot(dga_ref[...], wga_ref[...])

        @pl.when(k < nkg)
        def _():
            acc[...] += _dot(dzg_ref[...], wm_ref[...])

        @pl.when(k >= nkg)
        def _():
            acc[...] += _dot(dza_ref[...], wm_ref[...])

        @pl.when(k == nk - 1)
        def _():
            xv = x_ref[...]
            r = _rms_rows(xv)
            xh = xv * r
            dh = acc[...]
            dg_ref[...] += jnp.sum(dh * xh, axis=0, keepdims=True)
            dn = dh * g_ref[...]
            dx_ref[...] = d_ref[...] + r * (dn - xh * jnp.mean(dn * xh, axis=-1, keepdims=True))

    return pl.pallas_call(
        body, name="dh", grid=(s // tm, nk),
        in_specs=[pl.BlockSpec((tm, tk), lambda i, k: (i, jnp.minimum(k, nkg - 1))),
                  pl.BlockSpec((tm, tk), lambda i, k: (i, jnp.maximum(k - nkg, 0))),
                  pl.BlockSpec((tm, GAP), lambda i, k: (i, 0)), pl.BlockSpec((tk, D), lambda i, k: (k, 0)),
                  pl.BlockSpec((GAP, D), lambda i, k: (0, 0)), pl.BlockSpec((tm, D), lambda i, k: (i, 0)),
                  pl.BlockSpec((tm, D), lambda i, k: (i, 0)), pl.BlockSpec((1, D), lambda i, k: (0, 0))],
        out_specs=[pl.BlockSpec((tm, D), lambda i, k: (i, 0)), pl.BlockSpec((1, D), lambda i, k: (0, 0))],
        out_shape=[jax.ShapeDtypeStruct((s, D), F32), jax.ShapeDtypeStruct((1, D), F32)],
        scratch_shapes=[pltpu.VMEM((tm, D), F32)],
        compiler_params=_params(("arbitrary", "arbitrary")),
    )(dzg, dza, dga, wm, wga, x, dout, gpre)


def _adam(w, g, m, v, tr, name):
    rws, cols = w.shape

    def body(w_ref, g_ref, m_ref, v_ref, d_ref, mo_ref, vo_ref):
        gv = g_ref[...]
        mn = ADAM_B1 * m_ref[...] + (1.0 - ADAM_B1) * gv
        vn = ADAM_B2 * v_ref[...] + (1.0 - ADAM_B2) * (gv * gv)
        mh = mn / (1.0 - ADAM_B1 ** ADAM_STEP)
        vh = vn / (1.0 - ADAM_B2 ** ADAM_STEP)
        d_ref[...] = -ADAM_LR * (mh / (jnp.sqrt(vh) + ADAM_EPS) + ADAM_WD * w_ref[...])
        mo_ref[...] = mn
        vo_ref[...] = vn

    spec = pl.BlockSpec((tr, cols), lambda i: (i, 0))
    return pl.pallas_call(
        body, name=name, grid=(rws // tr,), in_specs=[spec] * 4, out_specs=[spec] * 3,
        out_shape=[jax.ShapeDtypeStruct((rws, cols), F32)] * 3,
        compiler_params=_params(("parallel",)),
    )(w, g, m, v)


def _adam_rows(w, g, m, v, tj, name):
    rows = w.shape[0]

    def body(w_ref, g_ref, m_ref, v_ref, d_ref, mo_ref, vo_ref):
        gv = g_ref[...]
        mn = ADAM_B1 * m_ref[...] + (1.0 - ADAM_B1) * gv
        vn = ADAM_B2 * v_ref[...] + (1.0 - ADAM_B2) * (gv * gv)
        mh = mn / (1.0 - ADAM_B1 ** ADAM_STEP)
        vh = vn / (1.0 - ADAM_B2 ** ADAM_STEP)
        d_ref[...] = -ADAM_LR * (mh / (jnp.sqrt(vh) + ADAM_EPS) + ADAM_WD * w_ref[...])
        mo_ref[...] = mn
        vo_ref[...] = vn

    spec = pl.BlockSpec((tj,) + w.shape[1:], lambda i: (i, 0, 0))
    return pl.pallas_call(
        body, name=name, grid=(pl.cdiv(rows, tj),), in_specs=[spec] * 4, out_specs=[spec] * 3,
        out_shape=[jax.ShapeDtypeStruct(w.shape, F32)] * 3,
        compiler_params=_params(("parallel",)),
    )(w, g, m, v)


def _gather_first(src_in, src_out, wa, buf_in, buf_out, wa_all):
    def variant(x, y, c, src_in_ref, src_out_ref, wa_ref, _b0, _b1, _b2, bin_ref, bout_ref, wa_ref_all, send_sems, recv_sems, wa_send, wa_recv):
        direct = []
        for part in range(2):
            direct += _gather_copies(part, x, y, c, src_in_ref, src_out_ref, bin_ref, bout_ref, send_sems.at[part], recv_sems.at[part])
        for k, (px, py) in enumerate([(1 - x, y), (x, 1 - y), (1 - x, 1 - y)]):
            direct.append(pltpu.make_async_remote_copy(src_ref=wa_ref, dst_ref=wa_ref_all.at[2 * x + y], send_sem=wa_send.at[k],
                                                       recv_sem=wa_recv.at[k], device_id=(px, py, c), device_id_type=MESH))
        for cpy in direct:
            cpy.start()
        for cpy in direct:
            cpy.wait()
        passed = _gather_copies(2, x, y, c, src_in_ref, src_out_ref, bin_ref, bout_ref, send_sems.at[2], recv_sems.at[2])
        for cpy in passed:
            cpy.start()
        for cpy in passed:
            cpy.wait()

    def body(*refs):
        _variants(lambda x, y, c: variant(x, y, c, *refs))

    return pl.pallas_call(
        body, name="gather_first", in_specs=[ANY] * 6, out_specs=[ANY] * 3, input_output_aliases={3: 0, 4: 1, 5: 2},
        out_shape=[jax.ShapeDtypeStruct(t.shape, t.dtype) for t in (buf_in, buf_out, wa_all)],
        scratch_shapes=[pltpu.SemaphoreType.DMA((3, 2, 3)), pltpu.SemaphoreType.DMA((3, 2, 3)),
                        pltpu.SemaphoreType.DMA((3,)), pltpu.SemaphoreType.DMA((3,))],
    )(src_in, src_out, wa, buf_in, buf_out, wa_all)


def _swap_halves(gin2, gout2):
    def body(gin_ref, gout_ref, rin, rout, send_sems, recv_sems):
        x, y, c, _ = _place()
        sib = (x, y, 1 - c)
        cps = [pltpu.make_async_remote_copy(src_ref=src.at[1 - c], dst_ref=dst, send_sem=send_sems.at[a],
                                            recv_sem=recv_sems.at[a], device_id=sib, device_id_type=MESH)
               for a, (src, dst) in enumerate([(gin_ref, rin), (gout_ref, rout)])]
        for cpy in cps:
            cpy.start()
        for cpy in cps:
            cpy.wait()

    return pl.pallas_call(
        body, name="swap_halves", in_specs=[ANY, ANY], out_specs=[ANY, ANY],
        out_shape=[jax.ShapeDtypeStruct(gin2.shape[1:], gin2.dtype), jax.ShapeDtypeStruct(gout2.shape[1:], gout2.dtype)],
        scratch_shapes=[pltpu.SemaphoreType.DMA((2,)), pltpu.SemaphoreType.DMA((2,))],
    )(gin2, gout2)


def _add_halves(c_idx, g2, r, tr, name):
    rows, cols = r.shape

    def body(c_ref, g_ref, r_ref, o_ref):
        o_ref[...] = (g_ref[0].astype(F32) + r_ref[...].astype(F32)).astype(_XDT)

    return pl.pallas_call(
        body, name=name,
        grid_spec=pltpu.PrefetchScalarGridSpec(
            num_scalar_prefetch=1, grid=(rows // tr,),
            in_specs=[pl.BlockSpec((1, tr, cols), lambda i, c_ref: (c_ref[0], i, 0)),
                      pl.BlockSpec((tr, cols), lambda i, c_ref: (i, 0))],
            out_specs=pl.BlockSpec((tr, cols), lambda i, c_ref: (i, 0))),
        out_shape=jax.ShapeDtypeStruct((rows, cols), _XDT),
        compiler_params=_params(("parallel",)),
    )(c_idx, g2, r)


def _send_to_owners(pin, pout):
    def body(pin_ref, pout_ref, rin_a, rout_a, rin_b, rout_b, send_sems, recv_sems):
        cps = (_owner_copies(0, pin_ref, pout_ref, rin_a, rout_a, send_sems.at[0], recv_sems.at[0])
               + _owner_copies(1, pin_ref, pout_ref, rin_b, rout_b, send_sems.at[1], recv_sems.at[1]))
        for cpy in cps:
            cpy.start()
        for cpy in cps:
            cpy.wait()

    shapes = [jax.ShapeDtypeStruct((3, _part_rows(t.shape[2], split, part)[1], D), t.dtype)
              for part in range(2) for t, split in zip((pin, pout), (SPLIT_IN, SPLIT_OUT))]
    return pl.pallas_call(
        body, name="send_to_owners", in_specs=[ANY, ANY], out_specs=[ANY] * 4, out_shape=shapes,
        scratch_shapes=[pltpu.SemaphoreType.DMA((2, 2, 3)), pltpu.SemaphoreType.DMA((2, 2, 3))],
    )(pin, pout)


def _add_chips(chip_idx, p, ra, rb, tr, name):
    rows = p.shape[2]
    na = ra.shape[1] // tr

    def body(c_ref, p_ref, ra_ref, rb_ref, o_ref):
        r = jnp.where(pl.program_id(0) < na, ra_ref[...], rb_ref[...]).astype(F32)
        o_ref[0] = ((p_ref[0, 0].astype(F32) + r[0]) + r[1]) + r[2]

    return pl.pallas_call(
        body, name=name,
        grid_spec=pltpu.PrefetchScalarGridSpec(
            num_scalar_prefetch=1, grid=(rows // tr,),
            in_specs=[pl.BlockSpec((1, 1, tr, D), lambda i, c_ref: (0, c_ref[0], i, 0)),
                      pl.BlockSpec((3, tr, D), lambda i, c_ref: (0, jnp.minimum(i, na - 1), 0)),
                      pl.BlockSpec((3, tr, D), lambda i, c_ref: (0, jnp.maximum(i - na, 0), 0))],
            out_specs=pl.BlockSpec((1, tr, D), lambda i, c_ref: (0, i, 0))),
        out_shape=jax.ShapeDtypeStruct((1, rows, D), F32),
        compiler_params=_params(("parallel",)),
    )(chip_idx, p, ra, rb)


def _exchange_halves(arrs):
    n = len(arrs)

    def body(*refs):
        x, y, c, _ = _place()
        cps = [pltpu.make_async_remote_copy(src_ref=refs[a], dst_ref=refs[n + a], send_sem=refs[2 * n].at[a], recv_sem=refs[2 * n + 1].at[a],
                                            device_id=(x, y, 1 - c), device_id_type=MESH) for a in range(n)]
        for cpy in cps:
            cpy.start()
        for cpy in cps:
            cpy.wait()

    return pl.pallas_call(
        body, name="exchange_halves", in_specs=[ANY] * n, out_specs=[ANY] * n,
        out_shape=[jax.ShapeDtypeStruct(t.shape, t.dtype) for t in arrs],
        scratch_shapes=[pltpu.SemaphoreType.DMA((n,)), pltpu.SemaphoreType.DMA((n,))],
    )(*arrs)


def _adam_halves(c_idx, w, g_own, g_other, m, v, tr, name):
    nl, _, rows, cols = w.shape

    def body(c_ref, w_ref, go_ref, gx_ref, m_ref, v_ref, g_ref, d_ref, mo_ref, vo_ref):
        gv = jnp.where(pl.program_id(1) == c_ref[0], go_ref[0], gx_ref[0])
        mn = ADAM_B1 * m_ref[0, 0] + (1.0 - ADAM_B1) * gv
        vn = ADAM_B2 * v_ref[0, 0] + (1.0 - ADAM_B2) * (gv * gv)
        mh = mn / (1.0 - ADAM_B1 ** ADAM_STEP)
        vh = vn / (1.0 - ADAM_B2 ** ADAM_STEP)
        g_ref[0, 0] = gv
        d_ref[0, 0] = -ADAM_LR * (mh / (jnp.sqrt(vh) + ADAM_EPS) + ADAM_WD * w_ref[0, 0])
        mo_ref[0, 0] = mn
        vo_ref[0, 0] = vn

    full = pl.BlockSpec((1, 1, tr, cols), lambda l, hh, i, c_ref: (l, hh, i, 0))
    own = pl.BlockSpec((1, tr, cols), lambda l, hh, i, c_ref: (l, jnp.where(hh == c_ref[0], i, 0), 0))
    other = pl.BlockSpec((1, tr, cols), lambda l, hh, i, c_ref: (l, jnp.where(hh == c_ref[0], 0, i), 0))
    return pl.pallas_call(
        body, name=name,
        grid_spec=pltpu.PrefetchScalarGridSpec(
            num_scalar_prefetch=1, grid=(nl, 2, rows // tr),
            in_specs=[full, own, other, full, full], out_specs=[full] * 4),
        out_shape=[jax.ShapeDtypeStruct(w.shape, F32)] * 4,
        compiler_params=_params(("parallel", "parallel", "parallel")),
    )(c_idx, w, g_own, g_other, m, v)


def _allreduce_small(sg):
    rows = sg.shape[0]
    vm = pl.BlockSpec(memory_space=pltpu.VMEM)

    def body(sg_ref, tot_ref, all_ref, send_sems, recv_sems):
        x, y, c, _ = _place()
        me = 4 * x + 2 * y + c
        all_ref[me] = sg_ref[...]
        cps = []
        for mask in range(1, 8):
            to = (1 - x if mask & 4 else x, 1 - y if mask & 2 else y, 1 - c if mask & 1 else c)
            cps.append(pltpu.make_async_remote_copy(src_ref=sg_ref, dst_ref=all_ref.at[me], send_sem=send_sems.at[mask - 1],
                                                    recv_sem=recv_sems.at[mask - 1], device_id=to, device_id_type=MESH))
        for cpy in cps:
            cpy.start()
        for cpy in cps:
            cpy.wait()
        acc = all_ref[0]
        for d in range(1, 8):
            acc = acc + all_ref[d]
        tot_ref[...] = acc

    return pl.pallas_call(
        body, name="allreduce_small", in_specs=[vm], out_specs=[vm, vm],
        out_shape=[jax.ShapeDtypeStruct((rows, 128), F32), jax.ShapeDtypeStruct((8, rows, 128), F32)],
        scratch_shapes=[pltpu.SemaphoreType.DMA((7,)), pltpu.SemaphoreType.DMA((7,))],
        compiler_params=_params(),
    )(sg)[0]


_CUTS = [0, 512, 1024, 2048, 3072, 3088, 4112, 5136, 6160, 7184]


def _rows_to_internal(w):
    tail = w.shape[1:]
    gq, gk, gv, gg, ga, aq, ak, av, ag = [w[_CUTS[i]:_CUTS[i + 1]] for i in range(9)]
    g = jnp.concatenate([gq.reshape((GH, GDK) + tail), gk.reshape((GH, GDK) + tail),
                         gv.reshape((GH, GDV) + tail), gg.reshape((GH, GDV) + tail)], axis=1).reshape((ZG,) + tail)
    a = jnp.concatenate([t.reshape((AH, AHD) + tail) for t in (aq, ak, av, ag)], axis=1).reshape((ZA,) + tail)
    pad = [(0, GAP - RANK)] + [(0, 0)] * len(tail)
    return jnp.concatenate([g, a], axis=0), jnp.pad(ga, pad)


def _rows_from_internal(g, a, ga):
    tail = g.shape[1:]
    g = g.reshape((GH, GW) + tail)
    a = a.reshape((AH, AW) + tail)
    parts = [g[:, 0:GDK], g[:, GDK:2 * GDK], g[:, 2 * GDK:2 * GDK + GDV], g[:, 2 * GDK + GDV:GW]]
    parts = [t.reshape((-1,) + tail) for t in parts] + [ga[0:RANK]]
    parts += [a[:, i * AHD:(i + 1) * AHD].reshape((-1,) + tail) for i in range(4)]
    return jnp.concatenate(parts, axis=0)


def _slab_lo(chip):
    return min(SHARD * chip // 16 * 16, DIN - SLAB)


def _pack_rows(parts):
    rows = []
    for t in parts:
        flat = t.reshape(-1)
        rows.append(jnp.pad(flat, (0, (-flat.shape[0]) % 128)).reshape(-1, 128))
    buf = jnp.concatenate(rows, axis=0)
    return jnp.pad(buf, ((0, (-buf.shape[0]) % 8), (0, 0)))


def _unpack_rows(buf, shapes):
    out, r = [], 0
    for shp in shapes:
        n = 1
        for d in shp:
            n *= d
        nr = -(-n // 128)
        out.append(buf[r:r + nr].reshape(-1)[:n].reshape(shp))
        r += nr
    return out


def _layer_fwd(x, wm, wga, wout, gpre, gpost, wa, ba, ggla, gatt, rbx, ride=None):
    z, zga, h, *bufs = _inproj(x, gpre, wm, wga, ride)
    ride = None if ride is None else (ride[0], ride[1], *bufs)
    cg = _gla_fwd(z, zga, wa, ba, ggla)
    ca, oraw, lse, *bufs = _att_fwd(z, rbx, gatt, ride)
    ride = None if ride is None else (ride[0], ride[1], *bufs)
    y, xo, *bufs = _outproj(cg, ca, wout, x, gpost, ride)
    return xo, (x, z, zga, h, cg, ca, oraw, lse, y), bufs


def _layer_bwd(dout, saved, wm, wga, wout, gpre, gpost, wa, ba, ggla, gatt, rbx, send=None):
    x, z, zga, h, cg, ca, oraw, lse, y = saved
    dy, dcg, dca, dgpost = _post_bwd(dout, y, gpost, wout)
    dwout = jnp.concatenate([_matmul_tn(cg, dy, _XDT, 512, 1024, "dwout_gla"),
                             _matmul_tn(ca, dy, _XDT, 512, 1024, "dwout_att")], axis=0)
    dza, dgatt, dbx, *got_a = _att_bwd(z, oraw, lse, dca, rbx, gatt, send)
    dzg, dga, dwa, dba, dggla, *got_b = _gla_bwd(z, zga, wa, ba, ggla, dcg, send)
    dx, dgpre = _dh(dzg, dza, dga, wm, wga, x, dout, gpre)
    dwin = (_matmul_tn(dzg, h, _XDT, 512, 1024, "dwin_gla"), _matmul_tn(dza, h, _XDT, 512, 1024, "dwin_att"),
            _matmul_tn(dga, h, _XDT, GAP, 1024, "dwin_gate"))
    drb = jnp.concatenate([jnp.zeros((AH, 1), F32), dbx[:, 0, ::-1]], axis=1)
    return dx, dwin, dwout, (dgpre[0], dgpost[0], dwa[0:RANK], dba[0], dggla[0], dgatt[0], drb), got_a + got_b


def _rel_rows(rb):
    return rb[:, :0:-1][:, None, :]


def kernel(x, w_in, w_out, g_pre, g_post, w_alpha, b_alpha, g_gla, g_att, rel_bias, loss_target, m_w_in, m_w_out, m_g_pre, m_g_post, m_w_alpha, m_b_alpha, m_g_gla, m_g_att, m_rel_bias, v_w_in, v_w_out, v_g_pre, v_g_post, v_w_alpha, v_b_alpha, v_g_gla, v_g_att, v_rel_bias):
    nl = w_in.shape[0]
    ax, ay, ac = lax.axis_index("x"), lax.axis_index("y"), lax.axis_index("c")
    chip = 2 * ax + ay
    c_idx = jnp.reshape(ac, (1,)).astype(jnp.int32)
    chip_idx = jnp.reshape(chip, (1,)).astype(jnp.int32)

    phase = [SHARD * i % 16 for i in range(NCHIP)]
    wt_rows = jnp.transpose(w_in, (0, 2, 1)).astype(_CDT)
    my_phase = sum(jnp.where(chip == i, phase[i], 0) for i in range(NCHIP))
    wt_src = lax.dynamic_update_slice(lax.empty((nl, WSLOT, D), _CDT), wt_rows, (0, my_phase, 0)).reshape(nl, 2, WSLOT // 2, D)
    wout_src = w_out.astype(_CDT).reshape(nl, 2, D // NCHIP // 2, D)

    def with_own(own):
        start = [chip] + [0] * own.ndim
        return lax.dynamic_update_slice(lax.empty((NCHIP,) + own.shape, own.dtype), own[None], start)

    def gather_operands(l):
        return wt_src[l], wout_src[l], with_own(wt_src[l]), with_own(wout_src[l])

    def layer_weights(bufs):
        wt4 = bufs[0].reshape(NCHIP, WSLOT, D)
        wref = jnp.concatenate([wt4[i, phase[i]:phase[i] + SHARD] for i in range(NCHIP)])
        return _rows_to_internal(wref) + (bufs[1].reshape(D, D),)

    first = gather_operands(0)
    bin0, bout0, wa_all = _gather_first(first[0], first[1], w_alpha, first[2], first[3], with_own(w_alpha))
    wa_full = jnp.transpose(wa_all, (1, 2, 0, 3)).reshape(nl, RANK, GH * GDK)
    wa_pad = jnp.pad(wa_full, ((0, 0), (0, GAP - RANK), (0, 0))).astype(_CDT)
    rbx = [_rel_rows(rel_bias[l]) for l in range(nl)]

    def weights(l):
        return big[l] + (g_pre[l][None], g_post[l][None], wa_pad[l], b_alpha[l][None], g_gla[l][None], g_att[l][None], rbx[l])

    h = x[0]
    saved, big = [], [None] * nl
    big[0] = layer_weights((bin0, bout0))
    for l in range(nl):
        h, sv, bufs = _layer_fwd(h, *weights(l), ride=gather_operands(l + 1) if l + 1 < nl else None)
        saved.append(sv)
        if l + 1 < nl:
            big[l + 1] = layer_weights(bufs)
    dout, loss_part = _loss_grad(h, loss_target[0])

    small, hin, hout = [None] * nl, [None] * nl, [None] * nl
    hw = D // NCHIP // 2

    def reduce_owner(sent, got):
        rin_a, rout_a, rin_b, rout_b = got
        return (_add_chips(chip_idx, sent[0], rin_a, rin_b, 48, "add_chips_in"),
                _add_chips(chip_idx, sent[1], rout_a, rout_b, 128, "add_chips_out"))

    sent = None
    for l in reversed(range(nl)):
        dout, dwin, dwout, small[l], got = _layer_bwd(dout, saved[l], *weights(l), send=sent)
        if sent is not None:
            hin[l + 1], hout[l + 1] = reduce_owner(sent, got)
        gt = _rows_from_internal(*dwin)
        slabs = jnp.stack([gt[_slab_lo(i):_slab_lo(i) + SLAB] for i in range(NCHIP)])
        gin2 = jnp.transpose(slabs.reshape(NCHIP, 2, HSLAB, D), (1, 0, 2, 3)).reshape(2, NCHIP * HSLAB, D)
        gout2 = jnp.transpose(dwout.reshape(NCHIP, 2, hw, D), (1, 0, 2, 3)).reshape(2, NCHIP * hw, D)
        rin, rout = _swap_halves(gin2, gout2)
        sent = (_add_halves(c_idx, gin2, rin, 192, "add_halves_in").reshape(1, NCHIP, HSLAB, D),
                _add_halves(c_idx, gout2, rout, 256, "add_halves_out").reshape(1, NCHIP, hw, D))
    grad_x = dout[None]
    hin[0], hout[0] = reduce_owner(sent, _send_to_owners(*sent))
    xchg = _exchange_halves(hin + hout)
    hin, xin = jnp.concatenate(hin), jnp.concatenate(xchg[:nl])
    hout, xout = jnp.concatenate(hout), jnp.concatenate(xchg[nl:])

    slab = jnp.concatenate([jnp.where(ac == 0, hin, xin), jnp.where(ac == 0, xin, hin)], axis=1)
    off = sum(jnp.where(chip == i, SHARD * i - _slab_lo(i), 0) for i in range(NCHIP))
    g_rows = jnp.transpose(lax.dynamic_slice_in_dim(slab, off, SHARD, axis=1), (1, 0, 2))
    rows_first = lambda t: jnp.transpose(t, (2, 0, 1))
    d_rows, nm_rows, nv_rows = _adam_rows(rows_first(w_in), g_rows, rows_first(m_w_in), rows_first(v_w_in), 32, "adam_w_in")
    g_w_in, d_w_in, nm_w_in, nv_w_in = [jnp.transpose(t, (1, 2, 0)) for t in (g_rows, d_rows, nm_rows, nv_rows)]

    def adam_big(w, g_own, g_other, m, v, name):
        shp = w.shape
        halves = lambda t: t.reshape(shp[0], 2, shp[1] // 2, shp[2])
        return [t.reshape(shp) for t in _adam_halves(c_idx, halves(w), g_own, g_other, halves(m), halves(v), 256, name)]

    g_w_out, d_w_out, nm_w_out, nv_w_out = adam_big(w_out, hout, xout, m_w_out, v_w_out, "adam_w_out")

    stacked = [jnp.stack([small[l][i] for l in range(nl)]) for i in range(7)] + [loss_part]
    g_small = _unpack_rows(_allreduce_small(_pack_rows(stacked)), [t.shape for t in stacked])
    g_gpre, g_gpost, g_wa_full, g_ba, g_ggla, g_gatt, g_rb, loss_sum = g_small
    loss = loss_sum[0, 0]
    g_wa = lax.dynamic_slice_in_dim(g_wa_full, chip * GDK, GDK, axis=2)
    names = [(g_pre, m_g_pre, v_g_pre, g_gpre), (g_post, m_g_post, v_g_post, g_gpost), (w_alpha, m_w_alpha, v_w_alpha, g_wa),
             (b_alpha, m_b_alpha, v_b_alpha, g_ba), (g_gla, m_g_gla, v_g_gla, g_ggla), (g_att, m_g_att, v_g_att, g_gatt),
             (rel_bias, m_rel_bias, v_rel_bias, g_rb)]
    shapes = [t[0].shape for t in names]
    packed = [_pack_rows([t[i] for t in names]) for i in range(4)]
    d_s, nm_s, nv_s = [_unpack_rows(t, shapes) for t in _adam(packed[0], packed[3], packed[1], packed[2], packed[0].shape[0], "adam_small")]

    grads = [g_w_in, g_w_out, g_gpre, g_gpost, g_wa, g_ba, g_ggla, g_gatt, g_rb]
    deltas = [d_w_in, d_w_out] + d_s
    new_m = [nm_w_in, nm_w_out] + nm_s
    new_v = [nv_w_in, nv_w_out] + nv_s
    return (loss, grad_x, *grads, *deltas, *new_m, *new_v)
```

```python
import functools

import jax
import jax.numpy as jnp
from jax import lax
from jax.experimental import pallas as pl
from jax.experimental.pallas import tpu as pltpu

D = 2048
DEPTH = 4
CHUNK = 64
GH, GDK, GDV = 4, 128, 256
DGLA = GH * GDV
RANK = 16
TAU = 16.0
AH, AHD = 8, 128
DATT = AH * AHD
LEFT = 8
NREL = 257
EPS = 1e-6
DIN = 7184
ADAM_LR, ADAM_B1, ADAM_B2, ADAM_EPS, ADAM_WD, ADAM_STEP = 0.001, 0.9, 0.999, 1e-08, 0.01, 10

GW = 2 * GDK + 2 * GDV
AW = 4 * AHD
ZG = GH * GW
ZA = AH * AW
ZM = ZG + ZA
GAP = 128
QB = 2 * CHUNK
HP = 2
BANDW = (LEFT + 2) * CHUNK
PADK = LEFT * CHUNK
NCHIP = 4
SHARD = DIN // NCHIP
SLAB = 1824
HSLAB = SLAB // 2
WSLOT = 1824
GPARTS_IN = [(0, 368), (368, 192), (560, 352)]
GPARTS_OUT = [(0, 112), (112, 48), (160, 96)]
NDIRECT = len(GPARTS_IN)
SPLIT_IN, SPLIT_OUT = 336, 128
NEG = -1e30
F32 = jnp.float32
_CDT = jnp.bfloat16
_XDT = jnp.bfloat16
_VMEM = 56 * 1024 * 1024
MESH = pl.DeviceIdType.MESH
ANY = pl.BlockSpec(memory_space=pl.ANY)


def _dot(a, b):
    return jnp.dot(a, b, preferred_element_type=F32)


def _dot_nt(a, b):
    return lax.dot_general(a, b, (((1,), (1,)), ((), ())), preferred_element_type=F32)


def _dot_tn(a, b):
    return lax.dot_general(a, b, (((0,), (0,)), ((), ())), preferred_element_type=F32)


def _rms_rows(v):
    return lax.rsqrt(jnp.mean(v * v, axis=-1, keepdims=True) + EPS)


def _sigmoid(v):
    return 1.0 / (1.0 + jnp.exp(-v))


def _log_sigmoid(v):
    return jnp.minimum(v, 0.0) - jnp.log(1.0 + jnp.exp(-jnp.abs(v)))


def _exact_dot(tri, v):
    hi = v.astype(_CDT)
    r1 = v - hi.astype(F32)
    mid = r1.astype(_CDT)
    lo = (r1 - mid.astype(F32)).astype(_CDT)
    return _dot(tri, hi) + _dot(tri, mid) + _dot(tri, lo)


def _tri(strict):
    row = lax.broadcasted_iota(jnp.int32, (CHUNK, CHUNK), 0)
    col = lax.broadcasted_iota(jnp.int32, (CHUNK, CHUNK), 1)
    return jnp.where((col < row) if strict else (col <= row), 1.0, 0.0).astype(_CDT)


def _norm_gate_bwd(o, g, gate, dcat):
    r = _rms_rows(o)
    oh = o * r
    sg = _sigmoid(gate)
    dn = dcat * (gate * sg)
    dgate = dcat * (oh * g) * (sg * (1.0 + gate * (1.0 - sg)))
    dg = jnp.sum(dn * oh, axis=0, keepdims=True)
    dnn = dn * g
    do = r * (dnn - oh * jnp.mean(dnn * oh, axis=-1, keepdims=True))
    return do, dgate, dg


def _params(sem=None, vmem=_VMEM):
    return pltpu.CompilerParams(dimension_semantics=sem, vmem_limit_bytes=vmem)


def _inproj(x, g, wm, wga, ride=None, tm=512, tn=1024):
    s = x.shape[0]
    gr = _gather_ride(ride, 0)

    def body(*refs):
        (x_ref, g_ref, wm_ref, wga_ref, z_ref, zga_ref, h_ref, hs), rr = gr.split(refs, 4, 3)
        i, j = pl.program_id(0), pl.program_id(1)
        gr.start(rr, (i == 0) & (j == 0))

        @pl.when(pl.program_id(1) == 0)
        def _():
            xv = x_ref[...]
            hv = (xv * _rms_rows(xv) * g_ref[...]).astype(_CDT)
            hs[...] = hv
            h_ref[...] = hv
            zga_ref[...] = _dot_nt(hv, wga_ref[...]).astype(_CDT)

        z_ref[...] = _dot_nt(hs[...], wm_ref[...]).astype(_CDT)
        gr.wait(rr, (i == s // tm - 1) & (j == ZM // tn - 1))

    return pl.pallas_call(
        body, name="inproj", grid=(s // tm, ZM // tn),
        in_specs=[pl.BlockSpec((tm, D), lambda i, j: (i, 0)), pl.BlockSpec((1, D), lambda i, j: (0, 0)),
                  pl.BlockSpec((tn, D), lambda i, j: (j, 0)), pl.BlockSpec((GAP, D), lambda i, j: (0, 0))] + gr.in_specs,
        out_specs=[pl.BlockSpec((tm, tn), lambda i, j: (i, j)), pl.BlockSpec((tm, GAP), lambda i, j: (i, 0)),
                   pl.BlockSpec((tm, D), lambda i, j: (i, 0))] + gr.out_specs,
        out_shape=[jax.ShapeDtypeStruct((s, ZM), _CDT), jax.ShapeDtypeStruct((s, GAP), _CDT),
                   jax.ShapeDtypeStruct((s, D), _CDT)] + gr.out_shape,
        scratch_shapes=[pltpu.VMEM((tm, D), _CDT)] + gr.scratch, input_output_aliases=gr.alias(4, 3),
        compiler_params=_params(("arbitrary", "arbitrary")),
    )(x, g, wm, wga, *gr.operands)


def _gla_fwd(z, zga, wa, ba, ggla, ride=None):
    s = z.shape[0]
    nc = s // CHUNK
    gr = _gather_ride(ride, 1)

    def body(*refs):
        (zg_ref, zga_ref, wa_ref, ba_ref, g_ref, cat_ref, la_s, st), rr = gr.split(refs, 5, 1)
        gr.start(rr, pl.program_id(0) == 0)
        la_s[...] = _log_sigmoid(_dot(zga_ref[...], wa_ref[...]) + ba_ref[...]) * (1.0 / TAU)
        st[...] = jnp.zeros_like(st)
        tri = _tri(False)

        def step(n, carry):
            rows = pl.ds(pl.multiple_of(n * CHUNK, CHUNK), CHUNK)
            for p in range(HP):
                z0 = p * GW
                la = la_s[rows, p * GDK:(p + 1) * GDK]
                lc = _exact_dot(tri, la)
                lend = jnp.sum(la, axis=0, keepdims=True)
                kdec = (zg_ref[rows, z0 + GDK:z0 + 2 * GDK].astype(F32) * jnp.exp(lend - lc)).astype(_CDT)
                stn = jnp.exp(lend) * st[p] + _dot_tn(zg_ref[rows, z0 + 2 * GDK:z0 + 2 * GDK + GDV], kdec)
                st[p] = stn
                qs = (zg_ref[rows, z0:z0 + GDK].astype(F32) * (GDK ** -0.5)).astype(_CDT)
                o = _dot_nt(qs, stn.astype(_CDT))
                gate = zg_ref[rows, z0 + 2 * GDK + GDV:z0 + GW].astype(F32)
                gain = g_ref[:, p * GDV:(p + 1) * GDV]
                cat_ref[rows, p * GDV:(p + 1) * GDV] = (o * _rms_rows(o) * gain * (gate * _sigmoid(gate))).astype(_CDT)
            return carry

        lax.fori_loop(0, nc, step, 0)
        gr.wait(rr, pl.program_id(0) == GH // HP - 1)

    return pl.pallas_call(
        body, name="gla_fwd", grid=(GH // HP,),
        in_specs=[pl.BlockSpec((s, HP * GW), lambda h: (0, h)), pl.BlockSpec((s, GAP), lambda h: (0, 0)),
                  pl.BlockSpec((GAP, HP * GDK), lambda h: (0, h)), pl.BlockSpec((1, HP * GDK), lambda h: (0, h)),
                  pl.BlockSpec((1, HP * GDV), lambda h: (0, h))] + gr.in_specs,
        out_specs=[pl.BlockSpec((s, HP * GDV), lambda h: (0, h))] + gr.out_specs,
        out_shape=[jax.ShapeDtypeStruct((s, DGLA), _CDT)] + gr.out_shape,
        scratch_shapes=[pltpu.VMEM((s, HP * GDK), F32), pltpu.VMEM((HP, GDV, GDK), F32)] + gr.scratch,
        input_output_aliases=gr.alias(5, 1), compiler_params=_params(("arbitrary",)),
    )(z, zga, wa, ba, ggla, *gr.operands)


def _band_bias(b0):
    row = lax.broadcasted_iota(jnp.int32, (QB, 256), 0)
    col = lax.broadcasted_iota(jnp.int32, (QB, 256), 1)
    lane = lax.broadcasted_iota(jnp.int32, (1, 256), 1)
    c0 = jnp.sum(jnp.where(lane == 0, b0, 0.0), axis=1, keepdims=True)
    xv = jnp.broadcast_to(b0, (QB, 256))
    for bit in range(7):
        xv = jnp.where(((row >> bit) & 1) == 1, pltpu.roll(xv, 1 << bit, 1), xv)
    xv = jnp.where(col < row, c0, xv)
    return jnp.concatenate([jnp.broadcast_to(c0, (QB, BANDW - 256)), xv], axis=1)


def _band_static_mask():
    row = lax.broadcasted_iota(jnp.int32, (QB, BANDW), 0) >> 6
    col = lax.broadcasted_iota(jnp.int32, (QB, BANDW), 1) >> 6
    return (col >= row) & (col <= row + LEFT)


def _fold_bias_grad(t):
    row = lax.broadcasted_iota(jnp.int32, (QB, 256), 0)
    col = lax.broadcasted_iota(jnp.int32, (QB, 256), 1)
    xv = t[:, BANDW - 256:]
    low = col < row
    far = jnp.sum(t[:, 0:BANDW - 256], axis=1, keepdims=True) + jnp.sum(jnp.where(low, xv, 0.0), axis=1, keepdims=True)
    far = jnp.sum(far, axis=0, keepdims=True)
    xv = jnp.where(low, 0.0, xv)
    for bit in range(7):
        xv = jnp.where(((row >> bit) & 1) == 1, pltpu.roll(xv, 256 - (1 << bit), 1), xv)
    dp = jnp.sum(xv, axis=0, keepdims=True)
    lane = lax.broadcasted_iota(jnp.int32, (1, 256), 1)
    return dp + jnp.where(lane == 0, far, 0.0)


def _att_fwd(z, rbx, gatt, ride=None):
    s = z.shape[0]
    nb = s // QB
    gr = _gather_ride(ride, 2)

    def body(*refs):
        (za_ref, rb_ref, g_ref, cat_ref, o_ref, lse_ref, kp, vp, bias_s), rr = gr.split(refs, 3, 3)
        gr.start(rr, pl.program_id(0) == 0)
        for p in range(HP):
            z0 = p * AW
            kp[p, 0:PADK, :] = jnp.zeros((PADK, AHD), _CDT)
            vp[p, 0:PADK, :] = jnp.zeros((PADK, AHD), _CDT)
            kp[p, PADK:, :] = za_ref[:, z0 + AHD:z0 + 2 * AHD]
            vp[p, PADK:, :] = za_ref[:, z0 + 2 * AHD:z0 + 3 * AHD]
            bias_s[p] = jnp.where(_band_static_mask(), _band_bias(rb_ref[p]), NEG)

        def step(b, carry):
            r0 = pl.multiple_of(b * QB, QB)
            rows = pl.ds(r0, QB)
            band = pl.ds(r0, BANDW)
            live = lax.broadcasted_iota(jnp.int32, (QB, BANDW), 1) >= PADK - r0
            for p in range(HP):
                z0 = p * AW
                cols = slice(p * AHD, (p + 1) * AHD)
                sc = _dot_nt(za_ref[rows, z0:z0 + AHD], kp[p, band, :]) * (AHD ** -0.5) + bias_s[p]
                sc = jnp.where(live, sc, NEG)
                m = jnp.max(sc, axis=-1, keepdims=True)
                pr = jnp.exp(sc - m)
                l = jnp.sum(pr, axis=-1, keepdims=True)
                o = _dot((pr * (1.0 / l)).astype(_CDT), vp[p, band, :])
                o_ref[rows, cols] = o.astype(_CDT)
                lse_ref[rows, cols] = jnp.broadcast_to(m + jnp.log(l), (QB, AHD))
                gate = za_ref[rows, z0 + 3 * AHD:z0 + AW].astype(F32)
                cat_ref[rows, cols] = (o * _rms_rows(o) * g_ref[:, cols] * (gate * _sigmoid(gate))).astype(_CDT)
            return carry

        lax.fori_loop(0, nb, step, 0)
        gr.wait(rr, pl.program_id(0) == AH // HP - 1)

    return pl.pallas_call(
        body, name="att_fwd", grid=(AH // HP,),
        in_specs=[pl.BlockSpec((s, HP * AW), lambda h: (0, ZG // (HP * AW) + h)), pl.BlockSpec((HP, 1, 256), lambda h: (h, 0, 0)),
                  pl.BlockSpec((1, HP * AHD), lambda h: (0, h))] + gr.in_specs,
        out_specs=[pl.BlockSpec((s, HP * AHD), lambda h: (0, h)), pl.BlockSpec((s, HP * AHD), lambda h: (0, h)),
                   pl.BlockSpec((s, HP * AHD), lambda h: (0, h))] + gr.out_specs,
        out_shape=[jax.ShapeDtypeStruct((s, DATT), _CDT), jax.ShapeDtypeStruct((s, DATT), _CDT),
                   jax.ShapeDtypeStruct((s, DATT), F32)] + gr.out_shape,
        scratch_shapes=[pltpu.VMEM((HP, s + PADK, AHD), _CDT), pltpu.VMEM((HP, s + PADK, AHD), _CDT),
                        pltpu.VMEM((HP, QB, BANDW), F32)] + gr.scratch, input_output_aliases=gr.alias(3, 3),
        compiler_params=_params(("arbitrary",)),
    )(z, rbx, gatt, *gr.operands)


def _outproj(cg, ca, wout, x, gpost, ride=None, tm=256):
    s = x.shape[0]
    gr = _gather_ride(ride, NDIRECT)

    def body(*refs):
        (cg_ref, ca_ref, w_ref, x_ref, g_ref, y_ref, xo_ref), rr = gr.split(refs, 5, 2)
        gr.start(rr, pl.program_id(0) == 0)
        y = _dot(cg_ref[...], w_ref[0:DGLA, :]) + _dot(ca_ref[...], w_ref[DGLA:, :])
        y_ref[...] = y
        xo_ref[...] = x_ref[...] + y * _rms_rows(y) * g_ref[...]
        gr.wait(rr, pl.program_id(0) == s // tm - 1)

    return pl.pallas_call(
        body, name="outproj", grid=(s // tm,),
        in_specs=[pl.BlockSpec((tm, DGLA), lambda i: (i, 0)), pl.BlockSpec((tm, DATT), lambda i: (i, 0)),
                  pl.BlockSpec((D, D), lambda i: (0, 0)), pl.BlockSpec((tm, D), lambda i: (i, 0)),
                  pl.BlockSpec((1, D), lambda i: (0, 0))] + gr.in_specs,
        out_specs=[pl.BlockSpec((tm, D), lambda i: (i, 0)), pl.BlockSpec((tm, D), lambda i: (i, 0))] + gr.out_specs,
        out_shape=[jax.ShapeDtypeStruct((s, D), F32), jax.ShapeDtypeStruct((s, D), F32)] + gr.out_shape,
        scratch_shapes=gr.scratch, input_output_aliases=gr.alias(5, 2),
        compiler_params=_params(("arbitrary",)),
    )(cg, ca, wout, x, gpost, *gr.operands)


def _loss_grad(xo, tgt, tm=256):
    s = xo.shape[0]

    def body(xo_ref, t_ref, d_ref, l_ref):
        @pl.when(pl.program_id(0) == 0)
        def _():
            l_ref[...] = jnp.zeros_like(l_ref)

        e = xo_ref[...] - t_ref[...]
        d_ref[...] = e * (1.0 / D)
        l_ref[...] += jnp.sum(jnp.sum(e * e, axis=1, keepdims=True), axis=0, keepdims=True) * (0.5 / D)

    return pl.pallas_call(
        body, name="loss_grad", grid=(s // tm,),
        in_specs=[pl.BlockSpec((tm, D), lambda i: (i, 0)), pl.BlockSpec((tm, D), lambda i: (i, 0))],
        out_specs=[pl.BlockSpec((tm, D), lambda i: (i, 0)), pl.BlockSpec((1, 1), lambda i: (0, 0))],
        out_shape=[jax.ShapeDtypeStruct((s, D), F32), jax.ShapeDtypeStruct((1, 1), F32)],
        compiler_params=_params(("arbitrary",)),
    )(xo, tgt)


def _place():
    x, y, c = lax.axis_index("x"), lax.axis_index("y"), lax.axis_index("c")
    chips = [(1 - x, y), (x, 1 - y), (1 - x, 1 - y)]
    return x, y, c, chips


def _variants(fn):
    x, y, c, _ = _place()
    for jx in range(2):
        for jy in range(2):
            for jc in range(2):
                pl.when((x == jx) & (y == jy) & (c == jc))(functools.partial(fn, jx, jy, jc))


def _gather_copies(part, x, y, c, src_in, src_out, buf_in, buf_out, send_sems, recv_sems):
    me = 2 * x + y
    peers = [(1 - x, y), (x, 1 - y), (1 - x, 1 - y)]
    cps = []
    for k, (px, py) in enumerate(peers):
        for a, (src, buf, parts) in enumerate([(src_in, buf_in, GPARTS_IN), (src_out, buf_out, GPARTS_OUT)]):
            if part < NDIRECT:
                rows = pl.ds(*parts[part])
                pair = (src.at[c, rows], buf.at[me, c, rows], (px, py, c))
            else:
                piece = buf.at[2 * px + py, c]
                pair = (piece, piece, (x, y, 1 - c))
            cps.append(pltpu.make_async_remote_copy(src_ref=pair[0], dst_ref=pair[1], send_sem=send_sems.at[a, k],
                                                    recv_sem=recv_sems.at[a, k], device_id=pair[2], device_id_type=MESH))
    return cps


class _gather_ride:
    def __init__(self, ride, part):
        self.part, self.on = part, ride is not None
        self.in_specs, self.out_specs, self.out_shape, self.scratch, self.operands, self.aliases = [], [], [], [], [], {}
        if self.on:
            self.operands = list(ride)
            self.in_specs, self.out_specs = [ANY] * 4, [ANY] * 2
            self.out_shape = [jax.ShapeDtypeStruct(t.shape, t.dtype) for t in ride[2:]]
            self.scratch = [pltpu.SemaphoreType.DMA((2, 3)), pltpu.SemaphoreType.DMA((2, 3))]

    def alias(self, n_in, n_out):
        return {n_in + 2: n_out, n_in + 3: n_out + 1} if self.on else {}

    def split(self, refs, n_in, n_out):
        if not self.on:
            return refs, None
        own = refs[:n_in] + refs[n_in + 4:n_in + 4 + n_out] + refs[n_in + 6 + n_out:-2]
        return own, refs[n_in:n_in + 2] + refs[n_in + 4 + n_out:n_in + 6 + n_out] + refs[-2:]

    def start(self, ride_refs, first):
        if self.on:
            def go(x, y, c):
                for cpy in _gather_copies(self.part, x, y, c, *ride_refs):
                    cpy.start()
            pl.when(first)(lambda: _variants(go))

    def wait(self, ride_refs, last):
        if self.on:
            def done(x, y, c):
                for cpy in _gather_copies(self.part, x, y, c, *ride_refs):
                    cpy.wait()
            pl.when(last)(lambda: _variants(done))


def _part_rows(total, split, part):
    return (0, split) if part == 0 else (split, total - split)


def _owner_copies(part, pin_ref, pout_ref, rin, rout, send_sems, recv_sems):
    x, y, c, chips = _place()
    cps = []
    for k, (px, py) in enumerate(chips):
        for a, (src, dst, split) in enumerate([(pin_ref, rin, SPLIT_IN), (pout_ref, rout, SPLIT_OUT)]):
            r0, n = _part_rows(src.shape[2], split, part)
            cps.append(pltpu.make_async_remote_copy(src_ref=src.at[0, 2 * px + py, pl.ds(r0, n)], dst_ref=dst.at[k],
                                                    send_sem=send_sems.at[a, k], recv_sem=recv_sems.at[a, k],
                                                    device_id=(px, py, c), device_id_type=MESH))
    return cps


class _ride_specs:
    def __init__(self, send, part):
        self.in_specs, self.out_specs, self.out_shape, self.scratch, self.operands = [], [], [], [], []
        if send is not None:
            self.in_specs, self.out_specs, self.operands = [ANY, ANY], [ANY, ANY], list(send)
            self.out_shape = [jax.ShapeDtypeStruct((3, _part_rows(t.shape[2], split, part)[1], D), t.dtype)
                              for t, split in zip(send, (SPLIT_IN, SPLIT_OUT))]
            self.scratch = [pltpu.SemaphoreType.DMA((2, 3)), pltpu.SemaphoreType.DMA((2, 3))]


def _ride_refs(refs, send, n_out):
    if send is None:
        return None, refs
    pin_ref, pout_ref = refs[:2]
    own_out = refs[2:2 + n_out]
    rin, rout = refs[2 + n_out:4 + n_out]
    return (pin_ref, pout_ref, rin, rout, refs[-2], refs[-1]), tuple(own_out) + tuple(refs[4 + n_out:-2])


def _ride_start(refs, send, n_out, part, first=None):
    ride = _ride_refs(refs, send, n_out)[0]
    if ride is None:
        return []
    cps = _owner_copies(part, *ride)

    @pl.when(pl.program_id(0) == 0 if first is None else first)
    def _():
        for cpy in cps:
            cpy.start()

    return cps


def _ride_wait(cps, steps, last=None):
    if cps:
        @pl.when(pl.program_id(0) == steps - 1 if last is None else last)
        def _():
            for cpy in cps:
                cpy.wait()


def _post_bwd(dout, y, gpost, wout, tm=256):
    s = y.shape[0]

    def body(d_ref, y_ref, g_ref, w_ref, dy_ref, dcg_ref, dca_ref, dg_ref):
        @pl.when(pl.program_id(0) == 0)
        def _():
            dg_ref[...] = jnp.zeros_like(dg_ref)

        yv = y_ref[...]
        r = _rms_rows(yv)
        yh = yv * r
        dv = d_ref[...]
        dg_ref[...] += jnp.sum(dv * yh, axis=0, keepdims=True)
        dn = dv * g_ref[...]
        dyb = (r * (dn - yh * jnp.mean(dn * yh, axis=-1, keepdims=True))).astype(_CDT)
        dy_ref[...] = dyb
        dcg_ref[...] = _dot_nt(dyb, w_ref[0:DGLA, :]).astype(_CDT)
        dca_ref[...] = _dot_nt(dyb, w_ref[DGLA:, :]).astype(_CDT)

    return pl.pallas_call(
        body, name="post_bwd", grid=(s // tm,),
        in_specs=[pl.BlockSpec((tm, D), lambda i: (i, 0)), pl.BlockSpec((tm, D), lambda i: (i, 0)),
                  pl.BlockSpec((1, D), lambda i: (0, 0)), pl.BlockSpec((D, D), lambda i: (0, 0))],
        out_specs=[pl.BlockSpec((tm, D), lambda i: (i, 0)), pl.BlockSpec((tm, DGLA), lambda i: (i, 0)),
                   pl.BlockSpec((tm, DATT), lambda i: (i, 0)), pl.BlockSpec((1, D), lambda i: (0, 0))],
        out_shape=[jax.ShapeDtypeStruct((s, D), _CDT), jax.ShapeDtypeStruct((s, DGLA), _CDT),
                   jax.ShapeDtypeStruct((s, DATT), _CDT), jax.ShapeDtypeStruct((1, D), F32)],
        compiler_params=_params(("arbitrary",)),
    )(dout, y, gpost, wout)


def _matmul_tn(a, b, out_dtype, tm, tn, name):
    k, m = a.shape
    n = b.shape[1]

    def body(a_ref, b_ref, o_ref):
        o_ref[...] = _dot_tn(a_ref[...], b_ref[...]).astype(out_dtype)

    return pl.pallas_call(
        body, name=name, grid=(m // tm, n // tn),
        in_specs=[pl.BlockSpec((k, tm), lambda i, j: (0, i)), pl.BlockSpec((k, tn), lambda i, j: (0, j))],
        out_specs=pl.BlockSpec((tm, tn), lambda i, j: (i, j)),
        out_shape=jax.ShapeDtypeStruct((m, n), out_dtype),
        compiler_params=_params(("parallel", "parallel")),
    )(a, b)


def _att_bwd(z, oraw, lse, dca, rbx, gatt, send=None):
    s = z.shape[0]
    nb = s // QB

    def body(*refs):
        za_ref, o_ref, lse_ref, dc_ref, rb_ref, g_ref = refs[:6]
        dz_ref, dg_ref, db_ref, kp, vp, dkp, dvp, bias_s, t_s, dg_s = _ride_refs(refs[6:], send, 3)[1]
        cps = _ride_start(refs[6:], send, 3, 0)
        for p in range(HP):
            z0 = p * AW
            kp[p, 0:PADK, :] = jnp.zeros((PADK, AHD), _CDT)
            vp[p, 0:PADK, :] = jnp.zeros((PADK, AHD), _CDT)
            kp[p, PADK:, :] = za_ref[:, z0 + AHD:z0 + 2 * AHD]
            vp[p, PADK:, :] = za_ref[:, z0 + 2 * AHD:z0 + 3 * AHD]
            bias_s[p] = jnp.where(_band_static_mask(), _band_bias(rb_ref[p]), NEG)
        dkp[...] = jnp.zeros_like(dkp)
        dvp[...] = jnp.zeros_like(dvp)
        t_s[...] = jnp.zeros_like(t_s)
        dg_s[...] = jnp.zeros_like(dg_s)

        def step(b, carry):
            r0 = pl.multiple_of(b * QB, QB)
            rows = pl.ds(r0, QB)
            band = pl.ds(r0, BANDW)
            live = lax.broadcasted_iota(jnp.int32, (QB, BANDW), 1) >= PADK - r0
            for p in range(HP):
                z0 = p * AW
                cols = slice(p * AHD, (p + 1) * AHD)
                o = o_ref[rows, cols].astype(F32)
                do, dgate, dg = _norm_gate_bwd(o, g_ref[:, cols], za_ref[rows, z0 + 3 * AHD:z0 + AW].astype(F32),
                                               dc_ref[rows, cols].astype(F32))
                dg_s[:, cols] += dg
                q = za_ref[rows, z0:z0 + AHD]
                kb = kp[p, band, :]
                sc = _dot_nt(q, kb) * (AHD ** -0.5) + bias_s[p]
                sc = jnp.where(live, sc, NEG)
                pr = jnp.exp(sc - jnp.max(lse_ref[rows, cols], axis=-1, keepdims=True))
                dob = do.astype(_CDT)
                dp = _dot_nt(dob, vp[p, band, :])
                ds = pr * (dp - jnp.sum(do * o, axis=-1, keepdims=True))
                t_s[p] += ds
                dsb = (ds * (AHD ** -0.5)).astype(_CDT)
                dz_ref[rows, z0:z0 + AHD] = _dot(dsb, kb).astype(_CDT)
                dz_ref[rows, z0 + 3 * AHD:z0 + AW] = dgate.astype(_CDT)
                dkp[p, band, :] += _dot_tn(dsb, q)
                dvp[p, band, :] += _dot_tn(pr.astype(_CDT), dob)
            return carry

        lax.fori_loop(0, nb, step, 0)
        for p in range(HP):
            z0 = p * AW
            dz_ref[:, z0 + AHD:z0 + 2 * AHD] = dkp[p, PADK:, :].astype(_CDT)
            dz_ref[:, z0 + 2 * AHD:z0 + 3 * AHD] = dvp[p, PADK:, :].astype(_CDT)
            db_ref[p] = _fold_bias_grad(t_s[p])
        dg_ref[...] = dg_s[...]
        _ride_wait(cps, AH // HP)

    ride = _ride_specs(send, 0)
    return pl.pallas_call(
        body, name="att_bwd", grid=(AH // HP,),
        in_specs=[pl.BlockSpec((s, HP * AW), lambda h: (0, ZG // (HP * AW) + h)), pl.BlockSpec((s, HP * AHD), lambda h: (0, h)),
                  pl.BlockSpec((s, HP * AHD), lambda h: (0, h)), pl.BlockSpec((s, HP * AHD), lambda h: (0, h)),
                  pl.BlockSpec((HP, 1, 256), lambda h: (h, 0, 0)), pl.BlockSpec((1, HP * AHD), lambda h: (0, h))] + ride.in_specs,
        out_specs=[pl.BlockSpec((s, HP * AW), lambda h: (0, h)), pl.BlockSpec((1, HP * AHD), lambda h: (0, h)),
                   pl.BlockSpec((HP, 1, 256), lambda h: (h, 0, 0))] + ride.out_specs,
        out_shape=[jax.ShapeDtypeStruct((s, ZA), _CDT), jax.ShapeDtypeStruct((1, DATT), F32),
                   jax.ShapeDtypeStruct((AH, 1, 256), F32)] + ride.out_shape,
        scratch_shapes=[pltpu.VMEM((HP, s + PADK, AHD), _CDT), pltpu.VMEM((HP, s + PADK, AHD), _CDT),
                        pltpu.VMEM((HP, s + PADK, AHD), F32), pltpu.VMEM((HP, s + PADK, AHD), F32),
                        pltpu.VMEM((HP, QB, BANDW), F32), pltpu.VMEM((HP, QB, BANDW), F32), pltpu.VMEM((1, HP * AHD), F32)] + ride.scratch,
        compiler_params=_params(("arbitrary",)),
    )(z, oraw, lse, dca, rbx, gatt, *ride.operands)


def _gla_bwd(z, zga, wa, ba, ggla, dcg, send=None):
    s = z.shape[0]
    nc = s // CHUNK

    def body(*refs):
        zg_ref, zga_ref, wa_ref, ba_ref, g_ref, dc_ref = refs[:6]
        (dz_ref, dga_ref, dwa_ref, dba_ref, dg_ref,
         la_s, om_s, sall, dpre_s, c_s, dga_s, dg_s) = _ride_refs(refs[6:], send, 5)[1]
        cps = _ride_start(refs[6:], send, 5, 1)
        h = pl.program_id(0)
        pre = _dot(zga_ref[...], wa_ref[...]) + ba_ref[...]
        la_s[...] = _log_sigmoid(pre) * (1.0 / TAU)
        om_s[...] = (1.0 - _sigmoid(pre)) * (1.0 / TAU)
        c_s[...] = jnp.zeros_like(c_s)
        dg_s[...] = jnp.zeros_like(dg_s)
        tri = _tri(False)
        tri_strict = _tri(True)

        def decay(rows, p):
            la = la_s[rows, p * GDK:(p + 1) * GDK]
            lend = jnp.sum(la, axis=0, keepdims=True)
            return jnp.exp(lend - _exact_dot(tri, la)), jnp.exp(lend)

        def fwd(n, sts):
            rows = pl.ds(pl.multiple_of(n * CHUNK, CHUNK), CHUNK)
            out = []
            for p in range(HP):
                z0 = p * GW
                dec, a = decay(rows, p)
                kdec = (zg_ref[rows, z0 + GDK:z0 + 2 * GDK].astype(F32) * dec).astype(_CDT)
                stn = a * sts[p] + _dot_tn(zg_ref[rows, z0 + 2 * GDK:z0 + 2 * GDK + GDV], kdec)
                sall[p, n] = stn
                out.append(stn)
            return tuple(out)

        lax.fori_loop(0, nc, fwd, tuple(jnp.zeros((GDV, GDK), F32) for _ in range(HP)))

        def bwd(i, carry):
            n = nc - 1 - i
            rows = pl.ds(pl.multiple_of(n * CHUNK, CHUNK), CHUNK)
            for p in range(HP):
                z0 = p * GW
                kc = slice(p * GDK, (p + 1) * GDK)
                vc = slice(p * GDV, (p + 1) * GDV)
                dec, a = decay(rows, p)
                kdec = zg_ref[rows, z0 + GDK:z0 + 2 * GDK].astype(F32) * dec
                kdb = kdec.astype(_CDT)
                v = zg_ref[rows, z0 + 2 * GDK:z0 + 2 * GDK + GDV]
                qs = (zg_ref[rows, z0:z0 + GDK].astype(F32) * (GDK ** -0.5)).astype(_CDT)
                stb = sall[p, n].astype(_CDT)
                st_prev = sall[p, jnp.maximum(n - 1, 0)] * jnp.where(n > 0, 1.0, 0.0)
                o = _dot_nt(qs, stb)
                do, dgate, dg = _norm_gate_bwd(o, g_ref[:, vc], zg_ref[rows, z0 + 2 * GDK + GDV:z0 + GW].astype(F32),
                                               dc_ref[rows, vc].astype(F32))
                dg_s[:, vc] += dg
                dob = do.astype(_CDT)
                gt = _dot_tn(dob, qs) + c_s[p]
                gtb = gt.astype(_CDT)
                da = jnp.sum(gt * st_prev, axis=0, keepdims=True)
                dkdec = _dot(v, gtb)
                dla = _exact_dot(tri_strict, dkdec * kdec) + da * a
                dpre_s[rows, kc] = dla * om_s[rows, kc]
                dz_ref[rows, z0:z0 + GDK] = (_dot(dob, stb) * (GDK ** -0.5)).astype(_CDT)
                dz_ref[rows, z0 + GDK:z0 + 2 * GDK] = (dkdec * dec).astype(_CDT)
                dz_ref[rows, z0 + 2 * GDK:z0 + 2 * GDK + GDV] = _dot_nt(kdb, gtb).astype(_CDT)
                dz_ref[rows, z0 + 2 * GDK + GDV:z0 + GW] = dgate.astype(_CDT)
                c_s[p] = a * gt
            return carry

        lax.fori_loop(0, nc, bwd, 0)
        dpre = dpre_s[...]
        dpb = dpre.astype(_CDT)
        dg_ref[...] = dg_s[...]
        dba_ref[...] = jnp.sum(dpre, axis=0, keepdims=True)
        dwa_ref[...] = _dot_tn(zga_ref[...], dpb)
        part = _dot_nt(dpb, wa_ref[...])

        @pl.when(h == 0)
        def _():
            dga_s[...] = part

        @pl.when(h > 0)
        def _():
            dga_s[...] += part

        @pl.when(h == GH // HP - 1)
        def _():
            dga_ref[...] = dga_s[...].astype(_CDT)

        _ride_wait(cps, GH // HP)

    ride = _ride_specs(send, 1)
    return pl.pallas_call(
        body, name="gla_bwd", grid=(GH // HP,),
        in_specs=[pl.BlockSpec((s, HP * GW), lambda h: (0, h)), pl.BlockSpec((s, GAP), lambda h: (0, 0)),
                  pl.BlockSpec((GAP, HP * GDK), lambda h: (0, h)), pl.BlockSpec((1, HP * GDK), lambda h: (0, h)),
                  pl.BlockSpec((1, HP * GDV), lambda h: (0, h)), pl.BlockSpec((s, HP * GDV), lambda h: (0, h))] + ride.in_specs,
        out_specs=[pl.BlockSpec((s, HP * GW), lambda h: (0, h)), pl.BlockSpec((s, GAP), lambda h: (0, 0)),
                   pl.BlockSpec((GAP, HP * GDK), lambda h: (0, h)), pl.BlockSpec((1, HP * GDK), lambda h: (0, h)),
                   pl.BlockSpec((1, HP * GDV), lambda h: (0, h))] + ride.out_specs,
        out_shape=[jax.ShapeDtypeStruct((s, ZG), _CDT), jax.ShapeDtypeStruct((s, GAP), _CDT),
                   jax.ShapeDtypeStruct((GAP, GH * GDK), F32), jax.ShapeDtypeStruct((1, GH * GDK), F32),
                   jax.ShapeDtypeStruct((1, DGLA), F32)] + ride.out_shape,
        scratch_shapes=[pltpu.VMEM((s, HP * GDK), F32), pltpu.VMEM((s, HP * GDK), F32), pltpu.VMEM((HP, nc, GDV, GDK), F32),
                        pltpu.VMEM((s, HP * GDK), F32), pltpu.VMEM((HP, GDV, GDK), F32), pltpu.VMEM((s, GAP), F32),
                        pltpu.VMEM((1, HP * GDV), F32)] + ride.scratch,
        compiler_params=_params(("arbitrary",)),
    )(z, zga, wa, ba, ggla, dcg, *ride.operands)


def _dh(dzg, dza, dga, wm, wga, x, dout, gpre, send=None, tm=512, tk=1024):
    s = x.shape[0]
    nkg, nk = ZG // tk, ZM // tk

    def body(*refs):
        dzg_ref, dza_ref, dga_ref, wm_ref, wga_ref, x_ref, d_ref, g_ref = refs[:8]
        dx_ref, dg_ref, acc = _ride_refs(refs[8:], send, 2)[1]
        i, k = pl.program_id(0), pl.program_id(1)
        cps = _ride_start(refs[8:], send, 2, 1, (i == 0) & (k == 0))

        @pl.when((i == 0) & (k == 0))
        def _():
            dg_ref[...] = jnp.zeros_like(dg_ref)

        @pl.when(k == 0)
        def _():
            acc[...] = _dot(dga_ref[...], wga_ref[...])

        @pl.when(k < nkg)
        def _():
            acc[...] += _dot(dzg_ref[...], wm_ref[...])

        @pl.when(k >= nkg)
        def _():
            acc[...] += _dot(dza_ref[...], wm_ref[...])

        @pl.when(k == nk - 1)
        def _():
            xv = x_ref[...]
            r = _rms_rows(xv)
            xh = xv * r
            dh = acc[...]
            dg_ref[...] += jnp.sum(dh * xh, axis=0, keepdims=True)
            dn = dh * g_ref[...]
            dx_ref[...] = d_ref[...] + r * (dn - xh * jnp.mean(dn * xh, axis=-1, keepdims=True))

        _ride_wait(cps, 0, (i == s // tm - 1) & (k == nk - 1))

    ride = _ride_specs(send, 1)
    return pl.pallas_call(
        body, name="dh", grid=(s // tm, nk),
        in_specs=[pl.BlockSpec((tm, tk), lambda i, k: (i, jnp.minimum(k, nkg - 1))),
                  pl.BlockSpec((tm, tk), lambda i, k: (i, jnp.maximum(k - nkg, 0))),
                  pl.BlockSpec((tm, GAP), lambda i, k: (i, 0)), pl.BlockSpec((tk, D), lambda i, k: (k, 0)),
                  pl.BlockSpec((GAP, D), lambda i, k: (0, 0)), pl.BlockSpec((tm, D), lambda i, k: (i, 0)),
                  pl.BlockSpec((tm, D), lambda i, k: (i, 0)), pl.BlockSpec((1, D), lambda i, k: (0, 0))] + ride.in_specs,
        out_specs=[pl.BlockSpec((tm, D), lambda i, k: (i, 0)), pl.BlockSpec((1, D), lambda i, k: (0, 0))] + ride.out_specs,
        out_shape=[jax.ShapeDtypeStruct((s, D), F32), jax.ShapeDtypeStruct((1, D), F32)] + ride.out_shape,
        scratch_shapes=[pltpu.VMEM((tm, D), F32)] + ride.scratch,
        compiler_params=_params(("arbitrary", "arbitrary")),
    )(dzg, dza, dga, wm, wga, x, dout, gpre, *ride.operands)


def _adam(w, g, m, v, tr, name):
    rws, cols = w.shape

    def body(w_ref, g_ref, m_ref, v_ref, d_ref, mo_ref, vo_ref):
        gv = g_ref[...]
        mn = ADAM_B1 * m_ref[...] + (1.0 - ADAM_B1) * gv
        vn = ADAM_B2 * v_ref[...] + (1.0 - ADAM_B2) * (gv * gv)
        mh = mn / (1.0 - ADAM_B1 ** ADAM_STEP)
        vh = vn / (1.0 - ADAM_B2 ** ADAM_STEP)
        d_ref[...] = -ADAM_LR * (mh / (jnp.sqrt(vh) + ADAM_EPS) + ADAM_WD * w_ref[...])
        mo_ref[...] = mn
        vo_ref[...] = vn

    spec = pl.BlockSpec((tr, cols), lambda i: (i, 0))
    return pl.pallas_call(
        body, name=name, grid=(rws // tr,), in_specs=[spec] * 4, out_specs=[spec] * 3,
        out_shape=[jax.ShapeDtypeStruct((rws, cols), F32)] * 3,
        compiler_params=_params(("parallel",)),
    )(w, g, m, v)


def _adam_rows(w, g, m, v, tj, name):
    rows = w.shape[0]

    def body(w_ref, g_ref, m_ref, v_ref, d_ref, mo_ref, vo_ref):
        gv = g_ref[...]
        mn = ADAM_B1 * m_ref[...] + (1.0 - ADAM_B1) * gv
        vn = ADAM_B2 * v_ref[...] + (1.0 - ADAM_B2) * (gv * gv)
        mh = mn / (1.0 - ADAM_B1 ** ADAM_STEP)
        vh = vn / (1.0 - ADAM_B2 ** ADAM_STEP)
        d_ref[...] = -ADAM_LR * (mh / (jnp.sqrt(vh) + ADAM_EPS) + ADAM_WD * w_ref[...])
        mo_ref[...] = mn
        vo_ref[...] = vn

    spec = pl.BlockSpec((tj,) + w.shape[1:], lambda i: (i, 0, 0))
    return pl.pallas_call(
        body, name=name, grid=(pl.cdiv(rows, tj),), in_specs=[spec] * 4, out_specs=[spec] * 3,
        out_shape=[jax.ShapeDtypeStruct(w.shape, F32)] * 3,
        compiler_params=_params(("parallel",)),
    )(w, g, m, v)


def _gather_first(src_in, src_out, wa, buf_in, buf_out, wa_all):
    def variant(x, y, c, src_in_ref, src_out_ref, wa_ref, _b0, _b1, _b2, bin_ref, bout_ref, wa_ref_all, send_sems, recv_sems, wa_send, wa_recv):
        direct = []
        for part in range(NDIRECT):
            direct += _gather_copies(part, x, y, c, src_in_ref, src_out_ref, bin_ref, bout_ref, send_sems.at[part], recv_sems.at[part])
        for k, (px, py) in enumerate([(1 - x, y), (x, 1 - y), (1 - x, 1 - y)]):
            direct.append(pltpu.make_async_remote_copy(src_ref=wa_ref, dst_ref=wa_ref_all.at[2 * x + y], send_sem=wa_send.at[k],
                                                       recv_sem=wa_recv.at[k], device_id=(px, py, c), device_id_type=MESH))
        for cpy in direct:
            cpy.start()
        for cpy in direct:
            cpy.wait()
        passed = _gather_copies(NDIRECT, x, y, c, src_in_ref, src_out_ref, bin_ref, bout_ref, send_sems.at[NDIRECT], recv_sems.at[NDIRECT])
        for cpy in passed:
            cpy.start()
        for cpy in passed:
            cpy.wait()

    def body(*refs):
        _variants(lambda x, y, c: variant(x, y, c, *refs))

    return pl.pallas_call(
        body, name="gather_first", in_specs=[ANY] * 6, out_specs=[ANY] * 3, input_output_aliases={3: 0, 4: 1, 5: 2},
        out_shape=[jax.ShapeDtypeStruct(t.shape, t.dtype) for t in (buf_in, buf_out, wa_all)],
        scratch_shapes=[pltpu.SemaphoreType.DMA((NDIRECT + 1, 2, 3)), pltpu.SemaphoreType.DMA((NDIRECT + 1, 2, 3)),
                        pltpu.SemaphoreType.DMA((3,)), pltpu.SemaphoreType.DMA((3,))],
    )(src_in, src_out, wa, buf_in, buf_out, wa_all)


def _swap_halves(gin2, gout2):
    def body(gin_ref, gout_ref, rin, rout, send_sems, recv_sems):
        x, y, c, _ = _place()
        sib = (x, y, 1 - c)
        cps = [pltpu.make_async_remote_copy(src_ref=src.at[1 - c], dst_ref=dst, send_sem=send_sems.at[a],
                                            recv_sem=recv_sems.at[a], device_id=sib, device_id_type=MESH)
               for a, (src, dst) in enumerate([(gin_ref, rin), (gout_ref, rout)])]
        for cpy in cps:
            cpy.start()
        for cpy in cps:
            cpy.wait()

    return pl.pallas_call(
        body, name="swap_halves", in_specs=[ANY, ANY], out_specs=[ANY, ANY],
        out_shape=[jax.ShapeDtypeStruct(gin2.shape[1:], gin2.dtype), jax.ShapeDtypeStruct(gout2.shape[1:], gout2.dtype)],
        scratch_shapes=[pltpu.SemaphoreType.DMA((2,)), pltpu.SemaphoreType.DMA((2,))],
    )(gin2, gout2)


def _add_halves(c_idx, g2, r, tr, name):
    rows, cols = r.shape

    def body(c_ref, g_ref, r_ref, o_ref):
        o_ref[...] = (g_ref[0].astype(F32) + r_ref[...].astype(F32)).astype(_XDT)

    return pl.pallas_call(
        body, name=name,
        grid_spec=pltpu.PrefetchScalarGridSpec(
            num_scalar_prefetch=1, grid=(rows // tr,),
            in_specs=[pl.BlockSpec((1, tr, cols), lambda i, c_ref: (c_ref[0], i, 0)),
                      pl.BlockSpec((tr, cols), lambda i, c_ref: (i, 0))],
            out_specs=pl.BlockSpec((tr, cols), lambda i, c_ref: (i, 0))),
        out_shape=jax.ShapeDtypeStruct((rows, cols), _XDT),
        compiler_params=_params(("parallel",)),
    )(c_idx, g2, r)


def _send_to_owners(pin, pout, parts):
    def body(pin_ref, pout_ref, *refs):
        send_sems, recv_sems = refs[-2:]
        cps = []
        for n, part in enumerate(parts):
            cps += _owner_copies(part, pin_ref, pout_ref, refs[2 * n], refs[2 * n + 1], send_sems.at[n], recv_sems.at[n])
        for cpy in cps:
            cpy.start()
        for cpy in cps:
            cpy.wait()

    shapes = [jax.ShapeDtypeStruct((3, _part_rows(t.shape[2], split, part)[1], D), t.dtype)
              for part in parts for t, split in zip((pin, pout), (SPLIT_IN, SPLIT_OUT))]
    return pl.pallas_call(
        body, name="send_to_owners", in_specs=[ANY, ANY], out_specs=[ANY] * len(shapes), out_shape=shapes,
        scratch_shapes=[pltpu.SemaphoreType.DMA((len(parts), 2, 3)), pltpu.SemaphoreType.DMA((len(parts), 2, 3))],
    )(pin, pout)


def _add_chips(chip_idx, p, ra, rb, tr, name):
    rows = p.shape[2]
    na = ra.shape[1] // tr

    def body(c_ref, p_ref, ra_ref, rb_ref, o_ref):
        r = jnp.where(pl.program_id(0) < na, ra_ref[...], rb_ref[...]).astype(F32)
        o_ref[0] = ((p_ref[0, 0].astype(F32) + r[0]) + r[1]) + r[2]

    return pl.pallas_call(
        body, name=name,
        grid_spec=pltpu.PrefetchScalarGridSpec(
            num_scalar_prefetch=1, grid=(rows // tr,),
            in_specs=[pl.BlockSpec((1, 1, tr, D), lambda i, c_ref: (0, c_ref[0], i, 0)),
                      pl.BlockSpec((3, tr, D), lambda i, c_ref: (0, jnp.minimum(i, na - 1), 0)),
                      pl.BlockSpec((3, tr, D), lambda i, c_ref: (0, jnp.maximum(i - na, 0), 0))],
            out_specs=pl.BlockSpec((1, tr, D), lambda i, c_ref: (0, i, 0))),
        out_shape=jax.ShapeDtypeStruct((1, rows, D), F32),
        compiler_params=_params(("parallel",)),
    )(chip_idx, p, ra, rb)


def _exchange_halves(arrs):
    n = len(arrs)

    def body(*refs):
        x, y, c, _ = _place()
        cps = [pltpu.make_async_remote_copy(src_ref=refs[a], dst_ref=refs[n + a], send_sem=refs[2 * n].at[a], recv_sem=refs[2 * n + 1].at[a],
                                            device_id=(x, y, 1 - c), device_id_type=MESH) for a in range(n)]
        for cpy in cps:
            cpy.start()
        for cpy in cps:
            cpy.wait()

    return pl.pallas_call(
        body, name="exchange_halves", in_specs=[ANY] * n, out_specs=[ANY] * n,
        out_shape=[jax.ShapeDtypeStruct(t.shape, t.dtype) for t in arrs],
        scratch_shapes=[pltpu.SemaphoreType.DMA((n,)), pltpu.SemaphoreType.DMA((n,))],
    )(*arrs)


def _adam_halves(c_idx, w, g_own, g_other, m, v, tr, name):
    nl, _, rows, cols = w.shape

    def body(c_ref, w_ref, go_ref, gx_ref, m_ref, v_ref, g_ref, d_ref, mo_ref, vo_ref):
        gv = jnp.where(pl.program_id(1) == c_ref[0], go_ref[0], gx_ref[0])
        mn = ADAM_B1 * m_ref[0, 0] + (1.0 - ADAM_B1) * gv
        vn = ADAM_B2 * v_ref[0, 0] + (1.0 - ADAM_B2) * (gv * gv)
        mh = mn / (1.0 - ADAM_B1 ** ADAM_STEP)
        vh = vn / (1.0 - ADAM_B2 ** ADAM_STEP)
        g_ref[0, 0] = gv
        d_ref[0, 0] = -ADAM_LR * (mh / (jnp.sqrt(vh) + ADAM_EPS) + ADAM_WD * w_ref[0, 0])
        mo_ref[0, 0] = mn
        vo_ref[0, 0] = vn

    full = pl.BlockSpec((1, 1, tr, cols), lambda l, hh, i, c_ref: (l, hh, i, 0))
    own = pl.BlockSpec((1, tr, cols), lambda l, hh, i, c_ref: (l, jnp.where(hh == c_ref[0], i, 0), 0))
    other = pl.BlockSpec((1, tr, cols), lambda l, hh, i, c_ref: (l, jnp.where(hh == c_ref[0], 0, i), 0))
    return pl.pallas_call(
        body, name=name,
        grid_spec=pltpu.PrefetchScalarGridSpec(
            num_scalar_prefetch=1, grid=(nl, 2, rows // tr),
            in_specs=[full, own, other, full, full], out_specs=[full] * 4),
        out_shape=[jax.ShapeDtypeStruct(w.shape, F32)] * 4,
        compiler_params=_params(("parallel", "parallel", "parallel")),
    )(c_idx, w, g_own, g_other, m, v)


def _allreduce_small(sg):
    rows = sg.shape[0]
    vm = pl.BlockSpec(memory_space=pltpu.VMEM)

    def body(sg_ref, tot_ref, all_ref, send_sems, recv_sems):
        x, y, c, _ = _place()
        me = 4 * x + 2 * y + c
        all_ref[me] = sg_ref[...]
        cps = []
        for mask in range(1, 8):
            to = (1 - x if mask & 4 else x, 1 - y if mask & 2 else y, 1 - c if mask & 1 else c)
            cps.append(pltpu.make_async_remote_copy(src_ref=sg_ref, dst_ref=all_ref.at[me], send_sem=send_sems.at[mask - 1],
                                                    recv_sem=recv_sems.at[mask - 1], device_id=to, device_id_type=MESH))
        for cpy in cps:
            cpy.start()
        for cpy in cps:
            cpy.wait()
        acc = all_ref[0]
        for d in range(1, 8):
            acc = acc + all_ref[d]
        tot_ref[...] = acc

    return pl.pallas_call(
        body, name="allreduce_small", in_specs=[vm], out_specs=[vm, vm],
        out_shape=[jax.ShapeDtypeStruct((rows, 128), F32), jax.ShapeDtypeStruct((8, rows, 128), F32)],
        scratch_shapes=[pltpu.SemaphoreType.DMA((7,)), pltpu.SemaphoreType.DMA((7,))],
        compiler_params=_params(),
    )(sg)[0]


_CUTS = [0, 512, 1024, 2048, 3072, 3088, 4112, 5136, 6160, 7184]


def _rows_to_internal(w):
    tail = w.shape[1:]
    gq, gk, gv, gg, ga, aq, ak, av, ag = [w[_CUTS[i]:_CUTS[i + 1]] for i in range(9)]
    g = jnp.concatenate([gq.reshape((GH, GDK) + tail), gk.reshape((GH, GDK) + tail),
                         gv.reshape((GH, GDV) + tail), gg.reshape((GH, GDV) + tail)], axis=1).reshape((ZG,) + tail)
    a = jnp.concatenate([t.reshape((AH, AHD) + tail) for t in (aq, ak, av, ag)], axis=1).reshape((ZA,) + tail)
    pad = [(0, GAP - RANK)] + [(0, 0)] * len(tail)
    return jnp.concatenate([g, a], axis=0), jnp.pad(ga, pad)


def _rows_from_internal(g, a, ga):
    tail = g.shape[1:]
    g = g.reshape((GH, GW) + tail)
    a = a.reshape((AH, AW) + tail)
    parts = [g[:, 0:GDK], g[:, GDK:2 * GDK], g[:, 2 * GDK:2 * GDK + GDV], g[:, 2 * GDK + GDV:GW]]
    parts = [t.reshape((-1,) + tail) for t in parts] + [ga[0:RANK]]
    parts += [a[:, i * AHD:(i + 1) * AHD].reshape((-1,) + tail) for i in range(4)]
    return jnp.concatenate(parts, axis=0)


def _slab_lo(chip):
    return min(SHARD * chip // 16 * 16, DIN - SLAB)


def _pack_rows(parts):
    rows = []
    for t in parts:
        flat = t.reshape(-1)
        rows.append(jnp.pad(flat, (0, (-flat.shape[0]) % 128)).reshape(-1, 128))
    buf = jnp.concatenate(rows, axis=0)
    return jnp.pad(buf, ((0, (-buf.shape[0]) % 8), (0, 0)))


def _unpack_rows(buf, shapes):
    out, r = [], 0
    for shp in shapes:
        n = 1
        for d in shp:
            n *= d
        nr = -(-n // 128)
        out.append(buf[r:r + nr].reshape(-1)[:n].reshape(shp))
        r += nr
    return out


def _layer_fwd(x, wm, wga, wout, gpre, gpost, wa, ba, ggla, gatt, rbx, ride=None):
    z, zga, h, *bufs = _inproj(x, gpre, wm, wga, ride)
    ride = None if ride is None else (ride[0], ride[1], *bufs)
    cg, *bufs = _gla_fwd(z, zga, wa, ba, ggla, ride)
    ride = None if ride is None else (ride[0], ride[1], *bufs)
    ca, oraw, lse, *bufs = _att_fwd(z, rbx, gatt, ride)
    ride = None if ride is None else (ride[0], ride[1], *bufs)
    y, xo, *bufs = _outproj(cg, ca, wout, x, gpost, ride)
    return xo, (x, z, zga, h, cg, ca, oraw, lse, y), bufs


def _layer_bwd(dout, saved, wm, wga, wout, gpre, gpost, wa, ba, ggla, gatt, rbx, send=None, reduce_now=None):
    x, z, zga, h, cg, ca, oraw, lse, y = saved
    dy, dcg, dca, dgpost = _post_bwd(dout, y, gpost, wout)
    dwout = jnp.concatenate([_matmul_tn(cg, dy, _XDT, 512, 1024, "dwout_gla"),
                             _matmul_tn(ca, dy, _XDT, 512, 1024, "dwout_att")], axis=0)
    dza, dgatt, dbx, *got_a = _att_bwd(z, oraw, lse, dca, rbx, gatt, send)
    dzg, dga, dwa, dba, dggla, *got_b = _gla_bwd(z, zga, wa, ba, ggla, dcg, send)
    dwin = (_matmul_tn(dzg, h, _XDT, 512, 1024, "dwin_gla"), _matmul_tn(dza, h, _XDT, 512, 1024, "dwin_att"),
            _matmul_tn(dga, h, _XDT, GAP, 1024, "dwin_gate"))
    own = None if reduce_now is None else reduce_now(dwin, dwout)
    dx, dgpre, *got_own = _dh(dzg, dza, dga, wm, wga, x, dout, gpre, own)
    drb = jnp.concatenate([jnp.zeros((AH, 1), F32), dbx[:, 0, ::-1]], axis=1)
    return dx, dwin, dwout, (dgpre[0], dgpost[0], dwa[0:RANK], dba[0], dggla[0], dgatt[0], drb), got_a + got_b, own, got_own


def _rel_rows(rb):
    return rb[:, :0:-1][:, None, :]


def kernel(x, w_in, w_out, g_pre, g_post, w_alpha, b_alpha, g_gla, g_att, rel_bias, loss_target, m_w_in, m_w_out, m_g_pre, m_g_post, m_w_alpha, m_b_alpha, m_g_gla, m_g_att, m_rel_bias, v_w_in, v_w_out, v_g_pre, v_g_post, v_w_alpha, v_b_alpha, v_g_gla, v_g_att, v_rel_bias):
    nl = w_in.shape[0]
    ax, ay, ac = lax.axis_index("x"), lax.axis_index("y"), lax.axis_index("c")
    chip = 2 * ax + ay
    c_idx = jnp.reshape(ac, (1,)).astype(jnp.int32)
    chip_idx = jnp.reshape(chip, (1,)).astype(jnp.int32)

    phase = [SHARD * i % 16 for i in range(NCHIP)]
    wt_rows = jnp.transpose(w_in, (0, 2, 1)).astype(_CDT)
    at_phase = [functools.partial(jnp.pad, wt_rows, ((0, 0), (p, WSLOT - SHARD - p), (0, 0))) for p in phase]
    wt_src = lax.switch(chip, at_phase).reshape(nl, 2, WSLOT // 2, D)
    wout_src = w_out.astype(_CDT).reshape(nl, 2, D // NCHIP // 2, D)

    def with_own(own):
        start = [chip] + [0] * own.ndim
        return lax.dynamic_update_slice(lax.empty((NCHIP,) + own.shape, own.dtype), own[None], start)

    def gather_operands(l):
        return wt_src[l], wout_src[l], with_own(wt_src[l]), with_own(wout_src[l])

    def layer_weights(bufs):
        wt4 = bufs[0].reshape(NCHIP, WSLOT, D)
        wref = jnp.concatenate([wt4[i, phase[i]:phase[i] + SHARD] for i in range(NCHIP)])
        return _rows_to_internal(wref) + (bufs[1].reshape(D, D),)

    first = gather_operands(0)
    bin0, bout0, wa_all = _gather_first(first[0], first[1], w_alpha, first[2], first[3], with_own(w_alpha))
    wa_full = jnp.transpose(wa_all, (1, 2, 0, 3)).reshape(nl, RANK, GH * GDK)
    wa_pad = jnp.pad(wa_full, ((0, 0), (0, GAP - RANK), (0, 0))).astype(_CDT)
    rbx = [_rel_rows(rel_bias[l]) for l in range(nl)]

    def weights(l):
        return big[l] + (g_pre[l][None], g_post[l][None], wa_pad[l], b_alpha[l][None], g_gla[l][None], g_att[l][None], rbx[l])

    h = x[0]
    saved, big = [], [None] * nl
    big[0] = layer_weights((bin0, bout0))
    for l in range(nl):
        h, sv, bufs = _layer_fwd(h, *weights(l), ride=gather_operands(l + 1) if l + 1 < nl else None)
        saved.append(sv)
        if l + 1 < nl:
            big[l + 1] = layer_weights(bufs)
    dout, loss_part = _loss_grad(h, loss_target[0])

    small, hin, hout = [None] * nl, [None] * nl, [None] * nl
    hw = D // NCHIP // 2

    def reduce_owner(sent, got):
        rin_a, rout_a, rin_b, rout_b = got
        return (_add_chips(chip_idx, sent[0], rin_a, rin_b, 48, "add_chips_in"),
                _add_chips(chip_idx, sent[1], rout_a, rout_b, 128, "add_chips_out"))

    def partial_sums(dwin, dwout):
        gt = _rows_from_internal(*dwin)
        slabs = jnp.stack([gt[_slab_lo(i):_slab_lo(i) + SLAB] for i in range(NCHIP)])
        gin2 = jnp.transpose(slabs.reshape(NCHIP, 2, HSLAB, D), (1, 0, 2, 3)).reshape(2, NCHIP * HSLAB, D)
        gout2 = jnp.transpose(dwout.reshape(NCHIP, 2, hw, D), (1, 0, 2, 3)).reshape(2, NCHIP * hw, D)
        rin, rout = _swap_halves(gin2, gout2)
        return (_add_halves(c_idx, gin2, rin, 192, "add_halves_in").reshape(1, NCHIP, HSLAB, D),
                _add_halves(c_idx, gout2, rout, 256, "add_halves_out").reshape(1, NCHIP, hw, D))

    sent = None
    for l in reversed(range(nl)):
        dout, dwin, dwout, small[l], got, own, got_own = _layer_bwd(dout, saved[l], *weights(l), send=sent,
                                                                    reduce_now=partial_sums if l == 0 else None)
        if sent is not None:
            hin[l + 1], hout[l + 1] = reduce_owner(sent, got)
        sent = partial_sums(dwin, dwout) if l > 0 else own
    grad_x = dout[None]
    hin[0], hout[0] = reduce_owner(sent, list(_send_to_owners(*sent, parts=(0,))) + list(got_own))
    xchg = _exchange_halves(hin + hout)
    hin, xin = jnp.concatenate(hin), jnp.concatenate(xchg[:nl])
    hout, xout = jnp.concatenate(hout), jnp.concatenate(xchg[nl:])

    slab = jnp.concatenate([jnp.where(ac == 0, hin, xin), jnp.where(ac == 0, xin, hin)], axis=1)
    off = sum(jnp.where(chip == i, SHARD * i - _slab_lo(i), 0) for i in range(NCHIP))
    g_rows = jnp.transpose(lax.dynamic_slice_in_dim(slab, off, SHARD, axis=1), (1, 0, 2))
    rows_first = lambda t: jnp.transpose(t, (2, 0, 1))
    d_rows, nm_rows, nv_rows = _adam_rows(rows_first(w_in), g_rows, rows_first(m_w_in), rows_first(v_w_in), 32, "adam_w_in")
    g_w_in, d_w_in, nm_w_in, nv_w_in = [jnp.transpose(t, (1, 2, 0)) for t in (g_rows, d_rows, nm_rows, nv_rows)]

    def adam_big(w, g_own, g_other, m, v, name):
        shp = w.shape
        halves = lambda t: t.reshape(shp[0], 2, shp[1] // 2, shp[2])
        return [t.reshape(shp) for t in _adam_halves(c_idx, halves(w), g_own, g_other, halves(m), halves(v), 256, name)]

    g_w_out, d_w_out, nm_w_out, nv_w_out = adam_big(w_out, hout, xout, m_w_out, v_w_out, "adam_w_out")

    stacked = [jnp.stack([small[l][i] for l in range(nl)]) for i in range(7)] + [loss_part]
    g_small = _unpack_rows(_allreduce_small(_pack_rows(stacked)), [t.shape for t in stacked])
    g_gpre, g_gpost, g_wa_full, g_ba, g_ggla, g_gatt, g_rb, loss_sum = g_small
    loss = loss_sum[0, 0]
    g_wa = lax.dynamic_slice_in_dim(g_wa_full, chip * GDK, GDK, axis=2)
    names = [(g_pre, m_g_pre, v_g_pre, g_gpre), (g_post, m_g_post, v_g_post, g_gpost), (w_alpha, m_w_alpha, v_w_alpha, g_wa),
             (b_alpha, m_b_alpha, v_b_alpha, g_ba), (g_gla, m_g_gla, v_g_gla, g_ggla), (g_att, m_g_att, v_g_att, g_gatt),
             (rel_bias, m_rel_bias, v_rel_bias, g_rb)]
    shapes = [t[0].shape for t in names]
    packed = [_pack_rows([t[i] for t in names]) for i in range(4)]
    d_s, nm_s, nv_s = [_unpack_rows(t, shapes) for t in _adam(packed[0], packed[3], packed[1], packed[2], packed[0].shape[0], "adam_small")]

    grads = [g_w_in, g_w_out, g_gpre, g_gpost, g_wa, g_ba, g_ggla, g_gatt, g_rb]
    deltas = [d_w_in, d_w_out] + d_s
    new_m = [nm_w_in, nm_w_out] + nm_s
    new_v = [nv_w_in, nv_w_out] + nv_s
    return (loss, grad_x, *grads, *deltas, *new_m, *new_v)
```

```python
import functools

import jax
import jax.numpy as jnp
from jax import lax
from jax.experimental import pallas as pl
from jax.experimental.pallas import tpu as pltpu

D = 2048
DEPTH = 4
CHUNK = 64
GH, GDK, GDV = 4, 128, 256
DGLA = GH * GDV
RANK = 16
TAU = 16.0
AH, AHD = 8, 128
DATT = AH * AHD
LEFT = 8
NREL = 257
EPS = 1e-6
DIN = 7184
ADAM_LR, ADAM_B1, ADAM_B2, ADAM_EPS, ADAM_WD, ADAM_STEP = 0.001, 0.9, 0.999, 1e-08, 0.01, 10

GW = 2 * GDK + 2 * GDV
AW = 4 * AHD
ZG = GH * GW
ZA = AH * AW
ZM = ZG + ZA
GAP = 128
QB = 2 * CHUNK
HP = 2
BANDW = (LEFT + 2) * CHUNK
PADK = LEFT * CHUNK
NCHIP = 4
SHARD = DIN // NCHIP
SLAB = 1824
HSLAB = SLAB // 2
WSLOT = 1824
GPARTS_IN = [(0, 368), (368, 192), (560, 352)]
GPARTS_OUT = [(0, 112), (112, 48), (160, 96)]
NDIRECT = len(GPARTS_IN)
SPLIT_IN, SPLIT_OUT = 336, 128
NEG = -1e30
F32 = jnp.float32
_CDT = jnp.bfloat16
_XDT = jnp.bfloat16
_VMEM = 56 * 1024 * 1024
MESH = pl.DeviceIdType.MESH
ANY = pl.BlockSpec(memory_space=pl.ANY)


def _dot(a, b):
    return jnp.dot(a, b, preferred_element_type=F32)


def _dot_nt(a, b):
    return lax.dot_general(a, b, (((1,), (1,)), ((), ())), preferred_element_type=F32)


def _dot_tn(a, b):
    return lax.dot_general(a, b, (((0,), (0,)), ((), ())), preferred_element_type=F32)


def _rms_rows(v):
    return lax.rsqrt(jnp.mean(v * v, axis=-1, keepdims=True) + EPS)


def _sigmoid(v):
    return 1.0 / (1.0 + jnp.exp(-v))


def _log_sigmoid(v):
    return jnp.minimum(v, 0.0) - jnp.log(1.0 + jnp.exp(-jnp.abs(v)))


def _exact_dot(tri, v):
    hi = v.astype(_CDT)
    r1 = v - hi.astype(F32)
    mid = r1.astype(_CDT)
    lo = (r1 - mid.astype(F32)).astype(_CDT)
    return _dot(tri, hi) + _dot(tri, mid) + _dot(tri, lo)


def _tri(strict):
    row = lax.broadcasted_iota(jnp.int32, (CHUNK, CHUNK), 0)
    col = lax.broadcasted_iota(jnp.int32, (CHUNK, CHUNK), 1)
    return jnp.where((col < row) if strict else (col <= row), 1.0, 0.0).astype(_CDT)


def _norm_gate_bwd(o, g, gate, dcat):
    r = _rms_rows(o)
    oh = o * r
    sg = _sigmoid(gate)
    dn = dcat * (gate * sg)
    dgate = dcat * (oh * g) * (sg * (1.0 + gate * (1.0 - sg)))
    dg = jnp.sum(dn * oh, axis=0, keepdims=True)
    dnn = dn * g
    do = r * (dnn - oh * jnp.mean(dnn * oh, axis=-1, keepdims=True))
    return do, dgate, dg


def _params(sem=None, vmem=_VMEM):
    return pltpu.CompilerParams(dimension_semantics=sem, vmem_limit_bytes=vmem)


def _inproj(x, g, wm, wga, ride=None, tm=512, tn=1024):
    s = x.shape[0]
    gr = _gather_ride(ride, 0)

    def body(*refs):
        (x_ref, g_ref, wm_ref, wga_ref, z_ref, zga_ref, h_ref, hs), rr = gr.split(refs, 4, 3)
        i, j = pl.program_id(0), pl.program_id(1)
        gr.start(rr, (i == 0) & (j == 0))

        @pl.when(pl.program_id(1) == 0)
        def _():
            xv = x_ref[...]
            hv = (xv * _rms_rows(xv) * g_ref[...]).astype(_CDT)
            hs[...] = hv
            h_ref[...] = hv
            zga_ref[...] = _dot_nt(hv, wga_ref[...]).astype(_CDT)

        z_ref[...] = _dot_nt(hs[...], wm_ref[...]).astype(_CDT)
        gr.wait(rr, (i == s // tm - 1) & (j == ZM // tn - 1))

    return pl.pallas_call(
        body, name="inproj", grid=(s // tm, ZM // tn),
        in_specs=[pl.BlockSpec((tm, D), lambda i, j: (i, 0)), pl.BlockSpec((1, D), lambda i, j: (0, 0)),
                  pl.BlockSpec((tn, D), lambda i, j: (j, 0)), pl.BlockSpec((GAP, D), lambda i, j: (0, 0))] + gr.in_specs,
        out_specs=[pl.BlockSpec((tm, tn), lambda i, j: (i, j)), pl.BlockSpec((tm, GAP), lambda i, j: (i, 0)),
                   pl.BlockSpec((tm, D), lambda i, j: (i, 0))] + gr.out_specs,
        out_shape=[jax.ShapeDtypeStruct((s, ZM), _CDT), jax.ShapeDtypeStruct((s, GAP), _CDT),
                   jax.ShapeDtypeStruct((s, D), _CDT)] + gr.out_shape,
        scratch_shapes=[pltpu.VMEM((tm, D), _CDT)] + gr.scratch, input_output_aliases=gr.alias(4, 3),
        compiler_params=_params(("arbitrary", "arbitrary")),
    )(x, g, wm, wga, *gr.operands)


def _gla_fwd(z, zga, wa, ba, ggla, ride=None):
    s = z.shape[0]
    nc = s // CHUNK
    gr = _gather_ride(ride, 1)

    def body(*refs):
        (zg_ref, zga_ref, wa_ref, ba_ref, g_ref, cat_ref, la_s, st), rr = gr.split(refs, 5, 1)
        gr.start(rr, pl.program_id(0) == 0)
        la_s[...] = _log_sigmoid(_dot(zga_ref[...], wa_ref[...]) + ba_ref[...]) * (1.0 / TAU)
        st[...] = jnp.zeros_like(st)
        tri = _tri(False)

        def step(n, carry):
            rows = pl.ds(pl.multiple_of(n * CHUNK, CHUNK), CHUNK)
            for p in range(HP):
                z0 = p * GW
                la = la_s[rows, p * GDK:(p + 1) * GDK]
                lc = _exact_dot(tri, la)
                lend = jnp.sum(la, axis=0, keepdims=True)
                kdec = (zg_ref[rows, z0 + GDK:z0 + 2 * GDK].astype(F32) * jnp.exp(lend - lc)).astype(_CDT)
                stn = jnp.exp(lend) * st[p] + _dot_tn(zg_ref[rows, z0 + 2 * GDK:z0 + 2 * GDK + GDV], kdec)
                st[p] = stn
                qs = (zg_ref[rows, z0:z0 + GDK].astype(F32) * (GDK ** -0.5)).astype(_CDT)
                o = _dot_nt(qs, stn.astype(_CDT))
                gate = zg_ref[rows, z0 + 2 * GDK + GDV:z0 + GW].astype(F32)
                gain = g_ref[:, p * GDV:(p + 1) * GDV]
                cat_ref[rows, p * GDV:(p + 1) * GDV] = (o * _rms_rows(o) * gain * (gate * _sigmoid(gate))).astype(_CDT)
            return carry

        lax.fori_loop(0, nc, step, 0, unroll=4)
        gr.wait(rr, pl.program_id(0) == GH // HP - 1)

    return pl.pallas_call(
        body, name="gla_fwd", grid=(GH // HP,),
        in_specs=[pl.BlockSpec((s, HP * GW), lambda h: (0, h)), pl.BlockSpec((s, GAP), lambda h: (0, 0)),
                  pl.BlockSpec((GAP, HP * GDK), lambda h: (0, h)), pl.BlockSpec((1, HP * GDK), lambda h: (0, h)),
                  pl.BlockSpec((1, HP * GDV), lambda h: (0, h))] + gr.in_specs,
        out_specs=[pl.BlockSpec((s, HP * GDV), lambda h: (0, h))] + gr.out_specs,
        out_shape=[jax.ShapeDtypeStruct((s, DGLA), _CDT)] + gr.out_shape,
        scratch_shapes=[pltpu.VMEM((s, HP * GDK), F32), pltpu.VMEM((HP, GDV, GDK), F32)] + gr.scratch,
        input_output_aliases=gr.alias(5, 1), compiler_params=_params(("arbitrary",)),
    )(z, zga, wa, ba, ggla, *gr.operands)


def _band_bias(b0):
    row = lax.broadcasted_iota(jnp.int32, (QB, 256), 0)
    col = lax.broadcasted_iota(jnp.int32, (QB, 256), 1)
    lane = lax.broadcasted_iota(jnp.int32, (1, 256), 1)
    c0 = jnp.sum(jnp.where(lane == 0, b0, 0.0), axis=1, keepdims=True)
    xv = jnp.broadcast_to(b0, (QB, 256))
    for bit in range(7):
        xv = jnp.where(((row >> bit) & 1) == 1, pltpu.roll(xv, 1 << bit, 1), xv)
    xv = jnp.where(col < row, c0, xv)
    return jnp.concatenate([jnp.broadcast_to(c0, (QB, BANDW - 256)), xv], axis=1)


def _band_static_mask():
    row = lax.broadcasted_iota(jnp.int32, (QB, BANDW), 0) >> 6
    col = lax.broadcasted_iota(jnp.int32, (QB, BANDW), 1) >> 6
    return (col >= row) & (col <= row + LEFT)


def _fold_bias_grad(t):
    row = lax.broadcasted_iota(jnp.int32, (QB, 256), 0)
    col = lax.broadcasted_iota(jnp.int32, (QB, 256), 1)
    xv = t[:, BANDW - 256:]
    low = col < row
    far = jnp.sum(t[:, 0:BANDW - 256], axis=1, keepdims=True) + jnp.sum(jnp.where(low, xv, 0.0), axis=1, keepdims=True)
    far = jnp.sum(far, axis=0, keepdims=True)
    xv = jnp.where(low, 0.0, xv)
    for bit in range(7):
        xv = jnp.where(((row >> bit) & 1) == 1, pltpu.roll(xv, 256 - (1 << bit), 1), xv)
    dp = jnp.sum(xv, axis=0, keepdims=True)
    lane = lax.broadcasted_iota(jnp.int32, (1, 256), 1)
    return dp + jnp.where(lane == 0, far, 0.0)


def _att_fwd(z, rbx, gatt, ride=None):
    s = z.shape[0]
    nb = s // QB
    gr = _gather_ride(ride, 2)

    def body(*refs):
        (za_ref, rb_ref, g_ref, cat_ref, o_ref, lse_ref, kp, vp, bias_s), rr = gr.split(refs, 3, 3)
        gr.start(rr, pl.program_id(0) == 0)
        for p in range(HP):
            z0 = p * AW
            kp[p, 0:PADK, :] = jnp.zeros((PADK, AHD), _CDT)
            vp[p, 0:PADK, :] = jnp.zeros((PADK, AHD), _CDT)
            kp[p, PADK:, :] = za_ref[:, z0 + AHD:z0 + 2 * AHD]
            vp[p, PADK:, :] = za_ref[:, z0 + 2 * AHD:z0 + 3 * AHD]
            bias_s[p] = jnp.where(_band_static_mask(), _band_bias(rb_ref[p]), NEG)

        def step(b, carry):
            r0 = pl.multiple_of(b * QB, QB)
            rows = pl.ds(r0, QB)
            band = pl.ds(r0, BANDW)
            live = lax.broadcasted_iota(jnp.int32, (QB, BANDW), 1) >= PADK - r0
            for p in range(HP):
                z0 = p * AW
                cols = slice(p * AHD, (p + 1) * AHD)
                sc = _dot_nt(za_ref[rows, z0:z0 + AHD], kp[p, band, :]) * (AHD ** -0.5) + bias_s[p]
                sc = jnp.where(live, sc, NEG)
                m = jnp.max(sc, axis=-1, keepdims=True)
                pr = jnp.exp(sc - m)
                l = jnp.sum(pr, axis=-1, keepdims=True)
                o = _dot((pr * (1.0 / l)).astype(_CDT), vp[p, band, :])
                o_ref[rows, cols] = o.astype(_CDT)
                lse_ref[rows, cols] = jnp.broadcast_to(m + jnp.log(l), (QB, AHD))
                gate = za_ref[rows, z0 + 3 * AHD:z0 + AW].astype(F32)
                cat_ref[rows, cols] = (o * _rms_rows(o) * g_ref[:, cols] * (gate * _sigmoid(gate))).astype(_CDT)
            return carry

        lax.fori_loop(0, nb, step, 0, unroll=4)
        gr.wait(rr, pl.program_id(0) == AH // HP - 1)

    return pl.pallas_call(
        body, name="att_fwd", grid=(AH // HP,),
        in_specs=[pl.BlockSpec((s, HP * AW), lambda h: (0, ZG // (HP * AW) + h)), pl.BlockSpec((HP, 1, 256), lambda h: (h, 0, 0)),
                  pl.BlockSpec((1, HP * AHD), lambda h: (0, h))] + gr.in_specs,
        out_specs=[pl.BlockSpec((s, HP * AHD), lambda h: (0, h)), pl.BlockSpec((s, HP * AHD), lambda h: (0, h)),
                   pl.BlockSpec((s, HP * AHD), lambda h: (0, h))] + gr.out_specs,
        out_shape=[jax.ShapeDtypeStruct((s, DATT), _CDT), jax.ShapeDtypeStruct((s, DATT), _CDT),
                   jax.ShapeDtypeStruct((s, DATT), F32)] + gr.out_shape,
        scratch_shapes=[pltpu.VMEM((HP, s + PADK, AHD), _CDT), pltpu.VMEM((HP, s + PADK, AHD), _CDT),
                        pltpu.VMEM((HP, QB, BANDW), F32)] + gr.scratch, input_output_aliases=gr.alias(3, 3),
        compiler_params=_params(("arbitrary",)),
    )(z, rbx, gatt, *gr.operands)


def _outproj(cg, ca, wout, x, gpost, ride=None, tm=256):
    s = x.shape[0]
    gr = _gather_ride(ride, NDIRECT)

    def body(*refs):
        (cg_ref, ca_ref, w_ref, x_ref, g_ref, y_ref, xo_ref), rr = gr.split(refs, 5, 2)
        gr.start(rr, pl.program_id(0) == 0)
        y = _dot(cg_ref[...], w_ref[0:DGLA, :]) + _dot(ca_ref[...], w_ref[DGLA:, :])
        y_ref[...] = y
        xo_ref[...] = x_ref[...] + y * _rms_rows(y) * g_ref[...]
        gr.wait(rr, pl.program_id(0) == s // tm - 1)

    return pl.pallas_call(
        body, name="outproj", grid=(s // tm,),
        in_specs=[pl.BlockSpec((tm, DGLA), lambda i: (i, 0)), pl.BlockSpec((tm, DATT), lambda i: (i, 0)),
                  pl.BlockSpec((D, D), lambda i: (0, 0)), pl.BlockSpec((tm, D), lambda i: (i, 0)),
                  pl.BlockSpec((1, D), lambda i: (0, 0))] + gr.in_specs,
        out_specs=[pl.BlockSpec((tm, D), lambda i: (i, 0)), pl.BlockSpec((tm, D), lambda i: (i, 0))] + gr.out_specs,
        out_shape=[jax.ShapeDtypeStruct((s, D), F32), jax.ShapeDtypeStruct((s, D), F32)] + gr.out_shape,
        scratch_shapes=gr.scratch, input_output_aliases=gr.alias(5, 2),
        compiler_params=_params(("arbitrary",)),
    )(cg, ca, wout, x, gpost, *gr.operands)


def _loss_grad(xo, tgt, tm=256):
    s = xo.shape[0]

    def body(xo_ref, t_ref, d_ref, l_ref):
        @pl.when(pl.program_id(0) == 0)
        def _():
            l_ref[...] = jnp.zeros_like(l_ref)

        e = xo_ref[...] - t_ref[...]
        d_ref[...] = e * (1.0 / D)
        l_ref[...] += jnp.sum(jnp.sum(e * e, axis=1, keepdims=True), axis=0, keepdims=True) * (0.5 / D)

    return pl.pallas_call(
        body, name="loss_grad", grid=(s // tm,),
        in_specs=[pl.BlockSpec((tm, D), lambda i: (i, 0)), pl.BlockSpec((tm, D), lambda i: (i, 0))],
        out_specs=[pl.BlockSpec((tm, D), lambda i: (i, 0)), pl.BlockSpec((1, 1), lambda i: (0, 0))],
        out_shape=[jax.ShapeDtypeStruct((s, D), F32), jax.ShapeDtypeStruct((1, 1), F32)],
        compiler_params=_params(("arbitrary",)),
    )(xo, tgt)


def _place():
    x, y, c = lax.axis_index("x"), lax.axis_index("y"), lax.axis_index("c")
    chips = [(1 - x, y), (x, 1 - y), (1 - x, 1 - y)]
    return x, y, c, chips


def _variants(fn):
    x, y, c, _ = _place()
    for jx in range(2):
        for jy in range(2):
            for jc in range(2):
                pl.when((x == jx) & (y == jy) & (c == jc))(functools.partial(fn, jx, jy, jc))


def _gather_copies(part, x, y, c, src_in, src_out, buf_in, buf_out, send_sems, recv_sems):
    me = 2 * x + y
    peers = [(1 - x, y), (x, 1 - y), (1 - x, 1 - y)]
    cps = []
    for k, (px, py) in enumerate(peers):
        for a, (src, buf, parts) in enumerate([(src_in, buf_in, GPARTS_IN), (src_out, buf_out, GPARTS_OUT)]):
            if part < NDIRECT:
                rows = pl.ds(*parts[part])
                pair = (src.at[c, rows], buf.at[me, c, rows], (px, py, c))
            else:
                piece = buf.at[2 * px + py, c]
                pair = (piece, piece, (x, y, 1 - c))
            cps.append(pltpu.make_async_remote_copy(src_ref=pair[0], dst_ref=pair[1], send_sem=send_sems.at[a, k],
                                                    recv_sem=recv_sems.at[a, k], device_id=pair[2], device_id_type=MESH))
    return cps


class _gather_ride:
    def __init__(self, ride, part):
        self.part, self.on = part, ride is not None
        self.in_specs, self.out_specs, self.out_shape, self.scratch, self.operands, self.aliases = [], [], [], [], [], {}
        if self.on:
            self.operands = list(ride)
            self.in_specs, self.out_specs = [ANY] * 4, [ANY] * 2
            self.out_shape = [jax.ShapeDtypeStruct(t.shape, t.dtype) for t in ride[2:]]
            self.scratch = [pltpu.SemaphoreType.DMA((2, 3)), pltpu.SemaphoreType.DMA((2, 3))]

    def alias(self, n_in, n_out):
        return {n_in + 2: n_out, n_in + 3: n_out + 1} if self.on else {}

    def split(self, refs, n_in, n_out):
        if not self.on:
            return refs, None
        own = refs[:n_in] + refs[n_in + 4:n_in + 4 + n_out] + refs[n_in + 6 + n_out:-2]
        return own, refs[n_in:n_in + 2] + refs[n_in + 4 + n_out:n_in + 6 + n_out] + refs[-2:]

    def start(self, ride_refs, first):
        if self.on:
            def go(x, y, c):
                for cpy in _gather_copies(self.part, x, y, c, *ride_refs):
                    cpy.start()
            pl.when(first)(lambda: _variants(go))

    def wait(self, ride_refs, last):
        if self.on:
            def done(x, y, c):
                for cpy in _gather_copies(self.part, x, y, c, *ride_refs):
                    cpy.wait()
            pl.when(last)(lambda: _variants(done))


def _part_rows(total, split, part):
    return (0, split) if part == 0 else (split, total - split)


def _owner_copies(part, pin_ref, pout_ref, rin, rout, send_sems, recv_sems):
    x, y, c, chips = _place()
    cps = []
    for k, (px, py) in enumerate(chips):
        for a, (src, dst, split) in enumerate([(pin_ref, rin, SPLIT_IN), (pout_ref, rout, SPLIT_OUT)]):
            r0, n = _part_rows(src.shape[2], split, part)
            cps.append(pltpu.make_async_remote_copy(src_ref=src.at[0, 2 * px + py, pl.ds(r0, n)], dst_ref=dst.at[k],
                                                    send_sem=send_sems.at[a, k], recv_sem=recv_sems.at[a, k],
                                                    device_id=(px, py, c), device_id_type=MESH))
    return cps


class _ride_specs:
    def __init__(self, send, part):
        self.in_specs, self.out_specs, self.out_shape, self.scratch, self.operands = [], [], [], [], []
        if send is not None:
            self.in_specs, self.out_specs, self.operands = [ANY, ANY], [ANY, ANY], list(send)
            self.out_shape = [jax.ShapeDtypeStruct((3, _part_rows(t.shape[2], split, part)[1], D), t.dtype)
                              for t, split in zip(send, (SPLIT_IN, SPLIT_OUT))]
            self.scratch = [pltpu.SemaphoreType.DMA((2, 3)), pltpu.SemaphoreType.DMA((2, 3))]


def _ride_refs(refs, send, n_out):
    if send is None:
        return None, refs
    pin_ref, pout_ref = refs[:2]
    own_out = refs[2:2 + n_out]
    rin, rout = refs[2 + n_out:4 + n_out]
    return (pin_ref, pout_ref, rin, rout, refs[-2], refs[-1]), tuple(own_out) + tuple(refs[4 + n_out:-2])


def _ride_start(refs, send, n_out, part, first=None):
    ride = _ride_refs(refs, send, n_out)[0]
    if ride is None:
        return []
    cps = _owner_copies(part, *ride)

    @pl.when(pl.program_id(0) == 0 if first is None else first)
    def _():
        for cpy in cps:
            cpy.start()

    return cps


def _ride_wait(cps, steps, last=None):
    if cps:
        @pl.when(pl.program_id(0) == steps - 1 if last is None else last)
        def _():
            for cpy in cps:
                cpy.wait()


def _post_bwd(dout, y, gpost, wout, tm=256):
    s = y.shape[0]

    def body(d_ref, y_ref, g_ref, w_ref, dy_ref, dcg_ref, dca_ref, dg_ref):
        @pl.when(pl.program_id(0) == 0)
        def _():
            dg_ref[...] = jnp.zeros_like(dg_ref)

        yv = y_ref[...]
        r = _rms_rows(yv)
        yh = yv * r
        dv = d_ref[...]
        dg_ref[...] += jnp.sum(dv * yh, axis=0, keepdims=True)
        dn = dv * g_ref[...]
        dyb = (r * (dn - yh * jnp.mean(dn * yh, axis=-1, keepdims=True))).astype(_CDT)
        dy_ref[...] = dyb
        dcg_ref[...] = _dot_nt(dyb, w_ref[0:DGLA, :]).astype(_CDT)
        dca_ref[...] = _dot_nt(dyb, w_ref[DGLA:, :]).astype(_CDT)

    return pl.pallas_call(
        body, name="post_bwd", grid=(s // tm,),
        in_specs=[pl.BlockSpec((tm, D), lambda i: (i, 0)), pl.BlockSpec((tm, D), lambda i: (i, 0)),
                  pl.BlockSpec((1, D), lambda i: (0, 0)), pl.BlockSpec((D, D), lambda i: (0, 0))],
        out_specs=[pl.BlockSpec((tm, D), lambda i: (i, 0)), pl.BlockSpec((tm, DGLA), lambda i: (i, 0)),
                   pl.BlockSpec((tm, DATT), lambda i: (i, 0)), pl.BlockSpec((1, D), lambda i: (0, 0))],
        out_shape=[jax.ShapeDtypeStruct((s, D), _CDT), jax.ShapeDtypeStruct((s, DGLA), _CDT),
                   jax.ShapeDtypeStruct((s, DATT), _CDT), jax.ShapeDtypeStruct((1, D), F32)],
        compiler_params=_params(("arbitrary",)),
    )(dout, y, gpost, wout)


def _matmul_tn(a, b, out_dtype, tm, tn, name):
    k, m = a.shape
    n = b.shape[1]

    def body(a_ref, b_ref, o_ref):
        o_ref[...] = _dot_tn(a_ref[...], b_ref[...]).astype(out_dtype)

    return pl.pallas_call(
        body, name=name, grid=(m // tm, n // tn),
        in_specs=[pl.BlockSpec((k, tm), lambda i, j: (0, i)), pl.BlockSpec((k, tn), lambda i, j: (0, j))],
        out_specs=pl.BlockSpec((tm, tn), lambda i, j: (i, j)),
        out_shape=jax.ShapeDtypeStruct((m, n), out_dtype),
        compiler_params=_params(("parallel", "parallel")),
    )(a, b)


def _att_bwd(z, oraw, lse, dca, rbx, gatt, send=None):
    s = z.shape[0]
    nb = s // QB

    def body(*refs):
        za_ref, o_ref, lse_ref, dc_ref, rb_ref, g_ref = refs[:6]
        dz_ref, dg_ref, db_ref, kp, vp, dkp, dvp, bias_s, t_s, dg_s = _ride_refs(refs[6:], send, 3)[1]
        cps = _ride_start(refs[6:], send, 3, 0)
        for p in range(HP):
            z0 = p * AW
            kp[p, 0:PADK, :] = jnp.zeros((PADK, AHD), _CDT)
            vp[p, 0:PADK, :] = jnp.zeros((PADK, AHD), _CDT)
            kp[p, PADK:, :] = za_ref[:, z0 + AHD:z0 + 2 * AHD]
            vp[p, PADK:, :] = za_ref[:, z0 + 2 * AHD:z0 + 3 * AHD]
            bias_s[p] = jnp.where(_band_static_mask(), _band_bias(rb_ref[p]), NEG)
        dkp[...] = jnp.zeros_like(dkp)
        dvp[...] = jnp.zeros_like(dvp)
        t_s[...] = jnp.zeros_like(t_s)
        dg_s[...] = jnp.zeros_like(dg_s)

        def step(b, carry):
            r0 = pl.multiple_of(b * QB, QB)
            rows = pl.ds(r0, QB)
            band = pl.ds(r0, BANDW)
            live = lax.broadcasted_iota(jnp.int32, (QB, BANDW), 1) >= PADK - r0
            for p in range(HP):
                z0 = p * AW
                cols = slice(p * AHD, (p + 1) * AHD)
                o = o_ref[rows, cols].astype(F32)
                do, dgate, dg = _norm_gate_bwd(o, g_ref[:, cols], za_ref[rows, z0 + 3 * AHD:z0 + AW].astype(F32),
                                               dc_ref[rows, cols].astype(F32))
                dg_s[:, cols] += dg
                q = za_ref[rows, z0:z0 + AHD]
                kb = kp[p, band, :]
                sc = _dot_nt(q, kb) * (AHD ** -0.5) + bias_s[p]
                sc = jnp.where(live, sc, NEG)
                pr = jnp.exp(sc - jnp.max(lse_ref[rows, cols], axis=-1, keepdims=True))
                dob = do.astype(_CDT)
                dp = _dot_nt(dob, vp[p, band, :])
                ds = pr * (dp - jnp.sum(do * o, axis=-1, keepdims=True))
                t_s[p] += ds
                dsb = (ds * (AHD ** -0.5)).astype(_CDT)
                dz_ref[rows, z0:z0 + AHD] = _dot(dsb, kb).astype(_CDT)
                dz_ref[rows, z0 + 3 * AHD:z0 + AW] = dgate.astype(_CDT)
                dkp[p, band, :] += _dot_tn(dsb, q)
                dvp[p, band, :] += _dot_tn(pr.astype(_CDT), dob)
            return carry

        lax.fori_loop(0, nb, step, 0, unroll=2)
        for p in range(HP):
            z0 = p * AW
            dz_ref[:, z0 + AHD:z0 + 2 * AHD] = dkp[p, PADK:, :].astype(_CDT)
            dz_ref[:, z0 + 2 * AHD:z0 + 3 * AHD] = dvp[p, PADK:, :].astype(_CDT)
            db_ref[p] = _fold_bias_grad(t_s[p])
        dg_ref[...] = dg_s[...]
        _ride_wait(cps, AH // HP)

    ride = _ride_specs(send, 0)
    return pl.pallas_call(
        body, name="att_bwd", grid=(AH // HP,),
        in_specs=[pl.BlockSpec((s, HP * AW), lambda h: (0, ZG // (HP * AW) + h)), pl.BlockSpec((s, HP * AHD), lambda h: (0, h)),
                  pl.BlockSpec((s, HP * AHD), lambda h: (0, h)), pl.BlockSpec((s, HP * AHD), lambda h: (0, h)),
                  pl.BlockSpec((HP, 1, 256), lambda h: (h, 0, 0)), pl.BlockSpec((1, HP * AHD), lambda h: (0, h))] + ride.in_specs,
        out_specs=[pl.BlockSpec((s, HP * AW), lambda h: (0, h)), pl.BlockSpec((1, HP * AHD), lambda h: (0, h)),
                   pl.BlockSpec((HP, 1, 256), lambda h: (h, 0, 0))] + ride.out_specs,
        out_shape=[jax.ShapeDtypeStruct((s, ZA), _CDT), jax.ShapeDtypeStruct((1, DATT), F32),
                   jax.ShapeDtypeStruct((AH, 1, 256), F32)] + ride.out_shape,
        scratch_shapes=[pltpu.VMEM((HP, s + PADK, AHD), _CDT), pltpu.VMEM((HP, s + PADK, AHD), _CDT),
                        pltpu.VMEM((HP, s + PADK, AHD), F32), pltpu.VMEM((HP, s + PADK, AHD), F32),
                        pltpu.VMEM((HP, QB, BANDW), F32), pltpu.VMEM((HP, QB, BANDW), F32), pltpu.VMEM((1, HP * AHD), F32)] + ride.scratch,
        compiler_params=_params(("arbitrary",)),
    )(z, oraw, lse, dca, rbx, gatt, *ride.operands)


def _gla_bwd(z, zga, wa, ba, ggla, dcg, send=None):
    s = z.shape[0]
    nc = s // CHUNK

    def body(*refs):
        zg_ref, zga_ref, wa_ref, ba_ref, g_ref, dc_ref = refs[:6]
        (dz_ref, dga_ref, dwa_ref, dba_ref, dg_ref,
         la_s, om_s, sall, dpre_s, c_s, dga_s, dg_s) = _ride_refs(refs[6:], send, 5)[1]
        cps = _ride_start(refs[6:], send, 5, 1)
        h = pl.program_id(0)
        pre = _dot(zga_ref[...], wa_ref[...]) + ba_ref[...]
        la_s[...] = _log_sigmoid(pre) * (1.0 / TAU)
        om_s[...] = (1.0 - _sigmoid(pre)) * (1.0 / TAU)
        c_s[...] = jnp.zeros_like(c_s)
        dg_s[...] = jnp.zeros_like(dg_s)
        tri = _tri(False)
        tri_strict = _tri(True)

        def decay(rows, p):
            la = la_s[rows, p * GDK:(p + 1) * GDK]
            lend = jnp.sum(la, axis=0, keepdims=True)
            return jnp.exp(lend - _exact_dot(tri, la)), jnp.exp(lend)

        def fwd(n, sts):
            rows = pl.ds(pl.multiple_of(n * CHUNK, CHUNK), CHUNK)
            out = []
            for p in range(HP):
                z0 = p * GW
                dec, a = decay(rows, p)
                kdec = (zg_ref[rows, z0 + GDK:z0 + 2 * GDK].astype(F32) * dec).astype(_CDT)
                stn = a * sts[p] + _dot_tn(zg_ref[rows, z0 + 2 * GDK:z0 + 2 * GDK + GDV], kdec)
                sall[p, n] = stn
                out.append(stn)
            return tuple(out)

        lax.fori_loop(0, nc, fwd, tuple(jnp.zeros((GDV, GDK), F32) for _ in range(HP)), unroll=4)

        def bwd(i, carry):
            n = nc - 1 - i
            rows = pl.ds(pl.multiple_of(n * CHUNK, CHUNK), CHUNK)
            for p in range(HP):
                z0 = p * GW
                kc = slice(p * GDK, (p + 1) * GDK)
                vc = slice(p * GDV, (p + 1) * GDV)
                dec, a = decay(rows, p)
                kdec = zg_ref[rows, z0 + GDK:z0 + 2 * GDK].astype(F32) * dec
                kdb = kdec.astype(_CDT)
                v = zg_ref[rows, z0 + 2 * GDK:z0 + 2 * GDK + GDV]
                qs = (zg_ref[rows, z0:z0 + GDK].astype(F32) * (GDK ** -0.5)).astype(_CDT)
                stb = sall[p, n].astype(_CDT)
                st_prev = sall[p, jnp.maximum(n - 1, 0)] * jnp.where(n > 0, 1.0, 0.0)
                o = _dot_nt(qs, stb)
                do, dgate, dg = _norm_gate_bwd(o, g_ref[:, vc], zg_ref[rows, z0 + 2 * GDK + GDV:z0 + GW].astype(F32),
                                               dc_ref[rows, vc].astype(F32))
                dg_s[:, vc] += dg
                dob = do.astype(_CDT)
                gt = _dot_tn(dob, qs) + c_s[p]
                gtb = gt.astype(_CDT)
                da = jnp.sum(gt * st_prev, axis=0, keepdims=True)
                dkdec = _dot(v, gtb)
                dla = _exact_dot(tri_strict, dkdec * kdec) + da * a
                dpre_s[rows, kc] = dla * om_s[rows, kc]
                dz_ref[rows, z0:z0 + GDK] = (_dot(dob, stb) * (GDK ** -0.5)).astype(_CDT)
                dz_ref[rows, z0 + GDK:z0 + 2 * GDK] = (dkdec * dec).astype(_CDT)
                dz_ref[rows, z0 + 2 * GDK:z0 + 2 * GDK + GDV] = _dot_nt(kdb, gtb).astype(_CDT)
                dz_ref[rows, z0 + 2 * GDK + GDV:z0 + GW] = dgate.astype(_CDT)
                c_s[p] = a * gt
            return carry

        lax.fori_loop(0, nc, bwd, 0)
        dpre = dpre_s[...]
        dpb = dpre.astype(_CDT)
        dg_ref[...] = dg_s[...]
        dba_ref[...] = jnp.sum(dpre, axis=0, keepdims=True)
        dwa_ref[...] = _dot_tn(zga_ref[...], dpb)
        part = _dot_nt(dpb, wa_ref[...])

        @pl.when(h == 0)
        def _():
            dga_s[...] = part

        @pl.when(h > 0)
        def _():
            dga_s[...] += part

        @pl.when(h == GH // HP - 1)
        def _():
            dga_ref[...] = dga_s[...].astype(_CDT)

        _ride_wait(cps, GH // HP)

    ride = _ride_specs(send, 1)
    return pl.pallas_call(
        body, name="gla_bwd", grid=(GH // HP,),
        in_specs=[pl.BlockSpec((s, HP * GW), lambda h: (0, h)), pl.BlockSpec((s, GAP), lambda h: (0, 0)),
                  pl.BlockSpec((GAP, HP * GDK), lambda h: (0, h)), pl.BlockSpec((1, HP * GDK), lambda h: (0, h)),
                  pl.BlockSpec((1, HP * GDV), lambda h: (0, h)), pl.BlockSpec((s, HP * GDV), lambda h: (0, h))] + ride.in_specs,
        out_specs=[pl.BlockSpec((s, HP * GW), lambda h: (0, h)), pl.BlockSpec((s, GAP), lambda h: (0, 0)),
                   pl.BlockSpec((GAP, HP * GDK), lambda h: (0, h)), pl.BlockSpec((1, HP * GDK), lambda h: (0, h)),
                   pl.BlockSpec((1, HP * GDV), lambda h: (0, h))] + ride.out_specs,
        out_shape=[jax.ShapeDtypeStruct((s, ZG), _CDT), jax.ShapeDtypeStruct((s, GAP), _CDT),
                   jax.ShapeDtypeStruct((GAP, GH * GDK), F32), jax.ShapeDtypeStruct((1, GH * GDK), F32),
                   jax.ShapeDtypeStruct((1, DGLA), F32)] + ride.out_shape,
        scratch_shapes=[pltpu.VMEM((s, HP * GDK), F32), pltpu.VMEM((s, HP * GDK), F32), pltpu.VMEM((HP, nc, GDV, GDK), F32),
                        pltpu.VMEM((s, HP * GDK), F32), pltpu.VMEM((HP, GDV, GDK), F32), pltpu.VMEM((s, GAP), F32),
                        pltpu.VMEM((1, HP * GDV), F32)] + ride.scratch,
        compiler_params=_params(("arbitrary",)),
    )(z, zga, wa, ba, ggla, dcg, *ride.operands)


def _dh(dzg, dza, dga, wm, wga, x, dout, gpre, send=None, tm=512, tk=1024):
    s = x.shape[0]
    nkg, nk = ZG // tk, ZM // tk

    def body(*refs):
        dzg_ref, dza_ref, dga_ref, wm_ref, wga_ref, x_ref, d_ref, g_ref = refs[:8]
        dx_ref, dg_ref, acc = _ride_refs(refs[8:], send, 2)[1]
        i, k = pl.program_id(0), pl.program_id(1)
        cps = _ride_start(refs[8:], send, 2, 1, (i == 0) & (k == 0))

        @pl.when((i == 0) & (k == 0))
        def _():
            dg_ref[...] = jnp.zeros_like(dg_ref)

        @pl.when(k == 0)
        def _():
            acc[...] = _dot(dga_ref[...], wga_ref[...])

        @pl.when(k < nkg)
        def _():
            acc[...] += _dot(dzg_ref[...], wm_ref[...])

        @pl.when(k >= nkg)
        def _():
            acc[...] += _dot(dza_ref[...], wm_ref[...])

        @pl.when(k == nk - 1)
        def _():
            xv = x_ref[...]
            r = _rms_rows(xv)
            xh = xv * r
            dh = acc[...]
            dg_ref[...] += jnp.sum(dh * xh, axis=0, keepdims=True)
            dn = dh * g_ref[...]
            dx_ref[...] = d_ref[...] + r * (dn - xh * jnp.mean(dn * xh, axis=-1, keepdims=True))

        _ride_wait(cps, 0, (i == s // tm - 1) & (k == nk - 1))

    ride = _ride_specs(send, 1)
    return pl.pallas_call(
        body, name="dh", grid=(s // tm, nk),
        in_specs=[pl.BlockSpec((tm, tk), lambda i, k: (i, jnp.minimum(k, nkg - 1))),
                  pl.BlockSpec((tm, tk), lambda i, k: (i, jnp.maximum(k - nkg, 0))),
                  pl.BlockSpec((tm, GAP), lambda i, k: (i, 0)), pl.BlockSpec((tk, D), lambda i, k: (k, 0)),
                  pl.BlockSpec((GAP, D), lambda i, k: (0, 0)), pl.BlockSpec((tm, D), lambda i, k: (i, 0)),
                  pl.BlockSpec((tm, D), lambda i, k: (i, 0)), pl.BlockSpec((1, D), lambda i, k: (0, 0))] + ride.in_specs,
        out_specs=[pl.BlockSpec((tm, D), lambda i, k: (i, 0)), pl.BlockSpec((1, D), lambda i, k: (0, 0))] + ride.out_specs,
        out_shape=[jax.ShapeDtypeStruct((s, D), F32), jax.ShapeDtypeStruct((1, D), F32)] + ride.out_shape,
        scratch_shapes=[pltpu.VMEM((tm, D), F32)] + ride.scratch,
        compiler_params=_params(("arbitrary", "arbitrary")),
    )(dzg, dza, dga, wm, wga, x, dout, gpre, *ride.operands)


def _adam(w, g, m, v, tr, name):
    rws, cols = w.shape

    def body(w_ref, g_ref, m_ref, v_ref, d_ref, mo_ref, vo_ref):
        gv = g_ref[...]
        mn = ADAM_B1 * m_ref[...] + (1.0 - ADAM_B1) * gv
        vn = ADAM_B2 * v_ref[...] + (1.0 - ADAM_B2) * (gv * gv)
        mh = mn / (1.0 - ADAM_B1 ** ADAM_STEP)
        vh = vn / (1.0 - ADAM_B2 ** ADAM_STEP)
        d_ref[...] = -ADAM_LR * (mh / (jnp.sqrt(vh) + ADAM_EPS) + ADAM_WD * w_ref[...])
        mo_ref[...] = mn
        vo_ref[...] = vn

    spec = pl.BlockSpec((tr, cols), lambda i: (i, 0))
    return pl.pallas_call(
        body, name=name, grid=(rws // tr,), in_specs=[spec] * 4, out_specs=[spec] * 3,
        out_shape=[jax.ShapeDtypeStruct((rws, cols), F32)] * 3,
        compiler_params=_params(("parallel",)),
    )(w, g, m, v)


def _adam_rows(w, g, m, v, tj, name):
    rows = w.shape[0]

    def body(w_ref, g_ref, m_ref, v_ref, d_ref, mo_ref, vo_ref):
        gv = g_ref[...]
        mn = ADAM_B1 * m_ref[...] + (1.0 - ADAM_B1) * gv
        vn = ADAM_B2 * v_ref[...] + (1.0 - ADAM_B2) * (gv * gv)
        mh = mn / (1.0 - ADAM_B1 ** ADAM_STEP)
        vh = vn / (1.0 - ADAM_B2 ** ADAM_STEP)
        d_ref[...] = -ADAM_LR * (mh / (jnp.sqrt(vh) + ADAM_EPS) + ADAM_WD * w_ref[...])
        mo_ref[...] = mn
        vo_ref[...] = vn

    spec = pl.BlockSpec((tj,) + w.shape[1:], lambda i: (i, 0, 0))
    return pl.pallas_call(
        body, name=name, grid=(pl.cdiv(rows, tj),), in_specs=[spec] * 4, out_specs=[spec] * 3,
        out_shape=[jax.ShapeDtypeStruct(w.shape, F32)] * 3,
        compiler_params=_params(("parallel",)),
    )(w, g, m, v)


def _gather_first(src_in, src_out, wa, buf_in, buf_out, wa_all):
    def variant(x, y, c, src_in_ref, src_out_ref, wa_ref, _b0, _b1, _b2, bin_ref, bout_ref, wa_ref_all, send_sems, recv_sems, wa_send, wa_recv):
        direct = []
        for part in range(NDIRECT):
            direct += _gather_copies(part, x, y, c, src_in_ref, src_out_ref, bin_ref, bout_ref, send_sems.at[part], recv_sems.at[part])
        for k, (px, py) in enumerate([(1 - x, y), (x, 1 - y), (1 - x, 1 - y)]):
            direct.append(pltpu.make_async_remote_copy(src_ref=wa_ref, dst_ref=wa_ref_all.at[2 * x + y], send_sem=wa_send.at[k],
                                                       recv_sem=wa_recv.at[k], device_id=(px, py, c), device_id_type=MESH))
        for cpy in direct:
            cpy.start()
        for cpy in direct:
            cpy.wait()
        passed = _gather_copies(NDIRECT, x, y, c, src_in_ref, src_out_ref, bin_ref, bout_ref, send_sems.at[NDIRECT], recv_sems.at[NDIRECT])
        for cpy in passed:
            cpy.start()
        for cpy in passed:
            cpy.wait()

    def body(*refs):
        _variants(lambda x, y, c: variant(x, y, c, *refs))

    return pl.pallas_call(
        body, name="gather_first", in_specs=[ANY] * 6, out_specs=[ANY] * 3, input_output_aliases={3: 0, 4: 1, 5: 2},
        out_shape=[jax.ShapeDtypeStruct(t.shape, t.dtype) for t in (buf_in, buf_out, wa_all)],
        scratch_shapes=[pltpu.SemaphoreType.DMA((NDIRECT + 1, 2, 3)), pltpu.SemaphoreType.DMA((NDIRECT + 1, 2, 3)),
                        pltpu.SemaphoreType.DMA((3,)), pltpu.SemaphoreType.DMA((3,))],
    )(src_in, src_out, wa, buf_in, buf_out, wa_all)


def _swap_halves(gin2, gout2):
    def body(gin_ref, gout_ref, rin, rout, send_sems, recv_sems):
        x, y, c, _ = _place()
        sib = (x, y, 1 - c)
        cps = [pltpu.make_async_remote_copy(src_ref=src.at[1 - c], dst_ref=dst, send_sem=send_sems.at[a],
                                            recv_sem=recv_sems.at[a], device_id=sib, device_id_type=MESH)
               for a, (src, dst) in enumerate([(gin_ref, rin), (gout_ref, rout)])]
        for cpy in cps:
            cpy.start()
        for cpy in cps:
            cpy.wait()

    return pl.pallas_call(
        body, name="swap_halves", in_specs=[ANY, ANY], out_specs=[ANY, ANY],
        out_shape=[jax.ShapeDtypeStruct(gin2.shape[1:], gin2.dtype), jax.ShapeDtypeStruct(gout2.shape[1:], gout2.dtype)],
        scratch_shapes=[pltpu.SemaphoreType.DMA((2,)), pltpu.SemaphoreType.DMA((2,))],
    )(gin2, gout2)


def _add_halves(c_idx, g2, r, tr, name):
    rows, cols = r.shape

    def body(c_ref, g_ref, r_ref, o_ref):
        o_ref[...] = (g_ref[0].astype(F32) + r_ref[...].astype(F32)).astype(_XDT)

    return pl.pallas_call(
        body, name=name,
        grid_spec=pltpu.PrefetchScalarGridSpec(
            num_scalar_prefetch=1, grid=(rows // tr,),
            in_specs=[pl.BlockSpec((1, tr, cols), lambda i, c_ref: (c_ref[0], i, 0)),
                      pl.BlockSpec((tr, cols), lambda i, c_ref: (i, 0))],
            out_specs=pl.BlockSpec((tr, cols), lambda i, c_ref: (i, 0))),
        out_shape=jax.ShapeDtypeStruct((rows, cols), _XDT),
        compiler_params=_params(("parallel",)),
    )(c_idx, g2, r)


def _send_to_owners(pin, pout, parts):
    def body(pin_ref, pout_ref, *refs):
        send_sems, recv_sems = refs[-2:]
        cps = []
        for n, part in enumerate(parts):
            cps += _owner_copies(part, pin_ref, pout_ref, refs[2 * n], refs[2 * n + 1], send_sems.at[n], recv_sems.at[n])
        for cpy in cps:
            cpy.start()
        for cpy in cps:
            cpy.wait()

    shapes = [jax.ShapeDtypeStruct((3, _part_rows(t.shape[2], split, part)[1], D), t.dtype)
              for part in parts for t, split in zip((pin, pout), (SPLIT_IN, SPLIT_OUT))]
    return pl.pallas_call(
        body, name="send_to_owners", in_specs=[ANY, ANY], out_specs=[ANY] * len(shapes), out_shape=shapes,
        scratch_shapes=[pltpu.SemaphoreType.DMA((len(parts), 2, 3)), pltpu.SemaphoreType.DMA((len(parts), 2, 3))],
    )(pin, pout)


def _add_chips(chip_idx, p, ra, rb, tr, name):
    rows = p.shape[2]
    na = ra.shape[1] // tr

    def body(c_ref, p_ref, ra_ref, rb_ref, o_ref):
        r = jnp.where(pl.program_id(0) < na, ra_ref[...], rb_ref[...]).astype(F32)
        o_ref[0] = ((p_ref[0, 0].astype(F32) + r[0]) + r[1]) + r[2]

    return pl.pallas_call(
        body, name=name,
        grid_spec=pltpu.PrefetchScalarGridSpec(
            num_scalar_prefetch=1, grid=(rows // tr,),
            in_specs=[pl.BlockSpec((1, 1, tr, D), lambda i, c_ref: (0, c_ref[0], i, 0)),
                      pl.BlockSpec((3, tr, D), lambda i, c_ref: (0, jnp.minimum(i, na - 1), 0)),
                      pl.BlockSpec((3, tr, D), lambda i, c_ref: (0, jnp.maximum(i - na, 0), 0))],
            out_specs=pl.BlockSpec((1, tr, D), lambda i, c_ref: (0, i, 0))),
        out_shape=jax.ShapeDtypeStruct((1, rows, D), F32),
        compiler_params=_params(("parallel",)),
    )(chip_idx, p, ra, rb)


def _exchange_halves(arrs):
    n = len(arrs)

    def body(*refs):
        x, y, c, _ = _place()
        cps = [pltpu.make_async_remote_copy(src_ref=refs[a], dst_ref=refs[n + a], send_sem=refs[2 * n].at[a], recv_sem=refs[2 * n + 1].at[a],
                                            device_id=(x, y, 1 - c), device_id_type=MESH) for a in range(n)]
        for cpy in cps:
            cpy.start()
        for cpy in cps:
            cpy.wait()

    return pl.pallas_call(
        body, name="exchange_halves", in_specs=[ANY] * n, out_specs=[ANY] * n,
        out_shape=[jax.ShapeDtypeStruct(t.shape, t.dtype) for t in arrs],
        scratch_shapes=[pltpu.SemaphoreType.DMA((n,)), pltpu.SemaphoreType.DMA((n,))],
    )(*arrs)


def _adam_halves(c_idx, w, g_own, g_other, m, v, tr, name):
    nl, _, rows, cols = w.shape

    def body(c_ref, w_ref, go_ref, gx_ref, m_ref, v_ref, g_ref, d_ref, mo_ref, vo_ref):
        gv = jnp.where(pl.program_id(1) == c_ref[0], go_ref[0], gx_ref[0])
        mn = ADAM_B1 * m_ref[0, 0] + (1.0 - ADAM_B1) * gv
        vn = ADAM_B2 * v_ref[0, 0] + (1.0 - ADAM_B2) * (gv * gv)
        mh = mn / (1.0 - ADAM_B1 ** ADAM_STEP)
        vh = vn / (1.0 - ADAM_B2 ** ADAM_STEP)
        g_ref[0, 0] = gv
        d_ref[0, 0] = -ADAM_LR * (mh / (jnp.sqrt(vh) + ADAM_EPS) + ADAM_WD * w_ref[0, 0])
        mo_ref[0, 0] = mn
        vo_ref[0, 0] = vn

    full = pl.BlockSpec((1, 1, tr, cols), lambda l, hh, i, c_ref: (l, hh, i, 0))
    own = pl.BlockSpec((1, tr, cols), lambda l, hh, i, c_ref: (l, jnp.where(hh == c_ref[0], i, 0), 0))
    other = pl.BlockSpec((1, tr, cols), lambda l, hh, i, c_ref: (l, jnp.where(hh == c_ref[0], 0, i), 0))
    return pl.pallas_call(
        body, name=name,
        grid_spec=pltpu.PrefetchScalarGridSpec(
            num_scalar_prefetch=1, grid=(nl, 2, rows // tr),
            in_specs=[full, own, other, full, full], out_specs=[full] * 4),
        out_shape=[jax.ShapeDtypeStruct(w.shape, F32)] * 4,
        compiler_params=_params(("parallel", "parallel", "parallel")),
    )(c_idx, w, g_own, g_other, m, v)


def _allreduce_small(sg):
    rows = sg.shape[0]
    vm = pl.BlockSpec(memory_space=pltpu.VMEM)

    def body(sg_ref, tot_ref, all_ref, send_sems, recv_sems):
        x, y, c, _ = _place()
        me = 4 * x + 2 * y + c
        all_ref[me] = sg_ref[...]
        cps = []
        for mask in range(1, 8):
            to = (1 - x if mask & 4 else x, 1 - y if mask & 2 else y, 1 - c if mask & 1 else c)
            cps.append(pltpu.make_async_remote_copy(src_ref=sg_ref, dst_ref=all_ref.at[me], send_sem=send_sems.at[mask - 1],
                                                    recv_sem=recv_sems.at[mask - 1], device_id=to, device_id_type=MESH))
        for cpy in cps:
            cpy.start()
        for cpy in cps:
            cpy.wait()
        acc = all_ref[0]
        for d in range(1, 8):
            acc = acc + all_ref[d]
        tot_ref[...] = acc

    return pl.pallas_call(
        body, name="allreduce_small", in_specs=[vm], out_specs=[vm, vm],
        out_shape=[jax.ShapeDtypeStruct((rows, 128), F32), jax.ShapeDtypeStruct((8, rows, 128), F32)],
        scratch_shapes=[pltpu.SemaphoreType.DMA((7,)), pltpu.SemaphoreType.DMA((7,))],
        compiler_params=_params(),
    )(sg)[0]


_CUTS = [0, 512, 1024, 2048, 3072, 3088, 4112, 5136, 6160, 7184]


def _rows_to_internal(w):
    tail = w.shape[1:]
    gq, gk, gv, gg, ga, aq, ak, av, ag = [w[_CUTS[i]:_CUTS[i + 1]] for i in range(9)]
    g = jnp.concatenate([gq.reshape((GH, GDK) + tail), gk.reshape((GH, GDK) + tail),
                         gv.reshape((GH, GDV) + tail), gg.reshape((GH, GDV) + tail)], axis=1).reshape((ZG,) + tail)
    a = jnp.concatenate([t.reshape((AH, AHD) + tail) for t in (aq, ak, av, ag)], axis=1).reshape((ZA,) + tail)
    pad = [(0, GAP - RANK)] + [(0, 0)] * len(tail)
    return jnp.concatenate([g, a], axis=0), jnp.pad(ga, pad)


def _rows_from_internal(g, a, ga):
    tail = g.shape[1:]
    g = g.reshape((GH, GW) + tail)
    a = a.reshape((AH, AW) + tail)
    parts = [g[:, 0:GDK], g[:, GDK:2 * GDK], g[:, 2 * GDK:2 * GDK + GDV], g[:, 2 * GDK + GDV:GW]]
    parts = [t.reshape((-1,) + tail) for t in parts] + [ga[0:RANK]]
    parts += [a[:, i * AHD:(i + 1) * AHD].reshape((-1,) + tail) for i in range(4)]
    return jnp.concatenate(parts, axis=0)


def _slab_lo(chip):
    return min(SHARD * chip // 16 * 16, DIN - SLAB)


def _pack_rows(parts):
    rows = []
    for t in parts:
        flat = t.reshape(-1)
        rows.append(jnp.pad(flat, (0, (-flat.shape[0]) % 128)).reshape(-1, 128))
    buf = jnp.concatenate(rows, axis=0)
    return jnp.pad(buf, ((0, (-buf.shape[0]) % 8), (0, 0)))


def _unpack_rows(buf, shapes):
    out, r = [], 0
    for shp in shapes:
        n = 1
        for d in shp:
            n *= d
        nr = -(-n // 128)
        out.append(buf[r:r + nr].reshape(-1)[:n].reshape(shp))
        r += nr
    return out


def _layer_fwd(x, wm, wga, wout, gpre, gpost, wa, ba, ggla, gatt, rbx, ride=None):
    z, zga, h, *bufs = _inproj(x, gpre, wm, wga, ride)
    ride = None if ride is None else (ride[0], ride[1], *bufs)
    cg, *bufs = _gla_fwd(z, zga, wa, ba, ggla, ride)
    ride = None if ride is None else (ride[0], ride[1], *bufs)
    ca, oraw, lse, *bufs = _att_fwd(z, rbx, gatt, ride)
    ride = None if ride is None else (ride[0], ride[1], *bufs)
    y, xo, *bufs = _outproj(cg, ca, wout, x, gpost, ride)
    return xo, (x, z, zga, h, cg, ca, oraw, lse, y), bufs


def _layer_bwd(dout, saved, wm, wga, wout, gpre, gpost, wa, ba, ggla, gatt, rbx, send=None, reduce_now=None):
    x, z, zga, h, cg, ca, oraw, lse, y = saved
    dy, dcg, dca, dgpost = _post_bwd(dout, y, gpost, wout)
    dwout = jnp.concatenate([_matmul_tn(cg, dy, _XDT, 512, 1024, "dwout_gla"),
                             _matmul_tn(ca, dy, _XDT, 512, 1024, "dwout_att")], axis=0)
    dza, dgatt, dbx, *got_a = _att_bwd(z, oraw, lse, dca, rbx, gatt, send)
    dzg, dga, dwa, dba, dggla, *got_b = _gla_bwd(z, zga, wa, ba, ggla, dcg, send)
    dwin = (_matmul_tn(dzg, h, _XDT, 512, 1024, "dwin_gla"), _matmul_tn(dza, h, _XDT, 512, 1024, "dwin_att"),
            _matmul_tn(dga, h, _XDT, GAP, 1024, "dwin_gate"))
    own = None if reduce_now is None else reduce_now(dwin, dwout)
    dx, dgpre, *got_own = _dh(dzg, dza, dga, wm, wga, x, dout, gpre, own)
    drb = jnp.concatenate([jnp.zeros((AH, 1), F32), dbx[:, 0, ::-1]], axis=1)
    return dx, dwin, dwout, (dgpre[0], dgpost[0], dwa[0:RANK], dba[0], dggla[0], dgatt[0], drb), got_a + got_b, own, got_own


def _rel_rows(rb):
    return rb[:, :0:-1][:, None, :]


def kernel(x, w_in, w_out, g_pre, g_post, w_alpha, b_alpha, g_gla, g_att, rel_bias, loss_target, m_w_in, m_w_out, m_g_pre, m_g_post, m_w_alpha, m_b_alpha, m_g_gla, m_g_att, m_rel_bias, v_w_in, v_w_out, v_g_pre, v_g_post, v_w_alpha, v_b_alpha, v_g_gla, v_g_att, v_rel_bias):
    nl = w_in.shape[0]
    ax, ay, ac = lax.axis_index("x"), lax.axis_index("y"), lax.axis_index("c")
    chip = 2 * ax + ay
    c_idx = jnp.reshape(ac, (1,)).astype(jnp.int32)
    chip_idx = jnp.reshape(chip, (1,)).astype(jnp.int32)

    phase = [SHARD * i % 16 for i in range(NCHIP)]
    wt_rows = jnp.transpose(w_in, (0, 2, 1)).astype(_CDT)
    at_phase = [functools.partial(jnp.pad, wt_rows, ((0, 0), (p, WSLOT - SHARD - p), (0, 0))) for p in phase]
    wt_src = lax.switch(chip, at_phase).reshape(nl, 2, WSLOT // 2, D)
    wout_src = w_out.astype(_CDT).reshape(nl, 2, D // NCHIP // 2, D)

    def with_own(own):
        start = [chip] + [0] * own.ndim
        return lax.dynamic_update_slice(lax.empty((NCHIP,) + own.shape, own.dtype), own[None], start)

    def gather_operands(l):
        return wt_src[l], wout_src[l], with_own(wt_src[l]), with_own(wout_src[l])

    def layer_weights(bufs):
        wt4 = bufs[0].reshape(NCHIP, WSLOT, D)
        wref = jnp.concatenate([wt4[i, phase[i]:phase[i] + SHARD] for i in range(NCHIP)])
        return _rows_to_internal(wref) + (bufs[1].reshape(D, D),)

    first = gather_operands(0)
    bin0, bout0, wa_all = _gather_first(first[0], first[1], w_alpha, first[2], first[3], with_own(w_alpha))
    wa_full = jnp.transpose(wa_all, (1, 2, 0, 3)).reshape(nl, RANK, GH * GDK)
    wa_pad = jnp.pad(wa_full, ((0, 0), (0, GAP - RANK), (0, 0))).astype(_CDT)
    rbx = [_rel_rows(rel_bias[l]) for l in range(nl)]

    def weights(l):
        return big[l] + (g_pre[l][None], g_post[l][None], wa_pad[l], b_alpha[l][None], g_gla[l][None], g_att[l][None], rbx[l])

    h = x[0]
    saved, big = [], [None] * nl
    big[0] = layer_weights((bin0, bout0))
    for l in range(nl):
        h, sv, bufs = _layer_fwd(h, *weights(l), ride=gather_operands(l + 1) if l + 1 < nl else None)
        saved.append(sv)
        if l + 1 < nl:
            big[l + 1] = layer_weights(bufs)
    dout, loss_part = _loss_grad(h, loss_target[0])

    small, hin, hout = [None] * nl, [None] * nl, [None] * nl
    hw = D // NCHIP // 2

    def reduce_owner(sent, got):
        rin_a, rout_a, rin_b, rout_b = got
        return (_add_chips(chip_idx, sent[0], rin_a, rin_b, 48, "add_chips_in"),
                _add_chips(chip_idx, sent[1], rout_a, rout_b, 128, "add_chips_out"))

    def partial_sums(dwin, dwout):
        gt = _rows_from_internal(*dwin)
        slabs = jnp.stack([gt[_slab_lo(i):_slab_lo(i) + SLAB] for i in range(NCHIP)])
        gin2 = jnp.transpose(slabs.reshape(NCHIP, 2, HSLAB, D), (1, 0, 2, 3)).reshape(2, NCHIP * HSLAB, D)
        gout2 = jnp.transpose(dwout.reshape(NCHIP, 2, hw, D), (1, 0, 2, 3)).reshape(2, NCHIP * hw, D)
        rin, rout = _swap_halves(gin2, gout2)
        return (_add_halves(c_idx, gin2, rin, 192, "add_halves_in").reshape(1, NCHIP, HSLAB, D),
                _add_halves(c_idx, gout2, rout, 256, "add_halves_out").reshape(1, NCHIP, hw, D))

    sent = None
    for l in reversed(range(nl)):
        dout, dwin, dwout, small[l], got, own, got_own = _layer_bwd(dout, saved[l], *weights(l), send=sent,
                                                                    reduce_now=partial_sums if l == 0 else None)
        if sent is not None:
            hin[l + 1], hout[l + 1] = reduce_owner(sent, got)
        sent = partial_sums(dwin, dwout) if l > 0 else own
    grad_x = dout[None]
    hin[0], hout[0] = reduce_owner(sent, list(_send_to_owners(*sent, parts=(0,))) + list(got_own))
    xchg = _exchange_halves(hin + hout)
    hin, xin = jnp.concatenate(hin), jnp.concatenate(xchg[:nl])
    hout, xout = jnp.concatenate(hout), jnp.concatenate(xchg[nl:])

    slab = jnp.concatenate([jnp.where(ac == 0, hin, xin), jnp.where(ac == 0, xin, hin)], axis=1)
    off = sum(jnp.where(chip == i, SHARD * i - _slab_lo(i), 0) for i in range(NCHIP))
    g_rows = jnp.transpose(lax.dynamic_slice_in_dim(slab, off, SHARD, axis=1), (1, 0, 2))
    rows_first = lambda t: jnp.transpose(t, (2, 0, 1))
    d_rows, nm_rows, nv_rows = _adam_rows(rows_first(w_in), g_rows, rows_first(m_w_in), rows_first(v_w_in), 32, "adam_w_in")
    g_w_in, d_w_in, nm_w_in, nv_w_in = [jnp.transpose(t, (1, 2, 0)) for t in (g_rows, d_rows, nm_rows, nv_rows)]

    def adam_big(w, g_own, g_other, m, v, name):
        shp = w.shape
        halves = lambda t: t.reshape(shp[0], 2, shp[1] // 2, shp[2])
        return [t.reshape(shp) for t in _adam_halves(c_idx, halves(w), g_own, g_other, halves(m), halves(v), 256, name)]

    g_w_out, d_w_out, nm_w_out, nv_w_out = adam_big(w_out, hout, xout, m_w_out, v_w_out, "adam_w_out")

    stacked = [jnp.stack([small[l][i] for l in range(nl)]) for i in range(7)] + [loss_part]
    g_small = _unpack_rows(_allreduce_small(_pack_rows(stacked)), [t.shape for t in stacked])
    g_gpre, g_gpost, g_wa_full, g_ba, g_ggla, g_gatt, g_rb, loss_sum = g_small
    loss = loss_sum[0, 0]
    g_wa = lax.dynamic_slice_in_dim(g_wa_full, chip * GDK, GDK, axis=2)
    names = [(g_pre, m_g_pre, v_g_pre, g_gpre), (g_post, m_g_post, v_g_post, g_gpost), (w_alpha, m_w_alpha, v_w_alpha, g_wa),
             (b_alpha, m_b_alpha, v_b_alpha, g_ba), (g_gla, m_g_gla, v_g_gla, g_ggla), (g_att, m_g_att, v_g_att, g_gatt),
             (rel_bias, m_rel_bias, v_rel_bias, g_rb)]
    shapes = [t[0].shape for t in names]
    packed = [_pack_rows([t[i] for t in names]) for i in range(4)]
    d_s, nm_s, nv_s = [_unpack_rows(t, shapes) for t in _adam(packed[0], packed[3], packed[1], packed[2], packed[0].shape[0], "adam_small")]

    grads = [g_w_in, g_w_out, g_gpre, g_gpost, g_wa, g_ba, g_ggla, g_gatt, g_rb]
    deltas = [d_w_in, d_w_out] + d_s
    new_m = [nm_w_in, nm_w_out] + nm_s
    new_v = [nv_w_in, nv_w_out] + nv_s
    return (loss, grad_x, *grads, *deltas, *new_m, *new_v)
```

```python
import functools

import jax
import jax.numpy as jnp
from jax import lax
from jax.experimental import pallas as pl
from jax.experimental.pallas import tpu as pltpu

D = 2048
DEPTH = 4
CHUNK = 64
GH, GDK, GDV = 4, 128, 256
DGLA = GH * GDV
RANK = 16
TAU = 16.0
AH, AHD = 8, 128
DATT = AH * AHD
LEFT = 8
NREL = 257
EPS = 1e-6
DIN = 7184
ADAM_LR, ADAM_B1, ADAM_B2, ADAM_EPS, ADAM_WD, ADAM_STEP = 0.001, 0.9, 0.999, 1e-08, 0.01, 10

GW = 2 * GDK + 2 * GDV
AW = 4 * AHD
ZG = GH * GW
ZA = AH * AW
ZM = ZG + ZA
GAP = 128
QB = 2 * CHUNK
HP = 2
BANDW = (LEFT + 2) * CHUNK
PADK = LEFT * CHUNK
NCHIP = 4
SHARD = DIN // NCHIP
SLAB = 1824
HSLAB = SLAB // 2
WSLOT = 1824
GPARTS_IN = [(0, 368), (368, 192), (560, 352)]
GPARTS_OUT = [(0, 112), (112, 48), (160, 96)]
NDIRECT = len(GPARTS_IN)
SPLIT_IN, SPLIT_OUT = 336, 128
NEG = -1e30
F32 = jnp.float32
_CDT = jnp.bfloat16
_XDT = jnp.bfloat16
_VMEM = 56 * 1024 * 1024
MESH = pl.DeviceIdType.MESH
ANY = pl.BlockSpec(memory_space=pl.ANY)


def _dot(a, b):
    return jnp.dot(a, b, preferred_element_type=F32)


def _dot_nt(a, b):
    return lax.dot_general(a, b, (((1,), (1,)), ((), ())), preferred_element_type=F32)


def _dot_tn(a, b):
    return lax.dot_general(a, b, (((0,), (0,)), ((), ())), preferred_element_type=F32)


def _rms_rows(v):
    return lax.rsqrt(jnp.mean(v * v, axis=-1, keepdims=True) + EPS)


def _sigmoid(v):
    return 1.0 / (1.0 + jnp.exp(-v))


def _log_sigmoid(v):
    return jnp.minimum(v, 0.0) - jnp.log(1.0 + jnp.exp(-jnp.abs(v)))


def _exact_dot(tri, v):
    hi = v.astype(_CDT)
    r1 = v - hi.astype(F32)
    mid = r1.astype(_CDT)
    lo = (r1 - mid.astype(F32)).astype(_CDT)
    return _dot(tri, hi) + _dot(tri, mid) + _dot(tri, lo)


def _tri(strict):
    row = lax.broadcasted_iota(jnp.int32, (CHUNK, CHUNK), 0)
    col = lax.broadcasted_iota(jnp.int32, (CHUNK, CHUNK), 1)
    return jnp.where((col < row) if strict else (col <= row), 1.0, 0.0).astype(_CDT)


def _norm_gate_bwd(o, g, gate, dcat):
    r = _rms_rows(o)
    oh = o * r
    sg = _sigmoid(gate)
    dn = dcat * (gate * sg)
    dgate = dcat * (oh * g) * (sg * (1.0 + gate * (1.0 - sg)))
    dg = jnp.sum(dn * oh, axis=0, keepdims=True)
    dnn = dn * g
    do = r * (dnn - oh * jnp.mean(dnn * oh, axis=-1, keepdims=True))
    return do, dgate, dg


def _params(sem=None, vmem=_VMEM):
    return pltpu.CompilerParams(dimension_semantics=sem, vmem_limit_bytes=vmem)


def _inproj(x, g, wm, wga, ride=None, tm=512, tn=1024):
    s = x.shape[0]
    gr = _gather_ride(ride, 0)

    def body(*refs):
        (x_ref, g_ref, wm_ref, wga_ref, z_ref, zga_ref, h_ref, hs), rr = gr.split(refs, 4, 3)
        i, j = pl.program_id(0), pl.program_id(1)
        gr.start(rr, (i == 0) & (j == 0))

        @pl.when(pl.program_id(1) == 0)
        def _():
            xv = x_ref[...]
            hv = (xv * _rms_rows(xv) * g_ref[...]).astype(_CDT)
            hs[...] = hv
            h_ref[...] = hv
            zga_ref[...] = _dot_nt(hv, wga_ref[...]).astype(_CDT)

        z_ref[...] = _dot_nt(hs[...], wm_ref[...]).astype(_CDT)
        gr.wait(rr, (i == s // tm - 1) & (j == ZM // tn - 1))

    return pl.pallas_call(
        body, name="inproj", grid=(s // tm, ZM // tn),
        in_specs=[pl.BlockSpec((tm, D), lambda i, j: (i, 0)), pl.BlockSpec((1, D), lambda i, j: (0, 0)),
                  pl.BlockSpec((tn, D), lambda i, j: (j, 0)), pl.BlockSpec((GAP, D), lambda i, j: (0, 0))] + gr.in_specs,
        out_specs=[pl.BlockSpec((tm, tn), lambda i, j: (i, j)), pl.BlockSpec((tm, GAP), lambda i, j: (i, 0)),
                   pl.BlockSpec((tm, D), lambda i, j: (i, 0))] + gr.out_specs,
        out_shape=[jax.ShapeDtypeStruct((s, ZM), _CDT), jax.ShapeDtypeStruct((s, GAP), _CDT),
                   jax.ShapeDtypeStruct((s, D), _CDT)] + gr.out_shape,
        scratch_shapes=[pltpu.VMEM((tm, D), _CDT)] + gr.scratch, input_output_aliases=gr.alias(4, 3),
        compiler_params=_params(("arbitrary", "arbitrary")),
    )(x, g, wm, wga, *gr.operands)


def _gla_fwd(z, zga, wa, ba, ggla, ride=None):
    s = z.shape[0]
    nc = s // CHUNK
    gr = _gather_ride(ride, 1)

    def body(*refs):
        (zg_ref, zga_ref, wa_ref, ba_ref, g_ref, cat_ref, la_s, st), rr = gr.split(refs, 5, 1)
        gr.start(rr, pl.program_id(0) == 0)
        la_s[...] = _log_sigmoid(_dot(zga_ref[...], wa_ref[...]) + ba_ref[...]) * (1.0 / TAU)
        st[...] = jnp.zeros_like(st)
        tri = _tri(False)

        def step(n, carry):
            rows = pl.ds(pl.multiple_of(n * CHUNK, CHUNK), CHUNK)
            for p in range(HP):
                z0 = p * GW
                la = la_s[rows, p * GDK:(p + 1) * GDK]
                lc = _exact_dot(tri, la)
                lend = jnp.sum(la, axis=0, keepdims=True)
                kdec = (zg_ref[rows, z0 + GDK:z0 + 2 * GDK].astype(F32) * jnp.exp(lend - lc)).astype(_CDT)
                stn = jnp.exp(lend) * st[p] + _dot_tn(zg_ref[rows, z0 + 2 * GDK:z0 + 2 * GDK + GDV], kdec)
                st[p] = stn
                qs = (zg_ref[rows, z0:z0 + GDK].astype(F32) * (GDK ** -0.5)).astype(_CDT)
                o = _dot_nt(qs, stn.astype(_CDT))
                gate = zg_ref[rows, z0 + 2 * GDK + GDV:z0 + GW].astype(F32)
                gain = g_ref[:, p * GDV:(p + 1) * GDV]
                cat_ref[rows, p * GDV:(p + 1) * GDV] = (o * _rms_rows(o) * gain * (gate * _sigmoid(gate))).astype(_CDT)
            return carry

        lax.fori_loop(0, nc, step, 0, unroll=4)
        gr.wait(rr, pl.program_id(0) == GH // HP - 1)

    return pl.pallas_call(
        body, name="gla_fwd", grid=(GH // HP,),
        in_specs=[pl.BlockSpec((s, HP * GW), lambda h: (0, h)), pl.BlockSpec((s, GAP), lambda h: (0, 0)),
                  pl.BlockSpec((GAP, HP * GDK), lambda h: (0, h)), pl.BlockSpec((1, HP * GDK), lambda h: (0, h)),
                  pl.BlockSpec((1, HP * GDV), lambda h: (0, h))] + gr.in_specs,
        out_specs=[pl.BlockSpec((s, HP * GDV), lambda h: (0, h))] + gr.out_specs,
        out_shape=[jax.ShapeDtypeStruct((s, DGLA), _CDT)] + gr.out_shape,
        scratch_shapes=[pltpu.VMEM((s, HP * GDK), F32), pltpu.VMEM((HP, GDV, GDK), F32)] + gr.scratch,
        input_output_aliases=gr.alias(5, 1), compiler_params=_params(("arbitrary",)),
    )(z, zga, wa, ba, ggla, *gr.operands)


def _band_bias(b0):
    row = lax.broadcasted_iota(jnp.int32, (QB, 256), 0)
    col = lax.broadcasted_iota(jnp.int32, (QB, 256), 1)
    lane = lax.broadcasted_iota(jnp.int32, (1, 256), 1)
    c0 = jnp.sum(jnp.where(lane == 0, b0, 0.0), axis=1, keepdims=True)
    xv = jnp.broadcast_to(b0, (QB, 256))
    for bit in range(7):
        xv = jnp.where(((row >> bit) & 1) == 1, pltpu.roll(xv, 1 << bit, 1), xv)
    xv = jnp.where(col < row, c0, xv)
    return jnp.concatenate([jnp.broadcast_to(c0, (QB, BANDW - 256)), xv], axis=1)


def _band_static_mask():
    row = lax.broadcasted_iota(jnp.int32, (QB, BANDW), 0) >> 6
    col = lax.broadcasted_iota(jnp.int32, (QB, BANDW), 1) >> 6
    return (col >= row) & (col <= row + LEFT)


def _fold_bias_grad(t):
    row = lax.broadcasted_iota(jnp.int32, (QB, 256), 0)
    col = lax.broadcasted_iota(jnp.int32, (QB, 256), 1)
    xv = t[:, BANDW - 256:]
    low = col < row
    far = jnp.sum(t[:, 0:BANDW - 256], axis=1, keepdims=True) + jnp.sum(jnp.where(low, xv, 0.0), axis=1, keepdims=True)
    far = jnp.sum(far, axis=0, keepdims=True)
    xv = jnp.where(low, 0.0, xv)
    for bit in range(7):
        xv = jnp.where(((row >> bit) & 1) == 1, pltpu.roll(xv, 256 - (1 << bit), 1), xv)
    dp = jnp.sum(xv, axis=0, keepdims=True)
    lane = lax.broadcasted_iota(jnp.int32, (1, 256), 1)
    return dp + jnp.where(lane == 0, far, 0.0)


def _att_fwd(z, rbx, gatt, ride=None):
    s = z.shape[0]
    nb = s // QB
    gr = _gather_ride(ride, 2)

    def body(*refs):
        (za_ref, rb_ref, g_ref, cat_ref, o_ref, lse_ref, kp, vp, bias_s), rr = gr.split(refs, 3, 3)
        gr.start(rr, pl.program_id(0) == 0)
        for p in range(HP):
            z0 = p * AW
            kp[p, 0:PADK, :] = jnp.zeros((PADK, AHD), _CDT)
            vp[p, 0:PADK, :] = jnp.zeros((PADK, AHD), _CDT)
            kp[p, PADK:, :] = za_ref[:, z0 + AHD:z0 + 2 * AHD]
            vp[p, PADK:, :] = za_ref[:, z0 + 2 * AHD:z0 + 3 * AHD]
            bias_s[p] = jnp.where(_band_static_mask(), _band_bias(rb_ref[p]), NEG)

        def step(b, carry):
            r0 = pl.multiple_of(b * QB, QB)
            rows = pl.ds(r0, QB)
            band = pl.ds(r0, BANDW)
            live = lax.broadcasted_iota(jnp.int32, (QB, BANDW), 1) >= PADK - r0
            for p in range(HP):
                z0 = p * AW
                cols = slice(p * AHD, (p + 1) * AHD)
                sc = _dot_nt(za_ref[rows, z0:z0 + AHD], kp[p, band, :]) * (AHD ** -0.5) + bias_s[p]
                sc = jnp.where(live, sc, NEG)
                m = jnp.max(sc, axis=-1, keepdims=True)
                pr = jnp.exp(sc - m)
                l = jnp.sum(pr, axis=-1, keepdims=True)
                o = _dot((pr * (1.0 / l)).astype(_CDT), vp[p, band, :])
                o_ref[rows, cols] = o.astype(_CDT)
                lse_ref[rows, cols] = jnp.broadcast_to(m + jnp.log(l), (QB, AHD))
                gate = za_ref[rows, z0 + 3 * AHD:z0 + AW].astype(F32)
                cat_ref[rows, cols] = (o * _rms_rows(o) * g_ref[:, cols] * (gate * _sigmoid(gate))).astype(_CDT)
            return carry

        lax.fori_loop(0, nb, step, 0, unroll=4)
        gr.wait(rr, pl.program_id(0) == AH // HP - 1)

    return pl.pallas_call(
        body, name="att_fwd", grid=(AH // HP,),
        in_specs=[pl.BlockSpec((s, HP * AW), lambda h: (0, ZG // (HP * AW) + h)), pl.BlockSpec((HP, 1, 256), lambda h: (h, 0, 0)),
                  pl.BlockSpec((1, HP * AHD), lambda h: (0, h))] + gr.in_specs,
        out_specs=[pl.BlockSpec((s, HP * AHD), lambda h: (0, h)), pl.BlockSpec((s, HP * AHD), lambda h: (0, h)),
                   pl.BlockSpec((s, HP * AHD), lambda h: (0, h))] + gr.out_specs,
        out_shape=[jax.ShapeDtypeStruct((s, DATT), _CDT), jax.ShapeDtypeStruct((s, DATT), _CDT),
                   jax.ShapeDtypeStruct((s, DATT), F32)] + gr.out_shape,
        scratch_shapes=[pltpu.VMEM((HP, s + PADK, AHD), _CDT), pltpu.VMEM((HP, s + PADK, AHD), _CDT),
                        pltpu.VMEM((HP, QB, BANDW), F32)] + gr.scratch, input_output_aliases=gr.alias(3, 3),
        compiler_params=_params(("arbitrary",)),
    )(z, rbx, gatt, *gr.operands)


def _outproj(cg, ca, wout, x, gpost, ride=None, tm=256):
    s = x.shape[0]
    gr = _gather_ride(ride, NDIRECT)

    def body(*refs):
        (cg_ref, ca_ref, w_ref, x_ref, g_ref, y_ref, xo_ref), rr = gr.split(refs, 5, 2)
        gr.start(rr, pl.program_id(0) == 0)
        y = _dot(cg_ref[...], w_ref[0:DGLA, :]) + _dot(ca_ref[...], w_ref[DGLA:, :])
        y_ref[...] = y
        xo_ref[...] = x_ref[...] + y * _rms_rows(y) * g_ref[...]
        gr.wait(rr, pl.program_id(0) == s // tm - 1)

    return pl.pallas_call(
        body, name="outproj", grid=(s // tm,),
        in_specs=[pl.BlockSpec((tm, DGLA), lambda i: (i, 0)), pl.BlockSpec((tm, DATT), lambda i: (i, 0)),
                  pl.BlockSpec((D, D), lambda i: (0, 0)), pl.BlockSpec((tm, D), lambda i: (i, 0)),
                  pl.BlockSpec((1, D), lambda i: (0, 0))] + gr.in_specs,
        out_specs=[pl.BlockSpec((tm, D), lambda i: (i, 0)), pl.BlockSpec((tm, D), lambda i: (i, 0))] + gr.out_specs,
        out_shape=[jax.ShapeDtypeStruct((s, D), F32), jax.ShapeDtypeStruct((s, D), F32)] + gr.out_shape,
        scratch_shapes=gr.scratch, input_output_aliases=gr.alias(5, 2),
        compiler_params=_params(("arbitrary",)),
    )(cg, ca, wout, x, gpost, *gr.operands)


def _loss_grad(xo, tgt, tm=256):
    s = xo.shape[0]

    def body(xo_ref, t_ref, d_ref, l_ref):
        @pl.when(pl.program_id(0) == 0)
        def _():
            l_ref[...] = jnp.zeros_like(l_ref)

        e = xo_ref[...] - t_ref[...]
        d_ref[...] = e * (1.0 / D)
        l_ref[...] += jnp.sum(jnp.sum(e * e, axis=1, keepdims=True), axis=0, keepdims=True) * (0.5 / D)

    return pl.pallas_call(
        body, name="loss_grad", grid=(s // tm,),
        in_specs=[pl.BlockSpec((tm, D), lambda i: (i, 0)), pl.BlockSpec((tm, D), lambda i: (i, 0))],
        out_specs=[pl.BlockSpec((tm, D), lambda i: (i, 0)), pl.BlockSpec((1, 1), lambda i: (0, 0))],
        out_shape=[jax.ShapeDtypeStruct((s, D), F32), jax.ShapeDtypeStruct((1, 1), F32)],
        compiler_params=_params(("arbitrary",)),
    )(xo, tgt)


def _place():
    x, y, c = lax.axis_index("x"), lax.axis_index("y"), lax.axis_index("c")
    chips = [(1 - x, y), (x, 1 - y), (1 - x, 1 - y)]
    return x, y, c, chips


def _variants(fn):
    x, y, c, _ = _place()
    for jx in range(2):
        for jy in range(2):
            for jc in range(2):
                pl.when((x == jx) & (y == jy) & (c == jc))(functools.partial(fn, jx, jy, jc))


def _gather_copies(part, x, y, c, src_in, src_out, buf_in, buf_out, send_sems, recv_sems):
    me = 2 * x + y
    peers = [(1 - x, y), (x, 1 - y), (1 - x, 1 - y)]
    cps = []
    for k, (px, py) in enumerate(peers):
        for a, (src, buf, parts) in enumerate([(src_in, buf_in, GPARTS_IN), (src_out, buf_out, GPARTS_OUT)]):
            if part < NDIRECT:
                rows = pl.ds(*parts[part])
                pair = (src.at[c, rows], buf.at[me, c, rows], (px, py, c))
            else:
                piece = buf.at[2 * px + py, c]
                pair = (piece, piece, (x, y, 1 - c))
            cps.append(pltpu.make_async_remote_copy(src_ref=pair[0], dst_ref=pair[1], send_sem=send_sems.at[a, k],
                                                    recv_sem=recv_sems.at[a, k], device_id=pair[2], device_id_type=MESH))
    return cps


class _gather_ride:
    def __init__(self, ride, part):
        self.part, self.on = part, ride is not None
        self.in_specs, self.out_specs, self.out_shape, self.scratch, self.operands, self.aliases = [], [], [], [], [], {}
        if self.on:
            self.operands = list(ride)
            self.in_specs, self.out_specs = [ANY] * 4, [ANY] * 2
            self.out_shape = [jax.ShapeDtypeStruct(t.shape, t.dtype) for t in ride[2:]]
            self.scratch = [pltpu.SemaphoreType.DMA((2, 3)), pltpu.SemaphoreType.DMA((2, 3))]

    def alias(self, n_in, n_out):
        return {n_in + 2: n_out, n_in + 3: n_out + 1} if self.on else {}

    def split(self, refs, n_in, n_out):
        if not self.on:
            return refs, None
        own = refs[:n_in] + refs[n_in + 4:n_in + 4 + n_out] + refs[n_in + 6 + n_out:-2]
        return own, refs[n_in:n_in + 2] + refs[n_in + 4 + n_out:n_in + 6 + n_out] + refs[-2:]

    def start(self, ride_refs, first):
        if self.on:
            def go(x, y, c):
                for cpy in _gather_copies(self.part, x, y, c, *ride_refs):
                    cpy.start()
            pl.when(first)(lambda: _variants(go))

    def wait(self, ride_refs, last):
        if self.on:
            def done(x, y, c):
                for cpy in _gather_copies(self.part, x, y, c, *ride_refs):
                    cpy.wait()
            pl.when(last)(lambda: _variants(done))


def _part_rows(total, split, part):
    return (0, split) if part == 0 else (split, total - split)


def _owner_copies(part, pin_ref, pout_ref, rin, rout, send_sems, recv_sems):
    x, y, c, chips = _place()
    cps = []
    for k, (px, py) in enumerate(chips):
        for a, (src, dst, split) in enumerate([(pin_ref, rin, SPLIT_IN), (pout_ref, rout, SPLIT_OUT)]):
            r0, n = _part_rows(src.shape[2], split, part)
            cps.append(pltpu.make_async_remote_copy(src_ref=src.at[0, 2 * px + py, pl.ds(r0, n)], dst_ref=dst.at[k],
                                                    send_sem=send_sems.at[a, k], recv_sem=recv_sems.at[a, k],
                                                    device_id=(px, py, c), device_id_type=MESH))
    return cps


class _ride_specs:
    def __init__(self, send, part):
        self.in_specs, self.out_specs, self.out_shape, self.scratch, self.operands = [], [], [], [], []
        if send is not None:
            self.in_specs, self.out_specs, self.operands = [ANY, ANY], [ANY, ANY], list(send)
            self.out_shape = [jax.ShapeDtypeStruct((3, _part_rows(t.shape[2], split, part)[1], D), t.dtype)
                              for t, split in zip(send, (SPLIT_IN, SPLIT_OUT))]
            self.scratch = [pltpu.SemaphoreType.DMA((2, 3)), pltpu.SemaphoreType.DMA((2, 3))]


def _ride_refs(refs, send, n_out):
    if send is None:
        return None, refs
    pin_ref, pout_ref = refs[:2]
    own_out = refs[2:2 + n_out]
    rin, rout = refs[2 + n_out:4 + n_out]
    return (pin_ref, pout_ref, rin, rout, refs[-2], refs[-1]), tuple(own_out) + tuple(refs[4 + n_out:-2])


def _ride_start(refs, send, n_out, part, first=None):
    ride = _ride_refs(refs, send, n_out)[0]
    if ride is None:
        return []
    cps = _owner_copies(part, *ride)

    @pl.when(pl.program_id(0) == 0 if first is None else first)
    def _():
        for cpy in cps:
            cpy.start()

    return cps


def _ride_wait(cps, steps, last=None):
    if cps:
        @pl.when(pl.program_id(0) == steps - 1 if last is None else last)
        def _():
            for cpy in cps:
                cpy.wait()


def _post_bwd(dout, y, gpost, wout, swap=None, tm=256):
    s = y.shape[0]

    def body(*refs):
        d_ref, y_ref, g_ref, w_ref = refs[:4]
        dy_ref, dcg_ref, dca_ref, dg_ref = refs[6:10] if swap is not None else refs[4:8]
        cps = []
        if swap is not None:
            x, yy, c, _ = _place()
            cps = [pltpu.make_async_remote_copy(src_ref=refs[4 + a].at[1 - c], dst_ref=refs[10 + a], send_sem=refs[12].at[a],
                                                recv_sem=refs[13].at[a], device_id=(x, yy, 1 - c), device_id_type=MESH) for a in range(2)]

            @pl.when(pl.program_id(0) == 0)
            def _():
                for cpy in cps:
                    cpy.start()

        @pl.when(pl.program_id(0) == 0)
        def _():
            dg_ref[...] = jnp.zeros_like(dg_ref)

        yv = y_ref[...]
        r = _rms_rows(yv)
        yh = yv * r
        dv = d_ref[...]
        dg_ref[...] += jnp.sum(dv * yh, axis=0, keepdims=True)
        dn = dv * g_ref[...]
        dyb = (r * (dn - yh * jnp.mean(dn * yh, axis=-1, keepdims=True))).astype(_CDT)
        dy_ref[...] = dyb
        dcg_ref[...] = _dot_nt(dyb, w_ref[0:DGLA, :]).astype(_CDT)
        dca_ref[...] = _dot_nt(dyb, w_ref[DGLA:, :]).astype(_CDT)
        _ride_wait(cps, s // tm)

    extra = [] if swap is None else list(swap)
    return pl.pallas_call(
        body, name="post_bwd", grid=(s // tm,),
        in_specs=[pl.BlockSpec((tm, D), lambda i: (i, 0)), pl.BlockSpec((tm, D), lambda i: (i, 0)),
                  pl.BlockSpec((1, D), lambda i: (0, 0)), pl.BlockSpec((D, D), lambda i: (0, 0))] + [ANY] * len(extra),
        out_specs=[pl.BlockSpec((tm, D), lambda i: (i, 0)), pl.BlockSpec((tm, DGLA), lambda i: (i, 0)),
                   pl.BlockSpec((tm, DATT), lambda i: (i, 0)), pl.BlockSpec((1, D), lambda i: (0, 0))] + [ANY] * len(extra),
        out_shape=[jax.ShapeDtypeStruct((s, D), _CDT), jax.ShapeDtypeStruct((s, DGLA), _CDT),
                   jax.ShapeDtypeStruct((s, DATT), _CDT), jax.ShapeDtypeStruct((1, D), F32)]
        + [jax.ShapeDtypeStruct(t.shape[1:], t.dtype) for t in extra],
        scratch_shapes=[pltpu.SemaphoreType.DMA((2,)), pltpu.SemaphoreType.DMA((2,))] if extra else [],
        compiler_params=_params(("arbitrary",)),
    )(dout, y, gpost, wout, *extra)


def _matmul_tn(a, b, out_dtype, tm, tn, name):
    k, m = a.shape
    n = b.shape[1]

    def body(a_ref, b_ref, o_ref):
        o_ref[...] = _dot_tn(a_ref[...], b_ref[...]).astype(out_dtype)

    return pl.pallas_call(
        body, name=name, grid=(m // tm, n // tn),
        in_specs=[pl.BlockSpec((k, tm), lambda i, j: (0, i)), pl.BlockSpec((k, tn), lambda i, j: (0, j))],
        out_specs=pl.BlockSpec((tm, tn), lambda i, j: (i, j)),
        out_shape=jax.ShapeDtypeStruct((m, n), out_dtype),
        compiler_params=_params(("parallel", "parallel")),
    )(a, b)


def _att_bwd(z, oraw, lse, dca, rbx, gatt, send=None):
    s = z.shape[0]
    nb = s // QB

    def body(*refs):
        za_ref, o_ref, lse_ref, dc_ref, rb_ref, g_ref = refs[:6]
        dz_ref, dg_ref, db_ref, kp, vp, dkp, dvp, bias_s, t_s, dg_s = _ride_refs(refs[6:], send, 3)[1]
        cps = _ride_start(refs[6:], send, 3, 0)
        for p in range(HP):
            z0 = p * AW
            kp[p, 0:PADK, :] = jnp.zeros((PADK, AHD), _CDT)
            vp[p, 0:PADK, :] = jnp.zeros((PADK, AHD), _CDT)
            kp[p, PADK:, :] = za_ref[:, z0 + AHD:z0 + 2 * AHD]
            vp[p, PADK:, :] = za_ref[:, z0 + 2 * AHD:z0 + 3 * AHD]
            bias_s[p] = jnp.where(_band_static_mask(), _band_bias(rb_ref[p]), NEG)
        dkp[...] = jnp.zeros_like(dkp)
        dvp[...] = jnp.zeros_like(dvp)
        t_s[...] = jnp.zeros_like(t_s)
        dg_s[...] = jnp.zeros_like(dg_s)

        def step(b, carry):
            r0 = pl.multiple_of(b * QB, QB)
            rows = pl.ds(r0, QB)
            band = pl.ds(r0, BANDW)
            live = lax.broadcasted_iota(jnp.int32, (QB, BANDW), 1) >= PADK - r0
            for p in range(HP):
                z0 = p * AW
                cols = slice(p * AHD, (p + 1) * AHD)
                o = o_ref[rows, cols].astype(F32)
                do, dgate, dg = _norm_gate_bwd(o, g_ref[:, cols], za_ref[rows, z0 + 3 * AHD:z0 + AW].astype(F32),
                                               dc_ref[rows, cols].astype(F32))
                dg_s[:, cols] += dg
                q = za_ref[rows, z0:z0 + AHD]
                kb = kp[p, band, :]
                sc = _dot_nt(q, kb) * (AHD ** -0.5) + bias_s[p]
                sc = jnp.where(live, sc, NEG)
                pr = jnp.exp(sc - jnp.max(lse_ref[rows, cols], axis=-1, keepdims=True))
                dob = do.astype(_CDT)
                dp = _dot_nt(dob, vp[p, band, :])
                ds = pr * (dp - jnp.sum(do * o, axis=-1, keepdims=True))
                t_s[p] += ds
                dsb = (ds * (AHD ** -0.5)).astype(_CDT)
                dz_ref[rows, z0:z0 + AHD] = _dot(dsb, kb).astype(_CDT)
                dz_ref[rows, z0 + 3 * AHD:z0 + AW] = dgate.astype(_CDT)
                dkp[p, band, :] += _dot_tn(dsb, q)
                dvp[p, band, :] += _dot_tn(pr.astype(_CDT), dob)
            return carry

        lax.fori_loop(0, nb, step, 0, unroll=2)
        for p in range(HP):
            z0 = p * AW
            dz_ref[:, z0 + AHD:z0 + 2 * AHD] = dkp[p, PADK:, :].astype(_CDT)
            dz_ref[:, z0 + 2 * AHD:z0 + 3 * AHD] = dvp[p, PADK:, :].astype(_CDT)
            db_ref[p] = _fold_bias_grad(t_s[p])
        dg_ref[...] = dg_s[...]
        _ride_wait(cps, AH // HP)

    ride = _ride_specs(send, 0)
    return pl.pallas_call(
        body, name="att_bwd", grid=(AH // HP,),
        in_specs=[pl.BlockSpec((s, HP * AW), lambda h: (0, ZG // (HP * AW) + h)), pl.BlockSpec((s, HP * AHD), lambda h: (0, h)),
                  pl.BlockSpec((s, HP * AHD), lambda h: (0, h)), pl.BlockSpec((s, HP * AHD), lambda h: (0, h)),
                  pl.BlockSpec((HP, 1, 256), lambda h: (h, 0, 0)), pl.BlockSpec((1, HP * AHD), lambda h: (0, h))] + ride.in_specs,
        out_specs=[pl.BlockSpec((s, HP * AW), lambda h: (0, h)), pl.BlockSpec((1, HP * AHD), lambda h: (0, h)),
                   pl.BlockSpec((HP, 1, 256), lambda h: (h, 0, 0))] + ride.out_specs,
        out_shape=[jax.ShapeDtypeStruct((s, ZA), _CDT), jax.ShapeDtypeStruct((1, DATT), F32),
                   jax.ShapeDtypeStruct((AH, 1, 256), F32)] + ride.out_shape,
        scratch_shapes=[pltpu.VMEM((HP, s + PADK, AHD), _CDT), pltpu.VMEM((HP, s + PADK, AHD), _CDT),
                        pltpu.VMEM((HP, s + PADK, AHD), F32), pltpu.VMEM((HP, s + PADK, AHD), F32),
                        pltpu.VMEM((HP, QB, BANDW), F32), pltpu.VMEM((HP, QB, BANDW), F32), pltpu.VMEM((1, HP * AHD), F32)] + ride.scratch,
        compiler_params=_params(("arbitrary",)),
    )(z, oraw, lse, dca, rbx, gatt, *ride.operands)


def _gla_bwd(z, zga, wa, ba, ggla, dcg, send=None):
    s = z.shape[0]
    nc = s // CHUNK

    def body(*refs):
        zg_ref, zga_ref, wa_ref, ba_ref, g_ref, dc_ref = refs[:6]
        (dz_ref, dga_ref, dwa_ref, dba_ref, dg_ref,
         la_s, om_s, sall, dpre_s, c_s, dga_s, dg_s, dec_s, a_s) = _ride_refs(refs[6:], send, 5)[1]
        cps = _ride_start(refs[6:], send, 5, 1)
        h = pl.program_id(0)
        pre = _dot(zga_ref[...], wa_ref[...]) + ba_ref[...]
        la_s[...] = _log_sigmoid(pre) * (1.0 / TAU)
        om_s[...] = (1.0 - _sigmoid(pre)) * (1.0 / TAU)
        c_s[...] = jnp.zeros_like(c_s)
        dg_s[...] = jnp.zeros_like(dg_s)
        tri = _tri(False)
        tri_strict = _tri(True)

        def decay(rows, p):
            la = la_s[rows, p * GDK:(p + 1) * GDK]
            lend = jnp.sum(la, axis=0, keepdims=True)
            return jnp.exp(lend - _exact_dot(tri, la)), jnp.exp(lend)

        def fwd(n, sts):
            rows = pl.ds(pl.multiple_of(n * CHUNK, CHUNK), CHUNK)
            out = []
            for p in range(HP):
                z0 = p * GW
                dec, a = decay(rows, p)
                dec_s[rows, p * GDK:(p + 1) * GDK] = dec
                a_s[n, :, p * GDK:(p + 1) * GDK] = jnp.broadcast_to(a, (8, GDK))
                kdec = (zg_ref[rows, z0 + GDK:z0 + 2 * GDK].astype(F32) * dec).astype(_CDT)
                stn = a * sts[p] + _dot_tn(zg_ref[rows, z0 + 2 * GDK:z0 + 2 * GDK + GDV], kdec)
                sall[p, n] = stn
                out.append(stn)
            return tuple(out)

        lax.fori_loop(0, nc, fwd, tuple(jnp.zeros((GDV, GDK), F32) for _ in range(HP)), unroll=4)

        def bwd(i, carry):
            n = nc - 1 - i
            rows = pl.ds(pl.multiple_of(n * CHUNK, CHUNK), CHUNK)
            for p in range(HP):
                z0 = p * GW
                kc = slice(p * GDK, (p + 1) * GDK)
                vc = slice(p * GDV, (p + 1) * GDV)
                dec, a = dec_s[rows, kc], a_s[n, 0:1, kc]
                kdec = zg_ref[rows, z0 + GDK:z0 + 2 * GDK].astype(F32) * dec
                kdb = kdec.astype(_CDT)
                v = zg_ref[rows, z0 + 2 * GDK:z0 + 2 * GDK + GDV]
                qs = (zg_ref[rows, z0:z0 + GDK].astype(F32) * (GDK ** -0.5)).astype(_CDT)
                stb = sall[p, n].astype(_CDT)
                st_prev = sall[p, jnp.maximum(n - 1, 0)] * jnp.where(n > 0, 1.0, 0.0)
                o = _dot_nt(qs, stb)
                do, dgate, dg = _norm_gate_bwd(o, g_ref[:, vc], zg_ref[rows, z0 + 2 * GDK + GDV:z0 + GW].astype(F32),
                                               dc_ref[rows, vc].astype(F32))
                dg_s[:, vc] += dg
                dob = do.astype(_CDT)
                gt = _dot_tn(dob, qs) + c_s[p]
                gtb = gt.astype(_CDT)
                da = jnp.sum(gt * st_prev, axis=0, keepdims=True)
                dkdec = _dot(v, gtb)
                dla = _exact_dot(tri_strict, dkdec * kdec) + da * a
                dpre_s[rows, kc] = dla * om_s[rows, kc]
                dz_ref[rows, z0:z0 + GDK] = (_dot(dob, stb) * (GDK ** -0.5)).astype(_CDT)
                dz_ref[rows, z0 + GDK:z0 + 2 * GDK] = (dkdec * dec).astype(_CDT)
                dz_ref[rows, z0 + 2 * GDK:z0 + 2 * GDK + GDV] = _dot_nt(kdb, gtb).astype(_CDT)
                dz_ref[rows, z0 + 2 * GDK + GDV:z0 + GW] = dgate.astype(_CDT)
                c_s[p] = a * gt
            return carry

        lax.fori_loop(0, nc, bwd, 0)
        dpre = dpre_s[...]
        dpb = dpre.astype(_CDT)
        dg_ref[...] = dg_s[...]
        dba_ref[...] = jnp.sum(dpre, axis=0, keepdims=True)
        dwa_ref[...] = _dot_tn(zga_ref[...], dpb)
        part = _dot_nt(dpb, wa_ref[...])

        @pl.when(h == 0)
        def _():
            dga_s[...] = part

        @pl.when(h > 0)
        def _():
            dga_s[...] += part

        @pl.when(h == GH // HP - 1)
        def _():
            dga_ref[...] = dga_s[...].astype(_CDT)

        _ride_wait(cps, GH // HP)

    ride = _ride_specs(send, 1)
    return pl.pallas_call(
        body, name="gla_bwd", grid=(GH // HP,),
        in_specs=[pl.BlockSpec((s, HP * GW), lambda h: (0, h)), pl.BlockSpec((s, GAP), lambda h: (0, 0)),
                  pl.BlockSpec((GAP, HP * GDK), lambda h: (0, h)), pl.BlockSpec((1, HP * GDK), lambda h: (0, h)),
                  pl.BlockSpec((1, HP * GDV), lambda h: (0, h)), pl.BlockSpec((s, HP * GDV), lambda h: (0, h))] + ride.in_specs,
        out_specs=[pl.BlockSpec((s, HP * GW), lambda h: (0, h)), pl.BlockSpec((s, GAP), lambda h: (0, 0)),
                   pl.BlockSpec((GAP, HP * GDK), lambda h: (0, h)), pl.BlockSpec((1, HP * GDK), lambda h: (0, h)),
                   pl.BlockSpec((1, HP * GDV), lambda h: (0, h))] + ride.out_specs,
        out_shape=[jax.ShapeDtypeStruct((s, ZG), _CDT), jax.ShapeDtypeStruct((s, GAP), _CDT),
                   jax.ShapeDtypeStruct((GAP, GH * GDK), F32), jax.ShapeDtypeStruct((1, GH * GDK), F32),
                   jax.ShapeDtypeStruct((1, DGLA), F32)] + ride.out_shape,
        scratch_shapes=[pltpu.VMEM((s, HP * GDK), F32), pltpu.VMEM((s, HP * GDK), F32), pltpu.VMEM((HP, nc, GDV, GDK), F32),
                        pltpu.VMEM((s, HP * GDK), F32), pltpu.VMEM((HP, GDV, GDK), F32), pltpu.VMEM((s, GAP), F32),
                        pltpu.VMEM((1, HP * GDV), F32), pltpu.VMEM((s, HP * GDK), F32),
                        pltpu.VMEM((nc, 8, HP * GDK), F32)] + ride.scratch,
        compiler_params=_params(("arbitrary",)),
    )(z, zga, wa, ba, ggla, dcg, *ride.operands)


def _dh(dzg, dza, dga, wm, wga, x, dout, gpre, send=None, tm=512, tk=1024):
    s = x.shape[0]
    nkg, nk = ZG // tk, ZM // tk

    def body(*refs):
        dzg_ref, dza_ref, dga_ref, wm_ref, wga_ref, x_ref, d_ref, g_ref = refs[:8]
        dx_ref, dg_ref, acc = _ride_refs(refs[8:], send, 2)[1]
        i, k = pl.program_id(0), pl.program_id(1)
        cps = _ride_start(refs[8:], send, 2, 1, (i == 0) & (k == 0))

        @pl.when((i == 0) & (k == 0))
        def _():
            dg_ref[...] = jnp.zeros_like(dg_ref)

        @pl.when(k == 0)
        def _():
            acc[...] = _dot(dga_ref[...], wga_ref[...])

        @pl.when(k < nkg)
        def _():
            acc[...] += _dot(dzg_ref[...], wm_ref[...])

        @pl.when(k >= nkg)
        def _():
            acc[...] += _dot(dza_ref[...], wm_ref[...])

        @pl.when(k == nk - 1)
        def _():
            xv = x_ref[...]
            r = _rms_rows(xv)
            xh = xv * r
            dh = acc[...]
            dg_ref[...] += jnp.sum(dh * xh, axis=0, keepdims=True)
            dn = dh * g_ref[...]
            dx_ref[...] = d_ref[...] + r * (dn - xh * jnp.mean(dn * xh, axis=-1, keepdims=True))

        _ride_wait(cps, 0, (i == s // tm - 1) & (k == nk - 1))

    ride = _ride_specs(send, 1)
    return pl.pallas_call(
        body, name="dh", grid=(s // tm, nk),
        in_specs=[pl.BlockSpec((tm, tk), lambda i, k: (i, jnp.minimum(k, nkg - 1))),
                  pl.BlockSpec((tm, tk), lambda i, k: (i, jnp.maximum(k - nkg, 0))),
                  pl.BlockSpec((tm, GAP), lambda i, k: (i, 0)), pl.BlockSpec((tk, D), lambda i, k: (k, 0)),
                  pl.BlockSpec((GAP, D), lambda i, k: (0, 0)), pl.BlockSpec((tm, D), lambda i, k: (i, 0)),
                  pl.BlockSpec((tm, D), lambda i, k: (i, 0)), pl.BlockSpec((1, D), lambda i, k: (0, 0))] + ride.in_specs,
        out_specs=[pl.BlockSpec((tm, D), lambda i, k: (i, 0)), pl.BlockSpec((1, D), lambda i, k: (0, 0))] + ride.out_specs,
        out_shape=[jax.ShapeDtypeStruct((s, D), F32), jax.ShapeDtypeStruct((1, D), F32)] + ride.out_shape,
        scratch_shapes=[pltpu.VMEM((tm, D), F32)] + ride.scratch,
        compiler_params=_params(("arbitrary", "arbitrary")),
    )(dzg, dza, dga, wm, wga, x, dout, gpre, *ride.operands)


def _adam(w, g, m, v, tr, name):
    rws, cols = w.shape

    def body(w_ref, g_ref, m_ref, v_ref, d_ref, mo_ref, vo_ref):
        gv = g_ref[...]
        mn = ADAM_B1 * m_ref[...] + (1.0 - ADAM_B1) * gv
        vn = ADAM_B2 * v_ref[...] + (1.0 - ADAM_B2) * (gv * gv)
        mh = mn / (1.0 - ADAM_B1 ** ADAM_STEP)
        vh = vn / (1.0 - ADAM_B2 ** ADAM_STEP)
        d_ref[...] = -ADAM_LR * (mh / (jnp.sqrt(vh) + ADAM_EPS) + ADAM_WD * w_ref[...])
        mo_ref[...] = mn
        vo_ref[...] = vn

    spec = pl.BlockSpec((tr, cols), lambda i: (i, 0))
    return pl.pallas_call(
        body, name=name, grid=(rws // tr,), in_specs=[spec] * 4, out_specs=[spec] * 3,
        out_shape=[jax.ShapeDtypeStruct((rws, cols), F32)] * 3,
        compiler_params=_params(("parallel",)),
    )(w, g, m, v)


def _adam_rows(w, g, m, v, tj, name):
    rows = w.shape[0]

    def body(w_ref, g_ref, m_ref, v_ref, d_ref, mo_ref, vo_ref):
        gv = g_ref[...]
        mn = ADAM_B1 * m_ref[...] + (1.0 - ADAM_B1) * gv
        vn = ADAM_B2 * v_ref[...] + (1.0 - ADAM_B2) * (gv * gv)
        mh = mn / (1.0 - ADAM_B1 ** ADAM_STEP)
        vh = vn / (1.0 - ADAM_B2 ** ADAM_STEP)
        d_ref[...] = -ADAM_LR * (mh / (jnp.sqrt(vh) + ADAM_EPS) + ADAM_WD * w_ref[...])
        mo_ref[...] = mn
        vo_ref[...] = vn

    spec = pl.BlockSpec((tj,) + w.shape[1:], lambda i: (i, 0, 0))
    return pl.pallas_call(
        body, name=name, grid=(pl.cdiv(rows, tj),), in_specs=[spec] * 4, out_specs=[spec] * 3,
        out_shape=[jax.ShapeDtypeStruct(w.shape, F32)] * 3,
        compiler_params=_params(("parallel",)),
    )(w, g, m, v)


def _gather_first(src_in, src_out, wa, buf_in, buf_out, wa_all):
    def variant(x, y, c, src_in_ref, src_out_ref, wa_ref, _b0, _b1, _b2, bin_ref, bout_ref, wa_ref_all, send_sems, recv_sems, wa_send, wa_recv):
        direct = []
        for part in range(NDIRECT):
            direct += _gather_copies(part, x, y, c, src_in_ref, src_out_ref, bin_ref, bout_ref, send_sems.at[part], recv_sems.at[part])
        for k, (px, py) in enumerate([(1 - x, y), (x, 1 - y), (1 - x, 1 - y)]):
            direct.append(pltpu.make_async_remote_copy(src_ref=wa_ref, dst_ref=wa_ref_all.at[2 * x + y], send_sem=wa_send.at[k],
                                                       recv_sem=wa_recv.at[k], device_id=(px, py, c), device_id_type=MESH))
        for cpy in direct:
            cpy.start()
        for cpy in direct:
            cpy.wait()
        passed = _gather_copies(NDIRECT, x, y, c, src_in_ref, src_out_ref, bin_ref, bout_ref, send_sems.at[NDIRECT], recv_sems.at[NDIRECT])
        for cpy in passed:
            cpy.start()
        for cpy in passed:
            cpy.wait()

    def body(*refs):
        _variants(lambda x, y, c: variant(x, y, c, *refs))

    return pl.pallas_call(
        body, name="gather_first", in_specs=[ANY] * 6, out_specs=[ANY] * 3, input_output_aliases={3: 0, 4: 1, 5: 2},
        out_shape=[jax.ShapeDtypeStruct(t.shape, t.dtype) for t in (buf_in, buf_out, wa_all)],
        scratch_shapes=[pltpu.SemaphoreType.DMA((NDIRECT + 1, 2, 3)), pltpu.SemaphoreType.DMA((NDIRECT + 1, 2, 3)),
                        pltpu.SemaphoreType.DMA((3,)), pltpu.SemaphoreType.DMA((3,))],
    )(src_in, src_out, wa, buf_in, buf_out, wa_all)


def _swap_halves(gin2, gout2):
    def body(gin_ref, gout_ref, rin, rout, send_sems, recv_sems):
        x, y, c, _ = _place()
        sib = (x, y, 1 - c)
        cps = [pltpu.make_async_remote_copy(src_ref=src.at[1 - c], dst_ref=dst, send_sem=send_sems.at[a],
                                            recv_sem=recv_sems.at[a], device_id=sib, device_id_type=MESH)
               for a, (src, dst) in enumerate([(gin_ref, rin), (gout_ref, rout)])]
        for cpy in cps:
            cpy.start()
        for cpy in cps:
            cpy.wait()

    return pl.pallas_call(
        body, name="swap_halves", in_specs=[ANY, ANY], out_specs=[ANY, ANY],
        out_shape=[jax.ShapeDtypeStruct(gin2.shape[1:], gin2.dtype), jax.ShapeDtypeStruct(gout2.shape[1:], gout2.dtype)],
        scratch_shapes=[pltpu.SemaphoreType.DMA((2,)), pltpu.SemaphoreType.DMA((2,))],
    )(gin2, gout2)


def _add_halves(c_idx, g2, r, tr, name):
    rows, cols = r.shape

    def body(c_ref, g_ref, r_ref, o_ref):
        o_ref[...] = (g_ref[0].astype(F32) + r_ref[...].astype(F32)).astype(_XDT)

    return pl.pallas_call(
        body, name=name,
        grid_spec=pltpu.PrefetchScalarGridSpec(
            num_scalar_prefetch=1, grid=(rows // tr,),
            in_specs=[pl.BlockSpec((1, tr, cols), lambda i, c_ref: (c_ref[0], i, 0)),
                      pl.BlockSpec((tr, cols), lambda i, c_ref: (i, 0))],
            out_specs=pl.BlockSpec((tr, cols), lambda i, c_ref: (i, 0))),
        out_shape=jax.ShapeDtypeStruct((rows, cols), _XDT),
        compiler_params=_params(("parallel",)),
    )(c_idx, g2, r)


def _send_to_owners(pin, pout, parts):
    def body(pin_ref, pout_ref, *refs):
        send_sems, recv_sems = refs[-2:]
        cps = []
        for n, part in enumerate(parts):
            cps += _owner_copies(part, pin_ref, pout_ref, refs[2 * n], refs[2 * n + 1], send_sems.at[n], recv_sems.at[n])
        for cpy in cps:
            cpy.start()
        for cpy in cps:
            cpy.wait()

    shapes = [jax.ShapeDtypeStruct((3, _part_rows(t.shape[2], split, part)[1], D), t.dtype)
              for part in parts for t, split in zip((pin, pout), (SPLIT_IN, SPLIT_OUT))]
    return pl.pallas_call(
        body, name="send_to_owners", in_specs=[ANY, ANY], out_specs=[ANY] * len(shapes), out_shape=shapes,
        scratch_shapes=[pltpu.SemaphoreType.DMA((len(parts), 2, 3)), pltpu.SemaphoreType.DMA((len(parts), 2, 3))],
    )(pin, pout)


def _add_chips(chip_idx, p, ra, rb, tr, name):
    rows = p.shape[2]
    na = ra.shape[1] // tr

    def body(c_ref, p_ref, ra_ref, rb_ref, o_ref):
        r = jnp.where(pl.program_id(0) < na, ra_ref[...], rb_ref[...]).astype(F32)
        o_ref[0] = ((p_ref[0, 0].astype(F32) + r[0]) + r[1]) + r[2]

    return pl.pallas_call(
        body, name=name,
        grid_spec=pltpu.PrefetchScalarGridSpec(
            num_scalar_prefetch=1, grid=(rows // tr,),
            in_specs=[pl.BlockSpec((1, 1, tr, D), lambda i, c_ref: (0, c_ref[0], i, 0)),
                      pl.BlockSpec((3, tr, D), lambda i, c_ref: (0, jnp.minimum(i, na - 1), 0)),
                      pl.BlockSpec((3, tr, D), lambda i, c_ref: (0, jnp.maximum(i - na, 0), 0))],
            out_specs=pl.BlockSpec((1, tr, D), lambda i, c_ref: (0, i, 0))),
        out_shape=jax.ShapeDtypeStruct((1, rows, D), F32),
        compiler_params=_params(("parallel",)),
    )(chip_idx, p, ra, rb)


def _exchange_halves(arrs):
    n = len(arrs)

    def body(*refs):
        x, y, c, _ = _place()
        cps = [pltpu.make_async_remote_copy(src_ref=refs[a], dst_ref=refs[n + a], send_sem=refs[2 * n].at[a], recv_sem=refs[2 * n + 1].at[a],
                                            device_id=(x, y, 1 - c), device_id_type=MESH) for a in range(n)]
        for cpy in cps:
            cpy.start()
        for cpy in cps:
            cpy.wait()

    return pl.pallas_call(
        body, name="exchange_halves", in_specs=[ANY] * n, out_specs=[ANY] * n,
        out_shape=[jax.ShapeDtypeStruct(t.shape, t.dtype) for t in arrs],
        scratch_shapes=[pltpu.SemaphoreType.DMA((n,)), pltpu.SemaphoreType.DMA((n,))],
    )(*arrs)


def _adam_halves(c_idx, w, g_own, g_other, m, v, tr, name):
    nl, _, rows, cols = w.shape

    def body(c_ref, w_ref, go_ref, gx_ref, m_ref, v_ref, g_ref, d_ref, mo_ref, vo_ref):
        gv = jnp.where(pl.program_id(1) == c_ref[0], go_ref[0], gx_ref[0])
        mn = ADAM_B1 * m_ref[0, 0] + (1.0 - ADAM_B1) * gv
        vn = ADAM_B2 * v_ref[0, 0] + (1.0 - ADAM_B2) * (gv * gv)
        mh = mn / (1.0 - ADAM_B1 ** ADAM_STEP)
        vh = vn / (1.0 - ADAM_B2 ** ADAM_STEP)
        g_ref[0, 0] = gv
        d_ref[0, 0] = -ADAM_LR * (mh / (jnp.sqrt(vh) + ADAM_EPS) + ADAM_WD * w_ref[0, 0])
        mo_ref[0, 0] = mn
        vo_ref[0, 0] = vn

    full = pl.BlockSpec((1, 1, tr, cols), lambda l, hh, i, c_ref: (l, hh, i, 0))
    own = pl.BlockSpec((1, tr, cols), lambda l, hh, i, c_ref: (l, jnp.where(hh == c_ref[0], i, 0), 0))
    other = pl.BlockSpec((1, tr, cols), lambda l, hh, i, c_ref: (l, jnp.where(hh == c_ref[0], 0, i), 0))
    return pl.pallas_call(
        body, name=name,
        grid_spec=pltpu.PrefetchScalarGridSpec(
            num_scalar_prefetch=1, grid=(nl, 2, rows // tr),
            in_specs=[full, own, other, full, full], out_specs=[full] * 4),
        out_shape=[jax.ShapeDtypeStruct(w.shape, F32)] * 4,
        compiler_params=_params(("parallel", "parallel", "parallel")),
    )(c_idx, w, g_own, g_other, m, v)


def _allreduce_small(sg):
    rows = sg.shape[0]
    vm = pl.BlockSpec(memory_space=pltpu.VMEM)

    def body(sg_ref, tot_ref, all_ref, send_sems, recv_sems):
        x, y, c, _ = _place()
        me = 4 * x + 2 * y + c
        all_ref[me] = sg_ref[...]
        cps = []
        for mask in range(1, 8):
            to = (1 - x if mask & 4 else x, 1 - y if mask & 2 else y, 1 - c if mask & 1 else c)
            cps.append(pltpu.make_async_remote_copy(src_ref=sg_ref, dst_ref=all_ref.at[me], send_sem=send_sems.at[mask - 1],
                                                    recv_sem=recv_sems.at[mask - 1], device_id=to, device_id_type=MESH))
        for cpy in cps:
            cpy.start()
        for cpy in cps:
            cpy.wait()
        acc = all_ref[0]
        for d in range(1, 8):
            acc = acc + all_ref[d]
        tot_ref[...] = acc

    return pl.pallas_call(
        body, name="allreduce_small", in_specs=[vm], out_specs=[vm, vm],
        out_shape=[jax.ShapeDtypeStruct((rows, 128), F32), jax.ShapeDtypeStruct((8, rows, 128), F32)],
        scratch_shapes=[pltpu.SemaphoreType.DMA((7,)), pltpu.SemaphoreType.DMA((7,))],
        compiler_params=_params(),
    )(sg)[0]


_CUTS = [0, 512, 1024, 2048, 3072, 3088, 4112, 5136, 6160, 7184]


def _rows_to_internal(w):
    tail = w.shape[1:]
    gq, gk, gv, gg, ga, aq, ak, av, ag = [w[_CUTS[i]:_CUTS[i + 1]] for i in range(9)]
    g = jnp.concatenate([gq.reshape((GH, GDK) + tail), gk.reshape((GH, GDK) + tail),
                         gv.reshape((GH, GDV) + tail), gg.reshape((GH, GDV) + tail)], axis=1).reshape((ZG,) + tail)
    a = jnp.concatenate([t.reshape((AH, AHD) + tail) for t in (aq, ak, av, ag)], axis=1).reshape((ZA,) + tail)
    pad = [(0, GAP - RANK)] + [(0, 0)] * len(tail)
    return jnp.concatenate([g, a], axis=0), jnp.pad(ga, pad)


def _rows_from_internal(g, a, ga):
    tail = g.shape[1:]
    g = g.reshape((GH, GW) + tail)
    a = a.reshape((AH, AW) + tail)
    parts = [g[:, 0:GDK], g[:, GDK:2 * GDK], g[:, 2 * GDK:2 * GDK + GDV], g[:, 2 * GDK + GDV:GW]]
    parts = [t.reshape((-1,) + tail) for t in parts] + [ga[0:RANK]]
    parts += [a[:, i * AHD:(i + 1) * AHD].reshape((-1,) + tail) for i in range(4)]
    return jnp.concatenate(parts, axis=0)


def _slab_lo(chip):
    return min(SHARD * chip // 16 * 16, DIN - SLAB)


def _pack_rows(parts):
    rows = []
    for t in parts:
        flat = t.reshape(-1)
        rows.append(jnp.pad(flat, (0, (-flat.shape[0]) % 128)).reshape(-1, 128))
    buf = jnp.concatenate(rows, axis=0)
    return jnp.pad(buf, ((0, (-buf.shape[0]) % 8), (0, 0)))


def _unpack_rows(buf, shapes):
    out, r = [], 0
    for shp in shapes:
        n = 1
        for d in shp:
            n *= d
        nr = -(-n // 128)
        out.append(buf[r:r + nr].reshape(-1)[:n].reshape(shp))
        r += nr
    return out


def _layer_fwd(x, wm, wga, wout, gpre, gpost, wa, ba, ggla, gatt, rbx, ride=None):
    z, zga, h, *bufs = _inproj(x, gpre, wm, wga, ride)
    ride = None if ride is None else (ride[0], ride[1], *bufs)
    cg, *bufs = _gla_fwd(z, zga, wa, ba, ggla, ride)
    ride = None if ride is None else (ride[0], ride[1], *bufs)
    ca, oraw, lse, *bufs = _att_fwd(z, rbx, gatt, ride)
    ride = None if ride is None else (ride[0], ride[1], *bufs)
    y, xo, *bufs = _outproj(cg, ca, wout, x, gpost, ride)
    return xo, (x, z, zga, h, cg, ca, oraw, lse, y), bufs


def _layer_bwd(dout, saved, wm, wga, wout, gpre, gpost, wa, ba, ggla, gatt, rbx, swap=None, finish=None, reduce_now=None):
    x, z, zga, h, cg, ca, oraw, lse, y = saved
    dy, dcg, dca, dgpost, *swapped = _post_bwd(dout, y, gpost, wout, swap)
    send = None if swap is None else finish(swap, swapped)
    dwout = jnp.concatenate([_matmul_tn(cg, dy, _XDT, 512, 1024, "dwout_gla"),
                             _matmul_tn(ca, dy, _XDT, 512, 1024, "dwout_att")], axis=0)
    dza, dgatt, dbx, *got_a = _att_bwd(z, oraw, lse, dca, rbx, gatt, send)
    dzg, dga, dwa, dba, dggla, *got_b = _gla_bwd(z, zga, wa, ba, ggla, dcg, send)
    dwin = (_matmul_tn(dzg, h, _XDT, 512, 1024, "dwin_gla"), _matmul_tn(dza, h, _XDT, 512, 1024, "dwin_att"),
            _matmul_tn(dga, h, _XDT, GAP, 1024, "dwin_gate"))
    own = None if reduce_now is None else reduce_now(dwin, dwout)
    dx, dgpre, *got_own = _dh(dzg, dza, dga, wm, wga, x, dout, gpre, own)
    drb = jnp.concatenate([jnp.zeros((AH, 1), F32), dbx[:, 0, ::-1]], axis=1)
    return dx, dwin, dwout, (dgpre[0], dgpost[0], dwa[0:RANK], dba[0], dggla[0], dgatt[0], drb), send, got_a + got_b, own, got_own


def _rel_rows(rb):
    return rb[:, :0:-1][:, None, :]


def kernel(x, w_in, w_out, g_pre, g_post, w_alpha, b_alpha, g_gla, g_att, rel_bias, loss_target, m_w_in, m_w_out, m_g_pre, m_g_post, m_w_alpha, m_b_alpha, m_g_gla, m_g_att, m_rel_bias, v_w_in, v_w_out, v_g_pre, v_g_post, v_w_alpha, v_b_alpha, v_g_gla, v_g_att, v_rel_bias):
    nl = w_in.shape[0]
    ax, ay, ac = lax.axis_index("x"), lax.axis_index("y"), lax.axis_index("c")
    chip = 2 * ax + ay
    c_idx = jnp.reshape(ac, (1,)).astype(jnp.int32)
    chip_idx = jnp.reshape(chip, (1,)).astype(jnp.int32)

    phase = [SHARD * i % 16 for i in range(NCHIP)]
    wt_rows = jnp.transpose(w_in, (0, 2, 1)).astype(_CDT)
    at_phase = [functools.partial(jnp.pad, wt_rows, ((0, 0), (p, WSLOT - SHARD - p), (0, 0))) for p in phase]
    wt_src = lax.switch(chip, at_phase).reshape(nl, 2, WSLOT // 2, D)
    wout_src = w_out.astype(_CDT).reshape(nl, 2, D // NCHIP // 2, D)

    def with_own(own):
        start = [chip] + [0] * own.ndim
        return lax.dynamic_update_slice(lax.empty((NCHIP,) + own.shape, own.dtype), own[None], start)

    def gather_operands(l):
        return wt_src[l], wout_src[l], with_own(wt_src[l]), with_own(wout_src[l])

    def layer_weights(bufs):
        wt4 = bufs[0].reshape(NCHIP, WSLOT, D)
        wref = jnp.concatenate([wt4[i, phase[i]:phase[i] + SHARD] for i in range(NCHIP)])
        return _rows_to_internal(wref) + (bufs[1].reshape(D, D),)

    first = gather_operands(0)
    bin0, bout0, wa_all = _gather_first(first[0], first[1], w_alpha, first[2], first[3], with_own(w_alpha))
    wa_full = jnp.transpose(wa_all, (1, 2, 0, 3)).reshape(nl, RANK, GH * GDK)
    wa_pad = jnp.pad(wa_full, ((0, 0), (0, GAP - RANK), (0, 0))).astype(_CDT)
    rbx = [_rel_rows(rel_bias[l]) for l in range(nl)]

    def weights(l):
        return big[l] + (g_pre[l][None], g_post[l][None], wa_pad[l], b_alpha[l][None], g_gla[l][None], g_att[l][None], rbx[l])

    h = x[0]
    saved, big = [], [None] * nl
    big[0] = layer_weights((bin0, bout0))
    for l in range(nl):
        h, sv, bufs = _layer_fwd(h, *weights(l), ride=gather_operands(l + 1) if l + 1 < nl else None)
        saved.append(sv)
        if l + 1 < nl:
            big[l + 1] = layer_weights(bufs)
    dout, loss_part = _loss_grad(h, loss_target[0])

    small, hin, hout = [None] * nl, [None] * nl, [None] * nl
    hw = D // NCHIP // 2

    def reduce_owner(sent, got):
        rin_a, rout_a, rin_b, rout_b = got
        return (_add_chips(chip_idx, sent[0], rin_a, rin_b, 48, "add_chips_in"),
                _add_chips(chip_idx, sent[1], rout_a, rout_b, 128, "add_chips_out"))

    def slab_halves(dwin, dwout):
        gt = _rows_from_internal(*dwin)
        slabs = jnp.stack([gt[_slab_lo(i):_slab_lo(i) + SLAB] for i in range(NCHIP)])
        return (jnp.transpose(slabs.reshape(NCHIP, 2, HSLAB, D), (1, 0, 2, 3)).reshape(2, NCHIP * HSLAB, D),
                jnp.transpose(dwout.reshape(NCHIP, 2, hw, D), (1, 0, 2, 3)).reshape(2, NCHIP * hw, D))

    def sum_halves(halves, swapped):
        return (_add_halves(c_idx, halves[0], swapped[0], 192, "add_halves_in").reshape(1, NCHIP, HSLAB, D),
                _add_halves(c_idx, halves[1], swapped[1], 256, "add_halves_out").reshape(1, NCHIP, hw, D))

    def partial_sums(dwin, dwout):
        halves = slab_halves(dwin, dwout)
        return sum_halves(halves, _swap_halves(*halves))

    halves = None
    for l in reversed(range(nl)):
        dout, dwin, dwout, small[l], sent, got, own, got_own = _layer_bwd(
            dout, saved[l], *weights(l), swap=halves, finish=sum_halves, reduce_now=partial_sums if l == 0 else None)
        if halves is not None:
            hin[l + 1], hout[l + 1] = reduce_owner(sent, got)
        halves = slab_halves(dwin, dwout) if l > 0 else None
    sent = own
    grad_x = dout[None]
    hin[0], hout[0] = reduce_owner(sent, list(_send_to_owners(*sent, parts=(0,))) + list(got_own))
    xchg = _exchange_halves(hin + hout)
    hin, xin = jnp.concatenate(hin), jnp.concatenate(xchg[:nl])
    hout, xout = jnp.concatenate(hout), jnp.concatenate(xchg[nl:])

    slab = jnp.concatenate([jnp.where(ac == 0, hin, xin), jnp.where(ac == 0, xin, hin)], axis=1)
    off = sum(jnp.where(chip == i, SHARD * i - _slab_lo(i), 0) for i in range(NCHIP))
    g_rows = jnp.transpose(lax.dynamic_slice_in_dim(slab, off, SHARD, axis=1), (1, 0, 2))
    rows_first = lambda t: jnp.transpose(t, (2, 0, 1))
    d_rows, nm_rows, nv_rows = _adam_rows(rows_first(w_in), g_rows, rows_first(m_w_in), rows_first(v_w_in), 32, "adam_w_in")
    g_w_in, d_w_in, nm_w_in, nv_w_in = [jnp.transpose(t, (1, 2, 0)) for t in (g_rows, d_rows, nm_rows, nv_rows)]

    def adam_big(w, g_own, g_other, m, v, name):
        shp = w.shape
        halves = lambda t: t.reshape(shp[0], 2, shp[1] // 2, shp[2])
        return [t.reshape(shp) for t in _adam_halves(c_idx, halves(w), g_own, g_other, halves(m), halves(v), 256, name)]

    g_w_out, d_w_out, nm_w_out, nv_w_out = adam_big(w_out, hout, xout, m_w_out, v_w_out, "adam_w_out")

    stacked = [jnp.stack([small[l][i] for l in range(nl)]) for i in range(7)] + [loss_part]
    g_small = _unpack_rows(_allreduce_small(_pack_rows(stacked)), [t.shape for t in stacked])
    g_gpre, g_gpost, g_wa_full, g_ba, g_ggla, g_gatt, g_rb, loss_sum = g_small
    loss = loss_sum[0, 0]
    g_wa = lax.dynamic_slice_in_dim(g_wa_full, chip * GDK, GDK, axis=2)
    names = [(g_pre, m_g_pre, v_g_pre, g_gpre), (g_post, m_g_post, v_g_post, g_gpost), (w_alpha, m_w_alpha, v_w_alpha, g_wa),
             (b_alpha, m_b_alpha, v_b_alpha, g_ba), (g_gla, m_g_gla, v_g_gla, g_ggla), (g_att, m_g_att, v_g_att, g_gatt),
             (rel_bias, m_rel_bias, v_rel_bias, g_rb)]
    shapes = [t[0].shape for t in names]
    packed = [_pack_rows([t[i] for t in names]) for i in range(4)]
    d_s, nm_s, nv_s = [_unpack_rows(t, shapes) for t in _adam(packed[0], packed[3], packed[1], packed[2], packed[0].shape[0], "adam_small")]

    grads = [g_w_in, g_w_out, g_gpre, g_gpost, g_wa, g_ba, g_ggla, g_gatt, g_rb]
    deltas = [d_w_in, d_w_out] + d_s
    new_m = [nm_w_in, nm_w_out] + nm_s
    new_v = [nv_w_in, nv_w_out] + nv_s
    return (loss, grad_x, *grads, *deltas, *new_m, *new_v)
```

```python
import functools

import jax
import jax.numpy as jnp
from jax import lax
from jax.experimental import pallas as pl
from jax.experimental.pallas import tpu as pltpu

D = 2048
DEPTH = 4
CHUNK = 64
GH, GDK, GDV = 4, 128, 256
DGLA = GH * GDV
RANK = 16
TAU = 16.0
AH, AHD = 8, 128
DATT = AH * AHD
LEFT = 8
NREL = 257
EPS = 1e-6
DIN = 7184
ADAM_LR, ADAM_B1, ADAM_B2, ADAM_EPS, ADAM_WD, ADAM_STEP = 0.001, 0.9, 0.999, 1e-08, 0.01, 10

GW = 2 * GDK + 2 * GDV
AW = 4 * AHD
ZG = GH * GW
ZA = AH * AW
ZM = ZG + ZA
GAP = 128
QB = 2 * CHUNK
HP = 2
BANDW = (LEFT + 2) * CHUNK
PADK = LEFT * CHUNK
NCHIP = 4
SHARD = DIN // NCHIP
SLAB = 1824
HSLAB = SLAB // 2
WSLOT = 1824
GPARTS_IN = [(0, 368), (368, 192), (560, 352)]
GPARTS_OUT = [(0, 112), (112, 48), (160, 96)]
NDIRECT = len(GPARTS_IN)
SPLIT_IN, SPLIT_OUT = 336, 128
NEG = -1e30
F32 = jnp.float32
_CDT = jnp.bfloat16
_XDT = jnp.bfloat16
_VMEM = 56 * 1024 * 1024
MESH = pl.DeviceIdType.MESH
ANY = pl.BlockSpec(memory_space=pl.ANY)


def _dot(a, b):
    return jnp.dot(a, b, preferred_element_type=F32)


def _dot_nt(a, b):
    return lax.dot_general(a, b, (((1,), (1,)), ((), ())), preferred_element_type=F32)


def _dot_tn(a, b):
    return lax.dot_general(a, b, (((0,), (0,)), ((), ())), preferred_element_type=F32)


def _rms_rows(v):
    return lax.rsqrt(jnp.mean(v * v, axis=-1, keepdims=True) + EPS)


def _sigmoid(v):
    return 1.0 / (1.0 + jnp.exp(-v))


def _log_sigmoid(v):
    return jnp.minimum(v, 0.0) - jnp.log(1.0 + jnp.exp(-jnp.abs(v)))


def _exact_dot(tri, v):
    hi = v.astype(_CDT)
    r1 = v - hi.astype(F32)
    mid = r1.astype(_CDT)
    lo = (r1 - mid.astype(F32)).astype(_CDT)
    return _dot(tri, hi) + _dot(tri, mid) + _dot(tri, lo)


def _tri(strict):
    row = lax.broadcasted_iota(jnp.int32, (CHUNK, CHUNK), 0)
    col = lax.broadcasted_iota(jnp.int32, (CHUNK, CHUNK), 1)
    return jnp.where((col < row) if strict else (col <= row), 1.0, 0.0).astype(_CDT)


def _norm_gate_bwd(o, g, gate, dcat):
    r = _rms_rows(o)
    oh = o * r
    sg = _sigmoid(gate)
    dn = dcat * (gate * sg)
    dgate = dcat * (oh * g) * (sg * (1.0 + gate * (1.0 - sg)))
    dg = jnp.sum(dn * oh, axis=0, keepdims=True)
    dnn = dn * g
    do = r * (dnn - oh * jnp.mean(dnn * oh, axis=-1, keepdims=True))
    return do, dgate, dg


def _params(sem=None, vmem=_VMEM):
    return pltpu.CompilerParams(dimension_semantics=sem, vmem_limit_bytes=vmem)


def _inproj(x, g, wm, wga, ride=None, tm=512, tn=1024):
    s = x.shape[0]
    gr = _gather_ride(ride, 0)

    def body(*refs):
        (x_ref, g_ref, wm_ref, wga_ref, z_ref, zga_ref, h_ref, hs), rr = gr.split(refs, 4, 3)
        i, j = pl.program_id(0), pl.program_id(1)
        gr.start(rr, (i == 0) & (j == 0))

        @pl.when(pl.program_id(1) == 0)
        def _():
            xv = x_ref[...]
            hv = (xv * _rms_rows(xv) * g_ref[...]).astype(_CDT)
            hs[...] = hv
            h_ref[...] = hv
            zga_ref[...] = _dot_nt(hv, wga_ref[...]).astype(_CDT)

        z_ref[...] = _dot_nt(hs[...], wm_ref[...]).astype(_CDT)
        gr.wait(rr, (i == s // tm - 1) & (j == ZM // tn - 1))

    return pl.pallas_call(
        body, name="inproj", grid=(s // tm, ZM // tn),
        in_specs=[pl.BlockSpec((tm, D), lambda i, j: (i, 0)), pl.BlockSpec((1, D), lambda i, j: (0, 0)),
                  pl.BlockSpec((tn, D), lambda i, j: (j, 0)), pl.BlockSpec((GAP, D), lambda i, j: (0, 0))] + gr.in_specs,
        out_specs=[pl.BlockSpec((tm, tn), lambda i, j: (i, j)), pl.BlockSpec((tm, GAP), lambda i, j: (i, 0)),
                   pl.BlockSpec((tm, D), lambda i, j: (i, 0))] + gr.out_specs,
        out_shape=[jax.ShapeDtypeStruct((s, ZM), _CDT), jax.ShapeDtypeStruct((s, GAP), _CDT),
                   jax.ShapeDtypeStruct((s, D), _CDT)] + gr.out_shape,
        scratch_shapes=[pltpu.VMEM((tm, D), _CDT)] + gr.scratch, input_output_aliases=gr.alias(4, 3),
        compiler_params=_params(("arbitrary", "arbitrary")),
    )(x, g, wm, wga, *gr.operands)


def _gla_fwd(z, zga, wa, ba, ggla, ride=None):
    s = z.shape[0]
    nc = s // CHUNK
    gr = _gather_ride(ride, 1)

    def body(*refs):
        (zg_ref, zga_ref, wa_ref, ba_ref, g_ref, cat_ref, la_s, st), rr = gr.split(refs, 5, 1)
        gr.start(rr, pl.program_id(0) == 0)
        la_s[...] = _log_sigmoid(_dot(zga_ref[...], wa_ref[...]) + ba_ref[...]) * (1.0 / TAU)
        st[...] = jnp.zeros_like(st)
        tri = _tri(False)

        def step(n, carry):
            rows = pl.ds(pl.multiple_of(n * CHUNK, CHUNK), CHUNK)
            for p in range(HP):
                z0 = p * GW
                la = la_s[rows, p * GDK:(p + 1) * GDK]
                lc = _exact_dot(tri, la)
                lend = jnp.sum(la, axis=0, keepdims=True)
                kdec = (zg_ref[rows, z0 + GDK:z0 + 2 * GDK].astype(F32) * jnp.exp(lend - lc)).astype(_CDT)
                stn = jnp.exp(lend) * st[p] + _dot_tn(zg_ref[rows, z0 + 2 * GDK:z0 + 2 * GDK + GDV], kdec)
                st[p] = stn
                qs = (zg_ref[rows, z0:z0 + GDK].astype(F32) * (GDK ** -0.5)).astype(_CDT)
                o = _dot_nt(qs, stn.astype(_CDT))
                gate = zg_ref[rows, z0 + 2 * GDK + GDV:z0 + GW].astype(F32)
                gain = g_ref[:, p * GDV:(p + 1) * GDV]
                cat_ref[rows, p * GDV:(p + 1) * GDV] = (o * _rms_rows(o) * gain * (gate * _sigmoid(gate))).astype(_CDT)
            return carry

        lax.fori_loop(0, nc, step, 0, unroll=4)
        gr.wait(rr, pl.program_id(0) == GH // HP - 1)

    return pl.pallas_call(
        body, name="gla_fwd", grid=(GH // HP,),
        in_specs=[pl.BlockSpec((s, HP * GW), lambda h: (0, h)), pl.BlockSpec((s, GAP), lambda h: (0, 0)),
                  pl.BlockSpec((GAP, HP * GDK), lambda h: (0, h)), pl.BlockSpec((1, HP * GDK), lambda h: (0, h)),
                  pl.BlockSpec((1, HP * GDV), lambda h: (0, h))] + gr.in_specs,
        out_specs=[pl.BlockSpec((s, HP * GDV), lambda h: (0, h))] + gr.out_specs,
        out_shape=[jax.ShapeDtypeStruct((s, DGLA), _CDT)] + gr.out_shape,
        scratch_shapes=[pltpu.VMEM((s, HP * GDK), F32), pltpu.VMEM((HP, GDV, GDK), F32)] + gr.scratch,
        input_output_aliases=gr.alias(5, 1), compiler_params=_params(("arbitrary",)),
    )(z, zga, wa, ba, ggla, *gr.operands)


def _band_bias(b0):
    row = lax.broadcasted_iota(jnp.int32, (QB, 256), 0)
    col = lax.broadcasted_iota(jnp.int32, (QB, 256), 1)
    lane = lax.broadcasted_iota(jnp.int32, (1, 256), 1)
    c0 = jnp.sum(jnp.where(lane == 0, b0, 0.0), axis=1, keepdims=True)
    xv = jnp.broadcast_to(b0, (QB, 256))
    for bit in range(7):
        xv = jnp.where(((row >> bit) & 1) == 1, pltpu.roll(xv, 1 << bit, 1), xv)
    xv = jnp.where(col < row, c0, xv)
    return jnp.concatenate([jnp.broadcast_to(c0, (QB, BANDW - 256)), xv], axis=1)


def _band_static_mask():
    row = lax.broadcasted_iota(jnp.int32, (QB, BANDW), 0) >> 6
    col = lax.broadcasted_iota(jnp.int32, (QB, BANDW), 1) >> 6
    return (col >= row) & (col <= row + LEFT)


def _fold_bias_grad(t):
    row = lax.broadcasted_iota(jnp.int32, (QB, 256), 0)
    col = lax.broadcasted_iota(jnp.int32, (QB, 256), 1)
    xv = t[:, BANDW - 256:]
    low = col < row
    far = jnp.sum(t[:, 0:BANDW - 256], axis=1, keepdims=True) + jnp.sum(jnp.where(low, xv, 0.0), axis=1, keepdims=True)
    far = jnp.sum(far, axis=0, keepdims=True)
    xv = jnp.where(low, 0.0, xv)
    for bit in range(7):
        xv = jnp.where(((row >> bit) & 1) == 1, pltpu.roll(xv, 256 - (1 << bit), 1), xv)
    dp = jnp.sum(xv, axis=0, keepdims=True)
    lane = lax.broadcasted_iota(jnp.int32, (1, 256), 1)
    return dp + jnp.where(lane == 0, far, 0.0)


def _att_fwd(z, rbx, gatt, ride=None):
    s = z.shape[0]
    nb = s // QB
    gr = _gather_ride(ride, 2)

    def body(*refs):
        (za_ref, rb_ref, g_ref, cat_ref, o_ref, lse_ref, kp, vp, bias_s), rr = gr.split(refs, 3, 3)
        gr.start(rr, pl.program_id(0) == 0)
        for p in range(HP):
            z0 = p * AW
            kp[p, 0:PADK, :] = jnp.zeros((PADK, AHD), _CDT)
            vp[p, 0:PADK, :] = jnp.zeros((PADK, AHD), _CDT)
            kp[p, PADK:, :] = za_ref[:, z0 + AHD:z0 + 2 * AHD]
            vp[p, PADK:, :] = za_ref[:, z0 + 2 * AHD:z0 + 3 * AHD]
            bias_s[p] = jnp.where(_band_static_mask(), _band_bias(rb_ref[p]), NEG)

        def step(b, carry):
            r0 = pl.multiple_of(b * QB, QB)
            rows = pl.ds(r0, QB)
            band = pl.ds(r0, BANDW)
            live = lax.broadcasted_iota(jnp.int32, (QB, BANDW), 1) >= PADK - r0
            for p in range(HP):
                z0 = p * AW
                cols = slice(p * AHD, (p + 1) * AHD)
                sc = _dot_nt(za_ref[rows, z0:z0 + AHD], kp[p, band, :]) * (AHD ** -0.5) + bias_s[p]
                sc = jnp.where(live, sc, NEG)
                m = jnp.max(sc, axis=-1, keepdims=True)
                pr = jnp.exp(sc - m)
                l = jnp.sum(pr, axis=-1, keepdims=True)
                o = _dot((pr * (1.0 / l)).astype(_CDT), vp[p, band, :])
                o_ref[rows, cols] = o.astype(_CDT)
                lse_ref[rows, cols] = jnp.broadcast_to(m + jnp.log(l), (QB, AHD))
                gate = za_ref[rows, z0 + 3 * AHD:z0 + AW].astype(F32)
                cat_ref[rows, cols] = (o * _rms_rows(o) * g_ref[:, cols] * (gate * _sigmoid(gate))).astype(_CDT)
            return carry

        lax.fori_loop(0, nb, step, 0, unroll=4)
        gr.wait(rr, pl.program_id(0) == AH // HP - 1)

    return pl.pallas_call(
        body, name="att_fwd", grid=(AH // HP,),
        in_specs=[pl.BlockSpec((s, HP * AW), lambda h: (0, ZG // (HP * AW) + h)), pl.BlockSpec((HP, 1, 256), lambda h: (h, 0, 0)),
                  pl.BlockSpec((1, HP * AHD), lambda h: (0, h))] + gr.in_specs,
        out_specs=[pl.BlockSpec((s, HP * AHD), lambda h: (0, h)), pl.BlockSpec((s, HP * AHD), lambda h: (0, h)),
                   pl.BlockSpec((s, HP * AHD), lambda h: (0, h))] + gr.out_specs,
        out_shape=[jax.ShapeDtypeStruct((s, DATT), _CDT), jax.ShapeDtypeStruct((s, DATT), _CDT),
                   jax.ShapeDtypeStruct((s, DATT), F32)] + gr.out_shape,
        scratch_shapes=[pltpu.VMEM((HP, s + PADK, AHD), _CDT), pltpu.VMEM((HP, s + PADK, AHD), _CDT),
                        pltpu.VMEM((HP, QB, BANDW), F32)] + gr.scratch, input_output_aliases=gr.alias(3, 3),
        compiler_params=_params(("arbitrary",)),
    )(z, rbx, gatt, *gr.operands)


def _outproj(cg, ca, wout, x, gpost, ride=None, tm=256):
    s = x.shape[0]
    gr = _gather_ride(ride, NDIRECT)

    def body(*refs):
        (cg_ref, ca_ref, w_ref, x_ref, g_ref, y_ref, xo_ref), rr = gr.split(refs, 5, 2)
        gr.start(rr, pl.program_id(0) == 0)
        y = _dot(cg_ref[...], w_ref[0:DGLA, :]) + _dot(ca_ref[...], w_ref[DGLA:, :])
        y_ref[...] = y
        xo_ref[...] = x_ref[...] + y * _rms_rows(y) * g_ref[...]
        gr.wait(rr, pl.program_id(0) == s // tm - 1)

    return pl.pallas_call(
        body, name="outproj", grid=(s // tm,),
        in_specs=[pl.BlockSpec((tm, DGLA), lambda i: (i, 0)), pl.BlockSpec((tm, DATT), lambda i: (i, 0)),
                  pl.BlockSpec((D, D), lambda i: (0, 0)), pl.BlockSpec((tm, D), lambda i: (i, 0)),
                  pl.BlockSpec((1, D), lambda i: (0, 0))] + gr.in_specs,
        out_specs=[pl.BlockSpec((tm, D), lambda i: (i, 0)), pl.BlockSpec((tm, D), lambda i: (i, 0))] + gr.out_specs,
        out_shape=[jax.ShapeDtypeStruct((s, D), F32), jax.ShapeDtypeStruct((s, D), F32)] + gr.out_shape,
        scratch_shapes=gr.scratch, input_output_aliases=gr.alias(5, 2),
        compiler_params=_params(("arbitrary",)),
    )(cg, ca, wout, x, gpost, *gr.operands)


def _loss_grad(xo, tgt, tm=256):
    s = xo.shape[0]

    def body(xo_ref, t_ref, d_ref, l_ref):
        @pl.when(pl.program_id(0) == 0)
        def _():
            l_ref[...] = jnp.zeros_like(l_ref)

        e = xo_ref[...] - t_ref[...]
        d_ref[...] = e * (1.0 / D)
        l_ref[...] += jnp.sum(jnp.sum(e * e, axis=1, keepdims=True), axis=0, keepdims=True) * (0.5 / D)

    return pl.pallas_call(
        body, name="loss_grad", grid=(s // tm,),
        in_specs=[pl.BlockSpec((tm, D), lambda i: (i, 0)), pl.BlockSpec((tm, D), lambda i: (i, 0))],
        out_specs=[pl.BlockSpec((tm, D), lambda i: (i, 0)), pl.BlockSpec((1, 1), lambda i: (0, 0))],
        out_shape=[jax.ShapeDtypeStruct((s, D), F32), jax.ShapeDtypeStruct((1, 1), F32)],
        compiler_params=_params(("arbitrary",)),
    )(xo, tgt)


def _place():
    x, y, c = lax.axis_index("x"), lax.axis_index("y"), lax.axis_index("c")
    chips = [(1 - x, y), (x, 1 - y), (1 - x, 1 - y)]
    return x, y, c, chips


def _variants(fn):
    x, y, c, _ = _place()
    for jx in range(2):
        for jy in range(2):
            for jc in range(2):
                pl.when((x == jx) & (y == jy) & (c == jc))(functools.partial(fn, jx, jy, jc))


def _gather_copies(part, x, y, c, src_in, src_out, buf_in, buf_out, send_sems, recv_sems):
    me = 2 * x + y
    peers = [(1 - x, y), (x, 1 - y), (1 - x, 1 - y)]
    cps = []
    for k, (px, py) in enumerate(peers):
        for a, (src, buf, parts) in enumerate([(src_in, buf_in, GPARTS_IN), (src_out, buf_out, GPARTS_OUT)]):
            if part < NDIRECT:
                rows = pl.ds(*parts[part])
                pair = (src.at[c, rows], buf.at[me, c, rows], (px, py, c))
            else:
                piece = buf.at[2 * px + py, c]
                pair = (piece, piece, (x, y, 1 - c))
            cps.append(pltpu.make_async_remote_copy(src_ref=pair[0], dst_ref=pair[1], send_sem=send_sems.at[a, k],
                                                    recv_sem=recv_sems.at[a, k], device_id=pair[2], device_id_type=MESH))
    return cps


class _gather_ride:
    def __init__(self, ride, part):
        self.part, self.on = part, ride is not None
        self.in_specs, self.out_specs, self.out_shape, self.scratch, self.operands, self.aliases = [], [], [], [], [], {}
        if self.on:
            self.operands = list(ride)
            self.in_specs, self.out_specs = [ANY] * 4, [ANY] * 2
            self.out_shape = [jax.ShapeDtypeStruct(t.shape, t.dtype) for t in ride[2:]]
            self.scratch = [pltpu.SemaphoreType.DMA((2, 3)), pltpu.SemaphoreType.DMA((2, 3))]

    def alias(self, n_in, n_out):
        return {n_in + 2: n_out, n_in + 3: n_out + 1} if self.on else {}

    def split(self, refs, n_in, n_out):
        if not self.on:
            return refs, None
        own = refs[:n_in] + refs[n_in + 4:n_in + 4 + n_out] + refs[n_in + 6 + n_out:-2]
        return own, refs[n_in:n_in + 2] + refs[n_in + 4 + n_out:n_in + 6 + n_out] + refs[-2:]

    def start(self, ride_refs, first):
        if self.on:
            def go(x, y, c):
                for cpy in _gather_copies(self.part, x, y, c, *ride_refs):
                    cpy.start()
            pl.when(first)(lambda: _variants(go))

    def wait(self, ride_refs, last):
        if self.on:
            def done(x, y, c):
                for cpy in _gather_copies(self.part, x, y, c, *ride_refs):
                    cpy.wait()
            pl.when(last)(lambda: _variants(done))


def _part_rows(total, split, part):
    return (0, split) if part == 0 else (split, total - split)


def _owner_copies(part, pin_ref, pout_ref, rin, rout, send_sems, recv_sems):
    x, y, c, chips = _place()
    cps = []
    for k, (px, py) in enumerate(chips):
        for a, (src, dst, split) in enumerate([(pin_ref, rin, SPLIT_IN), (pout_ref, rout, SPLIT_OUT)]):
            r0, n = _part_rows(src.shape[2], split, part)
            cps.append(pltpu.make_async_remote_copy(src_ref=src.at[0, 2 * px + py, pl.ds(r0, n)], dst_ref=dst.at[k],
                                                    send_sem=send_sems.at[a, k], recv_sem=recv_sems.at[a, k],
                                                    device_id=(px, py, c), device_id_type=MESH))
    return cps


class _ride_specs:
    def __init__(self, send, part):
        self.in_specs, self.out_specs, self.out_shape, self.scratch, self.operands = [], [], [], [], []
        if send is not None:
            self.in_specs, self.out_specs, self.operands = [ANY, ANY], [ANY, ANY], list(send)
            self.out_shape = [jax.ShapeDtypeStruct((3, _part_rows(t.shape[2], split, part)[1], D), t.dtype)
                              for t, split in zip(send, (SPLIT_IN, SPLIT_OUT))]
            self.scratch = [pltpu.SemaphoreType.DMA((2, 3)), pltpu.SemaphoreType.DMA((2, 3))]


def _ride_refs(refs, send, n_out):
    if send is None:
        return None, refs
    pin_ref, pout_ref = refs[:2]
    own_out = refs[2:2 + n_out]
    rin, rout = refs[2 + n_out:4 + n_out]
    return (pin_ref, pout_ref, rin, rout, refs[-2], refs[-1]), tuple(own_out) + tuple(refs[4 + n_out:-2])


def _ride_start(refs, send, n_out, part, first=None):
    ride = _ride_refs(refs, send, n_out)[0]
    if ride is None:
        return []
    cps = _owner_copies(part, *ride)

    @pl.when(pl.program_id(0) == 0 if first is None else first)
    def _():
        for cpy in cps:
            cpy.start()

    return cps


def _ride_wait(cps, steps, last=None):
    if cps:
        @pl.when(pl.program_id(0) == steps - 1 if last is None else last)
        def _():
            for cpy in cps:
                cpy.wait()


def _post_bwd(dout, y, gpost, wout, swap=None, tm=256):
    s = y.shape[0]

    def body(*refs):
        d_ref, y_ref, g_ref, w_ref = refs[:4]
        dy_ref, dcg_ref, dca_ref, dg_ref = refs[6:10] if swap is not None else refs[4:8]
        cps = []
        if swap is not None:
            x, yy, c, _ = _place()
            cps = [pltpu.make_async_remote_copy(src_ref=refs[4 + a].at[1 - c], dst_ref=refs[10 + a], send_sem=refs[12].at[a],
                                                recv_sem=refs[13].at[a], device_id=(x, yy, 1 - c), device_id_type=MESH) for a in range(2)]

            @pl.when(pl.program_id(0) == 0)
            def _():
                for cpy in cps:
                    cpy.start()

        @pl.when(pl.program_id(0) == 0)
        def _():
            dg_ref[...] = jnp.zeros_like(dg_ref)

        yv = y_ref[...]
        r = _rms_rows(yv)
        yh = yv * r
        dv = d_ref[...]
        dg_ref[...] += jnp.sum(dv * yh, axis=0, keepdims=True)
        dn = dv * g_ref[...]
        dyb = (r * (dn - yh * jnp.mean(dn * yh, axis=-1, keepdims=True))).astype(_CDT)
        dy_ref[...] = dyb
        dcg_ref[...] = _dot_nt(dyb, w_ref[0:DGLA, :]).astype(_CDT)
        dca_ref[...] = _dot_nt(dyb, w_ref[DGLA:, :]).astype(_CDT)
        _ride_wait(cps, s // tm)

    extra = [] if swap is None else list(swap)
    return pl.pallas_call(
        body, name="post_bwd", grid=(s // tm,),
        in_specs=[pl.BlockSpec((tm, D), lambda i: (i, 0)), pl.BlockSpec((tm, D), lambda i: (i, 0)),
                  pl.BlockSpec((1, D), lambda i: (0, 0)), pl.BlockSpec((D, D), lambda i: (0, 0))] + [ANY] * len(extra),
        out_specs=[pl.BlockSpec((tm, D), lambda i: (i, 0)), pl.BlockSpec((tm, DGLA), lambda i: (i, 0)),
                   pl.BlockSpec((tm, DATT), lambda i: (i, 0)), pl.BlockSpec((1, D), lambda i: (0, 0))] + [ANY] * len(extra),
        out_shape=[jax.ShapeDtypeStruct((s, D), _CDT), jax.ShapeDtypeStruct((s, DGLA), _CDT),
                   jax.ShapeDtypeStruct((s, DATT), _CDT), jax.ShapeDtypeStruct((1, D), F32)]
        + [jax.ShapeDtypeStruct(t.shape[1:], t.dtype) for t in extra],
        scratch_shapes=[pltpu.SemaphoreType.DMA((2,)), pltpu.SemaphoreType.DMA((2,))] if extra else [],
        compiler_params=_params(("arbitrary",)),
    )(dout, y, gpost, wout, *extra)


def _matmul_tn(a, b, out_dtype, tm, tn, name):
    k, m = a.shape
    n = b.shape[1]

    def body(a_ref, b_ref, o_ref):
        o_ref[...] = _dot_tn(a_ref[...], b_ref[...]).astype(out_dtype)

    return pl.pallas_call(
        body, name=name, grid=(m // tm, n // tn),
        in_specs=[pl.BlockSpec((k, tm), lambda i, j: (0, i)), pl.BlockSpec((k, tn), lambda i, j: (0, j))],
        out_specs=pl.BlockSpec((tm, tn), lambda i, j: (i, j)),
        out_shape=jax.ShapeDtypeStruct((m, n), out_dtype),
        compiler_params=_params(("parallel", "parallel")),
    )(a, b)


def _att_bwd(z, oraw, lse, dca, rbx, gatt, send=None):
    s = z.shape[0]
    nb = s // QB

    def body(*refs):
        za_ref, o_ref, lse_ref, dc_ref, rb_ref, g_ref = refs[:6]
        dz_ref, dg_ref, db_ref, kp, vp, dkp, dvp, bias_s, t_s, dg_s = _ride_refs(refs[6:], send, 3)[1]
        cps = _ride_start(refs[6:], send, 3, 0)
        for p in range(HP):
            z0 = p * AW
            kp[p, 0:PADK, :] = jnp.zeros((PADK, AHD), _CDT)
            vp[p, 0:PADK, :] = jnp.zeros((PADK, AHD), _CDT)
            kp[p, PADK:, :] = za_ref[:, z0 + AHD:z0 + 2 * AHD]
            vp[p, PADK:, :] = za_ref[:, z0 + 2 * AHD:z0 + 3 * AHD]
            bias_s[p] = jnp.where(_band_static_mask(), _band_bias(rb_ref[p]), NEG)
        dkp[...] = jnp.zeros_like(dkp)
        dvp[...] = jnp.zeros_like(dvp)
        t_s[...] = jnp.zeros_like(t_s)
        dg_s[...] = jnp.zeros_like(dg_s)

        def step(b, carry):
            r0 = pl.multiple_of(b * QB, QB)
            rows = pl.ds(r0, QB)
            band = pl.ds(r0, BANDW)
            live = lax.broadcasted_iota(jnp.int32, (QB, BANDW), 1) >= PADK - r0
            for p in range(HP):
                z0 = p * AW
                cols = slice(p * AHD, (p + 1) * AHD)
                o = o_ref[rows, cols].astype(F32)
                do, dgate, dg = _norm_gate_bwd(o, g_ref[:, cols], za_ref[rows, z0 + 3 * AHD:z0 + AW].astype(F32),
                                               dc_ref[rows, cols].astype(F32))
                dg_s[:, cols] += dg
                q = za_ref[rows, z0:z0 + AHD]
                kb = kp[p, band, :]
                sc = _dot_nt(q, kb) * (AHD ** -0.5) + bias_s[p]
                sc = jnp.where(live, sc, NEG)
                pr = jnp.exp(sc - jnp.max(lse_ref[rows, cols], axis=-1, keepdims=True))
                dob = do.astype(_CDT)
                dp = _dot_nt(dob, vp[p, band, :])
                ds = pr * (dp - jnp.sum(do * o, axis=-1, keepdims=True))
                t_s[p] += ds
                dsb = (ds * (AHD ** -0.5)).astype(_CDT)
                dz_ref[rows, z0:z0 + AHD] = _dot(dsb, kb).astype(_CDT)
                dz_ref[rows, z0 + 3 * AHD:z0 + AW] = dgate.astype(_CDT)
                dkp[p, band, :] += _dot_tn(dsb, q)
                dvp[p, band, :] += _dot_tn(pr.astype(_CDT), dob)
            return carry

        lax.fori_loop(0, nb, step, 0, unroll=2)
        for p in range(HP):
            z0 = p * AW
            dz_ref[:, z0 + AHD:z0 + 2 * AHD] = dkp[p, PADK:, :].astype(_CDT)
            dz_ref[:, z0 + 2 * AHD:z0 + 3 * AHD] = dvp[p, PADK:, :].astype(_CDT)
            db_ref[p] = _fold_bias_grad(t_s[p])
        dg_ref[...] = dg_s[...]
        _ride_wait(cps, AH // HP)

    ride = _ride_specs(send, 0)
    return pl.pallas_call(
        body, name="att_bwd", grid=(AH // HP,),
        in_specs=[pl.BlockSpec((s, HP * AW), lambda h: (0, ZG // (HP * AW) + h)), pl.BlockSpec((s, HP * AHD), lambda h: (0, h)),
                  pl.BlockSpec((s, HP * AHD), lambda h: (0, h)), pl.BlockSpec((s, HP * AHD), lambda h: (0, h)),
                  pl.BlockSpec((HP, 1, 256), lambda h: (h, 0, 0)), pl.BlockSpec((1, HP * AHD), lambda h: (0, h))] + ride.in_specs,
        out_specs=[pl.BlockSpec((s, HP * AW), lambda h: (0, h)), pl.BlockSpec((1, HP * AHD), lambda h: (0, h)),
                   pl.BlockSpec((HP, 1, 256), lambda h: (h, 0, 0))] + ride.out_specs,
        out_shape=[jax.ShapeDtypeStruct((s, ZA), _CDT), jax.ShapeDtypeStruct((1, DATT), F32),
                   jax.ShapeDtypeStruct((AH, 1, 256), F32)] + ride.out_shape,
        scratch_shapes=[pltpu.VMEM((HP, s + PADK, AHD), _CDT), pltpu.VMEM((HP, s + PADK, AHD), _CDT),
                        pltpu.VMEM((HP, s + PADK, AHD), F32), pltpu.VMEM((HP, s + PADK, AHD), F32),
                        pltpu.VMEM((HP, QB, BANDW), F32), pltpu.VMEM((HP, QB, BANDW), F32), pltpu.VMEM((1, HP * AHD), F32)] + ride.scratch,
        compiler_params=_params(("arbitrary",)),
    )(z, oraw, lse, dca, rbx, gatt, *ride.operands)


def _gla_bwd(z, zga, wa, ba, ggla, dcg, send=None):
    s = z.shape[0]
    nc = s // CHUNK

    def body(*refs):
        zg_ref, zga_ref, wa_ref, ba_ref, g_ref, dc_ref = refs[:6]
        (dz_ref, dga_ref, dwa_ref, dba_ref, dg_ref,
         la_s, om_s, sall, dpre_s, c_s, dga_s, dg_s) = _ride_refs(refs[6:], send, 5)[1]
        cps = _ride_start(refs[6:], send, 5, 1)
        h = pl.program_id(0)
        pre = _dot(zga_ref[...], wa_ref[...]) + ba_ref[...]
        la_s[...] = _log_sigmoid(pre) * (1.0 / TAU)
        om_s[...] = (1.0 - _sigmoid(pre)) * (1.0 / TAU)
        c_s[...] = jnp.zeros_like(c_s)
        dg_s[...] = jnp.zeros_like(dg_s)
        tri = _tri(False)
        tri_strict = _tri(True)

        def decay(rows, p):
            la = la_s[rows, p * GDK:(p + 1) * GDK]
            lend = jnp.sum(la, axis=0, keepdims=True)
            return jnp.exp(lend - _exact_dot(tri, la)), jnp.exp(lend)

        def fwd(n, sts):
            rows = pl.ds(pl.multiple_of(n * CHUNK, CHUNK), CHUNK)
            out = []
            for p in range(HP):
                z0 = p * GW
                dec, a = decay(rows, p)
                kdec = (zg_ref[rows, z0 + GDK:z0 + 2 * GDK].astype(F32) * dec).astype(_CDT)
                stn = a * sts[p] + _dot_tn(zg_ref[rows, z0 + 2 * GDK:z0 + 2 * GDK + GDV], kdec)
                sall[p, n] = stn
                out.append(stn)
            return tuple(out)

        lax.fori_loop(0, nc, fwd, tuple(jnp.zeros((GDV, GDK), F32) for _ in range(HP)), unroll=4)

        def bwd(i, carry):
            n = nc - 1 - i
            rows = pl.ds(pl.multiple_of(n * CHUNK, CHUNK), CHUNK)
            for p in range(HP):
                z0 = p * GW
                kc = slice(p * GDK, (p + 1) * GDK)
                vc = slice(p * GDV, (p + 1) * GDV)
                dec, a = decay(rows, p)
                kdec = zg_ref[rows, z0 + GDK:z0 + 2 * GDK].astype(F32) * dec
                kdb = kdec.astype(_CDT)
                v = zg_ref[rows, z0 + 2 * GDK:z0 + 2 * GDK + GDV]
                qs = (zg_ref[rows, z0:z0 + GDK].astype(F32) * (GDK ** -0.5)).astype(_CDT)
                stb = sall[p, n].astype(_CDT)
                st_prev = sall[p, jnp.maximum(n - 1, 0)] * jnp.where(n > 0, 1.0, 0.0)
                o = _dot_nt(qs, stb)
                do, dgate, dg = _norm_gate_bwd(o, g_ref[:, vc], zg_ref[rows, z0 + 2 * GDK + GDV:z0 + GW].astype(F32),
                                               dc_ref[rows, vc].astype(F32))
                dg_s[:, vc] += dg
                dob = do.astype(_CDT)
                gt = _dot_tn(dob, qs) + c_s[p]
                gtb = gt.astype(_CDT)
                da = jnp.sum(gt * st_prev, axis=0, keepdims=True)
                dkdec = _dot(v, gtb)
                dla = _exact_dot(tri_strict, dkdec * kdec) + da * a
                dpre_s[rows, kc] = dla * om_s[rows, kc]
                dz_ref[rows, z0:z0 + GDK] = (_dot(dob, stb) * (GDK ** -0.5)).astype(_CDT)
                dz_ref[rows, z0 + GDK:z0 + 2 * GDK] = (dkdec * dec).astype(_CDT)
                dz_ref[rows, z0 + 2 * GDK:z0 + 2 * GDK + GDV] = _dot_nt(kdb, gtb).astype(_CDT)
                dz_ref[rows, z0 + 2 * GDK + GDV:z0 + GW] = dgate.astype(_CDT)
                c_s[p] = a * gt
            return carry

        lax.fori_loop(0, nc, bwd, 0)
        dpre = dpre_s[...]
        dpb = dpre.astype(_CDT)
        dg_ref[...] = dg_s[...]
        dba_ref[...] = jnp.sum(dpre, axis=0, keepdims=True)
        dwa_ref[...] = _dot_tn(zga_ref[...], dpb)
        part = _dot_nt(dpb, wa_ref[...])

        @pl.when(h == 0)
        def _():
            dga_s[...] = part

        @pl.when(h > 0)
        def _():
            dga_s[...] += part

        @pl.when(h == GH // HP - 1)
        def _():
            dga_ref[...] = dga_s[...].astype(_CDT)

        _ride_wait(cps, GH // HP)

    ride = _ride_specs(send, 1)
    return pl.pallas_call(
        body, name="gla_bwd", grid=(GH // HP,),
        in_specs=[pl.BlockSpec((s, HP * GW), lambda h: (0, h)), pl.BlockSpec((s, GAP), lambda h: (0, 0)),
                  pl.BlockSpec((GAP, HP * GDK), lambda h: (0, h)), pl.BlockSpec((1, HP * GDK), lambda h: (0, h)),
                  pl.BlockSpec((1, HP * GDV), lambda h: (0, h)), pl.BlockSpec((s, HP * GDV), lambda h: (0, h))] + ride.in_specs,
        out_specs=[pl.BlockSpec((s, HP * GW), lambda h: (0, h)), pl.BlockSpec((s, GAP), lambda h: (0, 0)),
                   pl.BlockSpec((GAP, HP * GDK), lambda h: (0, h)), pl.BlockSpec((1, HP * GDK), lambda h: (0, h)),
                   pl.BlockSpec((1, HP * GDV), lambda h: (0, h))] + ride.out_specs,
        out_shape=[jax.ShapeDtypeStruct((s, ZG), _CDT), jax.ShapeDtypeStruct((s, GAP), _CDT),
                   jax.ShapeDtypeStruct((GAP, GH * GDK), F32), jax.ShapeDtypeStruct((1, GH * GDK), F32),
                   jax.ShapeDtypeStruct((1, DGLA), F32)] + ride.out_shape,
        scratch_shapes=[pltpu.VMEM((s, HP * GDK), F32), pltpu.VMEM((s, HP * GDK), F32), pltpu.VMEM((HP, nc, GDV, GDK), F32),
                        pltpu.VMEM((s, HP * GDK), F32), pltpu.VMEM((HP, GDV, GDK), F32), pltpu.VMEM((s, GAP), F32),
                        pltpu.VMEM((1, HP * GDV), F32)] + ride.scratch,
        compiler_params=_params(("arbitrary",)),
    )(z, zga, wa, ba, ggla, dcg, *ride.operands)


def _dh(dzg, dza, dga, wm, wga, x, dout, gpre, send=None, tm=512, tk=1024):
    s = x.shape[0]
    nkg, nk = ZG // tk, ZM // tk

    def body(*refs):
        dzg_ref, dza_ref, dga_ref, wm_ref, wga_ref, x_ref, d_ref, g_ref = refs[:8]
        dx_ref, dg_ref, acc = _ride_refs(refs[8:], send, 2)[1]
        i, k = pl.program_id(0), pl.program_id(1)
        cps = _ride_start(refs[8:], send, 2, 1, (i == 0) & (k == 0))

        @pl.when((i == 0) & (k == 0))
        def _():
            dg_ref[...] = jnp.zeros_like(dg_ref)

        @pl.when(k == 0)
        def _():
            acc[...] = _dot(dga_ref[...], wga_ref[...])

        @pl.when(k < nkg)
        def _():
            acc[...] += _dot(dzg_ref[...], wm_ref[...])

        @pl.when(k >= nkg)
        def _():
            acc[...] += _dot(dza_ref[...], wm_ref[...])

        @pl.when(k == nk - 1)
        def _():
            xv = x_ref[...]
            r = _rms_rows(xv)
            xh = xv * r
            dh = acc[...]
            dg_ref[...] += jnp.sum(dh * xh, axis=0, keepdims=True)
            dn = dh * g_ref[...]
            dx_ref[...] = d_ref[...] + r * (dn - xh * jnp.mean(dn * xh, axis=-1, keepdims=True))

        _ride_wait(cps, 0, (i == s // tm - 1) & (k == nk - 1))

    ride = _ride_specs(send, 1)
    return pl.pallas_call(
        body, name="dh", grid=(s // tm, nk),
        in_specs=[pl.BlockSpec((tm, tk), lambda i, k: (i, jnp.minimum(k, nkg - 1))),
                  pl.BlockSpec((tm, tk), lambda i, k: (i, jnp.maximum(k - nkg, 0))),
                  pl.BlockSpec((tm, GAP), lambda i, k: (i, 0)), pl.BlockSpec((tk, D), lambda i, k: (k, 0)),
                  pl.BlockSpec((GAP, D), lambda i, k: (0, 0)), pl.BlockSpec((tm, D), lambda i, k: (i, 0)),
                  pl.BlockSpec((tm, D), lambda i, k: (i, 0)), pl.BlockSpec((1, D), lambda i, k: (0, 0))] + ride.in_specs,
        out_specs=[pl.BlockSpec((tm, D), lambda i, k: (i, 0)), pl.BlockSpec((1, D), lambda i, k: (0, 0))] + ride.out_specs,
        out_shape=[jax.ShapeDtypeStruct((s, D), F32), jax.ShapeDtypeStruct((1, D), F32)] + ride.out_shape,
        scratch_shapes=[pltpu.VMEM((tm, D), F32)] + ride.scratch,
        compiler_params=_params(("arbitrary", "arbitrary")),
    )(dzg, dza, dga, wm, wga, x, dout, gpre, *ride.operands)


def _adam(w, g, m, v, tr, name):
    rws, cols = w.shape

    def body(w_ref, g_ref, m_ref, v_ref, d_ref, mo_ref, vo_ref):
        gv = g_ref[...]
        mn = ADAM_B1 * m_ref[...] + (1.0 - ADAM_B1) * gv
        vn = ADAM_B2 * v_ref[...] + (1.0 - ADAM_B2) * (gv * gv)
        mh = mn / (1.0 - ADAM_B1 ** ADAM_STEP)
        vh = vn / (1.0 - ADAM_B2 ** ADAM_STEP)
        d_ref[...] = -ADAM_LR * (mh / (jnp.sqrt(vh) + ADAM_EPS) + ADAM_WD * w_ref[...])
        mo_ref[...] = mn
        vo_ref[...] = vn

    spec = pl.BlockSpec((tr, cols), lambda i: (i, 0))
    return pl.pallas_call(
        body, name=name, grid=(rws // tr,), in_specs=[spec] * 4, out_specs=[spec] * 3,
        out_shape=[jax.ShapeDtypeStruct((rws, cols), F32)] * 3,
        compiler_params=_params(("parallel",)),
    )(w, g, m, v)


def _adam_rows(c_idx, w, g_own, g_other, m, v, tj, name):
    rows, nl = w.shape[:2]
    per_half = g_own.shape[1] // tj

    def body(c_ref, w_ref, go_ref, gx_ref, m_ref, v_ref, g_ref, d_ref, mo_ref, vo_ref):
        mine = pl.program_id(0) // per_half == c_ref[0]
        for l in range(nl):
            gv = jnp.where(mine, go_ref[l], gx_ref[l])
            mn = ADAM_B1 * m_ref[:, l, :] + (1.0 - ADAM_B1) * gv
            vn = ADAM_B2 * v_ref[:, l, :] + (1.0 - ADAM_B2) * (gv * gv)
            mh = mn / (1.0 - ADAM_B1 ** ADAM_STEP)
            vh = vn / (1.0 - ADAM_B2 ** ADAM_STEP)
            g_ref[:, l, :] = gv
            d_ref[:, l, :] = -ADAM_LR * (mh / (jnp.sqrt(vh) + ADAM_EPS) + ADAM_WD * w_ref[:, l, :])
            mo_ref[:, l, :] = mn
            vo_ref[:, l, :] = vn

    full = pl.BlockSpec((tj,) + w.shape[1:], lambda i, c_ref: (i, 0, 0))
    own = pl.BlockSpec((nl, tj, D), lambda i, c_ref: (0, jnp.where(i // per_half == c_ref[0], i % per_half, 0), 0))
    other = pl.BlockSpec((nl, tj, D), lambda i, c_ref: (0, jnp.where(i // per_half == c_ref[0], 0, i % per_half), 0))
    return pl.pallas_call(
        body, name=name,
        grid_spec=pltpu.PrefetchScalarGridSpec(num_scalar_prefetch=1, grid=(pl.cdiv(rows, tj),),
                                               in_specs=[full, own, other, full, full], out_specs=[full] * 4),
        out_shape=[jax.ShapeDtypeStruct(w.shape, F32)] * 4,
        compiler_params=_params(("parallel",)),
    )(c_idx, w, g_own, g_other, m, v)


def _gather_first(src_in, src_out, wa, buf_in, buf_out, wa_all):
    def variant(x, y, c, src_in_ref, src_out_ref, wa_ref, _b0, _b1, _b2, bin_ref, bout_ref, wa_ref_all, send_sems, recv_sems, wa_send, wa_recv):
        direct = []
        for part in range(NDIRECT):
            direct += _gather_copies(part, x, y, c, src_in_ref, src_out_ref, bin_ref, bout_ref, send_sems.at[part], recv_sems.at[part])
        for k, (px, py) in enumerate([(1 - x, y), (x, 1 - y), (1 - x, 1 - y)]):
            direct.append(pltpu.make_async_remote_copy(src_ref=wa_ref, dst_ref=wa_ref_all.at[2 * x + y], send_sem=wa_send.at[k],
                                                       recv_sem=wa_recv.at[k], device_id=(px, py, c), device_id_type=MESH))
        for cpy in direct:
            cpy.start()
        for cpy in direct:
            cpy.wait()
        passed = _gather_copies(NDIRECT, x, y, c, src_in_ref, src_out_ref, bin_ref, bout_ref, send_sems.at[NDIRECT], recv_sems.at[NDIRECT])
        for cpy in passed:
            cpy.start()
        for cpy in passed:
            cpy.wait()

    def body(*refs):
        _variants(lambda x, y, c: variant(x, y, c, *refs))

    return pl.pallas_call(
        body, name="gather_first", in_specs=[ANY] * 6, out_specs=[ANY] * 3, input_output_aliases={3: 0, 4: 1, 5: 2},
        out_shape=[jax.ShapeDtypeStruct(t.shape, t.dtype) for t in (buf_in, buf_out, wa_all)],
        scratch_shapes=[pltpu.SemaphoreType.DMA((NDIRECT + 1, 2, 3)), pltpu.SemaphoreType.DMA((NDIRECT + 1, 2, 3)),
                        pltpu.SemaphoreType.DMA((3,)), pltpu.SemaphoreType.DMA((3,))],
    )(src_in, src_out, wa, buf_in, buf_out, wa_all)


def _swap_halves(gin2, gout2):
    def body(gin_ref, gout_ref, rin, rout, send_sems, recv_sems):
        x, y, c, _ = _place()
        sib = (x, y, 1 - c)
        cps = [pltpu.make_async_remote_copy(src_ref=src.at[1 - c], dst_ref=dst, send_sem=send_sems.at[a],
                                            recv_sem=recv_sems.at[a], device_id=sib, device_id_type=MESH)
               for a, (src, dst) in enumerate([(gin_ref, rin), (gout_ref, rout)])]
        for cpy in cps:
            cpy.start()
        for cpy in cps:
            cpy.wait()

    return pl.pallas_call(
        body, name="swap_halves", in_specs=[ANY, ANY], out_specs=[ANY, ANY],
        out_shape=[jax.ShapeDtypeStruct(gin2.shape[1:], gin2.dtype), jax.ShapeDtypeStruct(gout2.shape[1:], gout2.dtype)],
        scratch_shapes=[pltpu.SemaphoreType.DMA((2,)), pltpu.SemaphoreType.DMA((2,))],
    )(gin2, gout2)


def _add_halves(c_idx, g2, r, tr, name):
    rows, cols = r.shape

    def body(c_ref, g_ref, r_ref, o_ref):
        o_ref[...] = (g_ref[0].astype(F32) + r_ref[...].astype(F32)).astype(_XDT)

    return pl.pallas_call(
        body, name=name,
        grid_spec=pltpu.PrefetchScalarGridSpec(
            num_scalar_prefetch=1, grid=(rows // tr,),
            in_specs=[pl.BlockSpec((1, tr, cols), lambda i, c_ref: (c_ref[0], i, 0)),
                      pl.BlockSpec((tr, cols), lambda i, c_ref: (i, 0))],
            out_specs=pl.BlockSpec((tr, cols), lambda i, c_ref: (i, 0))),
        out_shape=jax.ShapeDtypeStruct((rows, cols), _XDT),
        compiler_params=_params(("parallel",)),
    )(c_idx, g2, r)


def _send_to_owners(pin, pout, parts):
    def body(pin_ref, pout_ref, *refs):
        send_sems, recv_sems = refs[-2:]
        cps = []
        for n, part in enumerate(parts):
            cps += _owner_copies(part, pin_ref, pout_ref, refs[2 * n], refs[2 * n + 1], send_sems.at[n], recv_sems.at[n])
        for cpy in cps:
            cpy.start()
        for cpy in cps:
            cpy.wait()

    shapes = [jax.ShapeDtypeStruct((3, _part_rows(t.shape[2], split, part)[1], D), t.dtype)
              for part in parts for t, split in zip((pin, pout), (SPLIT_IN, SPLIT_OUT))]
    return pl.pallas_call(
        body, name="send_to_owners", in_specs=[ANY, ANY], out_specs=[ANY] * len(shapes), out_shape=shapes,
        scratch_shapes=[pltpu.SemaphoreType.DMA((len(parts), 2, 3)), pltpu.SemaphoreType.DMA((len(parts), 2, 3))],
    )(pin, pout)


def _add_chips(chip_idx, p, ra, rb, tr, name):
    rows = p.shape[2]
    na = ra.shape[1] // tr

    def body(c_ref, p_ref, ra_ref, rb_ref, o_ref):
        r = jnp.where(pl.program_id(0) < na, ra_ref[...], rb_ref[...]).astype(F32)
        o_ref[0] = ((p_ref[0, 0].astype(F32) + r[0]) + r[1]) + r[2]

    return pl.pallas_call(
        body, name=name,
        grid_spec=pltpu.PrefetchScalarGridSpec(
            num_scalar_prefetch=1, grid=(rows // tr,),
            in_specs=[pl.BlockSpec((1, 1, tr, D), lambda i, c_ref: (0, c_ref[0], i, 0)),
                      pl.BlockSpec((3, tr, D), lambda i, c_ref: (0, jnp.minimum(i, na - 1), 0)),
                      pl.BlockSpec((3, tr, D), lambda i, c_ref: (0, jnp.maximum(i - na, 0), 0))],
            out_specs=pl.BlockSpec((1, tr, D), lambda i, c_ref: (0, i, 0))),
        out_shape=jax.ShapeDtypeStruct((1, rows, D), F32),
        compiler_params=_params(("parallel",)),
    )(chip_idx, p, ra, rb)


def _exchange_halves(arrs):
    n = len(arrs)

    def body(*refs):
        x, y, c, _ = _place()
        cps = [pltpu.make_async_remote_copy(src_ref=refs[a], dst_ref=refs[n + a], send_sem=refs[2 * n].at[a], recv_sem=refs[2 * n + 1].at[a],
                                            device_id=(x, y, 1 - c), device_id_type=MESH) for a in range(n)]
        for cpy in cps:
            cpy.start()
        for cpy in cps:
            cpy.wait()

    return pl.pallas_call(
        body, name="exchange_halves", in_specs=[ANY] * n, out_specs=[ANY] * n,
        out_shape=[jax.ShapeDtypeStruct(t.shape, t.dtype) for t in arrs],
        scratch_shapes=[pltpu.SemaphoreType.DMA((n,)), pltpu.SemaphoreType.DMA((n,))],
    )(*arrs)


def _adam_halves(c_idx, w, g_own, g_other, m, v, tr, name):
    nl, _, rows, cols = w.shape

    def body(c_ref, w_ref, go_ref, gx_ref, m_ref, v_ref, g_ref, d_ref, mo_ref, vo_ref):
        gv = jnp.where(pl.program_id(1) == c_ref[0], go_ref[0], gx_ref[0])
        mn = ADAM_B1 * m_ref[0, 0] + (1.0 - ADAM_B1) * gv
        vn = ADAM_B2 * v_ref[0, 0] + (1.0 - ADAM_B2) * (gv * gv)
        mh = mn / (1.0 - ADAM_B1 ** ADAM_STEP)
        vh = vn / (1.0 - ADAM_B2 ** ADAM_STEP)
        g_ref[0, 0] = gv
        d_ref[0, 0] = -ADAM_LR * (mh / (jnp.sqrt(vh) + ADAM_EPS) + ADAM_WD * w_ref[0, 0])
        mo_ref[0, 0] = mn
        vo_ref[0, 0] = vn

    full = pl.BlockSpec((1, 1, tr, cols), lambda l, hh, i, c_ref: (l, hh, i, 0))
    own = pl.BlockSpec((1, tr, cols), lambda l, hh, i, c_ref: (l, jnp.where(hh == c_ref[0], i, 0), 0))
    other = pl.BlockSpec((1, tr, cols), lambda l, hh, i, c_ref: (l, jnp.where(hh == c_ref[0], 0, i), 0))
    return pl.pallas_call(
        body, name=name,
        grid_spec=pltpu.PrefetchScalarGridSpec(
            num_scalar_prefetch=1, grid=(nl, 2, rows // tr),
            in_specs=[full, own, other, full, full], out_specs=[full] * 4),
        out_shape=[jax.ShapeDtypeStruct(w.shape, F32)] * 4,
        compiler_params=_params(("parallel", "parallel", "parallel")),
    )(c_idx, w, g_own, g_other, m, v)


def _allreduce_small(sg):
    rows = sg.shape[0]
    vm = pl.BlockSpec(memory_space=pltpu.VMEM)

    def body(sg_ref, tot_ref, all_ref, send_sems, recv_sems):
        x, y, c, _ = _place()
        me = 4 * x + 2 * y + c
        all_ref[me] = sg_ref[...]
        cps = []
        for mask in range(1, 8):
            to = (1 - x if mask & 4 else x, 1 - y if mask & 2 else y, 1 - c if mask & 1 else c)
            cps.append(pltpu.make_async_remote_copy(src_ref=sg_ref, dst_ref=all_ref.at[me], send_sem=send_sems.at[mask - 1],
                                                    recv_sem=recv_sems.at[mask - 1], device_id=to, device_id_type=MESH))
        for cpy in cps:
            cpy.start()
        for cpy in cps:
            cpy.wait()
        acc = all_ref[0]
        for d in range(1, 8):
            acc = acc + all_ref[d]
        tot_ref[...] = acc

    return pl.pallas_call(
        body, name="allreduce_small", in_specs=[vm], out_specs=[vm, vm],
        out_shape=[jax.ShapeDtypeStruct((rows, 128), F32), jax.ShapeDtypeStruct((8, rows, 128), F32)],
        scratch_shapes=[pltpu.SemaphoreType.DMA((7,)), pltpu.SemaphoreType.DMA((7,))],
        compiler_params=_params(),
    )(sg)[0]


_CUTS = [0, 512, 1024, 2048, 3072, 3088, 4112, 5136, 6160, 7184]


def _rows_to_internal(w):
    tail = w.shape[1:]
    gq, gk, gv, gg, ga, aq, ak, av, ag = [w[_CUTS[i]:_CUTS[i + 1]] for i in range(9)]
    g = jnp.concatenate([gq.reshape((GH, GDK) + tail), gk.reshape((GH, GDK) + tail),
                         gv.reshape((GH, GDV) + tail), gg.reshape((GH, GDV) + tail)], axis=1).reshape((ZG,) + tail)
    a = jnp.concatenate([t.reshape((AH, AHD) + tail) for t in (aq, ak, av, ag)], axis=1).reshape((ZA,) + tail)
    pad = [(0, GAP - RANK)] + [(0, 0)] * len(tail)
    return jnp.concatenate([g, a], axis=0), jnp.pad(ga, pad)


def _rows_from_internal(g, a, ga):
    tail = g.shape[1:]
    g = g.reshape((GH, GW) + tail)
    a = a.reshape((AH, AW) + tail)
    parts = [g[:, 0:GDK], g[:, GDK:2 * GDK], g[:, 2 * GDK:2 * GDK + GDV], g[:, 2 * GDK + GDV:GW]]
    parts = [t.reshape((-1,) + tail) for t in parts] + [ga[0:RANK]]
    parts += [a[:, i * AHD:(i + 1) * AHD].reshape((-1,) + tail) for i in range(4)]
    return jnp.concatenate(parts, axis=0)


def _pack_rows(parts):
    rows = []
    for t in parts:
        flat = t.reshape(-1)
        rows.append(jnp.pad(flat, (0, (-flat.shape[0]) % 128)).reshape(-1, 128))
    buf = jnp.concatenate(rows, axis=0)
    return jnp.pad(buf, ((0, (-buf.shape[0]) % 8), (0, 0)))


def _unpack_rows(buf, shapes):
    out, r = [], 0
    for shp in shapes:
        n = 1
        for d in shp:
            n *= d
        nr = -(-n // 128)
        out.append(buf[r:r + nr].reshape(-1)[:n].reshape(shp))
        r += nr
    return out


def _layer_fwd(x, wm, wga, wout, gpre, gpost, wa, ba, ggla, gatt, rbx, ride=None):
    z, zga, h, *bufs = _inproj(x, gpre, wm, wga, ride)
    ride = None if ride is None else (ride[0], ride[1], *bufs)
    cg, *bufs = _gla_fwd(z, zga, wa, ba, ggla, ride)
    ride = None if ride is None else (ride[0], ride[1], *bufs)
    ca, oraw, lse, *bufs = _att_fwd(z, rbx, gatt, ride)
    ride = None if ride is None else (ride[0], ride[1], *bufs)
    y, xo, *bufs = _outproj(cg, ca, wout, x, gpost, ride)
    return xo, (x, z, zga, h, cg, ca, oraw, lse, y), bufs


def _layer_bwd(dout, saved, wm, wga, wout, gpre, gpost, wa, ba, ggla, gatt, rbx, swap=None, finish=None, reduce_now=None):
    x, z, zga, h, cg, ca, oraw, lse, y = saved
    dy, dcg, dca, dgpost, *swapped = _post_bwd(dout, y, gpost, wout, swap)
    send = None if swap is None else finish(swap, swapped)
    dwout = jnp.concatenate([_matmul_tn(cg, dy, _XDT, 512, 1024, "dwout_gla"),
                             _matmul_tn(ca, dy, _XDT, 512, 1024, "dwout_att")], axis=0)
    dza, dgatt, dbx, *got_a = _att_bwd(z, oraw, lse, dca, rbx, gatt, send)
    dzg, dga, dwa, dba, dggla, *got_b = _gla_bwd(z, zga, wa, ba, ggla, dcg, send)
    dwin = (_matmul_tn(dzg, h, _XDT, 512, 1024, "dwin_gla"), _matmul_tn(dza, h, _XDT, 512, 1024, "dwin_att"),
            _matmul_tn(dga, h, _XDT, GAP, 1024, "dwin_gate"))
    own = None if reduce_now is None else reduce_now(dwin, dwout)
    dx, dgpre, *got_own = _dh(dzg, dza, dga, wm, wga, x, dout, gpre, own)
    drb = jnp.concatenate([jnp.zeros((AH, 1), F32), dbx[:, 0, ::-1]], axis=1)
    return dx, dwin, dwout, (dgpre[0], dgpost[0], dwa[0:RANK], dba[0], dggla[0], dgatt[0], drb), send, got_a + got_b, own, got_own


def _rel_rows(rb):
    return rb[:, :0:-1][:, None, :]


def kernel(x, w_in, w_out, g_pre, g_post, w_alpha, b_alpha, g_gla, g_att, rel_bias, loss_target, m_w_in, m_w_out, m_g_pre, m_g_post, m_w_alpha, m_b_alpha, m_g_gla, m_g_att, m_rel_bias, v_w_in, v_w_out, v_g_pre, v_g_post, v_w_alpha, v_b_alpha, v_g_gla, v_g_att, v_rel_bias):
    nl = w_in.shape[0]
    ax, ay, ac = lax.axis_index("x"), lax.axis_index("y"), lax.axis_index("c")
    chip = 2 * ax + ay
    c_idx = jnp.reshape(ac, (1,)).astype(jnp.int32)
    chip_idx = jnp.reshape(chip, (1,)).astype(jnp.int32)

    phase = [SHARD * i % 16 for i in range(NCHIP)]
    wt_rows = jnp.transpose(w_in, (0, 2, 1)).astype(_CDT)
    at_phase = [functools.partial(jnp.pad, wt_rows, ((0, 0), (p, WSLOT - SHARD - p), (0, 0))) for p in phase]
    wt_src = lax.switch(chip, at_phase).reshape(nl, 2, WSLOT // 2, D)
    wout_src = w_out.astype(_CDT).reshape(nl, 2, D // NCHIP // 2, D)

    def with_own(own):
        start = [chip] + [0] * own.ndim
        return lax.dynamic_update_slice(lax.empty((NCHIP,) + own.shape, own.dtype), own[None], start)

    def gather_operands(l):
        return wt_src[l], wout_src[l], with_own(wt_src[l]), with_own(wout_src[l])

    def layer_weights(bufs):
        wt4 = bufs[0].reshape(NCHIP, WSLOT, D)
        wref = jnp.concatenate([wt4[i, phase[i]:phase[i] + SHARD] for i in range(NCHIP)])
        return _rows_to_internal(wref) + (bufs[1].reshape(D, D),)

    first = gather_operands(0)
    bin0, bout0, wa_all = _gather_first(first[0], first[1], w_alpha, first[2], first[3], with_own(w_alpha))
    wa_full = jnp.transpose(wa_all, (1, 2, 0, 3)).reshape(nl, RANK, GH * GDK)
    wa_pad = jnp.pad(wa_full, ((0, 0), (0, GAP - RANK), (0, 0))).astype(_CDT)
    rbx = [_rel_rows(rel_bias[l]) for l in range(nl)]

    def weights(l):
        return big[l] + (g_pre[l][None], g_post[l][None], wa_pad[l], b_alpha[l][None], g_gla[l][None], g_att[l][None], rbx[l])

    h = x[0]
    saved, big = [], [None] * nl
    big[0] = layer_weights((bin0, bout0))
    for l in range(nl):
        h, sv, bufs = _layer_fwd(h, *weights(l), ride=gather_operands(l + 1) if l + 1 < nl else None)
        saved.append(sv)
        if l + 1 < nl:
            big[l + 1] = layer_weights(bufs)
    dout, loss_part = _loss_grad(h, loss_target[0])

    small, hin, hout = [None] * nl, [None] * nl, [None] * nl
    hw = D // NCHIP // 2

    def reduce_owner(sent, got):
        rin_a, rout_a, rin_b, rout_b = got
        return (_add_chips(chip_idx, sent[0], rin_a, rin_b, 48, "add_chips_in"),
                _add_chips(chip_idx, sent[1], rout_a, rout_b, 128, "add_chips_out"))

    def slab_halves(dwin, dwout):
        gt = jnp.pad(_rows_from_internal(*dwin), ((0, NCHIP * SHARD - SHARD + SLAB - DIN), (0, 0)))
        slabs = jnp.stack([gt[SHARD * i:SHARD * i + SLAB] for i in range(NCHIP)])
        return (jnp.transpose(slabs.reshape(NCHIP, 2, HSLAB, D), (1, 0, 2, 3)).reshape(2, NCHIP * HSLAB, D),
                jnp.transpose(dwout.reshape(NCHIP, 2, hw, D), (1, 0, 2, 3)).reshape(2, NCHIP * hw, D))

    def sum_halves(halves, swapped):
        return (_add_halves(c_idx, halves[0], swapped[0], 192, "add_halves_in").reshape(1, NCHIP, HSLAB, D),
                _add_halves(c_idx, halves[1], swapped[1], 256, "add_halves_out").reshape(1, NCHIP, hw, D))

    def partial_sums(dwin, dwout):
        halves = slab_halves(dwin, dwout)
        return sum_halves(halves, _swap_halves(*halves))

    halves = None
    for l in reversed(range(nl)):
        dout, dwin, dwout, small[l], sent, got, own, got_own = _layer_bwd(
            dout, saved[l], *weights(l), swap=halves, finish=sum_halves, reduce_now=partial_sums if l == 0 else None)
        if halves is not None:
            hin[l + 1], hout[l + 1] = reduce_owner(sent, got)
        halves = slab_halves(dwin, dwout) if l > 0 else None
    sent = own
    grad_x = dout[None]
    hin[0], hout[0] = reduce_owner(sent, list(_send_to_owners(*sent, parts=(0,))) + list(got_own))
    xchg = _exchange_halves(hin + hout)
    hin, xin = jnp.concatenate(hin), jnp.concatenate(xchg[:nl])
    hout, xout = jnp.concatenate(hout), jnp.concatenate(xchg[nl:])

    rows_first = lambda t: jnp.transpose(t, (2, 0, 1))
    w_in_out = _adam_rows(c_idx, rows_first(w_in), hin, xin, rows_first(m_w_in), rows_first(v_w_in), 48, "adam_w_in")
    g_w_in, d_w_in, nm_w_in, nv_w_in = [jnp.transpose(t, (1, 2, 0)) for t in w_in_out]

    def adam_big(w, g_own, g_other, m, v, name):
        shp = w.shape
        halves = lambda t: t.reshape(shp[0], 2, shp[1] // 2, shp[2])
        return [t.reshape(shp) for t in _adam_halves(c_idx, halves(w), g_own, g_other, halves(m), halves(v), 256, name)]

    g_w_out, d_w_out, nm_w_out, nv_w_out = adam_big(w_out, hout, xout, m_w_out, v_w_out, "adam_w_out")

    stacked = [jnp.stack([small[l][i] for l in range(nl)]) for i in range(7)] + [loss_part]
    g_small = _unpack_rows(_allreduce_small(_pack_rows(stacked)), [t.shape for t in stacked])
    g_gpre, g_gpost, g_wa_full, g_ba, g_ggla, g_gatt, g_rb, loss_sum = g_small
    loss = loss_sum[0, 0]
    g_wa = lax.dynamic_slice_in_dim(g_wa_full, chip * GDK, GDK, axis=2)
    names = [(g_pre, m_g_pre, v_g_pre, g_gpre), (g_post, m_g_post, v_g_post, g_gpost), (w_alpha, m_w_alpha, v_w_alpha, g_wa),
             (b_alpha, m_b_alpha, v_b_alpha, g_ba), (g_gla, m_g_gla, v_g_gla, g_ggla), (g_att, m_g_att, v_g_att, g_gatt),
             (rel_bias, m_rel_bias, v_rel_bias, g_rb)]
    shapes = [t[0].shape for t in names]
    packed = [_pack_rows([t[i] for t in names]) for i in range(4)]
    d_s, nm_s, nv_s = [_unpack_rows(t, shapes) for t in _adam(packed[0], packed[3], packed[1], packed[2], packed[0].shape[0], "adam_small")]

    grads = [g_w_in, g_w_out, g_gpre, g_gpost, g_wa, g_ba, g_ggla, g_gatt, g_rb]
    deltas = [d_w_in, d_w_out] + d_s
    new_m = [nm_w_in, nm_w_out] + nm_s
    new_v = [nv_w_in, nv_w_out] + nv_s
    return (loss, grad_x, *grads, *deltas, *new_m, *new_v)
```

```python
import functools

import jax
import jax.numpy as jnp
from jax import lax
from jax.experimental import pallas as pl
from jax.experimental.pallas import tpu as pltpu

D = 2048
DEPTH = 4
CHUNK = 64
GH, GDK, GDV = 4, 128, 256
DGLA = GH * GDV
RANK = 16
TAU = 16.0
AH, AHD = 8, 128
DATT = AH * AHD
LEFT = 8
NREL = 257
EPS = 1e-6
DIN = 7184
ADAM_LR, ADAM_B1, ADAM_B2, ADAM_EPS, ADAM_WD, ADAM_STEP = 0.001, 0.9, 0.999, 1e-08, 0.01, 10

GW = 2 * GDK + 2 * GDV
AW = 4 * AHD
ZG = GH * GW
ZA = AH * AW
ZM = ZG + ZA
GAP = 128
QB = 2 * CHUNK
HP = 2
BANDW = (LEFT + 2) * CHUNK
PADK = LEFT * CHUNK
NCHIP = 4
SHARD = DIN // NCHIP
SLAB = 1824
HSLAB = SLAB // 2
WSLOT = 1824
GSUB_IN = [(0, 464), (464, 448)]
GSUB_OUT = [(0, 128), (128, 128)]
SPLIT_IN, SPLIT_OUT = 336, 128
NEG = -1e30
F32 = jnp.float32
_CDT = jnp.bfloat16
_XDT = jnp.bfloat16
_VMEM = 56 * 1024 * 1024
MESH = pl.DeviceIdType.MESH
ANY = pl.BlockSpec(memory_space=pl.ANY)


def _dot(a, b):
    return jnp.dot(a, b, preferred_element_type=F32)


def _dot_nt(a, b):
    return lax.dot_general(a, b, (((1,), (1,)), ((), ())), preferred_element_type=F32)


def _dot_tn(a, b):
    return lax.dot_general(a, b, (((0,), (0,)), ((), ())), preferred_element_type=F32)


def _rms_rows(v):
    return lax.rsqrt(jnp.mean(v * v, axis=-1, keepdims=True) + EPS)


def _sigmoid(v):
    return 1.0 / (1.0 + jnp.exp(-v))


def _log_sigmoid(v):
    return jnp.minimum(v, 0.0) - jnp.log(1.0 + jnp.exp(-jnp.abs(v)))


def _exact_dot(tri, v):
    hi = v.astype(_CDT)
    r1 = v - hi.astype(F32)
    mid = r1.astype(_CDT)
    lo = (r1 - mid.astype(F32)).astype(_CDT)
    return _dot(tri, hi) + _dot(tri, mid) + _dot(tri, lo)


def _tri(strict):
    row = lax.broadcasted_iota(jnp.int32, (CHUNK, CHUNK), 0)
    col = lax.broadcasted_iota(jnp.int32, (CHUNK, CHUNK), 1)
    return jnp.where((col < row) if strict else (col <= row), 1.0, 0.0).astype(_CDT)


def _norm_gate_bwd(o, g, gate, dcat):
    r = _rms_rows(o)
    oh = o * r
    sg = _sigmoid(gate)
    dn = dcat * (gate * sg)
    dgate = dcat * (oh * g) * (sg * (1.0 + gate * (1.0 - sg)))
    dg = jnp.sum(dn * oh, axis=0, keepdims=True)
    dnn = dn * g
    do = r * (dnn - oh * jnp.mean(dnn * oh, axis=-1, keepdims=True))
    return do, dgate, dg


def _params(sem=None, vmem=_VMEM):
    return pltpu.CompilerParams(dimension_semantics=sem, vmem_limit_bytes=vmem)


def _inproj(x, g, wm, wga, ride=None, tm=512, tn=1024):
    s = x.shape[0]
    gr = _gather_ride(ride, 0)

    def body(*refs):
        (x_ref, g_ref, wm_ref, wga_ref, z_ref, zga_ref, h_ref, hs), rr = gr.split(refs, 4, 3)
        i, j = pl.program_id(0), pl.program_id(1)
        gr.start(rr, (i == 0) & (j == 0))

        @pl.when(pl.program_id(1) == 0)
        def _():
            xv = x_ref[...]
            hv = (xv * _rms_rows(xv) * g_ref[...]).astype(_CDT)
            hs[...] = hv
            h_ref[...] = hv
            zga_ref[...] = _dot_nt(hv, wga_ref[...]).astype(_CDT)

        z_ref[...] = _dot_nt(hs[...], wm_ref[...]).astype(_CDT)
        gr.wait(rr, (i == s // tm - 1) & (j == ZM // tn - 1))

    return pl.pallas_call(
        body, name="inproj", grid=(s // tm, ZM // tn),
        in_specs=[pl.BlockSpec((tm, D), lambda i, j: (i, 0)), pl.BlockSpec((1, D), lambda i, j: (0, 0)),
                  pl.BlockSpec((tn, D), lambda i, j: (j, 0)), pl.BlockSpec((GAP, D), lambda i, j: (0, 0))] + gr.in_specs,
        out_specs=[pl.BlockSpec((tm, tn), lambda i, j: (i, j)), pl.BlockSpec((tm, GAP), lambda i, j: (i, 0)),
                   pl.BlockSpec((tm, D), lambda i, j: (i, 0))] + gr.out_specs,
        out_shape=[jax.ShapeDtypeStruct((s, ZM), _CDT), jax.ShapeDtypeStruct((s, GAP), _CDT),
                   jax.ShapeDtypeStruct((s, D), _CDT)] + gr.out_shape,
        scratch_shapes=[pltpu.VMEM((tm, D), _CDT)] + gr.scratch, input_output_aliases=gr.alias(4, 3),
        compiler_params=_params(("arbitrary", "arbitrary")),
    )(x, g, wm, wga, *gr.operands)


def _gla_fwd(z, zga, wa, ba, ggla, ride=None):
    s = z.shape[0]
    nc = s // CHUNK
    gr = _gather_ride(ride, 1)

    def body(*refs):
        (zg_ref, zga_ref, wa_ref, ba_ref, g_ref, cat_ref, la_s, st), rr = gr.split(refs, 5, 1)
        gr.start(rr, pl.program_id(0) == 0)
        la_s[...] = _log_sigmoid(_dot(zga_ref[...], wa_ref[...]) + ba_ref[...]) * (1.0 / TAU)
        st[...] = jnp.zeros_like(st)
        tri = _tri(False)

        def step(n, carry):
            rows = pl.ds(pl.multiple_of(n * CHUNK, CHUNK), CHUNK)
            for p in range(HP):
                z0 = p * GW
                la = la_s[rows, p * GDK:(p + 1) * GDK]
                lc = _exact_dot(tri, la)
                lend = jnp.sum(la, axis=0, keepdims=True)
                kdec = (zg_ref[rows, z0 + GDK:z0 + 2 * GDK].astype(F32) * jnp.exp(lend - lc)).astype(_CDT)
                stn = jnp.exp(lend) * st[p] + _dot_tn(zg_ref[rows, z0 + 2 * GDK:z0 + 2 * GDK + GDV], kdec)
                st[p] = stn
                qs = (zg_ref[rows, z0:z0 + GDK].astype(F32) * (GDK ** -0.5)).astype(_CDT)
                o = _dot_nt(qs, stn.astype(_CDT))
                gate = zg_ref[rows, z0 + 2 * GDK + GDV:z0 + GW].astype(F32)
                gain = g_ref[:, p * GDV:(p + 1) * GDV]
                cat_ref[rows, p * GDV:(p + 1) * GDV] = (o * _rms_rows(o) * gain * (gate * _sigmoid(gate))).astype(_CDT)
            return carry

        lax.fori_loop(0, nc, step, 0, unroll=4)
        gr.wait(rr, pl.program_id(0) == GH // HP - 1)

    return pl.pallas_call(
        body, name="gla_fwd", grid=(GH // HP,),
        in_specs=[pl.BlockSpec((s, HP * GW), lambda h: (0, h)), pl.BlockSpec((s, GAP), lambda h: (0, 0)),
                  pl.BlockSpec((GAP, HP * GDK), lambda h: (0, h)), pl.BlockSpec((1, HP * GDK), lambda h: (0, h)),
                  pl.BlockSpec((1, HP * GDV), lambda h: (0, h))] + gr.in_specs,
        out_specs=[pl.BlockSpec((s, HP * GDV), lambda h: (0, h))] + gr.out_specs,
        out_shape=[jax.ShapeDtypeStruct((s, DGLA), _CDT)] + gr.out_shape,
        scratch_shapes=[pltpu.VMEM((s, HP * GDK), F32), pltpu.VMEM((HP, GDV, GDK), F32)] + gr.scratch,
        input_output_aliases=gr.alias(5, 1), compiler_params=_params(("arbitrary",)),
    )(z, zga, wa, ba, ggla, *gr.operands)


def _band_bias(b0):
    row = lax.broadcasted_iota(jnp.int32, (QB, 256), 0)
    col = lax.broadcasted_iota(jnp.int32, (QB, 256), 1)
    lane = lax.broadcasted_iota(jnp.int32, (1, 256), 1)
    c0 = jnp.sum(jnp.where(lane == 0, b0, 0.0), axis=1, keepdims=True)
    xv = jnp.broadcast_to(b0, (QB, 256))
    for bit in range(7):
        xv = jnp.where(((row >> bit) & 1) == 1, pltpu.roll(xv, 1 << bit, 1), xv)
    xv = jnp.where(col < row, c0, xv)
    return jnp.concatenate([jnp.broadcast_to(c0, (QB, BANDW - 256)), xv], axis=1)


def _band_static_mask():
    row = lax.broadcasted_iota(jnp.int32, (QB, BANDW), 0) >> 6
    col = lax.broadcasted_iota(jnp.int32, (QB, BANDW), 1) >> 6
    return (col >= row) & (col <= row + LEFT)


def _fold_bias_grad(t):
    row = lax.broadcasted_iota(jnp.int32, (QB, 256), 0)
    col = lax.broadcasted_iota(jnp.int32, (QB, 256), 1)
    xv = t[:, BANDW - 256:]
    low = col < row
    far = jnp.sum(t[:, 0:BANDW - 256], axis=1, keepdims=True) + jnp.sum(jnp.where(low, xv, 0.0), axis=1, keepdims=True)
    far = jnp.sum(far, axis=0, keepdims=True)
    xv = jnp.where(low, 0.0, xv)
    for bit in range(7):
        xv = jnp.where(((row >> bit) & 1) == 1, pltpu.roll(xv, 256 - (1 << bit), 1), xv)
    dp = jnp.sum(xv, axis=0, keepdims=True)
    lane = lax.broadcasted_iota(jnp.int32, (1, 256), 1)
    return dp + jnp.where(lane == 0, far, 0.0)


def _att_fwd(z, rbx, gatt, ride=None):
    s = z.shape[0]
    nb = s // QB
    gr = _gather_ride(ride, 1)

    def body(*refs):
        (za_ref, rb_ref, g_ref, cat_ref, o_ref, lse_ref, kp, vp, bias_s), rr = gr.split(refs, 3, 3)
        gr.start(rr, pl.program_id(0) == 0)
        for p in range(HP):
            z0 = p * AW
            kp[p, 0:PADK, :] = jnp.zeros((PADK, AHD), _CDT)
            vp[p, 0:PADK, :] = jnp.zeros((PADK, AHD), _CDT)
            kp[p, PADK:, :] = za_ref[:, z0 + AHD:z0 + 2 * AHD]
            vp[p, PADK:, :] = za_ref[:, z0 + 2 * AHD:z0 + 3 * AHD]
            bias_s[p] = jnp.where(_band_static_mask(), _band_bias(rb_ref[p]), NEG)

        def step(b, carry):
            r0 = pl.multiple_of(b * QB, QB)
            rows = pl.ds(r0, QB)
            band = pl.ds(r0, BANDW)
            live = lax.broadcasted_iota(jnp.int32, (QB, BANDW), 1) >= PADK - r0
            for p in range(HP):
                z0 = p * AW
                cols = slice(p * AHD, (p + 1) * AHD)
                sc = _dot_nt(za_ref[rows, z0:z0 + AHD], kp[p, band, :]) * (AHD ** -0.5) + bias_s[p]
                sc = jnp.where(live, sc, NEG)
                m = jnp.max(sc, axis=-1, keepdims=True)
                pr = jnp.exp(sc - m)
                l = jnp.sum(pr, axis=-1, keepdims=True)
                o = _dot((pr * (1.0 / l)).astype(_CDT), vp[p, band, :])
                o_ref[rows, cols] = o.astype(_CDT)
                lse_ref[rows, cols] = jnp.broadcast_to(m + jnp.log(l), (QB, AHD))
                gate = za_ref[rows, z0 + 3 * AHD:z0 + AW].astype(F32)
                cat_ref[rows, cols] = (o * _rms_rows(o) * g_ref[:, cols] * (gate * _sigmoid(gate))).astype(_CDT)
            return carry

        lax.fori_loop(0, nb, step, 0, unroll=4)
        gr.wait(rr, pl.program_id(0) == AH // HP - 1)

    return pl.pallas_call(
        body, name="att_fwd", grid=(AH // HP,),
        in_specs=[pl.BlockSpec((s, HP * AW), lambda h: (0, ZG // (HP * AW) + h)), pl.BlockSpec((HP, 1, 256), lambda h: (h, 0, 0)),
                  pl.BlockSpec((1, HP * AHD), lambda h: (0, h))] + gr.in_specs,
        out_specs=[pl.BlockSpec((s, HP * AHD), lambda h: (0, h)), pl.BlockSpec((s, HP * AHD), lambda h: (0, h)),
                   pl.BlockSpec((s, HP * AHD), lambda h: (0, h))] + gr.out_specs,
        out_shape=[jax.ShapeDtypeStruct((s, DATT), _CDT), jax.ShapeDtypeStruct((s, DATT), _CDT),
                   jax.ShapeDtypeStruct((s, DATT), F32)] + gr.out_shape,
        scratch_shapes=[pltpu.VMEM((HP, s + PADK, AHD), _CDT), pltpu.VMEM((HP, s + PADK, AHD), _CDT),
                        pltpu.VMEM((HP, QB, BANDW), F32)] + gr.scratch, input_output_aliases=gr.alias(3, 3),
        compiler_params=_params(("arbitrary",)),
    )(z, rbx, gatt, *gr.operands)


def _outproj(cg, ca, wout, x, gpost, ride=None, tm=256):
    s = x.shape[0]
    gr = _gather_ride(ride, 2)

    def body(*refs):
        (cg_ref, ca_ref, w_ref, x_ref, g_ref, y_ref, xo_ref), rr = gr.split(refs, 5, 2)
        gr.start(rr, pl.program_id(0) == 0)
        y = _dot(cg_ref[...], w_ref[0:DGLA, :]) + _dot(ca_ref[...], w_ref[DGLA:, :])
        y_ref[...] = y
        xo_ref[...] = x_ref[...] + y * _rms_rows(y) * g_ref[...]
        gr.wait(rr, pl.program_id(0) == s // tm - 1)

    return pl.pallas_call(
        body, name="outproj", grid=(s // tm,),
        in_specs=[pl.BlockSpec((tm, DGLA), lambda i: (i, 0)), pl.BlockSpec((tm, DATT), lambda i: (i, 0)),
                  pl.BlockSpec((D, D), lambda i: (0, 0)), pl.BlockSpec((tm, D), lambda i: (i, 0)),
                  pl.BlockSpec((1, D), lambda i: (0, 0))] + gr.in_specs,
        out_specs=[pl.BlockSpec((tm, D), lambda i: (i, 0)), pl.BlockSpec((tm, D), lambda i: (i, 0))] + gr.out_specs,
        out_shape=[jax.ShapeDtypeStruct((s, D), F32), jax.ShapeDtypeStruct((s, D), F32)] + gr.out_shape,
        scratch_shapes=gr.scratch, input_output_aliases=gr.alias(5, 2),
        compiler_params=_params(("arbitrary",)),
    )(cg, ca, wout, x, gpost, *gr.operands)


def _loss_grad(xo, tgt, tm=256):
    s = xo.shape[0]

    def body(xo_ref, t_ref, d_ref, l_ref):
        @pl.when(pl.program_id(0) == 0)
        def _():
            l_ref[...] = jnp.zeros_like(l_ref)

        e = xo_ref[...] - t_ref[...]
        d_ref[...] = e * (1.0 / D)
        l_ref[...] += jnp.sum(jnp.sum(e * e, axis=1, keepdims=True), axis=0, keepdims=True) * (0.5 / D)

    return pl.pallas_call(
        body, name="loss_grad", grid=(s // tm,),
        in_specs=[pl.BlockSpec((tm, D), lambda i: (i, 0)), pl.BlockSpec((tm, D), lambda i: (i, 0))],
        out_specs=[pl.BlockSpec((tm, D), lambda i: (i, 0)), pl.BlockSpec((1, 1), lambda i: (0, 0))],
        out_shape=[jax.ShapeDtypeStruct((s, D), F32), jax.ShapeDtypeStruct((1, 1), F32)],
        compiler_params=_params(("arbitrary",)),
    )(xo, tgt)


def _place():
    x, y, c = lax.axis_index("x"), lax.axis_index("y"), lax.axis_index("c")
    chips = [(1 - x, y), (x, 1 - y), (1 - x, 1 - y)]
    return x, y, c, chips


def _variants(fn):
    x, y, c, _ = _place()
    for jx in range(2):
        for jy in range(2):
            for jc in range(2):
                pl.when((x == jx) & (y == jy) & (c == jc))(functools.partial(fn, jx, jy, jc))


def _gather_copies(part, x, y, c, src_in, src_out, buf_in, buf_out, send_sems, recv_sems):
    me = 2 * x + y
    xn, yn, dg = (1 - x, y), (x, 1 - y), (1 - x, 1 - y)
    cps = []
    for a, (src, buf, subs) in enumerate([(src_in, buf_in, GSUB_IN), (src_out, buf_out, GSUB_OUT)]):
        if part == 0:
            pairs = [(src.at[c], buf.at[me, c], (p[0], p[1], c)) for p in (xn, yn)]
        elif part == 1:
            from_x = buf.at[2 * xn[0] + xn[1], c, pl.ds(*subs[0])]
            from_y = buf.at[2 * yn[0] + yn[1], c, pl.ds(*subs[1])]
            pairs = [(from_x, from_x, (yn[0], yn[1], c)), (from_y, from_y, (xn[0], xn[1], c))]
        else:
            pairs = [(buf.at[2 * p[0] + p[1], c], buf.at[2 * p[0] + p[1], c], (x, y, 1 - c)) for p in (xn, yn, dg)]
        for k, (src_k, dst_k, to) in enumerate(pairs):
            cps.append(pltpu.make_async_remote_copy(src_ref=src_k, dst_ref=dst_k, send_sem=send_sems.at[a, k],
                                                    recv_sem=recv_sems.at[a, k], device_id=to, device_id_type=MESH))
    return cps


class _gather_ride:
    def __init__(self, ride, part):
        self.part, self.on = part, ride is not None
        self.in_specs, self.out_specs, self.out_shape, self.scratch, self.operands, self.aliases = [], [], [], [], [], {}
        if self.on:
            self.operands = list(ride)
            self.in_specs, self.out_specs = [ANY] * 4, [ANY] * 2
            self.out_shape = [jax.ShapeDtypeStruct(t.shape, t.dtype) for t in ride[2:]]
            self.scratch = [pltpu.SemaphoreType.DMA((2, 3)), pltpu.SemaphoreType.DMA((2, 3))]

    def alias(self, n_in, n_out):
        return {n_in + 2: n_out, n_in + 3: n_out + 1} if self.on else {}

    def split(self, refs, n_in, n_out):
        if not self.on:
            return refs, None
        own = refs[:n_in] + refs[n_in + 4:n_in + 4 + n_out] + refs[n_in + 6 + n_out:-2]
        return own, refs[n_in:n_in + 2] + refs[n_in + 4 + n_out:n_in + 6 + n_out] + refs[-2:]

    def start(self, ride_refs, first):
        if self.on:
            def go(x, y, c):
                for cpy in _gather_copies(self.part, x, y, c, *ride_refs):
                    cpy.start()
            pl.when(first)(lambda: _variants(go))

    def wait(self, ride_refs, last):
        if self.on:
            def done(x, y, c):
                for cpy in _gather_copies(self.part, x, y, c, *ride_refs):
                    cpy.wait()
            pl.when(last)(lambda: _variants(done))


def _part_rows(total, split, part):
    return (0, split) if part == 0 else (split, total - split)


def _owner_copies(part, pin_ref, pout_ref, rin, rout, send_sems, recv_sems):
    x, y, c, chips = _place()
    cps = []
    for k, (px, py) in enumerate(chips):
        for a, (src, dst, split) in enumerate([(pin_ref, rin, SPLIT_IN), (pout_ref, rout, SPLIT_OUT)]):
            r0, n = _part_rows(src.shape[2], split, part)
            cps.append(pltpu.make_async_remote_copy(src_ref=src.at[0, 2 * px + py, pl.ds(r0, n)], dst_ref=dst.at[k],
                                                    send_sem=send_sems.at[a, k], recv_sem=recv_sems.at[a, k],
                                                    device_id=(px, py, c), device_id_type=MESH))
    return cps


class _ride_specs:
    def __init__(self, send, part):
        self.in_specs, self.out_specs, self.out_shape, self.scratch, self.operands = [], [], [], [], []
        if send is not None:
            self.in_specs, self.out_specs, self.operands = [ANY, ANY], [ANY, ANY], list(send)
            self.out_shape = [jax.ShapeDtypeStruct((3, _part_rows(t.shape[2], split, part)[1], D), t.dtype)
                              for t, split in zip(send, (SPLIT_IN, SPLIT_OUT))]
            self.scratch = [pltpu.SemaphoreType.DMA((2, 3)), pltpu.SemaphoreType.DMA((2, 3))]


def _ride_refs(refs, send, n_out):
    if send is None:
        return None, refs
    pin_ref, pout_ref = refs[:2]
    own_out = refs[2:2 + n_out]
    rin, rout = refs[2 + n_out:4 + n_out]
    return (pin_ref, pout_ref, rin, rout, refs[-2], refs[-1]), tuple(own_out) + tuple(refs[4 + n_out:-2])


def _ride_start(refs, send, n_out, part, first=None):
    ride = _ride_refs(refs, send, n_out)[0]
    if ride is None:
        return []
    cps = _owner_copies(part, *ride)

    @pl.when(pl.program_id(0) == 0 if first is None else first)
    def _():
        for cpy in cps:
            cpy.start()

    return cps


def _ride_wait(cps, steps, last=None):
    if cps:
        @pl.when(pl.program_id(0) == steps - 1 if last is None else last)
        def _():
            for cpy in cps:
                cpy.wait()


def _post_bwd(dout, y, gpost, wout, swap=None, tm=256):
    s = y.shape[0]

    def body(*refs):
        d_ref, y_ref, g_ref, w_ref = refs[:4]
        dy_ref, dcg_ref, dca_ref, dg_ref = refs[6:10] if swap is not None else refs[4:8]
        cps = []
        if swap is not None:
            x, yy, c, _ = _place()
            cps = [pltpu.make_async_remote_copy(src_ref=refs[4 + a].at[1 - c], dst_ref=refs[10 + a], send_sem=refs[12].at[a],
                                                recv_sem=refs[13].at[a], device_id=(x, yy, 1 - c), device_id_type=MESH) for a in range(2)]

            @pl.when(pl.program_id(0) == 0)
            def _():
                for cpy in cps:
                    cpy.start()

        @pl.when(pl.program_id(0) == 0)
        def _():
            dg_ref[...] = jnp.zeros_like(dg_ref)

        yv = y_ref[...]
        r = _rms_rows(yv)
        yh = yv * r
        dv = d_ref[...]
        dg_ref[...] += jnp.sum(dv * yh, axis=0, keepdims=True)
        dn = dv * g_ref[...]
        dyb = (r * (dn - yh * jnp.mean(dn * yh, axis=-1, keepdims=True))).astype(_CDT)
        dy_ref[...] = dyb
        dcg_ref[...] = _dot_nt(dyb, w_ref[0:DGLA, :]).astype(_CDT)
        dca_ref[...] = _dot_nt(dyb, w_ref[DGLA:, :]).astype(_CDT)
        _ride_wait(cps, s // tm)

    extra = [] if swap is None else list(swap)
    return pl.pallas_call(
        body, name="post_bwd", grid=(s // tm,),
        in_specs=[pl.BlockSpec((tm, D), lambda i: (i, 0)), pl.BlockSpec((tm, D), lambda i: (i, 0)),
                  pl.BlockSpec((1, D), lambda i: (0, 0)), pl.BlockSpec((D, D), lambda i: (0, 0))] + [ANY] * len(extra),
        out_specs=[pl.BlockSpec((tm, D), lambda i: (i, 0)), pl.BlockSpec((tm, DGLA), lambda i: (i, 0)),
                   pl.BlockSpec((tm, DATT), lambda i: (i, 0)), pl.BlockSpec((1, D), lambda i: (0, 0))] + [ANY] * len(extra),
        out_shape=[jax.ShapeDtypeStruct((s, D), _CDT), jax.ShapeDtypeStruct((s, DGLA), _CDT),
                   jax.ShapeDtypeStruct((s, DATT), _CDT), jax.ShapeDtypeStruct((1, D), F32)]
        + [jax.ShapeDtypeStruct(t.shape[1:], t.dtype) for t in extra],
        scratch_shapes=[pltpu.SemaphoreType.DMA((2,)), pltpu.SemaphoreType.DMA((2,))] if extra else [],
        compiler_params=_params(("arbitrary",)),
    )(dout, y, gpost, wout, *extra)


def _matmul_tn(a, b, out_dtype, tm, tn, name):
    k, m = a.shape
    n = b.shape[1]

    def body(a_ref, b_ref, o_ref):
        o_ref[...] = _dot_tn(a_ref[...], b_ref[...]).astype(out_dtype)

    return pl.pallas_call(
        body, name=name, grid=(m // tm, n // tn),
        in_specs=[pl.BlockSpec((k, tm), lambda i, j: (0, i)), pl.BlockSpec((k, tn), lambda i, j: (0, j))],
        out_specs=pl.BlockSpec((tm, tn), lambda i, j: (i, j)),
        out_shape=jax.ShapeDtypeStruct((m, n), out_dtype),
        compiler_params=_params(("parallel", "parallel")),
    )(a, b)


def _att_bwd(z, oraw, lse, dca, rbx, gatt, send=None):
    s = z.shape[0]
    nb = s // QB

    def body(*refs):
        za_ref, o_ref, lse_ref, dc_ref, rb_ref, g_ref = refs[:6]
        dz_ref, dg_ref, db_ref, kp, vp, dkp, dvp, bias_s, t_s, dg_s = _ride_refs(refs[6:], send, 3)[1]
        cps = _ride_start(refs[6:], send, 3, 0)
        for p in range(HP):
            z0 = p * AW
            kp[p, 0:PADK, :] = jnp.zeros((PADK, AHD), _CDT)
            vp[p, 0:PADK, :] = jnp.zeros((PADK, AHD), _CDT)
            kp[p, PADK:, :] = za_ref[:, z0 + AHD:z0 + 2 * AHD]
            vp[p, PADK:, :] = za_ref[:, z0 + 2 * AHD:z0 + 3 * AHD]
            bias_s[p] = jnp.where(_band_static_mask(), _band_bias(rb_ref[p]), NEG)
        dkp[...] = jnp.zeros_like(dkp)
        dvp[...] = jnp.zeros_like(dvp)
        t_s[...] = jnp.zeros_like(t_s)
        dg_s[...] = jnp.zeros_like(dg_s)

        def step(b, carry):
            r0 = pl.multiple_of(b * QB, QB)
            rows = pl.ds(r0, QB)
            band = pl.ds(r0, BANDW)
            live = lax.broadcasted_iota(jnp.int32, (QB, BANDW), 1) >= PADK - r0
            for p in range(HP):
                z0 = p * AW
                cols = slice(p * AHD, (p + 1) * AHD)
                o = o_ref[rows, cols].astype(F32)
                do, dgate, dg = _norm_gate_bwd(o, g_ref[:, cols], za_ref[rows, z0 + 3 * AHD:z0 + AW].astype(F32),
                                               dc_ref[rows, cols].astype(F32))
                dg_s[:, cols] += dg
                q = za_ref[rows, z0:z0 + AHD]
                kb = kp[p, band, :]
                sc = _dot_nt(q, kb) * (AHD ** -0.5) + bias_s[p]
                sc = jnp.where(live, sc, NEG)
                pr = jnp.exp(sc - jnp.max(lse_ref[rows, cols], axis=-1, keepdims=True))
                dob = do.astype(_CDT)
                dp = _dot_nt(dob, vp[p, band, :])
                ds = pr * (dp - jnp.sum(do * o, axis=-1, keepdims=True))
                t_s[p] += ds
                dsb = (ds * (AHD ** -0.5)).astype(_CDT)
                dz_ref[rows, z0:z0 + AHD] = _dot(dsb, kb).astype(_CDT)
                dz_ref[rows, z0 + 3 * AHD:z0 + AW] = dgate.astype(_CDT)
                dkp[p, band, :] += _dot_tn(dsb, q)
                dvp[p, band, :] += _dot_tn(pr.astype(_CDT), dob)
            return carry

        lax.fori_loop(0, nb, step, 0, unroll=2)
        for p in range(HP):
            z0 = p * AW
            dz_ref[:, z0 + AHD:z0 + 2 * AHD] = dkp[p, PADK:, :].astype(_CDT)
            dz_ref[:, z0 + 2 * AHD:z0 + 3 * AHD] = dvp[p, PADK:, :].astype(_CDT)
            db_ref[p] = _fold_bias_grad(t_s[p])
        dg_ref[...] = dg_s[...]
        _ride_wait(cps, AH // HP)

    ride = _ride_specs(send, 0)
    return pl.pallas_call(
        body, name="att_bwd", grid=(AH // HP,),
        in_specs=[pl.BlockSpec((s, HP * AW), lambda h: (0, ZG // (HP * AW) + h)), pl.BlockSpec((s, HP * AHD), lambda h: (0, h)),
                  pl.BlockSpec((s, HP * AHD), lambda h: (0, h)), pl.BlockSpec((s, HP * AHD), lambda h: (0, h)),
                  pl.BlockSpec((HP, 1, 256), lambda h: (h, 0, 0)), pl.BlockSpec((1, HP * AHD), lambda h: (0, h))] + ride.in_specs,
        out_specs=[pl.BlockSpec((s, HP * AW), lambda h: (0, h)), pl.BlockSpec((1, HP * AHD), lambda h: (0, h)),
                   pl.BlockSpec((HP, 1, 256), lambda h: (h, 0, 0))] + ride.out_specs,
        out_shape=[jax.ShapeDtypeStruct((s, ZA), _CDT), jax.ShapeDtypeStruct((1, DATT), F32),
                   jax.ShapeDtypeStruct((AH, 1, 256), F32)] + ride.out_shape,
        scratch_shapes=[pltpu.VMEM((HP, s + PADK, AHD), _CDT), pltpu.VMEM((HP, s + PADK, AHD), _CDT),
                        pltpu.VMEM((HP, s + PADK, AHD), F32), pltpu.VMEM((HP, s + PADK, AHD), F32),
                        pltpu.VMEM((HP, QB, BANDW), F32), pltpu.VMEM((HP, QB, BANDW), F32), pltpu.VMEM((1, HP * AHD), F32)] + ride.scratch,
        compiler_params=_params(("arbitrary",)),
    )(z, oraw, lse, dca, rbx, gatt, *ride.operands)


def _gla_bwd(z, zga, wa, ba, ggla, dcg, send=None):
    s = z.shape[0]
    nc = s // CHUNK

    def body(*refs):
        zg_ref, zga_ref, wa_ref, ba_ref, g_ref, dc_ref = refs[:6]
        (dz_ref, dga_ref, dwa_ref, dba_ref, dg_ref,
         la_s, om_s, sall, dpre_s, c_s, dga_s, dg_s) = _ride_refs(refs[6:], send, 5)[1]
        cps = _ride_start(refs[6:], send, 5, 1)
        h = pl.program_id(0)
        pre = _dot(zga_ref[...], wa_ref[...]) + ba_ref[...]
        la_s[...] = _log_sigmoid(pre) * (1.0 / TAU)
        om_s[...] = (1.0 - _sigmoid(pre)) * (1.0 / TAU)
        c_s[...] = jnp.zeros_like(c_s)
        dg_s[...] = jnp.zeros_like(dg_s)
        tri = _tri(False)
        tri_strict = _tri(True)

        def decay(rows, p):
            la = la_s[rows, p * GDK:(p + 1) * GDK]
            lend = jnp.sum(la, axis=0, keepdims=True)
            return jnp.exp(lend - _exact_dot(tri, la)), jnp.exp(lend)

        def fwd(n, sts):
            rows = pl.ds(pl.multiple_of(n * CHUNK, CHUNK), CHUNK)
            out = []
            for p in range(HP):
                z0 = p * GW
                dec, a = decay(rows, p)
                kdec = (zg_ref[rows, z0 + GDK:z0 + 2 * GDK].astype(F32) * dec).astype(_CDT)
                stn = a * sts[p] + _dot_tn(zg_ref[rows, z0 + 2 * GDK:z0 + 2 * GDK + GDV], kdec)
                sall[p, n] = stn
                out.append(stn)
            return tuple(out)

        lax.fori_loop(0, nc, fwd, tuple(jnp.zeros((GDV, GDK), F32) for _ in range(HP)), unroll=4)

        def bwd(i, carry):
            n = nc - 1 - i
            rows = pl.ds(pl.multiple_of(n * CHUNK, CHUNK), CHUNK)
            for p in range(HP):
                z0 = p * GW
                kc = slice(p * GDK, (p + 1) * GDK)
                vc = slice(p * GDV, (p + 1) * GDV)
                dec, a = decay(rows, p)
                kdec = zg_ref[rows, z0 + GDK:z0 + 2 * GDK].astype(F32) * dec
                kdb = kdec.astype(_CDT)
                v = zg_ref[rows, z0 + 2 * GDK:z0 + 2 * GDK + GDV]
                qs = (zg_ref[rows, z0:z0 + GDK].astype(F32) * (GDK ** -0.5)).astype(_CDT)
                stb = sall[p, n].astype(_CDT)
                st_prev = sall[p, jnp.maximum(n - 1, 0)] * jnp.where(n > 0, 1.0, 0.0)
                o = _dot_nt(qs, stb)
                do, dgate, dg = _norm_gate_bwd(o, g_ref[:, vc], zg_ref[rows, z0 + 2 * GDK + GDV:z0 + GW].astype(F32),
                                               dc_ref[rows, vc].astype(F32))
                dg_s[:, vc] += dg
                dob = do.astype(_CDT)
                gt = _dot_tn(dob, qs) + c_s[p]
                gtb = gt.astype(_CDT)
                da = jnp.sum(gt * st_prev, axis=0, keepdims=True)
                dkdec = _dot(v, gtb)
                dla = _exact_dot(tri_strict, dkdec * kdec) + da * a
                dpre_s[rows, kc] = dla * om_s[rows, kc]
                dz_ref[rows, z0:z0 + GDK] = (_dot(dob, stb) * (GDK ** -0.5)).astype(_CDT)
                dz_ref[rows, z0 + GDK:z0 + 2 * GDK] = (dkdec * dec).astype(_CDT)
                dz_ref[rows, z0 + 2 * GDK:z0 + 2 * GDK + GDV] = _dot_nt(kdb, gtb).astype(_CDT)
                dz_ref[rows, z0 + 2 * GDK + GDV:z0 + GW] = dgate.astype(_CDT)
                c_s[p] = a * gt
            return carry

        lax.fori_loop(0, nc, bwd, 0)
        dpre = dpre_s[...]
        dpb = dpre.astype(_CDT)
        dg_ref[...] = dg_s[...]
        dba_ref[...] = jnp.sum(dpre, axis=0, keepdims=True)
        dwa_ref[...] = _dot_tn(zga_ref[...], dpb)
        part = _dot_nt(dpb, wa_ref[...])

        @pl.when(h == 0)
        def _():
            dga_s[...] = part

        @pl.when(h > 0)
        def _():
            dga_s[...] += part

        @pl.when(h == GH // HP - 1)
        def _():
            dga_ref[...] = dga_s[...].astype(_CDT)

        _ride_wait(cps, GH // HP)

    ride = _ride_specs(send, 1)
    return pl.pallas_call(
        body, name="gla_bwd", grid=(GH // HP,),
        in_specs=[pl.BlockSpec((s, HP * GW), lambda h: (0, h)), pl.BlockSpec((s, GAP), lambda h: (0, 0)),
                  pl.BlockSpec((GAP, HP * GDK), lambda h: (0, h)), pl.BlockSpec((1, HP * GDK), lambda h: (0, h)),
                  pl.BlockSpec((1, HP * GDV), lambda h: (0, h)), pl.BlockSpec((s, HP * GDV), lambda h: (0, h))] + ride.in_specs,
        out_specs=[pl.BlockSpec((s, HP * GW), lambda h: (0, h)), pl.BlockSpec((s, GAP), lambda h: (0, 0)),
                   pl.BlockSpec((GAP, HP * GDK), lambda h: (0, h)), pl.BlockSpec((1, HP * GDK), lambda h: (0, h)),
                   pl.BlockSpec((1, HP * GDV), lambda h: (0, h))] + ride.out_specs,
        out_shape=[jax.ShapeDtypeStruct((s, ZG), _CDT), jax.ShapeDtypeStruct((s, GAP), _CDT),
                   jax.ShapeDtypeStruct((GAP, GH * GDK), F32), jax.ShapeDtypeStruct((1, GH * GDK), F32),
                   jax.ShapeDtypeStruct((1, DGLA), F32)] + ride.out_shape,
        scratch_shapes=[pltpu.VMEM((s, HP * GDK), F32), pltpu.VMEM((s, HP * GDK), F32), pltpu.VMEM((HP, nc, GDV, GDK), F32),
                        pltpu.VMEM((s, HP * GDK), F32), pltpu.VMEM((HP, GDV, GDK), F32), pltpu.VMEM((s, GAP), F32),
                        pltpu.VMEM((1, HP * GDV), F32)] + ride.scratch,
        compiler_params=_params(("arbitrary",)),
    )(z, zga, wa, ba, ggla, dcg, *ride.operands)


def _dh(dzg, dza, dga, wm, wga, x, dout, gpre, send=None, tm=512, tk=1024):
    s = x.shape[0]
    nkg, nk = ZG // tk, ZM // tk

    def body(*refs):
        dzg_ref, dza_ref, dga_ref, wm_ref, wga_ref, x_ref, d_ref, g_ref = refs[:8]
        dx_ref, dg_ref, acc = _ride_refs(refs[8:], send, 2)[1]
        i, k = pl.program_id(0), pl.program_id(1)
        cps = _ride_start(refs[8:], send, 2, 1, (i == 0) & (k == 0))

        @pl.when((i == 0) & (k == 0))
        def _():
            dg_ref[...] = jnp.zeros_like(dg_ref)

        @pl.when(k == 0)
        def _():
            acc[...] = _dot(dga_ref[...], wga_ref[...])

        @pl.when(k < nkg)
        def _():
            acc[...] += _dot(dzg_ref[...], wm_ref[...])

        @pl.when(k >= nkg)
        def _():
            acc[...] += _dot(dza_ref[...], wm_ref[...])

        @pl.when(k == nk - 1)
        def _():
            xv = x_ref[...]
            r = _rms_rows(xv)
            xh = xv * r
            dh = acc[...]
            dg_ref[...] += jnp.sum(dh * xh, axis=0, keepdims=True)
            dn = dh * g_ref[...]
            dx_ref[...] = d_ref[...] + r * (dn - xh * jnp.mean(dn * xh, axis=-1, keepdims=True))

        _ride_wait(cps, 0, (i == s // tm - 1) & (k == nk - 1))

    ride = _ride_specs(send, 1)
    return pl.pallas_call(
        body, name="dh", grid=(s // tm, nk),
        in_specs=[pl.BlockSpec((tm, tk), lambda i, k: (i, jnp.minimum(k, nkg - 1))),
                  pl.BlockSpec((tm, tk), lambda i, k: (i, jnp.maximum(k - nkg, 0))),
                  pl.BlockSpec((tm, GAP), lambda i, k: (i, 0)), pl.BlockSpec((tk, D), lambda i, k: (k, 0)),
                  pl.BlockSpec((GAP, D), lambda i, k: (0, 0)), pl.BlockSpec((tm, D), lambda i, k: (i, 0)),
                  pl.BlockSpec((tm, D), lambda i, k: (i, 0)), pl.BlockSpec((1, D), lambda i, k: (0, 0))] + ride.in_specs,
        out_specs=[pl.BlockSpec((tm, D), lambda i, k: (i, 0)), pl.BlockSpec((1, D), lambda i, k: (0, 0))] + ride.out_specs,
        out_shape=[jax.ShapeDtypeStruct((s, D), F32), jax.ShapeDtypeStruct((1, D), F32)] + ride.out_shape,
        scratch_shapes=[pltpu.VMEM((tm, D), F32)] + ride.scratch,
        compiler_params=_params(("arbitrary", "arbitrary")),
    )(dzg, dza, dga, wm, wga, x, dout, gpre, *ride.operands)


def _adam(w, g, m, v, tr, name):
    rws, cols = w.shape

    def body(w_ref, g_ref, m_ref, v_ref, d_ref, mo_ref, vo_ref):
        gv = g_ref[...]
        mn = ADAM_B1 * m_ref[...] + (1.0 - ADAM_B1) * gv
        vn = ADAM_B2 * v_ref[...] + (1.0 - ADAM_B2) * (gv * gv)
        mh = mn / (1.0 - ADAM_B1 ** ADAM_STEP)
        vh = vn / (1.0 - ADAM_B2 ** ADAM_STEP)
        d_ref[...] = -ADAM_LR * (mh / (jnp.sqrt(vh) + ADAM_EPS) + ADAM_WD * w_ref[...])
        mo_ref[...] = mn
        vo_ref[...] = vn

    spec = pl.BlockSpec((tr, cols), lambda i: (i, 0))
    return pl.pallas_call(
        body, name=name, grid=(rws // tr,), in_specs=[spec] * 4, out_specs=[spec] * 3,
        out_shape=[jax.ShapeDtypeStruct((rws, cols), F32)] * 3,
        compiler_params=_params(("parallel",)),
    )(w, g, m, v)


def _adam_rows(c_idx, w, g_own, g_other, m, v, tj, name):
    rows, nl = w.shape[:2]
    per_half = g_own.shape[1] // tj

    def body(c_ref, w_ref, go_ref, gx_ref, m_ref, v_ref, g_ref, d_ref, mo_ref, vo_ref):
        mine = pl.program_id(0) // per_half == c_ref[0]
        for l in range(nl):
            gv = jnp.where(mine, go_ref[l], gx_ref[l])
            mn = ADAM_B1 * m_ref[:, l, :] + (1.0 - ADAM_B1) * gv
            vn = ADAM_B2 * v_ref[:, l, :] + (1.0 - ADAM_B2) * (gv * gv)
            mh = mn / (1.0 - ADAM_B1 ** ADAM_STEP)
            vh = vn / (1.0 - ADAM_B2 ** ADAM_STEP)
            g_ref[:, l, :] = gv
            d_ref[:, l, :] = -ADAM_LR * (mh / (jnp.sqrt(vh) + ADAM_EPS) + ADAM_WD * w_ref[:, l, :])
            mo_ref[:, l, :] = mn
            vo_ref[:, l, :] = vn

    full = pl.BlockSpec((tj,) + w.shape[1:], lambda i, c_ref: (i, 0, 0))
    own = pl.BlockSpec((nl, tj, D), lambda i, c_ref: (0, jnp.where(i // per_half == c_ref[0], i % per_half, 0), 0))
    other = pl.BlockSpec((nl, tj, D), lambda i, c_ref: (0, jnp.where(i // per_half == c_ref[0], 0, i % per_half), 0))
    return pl.pallas_call(
        body, name=name,
        grid_spec=pltpu.PrefetchScalarGridSpec(num_scalar_prefetch=1, grid=(pl.cdiv(rows, tj),),
                                               in_specs=[full, own, other, full, full], out_specs=[full] * 4),
        out_shape=[jax.ShapeDtypeStruct(w.shape, F32)] * 4,
        compiler_params=_params(("parallel",)),
    )(c_idx, w, g_own, g_other, m, v)


def _gather_first(src_in, src_out, wa, buf_in, buf_out, wa_all):
    def variant(x, y, c, src_in_ref, src_out_ref, wa_ref, _b0, _b1, _b2, bin_ref, bout_ref, wa_ref_all, send_sems, recv_sems, wa_send, wa_recv):
        for phase in range(3):
            cps = _gather_copies(phase, x, y, c, src_in_ref, src_out_ref, bin_ref, bout_ref, send_sems.at[phase], recv_sems.at[phase])
            if phase == 0:
                for k, (px, py) in enumerate([(1 - x, y), (x, 1 - y), (1 - x, 1 - y)]):
                    cps.append(pltpu.make_async_remote_copy(src_ref=wa_ref, dst_ref=wa_ref_all.at[2 * x + y], send_sem=wa_send.at[k],
                                                            recv_sem=wa_recv.at[k], device_id=(px, py, c), device_id_type=MESH))
            for cpy in cps:
                cpy.start()
            for cpy in cps:
                cpy.wait()

    def body(*refs):
        _variants(lambda x, y, c: variant(x, y, c, *refs))

    return pl.pallas_call(
        body, name="gather_first", in_specs=[ANY] * 6, out_specs=[ANY] * 3, input_output_aliases={3: 0, 4: 1, 5: 2},
        out_shape=[jax.ShapeDtypeStruct(t.shape, t.dtype) for t in (buf_in, buf_out, wa_all)],
        scratch_shapes=[pltpu.SemaphoreType.DMA((3, 2, 3)), pltpu.SemaphoreType.DMA((3, 2, 3)),
                        pltpu.SemaphoreType.DMA((3,)), pltpu.SemaphoreType.DMA((3,))],
    )(src_in, src_out, wa, buf_in, buf_out, wa_all)


def _swap_halves(gin2, gout2):
    def body(gin_ref, gout_ref, rin, rout, send_sems, recv_sems):
        x, y, c, _ = _place()
        sib = (x, y, 1 - c)
        cps = [pltpu.make_async_remote_copy(src_ref=src.at[1 - c], dst_ref=dst, send_sem=send_sems.at[a],
                                            recv_sem=recv_sems.at[a], device_id=sib, device_id_type=MESH)
               for a, (src, dst) in enumerate([(gin_ref, rin), (gout_ref, rout)])]
        for cpy in cps:
            cpy.start()
        for cpy in cps:
            cpy.wait()

    return pl.pallas_call(
        body, name="swap_halves", in_specs=[ANY, ANY], out_specs=[ANY, ANY],
        out_shape=[jax.ShapeDtypeStruct(gin2.shape[1:], gin2.dtype), jax.ShapeDtypeStruct(gout2.shape[1:], gout2.dtype)],
        scratch_shapes=[pltpu.SemaphoreType.DMA((2,)), pltpu.SemaphoreType.DMA((2,))],
    )(gin2, gout2)


def _add_halves(c_idx, g2, r, tr, name):
    rows, cols = r.shape

    def body(c_ref, g_ref, r_ref, o_ref):
        o_ref[...] = (g_ref[0].astype(F32) + r_ref[...].astype(F32)).astype(_XDT)

    return pl.pallas_call(
        body, name=name,
        grid_spec=pltpu.PrefetchScalarGridSpec(
            num_scalar_prefetch=1, grid=(rows // tr,),
            in_specs=[pl.BlockSpec((1, tr, cols), lambda i, c_ref: (c_ref[0], i, 0)),
                      pl.BlockSpec((tr, cols), lambda i, c_ref: (i, 0))],
            out_specs=pl.BlockSpec((tr, cols), lambda i, c_ref: (i, 0))),
        out_shape=jax.ShapeDtypeStruct((rows, cols), _XDT),
        compiler_params=_params(("parallel",)),
    )(c_idx, g2, r)


def _send_to_owners(pin, pout, parts):
    def body(pin_ref, pout_ref, *refs):
        send_sems, recv_sems = refs[-2:]
        cps = []
        for n, part in enumerate(parts):
            cps += _owner_copies(part, pin_ref, pout_ref, refs[2 * n], refs[2 * n + 1], send_sems.at[n], recv_sems.at[n])
        for cpy in cps:
            cpy.start()
        for cpy in cps:
            cpy.wait()

    shapes = [jax.ShapeDtypeStruct((3, _part_rows(t.shape[2], split, part)[1], D), t.dtype)
              for part in parts for t, split in zip((pin, pout), (SPLIT_IN, SPLIT_OUT))]
    return pl.pallas_call(
        body, name="send_to_owners", in_specs=[ANY, ANY], out_specs=[ANY] * len(shapes), out_shape=shapes,
        scratch_shapes=[pltpu.SemaphoreType.DMA((len(parts), 2, 3)), pltpu.SemaphoreType.DMA((len(parts), 2, 3))],
    )(pin, pout)


def _add_chips(chip_idx, p, ra, rb, tr, name):
    rows = p.shape[2]
    na = ra.shape[1] // tr

    def body(c_ref, p_ref, ra_ref, rb_ref, o_ref):
        r = jnp.where(pl.program_id(0) < na, ra_ref[...], rb_ref[...]).astype(F32)
        o_ref[0] = ((p_ref[0, 0].astype(F32) + r[0]) + r[1]) + r[2]

    return pl.pallas_call(
        body, name=name,
        grid_spec=pltpu.PrefetchScalarGridSpec(
            num_scalar_prefetch=1, grid=(rows // tr,),
            in_specs=[pl.BlockSpec((1, 1, tr, D), lambda i, c_ref: (0, c_ref[0], i, 0)),
                      pl.BlockSpec((3, tr, D), lambda i, c_ref: (0, jnp.minimum(i, na - 1), 0)),
                      pl.BlockSpec((3, tr, D), lambda i, c_ref: (0, jnp.maximum(i - na, 0), 0))],
            out_specs=pl.BlockSpec((1, tr, D), lambda i, c_ref: (0, i, 0))),
        out_shape=jax.ShapeDtypeStruct((1, rows, D), F32),
        compiler_params=_params(("parallel",)),
    )(chip_idx, p, ra, rb)


def _exchange_halves(arrs):
    n = len(arrs)

    def body(*refs):
        x, y, c, _ = _place()
        cps = [pltpu.make_async_remote_copy(src_ref=refs[a], dst_ref=refs[n + a], send_sem=refs[2 * n].at[a], recv_sem=refs[2 * n + 1].at[a],
                                            device_id=(x, y, 1 - c), device_id_type=MESH) for a in range(n)]
        for cpy in cps:
            cpy.start()
        for cpy in cps:
            cpy.wait()

    return pl.pallas_call(
        body, name="exchange_halves", in_specs=[ANY] * n, out_specs=[ANY] * n,
        out_shape=[jax.ShapeDtypeStruct(t.shape, t.dtype) for t in arrs],
        scratch_shapes=[pltpu.SemaphoreType.DMA((n,)), pltpu.SemaphoreType.DMA((n,))],
    )(*arrs)


def _adam_halves(c_idx, w, g_own, g_other, m, v, tr, name):
    nl, _, rows, cols = w.shape

    def body(c_ref, w_ref, go_ref, gx_ref, m_ref, v_ref, g_ref, d_ref, mo_ref, vo_ref):
        gv = jnp.where(pl.program_id(1) == c_ref[0], go_ref[0], gx_ref[0])
        mn = ADAM_B1 * m_ref[0, 0] + (1.0 - ADAM_B1) * gv
        vn = ADAM_B2 * v_ref[0, 0] + (1.0 - ADAM_B2) * (gv * gv)
        mh = mn / (1.0 - ADAM_B1 ** ADAM_STEP)
        vh = vn / (1.0 - ADAM_B2 ** ADAM_STEP)
        g_ref[0, 0] = gv
        d_ref[0, 0] = -ADAM_LR * (mh / (jnp.sqrt(vh) + ADAM_EPS) + ADAM_WD * w_ref[0, 0])
        mo_ref[0, 0] = mn
        vo_ref[0, 0] = vn

    full = pl.BlockSpec((1, 1, tr, cols), lambda l, hh, i, c_ref: (l, hh, i, 0))
    own = pl.BlockSpec((1, tr, cols), lambda l, hh, i, c_ref: (l, jnp.where(hh == c_ref[0], i, 0), 0))
    other = pl.BlockSpec((1, tr, cols), lambda l, hh, i, c_ref: (l, jnp.where(hh == c_ref[0], 0, i), 0))
    return pl.pallas_call(
        body, name=name,
        grid_spec=pltpu.PrefetchScalarGridSpec(
            num_scalar_prefetch=1, grid=(nl, 2, rows // tr),
            in_specs=[full, own, other, full, full], out_specs=[full] * 4),
        out_shape=[jax.ShapeDtypeStruct(w.shape, F32)] * 4,
        compiler_params=_params(("parallel", "parallel", "parallel")),
    )(c_idx, w, g_own, g_other, m, v)


def _allreduce_small(sg):
    rows = sg.shape[0]
    vm = pl.BlockSpec(memory_space=pltpu.VMEM)

    def body(sg_ref, tot_ref, all_ref, send_sems, recv_sems):
        x, y, c, _ = _place()
        me = 4 * x + 2 * y + c
        all_ref[me] = sg_ref[...]
        cps = []
        for mask in range(1, 8):
            to = (1 - x if mask & 4 else x, 1 - y if mask & 2 else y, 1 - c if mask & 1 else c)
            cps.append(pltpu.make_async_remote_copy(src_ref=sg_ref, dst_ref=all_ref.at[me], send_sem=send_sems.at[mask - 1],
                                                    recv_sem=recv_sems.at[mask - 1], device_id=to, device_id_type=MESH))
        for cpy in cps:
            cpy.start()
        for cpy in cps:
            cpy.wait()
        acc = all_ref[0]
        for d in range(1, 8):
            acc = acc + all_ref[d]
        tot_ref[...] = acc

    return pl.pallas_call(
        body, name="allreduce_small", in_specs=[vm], out_specs=[vm, vm],
        out_shape=[jax.ShapeDtypeStruct((rows, 128), F32), jax.ShapeDtypeStruct((8, rows, 128), F32)],
        scratch_shapes=[pltpu.SemaphoreType.DMA((7,)), pltpu.SemaphoreType.DMA((7,))],
        compiler_params=_params(),
    )(sg)[0]


_CUTS = [0, 512, 1024, 2048, 3072, 3088, 4112, 5136, 6160, 7184]


def _rows_to_internal(w):
    tail = w.shape[1:]
    gq, gk, gv, gg, ga, aq, ak, av, ag = [w[_CUTS[i]:_CUTS[i + 1]] for i in range(9)]
    g = jnp.concatenate([gq.reshape((GH, GDK) + tail), gk.reshape((GH, GDK) + tail),
                         gv.reshape((GH, GDV) + tail), gg.reshape((GH, GDV) + tail)], axis=1).reshape((ZG,) + tail)
    a = jnp.concatenate([t.reshape((AH, AHD) + tail) for t in (aq, ak, av, ag)], axis=1).reshape((ZA,) + tail)
    pad = [(0, GAP - RANK)] + [(0, 0)] * len(tail)
    return jnp.concatenate([g, a], axis=0), jnp.pad(ga, pad)


def _rows_from_internal(g, a, ga):
    tail = g.shape[1:]
    g = g.reshape((GH, GW) + tail)
    a = a.reshape((AH, AW) + tail)
    parts = [g[:, 0:GDK], g[:, GDK:2 * GDK], g[:, 2 * GDK:2 * GDK + GDV], g[:, 2 * GDK + GDV:GW]]
    parts = [t.reshape((-1,) + tail) for t in parts] + [ga[0:RANK]]
    parts += [a[:, i * AHD:(i + 1) * AHD].reshape((-1,) + tail) for i in range(4)]
    return jnp.concatenate(parts, axis=0)


def _pack_rows(parts):
    rows = []
    for t in parts:
        flat = t.reshape(-1)
        rows.append(jnp.pad(flat, (0, (-flat.shape[0]) % 128)).reshape(-1, 128))
    buf = jnp.concatenate(rows, axis=0)
    return jnp.pad(buf, ((0, (-buf.shape[0]) % 8), (0, 0)))


def _unpack_rows(buf, shapes):
    out, r = [], 0
    for shp in shapes:
        n = 1
        for d in shp:
            n *= d
        nr = -(-n // 128)
        out.append(buf[r:r + nr].reshape(-1)[:n].reshape(shp))
        r += nr
    return out


def _layer_fwd(x, wm, wga, wout, gpre, gpost, wa, ba, ggla, gatt, rbx, ride=None):
    z, zga, h, *bufs = _inproj(x, gpre, wm, wga, ride)
    ride = None if ride is None else (ride[0], ride[1], *bufs)
    cg, = _gla_fwd(z, zga, wa, ba, ggla)
    ca, oraw, lse, *bufs = _att_fwd(z, rbx, gatt, ride)
    ride = None if ride is None else (ride[0], ride[1], *bufs)
    y, xo, *bufs = _outproj(cg, ca, wout, x, gpost, ride)
    return xo, (x, z, zga, h, cg, ca, oraw, lse, y), bufs


def _layer_bwd(dout, saved, wm, wga, wout, gpre, gpost, wa, ba, ggla, gatt, rbx, swap=None, finish=None, reduce_now=None):
    x, z, zga, h, cg, ca, oraw, lse, y = saved
    dy, dcg, dca, dgpost, *swapped = _post_bwd(dout, y, gpost, wout, swap)
    send = None if swap is None else finish(swap, swapped)
    dwout = jnp.concatenate([_matmul_tn(cg, dy, _XDT, 512, 1024, "dwout_gla"),
                             _matmul_tn(ca, dy, _XDT, 512, 1024, "dwout_att")], axis=0)
    dza, dgatt, dbx, *got_a = _att_bwd(z, oraw, lse, dca, rbx, gatt, send)
    dzg, dga, dwa, dba, dggla, *got_b = _gla_bwd(z, zga, wa, ba, ggla, dcg, send)
    dwin = (_matmul_tn(dzg, h, _XDT, 512, 1024, "dwin_gla"), _matmul_tn(dza, h, _XDT, 512, 1024, "dwin_att"),
            _matmul_tn(dga, h, _XDT, GAP, 1024, "dwin_gate"))
    own = None if reduce_now is None else reduce_now(dwin, dwout)
    dx, dgpre, *got_own = _dh(dzg, dza, dga, wm, wga, x, dout, gpre, own)
    drb = jnp.concatenate([jnp.zeros((AH, 1), F32), dbx[:, 0, ::-1]], axis=1)
    return dx, dwin, dwout, (dgpre[0], dgpost[0], dwa[0:RANK], dba[0], dggla[0], dgatt[0], drb), send, got_a + got_b, own, got_own


def _rel_rows(rb):
    return rb[:, :0:-1][:, None, :]


def kernel(x, w_in, w_out, g_pre, g_post, w_alpha, b_alpha, g_gla, g_att, rel_bias, loss_target, m_w_in, m_w_out, m_g_pre, m_g_post, m_w_alpha, m_b_alpha, m_g_gla, m_g_att, m_rel_bias, v_w_in, v_w_out, v_g_pre, v_g_post, v_w_alpha, v_b_alpha, v_g_gla, v_g_att, v_rel_bias):
    nl = w_in.shape[0]
    ax, ay, ac = lax.axis_index("x"), lax.axis_index("y"), lax.axis_index("c")
    chip = 2 * ax + ay
    c_idx = jnp.reshape(ac, (1,)).astype(jnp.int32)
    chip_idx = jnp.reshape(chip, (1,)).astype(jnp.int32)

    phase = [SHARD * i % 16 for i in range(NCHIP)]
    wt_rows = jnp.transpose(w_in, (0, 2, 1)).astype(_CDT)
    at_phase = [functools.partial(jnp.pad, wt_rows, ((0, 0), (p, WSLOT - SHARD - p), (0, 0))) for p in phase]
    wt_src = lax.switch(chip, at_phase).reshape(nl, 2, WSLOT // 2, D)
    wout_src = w_out.astype(_CDT).reshape(nl, 2, D // NCHIP // 2, D)

    def with_own(own):
        start = [chip] + [0] * own.ndim
        return lax.dynamic_update_slice(lax.empty((NCHIP,) + own.shape, own.dtype), own[None], start)

    def gather_operands(l):
        return wt_src[l], wout_src[l], with_own(wt_src[l]), with_own(wout_src[l])

    def layer_weights(bufs):
        wt4 = bufs[0].reshape(NCHIP, WSLOT, D)
        wref = jnp.concatenate([wt4[i, phase[i]:phase[i] + SHARD] for i in range(NCHIP)])
        return _rows_to_internal(wref) + (bufs[1].reshape(D, D),)

    first = gather_operands(0)
    bin0, bout0, wa_all = _gather_first(first[0], first[1], w_alpha, first[2], first[3], with_own(w_alpha))
    wa_full = jnp.transpose(wa_all, (1, 2, 0, 3)).reshape(nl, RANK, GH * GDK)
    wa_pad = jnp.pad(wa_full, ((0, 0), (0, GAP - RANK), (0, 0))).astype(_CDT)
    rbx = [_rel_rows(rel_bias[l]) for l in range(nl)]

    def weights(l):
        return big[l] + (g_pre[l][None], g_post[l][None], wa_pad[l], b_alpha[l][None], g_gla[l][None], g_att[l][None], rbx[l])

    h = x[0]
    saved, big = [], [None] * nl
    big[0] = layer_weights((bin0, bout0))
    for l in range(nl):
        h, sv, bufs = _layer_fwd(h, *weights(l), ride=gather_operands(l + 1) if l + 1 < nl else None)
        saved.append(sv)
        if l + 1 < nl:
            big[l + 1] = layer_weights(bufs)
    dout, loss_part = _loss_grad(h, loss_target[0])

    small, hin, hout = [None] * nl, [None] * nl, [None] * nl
    hw = D // NCHIP // 2

    def reduce_owner(sent, got):
        rin_a, rout_a, rin_b, rout_b = got
        return (_add_chips(chip_idx, sent[0], rin_a, rin_b, 48, "add_chips_in"),
                _add_chips(chip_idx, sent[1], rout_a, rout_b, 128, "add_chips_out"))

    def slab_halves(dwin, dwout):
        gt = jnp.pad(_rows_from_internal(*dwin), ((0, NCHIP * SHARD - SHARD + SLAB - DIN), (0, 0)))
        slabs = jnp.stack([gt[SHARD * i:SHARD * i + SLAB] for i in range(NCHIP)])
        return (jnp.transpose(slabs.reshape(NCHIP, 2, HSLAB, D), (1, 0, 2, 3)).reshape(2, NCHIP * HSLAB, D),
                jnp.transpose(dwout.reshape(NCHIP, 2, hw, D), (1, 0, 2, 3)).reshape(2, NCHIP * hw, D))

    def sum_halves(halves, swapped):
        return (_add_halves(c_idx, halves[0], swapped[0], 192, "add_halves_in").reshape(1, NCHIP, HSLAB, D),
                _add_halves(c_idx, halves[1], swapped[1], 256, "add_halves_out").reshape(1, NCHIP, hw, D))

    def partial_sums(dwin, dwout):
        halves = slab_halves(dwin, dwout)
        return sum_halves(halves, _swap_halves(*halves))

    halves = None
    for l in reversed(range(nl)):
        dout, dwin, dwout, small[l], sent, got, own, got_own = _layer_bwd(
            dout, saved[l], *weights(l), swap=halves, finish=sum_halves, reduce_now=partial_sums if l == 0 else None)
        if halves is not None:
            hin[l + 1], hout[l + 1] = reduce_owner(sent, got)
        halves = slab_halves(dwin, dwout) if l > 0 else None
    sent = own
    grad_x = dout[None]
    hin[0], hout[0] = reduce_owner(sent, list(_send_to_owners(*sent, parts=(0,))) + list(got_own))
    xchg = _exchange_halves(hin + hout)
    hin, xin = jnp.concatenate(hin), jnp.concatenate(xchg[:nl])
    hout, xout = jnp.concatenate(hout), jnp.concatenate(xchg[nl:])

    rows_first = lambda t: jnp.transpose(t, (2, 0, 1))
    w_in_out = _adam_rows(c_idx, rows_first(w_in), hin, xin, rows_first(m_w_in), rows_first(v_w_in), 48, "adam_w_in")
    g_w_in, d_w_in, nm_w_in, nv_w_in = [jnp.transpose(t, (1, 2, 0)) for t in w_in_out]

    def adam_big(w, g_own, g_other, m, v, name):
        shp = w.shape
        halves = lambda t: t.reshape(shp[0], 2, shp[1] // 2, shp[2])
        return [t.reshape(shp) for t in _adam_halves(c_idx, halves(w), g_own, g_other, halves(m), halves(v), 256, name)]

    g_w_out, d_w_out, nm_w_out, nv_w_out = adam_big(w_out, hout, xout, m_w_out, v_w_out, "adam_w_out")

    stacked = [jnp.stack([small[l][i] for l in range(nl)]) for i in range(7)] + [loss_part]
    g_small = _unpack_rows(_allreduce_small(_pack_rows(stacked)), [t.shape for t in stacked])
    g_gpre, g_gpost, g_wa_full, g_ba, g_ggla, g_gatt, g_rb, loss_sum = g_small
    loss = loss_sum[0, 0]
    g_wa = lax.dynamic_slice_in_dim(g_wa_full, chip * GDK, GDK, axis=2)
    names = [(g_pre, m_g_pre, v_g_pre, g_gpre), (g_post, m_g_post, v_g_post, g_gpost), (w_alpha, m_w_alpha, v_w_alpha, g_wa),
             (b_alpha, m_b_alpha, v_b_alpha, g_ba), (g_gla, m_g_gla, v_g_gla, g_ggla), (g_att, m_g_att, v_g_att, g_gatt),
             (rel_bias, m_rel_bias, v_rel_bias, g_rb)]
    shapes = [t[0].shape for t in names]
    packed = [_pack_rows([t[i] for t in names]) for i in range(4)]
    d_s, nm_s, nv_s = [_unpack_rows(t, shapes) for t in _adam(packed[0], packed[3], packed[1], packed[2], packed[0].shape[0], "adam_small")]

    grads = [g_w_in, g_w_out, g_gpre, g_gpost, g_wa, g_ba, g_ggla, g_gatt, g_rb]
    deltas = [d_w_in, d_w_out] + d_s
    new_m = [nm_w_in, nm_w_out] + nm_s
    new_v = [nv_w_in, nv_w_out] + nv_s
    return (loss, grad_x, *grads, *deltas, *new_m, *new_v)
```

```python
import functools

import jax
import jax.numpy as jnp
from jax import lax
from jax.experimental import pallas as pl
from jax.experimental.pallas import tpu as pltpu

D = 2048
DEPTH = 4
CHUNK = 64
GH, GDK, GDV = 4, 128, 256
DGLA = GH * GDV
RANK = 16
TAU = 16.0
AH, AHD = 8, 128
DATT = AH * AHD
LEFT = 8
NREL = 257
EPS = 1e-6
DIN = 7184
ADAM_LR, ADAM_B1, ADAM_B2, ADAM_EPS, ADAM_WD, ADAM_STEP = 0.001, 0.9, 0.999, 1e-08, 0.01, 10

GW = 2 * GDK + 2 * GDV
AW = 4 * AHD
ZG = GH * GW
ZA = AH * AW
ZM = ZG + ZA
GAP = 128
QB = 2 * CHUNK
HP = 2
BANDW = (LEFT + 2) * CHUNK
PADK = LEFT * CHUNK
NCHIP = 4
SHARD = DIN // NCHIP
SLAB = 1824
HSLAB = SLAB // 2
WSLOT = 1824
GSUB_IN = [(0, 464), (464, 448)]
GSUB_OUT = [(0, 128), (128, 128)]
GP0_IN = [(0, 688), (688, 224)]
GP0_OUT = [(0, 192), (192, 64)]
SPLIT_IN, SPLIT_OUT = 336, 128
NEG = -1e30
F32 = jnp.float32
_CDT = jnp.bfloat16
_XDT = jnp.bfloat16
_VMEM = 56 * 1024 * 1024
MESH = pl.DeviceIdType.MESH
ANY = pl.BlockSpec(memory_space=pl.ANY)


def _dot(a, b):
    return jnp.dot(a, b, preferred_element_type=F32)


def _dot_nt(a, b):
    return lax.dot_general(a, b, (((1,), (1,)), ((), ())), preferred_element_type=F32)


def _dot_tn(a, b):
    return lax.dot_general(a, b, (((0,), (0,)), ((), ())), preferred_element_type=F32)


def _rms_rows(v):
    return lax.rsqrt(jnp.mean(v * v, axis=-1, keepdims=True) + EPS)


def _sigmoid(v):
    return 1.0 / (1.0 + jnp.exp(-v))


def _log_sigmoid(v):
    return jnp.minimum(v, 0.0) - jnp.log(1.0 + jnp.exp(-jnp.abs(v)))


def _exact_dot(tri, v):
    hi = v.astype(_CDT)
    r1 = v - hi.astype(F32)
    mid = r1.astype(_CDT)
    lo = (r1 - mid.astype(F32)).astype(_CDT)
    return _dot(tri, hi) + _dot(tri, mid) + _dot(tri, lo)


def _tri(strict):
    row = lax.broadcasted_iota(jnp.int32, (CHUNK, CHUNK), 0)
    col = lax.broadcasted_iota(jnp.int32, (CHUNK, CHUNK), 1)
    return jnp.where((col < row) if strict else (col <= row), 1.0, 0.0).astype(_CDT)


def _norm_gate_bwd(o, g, gate, dcat):
    r = _rms_rows(o)
    oh = o * r
    sg = _sigmoid(gate)
    dn = dcat * (gate * sg)
    dgate = dcat * (oh * g) * (sg * (1.0 + gate * (1.0 - sg)))
    dg = jnp.sum(dn * oh, axis=0, keepdims=True)
    dnn = dn * g
    do = r * (dnn - oh * jnp.mean(dnn * oh, axis=-1, keepdims=True))
    return do, dgate, dg


def _params(sem=None, vmem=_VMEM):
    return pltpu.CompilerParams(dimension_semantics=sem, vmem_limit_bytes=vmem)


def _inproj(x, g, wm, wga, ride=None, tm=512, tn=1024):
    s = x.shape[0]
    gr = _gather_ride(ride, 0)

    def body(*refs):
        (x_ref, g_ref, wm_ref, wga_ref, z_ref, zga_ref, h_ref, hs), rr = gr.split(refs, 4, 3)
        i, j = pl.program_id(0), pl.program_id(1)
        gr.start(rr, (i == 0) & (j == 0))

        @pl.when(pl.program_id(1) == 0)
        def _():
            xv = x_ref[...]
            hv = (xv * _rms_rows(xv) * g_ref[...]).astype(_CDT)
            hs[...] = hv
            h_ref[...] = hv
            zga_ref[...] = _dot_nt(hv, wga_ref[...]).astype(_CDT)

        z_ref[...] = _dot_nt(hs[...], wm_ref[...]).astype(_CDT)
        gr.wait(rr, (i == s // tm - 1) & (j == ZM // tn - 1))

    return pl.pallas_call(
        body, name="inproj", grid=(s // tm, ZM // tn),
        in_specs=[pl.BlockSpec((tm, D), lambda i, j: (i, 0)), pl.BlockSpec((1, D), lambda i, j: (0, 0)),
                  pl.BlockSpec((tn, D), lambda i, j: (j, 0)), pl.BlockSpec((GAP, D), lambda i, j: (0, 0))] + gr.in_specs,
        out_specs=[pl.BlockSpec((tm, tn), lambda i, j: (i, j)), pl.BlockSpec((tm, GAP), lambda i, j: (i, 0)),
                   pl.BlockSpec((tm, D), lambda i, j: (i, 0))] + gr.out_specs,
        out_shape=[jax.ShapeDtypeStruct((s, ZM), _CDT), jax.ShapeDtypeStruct((s, GAP), _CDT),
                   jax.ShapeDtypeStruct((s, D), _CDT)] + gr.out_shape,
        scratch_shapes=[pltpu.VMEM((tm, D), _CDT)] + gr.scratch, input_output_aliases=gr.alias(4, 3),
        compiler_params=_params(("arbitrary", "arbitrary")),
    )(x, g, wm, wga, *gr.operands)


def _gla_fwd(z, zga, wa, ba, ggla, ride=None):
    s = z.shape[0]
    nc = s // CHUNK
    gr = _gather_ride(ride, 3)

    def body(*refs):
        (zg_ref, zga_ref, wa_ref, ba_ref, g_ref, cat_ref, la_s, st), rr = gr.split(refs, 5, 1)
        gr.start(rr, pl.program_id(0) == 0)
        la_s[...] = _log_sigmoid(_dot(zga_ref[...], wa_ref[...]) + ba_ref[...]) * (1.0 / TAU)
        st[...] = jnp.zeros_like(st)
        tri = _tri(False)

        def step(n, carry):
            rows = pl.ds(pl.multiple_of(n * CHUNK, CHUNK), CHUNK)
            for p in range(HP):
                z0 = p * GW
                la = la_s[rows, p * GDK:(p + 1) * GDK]
                lc = _exact_dot(tri, la)
                lend = jnp.sum(la, axis=0, keepdims=True)
                kdec = (zg_ref[rows, z0 + GDK:z0 + 2 * GDK].astype(F32) * jnp.exp(lend - lc)).astype(_CDT)
                stn = jnp.exp(lend) * st[p] + _dot_tn(zg_ref[rows, z0 + 2 * GDK:z0 + 2 * GDK + GDV], kdec)
                st[p] = stn
                qs = (zg_ref[rows, z0:z0 + GDK].astype(F32) * (GDK ** -0.5)).astype(_CDT)
                o = _dot_nt(qs, stn.astype(_CDT))
                gate = zg_ref[rows, z0 + 2 * GDK + GDV:z0 + GW].astype(F32)
                gain = g_ref[:, p * GDV:(p + 1) * GDV]
                cat_ref[rows, p * GDV:(p + 1) * GDV] = (o * _rms_rows(o) * gain * (gate * _sigmoid(gate))).astype(_CDT)
            return carry

        lax.fori_loop(0, nc, step, 0, unroll=4)
        gr.wait(rr, pl.program_id(0) == GH // HP - 1)

    return pl.pallas_call(
        body, name="gla_fwd", grid=(GH // HP,),
        in_specs=[pl.BlockSpec((s, HP * GW), lambda h: (0, h)), pl.BlockSpec((s, GAP), lambda h: (0, 0)),
                  pl.BlockSpec((GAP, HP * GDK), lambda h: (0, h)), pl.BlockSpec((1, HP * GDK), lambda h: (0, h)),
                  pl.BlockSpec((1, HP * GDV), lambda h: (0, h))] + gr.in_specs,
        out_specs=[pl.BlockSpec((s, HP * GDV), lambda h: (0, h))] + gr.out_specs,
        out_shape=[jax.ShapeDtypeStruct((s, DGLA), _CDT)] + gr.out_shape,
        scratch_shapes=[pltpu.VMEM((s, HP * GDK), F32), pltpu.VMEM((HP, GDV, GDK), F32)] + gr.scratch,
        input_output_aliases=gr.alias(5, 1), compiler_params=_params(("arbitrary",)),
    )(z, zga, wa, ba, ggla, *gr.operands)


def _band_bias(b0):
    row = lax.broadcasted_iota(jnp.int32, (QB, 256), 0)
    col = lax.broadcasted_iota(jnp.int32, (QB, 256), 1)
    lane = lax.broadcasted_iota(jnp.int32, (1, 256), 1)
    c0 = jnp.sum(jnp.where(lane == 0, b0, 0.0), axis=1, keepdims=True)
    xv = jnp.broadcast_to(b0, (QB, 256))
    for bit in range(7):
        xv = jnp.where(((row >> bit) & 1) == 1, pltpu.roll(xv, 1 << bit, 1), xv)
    xv = jnp.where(col < row, c0, xv)
    return jnp.concatenate([jnp.broadcast_to(c0, (QB, BANDW - 256)), xv], axis=1)


def _band_static_mask():
    row = lax.broadcasted_iota(jnp.int32, (QB, BANDW), 0) >> 6
    col = lax.broadcasted_iota(jnp.int32, (QB, BANDW), 1) >> 6
    return (col >= row) & (col <= row + LEFT)


def _fold_bias_grad(t):
    row = lax.broadcasted_iota(jnp.int32, (QB, 256), 0)
    col = lax.broadcasted_iota(jnp.int32, (QB, 256), 1)
    xv = t[:, BANDW - 256:]
    low = col < row
    far = jnp.sum(t[:, 0:BANDW - 256], axis=1, keepdims=True) + jnp.sum(jnp.where(low, xv, 0.0), axis=1, keepdims=True)
    far = jnp.sum(far, axis=0, keepdims=True)
    xv = jnp.where(low, 0.0, xv)
    for bit in range(7):
        xv = jnp.where(((row >> bit) & 1) == 1, pltpu.roll(xv, 256 - (1 << bit), 1), xv)
    dp = jnp.sum(xv, axis=0, keepdims=True)
    lane = lax.broadcasted_iota(jnp.int32, (1, 256), 1)
    return dp + jnp.where(lane == 0, far, 0.0)


def _att_fwd(z, rbx, gatt, ride=None):
    s = z.shape[0]
    nb = s // QB
    gr = _gather_ride(ride, 1)

    def body(*refs):
        (za_ref, rb_ref, g_ref, cat_ref, o_ref, lse_ref, kp, vp, bias_s), rr = gr.split(refs, 3, 3)
        gr.start(rr, pl.program_id(0) == 0)
        for p in range(HP):
            z0 = p * AW
            kp[p, 0:PADK, :] = jnp.zeros((PADK, AHD), _CDT)
            vp[p, 0:PADK, :] = jnp.zeros((PADK, AHD), _CDT)
            kp[p, PADK:, :] = za_ref[:, z0 + AHD:z0 + 2 * AHD]
            vp[p, PADK:, :] = za_ref[:, z0 + 2 * AHD:z0 + 3 * AHD]
            bias_s[p] = jnp.where(_band_static_mask(), _band_bias(rb_ref[p]), NEG)

        def step(b, carry):
            r0 = pl.multiple_of(b * QB, QB)
            rows = pl.ds(r0, QB)
            band = pl.ds(r0, BANDW)
            live = lax.broadcasted_iota(jnp.int32, (QB, BANDW), 1) >= PADK - r0
            for p in range(HP):
                z0 = p * AW
                cols = slice(p * AHD, (p + 1) * AHD)
                sc = _dot_nt(za_ref[rows, z0:z0 + AHD], kp[p, band, :]) * (AHD ** -0.5) + bias_s[p]
                sc = jnp.where(live, sc, NEG)
                m = jnp.max(sc, axis=-1, keepdims=True)
                pr = jnp.exp(sc - m)
                l = jnp.sum(pr, axis=-1, keepdims=True)
                o = _dot((pr * (1.0 / l)).astype(_CDT), vp[p, band, :])
                o_ref[rows, cols] = o.astype(_CDT)
                lse_ref[rows, cols] = jnp.broadcast_to(m + jnp.log(l), (QB, AHD))
                gate = za_ref[rows, z0 + 3 * AHD:z0 + AW].astype(F32)
                cat_ref[rows, cols] = (o * _rms_rows(o) * g_ref[:, cols] * (gate * _sigmoid(gate))).astype(_CDT)
            return carry

        lax.fori_loop(0, nb, step, 0, unroll=4)
        gr.wait(rr, pl.program_id(0) == AH // HP - 1)

    return pl.pallas_call(
        body, name="att_fwd", grid=(AH // HP,),
        in_specs=[pl.BlockSpec((s, HP * AW), lambda h: (0, ZG // (HP * AW) + h)), pl.BlockSpec((HP, 1, 256), lambda h: (h, 0, 0)),
                  pl.BlockSpec((1, HP * AHD), lambda h: (0, h))] + gr.in_specs,
        out_specs=[pl.BlockSpec((s, HP * AHD), lambda h: (0, h)), pl.BlockSpec((s, HP * AHD), lambda h: (0, h)),
                   pl.BlockSpec((s, HP * AHD), lambda h: (0, h))] + gr.out_specs,
        out_shape=[jax.ShapeDtypeStruct((s, DATT), _CDT), jax.ShapeDtypeStruct((s, DATT), _CDT),
                   jax.ShapeDtypeStruct((s, DATT), F32)] + gr.out_shape,
        scratch_shapes=[pltpu.VMEM((HP, s + PADK, AHD), _CDT), pltpu.VMEM((HP, s + PADK, AHD), _CDT),
                        pltpu.VMEM((HP, QB, BANDW), F32)] + gr.scratch, input_output_aliases=gr.alias(3, 3),
        compiler_params=_params(("arbitrary",)),
    )(z, rbx, gatt, *gr.operands)


def _outproj(cg, ca, wout, x, gpost, ride=None, tm=256):
    s = x.shape[0]
    gr = _gather_ride(ride, 2)

    def body(*refs):
        (cg_ref, ca_ref, w_ref, x_ref, g_ref, y_ref, xo_ref), rr = gr.split(refs, 5, 2)
        gr.start(rr, pl.program_id(0) == 0)
        y = _dot(cg_ref[...], w_ref[0:DGLA, :]) + _dot(ca_ref[...], w_ref[DGLA:, :])
        y_ref[...] = y
        xo_ref[...] = x_ref[...] + y * _rms_rows(y) * g_ref[...]
        gr.wait(rr, pl.program_id(0) == s // tm - 1)

    return pl.pallas_call(
        body, name="outproj", grid=(s // tm,),
        in_specs=[pl.BlockSpec((tm, DGLA), lambda i: (i, 0)), pl.BlockSpec((tm, DATT), lambda i: (i, 0)),
                  pl.BlockSpec((D, D), lambda i: (0, 0)), pl.BlockSpec((tm, D), lambda i: (i, 0)),
                  pl.BlockSpec((1, D), lambda i: (0, 0))] + gr.in_specs,
        out_specs=[pl.BlockSpec((tm, D), lambda i: (i, 0)), pl.BlockSpec((tm, D), lambda i: (i, 0))] + gr.out_specs,
        out_shape=[jax.ShapeDtypeStruct((s, D), F32), jax.ShapeDtypeStruct((s, D), F32)] + gr.out_shape,
        scratch_shapes=gr.scratch, input_output_aliases=gr.alias(5, 2),
        compiler_params=_params(("arbitrary",)),
    )(cg, ca, wout, x, gpost, *gr.operands)


def _loss_grad(xo, tgt, tm=256):
    s = xo.shape[0]

    def body(xo_ref, t_ref, d_ref, l_ref):
        @pl.when(pl.program_id(0) == 0)
        def _():
            l_ref[...] = jnp.zeros_like(l_ref)

        e = xo_ref[...] - t_ref[...]
        d_ref[...] = e * (1.0 / D)
        l_ref[...] += jnp.sum(jnp.sum(e * e, axis=1, keepdims=True), axis=0, keepdims=True) * (0.5 / D)

    return pl.pallas_call(
        body, name="loss_grad", grid=(s // tm,),
        in_specs=[pl.BlockSpec((tm, D), lambda i: (i, 0)), pl.BlockSpec((tm, D), lambda i: (i, 0))],
        out_specs=[pl.BlockSpec((tm, D), lambda i: (i, 0)), pl.BlockSpec((1, 1), lambda i: (0, 0))],
        out_shape=[jax.ShapeDtypeStruct((s, D), F32), jax.ShapeDtypeStruct((1, 1), F32)],
        compiler_params=_params(("arbitrary",)),
    )(xo, tgt)


def _place():
    x, y, c = lax.axis_index("x"), lax.axis_index("y"), lax.axis_index("c")
    chips = [(1 - x, y), (x, 1 - y), (1 - x, 1 - y)]
    return x, y, c, chips


def _variants(fn):
    x, y, c, _ = _place()
    for jx in range(2):
        for jy in range(2):
            for jc in range(2):
                pl.when((x == jx) & (y == jy) & (c == jc))(functools.partial(fn, jx, jy, jc))


def _gather_copies(part, x, y, c, src_in, src_out, buf_in, buf_out, send_sems, recv_sems):
    me = 2 * x + y
    xn, yn, dg = (1 - x, y), (x, 1 - y), (1 - x, 1 - y)
    cps = []
    for a, (src, buf, subs, firsts) in enumerate([(src_in, buf_in, GSUB_IN, GP0_IN), (src_out, buf_out, GSUB_OUT, GP0_OUT)]):
        if part in (0, 3):
            rows = pl.ds(*firsts[0 if part == 0 else 1])
            pairs = [(src.at[c, rows], buf.at[me, c, rows], (p[0], p[1], c)) for p in (xn, yn)]
        elif part == 1:
            from_x = buf.at[2 * xn[0] + xn[1], c, pl.ds(*subs[0])]
            from_y = buf.at[2 * yn[0] + yn[1], c, pl.ds(*subs[1])]
            pairs = [(from_x, from_x, (yn[0], yn[1], c)), (from_y, from_y, (xn[0], xn[1], c))]
        else:
            pairs = [(buf.at[2 * p[0] + p[1], c], buf.at[2 * p[0] + p[1], c], (x, y, 1 - c)) for p in (xn, yn, dg)]
        for k, (src_k, dst_k, to) in enumerate(pairs):
            cps.append(pltpu.make_async_remote_copy(src_ref=src_k, dst_ref=dst_k, send_sem=send_sems.at[a, k],
                                                    recv_sem=recv_sems.at[a, k], device_id=to, device_id_type=MESH))
    return cps


class _gather_ride:
    def __init__(self, ride, part):
        self.part, self.on = part, ride is not None
        self.in_specs, self.out_specs, self.out_shape, self.scratch, self.operands, self.aliases = [], [], [], [], [], {}
        if self.on:
            self.operands = list(ride)
            self.in_specs, self.out_specs = [ANY] * 4, [ANY] * 2
            self.out_shape = [jax.ShapeDtypeStruct(t.shape, t.dtype) for t in ride[2:]]
            self.scratch = [pltpu.SemaphoreType.DMA((2, 3)), pltpu.SemaphoreType.DMA((2, 3))]

    def alias(self, n_in, n_out):
        return {n_in + 2: n_out, n_in + 3: n_out + 1} if self.on else {}

    def split(self, refs, n_in, n_out):
        if not self.on:
            return refs, None
        own = refs[:n_in] + refs[n_in + 4:n_in + 4 + n_out] + refs[n_in + 6 + n_out:-2]
        return own, refs[n_in:n_in + 2] + refs[n_in + 4 + n_out:n_in + 6 + n_out] + refs[-2:]

    def start(self, ride_refs, first):
        if self.on:
            def go(x, y, c):
                for cpy in _gather_copies(self.part, x, y, c, *ride_refs):
                    cpy.start()
            pl.when(first)(lambda: _variants(go))

    def wait(self, ride_refs, last):
        if self.on:
            def done(x, y, c):
                for cpy in _gather_copies(self.part, x, y, c, *ride_refs):
                    cpy.wait()
            pl.when(last)(lambda: _variants(done))


def _part_rows(total, split, part):
    return (0, split) if part == 0 else (split, total - split)


def _owner_copies(part, pin_ref, pout_ref, rin, rout, send_sems, recv_sems):
    x, y, c, chips = _place()
    cps = []
    for k, (px, py) in enumerate(chips):
        for a, (src, dst, split) in enumerate([(pin_ref, rin, SPLIT_IN), (pout_ref, rout, SPLIT_OUT)]):
            r0, n = _part_rows(src.shape[2], split, part)
            cps.append(pltpu.make_async_remote_copy(src_ref=src.at[0, 2 * px + py, pl.ds(r0, n)], dst_ref=dst.at[k],
                                                    send_sem=send_sems.at[a, k], recv_sem=recv_sems.at[a, k],
                                                    device_id=(px, py, c), device_id_type=MESH))
    return cps


class _ride_specs:
    def __init__(self, send, part):
        self.in_specs, self.out_specs, self.out_shape, self.scratch, self.operands = [], [], [], [], []
        if send is not None:
            self.in_specs, self.out_specs, self.operands = [ANY, ANY], [ANY, ANY], list(send)
            self.out_shape = [jax.ShapeDtypeStruct((3, _part_rows(t.shape[2], split, part)[1], D), t.dtype)
                              for t, split in zip(send, (SPLIT_IN, SPLIT_OUT))]
            self.scratch = [pltpu.SemaphoreType.DMA((2, 3)), pltpu.SemaphoreType.DMA((2, 3))]


def _ride_refs(refs, send, n_out):
    if send is None:
        return None, refs
    pin_ref, pout_ref = refs[:2]
    own_out = refs[2:2 + n_out]
    rin, rout = refs[2 + n_out:4 + n_out]
    return (pin_ref, pout_ref, rin, rout, refs[-2], refs[-1]), tuple(own_out) + tuple(refs[4 + n_out:-2])


def _ride_start(refs, send, n_out, part, first=None):
    ride = _ride_refs(refs, send, n_out)[0]
    if ride is None:
        return []
    cps = _owner_copies(part, *ride)

    @pl.when(pl.program_id(0) == 0 if first is None else first)
    def _():
        for cpy in cps:
            cpy.start()

    return cps


def _ride_wait(cps, steps, last=None):
    if cps:
        @pl.when(pl.program_id(0) == steps - 1 if last is None else last)
        def _():
            for cpy in cps:
                cpy.wait()


def _post_bwd(dout, y, gpost, wout, swap=None, tm=256):
    s = y.shape[0]

    def body(*refs):
        d_ref, y_ref, g_ref, w_ref = refs[:4]
        dy_ref, dcg_ref, dca_ref, dg_ref = refs[6:10] if swap is not None else refs[4:8]
        cps = []
        if swap is not None:
            x, yy, c, _ = _place()
            cps = [pltpu.make_async_remote_copy(src_ref=refs[4 + a].at[1 - c], dst_ref=refs[10 + a], send_sem=refs[12].at[a],
                                                recv_sem=refs[13].at[a], device_id=(x, yy, 1 - c), device_id_type=MESH) for a in range(2)]

            @pl.when(pl.program_id(0) == 0)
            def _():
                for cpy in cps:
                    cpy.start()

        @pl.when(pl.program_id(0) == 0)
        def _():
            dg_ref[...] = jnp.zeros_like(dg_ref)

        yv = y_ref[...]
        r = _rms_rows(yv)
        yh = yv * r
        dv = d_ref[...]
        dg_ref[...] += jnp.sum(dv * yh, axis=0, keepdims=True)
        dn = dv * g_ref[...]
        dyb = (r * (dn - yh * jnp.mean(dn * yh, axis=-1, keepdims=True))).astype(_CDT)
        dy_ref[...] = dyb
        dcg_ref[...] = _dot_nt(dyb, w_ref[0:DGLA, :]).astype(_CDT)
        dca_ref[...] = _dot_nt(dyb, w_ref[DGLA:, :]).astype(_CDT)
        _ride_wait(cps, s // tm)

    extra = [] if swap is None else list(swap)
    return pl.pallas_call(
        body, name="post_bwd", grid=(s // tm,),
        in_specs=[pl.BlockSpec((tm, D), lambda i: (i, 0)), pl.BlockSpec((tm, D), lambda i: (i, 0)),
                  pl.BlockSpec((1, D), lambda i: (0, 0)), pl.BlockSpec((D, D), lambda i: (0, 0))] + [ANY] * len(extra),
        out_specs=[pl.BlockSpec((tm, D), lambda i: (i, 0)), pl.BlockSpec((tm, DGLA), lambda i: (i, 0)),
                   pl.BlockSpec((tm, DATT), lambda i: (i, 0)), pl.BlockSpec((1, D), lambda i: (0, 0))] + [ANY] * len(extra),
        out_shape=[jax.ShapeDtypeStruct((s, D), _CDT), jax.ShapeDtypeStruct((s, DGLA), _CDT),
                   jax.ShapeDtypeStruct((s, DATT), _CDT), jax.ShapeDtypeStruct((1, D), F32)]
        + [jax.ShapeDtypeStruct(t.shape[1:], t.dtype) for t in extra],
        scratch_shapes=[pltpu.SemaphoreType.DMA((2,)), pltpu.SemaphoreType.DMA((2,))] if extra else [],
        compiler_params=_params(("arbitrary",)),
    )(dout, y, gpost, wout, *extra)


def _matmul_tn(a, b, out_dtype, tm, tn, name):
    k, m = a.shape
    n = b.shape[1]

    def body(a_ref, b_ref, o_ref):
        o_ref[...] = _dot_tn(a_ref[...], b_ref[...]).astype(out_dtype)

    return pl.pallas_call(
        body, name=name, grid=(m // tm, n // tn),
        in_specs=[pl.BlockSpec((k, tm), lambda i, j: (0, i)), pl.BlockSpec((k, tn), lambda i, j: (0, j))],
        out_specs=pl.BlockSpec((tm, tn), lambda i, j: (i, j)),
        out_shape=jax.ShapeDtypeStruct((m, n), out_dtype),
        compiler_params=_params(("parallel", "parallel")),
    )(a, b)


def _att_bwd(z, oraw, lse, dca, rbx, gatt, send=None):
    s = z.shape[0]
    nb = s // QB

    def body(*refs):
        za_ref, o_ref, lse_ref, dc_ref, rb_ref, g_ref = refs[:6]
        dz_ref, dg_ref, db_ref, kp, vp, dkp, dvp, bias_s, t_s, dg_s = _ride_refs(refs[6:], send, 3)[1]
        cps = _ride_start(refs[6:], send, 3, 0)
        for p in range(HP):
            z0 = p * AW
            kp[p, 0:PADK, :] = jnp.zeros((PADK, AHD), _CDT)
            vp[p, 0:PADK, :] = jnp.zeros((PADK, AHD), _CDT)
            kp[p, PADK:, :] = za_ref[:, z0 + AHD:z0 + 2 * AHD]
            vp[p, PADK:, :] = za_ref[:, z0 + 2 * AHD:z0 + 3 * AHD]
            bias_s[p] = jnp.where(_band_static_mask(), _band_bias(rb_ref[p]), NEG)
        dkp[...] = jnp.zeros_like(dkp)
        dvp[...] = jnp.zeros_like(dvp)
        t_s[...] = jnp.zeros_like(t_s)
        dg_s[...] = jnp.zeros_like(dg_s)

        def step(b, carry):
            r0 = pl.multiple_of(b * QB, QB)
            rows = pl.ds(r0, QB)
            band = pl.ds(r0, BANDW)
            live = lax.broadcasted_iota(jnp.int32, (QB, BANDW), 1) >= PADK - r0
            for p in range(HP):
                z0 = p * AW
                cols = slice(p * AHD, (p + 1) * AHD)
                o = o_ref[rows, cols].astype(F32)
                do, dgate, dg = _norm_gate_bwd(o, g_ref[:, cols], za_ref[rows, z0 + 3 * AHD:z0 + AW].astype(F32),
                                               dc_ref[rows, cols].astype(F32))
                dg_s[:, cols] += dg
                q = za_ref[rows, z0:z0 + AHD]
                kb = kp[p, band, :]
                sc = _dot_nt(q, kb) * (AHD ** -0.5) + bias_s[p]
                sc = jnp.where(live, sc, NEG)
                pr = jnp.exp(sc - jnp.max(lse_ref[rows, cols], axis=-1, keepdims=True))
                dob = do.astype(_CDT)
                dp = _dot_nt(dob, vp[p, band, :])
                ds = pr * (dp - jnp.sum(do * o, axis=-1, keepdims=True))
                t_s[p] += ds
                dsb = (ds * (AHD ** -0.5)).astype(_CDT)
                dz_ref[rows, z0:z0 + AHD] = _dot(dsb, kb).astype(_CDT)
                dz_ref[rows, z0 + 3 * AHD:z0 + AW] = dgate.astype(_CDT)
                dkp[p, band, :] += _dot_tn(dsb, q)
                dvp[p, band, :] += _dot_tn(pr.astype(_CDT), dob)
            return carry

        lax.fori_loop(0, nb, step, 0, unroll=2)
        for p in range(HP):
            z0 = p * AW
            dz_ref[:, z0 + AHD:z0 + 2 * AHD] = dkp[p, PADK:, :].astype(_CDT)
            dz_ref[:, z0 + 2 * AHD:z0 + 3 * AHD] = dvp[p, PADK:, :].astype(_CDT)
            db_ref[p] = _fold_bias_grad(t_s[p])
        dg_ref[...] = dg_s[...]
        _ride_wait(cps, AH // HP)

    ride = _ride_specs(send, 0)
    return pl.pallas_call(
        body, name="att_bwd", grid=(AH // HP,),
        in_specs=[pl.BlockSpec((s, HP * AW), lambda h: (0, ZG // (HP * AW) + h)), pl.BlockSpec((s, HP * AHD), lambda h: (0, h)),
                  pl.BlockSpec((s, HP * AHD), lambda h: (0, h)), pl.BlockSpec((s, HP * AHD), lambda h: (0, h)),
                  pl.BlockSpec((HP, 1, 256), lambda h: (h, 0, 0)), pl.BlockSpec((1, HP * AHD), lambda h: (0, h))] + ride.in_specs,
        out_specs=[pl.BlockSpec((s, HP * AW), lambda h: (0, h)), pl.BlockSpec((1, HP * AHD), lambda h: (0, h)),
                   pl.BlockSpec((HP, 1, 256), lambda h: (h, 0, 0))] + ride.out_specs,
        out_shape=[jax.ShapeDtypeStruct((s, ZA), _CDT), jax.ShapeDtypeStruct((1, DATT), F32),
                   jax.ShapeDtypeStruct((AH, 1, 256), F32)] + ride.out_shape,
        scratch_shapes=[pltpu.VMEM((HP, s + PADK, AHD), _CDT), pltpu.VMEM((HP, s + PADK, AHD), _CDT),
                        pltpu.VMEM((HP, s + PADK, AHD), F32), pltpu.VMEM((HP, s + PADK, AHD), F32),
                        pltpu.VMEM((HP, QB, BANDW), F32), pltpu.VMEM((HP, QB, BANDW), F32), pltpu.VMEM((1, HP * AHD), F32)] + ride.scratch,
        compiler_params=_params(("arbitrary",)),
    )(z, oraw, lse, dca, rbx, gatt, *ride.operands)


def _gla_bwd(z, zga, wa, ba, ggla, dcg, send=None):
    s = z.shape[0]
    nc = s // CHUNK

    def body(*refs):
        zg_ref, zga_ref, wa_ref, ba_ref, g_ref, dc_ref = refs[:6]
        (dz_ref, dga_ref, dwa_ref, dba_ref, dg_ref,
         la_s, om_s, sall, dpre_s, c_s, dga_s, dg_s) = _ride_refs(refs[6:], send, 5)[1]
        cps = _ride_start(refs[6:], send, 5, 1)
        h = pl.program_id(0)
        pre = _dot(zga_ref[...], wa_ref[...]) + ba_ref[...]
        la_s[...] = _log_sigmoid(pre) * (1.0 / TAU)
        om_s[...] = (1.0 - _sigmoid(pre)) * (1.0 / TAU)
        c_s[...] = jnp.zeros_like(c_s)
        dg_s[...] = jnp.zeros_like(dg_s)
        tri = _tri(False)
        tri_strict = _tri(True)

        def decay(rows, p):
            la = la_s[rows, p * GDK:(p + 1) * GDK]
            lend = jnp.sum(la, axis=0, keepdims=True)
            return jnp.exp(lend - _exact_dot(tri, la)), jnp.exp(lend)

        def fwd(n, sts):
            rows = pl.ds(pl.multiple_of(n * CHUNK, CHUNK), CHUNK)
            out = []
            for p in range(HP):
                z0 = p * GW
                dec, a = decay(rows, p)
                kdec = (zg_ref[rows, z0 + GDK:z0 + 2 * GDK].astype(F32) * dec).astype(_CDT)
                stn = a * sts[p] + _dot_tn(zg_ref[rows, z0 + 2 * GDK:z0 + 2 * GDK + GDV], kdec)
                sall[p, n] = stn
                out.append(stn)
            return tuple(out)

        lax.fori_loop(0, nc, fwd, tuple(jnp.zeros((GDV, GDK), F32) for _ in range(HP)), unroll=4)

        def bwd(i, carry):
            n = nc - 1 - i
            rows = pl.ds(pl.multiple_of(n * CHUNK, CHUNK), CHUNK)
            for p in range(HP):
                z0 = p * GW
                kc = slice(p * GDK, (p + 1) * GDK)
                vc = slice(p * GDV, (p + 1) * GDV)
                dec, a = decay(rows, p)
                kdec = zg_ref[rows, z0 + GDK:z0 + 2 * GDK].astype(F32) * dec
                kdb = kdec.astype(_CDT)
                v = zg_ref[rows, z0 + 2 * GDK:z0 + 2 * GDK + GDV]
                qs = (zg_ref[rows, z0:z0 + GDK].astype(F32) * (GDK ** -0.5)).astype(_CDT)
                stb = sall[p, n].astype(_CDT)
                st_prev = sall[p, jnp.maximum(n - 1, 0)] * jnp.where(n > 0, 1.0, 0.0)
                o = _dot_nt(qs, stb)
                do, dgate, dg = _norm_gate_bwd(o, g_ref[:, vc], zg_ref[rows, z0 + 2 * GDK + GDV:z0 + GW].astype(F32),
                                               dc_ref[rows, vc].astype(F32))
                dg_s[:, vc] += dg
                dob = do.astype(_CDT)
                gt = _dot_tn(dob, qs) + c_s[p]
                gtb = gt.astype(_CDT)
                da = jnp.sum(gt * st_prev, axis=0, keepdims=True)
                dkdec = _dot(v, gtb)
                dla = _exact_dot(tri_strict, dkdec * kdec) + da * a
                dpre_s[rows, kc] = dla * om_s[rows, kc]
                dz_ref[rows, z0:z0 + GDK] = (_dot(dob, stb) * (GDK ** -0.5)).astype(_CDT)
                dz_ref[rows, z0 + GDK:z0 + 2 * GDK] = (dkdec * dec).astype(_CDT)
                dz_ref[rows, z0 + 2 * GDK:z0 + 2 * GDK + GDV] = _dot_nt(kdb, gtb).astype(_CDT)
                dz_ref[rows, z0 + 2 * GDK + GDV:z0 + GW] = dgate.astype(_CDT)
                c_s[p] = a * gt
            return carry

        lax.fori_loop(0, nc, bwd, 0)
        dpre = dpre_s[...]
        dpb = dpre.astype(_CDT)
        dg_ref[...] = dg_s[...]
        dba_ref[...] = jnp.sum(dpre, axis=0, keepdims=True)
        dwa_ref[...] = _dot_tn(zga_ref[...], dpb)
        part = _dot_nt(dpb, wa_ref[...])

        @pl.when(h == 0)
        def _():
            dga_s[...] = part

        @pl.when(h > 0)
        def _():
            dga_s[...] += part

        @pl.when(h == GH // HP - 1)
        def _():
            dga_ref[...] = dga_s[...].astype(_CDT)

        _ride_wait(cps, GH // HP)

    ride = _ride_specs(send, 1)
    return pl.pallas_call(
        body, name="gla_bwd", grid=(GH // HP,),
        in_specs=[pl.BlockSpec((s, HP * GW), lambda h: (0, h)), pl.BlockSpec((s, GAP), lambda h: (0, 0)),
                  pl.BlockSpec((GAP, HP * GDK), lambda h: (0, h)), pl.BlockSpec((1, HP * GDK), lambda h: (0, h)),
                  pl.BlockSpec((1, HP * GDV), lambda h: (0, h)), pl.BlockSpec((s, HP * GDV), lambda h: (0, h))] + ride.in_specs,
        out_specs=[pl.BlockSpec((s, HP * GW), lambda h: (0, h)), pl.BlockSpec((s, GAP), lambda h: (0, 0)),
                   pl.BlockSpec((GAP, HP * GDK), lambda h: (0, h)), pl.BlockSpec((1, HP * GDK), lambda h: (0, h)),
                   pl.BlockSpec((1, HP * GDV), lambda h: (0, h))] + ride.out_specs,
        out_shape=[jax.ShapeDtypeStruct((s, ZG), _CDT), jax.ShapeDtypeStruct((s, GAP), _CDT),
                   jax.ShapeDtypeStruct((GAP, GH * GDK), F32), jax.ShapeDtypeStruct((1, GH * GDK), F32),
                   jax.ShapeDtypeStruct((1, DGLA), F32)] + ride.out_shape,
        scratch_shapes=[pltpu.VMEM((s, HP * GDK), F32), pltpu.VMEM((s, HP * GDK), F32), pltpu.VMEM((HP, nc, GDV, GDK), F32),
                        pltpu.VMEM((s, HP * GDK), F32), pltpu.VMEM((HP, GDV, GDK), F32), pltpu.VMEM((s, GAP), F32),
                        pltpu.VMEM((1, HP * GDV), F32)] + ride.scratch,
        compiler_params=_params(("arbitrary",)),
    )(z, zga, wa, ba, ggla, dcg, *ride.operands)


def _dh(dzg, dza, dga, wm, wga, x, dout, gpre, send=None, tm=512, tk=1024):
    s = x.shape[0]
    nkg, nk = ZG // tk, ZM // tk

    def body(*refs):
        dzg_ref, dza_ref, dga_ref, wm_ref, wga_ref, x_ref, d_ref, g_ref = refs[:8]
        dx_ref, dg_ref, acc = _ride_refs(refs[8:], send, 2)[1]
        i, k = pl.program_id(0), pl.program_id(1)
        cps = _ride_start(refs[8:], send, 2, 1, (i == 0) & (k == 0))

        @pl.when((i == 0) & (k == 0))
        def _():
            dg_ref[...] = jnp.zeros_like(dg_ref)

        @pl.when(k == 0)
        def _():
            acc[...] = _dot(dga_ref[...], wga_ref[...])

        @pl.when(k < nkg)
        def _():
            acc[...] += _dot(dzg_ref[...], wm_ref[...])

        @pl.when(k >= nkg)
        def _():
            acc[...] += _dot(dza_ref[...], wm_ref[...])

        @pl.when(k == nk - 1)
        def _():
            xv = x_ref[...]
            r = _rms_rows(xv)
            xh = xv * r
            dh = acc[...]
            dg_ref[...] += jnp.sum(dh * xh, axis=0, keepdims=True)
            dn = dh * g_ref[...]
            dx_ref[...] = d_ref[...] + r * (dn - xh * jnp.mean(dn * xh, axis=-1, keepdims=True))

        _ride_wait(cps, 0, (i == s // tm - 1) & (k == nk - 1))

    ride = _ride_specs(send, 1)
    return pl.pallas_call(
        body, name="dh", grid=(s // tm, nk),
        in_specs=[pl.BlockSpec((tm, tk), lambda i, k: (i, jnp.minimum(k, nkg - 1))),
                  pl.BlockSpec((tm, tk), lambda i, k: (i, jnp.maximum(k - nkg, 0))),
                  pl.BlockSpec((tm, GAP), lambda i, k: (i, 0)), pl.BlockSpec((tk, D), lambda i, k: (k, 0)),
                  pl.BlockSpec((GAP, D), lambda i, k: (0, 0)), pl.BlockSpec((tm, D), lambda i, k: (i, 0)),
                  pl.BlockSpec((tm, D), lambda i, k: (i, 0)), pl.BlockSpec((1, D), lambda i, k: (0, 0))] + ride.in_specs,
        out_specs=[pl.BlockSpec((tm, D), lambda i, k: (i, 0)), pl.BlockSpec((1, D), lambda i, k: (0, 0))] + ride.out_specs,
        out_shape=[jax.ShapeDtypeStruct((s, D), F32), jax.ShapeDtypeStruct((1, D), F32)] + ride.out_shape,
        scratch_shapes=[pltpu.VMEM((tm, D), F32)] + ride.scratch,
        compiler_params=_params(("arbitrary", "arbitrary")),
    )(dzg, dza, dga, wm, wga, x, dout, gpre, *ride.operands)


def _adam(w, g, m, v, tr, name):
    rws, cols = w.shape

    def body(w_ref, g_ref, m_ref, v_ref, d_ref, mo_ref, vo_ref):
        gv = g_ref[...]
        mn = ADAM_B1 * m_ref[...] + (1.0 - ADAM_B1) * gv
        vn = ADAM_B2 * v_ref[...] + (1.0 - ADAM_B2) * (gv * gv)
        mh = mn / (1.0 - ADAM_B1 ** ADAM_STEP)
        vh = vn / (1.0 - ADAM_B2 ** ADAM_STEP)
        d_ref[...] = -ADAM_LR * (mh / (jnp.sqrt(vh) + ADAM_EPS) + ADAM_WD * w_ref[...])
        mo_ref[...] = mn
        vo_ref[...] = vn

    spec = pl.BlockSpec((tr, cols), lambda i: (i, 0))
    return pl.pallas_call(
        body, name=name, grid=(rws // tr,), in_specs=[spec] * 4, out_specs=[spec] * 3,
        out_shape=[jax.ShapeDtypeStruct((rws, cols), F32)] * 3,
        compiler_params=_params(("parallel",)),
    )(w, g, m, v)


def _adam_rows(c_idx, w, g_own, g_other, m, v, tj, name):
    rows, nl = w.shape[:2]
    per_half = g_own.shape[1] // tj

    def body(c_ref, w_ref, go_ref, gx_ref, m_ref, v_ref, g_ref, d_ref, mo_ref, vo_ref):
        mine = pl.program_id(0) // per_half == c_ref[0]
        for l in range(nl):
            gv = jnp.where(mine, go_ref[l], gx_ref[l])
            mn = ADAM_B1 * m_ref[:, l, :] + (1.0 - ADAM_B1) * gv
            vn = ADAM_B2 * v_ref[:, l, :] + (1.0 - ADAM_B2) * (gv * gv)
            mh = mn / (1.0 - ADAM_B1 ** ADAM_STEP)
            vh = vn / (1.0 - ADAM_B2 ** ADAM_STEP)
            g_ref[:, l, :] = gv
            d_ref[:, l, :] = -ADAM_LR * (mh / (jnp.sqrt(vh) + ADAM_EPS) + ADAM_WD * w_ref[:, l, :])
            mo_ref[:, l, :] = mn
            vo_ref[:, l, :] = vn

    full = pl.BlockSpec((tj,) + w.shape[1:], lambda i, c_ref: (i, 0, 0))
    own = pl.BlockSpec((nl, tj, D), lambda i, c_ref: (0, jnp.where(i // per_half == c_ref[0], i % per_half, 0), 0))
    other = pl.BlockSpec((nl, tj, D), lambda i, c_ref: (0, jnp.where(i // per_half == c_ref[0], 0, i % per_half), 0))
    return pl.pallas_call(
        body, name=name,
        grid_spec=pltpu.PrefetchScalarGridSpec(num_scalar_prefetch=1, grid=(pl.cdiv(rows, tj),),
                                               in_specs=[full, own, other, full, full], out_specs=[full] * 4),
        out_shape=[jax.ShapeDtypeStruct(w.shape, F32)] * 4,
        compiler_params=_params(("parallel",)),
    )(c_idx, w, g_own, g_other, m, v)


def _gather_first(src_in, src_out, wa, buf_in, buf_out, wa_all):
    def variant(x, y, c, src_in_ref, src_out_ref, wa_ref, _b0, _b1, _b2, bin_ref, bout_ref, wa_ref_all, send_sems, recv_sems, wa_send, wa_recv):
        for phase in (0, 3, 1, 2):
            cps = _gather_copies(phase, x, y, c, src_in_ref, src_out_ref, bin_ref, bout_ref, send_sems.at[phase], recv_sems.at[phase])
            if phase == 0:
                for k, (px, py) in enumerate([(1 - x, y), (x, 1 - y), (1 - x, 1 - y)]):
                    cps.append(pltpu.make_async_remote_copy(src_ref=wa_ref, dst_ref=wa_ref_all.at[2 * x + y], send_sem=wa_send.at[k],
                                                            recv_sem=wa_recv.at[k], device_id=(px, py, c), device_id_type=MESH))
            for cpy in cps:
                cpy.start()
            for cpy in cps:
                cpy.wait()

    def body(*refs):
        _variants(lambda x, y, c: variant(x, y, c, *refs))

    return pl.pallas_call(
        body, name="gather_first", in_specs=[ANY] * 6, out_specs=[ANY] * 3, input_output_aliases={3: 0, 4: 1, 5: 2},
        out_shape=[jax.ShapeDtypeStruct(t.shape, t.dtype) for t in (buf_in, buf_out, wa_all)],
        scratch_shapes=[pltpu.SemaphoreType.DMA((4, 2, 3)), pltpu.SemaphoreType.DMA((4, 2, 3)),
                        pltpu.SemaphoreType.DMA((3,)), pltpu.SemaphoreType.DMA((3,))],
    )(src_in, src_out, wa, buf_in, buf_out, wa_all)


def _swap_halves(gin2, gout2):
    def body(gin_ref, gout_ref, rin, rout, send_sems, recv_sems):
        x, y, c, _ = _place()
        sib = (x, y, 1 - c)
        cps = [pltpu.make_async_remote_copy(src_ref=src.at[1 - c], dst_ref=dst, send_sem=send_sems.at[a],
                                            recv_sem=recv_sems.at[a], device_id=sib, device_id_type=MESH)
               for a, (src, dst) in enumerate([(gin_ref, rin), (gout_ref, rout)])]
        for cpy in cps:
            cpy.start()
        for cpy in cps:
            cpy.wait()

    return pl.pallas_call(
        body, name="swap_halves", in_specs=[ANY, ANY], out_specs=[ANY, ANY],
        out_shape=[jax.ShapeDtypeStruct(gin2.shape[1:], gin2.dtype), jax.ShapeDtypeStruct(gout2.shape[1:], gout2.dtype)],
        scratch_shapes=[pltpu.SemaphoreType.DMA((2,)), pltpu.SemaphoreType.DMA((2,))],
    )(gin2, gout2)


def _add_halves(c_idx, g2, r, tr, name):
    rows, cols = r.shape

    def body(c_ref, g_ref, r_ref, o_ref):
        o_ref[...] = (g_ref[0].astype(F32) + r_ref[...].astype(F32)).astype(_XDT)

    return pl.pallas_call(
        body, name=name,
        grid_spec=pltpu.PrefetchScalarGridSpec(
            num_scalar_prefetch=1, grid=(rows // tr,),
            in_specs=[pl.BlockSpec((1, tr, cols), lambda i, c_ref: (c_ref[0], i, 0)),
                      pl.BlockSpec((tr, cols), lambda i, c_ref: (i, 0))],
            out_specs=pl.BlockSpec((tr, cols), lambda i, c_ref: (i, 0))),
        out_shape=jax.ShapeDtypeStruct((rows, cols), _XDT),
        compiler_params=_params(("parallel",)),
    )(c_idx, g2, r)


def _send_to_owners(pin, pout, parts):
    def body(pin_ref, pout_ref, *refs):
        send_sems, recv_sems = refs[-2:]
        cps = []
        for n, part in enumerate(parts):
            cps += _owner_copies(part, pin_ref, pout_ref, refs[2 * n], refs[2 * n + 1], send_sems.at[n], recv_sems.at[n])
        for cpy in cps:
            cpy.start()
        for cpy in cps:
            cpy.wait()

    shapes = [jax.ShapeDtypeStruct((3, _part_rows(t.shape[2], split, part)[1], D), t.dtype)
              for part in parts for t, split in zip((pin, pout), (SPLIT_IN, SPLIT_OUT))]
    return pl.pallas_call(
        body, name="send_to_owners", in_specs=[ANY, ANY], out_specs=[ANY] * len(shapes), out_shape=shapes,
        scratch_shapes=[pltpu.SemaphoreType.DMA((len(parts), 2, 3)), pltpu.SemaphoreType.DMA((len(parts), 2, 3))],
    )(pin, pout)


def _add_chips(chip_idx, p, ra, rb, tr, name):
    rows = p.shape[2]
    na = ra.shape[1] // tr

    def body(c_ref, p_ref, ra_ref, rb_ref, o_ref):
        r = jnp.where(pl.program_id(0) < na, ra_ref[...], rb_ref[...]).astype(F32)
        o_ref[0] = ((p_ref[0, 0].astype(F32) + r[0]) + r[1]) + r[2]

    return pl.pallas_call(
        body, name=name,
        grid_spec=pltpu.PrefetchScalarGridSpec(
            num_scalar_prefetch=1, grid=(rows // tr,),
            in_specs=[pl.BlockSpec((1, 1, tr, D), lambda i, c_ref: (0, c_ref[0], i, 0)),
                      pl.BlockSpec((3, tr, D), lambda i, c_ref: (0, jnp.minimum(i, na - 1), 0)),
                      pl.BlockSpec((3, tr, D), lambda i, c_ref: (0, jnp.maximum(i - na, 0), 0))],
            out_specs=pl.BlockSpec((1, tr, D), lambda i, c_ref: (0, i, 0))),
        out_shape=jax.ShapeDtypeStruct((1, rows, D), F32),
        compiler_params=_params(("parallel",)),
    )(chip_idx, p, ra, rb)


def _exchange_halves(arrs):
    n = len(arrs)

    def body(*refs):
        x, y, c, _ = _place()
        cps = [pltpu.make_async_remote_copy(src_ref=refs[a], dst_ref=refs[n + a], send_sem=refs[2 * n].at[a], recv_sem=refs[2 * n + 1].at[a],
                                            device_id=(x, y, 1 - c), device_id_type=MESH) for a in range(n)]
        for cpy in cps:
            cpy.start()
        for cpy in cps:
            cpy.wait()

    return pl.pallas_call(
        body, name="exchange_halves", in_specs=[ANY] * n, out_specs=[ANY] * n,
        out_shape=[jax.ShapeDtypeStruct(t.shape, t.dtype) for t in arrs],
        scratch_shapes=[pltpu.SemaphoreType.DMA((n,)), pltpu.SemaphoreType.DMA((n,))],
    )(*arrs)


def _adam_halves(c_idx, w, g_own, g_other, m, v, tr, name):
    nl, _, rows, cols = w.shape

    def body(c_ref, w_ref, go_ref, gx_ref, m_ref, v_ref, g_ref, d_ref, mo_ref, vo_ref):
        gv = jnp.where(pl.program_id(1) == c_ref[0], go_ref[0], gx_ref[0])
        mn = ADAM_B1 * m_ref[0, 0] + (1.0 - ADAM_B1) * gv
        vn = ADAM_B2 * v_ref[0, 0] + (1.0 - ADAM_B2) * (gv * gv)
        mh = mn / (1.0 - ADAM_B1 ** ADAM_STEP)
        vh = vn / (1.0 - ADAM_B2 ** ADAM_STEP)
        g_ref[0, 0] = gv
        d_ref[0, 0] = -ADAM_LR * (mh / (jnp.sqrt(vh) + ADAM_EPS) + ADAM_WD * w_ref[0, 0])
        mo_ref[0, 0] = mn
        vo_ref[0, 0] = vn

    full = pl.BlockSpec((1, 1, tr, cols), lambda l, hh, i, c_ref: (l, hh, i, 0))
    own = pl.BlockSpec((1, tr, cols), lambda l, hh, i, c_ref: (l, jnp.where(hh == c_ref[0], i, 0), 0))
    other = pl.BlockSpec((1, tr, cols), lambda l, hh, i, c_ref: (l, jnp.where(hh == c_ref[0], 0, i), 0))
    return pl.pallas_call(
        body, name=name,
        grid_spec=pltpu.PrefetchScalarGridSpec(
            num_scalar_prefetch=1, grid=(nl, 2, rows // tr),
            in_specs=[full, own, other, full, full], out_specs=[full] * 4),
        out_shape=[jax.ShapeDtypeStruct(w.shape, F32)] * 4,
        compiler_params=_params(("parallel", "parallel", "parallel")),
    )(c_idx, w, g_own, g_other, m, v)


def _allreduce_small(sg):
    rows = sg.shape[0]
    vm = pl.BlockSpec(memory_space=pltpu.VMEM)

    def body(sg_ref, tot_ref, all_ref, send_sems, recv_sems):
        x, y, c, _ = _place()
        me = 4 * x + 2 * y + c
        all_ref[me] = sg_ref[...]
        cps = []
        for mask in range(1, 8):
            to = (1 - x if mask & 4 else x, 1 - y if mask & 2 else y, 1 - c if mask & 1 else c)
            cps.append(pltpu.make_async_remote_copy(src_ref=sg_ref, dst_ref=all_ref.at[me], send_sem=send_sems.at[mask - 1],
                                                    recv_sem=recv_sems.at[mask - 1], device_id=to, device_id_type=MESH))
        for cpy in cps:
            cpy.start()
        for cpy in cps:
            cpy.wait()
        acc = all_ref[0]
        for d in range(1, 8):
            acc = acc + all_ref[d]
        tot_ref[...] = acc

    return pl.pallas_call(
        body, name="allreduce_small", in_specs=[vm], out_specs=[vm, vm],
        out_shape=[jax.ShapeDtypeStruct((rows, 128), F32), jax.ShapeDtypeStruct((8, rows, 128), F32)],
        scratch_shapes=[pltpu.SemaphoreType.DMA((7,)), pltpu.SemaphoreType.DMA((7,))],
        compiler_params=_params(),
    )(sg)[0]


_CUTS = [0, 512, 1024, 2048, 3072, 3088, 4112, 5136, 6160, 7184]


def _rows_to_internal(w):
    tail = w.shape[1:]
    gq, gk, gv, gg, ga, aq, ak, av, ag = [w[_CUTS[i]:_CUTS[i + 1]] for i in range(9)]
    g = jnp.concatenate([gq.reshape((GH, GDK) + tail), gk.reshape((GH, GDK) + tail),
                         gv.reshape((GH, GDV) + tail), gg.reshape((GH, GDV) + tail)], axis=1).reshape((ZG,) + tail)
    a = jnp.concatenate([t.reshape((AH, AHD) + tail) for t in (aq, ak, av, ag)], axis=1).reshape((ZA,) + tail)
    pad = [(0, GAP - RANK)] + [(0, 0)] * len(tail)
    return jnp.concatenate([g, a], axis=0), jnp.pad(ga, pad)


def _rows_from_internal(g, a, ga):
    tail = g.shape[1:]
    g = g.reshape((GH, GW) + tail)
    a = a.reshape((AH, AW) + tail)
    parts = [g[:, 0:GDK], g[:, GDK:2 * GDK], g[:, 2 * GDK:2 * GDK + GDV], g[:, 2 * GDK + GDV:GW]]
    parts = [t.reshape((-1,) + tail) for t in parts] + [ga[0:RANK]]
    parts += [a[:, i * AHD:(i + 1) * AHD].reshape((-1,) + tail) for i in range(4)]
    return jnp.concatenate(parts, axis=0)


def _pack_rows(parts):
    rows = []
    for t in parts:
        flat = t.reshape(-1)
        rows.append(jnp.pad(flat, (0, (-flat.shape[0]) % 128)).reshape(-1, 128))
    buf = jnp.concatenate(rows, axis=0)
    return jnp.pad(buf, ((0, (-buf.shape[0]) % 8), (0, 0)))


def _unpack_rows(buf, shapes):
    out, r = [], 0
    for shp in shapes:
        n = 1
        for d in shp:
            n *= d
        nr = -(-n // 128)
        out.append(buf[r:r + nr].reshape(-1)[:n].reshape(shp))
        r += nr
    return out


def _layer_fwd(x, wm, wga, wout, gpre, gpost, wa, ba, ggla, gatt, rbx, ride=None):
    z, zga, h, *bufs = _inproj(x, gpre, wm, wga, ride)
    ride = None if ride is None else (ride[0], ride[1], *bufs)
    cg, *bufs = _gla_fwd(z, zga, wa, ba, ggla, ride)
    ride = None if ride is None else (ride[0], ride[1], *bufs)
    ca, oraw, lse, *bufs = _att_fwd(z, rbx, gatt, ride)
    ride = None if ride is None else (ride[0], ride[1], *bufs)
    y, xo, *bufs = _outproj(cg, ca, wout, x, gpost, ride)
    return xo, (x, z, zga, h, cg, ca, oraw, lse, y), bufs


def _layer_bwd(dout, saved, wm, wga, wout, gpre, gpost, wa, ba, ggla, gatt, rbx, swap=None, finish=None, reduce_now=None):
    x, z, zga, h, cg, ca, oraw, lse, y = saved
    dy, dcg, dca, dgpost, *swapped = _post_bwd(dout, y, gpost, wout, swap)
    send = None if swap is None else finish(swap, swapped)
    dwout = jnp.concatenate([_matmul_tn(cg, dy, _XDT, 512, 1024, "dwout_gla"),
                             _matmul_tn(ca, dy, _XDT, 512, 1024, "dwout_att")], axis=0)
    dza, dgatt, dbx, *got_a = _att_bwd(z, oraw, lse, dca, rbx, gatt, send)
    dzg, dga, dwa, dba, dggla, *got_b = _gla_bwd(z, zga, wa, ba, ggla, dcg, send)
    dwin = (_matmul_tn(dzg, h, _XDT, 512, 1024, "dwin_gla"), _matmul_tn(dza, h, _XDT, 512, 1024, "dwin_att"),
            _matmul_tn(dga, h, _XDT, GAP, 1024, "dwin_gate"))
    own = None if reduce_now is None else reduce_now(dwin, dwout)
    dx, dgpre, *got_own = _dh(dzg, dza, dga, wm, wga, x, dout, gpre, own)
    drb = jnp.concatenate([jnp.zeros((AH, 1), F32), dbx[:, 0, ::-1]], axis=1)
    return dx, dwin, dwout, (dgpre[0], dgpost[0], dwa[0:RANK], dba[0], dggla[0], dgatt[0], drb), send, got_a + got_b, own, got_own


def _rel_rows(rb):
    return rb[:, :0:-1][:, None, :]


def kernel(x, w_in, w_out, g_pre, g_post, w_alpha, b_alpha, g_gla, g_att, rel_bias, loss_target, m_w_in, m_w_out, m_g_pre, m_g_post, m_w_alpha, m_b_alpha, m_g_gla, m_g_att, m_rel_bias, v_w_in, v_w_out, v_g_pre, v_g_post, v_w_alpha, v_b_alpha, v_g_gla, v_g_att, v_rel_bias):
    nl = w_in.shape[0]
    ax, ay, ac = lax.axis_index("x"), lax.axis_index("y"), lax.axis_index("c")
    chip = 2 * ax + ay
    c_idx = jnp.reshape(ac, (1,)).astype(jnp.int32)
    chip_idx = jnp.reshape(chip, (1,)).astype(jnp.int32)

    phase = [SHARD * i % 16 for i in range(NCHIP)]
    wt_rows = jnp.transpose(w_in, (0, 2, 1)).astype(_CDT)
    at_phase = [functools.partial(jnp.pad, wt_rows, ((0, 0), (p, WSLOT - SHARD - p), (0, 0))) for p in phase]
    wt_src = lax.switch(chip, at_phase).reshape(nl, 2, WSLOT // 2, D)
    wout_src = w_out.astype(_CDT).reshape(nl, 2, D // NCHIP // 2, D)

    def with_own(own):
        start = [chip] + [0] * own.ndim
        return lax.dynamic_update_slice(lax.empty((NCHIP,) + own.shape, own.dtype), own[None], start)

    def gather_operands(l):
        return wt_src[l], wout_src[l], with_own(wt_src[l]), with_own(wout_src[l])

    def layer_weights(bufs):
        wt4 = bufs[0].reshape(NCHIP, WSLOT, D)
        wref = jnp.concatenate([wt4[i, phase[i]:phase[i] + SHARD] for i in range(NCHIP)])
        return _rows_to_internal(wref) + (bufs[1].reshape(D, D),)

    first = gather_operands(0)
    bin0, bout0, wa_all = _gather_first(first[0], first[1], w_alpha, first[2], first[3], with_own(w_alpha))
    wa_full = jnp.transpose(wa_all, (1, 2, 0, 3)).reshape(nl, RANK, GH * GDK)
    wa_pad = jnp.pad(wa_full, ((0, 0), (0, GAP - RANK), (0, 0))).astype(_CDT)
    rbx = [_rel_rows(rel_bias[l]) for l in range(nl)]

    def weights(l):
        return big[l] + (g_pre[l][None], g_post[l][None], wa_pad[l], b_alpha[l][None], g_gla[l][None], g_att[l][None], rbx[l])

    h = x[0]
    saved, big = [], [None] * nl
    big[0] = layer_weights((bin0, bout0))
    for l in range(nl):
        h, sv, bufs = _layer_fwd(h, *weights(l), ride=gather_operands(l + 1) if l + 1 < nl else None)
        saved.append(sv)
        if l + 1 < nl:
            big[l + 1] = layer_weights(bufs)
    dout, loss_part = _loss_grad(h, loss_target[0])

    small, hin, hout = [None] * nl, [None] * nl, [None] * nl
    hw = D // NCHIP // 2

    def reduce_owner(sent, got):
        rin_a, rout_a, rin_b, rout_b = got
        return (_add_chips(chip_idx, sent[0], rin_a, rin_b, 48, "add_chips_in"),
                _add_chips(chip_idx, sent[1], rout_a, rout_b, 128, "add_chips_out"))

    def slab_halves(dwin, dwout):
        gt = jnp.pad(_rows_from_internal(*dwin), ((0, NCHIP * SHARD - SHARD + SLAB - DIN), (0, 0)))
        slabs = jnp.stack([gt[SHARD * i:SHARD * i + SLAB] for i in range(NCHIP)])
        return (jnp.transpose(slabs.reshape(NCHIP, 2, HSLAB, D), (1, 0, 2, 3)).reshape(2, NCHIP * HSLAB, D),
                jnp.transpose(dwout.reshape(NCHIP, 2, hw, D), (1, 0, 2, 3)).reshape(2, NCHIP * hw, D))

    def sum_halves(halves, swapped):
        return (_add_halves(c_idx, halves[0], swapped[0], 192, "add_halves_in").reshape(1, NCHIP, HSLAB, D),
                _add_halves(c_idx, halves[1], swapped[1], 256, "add_halves_out").reshape(1, NCHIP, hw, D))

    def partial_sums(dwin, dwout):
        halves = slab_halves(dwin, dwout)
        return sum_halves(halves, _swap_halves(*halves))

    halves = None
    for l in reversed(range(nl)):
        dout, dwin, dwout, small[l], sent, got, own, got_own = _layer_bwd(
            dout, saved[l], *weights(l), swap=halves, finish=sum_halves, reduce_now=partial_sums if l == 0 else None)
        if halves is not None:
            hin[l + 1], hout[l + 1] = reduce_owner(sent, got)
        halves = slab_halves(dwin, dwout) if l > 0 else None
    sent = own
    grad_x = dout[None]
    hin[0], hout[0] = reduce_owner(sent, list(_send_to_owners(*sent, parts=(0,))) + list(got_own))
    xchg = _exchange_halves(hin + hout)
    hin, xin = jnp.concatenate(hin), jnp.concatenate(xchg[:nl])
    hout, xout = jnp.concatenate(hout), jnp.concatenate(xchg[nl:])

    rows_first = lambda t: jnp.transpose(t, (2, 0, 1))
    w_in_out = _adam_rows(c_idx, rows_first(w_in), hin, xin, rows_first(m_w_in), rows_first(v_w_in), 48, "adam_w_in")
    g_w_in, d_w_in, nm_w_in, nv_w_in = [jnp.transpose(t, (1, 2, 0)) for t in w_in_out]

    def adam_big(w, g_own, g_other, m, v, name):
        shp = w.shape
        halves = lambda t: t.reshape(shp[0], 2, shp[1] // 2, shp[2])
        return [t.reshape(shp) for t in _adam_halves(c_idx, halves(w), g_own, g_other, halves(m), halves(v), 256, name)]

    g_w_out, d_w_out, nm_w_out, nv_w_out = adam_big(w_out, hout, xout, m_w_out, v_w_out, "adam_w_out")

    stacked = [jnp.stack([small[l][i] for l in range(nl)]) for i in range(7)] + [loss_part]
    g_small = _unpack_rows(_allreduce_small(_pack_rows(stacked)), [t.shape for t in stacked])
    g_gpre, g_gpost, g_wa_full, g_ba, g_ggla, g_gatt, g_rb, loss_sum = g_small
    loss = loss_sum[0, 0]
    g_wa = lax.dynamic_slice_in_dim(g_wa_full, chip * GDK, GDK, axis=2)
    names = [(g_pre, m_g_pre, v_g_pre, g_gpre), (g_post, m_g_post, v_g_post, g_gpost), (w_alpha, m_w_alpha, v_w_alpha, g_wa),
             (b_alpha, m_b_alpha, v_b_alpha, g_ba), (g_gla, m_g_gla, v_g_gla, g_ggla), (g_att, m_g_att, v_g_att, g_gatt),
             (rel_bias, m_rel_bias, v_rel_bias, g_rb)]
    shapes = [t[0].shape for t in names]
    packed = [_pack_rows([t[i] for t in names]) for i in range(4)]
    d_s, nm_s, nv_s = [_unpack_rows(t, shapes) for t in _adam(packed[0], packed[3], packed[1], packed[2], packed[0].shape[0], "adam_small")]

    grads = [g_w_in, g_w_out, g_gpre, g_gpost, g_wa, g_ba, g_ggla, g_gatt, g_rb]
    deltas = [d_w_in, d_w_out] + d_s
    new_m = [nm_w_in, nm_w_out] + nm_s
    new_v = [nv_w_in, nv_w_out] + nv_s
    return (loss, grad_x, *grads, *deltas, *new_m, *new_v)
```

```python
import functools

import jax
import jax.numpy as jnp
from jax import lax
from jax.experimental import pallas as pl
from jax.experimental.pallas import tpu as pltpu

D = 2048
DEPTH = 4
CHUNK = 64
GH, GDK, GDV = 4, 128, 256
DGLA = GH * GDV
RANK = 16
TAU = 16.0
AH, AHD = 8, 128
DATT = AH * AHD
LEFT = 8
NREL = 257
EPS = 1e-6
DIN = 7184
ADAM_LR, ADAM_B1, ADAM_B2, ADAM_EPS, ADAM_WD, ADAM_STEP = 0.001, 0.9, 0.999, 1e-08, 0.01, 10

GW = 2 * GDK + 2 * GDV
AW = 4 * AHD
ZG = GH * GW
ZA = AH * AW
ZM = ZG + ZA
GAP = 128
QB = 2 * CHUNK
HP = 2
BANDW = (LEFT + 2) * CHUNK
PADK = LEFT * CHUNK
ROWB = 256
NCHIP = 4
SHARD = DIN // NCHIP
SLAB = 1824
HSLAB = SLAB // 2
WSLOT = 1824
GSUB_IN = [(0, 464), (464, 448)]
GSUB_OUT = [(0, 128), (128, 128)]
GP0_IN = [(0, 688), (688, 224)]
GP0_OUT = [(0, 192), (192, 64)]
SPLIT_IN, SPLIT_OUT = 336, 128
NEG = -1e30
F32 = jnp.float32
_CDT = jnp.bfloat16
_XDT = jnp.bfloat16
_VMEM = 56 * 1024 * 1024
MESH = pl.DeviceIdType.MESH
ANY = pl.BlockSpec(memory_space=pl.ANY)


def _dot(a, b):
    return jnp.dot(a, b, preferred_element_type=F32)


def _dot_nt(a, b):
    return lax.dot_general(a, b, (((1,), (1,)), ((), ())), preferred_element_type=F32)


def _dot_tn(a, b):
    return lax.dot_general(a, b, (((0,), (0,)), ((), ())), preferred_element_type=F32)


def _rms_rows(v):
    return lax.rsqrt(jnp.mean(v * v, axis=-1, keepdims=True) + EPS)


def _sigmoid(v):
    return 1.0 / (1.0 + jnp.exp(-v))


def _log_sigmoid(v):
    return jnp.minimum(v, 0.0) - jnp.log(1.0 + jnp.exp(-jnp.abs(v)))


def _exact_dot(tri, v):
    hi = v.astype(_CDT)
    r1 = v - hi.astype(F32)
    mid = r1.astype(_CDT)
    lo = (r1 - mid.astype(F32)).astype(_CDT)
    return _dot(tri, hi) + _dot(tri, mid) + _dot(tri, lo)


def _tri(strict):
    row = lax.broadcasted_iota(jnp.int32, (CHUNK, CHUNK), 0)
    col = lax.broadcasted_iota(jnp.int32, (CHUNK, CHUNK), 1)
    return jnp.where((col < row) if strict else (col <= row), 1.0, 0.0).astype(_CDT)


def _norm_gate_bwd(o, g, gate, dcat):
    r = _rms_rows(o)
    oh = o * r
    sg = _sigmoid(gate)
    dn = dcat * (gate * sg)
    dgate = dcat * (oh * g) * (sg * (1.0 + gate * (1.0 - sg)))
    dg = jnp.sum(dn * oh, axis=0, keepdims=True)
    dnn = dn * g
    do = r * (dnn - oh * jnp.mean(dnn * oh, axis=-1, keepdims=True))
    return do, dgate, dg


def _params(sem=None, vmem=_VMEM):
    return pltpu.CompilerParams(dimension_semantics=sem, vmem_limit_bytes=vmem)


def _inproj(x, g, wm, wga, ride=None, tm=512, tn=1024):
    s = x.shape[0]
    gr = _gather_ride(ride, 0)

    def body(*refs):
        (x_ref, g_ref, wm_ref, wga_ref, z_ref, zga_ref, h_ref, hs), rr = gr.split(refs, 4, 3)
        i, j = pl.program_id(0), pl.program_id(1)
        gr.start(rr, (i == 0) & (j == 0))

        @pl.when(pl.program_id(1) == 0)
        def _():
            xv = x_ref[...]
            hv = (xv * _rms_rows(xv) * g_ref[...]).astype(_CDT)
            hs[...] = hv
            h_ref[...] = hv
            zga_ref[...] = _dot_nt(hv, wga_ref[...]).astype(_CDT)

        z_ref[...] = _dot_nt(hs[...], wm_ref[...]).astype(_CDT)
        gr.wait(rr, (i == s // tm - 1) & (j == ZM // tn - 1))

    return pl.pallas_call(
        body, name="inproj", grid=(s // tm, ZM // tn),
        in_specs=[pl.BlockSpec((tm, D), lambda i, j: (i, 0)), pl.BlockSpec((1, D), lambda i, j: (0, 0)),
                  pl.BlockSpec((tn, D), lambda i, j: (j, 0)), pl.BlockSpec((GAP, D), lambda i, j: (0, 0))] + gr.in_specs,
        out_specs=[pl.BlockSpec((tm, tn), lambda i, j: (i, j)), pl.BlockSpec((tm, GAP), lambda i, j: (i, 0)),
                   pl.BlockSpec((tm, D), lambda i, j: (i, 0))] + gr.out_specs,
        out_shape=[jax.ShapeDtypeStruct((s, ZM), _CDT), jax.ShapeDtypeStruct((s, GAP), _CDT),
                   jax.ShapeDtypeStruct((s, D), _CDT)] + gr.out_shape,
        scratch_shapes=[pltpu.VMEM((tm, D), _CDT)] + gr.scratch, input_output_aliases=gr.alias(4, 3),
        compiler_params=_params(("arbitrary", "arbitrary")),
    )(x, g, wm, wga, *gr.operands)


def _gla_fwd(z, zga, wa, ba, ggla, ride=None):
    s = z.shape[0]
    nc = s // CHUNK
    gr = _gather_ride(ride, 3)

    def body(*refs):
        (zg_ref, zga_ref, wa_ref, ba_ref, g_ref, cat_ref, la_s, st), rr = gr.split(refs, 5, 1)
        gr.start(rr, pl.program_id(0) == 0)
        la_s[...] = _log_sigmoid(_dot(zga_ref[...], wa_ref[...]) + ba_ref[...]) * (1.0 / TAU)
        st[...] = jnp.zeros_like(st)
        tri = _tri(False)

        def step(n, carry):
            rows = pl.ds(pl.multiple_of(n * CHUNK, CHUNK), CHUNK)
            for p in range(HP):
                z0 = p * GW
                la = la_s[rows, p * GDK:(p + 1) * GDK]
                lc = _exact_dot(tri, la)
                lend = jnp.sum(la, axis=0, keepdims=True)
                kdec = (zg_ref[rows, z0 + GDK:z0 + 2 * GDK].astype(F32) * jnp.exp(lend - lc)).astype(_CDT)
                stn = jnp.exp(lend) * st[p] + _dot_tn(zg_ref[rows, z0 + 2 * GDK:z0 + 2 * GDK + GDV], kdec)
                st[p] = stn
                qs = (zg_ref[rows, z0:z0 + GDK].astype(F32) * (GDK ** -0.5)).astype(_CDT)
                o = _dot_nt(qs, stn.astype(_CDT))
                gate = zg_ref[rows, z0 + 2 * GDK + GDV:z0 + GW].astype(F32)
                gain = g_ref[:, p * GDV:(p + 1) * GDV]
                cat_ref[rows, p * GDV:(p + 1) * GDV] = (o * _rms_rows(o) * gain * (gate * _sigmoid(gate))).astype(_CDT)
            return carry

        lax.fori_loop(0, nc, step, 0, unroll=4)
        gr.wait(rr, pl.program_id(0) == GH // HP - 1)

    return pl.pallas_call(
        body, name="gla_fwd", grid=(GH // HP,),
        in_specs=[pl.BlockSpec((s, HP * GW), lambda h: (0, h)), pl.BlockSpec((s, GAP), lambda h: (0, 0)),
                  pl.BlockSpec((GAP, HP * GDK), lambda h: (0, h)), pl.BlockSpec((1, HP * GDK), lambda h: (0, h)),
                  pl.BlockSpec((1, HP * GDV), lambda h: (0, h))] + gr.in_specs,
        out_specs=[pl.BlockSpec((s, HP * GDV), lambda h: (0, h))] + gr.out_specs,
        out_shape=[jax.ShapeDtypeStruct((s, DGLA), _CDT)] + gr.out_shape,
        scratch_shapes=[pltpu.VMEM((s, HP * GDK), F32), pltpu.VMEM((HP, GDV, GDK), F32)] + gr.scratch,
        input_output_aliases=gr.alias(5, 1), compiler_params=_params(("arbitrary",)),
    )(z, zga, wa, ba, ggla, *gr.operands)


def _band_bias(b0):
    row = lax.broadcasted_iota(jnp.int32, (QB, 256), 0)
    col = lax.broadcasted_iota(jnp.int32, (QB, 256), 1)
    lane = lax.broadcasted_iota(jnp.int32, (1, 256), 1)
    c0 = jnp.sum(jnp.where(lane == 0, b0, 0.0), axis=1, keepdims=True)
    xv = jnp.broadcast_to(b0, (QB, 256))
    for bit in range(7):
        xv = jnp.where(((row >> bit) & 1) == 1, pltpu.roll(xv, 1 << bit, 1), xv)
    xv = jnp.where(col < row, c0, xv)
    return jnp.concatenate([jnp.broadcast_to(c0, (QB, BANDW - 256)), xv], axis=1)


def _band_static_mask():
    row = lax.broadcasted_iota(jnp.int32, (QB, BANDW), 0) >> 6
    col = lax.broadcasted_iota(jnp.int32, (QB, BANDW), 1) >> 6
    return (col >= row) & (col <= row + LEFT)


def _fold_bias_grad(t):
    row = lax.broadcasted_iota(jnp.int32, (QB, 256), 0)
    col = lax.broadcasted_iota(jnp.int32, (QB, 256), 1)
    xv = t[:, BANDW - 256:]
    low = col < row
    far = jnp.sum(t[:, 0:BANDW - 256], axis=1, keepdims=True) + jnp.sum(jnp.where(low, xv, 0.0), axis=1, keepdims=True)
    far = jnp.sum(far, axis=0, keepdims=True)
    xv = jnp.where(low, 0.0, xv)
    for bit in range(7):
        xv = jnp.where(((row >> bit) & 1) == 1, pltpu.roll(xv, 256 - (1 << bit), 1), xv)
    dp = jnp.sum(xv, axis=0, keepdims=True)
    lane = lax.broadcasted_iota(jnp.int32, (1, 256), 1)
    return dp + jnp.where(lane == 0, far, 0.0)


def _att_fwd(z, rbx, gatt, ride=None):
    s = z.shape[0]
    nb = s // QB
    gr = _gather_ride(ride, 1)

    def body(*refs):
        (za_ref, rb_ref, g_ref, cat_ref, o_ref, lse_ref, kp, vp, bias_s), rr = gr.split(refs, 3, 3)
        gr.start(rr, pl.program_id(0) == 0)
        for p in range(HP):
            z0 = p * AW
            kp[p, 0:PADK, :] = jnp.zeros((PADK, AHD), _CDT)
            vp[p, 0:PADK, :] = jnp.zeros((PADK, AHD), _CDT)
            for r0 in range(0, s, ROWB):
                kp[p, PADK + r0:PADK + r0 + ROWB, :] = za_ref[r0:r0 + ROWB, z0 + AHD:z0 + 2 * AHD]
                vp[p, PADK + r0:PADK + r0 + ROWB, :] = za_ref[r0:r0 + ROWB, z0 + 2 * AHD:z0 + 3 * AHD]
            bias_s[p] = jnp.where(_band_static_mask(), _band_bias(rb_ref[p]), NEG)

        def step(b, carry):
            r0 = pl.multiple_of(b * QB, QB)
            rows = pl.ds(r0, QB)
            band = pl.ds(r0, BANDW)
            live = lax.broadcasted_iota(jnp.int32, (QB, BANDW), 1) >= PADK - r0
            for p in range(HP):
                z0 = p * AW
                cols = slice(p * AHD, (p + 1) * AHD)
                sc = _dot_nt(za_ref[rows, z0:z0 + AHD], kp[p, band, :]) * (AHD ** -0.5) + bias_s[p]
                sc = jnp.where(live, sc, NEG)
                m = jnp.max(sc, axis=-1, keepdims=True)
                pr = jnp.exp(sc - m)
                l = jnp.sum(pr, axis=-1, keepdims=True)
                o = _dot((pr * (1.0 / l)).astype(_CDT), vp[p, band, :])
                o_ref[rows, cols] = o.astype(_CDT)
                lse_ref[rows, cols] = jnp.broadcast_to(m + jnp.log(l), (QB, AHD))
                gate = za_ref[rows, z0 + 3 * AHD:z0 + AW].astype(F32)
                cat_ref[rows, cols] = (o * _rms_rows(o) * g_ref[:, cols] * (gate * _sigmoid(gate))).astype(_CDT)
            return carry

        lax.fori_loop(0, nb, step, 0, unroll=4)
        gr.wait(rr, pl.program_id(0) == AH // HP - 1)

    return pl.pallas_call(
        body, name="att_fwd", grid=(AH // HP,),
        in_specs=[pl.BlockSpec((s, HP * AW), lambda h: (0, ZG // (HP * AW) + h)), pl.BlockSpec((HP, 1, 256), lambda h: (h, 0, 0)),
                  pl.BlockSpec((1, HP * AHD), lambda h: (0, h))] + gr.in_specs,
        out_specs=[pl.BlockSpec((s, HP * AHD), lambda h: (0, h)), pl.BlockSpec((s, HP * AHD), lambda h: (0, h)),
                   pl.BlockSpec((s, HP * AHD), lambda h: (0, h))] + gr.out_specs,
        out_shape=[jax.ShapeDtypeStruct((s, DATT), _CDT), jax.ShapeDtypeStruct((s, DATT), _CDT),
                   jax.ShapeDtypeStruct((s, DATT), F32)] + gr.out_shape,
        scratch_shapes=[pltpu.VMEM((HP, s + PADK, AHD), _CDT), pltpu.VMEM((HP, s + PADK, AHD), _CDT),
                        pltpu.VMEM((HP, QB, BANDW), F32)] + gr.scratch, input_output_aliases=gr.alias(3, 3),
        compiler_params=_params(("arbitrary",)),
    )(z, rbx, gatt, *gr.operands)


def _outproj(cg, ca, wout, x, gpost, ride=None, tm=256):
    s = x.shape[0]
    gr = _gather_ride(ride, 2)

    def body(*refs):
        (cg_ref, ca_ref, w_ref, x_ref, g_ref, y_ref, xo_ref), rr = gr.split(refs, 5, 2)
        gr.start(rr, pl.program_id(0) == 0)
        y = _dot(cg_ref[...], w_ref[0:DGLA, :]) + _dot(ca_ref[...], w_ref[DGLA:, :])
        y_ref[...] = y
        xo_ref[...] = x_ref[...] + y * _rms_rows(y) * g_ref[...]
        gr.wait(rr, pl.program_id(0) == s // tm - 1)

    return pl.pallas_call(
        body, name="outproj", grid=(s // tm,),
        in_specs=[pl.BlockSpec((tm, DGLA), lambda i: (i, 0)), pl.BlockSpec((tm, DATT), lambda i: (i, 0)),
                  pl.BlockSpec((D, D), lambda i: (0, 0)), pl.BlockSpec((tm, D), lambda i: (i, 0)),
                  pl.BlockSpec((1, D), lambda i: (0, 0))] + gr.in_specs,
        out_specs=[pl.BlockSpec((tm, D), lambda i: (i, 0)), pl.BlockSpec((tm, D), lambda i: (i, 0))] + gr.out_specs,
        out_shape=[jax.ShapeDtypeStruct((s, D), F32), jax.ShapeDtypeStruct((s, D), F32)] + gr.out_shape,
        scratch_shapes=gr.scratch, input_output_aliases=gr.alias(5, 2),
        compiler_params=_params(("arbitrary",)),
    )(cg, ca, wout, x, gpost, *gr.operands)


def _loss_grad(xo, tgt, tm=256):
    s = xo.shape[0]

    def body(xo_ref, t_ref, d_ref, l_ref):
        @pl.when(pl.program_id(0) == 0)
        def _():
            l_ref[...] = jnp.zeros_like(l_ref)

        e = xo_ref[...] - t_ref[...]
        d_ref[...] = e * (1.0 / D)
        l_ref[...] += jnp.sum(jnp.sum(e * e, axis=1, keepdims=True), axis=0, keepdims=True) * (0.5 / D)

    return pl.pallas_call(
        body, name="loss_grad", grid=(s // tm,),
        in_specs=[pl.BlockSpec((tm, D), lambda i: (i, 0)), pl.BlockSpec((tm, D), lambda i: (i, 0))],
        out_specs=[pl.BlockSpec((tm, D), lambda i: (i, 0)), pl.BlockSpec((1, 1), lambda i: (0, 0))],
        out_shape=[jax.ShapeDtypeStruct((s, D), F32), jax.ShapeDtypeStruct((1, 1), F32)],
        compiler_params=_params(("arbitrary",)),
    )(xo, tgt)


def _place():
    x, y, c = lax.axis_index("x"), lax.axis_index("y"), lax.axis_index("c")
    chips = [(1 - x, y), (x, 1 - y), (1 - x, 1 - y)]
    return x, y, c, chips


def _variants(fn):
    x, y, c, _ = _place()
    for jx in range(2):
        for jy in range(2):
            for jc in range(2):
                pl.when((x == jx) & (y == jy) & (c == jc))(functools.partial(fn, jx, jy, jc))


def _gather_copies(part, x, y, c, src_in, src_out, buf_in, buf_out, send_sems, recv_sems):
    me = 2 * x + y
    xn, yn, dg = (1 - x, y), (x, 1 - y), (1 - x, 1 - y)
    cps = []
    for a, (src, buf, subs, firsts) in enumerate([(src_in, buf_in, GSUB_IN, GP0_IN), (src_out, buf_out, GSUB_OUT, GP0_OUT)]):
        if part in (0, 3):
            rows = pl.ds(*firsts[0 if part == 0 else 1])
            pairs = [(src.at[c, rows], buf.at[me, c, rows], (p[0], p[1], c)) for p in (xn, yn)]
        elif part == 1:
            from_x = buf.at[2 * xn[0] + xn[1], c, pl.ds(*subs[0])]
            from_y = buf.at[2 * yn[0] + yn[1], c, pl.ds(*subs[1])]
            pairs = [(from_x, from_x, (yn[0], yn[1], c)), (from_y, from_y, (xn[0], xn[1], c))]
        else:
            pairs = [(buf.at[2 * p[0] + p[1], c], buf.at[2 * p[0] + p[1], c], (x, y, 1 - c)) for p in (xn, yn, dg)]
        for k, (src_k, dst_k, to) in enumerate(pairs):
            cps.append(pltpu.make_async_remote_copy(src_ref=src_k, dst_ref=dst_k, send_sem=send_sems.at[a, k],
                                                    recv_sem=recv_sems.at[a, k], device_id=to, device_id_type=MESH))
    return cps


class _gather_ride:
    def __init__(self, ride, part):
        self.part, self.on = part, ride is not None
        self.in_specs, self.out_specs, self.out_shape, self.scratch, self.operands, self.aliases = [], [], [], [], [], {}
        if self.on:
            self.operands = list(ride)
            self.in_specs, self.out_specs = [ANY] * 4, [ANY] * 2
            self.out_shape = [jax.ShapeDtypeStruct(t.shape, t.dtype) for t in ride[2:]]
            self.scratch = [pltpu.SemaphoreType.DMA((2, 3)), pltpu.SemaphoreType.DMA((2, 3))]

    def alias(self, n_in, n_out):
        return {n_in + 2: n_out, n_in + 3: n_out + 1} if self.on else {}

    def split(self, refs, n_in, n_out):
        if not self.on:
            return refs, None
        own = refs[:n_in] + refs[n_in + 4:n_in + 4 + n_out] + refs[n_in + 6 + n_out:-2]
        return own, refs[n_in:n_in + 2] + refs[n_in + 4 + n_out:n_in + 6 + n_out] + refs[-2:]

    def start(self, ride_refs, first):
        if self.on:
            def go(x, y, c):
                for cpy in _gather_copies(self.part, x, y, c, *ride_refs):
                    cpy.start()
            pl.when(first)(lambda: _variants(go))

    def wait(self, ride_refs, last):
        if self.on:
            def done(x, y, c):
                for cpy in _gather_copies(self.part, x, y, c, *ride_refs):
                    cpy.wait()
            pl.when(last)(lambda: _variants(done))


def _part_rows(total, split, part):
    return (0, split) if part == 0 else (split, total - split)


def _owner_copies(part, pin_ref, pout_ref, rin, rout, send_sems, recv_sems):
    x, y, c, chips = _place()
    cps = []
    for k, (px, py) in enumerate(chips):
        for a, (src, dst, split) in enumerate([(pin_ref, rin, SPLIT_IN), (pout_ref, rout, SPLIT_OUT)]):
            r0, n = _part_rows(src.shape[2], split, part)
            cps.append(pltpu.make_async_remote_copy(src_ref=src.at[0, 2 * px + py, pl.ds(r0, n)], dst_ref=dst.at[k],
                                                    send_sem=send_sems.at[a, k], recv_sem=recv_sems.at[a, k],
                                                    device_id=(px, py, c), device_id_type=MESH))
    return cps


class _ride_specs:
    def __init__(self, send, part):
        self.in_specs, self.out_specs, self.out_shape, self.scratch, self.operands = [], [], [], [], []
        if send is not None:
            self.in_specs, self.out_specs, self.operands = [ANY, ANY], [ANY, ANY], list(send)
            self.out_shape = [jax.ShapeDtypeStruct((3, _part_rows(t.shape[2], split, part)[1], D), t.dtype)
                              for t, split in zip(send, (SPLIT_IN, SPLIT_OUT))]
            self.scratch = [pltpu.SemaphoreType.DMA((2, 3)), pltpu.SemaphoreType.DMA((2, 3))]


def _ride_refs(refs, send, n_out):
    if send is None:
        return None, refs
    pin_ref, pout_ref = refs[:2]
    own_out = refs[2:2 + n_out]
    rin, rout = refs[2 + n_out:4 + n_out]
    return (pin_ref, pout_ref, rin, rout, refs[-2], refs[-1]), tuple(own_out) + tuple(refs[4 + n_out:-2])


def _ride_start(refs, send, n_out, part, first=None):
    ride = _ride_refs(refs, send, n_out)[0]
    if ride is None:
        return []
    cps = _owner_copies(part, *ride)

    @pl.when(pl.program_id(0) == 0 if first is None else first)
    def _():
        for cpy in cps:
            cpy.start()

    return cps


def _ride_wait(cps, steps, last=None):
    if cps:
        @pl.when(pl.program_id(0) == steps - 1 if last is None else last)
        def _():
            for cpy in cps:
                cpy.wait()


def _post_bwd(dout, y, gpost, wout, swap=None, tm=256):
    s = y.shape[0]

    def body(*refs):
        d_ref, y_ref, g_ref, w_ref = refs[:4]
        dy_ref, dcg_ref, dca_ref, dg_ref = refs[6:10] if swap is not None else refs[4:8]
        cps = []
        if swap is not None:
            x, yy, c, _ = _place()
            cps = [pltpu.make_async_remote_copy(src_ref=refs[4 + a].at[1 - c], dst_ref=refs[10 + a], send_sem=refs[12].at[a],
                                                recv_sem=refs[13].at[a], device_id=(x, yy, 1 - c), device_id_type=MESH) for a in range(2)]

            @pl.when(pl.program_id(0) == 0)
            def _():
                for cpy in cps:
                    cpy.start()

        @pl.when(pl.program_id(0) == 0)
        def _():
            dg_ref[...] = jnp.zeros_like(dg_ref)

        yv = y_ref[...]
        r = _rms_rows(yv)
        yh = yv * r
        dv = d_ref[...]
        dg_ref[...] += jnp.sum(dv * yh, axis=0, keepdims=True)
        dn = dv * g_ref[...]
        dyb = (r * (dn - yh * jnp.mean(dn * yh, axis=-1, keepdims=True))).astype(_CDT)
        dy_ref[...] = dyb
        dcg_ref[...] = _dot_nt(dyb, w_ref[0:DGLA, :]).astype(_CDT)
        dca_ref[...] = _dot_nt(dyb, w_ref[DGLA:, :]).astype(_CDT)
        _ride_wait(cps, s // tm)

    extra = [] if swap is None else list(swap)
    return pl.pallas_call(
        body, name="post_bwd", grid=(s // tm,),
        in_specs=[pl.BlockSpec((tm, D), lambda i: (i, 0)), pl.BlockSpec((tm, D), lambda i: (i, 0)),
                  pl.BlockSpec((1, D), lambda i: (0, 0)), pl.BlockSpec((D, D), lambda i: (0, 0))] + [ANY] * len(extra),
        out_specs=[pl.BlockSpec((tm, D), lambda i: (i, 0)), pl.BlockSpec((tm, DGLA), lambda i: (i, 0)),
                   pl.BlockSpec((tm, DATT), lambda i: (i, 0)), pl.BlockSpec((1, D), lambda i: (0, 0))] + [ANY] * len(extra),
        out_shape=[jax.ShapeDtypeStruct((s, D), _CDT), jax.ShapeDtypeStruct((s, DGLA), _CDT),
                   jax.ShapeDtypeStruct((s, DATT), _CDT), jax.ShapeDtypeStruct((1, D), F32)]
        + [jax.ShapeDtypeStruct(t.shape[1:], t.dtype) for t in extra],
        scratch_shapes=[pltpu.SemaphoreType.DMA((2,)), pltpu.SemaphoreType.DMA((2,))] if extra else [],
        compiler_params=_params(("arbitrary",)),
    )(dout, y, gpost, wout, *extra)


def _matmul_tn(a, b, out_dtype, tm, tn, name):
    k, m = a.shape
    n = b.shape[1]

    def body(a_ref, b_ref, o_ref):
        o_ref[...] = _dot_tn(a_ref[...], b_ref[...]).astype(out_dtype)

    return pl.pallas_call(
        body, name=name, grid=(m // tm, n // tn),
        in_specs=[pl.BlockSpec((k, tm), lambda i, j: (0, i)), pl.BlockSpec((k, tn), lambda i, j: (0, j))],
        out_specs=pl.BlockSpec((tm, tn), lambda i, j: (i, j)),
        out_shape=jax.ShapeDtypeStruct((m, n), out_dtype),
        compiler_params=_params(("parallel", "parallel")),
    )(a, b)


def _att_bwd(z, oraw, lse, dca, rbx, gatt, send=None):
    s = z.shape[0]
    nb = s // QB

    def body(*refs):
        za_ref, o_ref, lse_ref, dc_ref, rb_ref, g_ref = refs[:6]
        dz_ref, dg_ref, db_ref, kp, vp, dkp, dvp, bias_s, t_s, dg_s = _ride_refs(refs[6:], send, 3)[1]
        cps = _ride_start(refs[6:], send, 3, 0)
        for p in range(HP):
            z0 = p * AW
            kp[p, 0:PADK, :] = jnp.zeros((PADK, AHD), _CDT)
            vp[p, 0:PADK, :] = jnp.zeros((PADK, AHD), _CDT)
            for r0 in range(0, s, ROWB):
                kp[p, PADK + r0:PADK + r0 + ROWB, :] = za_ref[r0:r0 + ROWB, z0 + AHD:z0 + 2 * AHD]
                vp[p, PADK + r0:PADK + r0 + ROWB, :] = za_ref[r0:r0 + ROWB, z0 + 2 * AHD:z0 + 3 * AHD]
            bias_s[p] = jnp.where(_band_static_mask(), _band_bias(rb_ref[p]), NEG)
        for p in range(HP):
            for r0 in range(0, s + PADK, ROWB):
                dkp[p, r0:r0 + ROWB, :] = jnp.zeros((ROWB, AHD), F32)
                dvp[p, r0:r0 + ROWB, :] = jnp.zeros((ROWB, AHD), F32)
        t_s[...] = jnp.zeros_like(t_s)
        dg_s[...] = jnp.zeros_like(dg_s)

        def step(b, carry):
            r0 = pl.multiple_of(b * QB, QB)
            rows = pl.ds(r0, QB)
            band = pl.ds(r0, BANDW)
            live = lax.broadcasted_iota(jnp.int32, (QB, BANDW), 1) >= PADK - r0
            for p in range(HP):
                z0 = p * AW
                cols = slice(p * AHD, (p + 1) * AHD)
                o = o_ref[rows, cols].astype(F32)
                do, dgate, dg = _norm_gate_bwd(o, g_ref[:, cols], za_ref[rows, z0 + 3 * AHD:z0 + AW].astype(F32),
                                               dc_ref[rows, cols].astype(F32))
                dg_s[:, cols] += dg
                q = za_ref[rows, z0:z0 + AHD]
                kb = kp[p, band, :]
                sc = _dot_nt(q, kb) * (AHD ** -0.5) + bias_s[p]
                sc = jnp.where(live, sc, NEG)
                pr = jnp.exp(sc - jnp.max(lse_ref[rows, cols], axis=-1, keepdims=True))
                dob = do.astype(_CDT)
                dp = _dot_nt(dob, vp[p, band, :])
                ds = pr * (dp - jnp.sum(do * o, axis=-1, keepdims=True))
                t_s[p] += ds
                dsb = (ds * (AHD ** -0.5)).astype(_CDT)
                dz_ref[rows, z0:z0 + AHD] = _dot(dsb, kb).astype(_CDT)
                dz_ref[rows, z0 + 3 * AHD:z0 + AW] = dgate.astype(_CDT)
                dkp[p, band, :] += _dot_tn(dsb, q)
                dvp[p, band, :] += _dot_tn(pr.astype(_CDT), dob)
            return carry

        lax.fori_loop(0, nb, step, 0, unroll=2)
        for p in range(HP):
            z0 = p * AW
            for r0 in range(0, s, ROWB):
                dz_ref[r0:r0 + ROWB, z0 + AHD:z0 + 2 * AHD] = dkp[p, PADK + r0:PADK + r0 + ROWB, :].astype(_CDT)
                dz_ref[r0:r0 + ROWB, z0 + 2 * AHD:z0 + 3 * AHD] = dvp[p, PADK + r0:PADK + r0 + ROWB, :].astype(_CDT)
            db_ref[p] = _fold_bias_grad(t_s[p])
        dg_ref[...] = dg_s[...]
        _ride_wait(cps, AH // HP)

    ride = _ride_specs(send, 0)
    return pl.pallas_call(
        body, name="att_bwd", grid=(AH // HP,),
        in_specs=[pl.BlockSpec((s, HP * AW), lambda h: (0, ZG // (HP * AW) + h)), pl.BlockSpec((s, HP * AHD), lambda h: (0, h)),
                  pl.BlockSpec((s, HP * AHD), lambda h: (0, h)), pl.BlockSpec((s, HP * AHD), lambda h: (0, h)),
                  pl.BlockSpec((HP, 1, 256), lambda h: (h, 0, 0)), pl.BlockSpec((1, HP * AHD), lambda h: (0, h))] + ride.in_specs,
        out_specs=[pl.BlockSpec((s, HP * AW), lambda h: (0, h)), pl.BlockSpec((1, HP * AHD), lambda h: (0, h)),
                   pl.BlockSpec((HP, 1, 256), lambda h: (h, 0, 0))] + ride.out_specs,
        out_shape=[jax.ShapeDtypeStruct((s, ZA), _CDT), jax.ShapeDtypeStruct((1, DATT), F32),
                   jax.ShapeDtypeStruct((AH, 1, 256), F32)] + ride.out_shape,
        scratch_shapes=[pltpu.VMEM((HP, s + PADK, AHD), _CDT), pltpu.VMEM((HP, s + PADK, AHD), _CDT),
                        pltpu.VMEM((HP, s + PADK, AHD), F32), pltpu.VMEM((HP, s + PADK, AHD), F32),
                        pltpu.VMEM((HP, QB, BANDW), F32), pltpu.VMEM((HP, QB, BANDW), F32), pltpu.VMEM((1, HP * AHD), F32)] + ride.scratch,
        compiler_params=_params(("arbitrary",)),
    )(z, oraw, lse, dca, rbx, gatt, *ride.operands)


def _gla_bwd(z, zga, wa, ba, ggla, dcg, send=None):
    s = z.shape[0]
    nc = s // CHUNK

    def body(*refs):
        zg_ref, zga_ref, wa_ref, ba_ref, g_ref, dc_ref = refs[:6]
        (dz_ref, dga_ref, dwa_ref, dba_ref, dg_ref,
         la_s, om_s, sall, dpre_s, c_s, dga_s, dg_s) = _ride_refs(refs[6:], send, 5)[1]
        cps = _ride_start(refs[6:], send, 5, 1)
        h = pl.program_id(0)
        pre = _dot(zga_ref[...], wa_ref[...]) + ba_ref[...]
        la_s[...] = _log_sigmoid(pre) * (1.0 / TAU)
        om_s[...] = (1.0 - _sigmoid(pre)) * (1.0 / TAU)
        c_s[...] = jnp.zeros_like(c_s)
        dg_s[...] = jnp.zeros_like(dg_s)
        tri = _tri(False)
        tri_strict = _tri(True)

        def decay(rows, p):
            la = la_s[rows, p * GDK:(p + 1) * GDK]
            lend = jnp.sum(la, axis=0, keepdims=True)
            return jnp.exp(lend - _exact_dot(tri, la)), jnp.exp(lend)

        def fwd(n, sts):
            rows = pl.ds(pl.multiple_of(n * CHUNK, CHUNK), CHUNK)
            out = []
            for p in range(HP):
                z0 = p * GW
                dec, a = decay(rows, p)
                kdec = (zg_ref[rows, z0 + GDK:z0 + 2 * GDK].astype(F32) * dec).astype(_CDT)
                stn = a * sts[p] + _dot_tn(zg_ref[rows, z0 + 2 * GDK:z0 + 2 * GDK + GDV], kdec)
                sall[p, n] = stn
                out.append(stn)
            return tuple(out)

        lax.fori_loop(0, nc, fwd, tuple(jnp.zeros((GDV, GDK), F32) for _ in range(HP)), unroll=4)

        def bwd(i, carry):
            n = nc - 1 - i
            rows = pl.ds(pl.multiple_of(n * CHUNK, CHUNK), CHUNK)
            for p in range(HP):
                z0 = p * GW
                kc = slice(p * GDK, (p + 1) * GDK)
                vc = slice(p * GDV, (p + 1) * GDV)
                dec, a = decay(rows, p)
                kdec = zg_ref[rows, z0 + GDK:z0 + 2 * GDK].astype(F32) * dec
                kdb = kdec.astype(_CDT)
                v = zg_ref[rows, z0 + 2 * GDK:z0 + 2 * GDK + GDV]
                qs = (zg_ref[rows, z0:z0 + GDK].astype(F32) * (GDK ** -0.5)).astype(_CDT)
                stb = sall[p, n].astype(_CDT)
                st_prev = sall[p, jnp.maximum(n - 1, 0)] * jnp.where(n > 0, 1.0, 0.0)
                o = _dot_nt(qs, stb)
                do, dgate, dg = _norm_gate_bwd(o, g_ref[:, vc], zg_ref[rows, z0 + 2 * GDK + GDV:z0 + GW].astype(F32),
                                               dc_ref[rows, vc].astype(F32))
                dg_s[:, vc] += dg
                dob = do.astype(_CDT)
                gt = _dot_tn(dob, qs) + c_s[p]
                gtb = gt.astype(_CDT)
                da = jnp.sum(gt * st_prev, axis=0, keepdims=True)
                dkdec = _dot(v, gtb)
                dla = _exact_dot(tri_strict, dkdec * kdec) + da * a
                dpre_s[rows, kc] = dla * om_s[rows, kc]
                dz_ref[rows, z0:z0 + GDK] = (_dot(dob, stb) * (GDK ** -0.5)).astype(_CDT)
                dz_ref[rows, z0 + GDK:z0 + 2 * GDK] = (dkdec * dec).astype(_CDT)
                dz_ref[rows, z0 + 2 * GDK:z0 + 2 * GDK + GDV] = _dot_nt(kdb, gtb).astype(_CDT)
                dz_ref[rows, z0 + 2 * GDK + GDV:z0 + GW] = dgate.astype(_CDT)
                c_s[p] = a * gt
            return carry

        lax.fori_loop(0, nc, bwd, 0)
        dpre = dpre_s[...]
        dpb = dpre.astype(_CDT)
        dg_ref[...] = dg_s[...]
        dba_ref[...] = jnp.sum(dpre, axis=0, keepdims=True)
        dwa_ref[...] = _dot_tn(zga_ref[...], dpb)
        part = _dot_nt(dpb, wa_ref[...])

        @pl.when(h == 0)
        def _():
            dga_s[...] = part

        @pl.when(h > 0)
        def _():
            dga_s[...] += part

        @pl.when(h == GH // HP - 1)
        def _():
            dga_ref[...] = dga_s[...].astype(_CDT)

        _ride_wait(cps, GH // HP)

    ride = _ride_specs(send, 1)
    return pl.pallas_call(
        body, name="gla_bwd", grid=(GH // HP,),
        in_specs=[pl.BlockSpec((s, HP * GW), lambda h: (0, h)), pl.BlockSpec((s, GAP), lambda h: (0, 0)),
                  pl.BlockSpec((GAP, HP * GDK), lambda h: (0, h)), pl.BlockSpec((1, HP * GDK), lambda h: (0, h)),
                  pl.BlockSpec((1, HP * GDV), lambda h: (0, h)), pl.BlockSpec((s, HP * GDV), lambda h: (0, h))] + ride.in_specs,
        out_specs=[pl.BlockSpec((s, HP * GW), lambda h: (0, h)), pl.BlockSpec((s, GAP), lambda h: (0, 0)),
                   pl.BlockSpec((GAP, HP * GDK), lambda h: (0, h)), pl.BlockSpec((1, HP * GDK), lambda h: (0, h)),
                   pl.BlockSpec((1, HP * GDV), lambda h: (0, h))] + ride.out_specs,
        out_shape=[jax.ShapeDtypeStruct((s, ZG), _CDT), jax.ShapeDtypeStruct((s, GAP), _CDT),
                   jax.ShapeDtypeStruct((GAP, GH * GDK), F32), jax.ShapeDtypeStruct((1, GH * GDK), F32),
                   jax.ShapeDtypeStruct((1, DGLA), F32)] + ride.out_shape,
        scratch_shapes=[pltpu.VMEM((s, HP * GDK), F32), pltpu.VMEM((s, HP * GDK), F32), pltpu.VMEM((HP, nc, GDV, GDK), F32),
                        pltpu.VMEM((s, HP * GDK), F32), pltpu.VMEM((HP, GDV, GDK), F32), pltpu.VMEM((s, GAP), F32),
                        pltpu.VMEM((1, HP * GDV), F32)] + ride.scratch,
        compiler_params=_params(("arbitrary",)),
    )(z, zga, wa, ba, ggla, dcg, *ride.operands)


def _dh(dzg, dza, dga, wm, wga, x, dout, gpre, send=None, tm=512, tk=1024):
    s = x.shape[0]
    nkg, nk = ZG // tk, ZM // tk

    def body(*refs):
        dzg_ref, dza_ref, dga_ref, wm_ref, wga_ref, x_ref, d_ref, g_ref = refs[:8]
        dx_ref, dg_ref, acc = _ride_refs(refs[8:], send, 2)[1]
        i, k = pl.program_id(0), pl.program_id(1)
        cps = _ride_start(refs[8:], send, 2, 1, (i == 0) & (k == 0))

        @pl.when((i == 0) & (k == 0))
        def _():
            dg_ref[...] = jnp.zeros_like(dg_ref)

        @pl.when(k == 0)
        def _():
            acc[...] = _dot(dga_ref[...], wga_ref[...])

        @pl.when(k < nkg)
        def _():
            acc[...] += _dot(dzg_ref[...], wm_ref[...])

        @pl.when(k >= nkg)
        def _():
            acc[...] += _dot(dza_ref[...], wm_ref[...])

        @pl.when(k == nk - 1)
        def _():
            xv = x_ref[...]
            r = _rms_rows(xv)
            xh = xv * r
            dh = acc[...]
            dg_ref[...] += jnp.sum(dh * xh, axis=0, keepdims=True)
            dn = dh * g_ref[...]
            dx_ref[...] = d_ref[...] + r * (dn - xh * jnp.mean(dn * xh, axis=-1, keepdims=True))

        _ride_wait(cps, 0, (i == s // tm - 1) & (k == nk - 1))

    ride = _ride_specs(send, 1)
    return pl.pallas_call(
        body, name="dh", grid=(s // tm, nk),
        in_specs=[pl.BlockSpec((tm, tk), lambda i, k: (i, jnp.minimum(k, nkg - 1))),
                  pl.BlockSpec((tm, tk), lambda i, k: (i, jnp.maximum(k - nkg, 0))),
                  pl.BlockSpec((tm, GAP), lambda i, k: (i, 0)), pl.BlockSpec((tk, D), lambda i, k: (k, 0)),
                  pl.BlockSpec((GAP, D), lambda i, k: (0, 0)), pl.BlockSpec((tm, D), lambda i, k: (i, 0)),
                  pl.BlockSpec((tm, D), lambda i, k: (i, 0)), pl.BlockSpec((1, D), lambda i, k: (0, 0))] + ride.in_specs,
        out_specs=[pl.BlockSpec((tm, D), lambda i, k: (i, 0)), pl.BlockSpec((1, D), lambda i, k: (0, 0))] + ride.out_specs,
        out_shape=[jax.ShapeDtypeStruct((s, D), F32), jax.ShapeDtypeStruct((1, D), F32)] + ride.out_shape,
        scratch_shapes=[pltpu.VMEM((tm, D), F32)] + ride.scratch,
        compiler_params=_params(("arbitrary", "arbitrary")),
    )(dzg, dza, dga, wm, wga, x, dout, gpre, *ride.operands)


def _adam(w, g, m, v, tr, name):
    rws, cols = w.shape

    def body(w_ref, g_ref, m_ref, v_ref, d_ref, mo_ref, vo_ref):
        gv = g_ref[...]
        mn = ADAM_B1 * m_ref[...] + (1.0 - ADAM_B1) * gv
        vn = ADAM_B2 * v_ref[...] + (1.0 - ADAM_B2) * (gv * gv)
        mh = mn / (1.0 - ADAM_B1 ** ADAM_STEP)
        vh = vn / (1.0 - ADAM_B2 ** ADAM_STEP)
        d_ref[...] = -ADAM_LR * (mh / (jnp.sqrt(vh) + ADAM_EPS) + ADAM_WD * w_ref[...])
        mo_ref[...] = mn
        vo_ref[...] = vn

    spec = pl.BlockSpec((tr, cols), lambda i: (i, 0))
    return pl.pallas_call(
        body, name=name, grid=(rws // tr,), in_specs=[spec] * 4, out_specs=[spec] * 3,
        out_shape=[jax.ShapeDtypeStruct((rws, cols), F32)] * 3,
        compiler_params=_params(("parallel",)),
    )(w, g, m, v)


def _adam_rows(c_idx, w, g_own, g_other, m, v, tj, name):
    rows, nl = w.shape[:2]
    per_half = g_own.shape[1] // tj

    def body(c_ref, w_ref, go_ref, gx_ref, m_ref, v_ref, g_ref, d_ref, mo_ref, vo_ref):
        mine = pl.program_id(0) // per_half == c_ref[0]
        for l in range(nl):
            gv = jnp.where(mine, go_ref[l], gx_ref[l])
            mn = ADAM_B1 * m_ref[:, l, :] + (1.0 - ADAM_B1) * gv
            vn = ADAM_B2 * v_ref[:, l, :] + (1.0 - ADAM_B2) * (gv * gv)
            mh = mn / (1.0 - ADAM_B1 ** ADAM_STEP)
            vh = vn / (1.0 - ADAM_B2 ** ADAM_STEP)
            g_ref[:, l, :] = gv
            d_ref[:, l, :] = -ADAM_LR * (mh / (jnp.sqrt(vh) + ADAM_EPS) + ADAM_WD * w_ref[:, l, :])
            mo_ref[:, l, :] = mn
            vo_ref[:, l, :] = vn

    full = pl.BlockSpec((tj,) + w.shape[1:], lambda i, c_ref: (i, 0, 0))
    own = pl.BlockSpec((nl, tj, D), lambda i, c_ref: (0, jnp.where(i // per_half == c_ref[0], i % per_half, 0), 0))
    other = pl.BlockSpec((nl, tj, D), lambda i, c_ref: (0, jnp.where(i // per_half == c_ref[0], 0, i % per_half), 0))
    return pl.pallas_call(
        body, name=name,
        grid_spec=pltpu.PrefetchScalarGridSpec(num_scalar_prefetch=1, grid=(pl.cdiv(rows, tj),),
                                               in_specs=[full, own, other, full, full], out_specs=[full] * 4),
        out_shape=[jax.ShapeDtypeStruct(w.shape, F32)] * 4,
        compiler_params=_params(("parallel",)),
    )(c_idx, w, g_own, g_other, m, v)


def _gather_first(src_in, src_out, wa, buf_in, buf_out, wa_all):
    def variant(x, y, c, src_in_ref, src_out_ref, wa_ref, _b0, _b1, _b2, bin_ref, bout_ref, wa_ref_all, send_sems, recv_sems, wa_send, wa_recv):
        for phase in (0, 3, 1, 2):
            cps = _gather_copies(phase, x, y, c, src_in_ref, src_out_ref, bin_ref, bout_ref, send_sems.at[phase], recv_sems.at[phase])
            if phase == 0:
                for k, (px, py) in enumerate([(1 - x, y), (x, 1 - y), (1 - x, 1 - y)]):
                    cps.append(pltpu.make_async_remote_copy(src_ref=wa_ref, dst_ref=wa_ref_all.at[2 * x + y], send_sem=wa_send.at[k],
                                                            recv_sem=wa_recv.at[k], device_id=(px, py, c), device_id_type=MESH))
            for cpy in cps:
                cpy.start()
            for cpy in cps:
                cpy.wait()

    def body(*refs):
        _variants(lambda x, y, c: variant(x, y, c, *refs))

    return pl.pallas_call(
        body, name="gather_first", in_specs=[ANY] * 6, out_specs=[ANY] * 3, input_output_aliases={3: 0, 4: 1, 5: 2},
        out_shape=[jax.ShapeDtypeStruct(t.shape, t.dtype) for t in (buf_in, buf_out, wa_all)],
        scratch_shapes=[pltpu.SemaphoreType.DMA((4, 2, 3)), pltpu.SemaphoreType.DMA((4, 2, 3)),
                        pltpu.SemaphoreType.DMA((3,)), pltpu.SemaphoreType.DMA((3,))],
    )(src_in, src_out, wa, buf_in, buf_out, wa_all)


def _swap_halves(gin2, gout2):
    def body(gin_ref, gout_ref, rin, rout, send_sems, recv_sems):
        x, y, c, _ = _place()
        sib = (x, y, 1 - c)
        cps = [pltpu.make_async_remote_copy(src_ref=src.at[1 - c], dst_ref=dst, send_sem=send_sems.at[a],
                                            recv_sem=recv_sems.at[a], device_id=sib, device_id_type=MESH)
               for a, (src, dst) in enumerate([(gin_ref, rin), (gout_ref, rout)])]
        for cpy in cps:
            cpy.start()
        for cpy in cps:
            cpy.wait()

    return pl.pallas_call(
        body, name="swap_halves", in_specs=[ANY, ANY], out_specs=[ANY, ANY],
        out_shape=[jax.ShapeDtypeStruct(gin2.shape[1:], gin2.dtype), jax.ShapeDtypeStruct(gout2.shape[1:], gout2.dtype)],
        scratch_shapes=[pltpu.SemaphoreType.DMA((2,)), pltpu.SemaphoreType.DMA((2,))],
    )(gin2, gout2)


def _add_halves(c_idx, g2, r, tr, name):
    rows, cols = r.shape

    def body(c_ref, g_ref, r_ref, o_ref):
        o_ref[...] = (g_ref[0].astype(F32) + r_ref[...].astype(F32)).astype(_XDT)

    return pl.pallas_call(
        body, name=name,
        grid_spec=pltpu.PrefetchScalarGridSpec(
            num_scalar_prefetch=1, grid=(rows // tr,),
            in_specs=[pl.BlockSpec((1, tr, cols), lambda i, c_ref: (c_ref[0], i, 0)),
                      pl.BlockSpec((tr, cols), lambda i, c_ref: (i, 0))],
            out_specs=pl.BlockSpec((tr, cols), lambda i, c_ref: (i, 0))),
        out_shape=jax.ShapeDtypeStruct((rows, cols), _XDT),
        compiler_params=_params(("parallel",)),
    )(c_idx, g2, r)


def _send_to_owners(pin, pout, parts):
    def body(pin_ref, pout_ref, *refs):
        send_sems, recv_sems = refs[-2:]
        cps = []
        for n, part in enumerate(parts):
            cps += _owner_copies(part, pin_ref, pout_ref, refs[2 * n], refs[2 * n + 1], send_sems.at[n], recv_sems.at[n])
        for cpy in cps:
            cpy.start()
        for cpy in cps:
            cpy.wait()

    shapes = [jax.ShapeDtypeStruct((3, _part_rows(t.shape[2], split, part)[1], D), t.dtype)
              for part in parts for t, split in zip((pin, pout), (SPLIT_IN, SPLIT_OUT))]
    return pl.pallas_call(
        body, name="send_to_owners", in_specs=[ANY, ANY], out_specs=[ANY] * len(shapes), out_shape=shapes,
        scratch_shapes=[pltpu.SemaphoreType.DMA((len(parts), 2, 3)), pltpu.SemaphoreType.DMA((len(parts), 2, 3))],
    )(pin, pout)


def _add_chips(chip_idx, p, ra, rb, tr, name):
    rows = p.shape[2]
    na = ra.shape[1] // tr

    def body(c_ref, p_ref, ra_ref, rb_ref, o_ref):
        r = jnp.where(pl.program_id(0) < na, ra_ref[...], rb_ref[...]).astype(F32)
        o_ref[0] = ((p_ref[0, 0].astype(F32) + r[0]) + r[1]) + r[2]

    return pl.pallas_call(
        body, name=name,
        grid_spec=pltpu.PrefetchScalarGridSpec(
            num_scalar_prefetch=1, grid=(rows // tr,),
            in_specs=[pl.BlockSpec((1, 1, tr, D), lambda i, c_ref: (0, c_ref[0], i, 0)),
                      pl.BlockSpec((3, tr, D), lambda i, c_ref: (0, jnp.minimum(i, na - 1), 0)),
                      pl.BlockSpec((3, tr, D), lambda i, c_ref: (0, jnp.maximum(i - na, 0), 0))],
            out_specs=pl.BlockSpec((1, tr, D), lambda i, c_ref: (0, i, 0))),
        out_shape=jax.ShapeDtypeStruct((1, rows, D), F32),
        compiler_params=_params(("parallel",)),
    )(chip_idx, p, ra, rb)


def _exchange_halves(arrs):
    n = len(arrs)

    def body(*refs):
        x, y, c, _ = _place()
        cps = [pltpu.make_async_remote_copy(src_ref=refs[a], dst_ref=refs[n + a], send_sem=refs[2 * n].at[a], recv_sem=refs[2 * n + 1].at[a],
                                            device_id=(x, y, 1 - c), device_id_type=MESH) for a in range(n)]
        for cpy in cps:
            cpy.start()
        for cpy in cps:
            cpy.wait()

    return pl.pallas_call(
        body, name="exchange_halves", in_specs=[ANY] * n, out_specs=[ANY] * n,
        out_shape=[jax.ShapeDtypeStruct(t.shape, t.dtype) for t in arrs],
        scratch_shapes=[pltpu.SemaphoreType.DMA((n,)), pltpu.SemaphoreType.DMA((n,))],
    )(*arrs)


def _adam_halves(c_idx, w, g_own, g_other, m, v, tr, name):
    nl, _, rows, cols = w.shape

    def body(c_ref, w_ref, go_ref, gx_ref, m_ref, v_ref, g_ref, d_ref, mo_ref, vo_ref):
        gv = jnp.where(pl.program_id(1) == c_ref[0], go_ref[0], gx_ref[0])
        mn = ADAM_B1 * m_ref[0, 0] + (1.0 - ADAM_B1) * gv
        vn = ADAM_B2 * v_ref[0, 0] + (1.0 - ADAM_B2) * (gv * gv)
        mh = mn / (1.0 - ADAM_B1 ** ADAM_STEP)
        vh = vn / (1.0 - ADAM_B2 ** ADAM_STEP)
        g_ref[0, 0] = gv
        d_ref[0, 0] = -ADAM_LR * (mh / (jnp.sqrt(vh) + ADAM_EPS) + ADAM_WD * w_ref[0, 0])
        mo_ref[0, 0] = mn
        vo_ref[0, 0] = vn

    full = pl.BlockSpec((1, 1, tr, cols), lambda l, hh, i, c_ref: (l, hh, i, 0))
    own = pl.BlockSpec((1, tr, cols), lambda l, hh, i, c_ref: (l, jnp.where(hh == c_ref[0], i, 0), 0))
    other = pl.BlockSpec((1, tr, cols), lambda l, hh, i, c_ref: (l, jnp.where(hh == c_ref[0], 0, i), 0))
    return pl.pallas_call(
        body, name=name,
        grid_spec=pltpu.PrefetchScalarGridSpec(
            num_scalar_prefetch=1, grid=(nl, 2, rows // tr),
            in_specs=[full, own, other, full, full], out_specs=[full] * 4),
        out_shape=[jax.ShapeDtypeStruct(w.shape, F32)] * 4,
        compiler_params=_params(("parallel", "parallel", "parallel")),
    )(c_idx, w, g_own, g_other, m, v)


def _allreduce_small(sg):
    rows = sg.shape[0]
    vm = pl.BlockSpec(memory_space=pltpu.VMEM)

    def body(sg_ref, tot_ref, all_ref, send_sems, recv_sems):
        x, y, c, _ = _place()
        me = 4 * x + 2 * y + c
        all_ref[me] = sg_ref[...]
        cps = []
        for mask in range(1, 8):
            to = (1 - x if mask & 4 else x, 1 - y if mask & 2 else y, 1 - c if mask & 1 else c)
            cps.append(pltpu.make_async_remote_copy(src_ref=sg_ref, dst_ref=all_ref.at[me], send_sem=send_sems.at[mask - 1],
                                                    recv_sem=recv_sems.at[mask - 1], device_id=to, device_id_type=MESH))
        for cpy in cps:
            cpy.start()
        for cpy in cps:
            cpy.wait()
        acc = all_ref[0]
        for d in range(1, 8):
            acc = acc + all_ref[d]
        tot_ref[...] = acc

    return pl.pallas_call(
        body, name="allreduce_small", in_specs=[vm], out_specs=[vm, vm],
        out_shape=[jax.ShapeDtypeStruct((rows, 128), F32), jax.ShapeDtypeStruct((8, rows, 128), F32)],
        scratch_shapes=[pltpu.SemaphoreType.DMA((7,)), pltpu.SemaphoreType.DMA((7,))],
        compiler_params=_params(),
    )(sg)[0]


_CUTS = [0, 512, 1024, 2048, 3072, 3088, 4112, 5136, 6160, 7184]


def _rows_to_internal(w):
    tail = w.shape[1:]
    gq, gk, gv, gg, ga, aq, ak, av, ag = [w[_CUTS[i]:_CUTS[i + 1]] for i in range(9)]
    g = jnp.concatenate([gq.reshape((GH, GDK) + tail), gk.reshape((GH, GDK) + tail),
                         gv.reshape((GH, GDV) + tail), gg.reshape((GH, GDV) + tail)], axis=1).reshape((ZG,) + tail)
    a = jnp.concatenate([t.reshape((AH, AHD) + tail) for t in (aq, ak, av, ag)], axis=1).reshape((ZA,) + tail)
    pad = [(0, GAP - RANK)] + [(0, 0)] * len(tail)
    return jnp.concatenate([g, a], axis=0), jnp.pad(ga, pad)


def _rows_from_internal(g, a, ga):
    tail = g.shape[1:]
    g = g.reshape((GH, GW) + tail)
    a = a.reshape((AH, AW) + tail)
    parts = [g[:, 0:GDK], g[:, GDK:2 * GDK], g[:, 2 * GDK:2 * GDK + GDV], g[:, 2 * GDK + GDV:GW]]
    parts = [t.reshape((-1,) + tail) for t in parts] + [ga[0:RANK]]
    parts += [a[:, i * AHD:(i + 1) * AHD].reshape((-1,) + tail) for i in range(4)]
    return jnp.concatenate(parts, axis=0)


def _pack_rows(parts):
    rows = []
    for t in parts:
        flat = t.reshape(-1)
        rows.append(jnp.pad(flat, (0, (-flat.shape[0]) % 128)).reshape(-1, 128))
    buf = jnp.concatenate(rows, axis=0)
    return jnp.pad(buf, ((0, (-buf.shape[0]) % 8), (0, 0)))


def _unpack_rows(buf, shapes):
    out, r = [], 0
    for shp in shapes:
        n = 1
        for d in shp:
            n *= d
        nr = -(-n // 128)
        out.append(buf[r:r + nr].reshape(-1)[:n].reshape(shp))
        r += nr
    return out


def _layer_fwd(x, wm, wga, wout, gpre, gpost, wa, ba, ggla, gatt, rbx, ride=None):
    z, zga, h, *bufs = _inproj(x, gpre, wm, wga, ride)
    ride = None if ride is None else (ride[0], ride[1], *bufs)
    cg, *bufs = _gla_fwd(z, zga, wa, ba, ggla, ride)
    ride = None if ride is None else (ride[0], ride[1], *bufs)
    ca, oraw, lse, *bufs = _att_fwd(z, rbx, gatt, ride)
    ride = None if ride is None else (ride[0], ride[1], *bufs)
    y, xo, *bufs = _outproj(cg, ca, wout, x, gpost, ride)
    return xo, (x, z, zga, h, cg, ca, oraw, lse, y), bufs


def _layer_bwd(dout, saved, wm, wga, wout, gpre, gpost, wa, ba, ggla, gatt, rbx, swap=None, finish=None, reduce_now=None):
    x, z, zga, h, cg, ca, oraw, lse, y = saved
    dy, dcg, dca, dgpost, *swapped = _post_bwd(dout, y, gpost, wout, swap)
    send = None if swap is None else finish(swap, swapped)
    dwout = jnp.concatenate([_matmul_tn(cg, dy, _XDT, 512, 1024, "dwout_gla"),
                             _matmul_tn(ca, dy, _XDT, 512, 1024, "dwout_att")], axis=0)
    dza, dgatt, dbx, *got_a = _att_bwd(z, oraw, lse, dca, rbx, gatt, send)
    dzg, dga, dwa, dba, dggla, *got_b = _gla_bwd(z, zga, wa, ba, ggla, dcg, send)
    dwin = (_matmul_tn(dzg, h, _XDT, 512, 1024, "dwin_gla"), _matmul_tn(dza, h, _XDT, 512, 1024, "dwin_att"),
            _matmul_tn(dga, h, _XDT, GAP, 1024, "dwin_gate"))
    own = None if reduce_now is None else reduce_now(dwin, dwout)
    dx, dgpre, *got_own = _dh(dzg, dza, dga, wm, wga, x, dout, gpre, own)
    drb = jnp.concatenate([jnp.zeros((AH, 1), F32), dbx[:, 0, ::-1]], axis=1)
    return dx, dwin, dwout, (dgpre[0], dgpost[0], dwa[0:RANK], dba[0], dggla[0], dgatt[0], drb), send, got_a + got_b, own, got_own


def _rel_rows(rb):
    return rb[:, :0:-1][:, None, :]


def kernel(x, w_in, w_out, g_pre, g_post, w_alpha, b_alpha, g_gla, g_att, rel_bias, loss_target, m_w_in, m_w_out, m_g_pre, m_g_post, m_w_alpha, m_b_alpha, m_g_gla, m_g_att, m_rel_bias, v_w_in, v_w_out, v_g_pre, v_g_post, v_w_alpha, v_b_alpha, v_g_gla, v_g_att, v_rel_bias):
    nl = w_in.shape[0]
    ax, ay, ac = lax.axis_index("x"), lax.axis_index("y"), lax.axis_index("c")
    chip = 2 * ax + ay
    c_idx = jnp.reshape(ac, (1,)).astype(jnp.int32)
    chip_idx = jnp.reshape(chip, (1,)).astype(jnp.int32)

    phase = [SHARD * i % 16 for i in range(NCHIP)]
    wt_rows = jnp.transpose(w_in, (0, 2, 1)).astype(_CDT)
    at_phase = [functools.partial(jnp.pad, wt_rows, ((0, 0), (p, WSLOT - SHARD - p), (0, 0))) for p in phase]
    wt_src = lax.switch(chip, at_phase).reshape(nl, 2, WSLOT // 2, D)
    wout_src = w_out.astype(_CDT).reshape(nl, 2, D // NCHIP // 2, D)

    def with_own(own):
        start = [chip] + [0] * own.ndim
        return lax.dynamic_update_slice(lax.empty((NCHIP,) + own.shape, own.dtype), own[None], start)

    def gather_operands(l):
        return wt_src[l], wout_src[l], with_own(wt_src[l]), with_own(wout_src[l])

    def layer_weights(bufs):
        wt4 = bufs[0].reshape(NCHIP, WSLOT, D)
        wref = jnp.concatenate([wt4[i, phase[i]:phase[i] + SHARD] for i in range(NCHIP)])
        return _rows_to_internal(wref) + (bufs[1].reshape(D, D),)

    first = gather_operands(0)
    bin0, bout0, wa_all = _gather_first(first[0], first[1], w_alpha, first[2], first[3], with_own(w_alpha))
    wa_full = jnp.transpose(wa_all, (1, 2, 0, 3)).reshape(nl, RANK, GH * GDK)
    wa_pad = jnp.pad(wa_full, ((0, 0), (0, GAP - RANK), (0, 0))).astype(_CDT)
    rbx = [_rel_rows(rel_bias[l]) for l in range(nl)]

    def weights(l):
        return big[l] + (g_pre[l][None], g_post[l][None], wa_pad[l], b_alpha[l][None], g_gla[l][None], g_att[l][None], rbx[l])

    h = x[0]
    saved, big = [], [None] * nl
    big[0] = layer_weights((bin0, bout0))
    for l in range(nl):
        h, sv, bufs = _layer_fwd(h, *weights(l), ride=gather_operands(l + 1) if l + 1 < nl else None)
        saved.append(sv)
        if l + 1 < nl:
            big[l + 1] = layer_weights(bufs)
    dout, loss_part = _loss_grad(h, loss_target[0])

    small, hin, hout = [None] * nl, [None] * nl, [None] * nl
    hw = D // NCHIP // 2

    def reduce_owner(sent, got):
        rin_a, rout_a, rin_b, rout_b = got
        return (_add_chips(chip_idx, sent[0], rin_a, rin_b, 48, "add_chips_in"),
                _add_chips(chip_idx, sent[1], rout_a, rout_b, 128, "add_chips_out"))

    def slab_halves(dwin, dwout):
        gt = jnp.pad(_rows_from_internal(*dwin), ((0, NCHIP * SHARD - SHARD + SLAB - DIN), (0, 0)))
        slabs = jnp.stack([gt[SHARD * i:SHARD * i + SLAB] for i in range(NCHIP)])
        return (jnp.transpose(slabs.reshape(NCHIP, 2, HSLAB, D), (1, 0, 2, 3)).reshape(2, NCHIP * HSLAB, D),
                jnp.transpose(dwout.reshape(NCHIP, 2, hw, D), (1, 0, 2, 3)).reshape(2, NCHIP * hw, D))

    def sum_halves(halves, swapped):
        return (_add_halves(c_idx, halves[0], swapped[0], 192, "add_halves_in").reshape(1, NCHIP, HSLAB, D),
                _add_halves(c_idx, halves[1], swapped[1], 256, "add_halves_out").reshape(1, NCHIP, hw, D))

    def partial_sums(dwin, dwout):
        halves = slab_halves(dwin, dwout)
        return sum_halves(halves, _swap_halves(*halves))

    halves = None
    for l in reversed(range(nl)):
        dout, dwin, dwout, small[l], sent, got, own, got_own = _layer_bwd(
            dout, saved[l], *weights(l), swap=halves, finish=sum_halves, reduce_now=partial_sums if l == 0 else None)
        if halves is not None:
            hin[l + 1], hout[l + 1] = reduce_owner(sent, got)
        halves = slab_halves(dwin, dwout) if l > 0 else None
    sent = own
    grad_x = dout[None]
    hin[0], hout[0] = reduce_owner(sent, list(_send_to_owners(*sent, parts=(0,))) + list(got_own))
    xchg = _exchange_halves(hin + hout)
    hin, xin = jnp.concatenate(hin), jnp.concatenate(xchg[:nl])
    hout, xout = jnp.concatenate(hout), jnp.concatenate(xchg[nl:])

    rows_first = lambda t: jnp.transpose(t, (2, 0, 1))
    w_in_out = _adam_rows(c_idx, rows_first(w_in), hin, xin, rows_first(m_w_in), rows_first(v_w_in), 48, "adam_w_in")
    g_w_in, d_w_in, nm_w_in, nv_w_in = [jnp.transpose(t, (1, 2, 0)) for t in w_in_out]

    def adam_big(w, g_own, g_other, m, v, name):
        shp = w.shape
        halves = lambda t: t.reshape(shp[0], 2, shp[1] // 2, shp[2])
        return [t.reshape(shp) for t in _adam_halves(c_idx, halves(w), g_own, g_other, halves(m), halves(v), 256, name)]

    g_w_out, d_w_out, nm_w_out, nv_w_out = adam_big(w_out, hout, xout, m_w_out, v_w_out, "adam_w_out")

    stacked = [jnp.stack([small[l][i] for l in range(nl)]) for i in range(7)] + [loss_part]
    g_small = _unpack_rows(_allreduce_small(_pack_rows(stacked)), [t.shape for t in stacked])
    g_gpre, g_gpost, g_wa_full, g_ba, g_ggla, g_gatt, g_rb, loss_sum = g_small
    loss = loss_sum[0, 0]
    g_wa = lax.dynamic_slice_in_dim(g_wa_full, chip * GDK, GDK, axis=2)
    names = [(g_pre, m_g_pre, v_g_pre, g_gpre), (g_post, m_g_post, v_g_post, g_gpost), (w_alpha, m_w_alpha, v_w_alpha, g_wa),
             (b_alpha, m_b_alpha, v_b_alpha, g_ba), (g_gla, m_g_gla, v_g_gla, g_ggla), (g_att, m_g_att, v_g_att, g_gatt),
             (rel_bias, m_rel_bias, v_rel_bias, g_rb)]
    shapes = [t[0].shape for t in names]
    packed = [_pack_rows([t[i] for t in names]) for i in range(4)]
    d_s, nm_s, nv_s = [_unpack_rows(t, shapes) for t in _adam(packed[0], packed[3], packed[1], packed[2], packed[0].shape[0], "adam_small")]

    grads = [g_w_in, g_w_out, g_gpre, g_gpost, g_wa, g_ba, g_ggla, g_gatt, g_rb]
    deltas = [d_w_in, d_w_out] + d_s
    new_m = [nm_w_in, nm_w_out] + nm_s
    new_v = [nv_w_in, nv_w_out] + nv_s
    return (loss, grad_x, *grads, *deltas, *new_m, *new_v)
```
